```python
import jax, jax.numpy as jnp
from jax import lax
import numpy as np

D_MODEL = 2048
BATCH = 8
SEQ = 2048
DEPTH = 4

A_WIDTH = D_MODEL // 2
A_GROUPS = 8
A_GROUP_DIM = A_WIDTH // A_GROUPS
CHUNK = 128
B_HEAD_DIM = 64
B_Q_HEADS = (D_MODEL // 2) // B_HEAD_DIM
B_KV_HEADS = 2
B_WIDTH = B_Q_HEADS * B_HEAD_DIM
B_KV_WIDTH = B_KV_HEADS * B_HEAD_DIM
WINDOW = 128
ROT_DIM = B_HEAD_DIM // 4
ROPE_THETA = 500000.0
C_WIDTH = D_MODEL // 2
C_HEADS = 8
C_HEAD_DIM = C_WIDTH // C_HEADS
CONV_WIDTH = 4
LRU_C = 8.0
D_WIDTH = D_MODEL // 2
POOL_WINDOWS = (2, 4, 8, 16)
D_GROUPS = len(POOL_WINDOWS)
D_GROUP_DIM = D_WIDTH // D_GROUPS

EVEN_SIZES = (A_WIDTH, A_WIDTH, A_WIDTH, B_WIDTH, B_KV_WIDTH, B_KV_WIDTH, B_WIDTH)
ODD_SIZES = (C_WIDTH, C_WIDTH, D_WIDTH, D_WIDTH)
EVEN_IN = sum(EVEN_SIZES)
ODD_IN = sum(ODD_SIZES)
EVEN_MIX = A_WIDTH + B_WIDTH
ODD_MIX = C_WIDTH + D_WIDTH
N_EVEN = (DEPTH + 1) // 2
N_ODD = DEPTH // 2
DN_ALPHA = (2 * DEPTH) ** 0.25
DN_BETA = (8 * DEPTH) ** -0.25
LN_EPS = 1e-5

kernel_name = "hybrid_gmlp_swa_rglru_pool_deepnorm"


def _split(h, sizes):
    idx = [int(i) for i in np.cumsum(sizes)[:-1]]
    return jnp.split(h, idx, axis=-1)


def layer_norm(x, g, b):
    xf = x.astype(jnp.float32)
    mu = jnp.mean(xf, axis=-1, keepdims=True)
    xc = xf - mu
    var = jnp.mean(xc * xc, axis=-1, keepdims=True)
    y = xc * lax.rsqrt(var + LN_EPS) * g.astype(jnp.float32) + b.astype(jnp.float32)
    return y.astype(x.dtype)


def partial_rope(t, cos, sin):
    half = ROT_DIM // 2
    t1, t2, rest = t[..., :half], t[..., half:ROT_DIM], t[..., ROT_DIM:]
    return jnp.concatenate([t1 * cos - t2 * sin, t2 * cos + t1 * sin, rest], axis=-1)


def chunked_gmlp(u, v, ln_g, ln_b, w_s, b_s):
    bsz, s, _ = v.shape
    nc = s // CHUNK
    v = layer_norm(v, ln_g, ln_b)
    vb = v.reshape(bsz, nc, CHUNK, A_GROUPS, A_GROUP_DIM)
    causal = jnp.tril(jnp.ones((CHUNK, CHUNK), dtype=bool))
    ws = jnp.where(causal[None], w_s, jnp.zeros_like(w_s))
    mixed = jnp.einsum('gts,bcsgd->bctgd', ws, vb) + b_s.T[None, None, :, :, None]
    return u * mixed.reshape(bsz, s, A_WIDTH)


def swa_with_sinks(q, k, v, sinks):
    bsz, s, hq, dh = q.shape
    grp = hq // B_KV_HEADS
    nb = s // WINDOW
    qb = q.reshape(bsz, nb, WINDOW, B_KV_HEADS, grp, dh)
    pad = ((0, 0), (1, 0), (0, 0), (0, 0), (0, 0))
    kb = k.reshape(bsz, nb, WINDOW, B_KV_HEADS, dh)
    vb = v.reshape(bsz, nb, WINDOW, B_KV_HEADS, dh)
    kband = jnp.concatenate([jnp.pad(kb[:, :-1], pad), kb], axis=2)
    vband = jnp.concatenate([jnp.pad(vb[:, :-1], pad), vb], axis=2)
    scores = jnp.einsum('bnqhgd,bnkhd->bnhgqk', qb, kband).astype(jnp.float32) * (dh ** -0.5)
    qi = jnp.arange(WINDOW)[:, None]
    kj = jnp.arange(2 * WINDOW)[None, :]
    diff = qi + WINDOW - kj
    band = (diff >= 0) & (diff < WINDOW)
    blk = jnp.arange(nb)[:, None, None]
    mask = band[None] & ((blk > 0) | (kj[None] >= WINDOW))
    scores = jnp.where(mask[None, :, None, None], scores, -jnp.inf)
    sink = jnp.broadcast_to(
        sinks.astype(jnp.float32).reshape(B_KV_HEADS, grp)[None, None, :, :, None, None],
        scores.shape[:-1] + (1,))
    probs = jax.nn.softmax(jnp.concatenate([scores, sink], axis=-1), axis=-1)[..., :-1]
    out = jnp.einsum('bnhgqk,bnkhd->bnqhgd', probs.astype(v.dtype), vband)
    return out.reshape(bsz, s, hq * dh)


def rg_lru(xc, conv_w, conv_b, w_a, b_a, w_x, b_x, lam):
    bsz, s, _ = xc.shape
    xconv = lax.conv_general_dilated(
        xc, conv_w[:, None, :], window_strides=(1,), padding=[(CONV_WIDTH - 1, 0)],
        dimension_numbers=('NWC', 'WIO', 'NWC'), feature_group_count=C_WIDTH) + conv_b
    xh = xconv.reshape(bsz, s, C_HEADS, C_HEAD_DIM)
    r = jax.nn.sigmoid(jnp.einsum('bshi,hij->bshj', xh, w_a).reshape(bsz, s, C_WIDTH) + b_a)
    i = jax.nn.sigmoid(jnp.einsum('bshi,hij->bshj', xh, w_x).reshape(bsz, s, C_WIDTH) + b_x)
    log_a = -LRU_C * r.astype(jnp.float32) * jax.nn.softplus(-lam.astype(jnp.float32))
    a = jnp.exp(log_a)
    mult = jnp.sqrt(-jnp.expm1(2.0 * log_a))
    bterm = mult * (i * xconv).astype(jnp.float32)

    def combine(left, right):
        a1, b1 = left
        a2, b2 = right
        return a1 * a2, a2 * b1 + b2

    _, h = lax.associative_scan(combine, (a, bterm), axis=1)
    return h.astype(xc.dtype)


def multiscale_pool(xd, w_pool, d_scale):
    bsz, s, _ = xd.shape
    xf = xd.astype(jnp.float32)
    csum = jnp.cumsum(xf, axis=1)
    pos1 = jnp.arange(s, dtype=jnp.float32)[None, :, None] + 1.0
    outs = []
    for g, w in enumerate(POOL_WINDOWS):
        sl = slice(g * D_GROUP_DIM, (g + 1) * D_GROUP_DIM)
        cg = csum[..., sl]
        shifted = jnp.pad(cg[:, :-w], ((0, 0), (w, 0), (0, 0)))
        mean = (cg - shifted) / jnp.minimum(pos1, float(w))
        outs.append(mean - xf[..., sl])
    pooled = jnp.stack(outs, axis=2).astype(xd.dtype)
    mixed = jnp.einsum('bsgi,gij->bsgj', pooled, w_pool).reshape(bsz, s, D_WIDTH)
    return mixed * d_scale


def even_layer(x, cos, sin, w_in, a_ln_g, a_ln_b, a_ws, a_bs, b_sinks, w_out, ln_g, ln_b):
    bsz, s, _ = x.shape
    h = x @ w_in
    u, v, a_gate, q, k, vv, b_gate = _split(h, EVEN_SIZES)
    a_out = chunked_gmlp(u, v, a_ln_g, a_ln_b, a_ws, a_bs) * jax.nn.silu(a_gate)
    q = partial_rope(q.reshape(bsz, s, B_Q_HEADS, B_HEAD_DIM), cos, sin)
    k = partial_rope(k.reshape(bsz, s, B_KV_HEADS, B_HEAD_DIM), cos, sin)
    vv = vv.reshape(bsz, s, B_KV_HEADS, B_HEAD_DIM)
    b_out = swa_with_sinks(q, k, vv, b_sinks) * jax.nn.silu(b_gate)
    y = jnp.concatenate([a_out, b_out], axis=-1) @ w_out
    return layer_norm(DN_ALPHA * x + y, ln_g, ln_b)


def odd_layer(x, w_in, conv_w, conv_b, w_a, b_a, w_x, b_x, lam, w_pool, d_scale,
              w_out, ln_g, ln_b):
    h = x @ w_in
    xc, c_gate, xd, d_gate = _split(h, ODD_SIZES)
    c_out = rg_lru(xc, conv_w, conv_b, w_a, b_a, w_x, b_x, lam) * jax.nn.silu(c_gate)
    d_out = multiscale_pool(xd, w_pool, d_scale) * jax.nn.silu(d_gate)
    y = jnp.concatenate([c_out, d_out], axis=-1) @ w_out
    return layer_norm(DN_ALPHA * x + y, ln_g, ln_b)


def _fwd_setup_inputs(seed: int = 0) -> dict:
    key = jax.random.key(seed)
    ks = jax.random.split(key, 26)
    f32 = jnp.float32
    nrm = lambda k, shp, sc: sc * jax.random.normal(k, shp, f32)
    ne, no = N_EVEN, N_ODD
    u = jax.random.uniform(ks[25], (no, C_WIDTH), f32, 0.9, 0.999)
    a_base = u ** (1.0 / LRU_C)
    lam = jnp.log(a_base) - jnp.log1p(-a_base)
    return {
        "x": nrm(ks[0], (BATCH, SEQ, D_MODEL), 1.0),
        "positions": jnp.broadcast_to(jnp.arange(SEQ, dtype=jnp.int32)[None], (BATCH, SEQ)),
        "even_w_in": nrm(ks[1], (ne, D_MODEL, EVEN_IN), D_MODEL ** -0.5),
        "even_a_ln_g": 1.0 + nrm(ks[2], (ne, A_WIDTH), 0.01),
        "even_a_ln_b": nrm(ks[3], (ne, A_WIDTH), 0.01),
        "even_a_ws": nrm(ks[4], (ne, A_GROUPS, CHUNK, CHUNK), CHUNK ** -0.5),
        "even_a_bs": 1.0 + nrm(ks[5], (ne, A_GROUPS, CHUNK), 0.1),
        "even_b_sinks": nrm(ks[6], (ne, B_Q_HEADS), 0.5),
        "even_w_out": nrm(ks[7], (ne, EVEN_MIX, D_MODEL), DN_BETA * EVEN_MIX ** -0.5),
        "even_ln_g": 1.0 + nrm(ks[8], (ne, D_MODEL), 0.01),
        "even_ln_b": nrm(ks[9], (ne, D_MODEL), 0.01),
        "odd_w_in": nrm(ks[10], (no, D_MODEL, ODD_IN), D_MODEL ** -0.5),
        "odd_conv_w": nrm(ks[11], (no, CONV_WIDTH, C_WIDTH), CONV_WIDTH ** -0.5),
        "odd_conv_b": nrm(ks[12], (no, C_WIDTH), 0.01),
        "odd_w_a": nrm(ks[13], (no, C_HEADS, C_HEAD_DIM, C_HEAD_DIM), C_HEAD_DIM ** -0.5),
        "odd_b_a": nrm(ks[14], (no, C_WIDTH), 0.01),
        "odd_w_x": nrm(ks[15], (no, C_HEADS, C_HEAD_DIM, C_HEAD_DIM), C_HEAD_DIM ** -0.5),
        "odd_b_x": nrm(ks[16], (no, C_WIDTH), 0.01),
        "odd_lam": lam,
        "odd_w_pool": nrm(ks[17], (no, D_GROUPS, D_GROUP_DIM, D_GROUP_DIM), D_GROUP_DIM ** -0.5),
        "odd_d_scale": 1.0 + nrm(ks[18], (no, D_WIDTH), 0.1),
        "odd_w_out": nrm(ks[19], (no, ODD_MIX, D_MODEL), DN_BETA * ODD_MIX ** -0.5),
        "odd_ln_g": 1.0 + nrm(ks[20], (no, D_MODEL), 0.01),
        "odd_ln_b": nrm(ks[21], (no, D_MODEL), 0.01),
    }


def _fwd_reference(x, positions, even_w_in, even_a_ln_g, even_a_ln_b, even_a_ws, even_a_bs,
              even_b_sinks, even_w_out, even_ln_g, even_ln_b, odd_w_in, odd_conv_w,
              odd_conv_b, odd_w_a, odd_b_a, odd_w_x, odd_b_x, odd_lam, odd_w_pool,
              odd_d_scale, odd_w_out, odd_ln_g, odd_ln_b):
    inv_freq = ROPE_THETA ** (-jnp.arange(0, ROT_DIM, 2, dtype=jnp.float32) / ROT_DIM)
    ang = positions.astype(jnp.float32)[..., None] * inv_freq
    cos = jnp.cos(ang)[:, :, None, :].astype(x.dtype)
    sin = jnp.sin(ang)[:, :, None, :].astype(x.dtype)
    for layer in range(DEPTH):
        j = layer // 2
        if layer % 2 == 0:
            x = even_layer(x, cos, sin, even_w_in[j], even_a_ln_g[j], even_a_ln_b[j],
                           even_a_ws[j], even_a_bs[j], even_b_sinks[j], even_w_out[j],
                           even_ln_g[j], even_ln_b[j])
        else:
            x = odd_layer(x, odd_w_in[j], odd_conv_w[j], odd_conv_b[j], odd_w_a[j],
                          odd_b_a[j], odd_w_x[j], odd_b_x[j], odd_lam[j], odd_w_pool[j],
                          odd_d_scale[j], odd_w_out[j], odd_ln_g[j], odd_ln_b[j])
    return x


import jax as _jax
import jax.numpy as _jnp

TWIN_FORMAT = 'train_step'
FWD_PARAMS = ['x', 'positions', 'even_w_in', 'even_a_ln_g', 'even_a_ln_b', 'even_a_ws', 'even_a_bs', 'even_b_sinks', 'even_w_out', 'even_ln_g', 'even_ln_b', 'odd_w_in', 'odd_conv_w', 'odd_conv_b', 'odd_w_a', 'odd_b_a', 'odd_w_x', 'odd_b_x', 'odd_lam', 'odd_w_pool', 'odd_d_scale', 'odd_w_out', 'odd_ln_g', 'odd_ln_b']
TWIN_WEIGHTS = ['even_w_in', 'even_a_ln_g', 'even_a_ln_b', 'even_a_ws', 'even_a_bs', 'even_b_sinks', 'even_w_out', 'even_ln_g', 'even_ln_b', 'odd_w_in', 'odd_conv_w', 'odd_conv_b', 'odd_w_a', 'odd_b_a', 'odd_w_x', 'odd_b_x', 'odd_lam', 'odd_w_pool', 'odd_d_scale', 'odd_w_out', 'odd_ln_g', 'odd_ln_b']
TWIN_DIFF_INPUT = 'x'
TWIN_INPUTS = ['x', 'positions', 'even_w_in', 'even_a_ln_g', 'even_a_ln_b', 'even_a_ws', 'even_a_bs', 'even_b_sinks', 'even_w_out', 'even_ln_g', 'even_ln_b', 'odd_w_in', 'odd_conv_w', 'odd_conv_b', 'odd_w_a', 'odd_b_a', 'odd_w_x', 'odd_b_x', 'odd_lam', 'odd_w_pool', 'odd_d_scale', 'odd_w_out', 'odd_ln_g', 'odd_ln_b', 'loss_target', 'm_even_w_in', 'm_even_a_ln_g', 'm_even_a_ln_b', 'm_even_a_ws', 'm_even_a_bs', 'm_even_b_sinks', 'm_even_w_out', 'm_even_ln_g', 'm_even_ln_b', 'm_odd_w_in', 'm_odd_conv_w', 'm_odd_conv_b', 'm_odd_w_a', 'm_odd_b_a', 'm_odd_w_x', 'm_odd_b_x', 'm_odd_lam', 'm_odd_w_pool', 'm_odd_d_scale', 'm_odd_w_out', 'm_odd_ln_g', 'm_odd_ln_b', 'v_even_w_in', 'v_even_a_ln_g', 'v_even_a_ln_b', 'v_even_a_ws', 'v_even_a_bs', 'v_even_b_sinks', 'v_even_w_out', 'v_even_ln_g', 'v_even_ln_b', 'v_odd_w_in', 'v_odd_conv_w', 'v_odd_conv_b', 'v_odd_w_a', 'v_odd_b_a', 'v_odd_w_x', 'v_odd_b_x', 'v_odd_lam', 'v_odd_w_pool', 'v_odd_d_scale', 'v_odd_w_out', 'v_odd_ln_g', 'v_odd_ln_b']
TWIN_OUTPUTS = ['loss', 'grad_x', 'grad_even_w_in', 'grad_even_a_ln_g', 'grad_even_a_ln_b', 'grad_even_a_ws', 'grad_even_a_bs', 'grad_even_b_sinks', 'grad_even_w_out', 'grad_even_ln_g', 'grad_even_ln_b', 'grad_odd_w_in', 'grad_odd_conv_w', 'grad_odd_conv_b', 'grad_odd_w_a', 'grad_odd_b_a', 'grad_odd_w_x', 'grad_odd_b_x', 'grad_odd_lam', 'grad_odd_w_pool', 'grad_odd_d_scale', 'grad_odd_w_out', 'grad_odd_ln_g', 'grad_odd_ln_b', 'delta_even_w_in', 'delta_even_a_ln_g', 'delta_even_a_ln_b', 'delta_even_a_ws', 'delta_even_a_bs', 'delta_even_b_sinks', 'delta_even_w_out', 'delta_even_ln_g', 'delta_even_ln_b', 'delta_odd_w_in', 'delta_odd_conv_w', 'delta_odd_conv_b', 'delta_odd_w_a', 'delta_odd_b_a', 'delta_odd_w_x', 'delta_odd_b_x', 'delta_odd_lam', 'delta_odd_w_pool', 'delta_odd_d_scale', 'delta_odd_w_out', 'delta_odd_ln_g', 'delta_odd_ln_b', 'new_m_even_w_in', 'new_m_even_a_ln_g', 'new_m_even_a_ln_b', 'new_m_even_a_ws', 'new_m_even_a_bs', 'new_m_even_b_sinks', 'new_m_even_w_out', 'new_m_even_ln_g', 'new_m_even_ln_b', 'new_m_odd_w_in', 'new_m_odd_conv_w', 'new_m_odd_conv_b', 'new_m_odd_w_a', 'new_m_odd_b_a', 'new_m_odd_w_x', 'new_m_odd_b_x', 'new_m_odd_lam', 'new_m_odd_w_pool', 'new_m_odd_d_scale', 'new_m_odd_w_out', 'new_m_odd_ln_g', 'new_m_odd_ln_b', 'new_v_even_w_in', 'new_v_even_a_ln_g', 'new_v_even_a_ln_b', 'new_v_even_a_ws', 'new_v_even_a_bs', 'new_v_even_b_sinks', 'new_v_even_w_out', 'new_v_even_ln_g', 'new_v_even_ln_b', 'new_v_odd_w_in', 'new_v_odd_conv_w', 'new_v_odd_conv_b', 'new_v_odd_w_a', 'new_v_odd_b_a', 'new_v_odd_w_x', 'new_v_odd_b_x', 'new_v_odd_lam', 'new_v_odd_w_pool', 'new_v_odd_d_scale', 'new_v_odd_w_out', 'new_v_odd_ln_g', 'new_v_odd_ln_b']
TWIN_LEAF_KINDS = {'loss': 'loss', 'grad_x': 'grad_x', 'grad_even_w_in': 'grad_w', 'grad_even_a_ln_g': 'grad_w', 'grad_even_a_ln_b': 'grad_w', 'grad_even_a_ws': 'grad_w', 'grad_even_a_bs': 'grad_w', 'grad_even_b_sinks': 'grad_w', 'grad_even_w_out': 'grad_w', 'grad_even_ln_g': 'grad_w', 'grad_even_ln_b': 'grad_w', 'grad_odd_w_in': 'grad_w', 'grad_odd_conv_w': 'grad_w', 'grad_odd_conv_b': 'grad_w', 'grad_odd_w_a': 'grad_w', 'grad_odd_b_a': 'grad_w', 'grad_odd_w_x': 'grad_w', 'grad_odd_b_x': 'grad_w', 'grad_odd_lam': 'grad_w', 'grad_odd_w_pool': 'grad_w', 'grad_odd_d_scale': 'grad_w', 'grad_odd_w_out': 'grad_w', 'grad_odd_ln_g': 'grad_w', 'grad_odd_ln_b': 'grad_w', 'delta_even_w_in': 'delta_w', 'delta_even_a_ln_g': 'delta_w', 'delta_even_a_ln_b': 'delta_w', 'delta_even_a_ws': 'delta_w', 'delta_even_a_bs': 'delta_w', 'delta_even_b_sinks': 'delta_w', 'delta_even_w_out': 'delta_w', 'delta_even_ln_g': 'delta_w', 'delta_even_ln_b': 'delta_w', 'delta_odd_w_in': 'delta_w', 'delta_odd_conv_w': 'delta_w', 'delta_odd_conv_b': 'delta_w', 'delta_odd_w_a': 'delta_w', 'delta_odd_b_a': 'delta_w', 'delta_odd_w_x': 'delta_w', 'delta_odd_b_x': 'delta_w', 'delta_odd_lam': 'delta_w', 'delta_odd_w_pool': 'delta_w', 'delta_odd_d_scale': 'delta_w', 'delta_odd_w_out': 'delta_w', 'delta_odd_ln_g': 'delta_w', 'delta_odd_ln_b': 'delta_w', 'new_m_even_w_in': 'new_m', 'new_m_even_a_ln_g': 'new_m', 'new_m_even_a_ln_b': 'new_m', 'new_m_even_a_ws': 'new_m', 'new_m_even_a_bs': 'new_m', 'new_m_even_b_sinks': 'new_m', 'new_m_even_w_out': 'new_m', 'new_m_even_ln_g': 'new_m', 'new_m_even_ln_b': 'new_m', 'new_m_odd_w_in': 'new_m', 'new_m_odd_conv_w': 'new_m', 'new_m_odd_conv_b': 'new_m', 'new_m_odd_w_a': 'new_m', 'new_m_odd_b_a': 'new_m', 'new_m_odd_w_x': 'new_m', 'new_m_odd_b_x': 'new_m', 'new_m_odd_lam': 'new_m', 'new_m_odd_w_pool': 'new_m', 'new_m_odd_d_scale': 'new_m', 'new_m_odd_w_out': 'new_m', 'new_m_odd_ln_g': 'new_m', 'new_m_odd_ln_b': 'new_m', 'new_v_even_w_in': 'new_v', 'new_v_even_a_ln_g': 'new_v', 'new_v_even_a_ln_b': 'new_v', 'new_v_even_a_ws': 'new_v', 'new_v_even_a_bs': 'new_v', 'new_v_even_b_sinks': 'new_v', 'new_v_even_w_out': 'new_v', 'new_v_even_ln_g': 'new_v', 'new_v_even_ln_b': 'new_v', 'new_v_odd_w_in': 'new_v', 'new_v_odd_conv_w': 'new_v', 'new_v_odd_conv_b': 'new_v', 'new_v_odd_w_a': 'new_v', 'new_v_odd_b_a': 'new_v', 'new_v_odd_w_x': 'new_v', 'new_v_odd_b_x': 'new_v', 'new_v_odd_lam': 'new_v', 'new_v_odd_w_pool': 'new_v', 'new_v_odd_d_scale': 'new_v', 'new_v_odd_w_out': 'new_v', 'new_v_odd_ln_g': 'new_v', 'new_v_odd_ln_b': 'new_v'}


def _forward(args):
    return _fwd_reference(*[args[k] for k in FWD_PARAMS])


def _output_shape():
    out = _jax.eval_shape(lambda: _forward(_fwd_setup_inputs(0)))
    return out.shape, out.dtype

N_MICROBATCH = 1
ADAM_LR = 0.001
ADAM_B1 = 0.9
ADAM_B2 = 0.999
ADAM_EPS = 1e-08
ADAM_WD = 0.01
ADAM_STEP = 10
PER_EXAMPLE_BATCH_AXIS = {'x': 0, 'positions': 0, 'loss_target': 0}
SHARED_INPUTS = []
_WEIGHT_DTYPES = {'even_w_in': _jnp.float32, 'even_a_ln_g': _jnp.float32, 'even_a_ln_b': _jnp.float32, 'even_a_ws': _jnp.float32, 'even_a_bs': _jnp.float32, 'even_b_sinks': _jnp.float32, 'even_w_out': _jnp.float32, 'even_ln_g': _jnp.float32, 'even_ln_b': _jnp.float32, 'odd_w_in': _jnp.float32, 'odd_conv_w': _jnp.float32, 'odd_conv_b': _jnp.float32, 'odd_w_a': _jnp.float32, 'odd_b_a': _jnp.float32, 'odd_w_x': _jnp.float32, 'odd_b_x': _jnp.float32, 'odd_lam': _jnp.float32, 'odd_w_pool': _jnp.float32, 'odd_d_scale': _jnp.float32, 'odd_w_out': _jnp.float32, 'odd_ln_g': _jnp.float32, 'odd_ln_b': _jnp.float32}
MOMENT_SCALE = {'even_w_in': 7.966867e-03, 'even_a_ln_g': 6.611781e-03, 'even_a_ln_b': 6.856010e-03, 'even_a_ws': 6.676206e-03, 'even_a_bs': 9.530810e-03, 'even_b_sinks': 1.471353e-03, 'even_w_out': 1.969283e-02, 'even_ln_g': 1.391905e-01, 'even_ln_b': 1.012029e-01, 'odd_w_in': 7.306669e-03, 'odd_conv_w': 6.270090e-03, 'odd_conv_b': 6.468350e-02, 'odd_w_a': 1.945704e-03, 'odd_b_a': 1.597250e-03, 'odd_w_x': 3.490630e-03, 'odd_b_x': 2.168224e-03, 'odd_lam': 3.202270e-03, 'odd_w_pool': 8.240257e-03, 'odd_d_scale': 8.317453e-03, 'odd_w_out': 1.736677e-02, 'odd_ln_g': 5.657500e+00, 'odd_ln_b': 1.464506e-01}


def _to_microbatches(a, axis):
    t = _jnp.moveaxis(a, axis, 0)
    t = t.reshape((N_MICROBATCH, t.shape[0] // N_MICROBATCH) + t.shape[1:])
    return _jnp.moveaxis(t, 1, axis + 1)


def setup_inputs(seed: int = 0) -> dict:
    inp = _fwd_setup_inputs(seed)
    key = _jax.random.fold_in(_jax.random.key(seed), 7919)
    shape, _ = _output_shape()
    out = dict(inp)
    out["loss_target"] = _jax.random.normal(_jax.random.fold_in(key, 0), shape, _jnp.float32)
    for i, name in enumerate(TWIN_WEIGHTS):
        w = inp[name].astype(_jnp.float32)
        if MOMENT_SCALE is None:
            s = _jnp.sqrt(_jnp.mean(_jnp.square(w)) + 1e-30)
        else:
            s = MOMENT_SCALE[name]
        km, kv = _jax.random.split(_jax.random.fold_in(key, i + 1))
        out[name] = w
        out["m_" + name] = s * _jax.random.normal(km, w.shape, _jnp.float32)
        out["v_" + name] = (s * s) * _jax.random.uniform(kv, w.shape, _jnp.float32, 0.5, 1.5)
    if N_MICROBATCH > 1:
        for name, axis in PER_EXAMPLE_BATCH_AXIS.items():
            out[name] = _to_microbatches(out[name], axis)
    return {'x': out['x'], 'positions': out['positions'], 'even_w_in': out['even_w_in'], 'even_a_ln_g': out['even_a_ln_g'], 'even_a_ln_b': out['even_a_ln_b'], 'even_a_ws': out['even_a_ws'], 'even_a_bs': out['even_a_bs'], 'even_b_sinks': out['even_b_sinks'], 'even_w_out': out['even_w_out'], 'even_ln_g': out['even_ln_g'], 'even_ln_b': out['even_ln_b'], 'odd_w_in': out['odd_w_in'], 'odd_conv_w': out['odd_conv_w'], 'odd_conv_b': out['odd_conv_b'], 'odd_w_a': out['odd_w_a'], 'odd_b_a': out['odd_b_a'], 'odd_w_x': out['odd_w_x'], 'odd_b_x': out['odd_b_x'], 'odd_lam': out['odd_lam'], 'odd_w_pool': out['odd_w_pool'], 'odd_d_scale': out['odd_d_scale'], 'odd_w_out': out['odd_w_out'], 'odd_ln_g': out['odd_ln_g'], 'odd_ln_b': out['odd_ln_b'], 'loss_target': out['loss_target'], 'm_even_w_in': out['m_even_w_in'], 'm_even_a_ln_g': out['m_even_a_ln_g'], 'm_even_a_ln_b': out['m_even_a_ln_b'], 'm_even_a_ws': out['m_even_a_ws'], 'm_even_a_bs': out['m_even_a_bs'], 'm_even_b_sinks': out['m_even_b_sinks'], 'm_even_w_out': out['m_even_w_out'], 'm_even_ln_g': out['m_even_ln_g'], 'm_even_ln_b': out['m_even_ln_b'], 'm_odd_w_in': out['m_odd_w_in'], 'm_odd_conv_w': out['m_odd_conv_w'], 'm_odd_conv_b': out['m_odd_conv_b'], 'm_odd_w_a': out['m_odd_w_a'], 'm_odd_b_a': out['m_odd_b_a'], 'm_odd_w_x': out['m_odd_w_x'], 'm_odd_b_x': out['m_odd_b_x'], 'm_odd_lam': out['m_odd_lam'], 'm_odd_w_pool': out['m_odd_w_pool'], 'm_odd_d_scale': out['m_odd_d_scale'], 'm_odd_w_out': out['m_odd_w_out'], 'm_odd_ln_g': out['m_odd_ln_g'], 'm_odd_ln_b': out['m_odd_ln_b'], 'v_even_w_in': out['v_even_w_in'], 'v_even_a_ln_g': out['v_even_a_ln_g'], 'v_even_a_ln_b': out['v_even_a_ln_b'], 'v_even_a_ws': out['v_even_a_ws'], 'v_even_a_bs': out['v_even_a_bs'], 'v_even_b_sinks': out['v_even_b_sinks'], 'v_even_w_out': out['v_even_w_out'], 'v_even_ln_g': out['v_even_ln_g'], 'v_even_ln_b': out['v_even_ln_b'], 'v_odd_w_in': out['v_odd_w_in'], 'v_odd_conv_w': out['v_odd_conv_w'], 'v_odd_conv_b': out['v_odd_conv_b'], 'v_odd_w_a': out['v_odd_w_a'], 'v_odd_b_a': out['v_odd_b_a'], 'v_odd_w_x': out['v_odd_w_x'], 'v_odd_b_x': out['v_odd_b_x'], 'v_odd_lam': out['v_odd_lam'], 'v_odd_w_pool': out['v_odd_w_pool'], 'v_odd_d_scale': out['v_odd_d_scale'], 'v_odd_w_out': out['v_odd_w_out'], 'v_odd_ln_g': out['v_odd_ln_g'], 'v_odd_ln_b': out['v_odd_ln_b']}


def _loss(weights, diff, rest, loss_target):
    with _jax.named_scope("forward"):
        args = {**rest, TWIN_DIFF_INPUT: diff, **{k: w.astype(_WEIGHT_DTYPES[k]) for k, w in weights.items()}}
        y = _forward(args)
    with _jax.named_scope("loss_head"):
        err = _jnp.square(y.astype(_jnp.float32) - loss_target)
        return 0.5 * _jnp.sum(_jnp.mean(err, axis=-1)) if err.ndim else 0.5 * err


def _adamw(w, g, m, v):
    m = ADAM_B1 * m + (1.0 - ADAM_B1) * g
    v = ADAM_B2 * v + (1.0 - ADAM_B2) * _jnp.square(g)
    m_hat = m / (1.0 - ADAM_B1 ** ADAM_STEP)
    v_hat = v / (1.0 - ADAM_B2 ** ADAM_STEP)
    delta = -ADAM_LR * (m_hat / (_jnp.sqrt(v_hat) + ADAM_EPS) + ADAM_WD * w)
    return delta, m, v


def reference(x, positions, even_w_in, even_a_ln_g, even_a_ln_b, even_a_ws, even_a_bs, even_b_sinks, even_w_out, even_ln_g, even_ln_b, odd_w_in, odd_conv_w, odd_conv_b, odd_w_a, odd_b_a, odd_w_x, odd_b_x, odd_lam, odd_w_pool, odd_d_scale, odd_w_out, odd_ln_g, odd_ln_b, loss_target, m_even_w_in, m_even_a_ln_g, m_even_a_ln_b, m_even_a_ws, m_even_a_bs, m_even_b_sinks, m_even_w_out, m_even_ln_g, m_even_ln_b, m_odd_w_in, m_odd_conv_w, m_odd_conv_b, m_odd_w_a, m_odd_b_a, m_odd_w_x, m_odd_b_x, m_odd_lam, m_odd_w_pool, m_odd_d_scale, m_odd_w_out, m_odd_ln_g, m_odd_ln_b, v_even_w_in, v_even_a_ln_g, v_even_a_ln_b, v_even_a_ws, v_even_a_bs, v_even_b_sinks, v_even_w_out, v_even_ln_g, v_even_ln_b, v_odd_w_in, v_odd_conv_w, v_odd_conv_b, v_odd_w_a, v_odd_b_a, v_odd_w_x, v_odd_b_x, v_odd_lam, v_odd_w_pool, v_odd_d_scale, v_odd_w_out, v_odd_ln_g, v_odd_ln_b):
    given = dict(x=x, positions=positions, even_w_in=even_w_in, even_a_ln_g=even_a_ln_g, even_a_ln_b=even_a_ln_b, even_a_ws=even_a_ws, even_a_bs=even_a_bs, even_b_sinks=even_b_sinks, even_w_out=even_w_out, even_ln_g=even_ln_g, even_ln_b=even_ln_b, odd_w_in=odd_w_in, odd_conv_w=odd_conv_w, odd_conv_b=odd_conv_b, odd_w_a=odd_w_a, odd_b_a=odd_b_a, odd_w_x=odd_w_x, odd_b_x=odd_b_x, odd_lam=odd_lam, odd_w_pool=odd_w_pool, odd_d_scale=odd_d_scale, odd_w_out=odd_w_out, odd_ln_g=odd_ln_g, odd_ln_b=odd_ln_b, loss_target=loss_target, m_even_w_in=m_even_w_in, m_even_a_ln_g=m_even_a_ln_g, m_even_a_ln_b=m_even_a_ln_b, m_even_a_ws=m_even_a_ws, m_even_a_bs=m_even_a_bs, m_even_b_sinks=m_even_b_sinks, m_even_w_out=m_even_w_out, m_even_ln_g=m_even_ln_g, m_even_ln_b=m_even_ln_b, m_odd_w_in=m_odd_w_in, m_odd_conv_w=m_odd_conv_w, m_odd_conv_b=m_odd_conv_b, m_odd_w_a=m_odd_w_a, m_odd_b_a=m_odd_b_a, m_odd_w_x=m_odd_w_x, m_odd_b_x=m_odd_b_x, m_odd_lam=m_odd_lam, m_odd_w_pool=m_odd_w_pool, m_odd_d_scale=m_odd_d_scale, m_odd_w_out=m_odd_w_out, m_odd_ln_g=m_odd_ln_g, m_odd_ln_b=m_odd_ln_b, v_even_w_in=v_even_w_in, v_even_a_ln_g=v_even_a_ln_g, v_even_a_ln_b=v_even_a_ln_b, v_even_a_ws=v_even_a_ws, v_even_a_bs=v_even_a_bs, v_even_b_sinks=v_even_b_sinks, v_even_w_out=v_even_w_out, v_even_ln_g=v_even_ln_g, v_even_ln_b=v_even_ln_b, v_odd_w_in=v_odd_w_in, v_odd_conv_w=v_odd_conv_w, v_odd_conv_b=v_odd_conv_b, v_odd_w_a=v_odd_w_a, v_odd_b_a=v_odd_b_a, v_odd_w_x=v_odd_w_x, v_odd_b_x=v_odd_b_x, v_odd_lam=v_odd_lam, v_odd_w_pool=v_odd_w_pool, v_odd_d_scale=v_odd_d_scale, v_odd_w_out=v_odd_w_out, v_odd_ln_g=v_odd_ln_g, v_odd_ln_b=v_odd_ln_b)
    weights = {n: given[n] for n in TWIN_WEIGHTS}
    shared = {n: given[n] for n in SHARED_INPUTS}
    per_example = {n: given[n] for n in ['x', 'positions']}
    grad_fn = _jax.value_and_grad(_loss, argnums=(0, 1))

    def one_microbatch(ex, loss_target):
        ex = dict(ex)
        diff = ex.pop(TWIN_DIFF_INPUT)
        return grad_fn(weights, diff, {**shared, **ex}, loss_target)

    if N_MICROBATCH == 1:
        loss, (grad_w, grad_x) = one_microbatch(per_example, given["loss_target"])
    else:
        def body(carry, xs):
            loss_sum, grad_sum = carry
            l_k, (gw_k, gx_k) = one_microbatch(xs[0], xs[1])
            with _jax.named_scope("update"):
                return (loss_sum + l_k, _jax.tree.map(_jnp.add, grad_sum, gw_k)), gx_k

        init = (_jnp.zeros((), _jnp.float32), _jax.tree.map(_jnp.zeros_like, weights))
        (loss, grad_w), grad_x = _jax.lax.scan(body, init, (per_example, given["loss_target"]))
    with _jax.named_scope("update"):
        delta_w, new_m, new_v = {}, {}, {}
        for n in TWIN_WEIGHTS:
            delta_w[n], new_m[n], new_v[n] = _adamw(weights[n], grad_w[n], given["m_" + n], given["v_" + n])
    return (loss, grad_x, *[grad_w[n] for n in TWIN_WEIGHTS], *[delta_w[n] for n in TWIN_WEIGHTS],
            *[new_m[n] for n in TWIN_WEIGHTS], *[new_v[n] for n in TWIN_WEIGHTS])
```

```python
import functools

import jax
import jax.numpy as jnp
from jax import lax
from jax.experimental import pallas as pl
from jax.experimental.pallas import tpu as pltpu

F32 = jnp.float32
BF = jnp.bfloat16
MESH = pl.DeviceIdType.MESH
ANY = pl.BlockSpec(memory_space=pl.ANY)

N_DEV = 8
D = 2048
W = 1024
EVEN_IN = 5376
ODD_IN = 4096
CHUNK = 128
ALPHA = (2 * 4) ** 0.25
LN_EPS = 1e-5
ROPE_THETA = 500000.0
LRU_C = 8.0
LR, B1, B2, ADAM_EPS, WD, STEP = 0.001, 0.9, 0.999, 1e-08, 0.01, 10
NEG = -1e30


def _cp(vmem_mb=48):
    return pltpu.CompilerParams(vmem_limit_bytes=vmem_mb * 1024 * 1024)


def _sig(x):
    return jax.nn.sigmoid(x)


def _silu_grad(x):
    s = _sig(x)
    return x * s, s * (1.0 + x * (1.0 - s))


def _dot(a, b):
    return jnp.dot(a, b, preferred_element_type=F32)


def _dot_nt(a, b):
    return lax.dot_general(a, b, (((1,), (1,)), ((), ())), preferred_element_type=F32)


def _dot_tn(a, b):
    return lax.dot_general(a, b, (((0,), (0,)), ((), ())), preferred_element_type=F32)


def _mm_nt(a, w, tm, tn, name, out3=False):
    M, K = a.shape
    N = w.shape[0]
    tm = min(tm, M)

    def body(a_ref, w_ref, o_ref):
        o_ref[...] = _dot_nt(a_ref[...], w_ref[...])

    if out3:
        per = W // tn
        out_shape = jax.ShapeDtypeStruct((N // W, M, W), F32)
        out_spec = pl.BlockSpec((None, tm, tn), lambda i, j: (j // per, i, j % per))
    else:
        out_shape = jax.ShapeDtypeStruct((M, N), F32)
        out_spec = pl.BlockSpec((tm, tn), lambda i, j: (i, j))
    return pl.pallas_call(
        body, grid=(M // tm, N // tn),
        in_specs=[pl.BlockSpec((tm, K), lambda i, j: (i, 0)), pl.BlockSpec((tn, K), lambda i, j: (j, 0))],
        out_specs=out_spec, out_shape=out_shape, name=name, compiler_params=_cp(),
    )(a, w)


def _mm_tn(a, b, tm, name):
    K, N = b.shape
    if a.ndim == 3:
        M = a.shape[0] * W
        per = W // tm
        a_spec = pl.BlockSpec((None, K, tm), lambda i: (i // per, 0, i % per))
    else:
        M = a.shape[1]
        a_spec = pl.BlockSpec((K, tm), lambda i: (0, i))

    def body(a_ref, b_ref, o_ref):
        o_ref[...] = _dot_tn(a_ref[...], b_ref[...])

    return pl.pallas_call(
        body, grid=(M // tm,),
        in_specs=[a_spec, pl.BlockSpec((K, N), lambda i: (0, 0))],
        out_specs=pl.BlockSpec((tm, N), lambda i: (i, 0)),
        out_shape=jax.ShapeDtypeStruct((M, N), F32), name=name, compiler_params=_cp(56),
    )(a, b)


def _mm_nn_res(a, w, res, tm, tk, name):
    K, N = w.shape
    if a.ndim == 3:
        M = a.shape[1]
        tm = min(tm, M)
        per = W // tk
        a_spec = pl.BlockSpec((None, tm, tk), lambda i, k: (k // per, i, k % per))
    else:
        M = a.shape[0]
        tm = min(tm, M)
        a_spec = pl.BlockSpec((tm, tk), lambda i, k: (i, k))

    def body(a_ref, w_ref, r_ref, o_ref):
        k = pl.program_id(1)
        d = _dot(a_ref[...], w_ref[...])

        @pl.when(k == 0)
        def _():
            o_ref[...] = ALPHA * r_ref[...] + d

        @pl.when(k > 0)
        def _():
            o_ref[...] += d

    return pl.pallas_call(
        body, grid=(M // tm, K // tk),
        in_specs=[a_spec, pl.BlockSpec((tk, N), lambda i, k: (k, 0)), pl.BlockSpec((tm, N), lambda i, k: (i, 0))],
        out_specs=pl.BlockSpec((tm, N), lambda i, k: (i, 0)),
        out_shape=jax.ShapeDtypeStruct((M, N), F32), name=name, compiler_params=_cp(),
    )(a, w, res)


def _mm_out_ln(mix3, w_out, x, g, b, name):
    S = x.shape[0]
    tm = min(256, S)

    def body(m_ref, w_ref, x_ref, g_ref, b_ref, z_ref, xn_ref, xb_ref):
        acc = _dot(m_ref[0], w_ref[0:W, :]) + _dot(m_ref[1], w_ref[W:2 * W, :])
        z = ALPHA * x_ref[...] + acc
        mu = jnp.mean(z, axis=1, keepdims=True)
        zc = z - mu
        var = jnp.mean(zc * zc, axis=1, keepdims=True)
        xn = zc * lax.rsqrt(var + LN_EPS) * g_ref[...] + b_ref[...]
        z_ref[...] = z
        xn_ref[...] = xn
        xb_ref[...] = xn.astype(BF)

    row = pl.BlockSpec((tm, D), lambda i: (i, 0))
    vec = pl.BlockSpec((1, D), lambda i: (0, 0))
    return pl.pallas_call(
        body, grid=(S // tm,),
        in_specs=[pl.BlockSpec((2, tm, W), lambda i: (0, i, 0)), pl.BlockSpec((D, D), lambda i: (0, 0)), row, vec, vec],
        out_specs=[row, row, row],
        out_shape=[jax.ShapeDtypeStruct((S, D), F32), jax.ShapeDtypeStruct((S, D), F32), jax.ShapeDtypeStruct((S, D), BF)],
        name=name, compiler_params=_cp(),
    )(mix3, w_out, x, g.reshape(1, D), b.reshape(1, D))


def _ln_bwd(dxn, z, g, name):
    S = z.shape[0]
    tm = min(256, S)

    def body(d_ref, z_ref, g_ref, dz_ref, dzb_ref, dg_ref, db_ref):
        i = pl.program_id(0)
        zz = z_ref[...]
        mu = jnp.mean(zz, axis=1, keepdims=True)
        zc = zz - mu
        var = jnp.mean(zc * zc, axis=1, keepdims=True)
        rstd = lax.rsqrt(var + LN_EPS)
        xhat = zc * rstd
        dy = d_ref[...]
        dyg = dy * g_ref[...]
        m1 = jnp.mean(dyg, axis=1, keepdims=True)
        m2 = jnp.mean(dyg * xhat, axis=1, keepdims=True)
        dz = rstd * (dyg - m1 - xhat * m2)
        dz_ref[...] = dz
        dzb_ref[...] = dz.astype(BF)

        @pl.when(i == 0)
        def _():
            dg_ref[...] = jnp.zeros_like(dg_ref)
            db_ref[...] = jnp.zeros_like(db_ref)

        dg_ref[...] += jnp.sum(dy * xhat, axis=0, keepdims=True)
        db_ref[...] += jnp.sum(dy, axis=0, keepdims=True)

    row = pl.BlockSpec((tm, D), lambda i: (i, 0))
    vec = pl.BlockSpec((1, D), lambda i: (0, 0))
    return pl.pallas_call(
        body, grid=(S // tm,), in_specs=[row, row, vec], out_specs=[row, row, vec, vec],
        out_shape=[jax.ShapeDtypeStruct((S, D), F32), jax.ShapeDtypeStruct((S, D), BF),
                   jax.ShapeDtypeStruct((1, D), F32), jax.ShapeDtypeStruct((1, D), F32)],
        name=name, compiler_params=_cp(),
    )(dxn, z, g.reshape(1, D))


def _loss_grad(xn, target):
    S = xn.shape[0]
    tm = min(256, S)

    def body(x_ref, t_ref, d_ref, p_ref):
        i = pl.program_id(0)
        e = x_ref[...] - t_ref[...]
        d_ref[...] = e * (1.0 / D)

        @pl.when(i == 0)
        def _():
            p_ref[...] = jnp.zeros_like(p_ref)

        p_ref[...] += jnp.sum(jnp.sum(e * e, axis=1, keepdims=True), axis=0, keepdims=True)

    row = pl.BlockSpec((tm, D), lambda i: (i, 0))
    return pl.pallas_call(
        body, grid=(S // tm,), in_specs=[row, row],
        out_specs=[row, pl.BlockSpec((8, 128), lambda i: (0, 0))],
        out_shape=[jax.ShapeDtypeStruct((S, D), F32), jax.ShapeDtypeStruct((8, 128), F32)],
        name="loss_grad", compiler_params=_cp(),
    )(xn, target)


def _rope_fwd(t, r_ref):
    return (t * r_ref[:, 0:128] + pltpu.roll(t, 120, 1) * r_ref[:, 128:256]
            + pltpu.roll(t, 8, 1) * r_ref[:, 256:384])


def _rope_bwd(g, r_ref):
    return (g * r_ref[:, 0:128] + pltpu.roll(g * r_ref[:, 128:256], 8, 1)
            + pltpu.roll(g * r_ref[:, 256:384], 120, 1))


def _dup_heads(kb):
    lo = lax.broadcasted_iota(jnp.int32, kb.shape, 1) < 64
    sw = pltpu.roll(kb, 64, 1)
    return [jnp.where(lo, kb, sw).astype(BF), jnp.where(lo, sw, kb).astype(BF)]


def _even_fwd(h, rope, lng, lnb, ws, bsb, sinks, name):
    S = h.shape[0]
    nb = S // CHUNK

    def body(h_ref, hp_ref, rc_ref, rp_ref, lng_ref, lnb_ref, ws_ref, bsb_ref, sink_ref, mix_ref, o_ref, l_ref):
        n = pl.program_id(0)
        lane = lax.broadcasted_iota(jnp.int32, (128, 128), 1)
        rowi = lax.broadcasted_iota(jnp.int32, (128, 128), 0)
        tri = rowi >= lane
        lane_lo = lane < 64
        v = h_ref[:, W:2 * W]
        mu = jnp.mean(v, axis=1, keepdims=True)
        vc = v - mu
        var = jnp.mean(vc * vc, axis=1, keepdims=True)
        vn = vc * lax.rsqrt(var + LN_EPS) * lng_ref[...] + lnb_ref[...]
        for g in range(8):
            sl = slice(g * 128, (g + 1) * 128)
            w = jnp.where(tri, ws_ref[g], 0.0).astype(BF)
            m = _dot(w, vn[:, sl].astype(BF)) + bsb_ref[g]
            ag = h_ref[:, 2 * W + g * 128:2 * W + (g + 1) * 128]
            mix_ref[0, :, sl] = (h_ref[:, sl] * m * (ag * _sig(ag))).astype(BF)
        kb = jnp.concatenate([_rope_fwd(hp_ref[:, 0:128], rp_ref), _rope_fwd(h_ref[:, 4096:4224], rc_ref)], axis=0)
        vb = jnp.concatenate([hp_ref[:, 128:256], h_ref[:, 4224:4352]], axis=0)
        k2 = _dup_heads(kb)
        v2 = _dup_heads(vb)
        qi = lax.broadcasted_iota(jnp.int32, (128, 256), 0)
        kj = lax.broadcasted_iota(jnp.int32, (128, 256), 1)
        diff = qi + 128 - kj
        valid = (diff >= 0) & (diff < 128) & ((n > 0) | (kj >= 128))
        lacc = jnp.zeros((128, 128), F32)
        for j in range(8):
            hk = j // 4
            cs = slice(j * 128, (j + 1) * 128)
            qc = _rope_fwd(h_ref[:, 3072 + j * 128:3072 + (j + 1) * 128], rc_ref)
            ocol = jnp.zeros((128, 128), F32)
            for half in range(2):
                hq = 2 * j + half
                hm = lane_lo if half == 0 else jnp.logical_not(lane_lo)
                qm = jnp.where(hm, qc, 0.0).astype(BF)
                s = jnp.where(valid, _dot_nt(qm, k2[hk]) * 0.125, NEG)
                sk = sink_ref[hq]
                mx = jnp.maximum(jnp.max(s, axis=1, keepdims=True), sk)
                p = jnp.exp(s - mx)
                den = jnp.sum(p, axis=1, keepdims=True) + jnp.exp(sk - mx)
                oh = _dot((p / den).astype(BF), v2[hk])
                ocol = jnp.where(hm, oh, ocol)
                lacc = jnp.where(lane == hq, mx + jnp.log(den), lacc)
            bg = h_ref[:, 4352 + j * 128:4352 + (j + 1) * 128]
            o_ref[:, cs] = ocol
            mix_ref[1, :, cs] = (ocol * (bg * _sig(bg))).astype(BF)
        l_ref[...] = lacc

    prev = lambda n: jnp.maximum(n - 1, 0)
    full = lambda shape: pl.BlockSpec(shape, lambda n: (0,) * len(shape))
    return pl.pallas_call(
        body, grid=(nb,),
        in_specs=[pl.BlockSpec((CHUNK, EVEN_IN), lambda n: (n, 0)),
                  pl.BlockSpec((CHUNK, 256), lambda n: (prev(n), 16)),
                  pl.BlockSpec((CHUNK, 384), lambda n: (n, 0)),
                  pl.BlockSpec((CHUNK, 384), lambda n: (prev(n), 0)),
                  full((1, W)), full((1, W)), full((8, 128, 128)), full((8, 128, 128)),
                  pl.BlockSpec(memory_space=pltpu.SMEM)],
        out_specs=[pl.BlockSpec((2, CHUNK, W), lambda n: (0, n, 0)),
                   pl.BlockSpec((CHUNK, W), lambda n: (n, 0)),
                   pl.BlockSpec((CHUNK, 128), lambda n: (n, 0))],
        out_shape=[jax.ShapeDtypeStruct((2, S, W), BF), jax.ShapeDtypeStruct((S, W), F32),
                   jax.ShapeDtypeStruct((S, 128), F32)],
        name=name, compiler_params=_cp(),
    )(h, h, rope, rope, lng.reshape(1, W), lnb.reshape(1, W), ws, bsb, sinks)


def _even_bwd(h, dmix3, o, l, rope, lng, lnb, ws, wst, bsb, sinks, name):
    S = h.shape[0]
    nb = S // CHUNK

    def body(h_ref, hp_ref, hn_ref, dm_ref, dmn_ref, o_ref, on_ref, l_ref, ln_ref, rc_ref, rp_ref, rn_ref,
             lng_ref, lnb_ref, ws_ref, wst_ref, bsb_ref, sink_ref,
             dh_ref, dws_ref, dbs_ref, dlng_ref, dlnb_ref, dsink_ref, dvn_ref):
        n = pl.program_id(0)

        @pl.when(n == 0)
        def _():
            dws_ref[...] = jnp.zeros_like(dws_ref)
            dbs_ref[...] = jnp.zeros_like(dbs_ref)
            dlng_ref[...] = jnp.zeros_like(dlng_ref)
            dlnb_ref[...] = jnp.zeros_like(dlnb_ref)
            dsink_ref[...] = jnp.zeros_like(dsink_ref)

        lane = lax.broadcasted_iota(jnp.int32, (128, 128), 1)
        rowi = lax.broadcasted_iota(jnp.int32, (128, 128), 0)
        lane1 = lax.broadcasted_iota(jnp.int32, (1, 128), 1)
        tri = rowi >= lane
        tri_t = lane >= rowi
        lane_lo = lane < 64
        v = h_ref[:, W:2 * W]
        mu = jnp.mean(v, axis=1, keepdims=True)
        vc = v - mu
        var = jnp.mean(vc * vc, axis=1, keepdims=True)
        rstd = lax.rsqrt(var + LN_EPS)
        vhat = vc * rstd
        vn = vhat * lng_ref[...] + lnb_ref[...]
        dbs_acc = jnp.zeros((128, 128), F32)
        for g in range(8):
            sl = slice(g * 128, (g + 1) * 128)
            w = jnp.where(tri, ws_ref[g], 0.0).astype(BF)
            wt = jnp.where(tri_t, wst_ref[g], 0.0).astype(BF)
            vng = vn[:, sl].astype(BF)
            m = _dot(w, vng) + bsb_ref[g]
            ag = h_ref[:, 2 * W + g * 128:2 * W + (g + 1) * 128]
            sg, dsg = _silu_grad(ag)
            u = h_ref[:, sl]
            da = dm_ref[0, :, sl]
            dmm = da * u * sg
            dh_ref[:, sl] = (da * m * sg).astype(BF)
            dh_ref[:, 2 * W + g * 128:2 * W + (g + 1) * 128] = (da * u * m * dsg).astype(BF)
            dmb = dmm.astype(BF)
            dvn_ref[:, sl] = _dot(wt, dmb)
            dws_ref[g] += jnp.where(tri, _dot_nt(dmb, vng), 0.0)
            dbs_acc = jnp.where(lane == g, jnp.sum(dmm, axis=1, keepdims=True), dbs_acc)
        dbs_ref[...] += dbs_acc
        dvn = dvn_ref[...]
        dlng_ref[...] += jnp.sum(dvn * vhat, axis=0, keepdims=True)
        dlnb_ref[...] += jnp.sum(dvn, axis=0, keepdims=True)
        dyg = dvn * lng_ref[...]
        m1 = jnp.mean(dyg, axis=1, keepdims=True)
        m2 = jnp.mean(dyg * vhat, axis=1, keepdims=True)
        dh_ref[:, W:2 * W] = (rstd * (dyg - m1 - vhat * m2)).astype(BF)
        kcur = _rope_fwd(h_ref[:, 4096:4224], rc_ref)
        kb = jnp.concatenate([_rope_fwd(hp_ref[:, 0:128], rp_ref), kcur], axis=0)
        vb = jnp.concatenate([hp_ref[:, 128:256], h_ref[:, 4224:4352]], axis=0)
        k2 = _dup_heads(kb)
        v2 = _dup_heads(vb)
        kc2 = _dup_heads(kcur)
        vc2 = _dup_heads(h_ref[:, 4224:4352])
        qi = lax.broadcasted_iota(jnp.int32, (128, 256), 0)
        kj = lax.broadcasted_iota(jnp.int32, (128, 256), 1)
        diff = qi + 128 - kj
        valid = (diff >= 0) & (diff < 128) & ((n > 0) | (kj >= 128))
        validn = (lane > rowi) & (n < nb - 1)
        lc = l_ref[...]
        lnx = ln_ref[...]
        dk = [jnp.zeros((128, 128), F32), jnp.zeros((128, 128), F32)]
        dv = [jnp.zeros((128, 128), F32), jnp.zeros((128, 128), F32)]
        dsk_acc = jnp.zeros((1, 128), F32)
        for j in range(8):
            hk = j // 4
            cs = slice(j * 128, (j + 1) * 128)
            qc = _rope_fwd(h_ref[:, 3072 + j * 128:3072 + (j + 1) * 128], rc_ref)
            qn = _rope_fwd(hn_ref[:, 3072 + j * 128:3072 + (j + 1) * 128], rn_ref)
            bg = h_ref[:, 4352 + j * 128:4352 + (j + 1) * 128]
            sgb, dsgb = _silu_grad(bg)
            db = dm_ref[1, :, cs]
            oc = o_ref[:, cs]
            do = db * sgb
            dh_ref[:, 4352 + j * 128:4352 + (j + 1) * 128] = (db * oc * dsgb).astype(BF)
            bgn = hn_ref[:, 4352 + j * 128:4352 + (j + 1) * 128]
            don = dmn_ref[1, :, cs] * (bgn * _sig(bgn))
            prod = do * oc
            prodn = don * on_ref[:, cs]
            dqcol = jnp.zeros((128, 128), F32)
            for half in range(2):
                hq = 2 * j + half
                hm = lane_lo if half == 0 else jnp.logical_not(lane_lo)
                dsum = jnp.sum(jnp.where(hm, prod, 0.0), axis=1, keepdims=True)
                dsumn = jnp.sum(jnp.where(hm, prodn, 0.0), axis=1, keepdims=True)
                lh = jnp.sum(jnp.where(lane == hq, lc, 0.0), axis=1, keepdims=True)
                lhn = jnp.sum(jnp.where(lane == hq, lnx, 0.0), axis=1, keepdims=True)
                qm = jnp.where(hm, qc, 0.0).astype(BF)
                dom = jnp.where(hm, do, 0.0).astype(BF)
                s = _dot_nt(qm, k2[hk]) * 0.125
                p = jnp.exp(jnp.where(valid, s - lh, NEG))
                ds = p * (_dot_nt(dom, v2[hk]) - dsum)
                dqcol = jnp.where(hm, _dot(ds.astype(BF), k2[hk]) * 0.125, dqcol)
                psink = jnp.exp(sink_ref[hq] - lh)
                dsk = -jnp.sum(psink * dsum, axis=0, keepdims=True)
                dsk_acc = jnp.where(lane1 == hq, dsk, dsk_acc)
                dv[hk] = dv[hk] + _dot(jnp.transpose(p[:, 128:256]).astype(BF), dom)
                dk[hk] = dk[hk] + _dot(jnp.transpose(ds[:, 128:256]).astype(BF), qm) * 0.125
                qnm = jnp.where(hm, qn, 0.0).astype(BF)
                donm = jnp.where(hm, don, 0.0).astype(BF)
                sn = _dot_nt(qnm, kc2[hk]) * 0.125
                pn = jnp.exp(jnp.where(validn, sn - lhn, NEG))
                dsn = pn * (_dot_nt(donm, vc2[hk]) - dsumn)
                dv[hk] = dv[hk] + _dot(jnp.transpose(pn).astype(BF), donm)
                dk[hk] = dk[hk] + _dot(jnp.transpose(dsn).astype(BF), qnm) * 0.125
            dh_ref[:, 3072 + j * 128:3072 + (j + 1) * 128] = _rope_bwd(dqcol, rc_ref).astype(BF)
        fold = lambda a: a + pltpu.roll(a, 64, 1)
        dh_ref[:, 4096:4224] = _rope_bwd(jnp.where(lane_lo, fold(dk[0]), fold(dk[1])), rc_ref).astype(BF)
        dh_ref[:, 4224:4352] = jnp.where(lane_lo, fold(dv[0]), fold(dv[1])).astype(BF)
        dsink_ref[...] += dsk_acc

    prev = lambda n: jnp.maximum(n - 1, 0)
    nxt = lambda n: jnp.minimum(n + 1, nb - 1)
    full = lambda shape: pl.BlockSpec(shape, lambda n: (0,) * len(shape))
    return pl.pallas_call(
        body, grid=(nb,),
        in_specs=[pl.BlockSpec((CHUNK, EVEN_IN), lambda n: (n, 0)),
                  pl.BlockSpec((CHUNK, 256), lambda n: (prev(n), 16)),
                  pl.BlockSpec((CHUNK, EVEN_IN), lambda n: (nxt(n), 0)),
                  pl.BlockSpec((2, CHUNK, W), lambda n: (0, n, 0)),
                  pl.BlockSpec((2, CHUNK, W), lambda n: (0, nxt(n), 0)),
                  pl.BlockSpec((CHUNK, W), lambda n: (n, 0)),
                  pl.BlockSpec((CHUNK, W), lambda n: (nxt(n), 0)),
                  pl.BlockSpec((CHUNK, 128), lambda n: (n, 0)),
                  pl.BlockSpec((CHUNK, 128), lambda n: (nxt(n), 0)),
                  pl.BlockSpec((CHUNK, 384), lambda n: (n, 0)),
                  pl.BlockSpec((CHUNK, 384), lambda n: (prev(n), 0)),
                  pl.BlockSpec((CHUNK, 384), lambda n: (nxt(n), 0)),
                  full((1, W)), full((1, W)), full((8, 128, 128)), full((8, 128, 128)), full((8, 128, 128)),
                  pl.BlockSpec(memory_space=pltpu.SMEM)],
        out_specs=[pl.BlockSpec((CHUNK, EVEN_IN), lambda n: (n, 0)),
                   full((8, 128, 128)), full((128, 128)), full((1, W)), full((1, W)), full((1, 128))],
        out_shape=[jax.ShapeDtypeStruct((S, EVEN_IN), BF), jax.ShapeDtypeStruct((8, 128, 128), F32),
                   jax.ShapeDtypeStruct((128, 128), F32), jax.ShapeDtypeStruct((1, W), F32),
                   jax.ShapeDtypeStruct((1, W), F32), jax.ShapeDtypeStruct((1, 128), F32)],
        scratch_shapes=[pltpu.VMEM((CHUNK, W), F32)],
        name=name, compiler_params=_cp(),
    )(h, h, h, dmix3, dmix3, o, o, l, l, rope, rope, rope, lng.reshape(1, W), lnb.reshape(1, W), ws, wst, bsb, sinks)


def _expm1(x):
    ser = x * (1.0 + x * (0.5 + x * (1.0 / 6.0 + x * (1.0 / 24.0))))
    return jnp.where(jnp.abs(x) < 1e-2, ser, jnp.exp(x) - 1.0)


def _softplus_neg(lam):
    z = -lam
    e = jnp.exp(-jnp.abs(z))
    l1p = jnp.where(e < 1e-3, e * (1.0 - e * (0.5 - e * (1.0 / 3.0))), jnp.log(1.0 + e))
    return jnp.maximum(z, 0.0) + l1p


def _shift_down(x, k, row, fill=0.0):
    return jnp.where(row >= k, pltpu.roll(x, k, 0), fill)


def _shift_up(x, k, row, fill=0.0):
    S = x.shape[0]
    return jnp.where(row < S - k, pltpu.roll(x, S - k, 0), fill)


def _lru_gates(xc, row, cw_ref, cb_ref, wa_ref, wx_ref, ba_ref, bx_ref, lam_ref):
    xconv = (cw_ref[3:4, :] * xc + cw_ref[2:3, :] * _shift_down(xc, 1, row) + cw_ref[1:2, :] * _shift_down(xc, 2, row)
             + cw_ref[0:1, :] * _shift_down(xc, 3, row) + cb_ref[...])
    xb = xconv.astype(BF)
    r = _sig(_dot(xb, wa_ref[...]) + ba_ref[...])
    i = _sig(_dot(xb, wx_ref[...]) + bx_ref[...])
    sp = _softplus_neg(lam_ref[...])
    log_a = -LRU_C * r * sp
    a = jnp.exp(log_a)
    mult = jnp.sqrt(-_expm1(2.0 * log_a))
    return xconv, r, i, sp, a, mult


def _odd_c_fwd(h, cw, cb, wa, wx, ba, bx, lam, name):
    S = h.shape[0]

    def body(xc_ref, cg_ref, cw_ref, cb_ref, wa_ref, wx_ref, ba_ref, bx_ref, lam_ref, mix_ref, hst_ref):
        row = lax.broadcasted_iota(jnp.int32, (S, 128), 0)
        xconv, r, i, sp, a, mult = _lru_gates(xc_ref[...], row, cw_ref, cb_ref, wa_ref, wx_ref, ba_ref, bx_ref, lam_ref)
        aa = a
        bb = mult * (i * xconv)
        k = 1
        while k < S:
            bb = aa * _shift_down(bb, k, row) + bb
            if 2 * k < S:
                aa = aa * _shift_down(aa, k, row, 1.0)
            k *= 2
        hst_ref[...] = bb
        cg = cg_ref[...]
        mix_ref[...] = (bb * (cg * _sig(cg))).astype(BF)

    col = lambda off: pl.BlockSpec((S, 128), lambda j: (0, off + j))
    vec = pl.BlockSpec((1, 128), lambda j: (0, j))
    mat = pl.BlockSpec((None, 128, 128), lambda j: (j, 0, 0))
    return pl.pallas_call(
        body, grid=(8,),
        in_specs=[col(0), col(8), pl.BlockSpec((4, 128), lambda j: (0, j)), vec, mat, mat, vec, vec, vec],
        out_specs=[pl.BlockSpec((None, S, 128), lambda j: (0, 0, j)), pl.BlockSpec((S, 128), lambda j: (0, j))],
        out_shape=[jax.ShapeDtypeStruct((2, S, W), BF), jax.ShapeDtypeStruct((S, W), F32)],
        name=name, compiler_params=_cp(),
    )(h, h, cw, cb.reshape(1, W), wa, wx, ba.reshape(1, W), bx.reshape(1, W), lam.reshape(1, W))


def _pool_sums(x, g, row, shift):
    s2 = x + shift(x, 1, row)
    s4 = s2 + shift(s2, 2, row)
    s8 = s4 + shift(s4, 4, row)
    s16 = s8 + shift(s8, 8, row)
    return jnp.where(g == 0, s2, jnp.where(g == 1, s4, jnp.where(g == 2, s8, s16)))


def _odd_d_fwd(h, mix3, wp, dscale, name):
    S = h.shape[0]

    def body(xd_ref, dg_ref, wp_ref, ds_ref, mix_in, mix_ref):
        g = pl.program_id(0)
        row = lax.broadcasted_iota(jnp.int32, (S, 256), 0)
        xd = xd_ref[...]
        cnt = jnp.minimum(row + 1, jnp.left_shift(2, g)).astype(F32)
        pooled = _pool_sums(xd, g, row, _shift_down) / cnt - xd
        mixed = _dot(pooled.astype(BF), wp_ref[...])
        dg = dg_ref[...]
        mix_ref[...] = (mixed * ds_ref[...] * (dg * _sig(dg))).astype(BF)

    col = lambda off: pl.BlockSpec((S, 256), lambda g: (0, off + g))
    return pl.pallas_call(
        body, grid=(4,),
        in_specs=[col(8), col(12), pl.BlockSpec((None, 256, 256), lambda g: (g, 0, 0)),
                  pl.BlockSpec((1, 256), lambda g: (0, g)), ANY],
        out_specs=pl.BlockSpec((None, S, 256), lambda g: (1, 0, g)),
        out_shape=jax.ShapeDtypeStruct((2, S, W), BF), input_output_aliases={4: 0},
        name=name, compiler_params=_cp(),
    )(h, h, wp, dscale.reshape(1, W), mix3)


def _odd_c_bwd(h, hst, dmix3, cw, cb, wa, wx, wat, wxt, ba, bx, lam, name):
    S = h.shape[0]

    def body(xc_ref, cg_ref, hst_ref, dc_ref, cw_ref, cb_ref, wa_ref, wx_ref, wat_ref, wxt_ref, ba_ref, bx_ref, lam_ref,
             dh_ref, dcw_ref, dcb_ref, dwa_ref, dwx_ref, dba_ref, dbx_ref, dlam_ref):
        row = lax.broadcasted_iota(jnp.int32, (S, 128), 0)
        xc = xc_ref[...]
        xconv, r, i, sp, a, mult = _lru_gates(xc, row, cw_ref, cb_ref, wa_ref, wx_ref, ba_ref, bx_ref, lam_ref)
        hst = hst_ref[...]
        cg = cg_ref[...]
        sg, dsg = _silu_grad(cg)
        dc = dc_ref[...]
        dh_ref[1] = (dc * hst * dsg).astype(BF)
        aa = _shift_up(a, 1, row)
        bb = dc * sg
        k = 1
        while k < S:
            bb = aa * _shift_up(bb, k, row) + bb
            if 2 * k < S:
                aa = aa * _shift_up(aa, k, row, 1.0)
            k *= 2
        lam_t = bb
        da = lam_t * _shift_down(hst, 1, row)
        ix = i * xconv
        dmult = lam_t * ix
        di = lam_t * mult * xconv
        dxconv = lam_t * mult * i
        dlog_a = da * a - dmult * (a * a / mult)
        dr = dlog_a * (-LRU_C * sp)
        dsp = jnp.sum(dlog_a * (-LRU_C * r), axis=0, keepdims=True)
        dlam_ref[...] = dsp * (-_sig(-lam_ref[...]))
        dpa = dr * r * (1.0 - r)
        dpx = di * i * (1.0 - i)
        dpab = dpa.astype(BF)
        dpxb = dpx.astype(BF)
        xb = xconv.astype(BF)
        dxconv = dxconv + _dot(dpab, wat_ref[...]) + _dot(dpxb, wxt_ref[...])
        dwa_ref[...] = _dot_tn(xb, dpab)
        dwx_ref[...] = _dot_tn(xb, dpxb)
        dba_ref[...] = jnp.sum(dpa, axis=0, keepdims=True)
        dbx_ref[...] = jnp.sum(dpx, axis=0, keepdims=True)
        dh_ref[0] = (cw_ref[3:4, :] * dxconv + cw_ref[2:3, :] * _shift_up(dxconv, 1, row)
                     + cw_ref[1:2, :] * _shift_up(dxconv, 2, row) + cw_ref[0:1, :] * _shift_up(dxconv, 3, row)).astype(BF)
        for j in range(4):
            src = xc if j == 3 else _shift_down(xc, 3 - j, row)
            dcw_ref[j:j + 1, :] = jnp.sum(dxconv * src, axis=0, keepdims=True)
        dcb_ref[...] = jnp.sum(dxconv, axis=0, keepdims=True)

    col = lambda off: pl.BlockSpec((S, 128), lambda j: (0, off + j))
    vec = pl.BlockSpec((1, 128), lambda j: (0, j))
    mat = pl.BlockSpec((None, 128, 128), lambda j: (j, 0, 0))
    vshape = jax.ShapeDtypeStruct((1, W), F32)
    mshape = jax.ShapeDtypeStruct((8, 128, 128), F32)
    return pl.pallas_call(
        body, grid=(8,),
        in_specs=[col(0), col(8), col(0), pl.BlockSpec((None, S, 128), lambda j: (0, 0, j)),
                  pl.BlockSpec((4, 128), lambda j: (0, j)), vec, mat, mat, mat, mat, vec, vec, vec],
        out_specs=[pl.BlockSpec((2, S, 128), lambda j: (0, 0, j)), pl.BlockSpec((4, 128), lambda j: (0, j)), vec,
                   mat, mat, vec, vec, vec],
        out_shape=[jax.ShapeDtypeStruct((4, S, W), BF), jax.ShapeDtypeStruct((4, W), F32), vshape, mshape, mshape,
                   vshape, vshape, vshape],
        name=name, compiler_params=_cp(56),
    )(h, h, hst, dmix3, cw, cb.reshape(1, W), wa, wx, wat, wxt, ba.reshape(1, W), bx.reshape(1, W), lam.reshape(1, W))


def _odd_d_bwd(h, dmix3, dh4, wp, wpt, dscale, name):
    S = h.shape[0]

    def body(xd_ref, dg_ref, dd_ref, wp_ref, wpt_ref, ds_ref, dh_in, dh_ref, dwp_ref, dds_ref):
        g = pl.program_id(0)
        row = lax.broadcasted_iota(jnp.int32, (S, 256), 0)
        xd = xd_ref[...]
        cnt = jnp.minimum(row + 1, jnp.left_shift(2, g)).astype(F32)
        pooled = _pool_sums(xd, g, row, _shift_down) / cnt - xd
        pb = pooled.astype(BF)
        mixed = _dot(pb, wp_ref[...])
        dg = dg_ref[...]
        sg, dsg = _silu_grad(dg)
        dd = dd_ref[...]
        dmixed = dd * ds_ref[...] * sg
        dds_ref[...] = jnp.sum(dd * mixed * sg, axis=0, keepdims=True)
        dh_ref[1] = (dd * mixed * ds_ref[...] * dsg).astype(BF)
        dmb = dmixed.astype(BF)
        dpooled = _dot(dmb, wpt_ref[...])
        dwp_ref[...] = _dot_tn(pb, dmb)
        dh_ref[0] = (_pool_sums(dpooled / cnt, g, row, _shift_up) - dpooled).astype(BF)

    col = lambda off: pl.BlockSpec((S, 256), lambda g: (0, off + g))
    mat = pl.BlockSpec((None, 256, 256), lambda g: (g, 0, 0))
    vec = pl.BlockSpec((1, 256), lambda g: (0, g))
    return pl.pallas_call(
        body, grid=(4,),
        in_specs=[col(8), col(12), pl.BlockSpec((None, S, 256), lambda g: (1, 0, g)), mat, mat, vec, ANY],
        out_specs=[pl.BlockSpec((2, S, 256), lambda g: (1, 0, g)), mat, vec],
        out_shape=[jax.ShapeDtypeStruct((4, S, W), BF), jax.ShapeDtypeStruct((4, 256, 256), F32),
                   jax.ShapeDtypeStruct((1, W), F32)],
        input_output_aliases={6: 0}, name=name, compiler_params=_cp(56),
    )(h, h, dmix3, wp, wpt, dscale.reshape(1, W), dh4)


def _peer(d):
    x, y, c = lax.axis_index("x"), lax.axis_index("y"), lax.axis_index("c")
    px = 1 - x if d & 4 else x
    py = 1 - y if d & 2 else y
    pc = 1 - c if d & 1 else c
    return (px, py, pc), 4 * px + 2 * py + pc


def _all_gather(xs, name):
    R, C = xs.shape

    def body(x_ref, out_ref, send_sems, recv_sems, local_sem):
        _, me = _peer(0)
        mine = pltpu.make_async_copy(x_ref, out_ref.at[me], local_sem)
        mine.start()
        copies = []
        for d in range(1, N_DEV):
            peer, _ = _peer(d)
            cp = pltpu.make_async_remote_copy(src_ref=x_ref, dst_ref=out_ref.at[me], send_sem=send_sems.at[d - 1],
                                              recv_sem=recv_sems.at[d - 1], device_id=peer, device_id_type=MESH)
            cp.start()
            copies.append(cp)
        for cp in copies:
            cp.wait()
        mine.wait()

    return pl.pallas_call(
        body, in_specs=[ANY], out_specs=ANY, out_shape=jax.ShapeDtypeStruct((N_DEV, R, C), xs.dtype),
        scratch_shapes=[pltpu.SemaphoreType.DMA((N_DEV - 1,)), pltpu.SemaphoreType.DMA((N_DEV - 1,)),
                        pltpu.SemaphoreType.DMA],
        name=name,
    )(xs)


def _exchange(g8, name):
    _, R, C = g8.shape

    def body(g_ref, out_ref, send_sems, recv_sems, local_sem):
        _, me = _peer(0)
        mine = pltpu.make_async_copy(g_ref.at[me], out_ref.at[0], local_sem)
        mine.start()
        copies = []
        for d in range(1, N_DEV):
            peer, pidx = _peer(d)
            cp = pltpu.make_async_remote_copy(src_ref=g_ref.at[pidx], dst_ref=out_ref.at[d], send_sem=send_sems.at[d - 1],
                                              recv_sem=recv_sems.at[d - 1], device_id=peer, device_id_type=MESH)
            cp.start()
            copies.append(cp)
        for cp in copies:
            cp.wait()
        mine.wait()

    return pl.pallas_call(
        body, in_specs=[ANY], out_specs=ANY, out_shape=jax.ShapeDtypeStruct((N_DEV, R, C), g8.dtype),
        scratch_shapes=[pltpu.SemaphoreType.DMA((N_DEV - 1,)), pltpu.SemaphoreType.DMA((N_DEV - 1,)),
                        pltpu.SemaphoreType.DMA],
        name=name,
    )(g8)


def _sum8(r8, tr, name):
    _, R, C = r8.shape
    tr = min(tr, R)
    assert R % tr == 0

    def body(r_ref, o_ref):
        acc = r_ref[0]
        for d in range(1, N_DEV):
            acc = acc + r_ref[d]
        o_ref[...] = acc

    return pl.pallas_call(
        body, grid=(R // tr,), in_specs=[pl.BlockSpec((N_DEV, tr, C), lambda i: (0, i, 0))],
        out_specs=pl.BlockSpec((tr, C), lambda i: (i, 0)), out_shape=jax.ShapeDtypeStruct((R, C), F32),
        name=name, compiler_params=_cp(),
    )(r8)


def _adamw(w, g, m, v, tr, name):
    R, C = w.shape
    tr = min(tr, R)

    def body(w_ref, g_ref, m_ref, v_ref, d_ref, m2_ref, v2_ref):
        gg = g_ref[...]
        m2 = B1 * m_ref[...] + (1.0 - B1) * gg
        v2 = B2 * v_ref[...] + (1.0 - B2) * (gg * gg)
        m_hat = m2 / (1.0 - B1 ** STEP)
        v_hat = v2 / (1.0 - B2 ** STEP)
        d_ref[...] = -LR * (m_hat / (jnp.sqrt(v_hat) + ADAM_EPS) + WD * w_ref[...])
        m2_ref[...] = m2
        v2_ref[...] = v2

    blk = pl.BlockSpec((tr, C), lambda i: (i, 0))
    shp = jax.ShapeDtypeStruct((R, C), F32)
    return pl.pallas_call(
        body, grid=(R // tr,), in_specs=[blk] * 4, out_specs=[blk] * 3, out_shape=[shp] * 3,
        name=name, compiler_params=_cp(),
    )(w, g, m, v)


def _rep_pack(a):
    n = a.size
    pad = (-n) % 1024
    f = a.reshape(-1)
    if pad:
        f = jnp.concatenate([f, jnp.zeros((pad,), a.dtype)])
    return f.reshape(N_DEV, -1, 128)


def _rep_unpack(p, shape):
    n = 1
    for s in shape:
        n *= s
    return p.reshape(-1)[:n].reshape(shape)


def _sh_pack(a, axis):
    shp = a.shape
    a = a.reshape(shp[:axis] + (N_DEV, shp[axis] // N_DEV) + shp[axis + 1:])
    return jnp.moveaxis(a, axis, 0).reshape(N_DEV, -1, 128)


def _sh_unpack(p, shape, axis):
    a = p.reshape((N_DEV,) + shape[:axis] + (shape[axis] // N_DEV,) + shape[axis + 1:])
    return jnp.moveaxis(a, 0, axis).reshape(shape)


def _pad_rows(a, mult=8):
    pad = (-a.shape[-2]) % mult
    if pad:
        a = jnp.concatenate([a, jnp.zeros(a.shape[:-2] + (pad, a.shape[-1]), a.dtype)], axis=-2)
    return a


REP = ["even_a_ln_g", "even_a_ln_b", "even_a_ws", "even_a_bs", "even_b_sinks", "even_ln_g", "even_ln_b",
       "odd_w_a", "odd_w_x"]
SH = [("odd_conv_w", (2, 4, W), 2), ("odd_conv_b", (2, W), 1), ("odd_b_a", (2, W), 1), ("odd_b_x", (2, W), 1),
      ("odd_lam", (2, W), 1), ("odd_w_pool", (2, 4, 256, 256), 2), ("odd_d_scale", (2, W), 1),
      ("odd_ln_g", (2, D), 1), ("odd_ln_b", (2, D), 1)]
BIG = ["even_w_in", "even_w_out", "odd_w_in", "odd_w_out"]
NAMES = ["even_w_in", "even_a_ln_g", "even_a_ln_b", "even_a_ws", "even_a_bs", "even_b_sinks", "even_w_out",
         "even_ln_g", "even_ln_b", "odd_w_in", "odd_conv_w", "odd_conv_b", "odd_w_a", "odd_b_a", "odd_w_x", "odd_b_x",
         "odd_lam", "odd_w_pool", "odd_d_scale", "odd_w_out", "odd_ln_g", "odd_ln_b"]


def _rope_table(positions):
    S = positions.shape[0]
    inv = ROPE_THETA ** (-jnp.arange(0, 16, 2, dtype=F32) / 16)
    ang = positions.astype(F32)[:, None] * inv
    cos, sin = jnp.cos(ang), jnp.sin(ang)
    one, zero = jnp.ones((S, 48), F32), jnp.zeros((S, 48), F32)
    z8 = jnp.zeros((S, 8), F32)
    c64 = jnp.concatenate([cos, cos, one], axis=1)
    s1 = jnp.concatenate([-sin, z8, zero], axis=1)
    s2 = jnp.concatenate([z8, sin, zero], axis=1)
    return jnp.concatenate([c64, c64, s1, s1, s2, s2], axis=1)


def kernel(x, positions, even_w_in, even_a_ln_g, even_a_ln_b, even_a_ws, even_a_bs, even_b_sinks, even_w_out, even_ln_g, even_ln_b, odd_w_in, odd_conv_w, odd_conv_b, odd_w_a, odd_b_a, odd_w_x, odd_b_x, odd_lam, odd_w_pool, odd_d_scale, odd_w_out, odd_ln_g, odd_ln_b, loss_target, m_even_w_in, m_even_a_ln_g, m_even_a_ln_b, m_even_a_ws, m_even_a_bs, m_even_b_sinks, m_even_w_out, m_even_ln_g, m_even_ln_b, m_odd_w_in, m_odd_conv_w, m_odd_conv_b, m_odd_w_a, m_odd_b_a, m_odd_w_x, m_odd_b_x, m_odd_lam, m_odd_w_pool, m_odd_d_scale, m_odd_w_out, m_odd_ln_g, m_odd_ln_b, v_even_w_in, v_even_a_ln_g, v_even_a_ln_b, v_even_a_ws, v_even_a_bs, v_even_b_sinks, v_even_w_out, v_even_ln_g, v_even_ln_b, v_odd_w_in, v_odd_conv_w, v_odd_conv_b, v_odd_w_a, v_odd_b_a, v_odd_w_x, v_odd_b_x, v_odd_lam, v_odd_w_pool, v_odd_d_scale, v_odd_w_out, v_odd_ln_g, v_odd_ln_b):
    args = (even_w_in, even_a_ln_g, even_a_ln_b, even_a_ws, even_a_bs, even_b_sinks, even_w_out, even_ln_g, even_ln_b,
            odd_w_in, odd_conv_w, odd_conv_b, odd_w_a, odd_b_a, odd_w_x, odd_b_x, odd_lam, odd_w_pool, odd_d_scale,
            odd_w_out, odd_ln_g, odd_ln_b)
    margs = (m_even_w_in, m_even_a_ln_g, m_even_a_ln_b, m_even_a_ws, m_even_a_bs, m_even_b_sinks, m_even_w_out,
             m_even_ln_g, m_even_ln_b, m_odd_w_in, m_odd_conv_w, m_odd_conv_b, m_odd_w_a, m_odd_b_a, m_odd_w_x,
             m_odd_b_x, m_odd_lam, m_odd_w_pool, m_odd_d_scale, m_odd_w_out, m_odd_ln_g, m_odd_ln_b)
    vargs = (v_even_w_in, v_even_a_ln_g, v_even_a_ln_b, v_even_a_ws, v_even_a_bs, v_even_b_sinks, v_even_w_out,
             v_even_ln_g, v_even_ln_b, v_odd_w_in, v_odd_conv_w, v_odd_conv_b, v_odd_w_a, v_odd_b_a, v_odd_w_x,
             v_odd_b_x, v_odd_lam, v_odd_w_pool, v_odd_d_scale, v_odd_w_out, v_odd_ln_g, v_odd_ln_b)
    wts = dict(zip(NAMES, args))
    mom = dict(zip(NAMES, margs))
    var = dict(zip(NAMES, vargs))
    S = x.shape[1]
    x0 = x[0]
    rope = _rope_table(positions[0])

    wt_in, w_out = {}, {}
    for kind, n_in in (("even", EVEN_IN), ("odd", ODD_IN)):
        for j in range(2):
            t = jnp.transpose(wts[kind + "_w_in"][j]).astype(BF)
            wt_in[kind, j] = _all_gather(t, f"ag_{kind}_w_in").reshape(n_in, D)
            w_out[kind, j] = _all_gather(wts[kind + "_w_out"][j].astype(BF), f"ag_{kind}_w_out").reshape(D, D)
    sh_local = _pad_rows(jnp.concatenate([wts[nm].reshape(-1, 128) for nm, _, _ in SH], axis=0))
    sh_all = _all_gather(sh_local, "ag_small")
    full = {}
    off = 0
    for nm, shape, axis in SH:
        r = wts[nm].size // 128
        full[nm] = _sh_unpack(sh_all[:, off:off + r, :], shape, axis)
        off += r
    for nm in REP:
        full[nm] = wts[nm]

    tri = jnp.tril(jnp.ones((128, 128), bool))
    saved = []
    xf, xb = x0, x0.astype(BF)
    for layer in range(4):
        j = layer // 2
        if layer % 2 == 0:
            h = _mm_nt(xb, wt_in["even", j], 1024, 768, "mm_h_even")
            bsb = jnp.broadcast_to(full["even_a_bs"][j][:, :, None], (8, 128, 128))
            mix3, o, l = _even_fwd(h, rope, full["even_a_ln_g"][j], full["even_a_ln_b"][j], full["even_a_ws"][j], bsb,
                                   full["even_b_sinks"][j], "even_fwd")
            z, xn, xnb = _mm_out_ln(mix3, w_out["even", j], xf, full["even_ln_g"][j], full["even_ln_b"][j], "mm_out_ln")
            saved.append((xb, h, mix3, z, (o, l, bsb)))
        else:
            h = _mm_nt(xb, wt_in["odd", j], 1024, 512, "mm_h_odd")
            wa, wx = full["odd_w_a"][j].astype(BF), full["odd_w_x"][j].astype(BF)
            wp = full["odd_w_pool"][j].astype(BF)
            mix3, hst = _odd_c_fwd(h, full["odd_conv_w"][j], full["odd_conv_b"][j], wa, wx, full["odd_b_a"][j],
                                   full["odd_b_x"][j], full["odd_lam"][j], "odd_c_fwd")
            mix3 = _odd_d_fwd(h, mix3, wp, full["odd_d_scale"][j], "odd_d_fwd")
            z, xn, xnb = _mm_out_ln(mix3, w_out["odd", j], xf, full["odd_ln_g"][j], full["odd_ln_b"][j], "mm_out_ln")
            saved.append((xb, h, mix3, z, (hst, wa, wx, wp)))
        xf, xb = xn, xnb

    dxn, part = _loss_grad(xf, loss_target[0])
    loss = lax.psum(part[0, 0] * (0.5 / D), ("x", "y", "c"))

    gsum = {nm: [None, None] for nm in NAMES}
    gbig = {}
    for layer in (3, 2, 1, 0):
        j = layer // 2
        xb, h, mix3, z, extra = saved[layer]
        kind = "even" if layer % 2 == 0 else "odd"
        dz, dzb, dg, dbeta = _ln_bwd(dxn, z, full[kind + "_ln_g"][j], "ln_bwd")
        gsum[kind + "_ln_g"][j] = dg.reshape(D)
        gsum[kind + "_ln_b"][j] = dbeta.reshape(D)
        dmix3 = _mm_nt(dzb, w_out[kind, j], 1024, 512, "mm_dmix", out3=True)
        gbig[kind + "_w_out", j] = _mm_tn(mix3, dzb, 512, "mm_dw_out")
        if kind == "even":
            o, l, bsb = extra
            ws = full["even_a_ws"][j]
            dh, dws, dbs, dlng, dlnb, dsink = _even_bwd(
                h, dmix3, o, l, rope, full["even_a_ln_g"][j], full["even_a_ln_b"][j], ws, jnp.swapaxes(ws, 1, 2), bsb,
                full["even_b_sinks"][j], "even_bwd")
            gsum["even_a_ws"][j] = dws
            gsum["even_a_bs"][j] = jnp.transpose(dbs[:, :8])
            gsum["even_a_ln_g"][j] = dlng.reshape(W)
            gsum["even_a_ln_b"][j] = dlnb.reshape(W)
            gsum["even_b_sinks"][j] = dsink[0, :16]
            gbig["even_w_in", j] = _mm_tn(dh, xb, 384, "mm_dw_in_even")
            dxn = _mm_nn_res(dh, wt_in["even", j], dz, 512, 768, "mm_dx_even")
        else:
            hst, wa, wx, wp = extra
            dh4, dcw, dcb, dwa, dwx, dba, dbx, dlam = _odd_c_bwd(
                h, hst, dmix3, full["odd_conv_w"][j], full["odd_conv_b"][j], wa, wx, jnp.swapaxes(wa, 1, 2),
                jnp.swapaxes(wx, 1, 2), full["odd_b_a"][j], full["odd_b_x"][j], full["odd_lam"][j], "odd_c_bwd")
            dh4, dwp, dds = _odd_d_bwd(h, dmix3, dh4, wp, jnp.swapaxes(wp, 1, 2), full["odd_d_scale"][j], "odd_d_bwd")
            gsum["odd_conv_w"][j], gsum["odd_conv_b"][j] = dcw, dcb.reshape(W)
            gsum["odd_w_a"][j], gsum["odd_w_x"][j] = dwa, dwx
            gsum["odd_b_a"][j], gsum["odd_b_x"][j], gsum["odd_lam"][j] = dba.reshape(W), dbx.reshape(W), dlam.reshape(W)
            gsum["odd_w_pool"][j], gsum["odd_d_scale"][j] = dwp, dds.reshape(W)
            gbig["odd_w_in", j] = _mm_tn(dh4, xb, 512, "mm_dw_in_odd")
            dxn = _mm_nn_res(dh4, wt_in["odd", j], dz, 512, 512, "mm_dx_odd")
    grad_x = dxn[None]

    out_g, out_d, out_m, out_v = {}, {}, {}, {}
    for nm in BIG:
        kind = nm.split("_")[0]
        gl = []
        for j in range(2):
            g = gbig[nm, j]
            r8 = _exchange(g.reshape(N_DEV, g.shape[0] // N_DEV, D), f"rs_{nm}")
            gs = _sum8(r8, 32, f"sum_{nm}")
            gl.append(jnp.transpose(gs) if nm.endswith("w_in") else gs)
        g = jnp.stack(gl)
        shp = wts[nm].shape
        d2, m2, v2 = _adamw(wts[nm].reshape(-1, shp[-1]), g.reshape(-1, shp[-1]), mom[nm].reshape(-1, shp[-1]),
                            var[nm].reshape(-1, shp[-1]), 512, f"adamw_{nm}")
        out_g[nm], out_d[nm], out_m[nm], out_v[nm] = g, d2.reshape(shp), m2.reshape(shp), v2.reshape(shp)

    rep_rows = [_rep_pack(jnp.stack(gsum[nm]).reshape(wts[nm].shape)) for nm in REP]
    sh_rows = [_sh_pack(jnp.stack(gsum[nm]).reshape(shape), axis) for nm, shape, axis in SH]
    n_rep = sum(p.shape[1] for p in rep_rows)
    packed = _pad_rows(jnp.concatenate(rep_rows + sh_rows, axis=1))
    red = _sum8(_exchange(packed, "rs_small"), 1 << 20, "sum_small")
    n_rep8 = n_rep + (-n_rep) % 8
    rep_all = _all_gather(_pad_rows(red[:n_rep]), "ag_small_grad")
    g_small = {}
    off = 0
    for nm, p in zip(REP, rep_rows):
        r = p.shape[1]
        g_small[nm] = _rep_unpack(rep_all[:, off:off + r, :], wts[nm].shape)
        off += r
    off = n_rep
    for (nm, shape, axis), p in zip(SH, sh_rows):
        r = p.shape[1]
        g_small[nm] = red[off:off + r].reshape(wts[nm].shape)
        off += r

    def rows(a):
        f = a.reshape(-1)
        pad = (-f.shape[0]) % 128
        if pad:
            f = jnp.concatenate([f, jnp.zeros((pad,), a.dtype)])
        return f.reshape(-1, 128)

    small = REP + [nm for nm, _, _ in SH]
    cat = lambda src: _pad_rows(jnp.concatenate([rows(src[nm]) for nm in small], axis=0))
    d2, m2, v2 = _adamw(cat(wts), cat(g_small), cat(mom), cat(var), 1 << 20, "adamw_small")
    off = 0
    for nm in small:
        n = wts[nm].size
        r = (n + 127) // 128
        shp = wts[nm].shape
        take = lambda a: a[off:off + r].reshape(-1)[:n].reshape(shp)
        out_g[nm], out_d[nm], out_m[nm], out_v[nm] = g_small[nm], take(d2), take(m2), take(v2)
        off += r

    return (loss, grad_x, *[out_g[nm] for nm in NAMES], *[out_d[nm] for nm in NAMES],
            *[out_m[nm] for nm in NAMES], *[out_v[nm] for nm in NAMES])
```

```python
import functools

import jax
import jax.numpy as jnp
from jax import lax
from jax.experimental import pallas as pl
from jax.experimental.pallas import tpu as pltpu

F32 = jnp.float32
BF = jnp.bfloat16
MESH = pl.DeviceIdType.MESH
ANY = pl.BlockSpec(memory_space=pl.ANY)

N_DEV = 8
D = 2048
W = 1024
EVEN_IN = 5376
ODD_IN = 4096
CHUNK = 128
ALPHA = (2 * 4) ** 0.25
LN_EPS = 1e-5
ROPE_THETA = 500000.0
LRU_C = 8.0
LR, B1, B2, ADAM_EPS, WD, STEP = 0.001, 0.9, 0.999, 1e-08, 0.01, 10
NEG = -1e30


def _cp(vmem_mb=48):
    return pltpu.CompilerParams(vmem_limit_bytes=vmem_mb * 1024 * 1024)


def _sig(x):
    return jax.nn.sigmoid(x)


def _silu_grad(x):
    s = _sig(x)
    return x * s, s * (1.0 + x * (1.0 - s))


def _dot(a, b):
    return jnp.dot(a, b, preferred_element_type=F32)


def _dot_nt(a, b):
    return lax.dot_general(a, b, (((1,), (1,)), ((), ())), preferred_element_type=F32)


def _dot_tn(a, b):
    return lax.dot_general(a, b, (((0,), (0,)), ((), ())), preferred_element_type=F32)


def _coords():
    return lax.axis_index("x"), lax.axis_index("y"), lax.axis_index("c")


def _chip(j):
    x, y, _ = _coords()
    return (1 - x if j & 2 else x), (1 - y if j & 1 else y)


class _Comm:
    def start(self, ins, outs, sems):
        for cp in self.copies(ins, outs, sems):
            cp.start()

    def wait(self, ins, outs, sems):
        for cp in self.copies(ins, outs, sems):
            cp.wait()


class _AgIci(_Comm):
    def __init__(self, blocks):
        n = len(blocks)
        self.inputs = list(blocks)
        self.out_shapes = [jax.ShapeDtypeStruct((N_DEV,) + b.shape, b.dtype) for b in blocks]
        self.sem_shapes = [pltpu.SemaphoreType.DMA((n, 3)), pltpu.SemaphoreType.DMA((n, 3)), pltpu.SemaphoreType.DMA((n,))]

    def copies(self, ins, outs, sems):
        send, recv, loc = sems
        x, y, c = _coords()
        me = 4 * x + 2 * y + c
        res = []
        for k, (src, dst) in enumerate(zip(ins, outs)):
            res.append(pltpu.make_async_copy(src, dst.at[me], loc.at[k]))
            for j in (1, 2, 3):
                px, py = _chip(j)
                res.append(pltpu.make_async_remote_copy(
                    src_ref=src, dst_ref=dst.at[me], send_sem=send.at[k, j - 1], recv_sem=recv.at[k, j - 1],
                    device_id=(px, py, c), device_id_type=MESH))
        return res


class _RsIci(_Comm):
    def __init__(self, sums):
        n = len(sums)
        self.inputs = list(sums)
        self.out_shapes = [jax.ShapeDtypeStruct((3,) + s.shape[1:], s.dtype) for s in sums]
        self.sem_shapes = [pltpu.SemaphoreType.DMA((n, 3)), pltpu.SemaphoreType.DMA((n, 3))]

    def copies(self, ins, outs, sems):
        send, recv = sems
        _, _, c = _coords()
        res = []
        for k, (src, dst) in enumerate(zip(ins, outs)):
            for j in (1, 2, 3):
                px, py = _chip(j)
                res.append(pltpu.make_async_remote_copy(
                    src_ref=src.at[j], dst_ref=dst.at[j - 1], send_sem=send.at[k, j - 1], recv_sem=recv.at[k, j - 1],
                    device_id=(px, py, c), device_id_type=MESH))
        return res


def _pcall(body, *, grid, in_specs, out_specs, out_shape, name, scratch=(), vmem=48, comm=None):
    in_specs, out_specs, out_shape, scratch = list(in_specs), list(out_specs), list(out_shape), list(scratch)
    if comm is None:
        call = pl.pallas_call(body, grid=grid, in_specs=in_specs, out_specs=out_specs, out_shape=out_shape,
                              scratch_shapes=scratch, name=name, compiler_params=_cp(vmem))
        return lambda *args: (call(*args), [])
    n_in, n_out, n_scr = len(in_specs), len(out_specs), len(scratch)
    c_in, c_out = len(comm.inputs), len(comm.out_shapes)

    def wrapped(*refs):
        ins, cins = refs[:n_in], refs[n_in:n_in + c_in]
        o0 = n_in + c_in
        outs, couts = refs[o0:o0 + n_out], refs[o0 + n_out:o0 + n_out + c_out]
        s0 = o0 + n_out + c_out
        scr, sems = refs[s0:s0 + n_scr], refs[s0 + n_scr:]
        ids = [pl.program_id(a) for a in range(len(grid))]
        first = functools.reduce(jnp.logical_and, [i == 0 for i in ids])
        last = functools.reduce(jnp.logical_and, [i == g - 1 for i, g in zip(ids, grid)])

        @pl.when(first)
        def _():
            comm.start(cins, couts, sems)

        body(*ins, *outs, *scr)

        @pl.when(last)
        def _():
            comm.wait(cins, couts, sems)

    call = pl.pallas_call(wrapped, grid=grid, in_specs=in_specs + [ANY] * c_in, out_specs=out_specs + [ANY] * c_out,
                          out_shape=out_shape + list(comm.out_shapes), scratch_shapes=scratch + list(comm.sem_shapes),
                          name=name, compiler_params=_cp(vmem))

    def run(*args):
        res = call(*args, *comm.inputs)
        return res[:n_out], res[n_out:]

    return run


def _comm_only(comm, name):
    c_in, c_out = len(comm.inputs), len(comm.out_shapes)

    def body(*refs):
        cins, couts, sems = refs[:c_in], refs[c_in:c_in + c_out], refs[c_in + c_out:]
        comm.start(cins, couts, sems)
        comm.wait(cins, couts, sems)

    return pl.pallas_call(body, in_specs=[ANY] * c_in, out_specs=[ANY] * c_out, out_shape=list(comm.out_shapes),
                          scratch_shapes=list(comm.sem_shapes), name=name)(*comm.inputs)


def _ag_d2d(bufs, name):
    n = len(bufs)

    def body(*refs):
        outs, (send, recv) = refs[n:2 * n], refs[2 * n:]
        x, y, c = _coords()
        cps = []
        for k in range(n):
            for j in range(4):
                px, py = _chip(j)
                blk = outs[k].at[4 * px + 2 * py + c]
                cps.append(pltpu.make_async_remote_copy(src_ref=blk, dst_ref=blk, send_sem=send.at[k, j],
                                                        recv_sem=recv.at[k, j], device_id=(x, y, 1 - c), device_id_type=MESH))
        for cp in cps:
            cp.start()
        for cp in cps:
            cp.wait()

    return pl.pallas_call(
        body, in_specs=[ANY] * n, out_specs=[ANY] * n, out_shape=[jax.ShapeDtypeStruct(b.shape, b.dtype) for b in bufs],
        input_output_aliases={k: k for k in range(n)},
        scratch_shapes=[pltpu.SemaphoreType.DMA((n, 4)), pltpu.SemaphoreType.DMA((n, 4))], name=name)(*bufs)


def _rs_d2d(parts, name):
    n = len(parts)

    def body(*refs):
        ins, own, got = refs[:n], refs[n:2 * n], refs[2 * n:3 * n]
        send, recv, loc = refs[3 * n:]
        x, y, c = _coords()
        cps = []
        for k in range(n):
            for j in range(4):
                px, py = _chip(j)
                cps.append(pltpu.make_async_copy(ins[k].at[4 * px + 2 * py + c], own[k].at[j], loc.at[k, j]))
                cps.append(pltpu.make_async_remote_copy(
                    src_ref=ins[k].at[4 * px + 2 * py + 1 - c], dst_ref=got[k].at[j], send_sem=send.at[k, j],
                    recv_sem=recv.at[k, j], device_id=(x, y, 1 - c), device_id_type=MESH))
        for cp in cps:
            cp.start()
        for cp in cps:
            cp.wait()

    shapes = [jax.ShapeDtypeStruct((4,) + p.shape[1:], p.dtype) for p in parts]
    res = pl.pallas_call(
        body, in_specs=[ANY] * n, out_specs=[ANY] * (2 * n), out_shape=shapes + shapes,
        scratch_shapes=[pltpu.SemaphoreType.DMA((n, 4)), pltpu.SemaphoreType.DMA((n, 4)), pltpu.SemaphoreType.DMA((n, 4))],
        name=name)(*parts)
    return res[:n], res[n:]


def _add_pairs(a4, b4, name):
    _, R, C = a4.shape

    def body(a_ref, b_ref, o_ref):
        o_ref[...] = (a_ref[...].astype(F32) + b_ref[...].astype(F32)).astype(BF)

    blk = pl.BlockSpec((None, R, C), lambda j: (j, 0, 0))
    return pl.pallas_call(body, grid=(4,), in_specs=[blk, blk], out_specs=blk,
                          out_shape=jax.ShapeDtypeStruct(a4.shape, BF), name=name, compiler_params=_cp())(a4, b4)


def _rs_final(s4, r3, name):
    _, R, C = s4.shape
    tr = R // 2

    def body(s_ref, r_ref, o_ref):
        o_ref[...] = ((s_ref[...].astype(F32) + r_ref[0].astype(F32)) + r_ref[1].astype(F32)) + r_ref[2].astype(F32)

    return pl.pallas_call(
        body, grid=(2,),
        in_specs=[pl.BlockSpec((None, tr, C), lambda i: (0, i, 0)), pl.BlockSpec((3, tr, C), lambda i: (0, i, 0))],
        out_specs=pl.BlockSpec((tr, C), lambda i: (i, 0)), out_shape=jax.ShapeDtypeStruct((R, C), F32),
        name=name, compiler_params=_cp())(s4, r3)


def _mm_nt(a, w, tm, tn, name, out3=False, comm=None):
    M, K = a.shape
    N = w.shape[0]
    tm = min(tm, M)

    def body(a_ref, w_ref, o_ref):
        o_ref[...] = _dot_nt(a_ref[...], w_ref[...])

    if out3:
        per = W // tn
        out_shape = jax.ShapeDtypeStruct((N // W, M, W), F32)
        out_spec = pl.BlockSpec((None, tm, tn), lambda i, j: (j // per, i, j % per))
    else:
        out_shape = jax.ShapeDtypeStruct((M, N), F32)
        out_spec = pl.BlockSpec((tm, tn), lambda i, j: (i, j))
    (res,), extra = _pcall(
        body, grid=(M // tm, N // tn),
        in_specs=[pl.BlockSpec((tm, K), lambda i, j: (i, 0)), pl.BlockSpec((tn, K), lambda i, j: (j, 0))],
        out_specs=[out_spec], out_shape=[out_shape], name=name, comm=comm)(a, w)
    return res, extra


def _mm_tn(a, b, tm, name):
    K, N = b.shape
    if a.ndim == 3:
        M = a.shape[0] * W
        per = W // tm
        a_spec = pl.BlockSpec((None, K, tm), lambda i: (i // per, 0, i % per))
    else:
        M = a.shape[1]
        a_spec = pl.BlockSpec((K, tm), lambda i: (0, i))

    def body(a_ref, b_ref, o_ref):
        o_ref[...] = _dot_tn(a_ref[...], b_ref[...]).astype(BF)

    return pl.pallas_call(
        body, grid=(M // tm,),
        in_specs=[a_spec, pl.BlockSpec((K, N), lambda i: (0, 0))],
        out_specs=pl.BlockSpec((tm, N), lambda i: (i, 0)),
        out_shape=jax.ShapeDtypeStruct((M, N), BF), name=name, compiler_params=_cp(56),
    )(a, b)


def _mm_nn_res(a, w, res, tm, tk, name, comm=None):
    K, N = w.shape
    if a.ndim == 3:
        M = a.shape[1]
        tm = min(tm, M)
        per = W // tk
        a_spec = pl.BlockSpec((None, tm, tk), lambda i, k: (k // per, i, k % per))
    else:
        M = a.shape[0]
        tm = min(tm, M)
        a_spec = pl.BlockSpec((tm, tk), lambda i, k: (i, k))

    def body(a_ref, w_ref, r_ref, o_ref):
        k = pl.program_id(1)
        d = _dot(a_ref[...], w_ref[...])

        @pl.when(k == 0)
        def _():
            o_ref[...] = ALPHA * r_ref[...] + d

        @pl.when(k > 0)
        def _():
            o_ref[...] += d

    (out,), extra = _pcall(
        body, grid=(M // tm, K // tk),
        in_specs=[a_spec, pl.BlockSpec((tk, N), lambda i, k: (k, 0)), pl.BlockSpec((tm, N), lambda i, k: (i, 0))],
        out_specs=[pl.BlockSpec((tm, N), lambda i, k: (i, 0))],
        out_shape=[jax.ShapeDtypeStruct((M, N), F32)], name=name, comm=comm)(a, w, res)
    return out, extra


def _mm_out_ln(mix3, w_out, x, g, b, name, comm=None):
    S = x.shape[0]
    tm = min(256, S)

    def body(m_ref, w_ref, x_ref, g_ref, b_ref, z_ref, xn_ref, xb_ref):
        acc = _dot(m_ref[0], w_ref[0:W, :]) + _dot(m_ref[1], w_ref[W:2 * W, :])
        z = ALPHA * x_ref[...] + acc
        mu = jnp.mean(z, axis=1, keepdims=True)
        zc = z - mu
        var = jnp.mean(zc * zc, axis=1, keepdims=True)
        xn = zc * lax.rsqrt(var + LN_EPS) * g_ref[...] + b_ref[...]
        z_ref[...] = z
        xn_ref[...] = xn
        xb_ref[...] = xn.astype(BF)

    row = pl.BlockSpec((tm, D), lambda i: (i, 0))
    vec = pl.BlockSpec((1, D), lambda i: (0, 0))
    return _pcall(
        body, grid=(S // tm,),
        in_specs=[pl.BlockSpec((2, tm, W), lambda i: (0, i, 0)), pl.BlockSpec((D, D), lambda i: (0, 0)), row, vec, vec],
        out_specs=[row, row, row],
        out_shape=[jax.ShapeDtypeStruct((S, D), F32), jax.ShapeDtypeStruct((S, D), F32), jax.ShapeDtypeStruct((S, D), BF)],
        name=name, comm=comm)(mix3, w_out, x, g.reshape(1, D), b.reshape(1, D))


def _ln_bwd(dxn, z, g, name):
    S = z.shape[0]
    tm = min(256, S)

    def body(d_ref, z_ref, g_ref, dz_ref, dzb_ref, dg_ref, db_ref):
        i = pl.program_id(0)
        zz = z_ref[...]
        mu = jnp.mean(zz, axis=1, keepdims=True)
        zc = zz - mu
        var = jnp.mean(zc * zc, axis=1, keepdims=True)
        rstd = lax.rsqrt(var + LN_EPS)
        xhat = zc * rstd
        dy = d_ref[...]
        dyg = dy * g_ref[...]
        m1 = jnp.mean(dyg, axis=1, keepdims=True)
        m2 = jnp.mean(dyg * xhat, axis=1, keepdims=True)
        dz = rstd * (dyg - m1 - xhat * m2)
        dz_ref[...] = dz
        dzb_ref[...] = dz.astype(BF)

        @pl.when(i == 0)
        def _():
            dg_ref[...] = jnp.zeros_like(dg_ref)
            db_ref[...] = jnp.zeros_like(db_ref)

        dg_ref[...] += jnp.sum(dy * xhat, axis=0, keepdims=True)
        db_ref[...] += jnp.sum(dy, axis=0, keepdims=True)

    row = pl.BlockSpec((tm, D), lambda i: (i, 0))
    vec = pl.BlockSpec((1, D), lambda i: (0, 0))
    return pl.pallas_call(
        body, grid=(S // tm,), in_specs=[row, row, vec], out_specs=[row, row, vec, vec],
        out_shape=[jax.ShapeDtypeStruct((S, D), F32), jax.ShapeDtypeStruct((S, D), BF),
                   jax.ShapeDtypeStruct((1, D), F32), jax.ShapeDtypeStruct((1, D), F32)],
        name=name, compiler_params=_cp(),
    )(dxn, z, g.reshape(1, D))


def _loss_grad(xn, target):
    S = xn.shape[0]
    tm = min(256, S)

    def body(x_ref, t_ref, d_ref, p_ref):
        i = pl.program_id(0)
        e = x_ref[...] - t_ref[...]
        d_ref[...] = e * (1.0 / D)

        @pl.when(i == 0)
        def _():
            p_ref[...] = jnp.zeros_like(p_ref)

        p_ref[...] += jnp.sum(jnp.sum(e * e, axis=1, keepdims=True), axis=0, keepdims=True)

    row = pl.BlockSpec((tm, D), lambda i: (i, 0))
    return pl.pallas_call(
        body, grid=(S // tm,), in_specs=[row, row],
        out_specs=[row, pl.BlockSpec((8, 128), lambda i: (0, 0))],
        out_shape=[jax.ShapeDtypeStruct((S, D), F32), jax.ShapeDtypeStruct((8, 128), F32)],
        name="loss_grad", compiler_params=_cp(),
    )(xn, target)


def _rope_fwd(t, r_ref):
    return (t * r_ref[:, 0:128] + pltpu.roll(t, 120, 1) * r_ref[:, 128:256]
            + pltpu.roll(t, 8, 1) * r_ref[:, 256:384])


def _rope_bwd(g, r_ref):
    return (g * r_ref[:, 0:128] + pltpu.roll(g * r_ref[:, 128:256], 8, 1)
            + pltpu.roll(g * r_ref[:, 256:384], 120, 1))


def _dup_heads(kb):
    lo = lax.broadcasted_iota(jnp.int32, kb.shape, 1) < 64
    sw = pltpu.roll(kb, 64, 1)
    return [jnp.where(lo, kb, sw).astype(BF), jnp.where(lo, sw, kb).astype(BF)]


def _even_fwd(h, rope, lng, lnb, ws, bsb, sinks, name, comm=None):
    S = h.shape[0]
    nb = S // CHUNK

    def body(h_ref, hp_ref, rc_ref, rp_ref, lng_ref, lnb_ref, ws_ref, bsb_ref, sink_ref, mix_ref, o_ref, l_ref):
        n = pl.program_id(0)
        lane = lax.broadcasted_iota(jnp.int32, (128, 128), 1)
        rowi = lax.broadcasted_iota(jnp.int32, (128, 128), 0)
        tri = rowi >= lane
        lane_lo = lane < 64
        v = h_ref[:, W:2 * W]
        mu = jnp.mean(v, axis=1, keepdims=True)
        vc = v - mu
        var = jnp.mean(vc * vc, axis=1, keepdims=True)
        vn = vc * lax.rsqrt(var + LN_EPS) * lng_ref[...] + lnb_ref[...]
        for g in range(8):
            sl = slice(g * 128, (g + 1) * 128)
            w = jnp.where(tri, ws_ref[g], 0.0).astype(BF)
            m = _dot(w, vn[:, sl].astype(BF)) + bsb_ref[g]
            ag = h_ref[:, 2 * W + g * 128:2 * W + (g + 1) * 128]
            mix_ref[0, :, sl] = (h_ref[:, sl] * m * (ag * _sig(ag))).astype(BF)
        kb = jnp.concatenate([_rope_fwd(hp_ref[:, 0:128], rp_ref), _rope_fwd(h_ref[:, 4096:4224], rc_ref)], axis=0)
        vb = jnp.concatenate([hp_ref[:, 128:256], h_ref[:, 4224:4352]], axis=0)
        k2 = _dup_heads(kb)
        v2 = _dup_heads(vb)
        qi = lax.broadcasted_iota(jnp.int32, (128, 256), 0)
        kj = lax.broadcasted_iota(jnp.int32, (128, 256), 1)
        diff = qi + 128 - kj
        valid = (diff >= 0) & (diff < 128) & ((n > 0) | (kj >= 128))
        lacc = jnp.zeros((128, 128), F32)
        for j in range(8):
            hk = j // 4
            cs = slice(j * 128, (j + 1) * 128)
            qc = _rope_fwd(h_ref[:, 3072 + j * 128:3072 + (j + 1) * 128], rc_ref)
            ocol = jnp.zeros((128, 128), F32)
            for half in range(2):
                hq = 2 * j + half
                hm = lane_lo if half == 0 else jnp.logical_not(lane_lo)
                qm = jnp.where(hm, qc, 0.0).astype(BF)
                s = jnp.where(valid, _dot_nt(qm, k2[hk]) * 0.125, NEG)
                sk = sink_ref[hq]
                mx = jnp.maximum(jnp.max(s, axis=1, keepdims=True), sk)
                p = jnp.exp(s - mx)
                den = jnp.sum(p, axis=1, keepdims=True) + jnp.exp(sk - mx)
                oh = _dot((p / den).astype(BF), v2[hk])
                ocol = jnp.where(hm, oh, ocol)
                lacc = jnp.where(lane == hq, mx + jnp.log(den), lacc)
            bg = h_ref[:, 4352 + j * 128:4352 + (j + 1) * 128]
            o_ref[:, cs] = ocol
            mix_ref[1, :, cs] = (ocol * (bg * _sig(bg))).astype(BF)
        l_ref[...] = lacc

    prev = lambda n: jnp.maximum(n - 1, 0)
    full = lambda shape: pl.BlockSpec(shape, lambda n: (0,) * len(shape))
    return _pcall(
        body, grid=(nb,),
        in_specs=[pl.BlockSpec((CHUNK, EVEN_IN), lambda n: (n, 0)),
                  pl.BlockSpec((CHUNK, 256), lambda n: (prev(n), 16)),
                  pl.BlockSpec((CHUNK, 384), lambda n: (n, 0)),
                  pl.BlockSpec((CHUNK, 384), lambda n: (prev(n), 0)),
                  full((1, W)), full((1, W)), full((8, 128, 128)), full((8, 128, 128)),
                  pl.BlockSpec(memory_space=pltpu.SMEM)],
        out_specs=[pl.BlockSpec((2, CHUNK, W), lambda n: (0, n, 0)),
                   pl.BlockSpec((CHUNK, W), lambda n: (n, 0)),
                   pl.BlockSpec((CHUNK, 128), lambda n: (n, 0))],
        out_shape=[jax.ShapeDtypeStruct((2, S, W), BF), jax.ShapeDtypeStruct((S, W), F32),
                   jax.ShapeDtypeStruct((S, 128), F32)],
        name=name, comm=comm)(h, h, rope, rope, lng.reshape(1, W), lnb.reshape(1, W), ws, bsb, sinks)


def _even_bwd(h, dmix3, o, l, rope, lng, lnb, ws, wst, bsb, sinks, name, comm=None):
    S = h.shape[0]
    nb = S // CHUNK

    def body(h_ref, hp_ref, hn_ref, dm_ref, dmn_ref, o_ref, on_ref, l_ref, ln_ref, rc_ref, rp_ref, rn_ref,
             lng_ref, lnb_ref, ws_ref, wst_ref, bsb_ref, sink_ref,
             dh_ref, dws_ref, dbs_ref, dlng_ref, dlnb_ref, dsink_ref, dvn_ref):
        n = pl.program_id(0)

        @pl.when(n == 0)
        def _():
            dws_ref[...] = jnp.zeros_like(dws_ref)
            dbs_ref[...] = jnp.zeros_like(dbs_ref)
            dlng_ref[...] = jnp.zeros_like(dlng_ref)
            dlnb_ref[...] = jnp.zeros_like(dlnb_ref)
            dsink_ref[...] = jnp.zeros_like(dsink_ref)

        lane = lax.broadcasted_iota(jnp.int32, (128, 128), 1)
        rowi = lax.broadcasted_iota(jnp.int32, (128, 128), 0)
        lane1 = lax.broadcasted_iota(jnp.int32, (1, 128), 1)
        tri = rowi >= lane
        tri_t = lane >= rowi
        lane_lo = lane < 64
        v = h_ref[:, W:2 * W]
        mu = jnp.mean(v, axis=1, keepdims=True)
        vc = v - mu
        var = jnp.mean(vc * vc, axis=1, keepdims=True)
        rstd = lax.rsqrt(var + LN_EPS)
        vhat = vc * rstd
        vn = vhat * lng_ref[...] + lnb_ref[...]
        dbs_acc = jnp.zeros((128, 128), F32)
        for g in range(8):
            sl = slice(g * 128, (g + 1) * 128)
            w = jnp.where(tri, ws_ref[g], 0.0).astype(BF)
            wt = jnp.where(tri_t, wst_ref[g], 0.0).astype(BF)
            vng = vn[:, sl].astype(BF)
            m = _dot(w, vng) + bsb_ref[g]
            ag = h_ref[:, 2 * W + g * 128:2 * W + (g + 1) * 128]
            sg, dsg = _silu_grad(ag)
            u = h_ref[:, sl]
            da = dm_ref[0, :, sl]
            dmm = da * u * sg
            dh_ref[:, sl] = (da * m * sg).astype(BF)
            dh_ref[:, 2 * W + g * 128:2 * W + (g + 1) * 128] = (da * u * m * dsg).astype(BF)
            dmb = dmm.astype(BF)
            dvn_ref[:, sl] = _dot(wt, dmb)
            dws_ref[g] += jnp.where(tri, _dot_nt(dmb, vng), 0.0)
            dbs_acc = jnp.where(lane == g, jnp.sum(dmm, axis=1, keepdims=True), dbs_acc)
        dbs_ref[...] += dbs_acc
        dvn = dvn_ref[...]
        dlng_ref[...] += jnp.sum(dvn * vhat, axis=0, keepdims=True)
        dlnb_ref[...] += jnp.sum(dvn, axis=0, keepdims=True)
        dyg = dvn * lng_ref[...]
        m1 = jnp.mean(dyg, axis=1, keepdims=True)
        m2 = jnp.mean(dyg * vhat, axis=1, keepdims=True)
        dh_ref[:, W:2 * W] = (rstd * (dyg - m1 - vhat * m2)).astype(BF)
        kcur = _rope_fwd(h_ref[:, 4096:4224], rc_ref)
        kb = jnp.concatenate([_rope_fwd(hp_ref[:, 0:128], rp_ref), kcur], axis=0)
        vb = jnp.concatenate([hp_ref[:, 128:256], h_ref[:, 4224:4352]], axis=0)
        k2 = _dup_heads(kb)
        v2 = _dup_heads(vb)
        kc2 = _dup_heads(kcur)
        vc2 = _dup_heads(h_ref[:, 4224:4352])
        qi = lax.broadcasted_iota(jnp.int32, (128, 256), 0)
        kj = lax.broadcasted_iota(jnp.int32, (128, 256), 1)
        diff = qi + 128 - kj
        valid = (diff >= 0) & (diff < 128) & ((n > 0) | (kj >= 128))
        validn = (lane > rowi) & (n < nb - 1)
        lc = l_ref[...]
        lnx = ln_ref[...]
        dk = [jnp.zeros((128, 128), F32), jnp.zeros((128, 128), F32)]
        dv = [jnp.zeros((128, 128), F32), jnp.zeros((128, 128), F32)]
        dsk_acc = jnp.zeros((1, 128), F32)
        for j in range(8):
            hk = j // 4
            cs = slice(j * 128, (j + 1) * 128)
            qc = _rope_fwd(h_ref[:, 3072 + j * 128:3072 + (j + 1) * 128], rc_ref)
            qn = _rope_fwd(hn_ref[:, 3072 + j * 128:3072 + (j + 1) * 128], rn_ref)
            bg = h_ref[:, 4352 + j * 128:4352 + (j + 1) * 128]
            sgb, dsgb = _silu_grad(bg)
            db = dm_ref[1, :, cs]
            oc = o_ref[:, cs]
            do = db * sgb
            dh_ref[:, 4352 + j * 128:4352 + (j + 1) * 128] = (db * oc * dsgb).astype(BF)
            bgn = hn_ref[:, 4352 + j * 128:4352 + (j + 1) * 128]
            don = dmn_ref[1, :, cs] * (bgn * _sig(bgn))
            prod = do * oc
            prodn = don * on_ref[:, cs]
            dqcol = jnp.zeros((128, 128), F32)
            for half in range(2):
                hq = 2 * j + half
                hm = lane_lo if half == 0 else jnp.logical_not(lane_lo)
                dsum = jnp.sum(jnp.where(hm, prod, 0.0), axis=1, keepdims=True)
                dsumn = jnp.sum(jnp.where(hm, prodn, 0.0), axis=1, keepdims=True)
                lh = jnp.sum(jnp.where(lane == hq, lc, 0.0), axis=1, keepdims=True)
                lhn = jnp.sum(jnp.where(lane == hq, lnx, 0.0), axis=1, keepdims=True)
                qm = jnp.where(hm, qc, 0.0).astype(BF)
                dom = jnp.where(hm, do, 0.0).astype(BF)
                s = _dot_nt(qm, k2[hk]) * 0.125
                p = jnp.exp(jnp.where(valid, s - lh, NEG))
                ds = p * (_dot_nt(dom, v2[hk]) - dsum)
                dqcol = jnp.where(hm, _dot(ds.astype(BF), k2[hk]) * 0.125, dqcol)
                psink = jnp.exp(sink_ref[hq] - lh)
                dsk = -jnp.sum(psink * dsum, axis=0, keepdims=True)
                dsk_acc = jnp.where(lane1 == hq, dsk, dsk_acc)
                dv[hk] = dv[hk] + _dot(jnp.transpose(p[:, 128:256]).astype(BF), dom)
                dk[hk] = dk[hk] + _dot(jnp.transpose(ds[:, 128:256]).astype(BF), qm) * 0.125
                qnm = jnp.where(hm, qn, 0.0).astype(BF)
                donm = jnp.where(hm, don, 0.0).astype(BF)
                sn = _dot_nt(qnm, kc2[hk]) * 0.125
                pn = jnp.exp(jnp.where(validn, sn - lhn, NEG))
                dsn = pn * (_dot_nt(donm, vc2[hk]) - dsumn)
                dv[hk] = dv[hk] + _dot(jnp.transpose(pn).astype(BF), donm)
                dk[hk] = dk[hk] + _dot(jnp.transpose(dsn).astype(BF), qnm) * 0.125
            dh_ref[:, 3072 + j * 128:3072 + (j + 1) * 128] = _rope_bwd(dqcol, rc_ref).astype(BF)
        fold = lambda a: a + pltpu.roll(a, 64, 1)
        dh_ref[:, 4096:4224] = _rope_bwd(jnp.where(lane_lo, fold(dk[0]), fold(dk[1])), rc_ref).astype(BF)
        dh_ref[:, 4224:4352] = jnp.where(lane_lo, fold(dv[0]), fold(dv[1])).astype(BF)
        dsink_ref[...] += dsk_acc

    prev = lambda n: jnp.maximum(n - 1, 0)
    nxt = lambda n: jnp.minimum(n + 1, nb - 1)
    full = lambda shape: pl.BlockSpec(shape, lambda n: (0,) * len(shape))
    return _pcall(
        body, grid=(nb,),
        in_specs=[pl.BlockSpec((CHUNK, EVEN_IN), lambda n: (n, 0)),
                  pl.BlockSpec((CHUNK, 256), lambda n: (prev(n), 16)),
                  pl.BlockSpec((CHUNK, EVEN_IN), lambda n: (nxt(n), 0)),
                  pl.BlockSpec((2, CHUNK, W), lambda n: (0, n, 0)),
                  pl.BlockSpec((2, CHUNK, W), lambda n: (0, nxt(n), 0)),
                  pl.BlockSpec((CHUNK, W), lambda n: (n, 0)),
                  pl.BlockSpec((CHUNK, W), lambda n: (nxt(n), 0)),
                  pl.BlockSpec((CHUNK, 128), lambda n: (n, 0)),
                  pl.BlockSpec((CHUNK, 128), lambda n: (nxt(n), 0)),
                  pl.BlockSpec((CHUNK, 384), lambda n: (n, 0)),
                  pl.BlockSpec((CHUNK, 384), lambda n: (prev(n), 0)),
                  pl.BlockSpec((CHUNK, 384), lambda n: (nxt(n), 0)),
                  full((1, W)), full((1, W)), full((8, 128, 128)), full((8, 128, 128)), full((8, 128, 128)),
                  pl.BlockSpec(memory_space=pltpu.SMEM)],
        out_specs=[pl.BlockSpec((CHUNK, EVEN_IN), lambda n: (n, 0)),
                   full((8, 128, 128)), full((128, 128)), full((1, W)), full((1, W)), full((1, 128))],
        out_shape=[jax.ShapeDtypeStruct((S, EVEN_IN), BF), jax.ShapeDtypeStruct((8, 128, 128), F32),
                   jax.ShapeDtypeStruct((128, 128), F32), jax.ShapeDtypeStruct((1, W), F32),
                   jax.ShapeDtypeStruct((1, W), F32), jax.ShapeDtypeStruct((1, 128), F32)],
        scratch=[pltpu.VMEM((CHUNK, W), F32)], name=name, comm=comm,
    )(h, h, h, dmix3, dmix3, o, o, l, l, rope, rope, rope, lng.reshape(1, W), lnb.reshape(1, W), ws, wst, bsb, sinks)


def _expm1(x):
    ser = x * (1.0 + x * (0.5 + x * (1.0 / 6.0 + x * (1.0 / 24.0))))
    return jnp.where(jnp.abs(x) < 1e-2, ser, jnp.exp(x) - 1.0)


def _softplus_neg(lam):
    z = -lam
    e = jnp.exp(-jnp.abs(z))
    l1p = jnp.where(e < 1e-3, e * (1.0 - e * (0.5 - e * (1.0 / 3.0))), jnp.log(1.0 + e))
    return jnp.maximum(z, 0.0) + l1p


def _shift_down(x, k, row, fill=0.0):
    return jnp.where(row >= k, pltpu.roll(x, k, 0), fill)


def _shift_up(x, k, row, fill=0.0):
    S = x.shape[0]
    return jnp.where(row < S - k, pltpu.roll(x, S - k, 0), fill)


def _lru_gates(xc, row, cw_ref, cb_ref, wa_ref, wx_ref, ba_ref, bx_ref, lam_ref):
    xconv = (cw_ref[3:4, :] * xc + cw_ref[2:3, :] * _shift_down(xc, 1, row) + cw_ref[1:2, :] * _shift_down(xc, 2, row)
             + cw_ref[0:1, :] * _shift_down(xc, 3, row) + cb_ref[...])
    xb = xconv.astype(BF)
    r = _sig(_dot(xb, wa_ref[...]) + ba_ref[...])
    i = _sig(_dot(xb, wx_ref[...]) + bx_ref[...])
    sp = _softplus_neg(lam_ref[...])
    log_a = -LRU_C * r * sp
    a = jnp.exp(log_a)
    mult = jnp.sqrt(-_expm1(2.0 * log_a))
    return xconv, r, i, sp, a, mult


def _odd_c_fwd(h, cw, cb, wa, wx, ba, bx, lam, name, comm=None):
    S = h.shape[0]

    def body(xc_ref, cg_ref, cw_ref, cb_ref, wa_ref, wx_ref, ba_ref, bx_ref, lam_ref, mix_ref, hst_ref):
        row = lax.broadcasted_iota(jnp.int32, (S, 128), 0)
        xconv, r, i, sp, a, mult = _lru_gates(xc_ref[...], row, cw_ref, cb_ref, wa_ref, wx_ref, ba_ref, bx_ref, lam_ref)
        aa = a
        bb = mult * (i * xconv)
        k = 1
        while k < S:
            bb = aa * _shift_down(bb, k, row) + bb
            if 2 * k < S:
                aa = aa * _shift_down(aa, k, row, 1.0)
            k *= 2
        hst_ref[...] = bb
        cg = cg_ref[...]
        mix_ref[...] = (bb * (cg * _sig(cg))).astype(BF)

    col = lambda off: pl.BlockSpec((S, 128), lambda j: (0, off + j))
    vec = pl.BlockSpec((1, 128), lambda j: (0, j))
    mat = pl.BlockSpec((None, 128, 128), lambda j: (j, 0, 0))
    return _pcall(
        body, grid=(8,),
        in_specs=[col(0), col(8), pl.BlockSpec((4, 128), lambda j: (0, j)), vec, mat, mat, vec, vec, vec],
        out_specs=[pl.BlockSpec((None, S, 128), lambda j: (0, 0, j)), pl.BlockSpec((S, 128), lambda j: (0, j))],
        out_shape=[jax.ShapeDtypeStruct((2, S, W), BF), jax.ShapeDtypeStruct((S, W), F32)],
        name=name, comm=comm,
    )(h, h, cw, cb.reshape(1, W), wa, wx, ba.reshape(1, W), bx.reshape(1, W), lam.reshape(1, W))


def _pool_sums(x, g, row, shift):
    s2 = x + shift(x, 1, row)
    s4 = s2 + shift(s2, 2, row)
    s8 = s4 + shift(s4, 4, row)
    s16 = s8 + shift(s8, 8, row)
    return jnp.where(g == 0, s2, jnp.where(g == 1, s4, jnp.where(g == 2, s8, s16)))


def _odd_d_fwd(h, mix3, wp, dscale, name):
    S = h.shape[0]

    def body(xd_ref, dg_ref, wp_ref, ds_ref, mix_in, mix_ref):
        g = pl.program_id(0)
        row = lax.broadcasted_iota(jnp.int32, (S, 256), 0)
        xd = xd_ref[...]
        cnt = jnp.minimum(row + 1, jnp.left_shift(2, g)).astype(F32)
        pooled = _pool_sums(xd, g, row, _shift_down) / cnt - xd
        mixed = _dot(pooled.astype(BF), wp_ref[...])
        dg = dg_ref[...]
        mix_ref[...] = (mixed * ds_ref[...] * (dg * _sig(dg))).astype(BF)

    col = lambda off: pl.BlockSpec((S, 256), lambda g: (0, off + g))
    return pl.pallas_call(
        body, grid=(4,),
        in_specs=[col(8), col(12), pl.BlockSpec((None, 256, 256), lambda g: (g, 0, 0)),
                  pl.BlockSpec((1, 256), lambda g: (0, g)), ANY],
        out_specs=pl.BlockSpec((None, S, 256), lambda g: (1, 0, g)),
        out_shape=jax.ShapeDtypeStruct((2, S, W), BF), input_output_aliases={4: 0},
        name=name, compiler_params=_cp(),
    )(h, h, wp, dscale.reshape(1, W), mix3)


def _odd_c_bwd(h, hst, dmix3, cw, cb, wa, wx, wat, wxt, ba, bx, lam, name, comm=None):
    S = h.shape[0]

    def body(xc_ref, cg_ref, hst_ref, dc_ref, cw_ref, cb_ref, wa_ref, wx_ref, wat_ref, wxt_ref, ba_ref, bx_ref, lam_ref,
             dh_ref, dcw_ref, dcb_ref, dwa_ref, dwx_ref, dba_ref, dbx_ref, dlam_ref):
        row = lax.broadcasted_iota(jnp.int32, (S, 128), 0)
        xc = xc_ref[...]
        xconv, r, i, sp, a, mult = _lru_gates(xc, row, cw_ref, cb_ref, wa_ref, wx_ref, ba_ref, bx_ref, lam_ref)
        hst = hst_ref[...]
        cg = cg_ref[...]
        sg, dsg = _silu_grad(cg)
        dc = dc_ref[...]
        dh_ref[1] = (dc * hst * dsg).astype(BF)
        aa = _shift_up(a, 1, row)
        bb = dc * sg
        k = 1
        while k < S:
            bb = aa * _shift_up(bb, k, row) + bb
            if 2 * k < S:
                aa = aa * _shift_up(aa, k, row, 1.0)
            k *= 2
        lam_t = bb
        da = lam_t * _shift_down(hst, 1, row)
        ix = i * xconv
        dmult = lam_t * ix
        di = lam_t * mult * xconv
        dxconv = lam_t * mult * i
        dlog_a = da * a - dmult * (a * a / mult)
        dr = dlog_a * (-LRU_C * sp)
        dsp = jnp.sum(dlog_a * (-LRU_C * r), axis=0, keepdims=True)
        dlam_ref[...] = dsp * (-_sig(-lam_ref[...]))
        dpa = dr * r * (1.0 - r)
        dpx = di * i * (1.0 - i)
        dpab = dpa.astype(BF)
        dpxb = dpx.astype(BF)
        xb = xconv.astype(BF)
        dxconv = dxconv + _dot(dpab, wat_ref[...]) + _dot(dpxb, wxt_ref[...])
        dwa_ref[...] = _dot_tn(xb, dpab)
        dwx_ref[...] = _dot_tn(xb, dpxb)
        dba_ref[...] = jnp.sum(dpa, axis=0, keepdims=True)
        dbx_ref[...] = jnp.sum(dpx, axis=0, keepdims=True)
        dh_ref[0] = (cw_ref[3:4, :] * dxconv + cw_ref[2:3, :] * _shift_up(dxconv, 1, row)
                     + cw_ref[1:2, :] * _shift_up(dxconv, 2, row) + cw_ref[0:1, :] * _shift_up(dxconv, 3, row)).astype(BF)
        for j in range(4):
            src = xc if j == 3 else _shift_down(xc, 3 - j, row)
            dcw_ref[j:j + 1, :] = jnp.sum(dxconv * src, axis=0, keepdims=True)
        dcb_ref[...] = jnp.sum(dxconv, axis=0, keepdims=True)

    col = lambda off: pl.BlockSpec((S, 128), lambda j: (0, off + j))
    vec = pl.BlockSpec((1, 128), lambda j: (0, j))
    mat = pl.BlockSpec((None, 128, 128), lambda j: (j, 0, 0))
    vshape = jax.ShapeDtypeStruct((1, W), F32)
    mshape = jax.ShapeDtypeStruct((8, 128, 128), F32)
    return _pcall(
        body, grid=(8,),
        in_specs=[col(0), col(8), col(0), pl.BlockSpec((None, S, 128), lambda j: (0, 0, j)),
                  pl.BlockSpec((4, 128), lambda j: (0, j)), vec, mat, mat, mat, mat, vec, vec, vec],
        out_specs=[pl.BlockSpec((2, S, 128), lambda j: (0, 0, j)), pl.BlockSpec((4, 128), lambda j: (0, j)), vec,
                   mat, mat, vec, vec, vec],
        out_shape=[jax.ShapeDtypeStruct((4, S, W), BF), jax.ShapeDtypeStruct((4, W), F32), vshape, mshape, mshape,
                   vshape, vshape, vshape],
        name=name, vmem=56, comm=comm,
    )(h, h, hst, dmix3, cw, cb.reshape(1, W), wa, wx, wat, wxt, ba.reshape(1, W), bx.reshape(1, W), lam.reshape(1, W))


def _odd_d_bwd(h, dmix3, dh4, wp, wpt, dscale, name):
    S = h.shape[0]

    def body(xd_ref, dg_ref, dd_ref, wp_ref, wpt_ref, ds_ref, dh_in, dh_ref, dwp_ref, dds_ref):
        g = pl.program_id(0)
        row = lax.broadcasted_iota(jnp.int32, (S, 256), 0)
        xd = xd_ref[...]
        cnt = jnp.minimum(row + 1, jnp.left_shift(2, g)).astype(F32)
        pooled = _pool_sums(xd, g, row, _shift_down) / cnt - xd
        pb = pooled.astype(BF)
        mixed = _dot(pb, wp_ref[...])
        dg = dg_ref[...]
        sg, dsg = _silu_grad(dg)
        dd = dd_ref[...]
        dmixed = dd * ds_ref[...] * sg
        dds_ref[...] = jnp.sum(dd * mixed * sg, axis=0, keepdims=True)
        dh_ref[1] = (dd * mixed * ds_ref[...] * dsg).astype(BF)
        dmb = dmixed.astype(BF)
        dpooled = _dot(dmb, wpt_ref[...])
        dwp_ref[...] = _dot_tn(pb, dmb)
        dh_ref[0] = (_pool_sums(dpooled / cnt, g, row, _shift_up) - dpooled).astype(BF)

    col = lambda off: pl.BlockSpec((S, 256), lambda g: (0, off + g))
    mat = pl.BlockSpec((None, 256, 256), lambda g: (g, 0, 0))
    vec = pl.BlockSpec((1, 256), lambda g: (0, g))
    return pl.pallas_call(
        body, grid=(4,),
        in_specs=[col(8), col(12), pl.BlockSpec((None, S, 256), lambda g: (1, 0, g)), mat, mat, vec, ANY],
        out_specs=[pl.BlockSpec((2, S, 256), lambda g: (1, 0, g)), mat, vec],
        out_shape=[jax.ShapeDtypeStruct((4, S, W), BF), jax.ShapeDtypeStruct((4, 256, 256), F32),
                   jax.ShapeDtypeStruct((1, W), F32)],
        input_output_aliases={6: 0}, name=name, compiler_params=_cp(56),
    )(h, h, dmix3, wp, wpt, dscale.reshape(1, W), dh4)


def _peer(d):
    x, y, c = lax.axis_index("x"), lax.axis_index("y"), lax.axis_index("c")
    px = 1 - x if d & 4 else x
    py = 1 - y if d & 2 else y
    pc = 1 - c if d & 1 else c
    return (px, py, pc), 4 * px + 2 * py + pc


def _all_gather(xs, name):
    R, C = xs.shape

    def body(x_ref, out_ref, send_sems, recv_sems, local_sem):
        _, me = _peer(0)
        mine = pltpu.make_async_copy(x_ref, out_ref.at[me], local_sem)
        mine.start()
        copies = []
        for d in range(1, N_DEV):
            peer, _ = _peer(d)
            cp = pltpu.make_async_remote_copy(src_ref=x_ref, dst_ref=out_ref.at[me], send_sem=send_sems.at[d - 1],
                                              recv_sem=recv_sems.at[d - 1], device_id=peer, device_id_type=MESH)
            cp.start()
            copies.append(cp)
        for cp in copies:
            cp.wait()
        mine.wait()

    return pl.pallas_call(
        body, in_specs=[ANY], out_specs=ANY, out_shape=jax.ShapeDtypeStruct((N_DEV, R, C), xs.dtype),
        scratch_shapes=[pltpu.SemaphoreType.DMA((N_DEV - 1,)), pltpu.SemaphoreType.DMA((N_DEV - 1,)),
                        pltpu.SemaphoreType.DMA],
        name=name,
    )(xs)


def _exchange(g8, name):
    _, R, C = g8.shape

    def body(g_ref, out_ref, send_sems, recv_sems, local_sem):
        _, me = _peer(0)
        mine = pltpu.make_async_copy(g_ref.at[me], out_ref.at[0], local_sem)
        mine.start()
        copies = []
        for d in range(1, N_DEV):
            peer, pidx = _peer(d)
            cp = pltpu.make_async_remote_copy(src_ref=g_ref.at[pidx], dst_ref=out_ref.at[d], send_sem=send_sems.at[d - 1],
                                              recv_sem=recv_sems.at[d - 1], device_id=peer, device_id_type=MESH)
            cp.start()
            copies.append(cp)
        for cp in copies:
            cp.wait()
        mine.wait()

    return pl.pallas_call(
        body, in_specs=[ANY], out_specs=ANY, out_shape=jax.ShapeDtypeStruct((N_DEV, R, C), g8.dtype),
        scratch_shapes=[pltpu.SemaphoreType.DMA((N_DEV - 1,)), pltpu.SemaphoreType.DMA((N_DEV - 1,)),
                        pltpu.SemaphoreType.DMA],
        name=name,
    )(g8)


def _sum8(r8, tr, name):
    _, R, C = r8.shape
    tr = min(tr, R)
    assert R % tr == 0

    def body(r_ref, o_ref):
        acc = r_ref[0]
        for d in range(1, N_DEV):
            acc = acc + r_ref[d]
        o_ref[...] = acc

    return pl.pallas_call(
        body, grid=(R // tr,), in_specs=[pl.BlockSpec((N_DEV, tr, C), lambda i: (0, i, 0))],
        out_specs=pl.BlockSpec((tr, C), lambda i: (i, 0)), out_shape=jax.ShapeDtypeStruct((R, C), F32),
        name=name, compiler_params=_cp(),
    )(r8)


def _adamw(w, g, m, v, tr, name):
    R, C = w.shape
    tr = min(tr, R)

    def body(w_ref, g_ref, m_ref, v_ref, d_ref, m2_ref, v2_ref):
        gg = g_ref[...]
        m2 = B1 * m_ref[...] + (1.0 - B1) * gg
        v2 = B2 * v_ref[...] + (1.0 - B2) * (gg * gg)
        m_hat = m2 / (1.0 - B1 ** STEP)
        v_hat = v2 / (1.0 - B2 ** STEP)
        d_ref[...] = -LR * (m_hat / (jnp.sqrt(v_hat) + ADAM_EPS) + WD * w_ref[...])
        m2_ref[...] = m2
        v2_ref[...] = v2

    blk = pl.BlockSpec((tr, C), lambda i: (i, 0))
    shp = jax.ShapeDtypeStruct((R, C), F32)
    return pl.pallas_call(
        body, grid=(R // tr,), in_specs=[blk] * 4, out_specs=[blk] * 3, out_shape=[shp] * 3,
        name=name, compiler_params=_cp(),
    )(w, g, m, v)


def _rep_pack(a):
    n = a.size
    pad = (-n) % 1024
    f = a.reshape(-1)
    if pad:
        f = jnp.concatenate([f, jnp.zeros((pad,), a.dtype)])
    return f.reshape(N_DEV, -1, 128)


def _rep_unpack(p, shape):
    n = 1
    for s in shape:
        n *= s
    return p.reshape(-1)[:n].reshape(shape)


def _sh_pack(a, axis):
    shp = a.shape
    a = a.reshape(shp[:axis] + (N_DEV, shp[axis] // N_DEV) + shp[axis + 1:])
    return jnp.moveaxis(a, axis, 0).reshape(N_DEV, -1, 128)


def _sh_unpack(p, shape, axis):
    a = p.reshape((N_DEV,) + shape[:axis] + (shape[axis] // N_DEV,) + shape[axis + 1:])
    return jnp.moveaxis(a, 0, axis).reshape(shape)


def _pad_rows(a, mult=8):
    pad = (-a.shape[-2]) % mult
    if pad:
        a = jnp.concatenate([a, jnp.zeros(a.shape[:-2] + (pad, a.shape[-1]), a.dtype)], axis=-2)
    return a


REP = ["even_a_ln_g", "even_a_ln_b", "even_a_ws", "even_a_bs", "even_b_sinks", "even_ln_g", "even_ln_b",
       "odd_w_a", "odd_w_x"]
SH = [("odd_conv_w", (2, 4, W), 2), ("odd_conv_b", (2, W), 1), ("odd_b_a", (2, W), 1), ("odd_b_x", (2, W), 1),
      ("odd_lam", (2, W), 1), ("odd_w_pool", (2, 4, 256, 256), 2), ("odd_d_scale", (2, W), 1),
      ("odd_ln_g", (2, D), 1), ("odd_ln_b", (2, D), 1)]
BIG = ["even_w_in", "even_w_out", "odd_w_in", "odd_w_out"]
NAMES = ["even_w_in", "even_a_ln_g", "even_a_ln_b", "even_a_ws", "even_a_bs", "even_b_sinks", "even_w_out",
         "even_ln_g", "even_ln_b", "odd_w_in", "odd_conv_w", "odd_conv_b", "odd_w_a", "odd_b_a", "odd_w_x", "odd_b_x",
         "odd_lam", "odd_w_pool", "odd_d_scale", "odd_w_out", "odd_ln_g", "odd_ln_b"]


def _rope_table(positions):
    S = positions.shape[0]
    inv = ROPE_THETA ** (-jnp.arange(0, 16, 2, dtype=F32) / 16)
    ang = positions.astype(F32)[:, None] * inv
    cos, sin = jnp.cos(ang), jnp.sin(ang)
    one, zero = jnp.ones((S, 48), F32), jnp.zeros((S, 48), F32)
    z8 = jnp.zeros((S, 8), F32)
    c64 = jnp.concatenate([cos, cos, one], axis=1)
    s1 = jnp.concatenate([-sin, z8, zero], axis=1)
    s2 = jnp.concatenate([z8, sin, zero], axis=1)
    return jnp.concatenate([c64, c64, s1, s1, s2, s2], axis=1)


def kernel(x, positions, even_w_in, even_a_ln_g, even_a_ln_b, even_a_ws, even_a_bs, even_b_sinks, even_w_out, even_ln_g, even_ln_b, odd_w_in, odd_conv_w, odd_conv_b, odd_w_a, odd_b_a, odd_w_x, odd_b_x, odd_lam, odd_w_pool, odd_d_scale, odd_w_out, odd_ln_g, odd_ln_b, loss_target, m_even_w_in, m_even_a_ln_g, m_even_a_ln_b, m_even_a_ws, m_even_a_bs, m_even_b_sinks, m_even_w_out, m_even_ln_g, m_even_ln_b, m_odd_w_in, m_odd_conv_w, m_odd_conv_b, m_odd_w_a, m_odd_b_a, m_odd_w_x, m_odd_b_x, m_odd_lam, m_odd_w_pool, m_odd_d_scale, m_odd_w_out, m_odd_ln_g, m_odd_ln_b, v_even_w_in, v_even_a_ln_g, v_even_a_ln_b, v_even_a_ws, v_even_a_bs, v_even_b_sinks, v_even_w_out, v_even_ln_g, v_even_ln_b, v_odd_w_in, v_odd_conv_w, v_odd_conv_b, v_odd_w_a, v_odd_b_a, v_odd_w_x, v_odd_b_x, v_odd_lam, v_odd_w_pool, v_odd_d_scale, v_odd_w_out, v_odd_ln_g, v_odd_ln_b):
    args = (even_w_in, even_a_ln_g, even_a_ln_b, even_a_ws, even_a_bs, even_b_sinks, even_w_out, even_ln_g, even_ln_b,
            odd_w_in, odd_conv_w, odd_conv_b, odd_w_a, odd_b_a, odd_w_x, odd_b_x, odd_lam, odd_w_pool, odd_d_scale,
            odd_w_out, odd_ln_g, odd_ln_b)
    margs = (m_even_w_in, m_even_a_ln_g, m_even_a_ln_b, m_even_a_ws, m_even_a_bs, m_even_b_sinks, m_even_w_out,
             m_even_ln_g, m_even_ln_b, m_odd_w_in, m_odd_conv_w, m_odd_conv_b, m_odd_w_a, m_odd_b_a, m_odd_w_x,
             m_odd_b_x, m_odd_lam, m_odd_w_pool, m_odd_d_scale, m_odd_w_out, m_odd_ln_g, m_odd_ln_b)
    vargs = (v_even_w_in, v_even_a_ln_g, v_even_a_ln_b, v_even_a_ws, v_even_a_bs, v_even_b_sinks, v_even_w_out,
             v_even_ln_g, v_even_ln_b, v_odd_w_in, v_odd_conv_w, v_odd_conv_b, v_odd_w_a, v_odd_b_a, v_odd_w_x,
             v_odd_b_x, v_odd_lam, v_odd_w_pool, v_odd_d_scale, v_odd_w_out, v_odd_ln_g, v_odd_ln_b)
    wts = dict(zip(NAMES, args))
    mom = dict(zip(NAMES, margs))
    var = dict(zip(NAMES, vargs))
    S = x.shape[1]
    x0 = x[0]
    rope = _rope_table(positions[0])

    kinds = ("even", "odd", "even", "odd")
    blk_in = [jnp.transpose(wts[kinds[l] + "_w_in"][l // 2]).astype(BF) for l in range(4)]
    blk_out = [wts[kinds[l] + "_w_out"][l // 2].astype(BF) for l in range(4)]
    carry = {(0, "mm_h"): ("out", 0), (0, "branch"): ("in", 1), (0, "mm_out"): ("out", 1), (1, "mm_h"): ("in", 2),
             (1, "branch"): ("out", 2), (2, "mm_h"): ("in", 3), (2, "branch"): ("out", 3)}
    wt_in, w_out = [None] * 4, [None] * 4

    def fetch(layer, kernel_name):
        if (layer, kernel_name) not in carry:
            return None
        what, l = carry[layer, kernel_name]
        return _AgIci([blk_in[l] if what == "in" else blk_out[l]])

    def landed(layer, kernel_name, bufs):
        if (layer, kernel_name) not in carry:
            return
        what, l = carry[layer, kernel_name]
        (buf,) = _ag_d2d(list(bufs), f"ag_d2d_{what}_{kinds[l]}")
        if what == "in":
            wt_in[l] = buf.reshape(-1, D)
        else:
            w_out[l] = buf.reshape(D, D)

    sh_local = _pad_rows(jnp.concatenate([wts[nm].reshape(-1, 128) for nm, _, _ in SH], axis=0))
    first = _comm_only(_AgIci([blk_in[0], sh_local]), "ag_first")
    first = _ag_d2d(list(first), "ag_d2d_first")
    wt_in[0] = first[0].reshape(-1, D)
    sh_all = first[1]
    full = {}
    off = 0
    for nm, shape, axis in SH:
        r = wts[nm].size // 128
        full[nm] = _sh_unpack(sh_all[:, off:off + r, :], shape, axis)
        off += r
    for nm in REP:
        full[nm] = wts[nm]

    saved = []
    xf, xb = x0, x0.astype(BF)
    for layer in range(4):
        j = layer // 2
        kind = kinds[layer]
        h, got = _mm_nt(xb, wt_in[layer], 1024, 768 if kind == "even" else 512, "mm_h_" + kind,
                        comm=fetch(layer, "mm_h"))
        landed(layer, "mm_h", got)
        if kind == "even":
            bsb = jnp.broadcast_to(full["even_a_bs"][j][:, :, None], (8, 128, 128))
            (mix3, o, l), got = _even_fwd(h, rope, full["even_a_ln_g"][j], full["even_a_ln_b"][j], full["even_a_ws"][j],
                                          bsb, full["even_b_sinks"][j], "even_fwd", comm=fetch(layer, "branch"))
            extra = (o, l, bsb)
        else:
            wa, wx = full["odd_w_a"][j].astype(BF), full["odd_w_x"][j].astype(BF)
            wp = full["odd_w_pool"][j].astype(BF)
            (mix3, hst), got = _odd_c_fwd(h, full["odd_conv_w"][j], full["odd_conv_b"][j], wa, wx, full["odd_b_a"][j],
                                          full["odd_b_x"][j], full["odd_lam"][j], "odd_c_fwd", comm=fetch(layer, "branch"))
            mix3 = _odd_d_fwd(h, mix3, wp, full["odd_d_scale"][j], "odd_d_fwd")
            extra = (hst, wa, wx, wp)
        landed(layer, "branch", got)
        (z, xn, xnb), got = _mm_out_ln(mix3, w_out[layer], xf, full[kind + "_ln_g"][j], full[kind + "_ln_b"][j],
                                       "mm_out_ln", comm=fetch(layer, "mm_out"))
        landed(layer, "mm_out", got)
        saved.append((xb, h, mix3, z, extra))
        xf, xb = xn, xnb

    dxn, part = _loss_grad(xf, loss_target[0])
    loss = lax.psum(part[0, 0] * (0.5 / D), ("x", "y", "c"))

    gsum = {nm: [None, None] for nm in NAMES}
    gred = {}

    def chip_sum(g, tag):
        own, got = _rs_d2d([g.reshape(N_DEV, g.shape[0] // N_DEV, D)], "rs_d2d_" + tag)
        return _add_pairs(own[0], got[0], "rs_add_" + tag)

    for layer in (3, 2, 1, 0):
        j = layer // 2
        xb, h, mix3, z, extra = saved[layer]
        kind = kinds[layer]
        dz, dzb, dg, dbeta = _ln_bwd(dxn, z, full[kind + "_ln_g"][j], "ln_bwd")
        gsum[kind + "_ln_g"][j] = dg.reshape(D)
        gsum[kind + "_ln_b"][j] = dbeta.reshape(D)
        dmix3, _ = _mm_nt(dzb, w_out[layer], 1024, 512, "mm_dmix", out3=True)
        s_out = chip_sum(_mm_tn(mix3, dzb, 512, "mm_dw_out"), "w_out")
        if kind == "even":
            o, l, bsb = extra
            ws = full["even_a_ws"][j]
            (dh, dws, dbs, dlng, dlnb, dsink), (r_out,) = _even_bwd(
                h, dmix3, o, l, rope, full["even_a_ln_g"][j], full["even_a_ln_b"][j], ws, jnp.swapaxes(ws, 1, 2), bsb,
                full["even_b_sinks"][j], "even_bwd", comm=_RsIci([s_out]))
            gsum["even_a_ws"][j] = dws
            gsum["even_a_bs"][j] = jnp.transpose(dbs[:, :8])
            gsum["even_a_ln_g"][j] = dlng.reshape(W)
            gsum["even_a_ln_b"][j] = dlnb.reshape(W)
            gsum["even_b_sinks"][j] = dsink[0, :16]
            s_in = chip_sum(_mm_tn(dh, xb, 384, "mm_dw_in_even"), "w_in_even")
            dxn, (r_in,) = _mm_nn_res(dh, wt_in[layer], dz, 512, 768, "mm_dx_even", comm=_RsIci([s_in]))
        else:
            hst, wa, wx, wp = extra
            (dh4, dcw, dcb, dwa, dwx, dba, dbx, dlam), (r_out,) = _odd_c_bwd(
                h, hst, dmix3, full["odd_conv_w"][j], full["odd_conv_b"][j], wa, wx, jnp.swapaxes(wa, 1, 2),
                jnp.swapaxes(wx, 1, 2), full["odd_b_a"][j], full["odd_b_x"][j], full["odd_lam"][j], "odd_c_bwd",
                comm=_RsIci([s_out]))
            dh4, dwp, dds = _odd_d_bwd(h, dmix3, dh4, wp, jnp.swapaxes(wp, 1, 2), full["odd_d_scale"][j], "odd_d_bwd")
            gsum["odd_conv_w"][j], gsum["odd_conv_b"][j] = dcw, dcb.reshape(W)
            gsum["odd_w_a"][j], gsum["odd_w_x"][j] = dwa, dwx
            gsum["odd_b_a"][j], gsum["odd_b_x"][j], gsum["odd_lam"][j] = dba.reshape(W), dbx.reshape(W), dlam.reshape(W)
            gsum["odd_w_pool"][j], gsum["odd_d_scale"][j] = dwp, dds.reshape(W)
            s_in = chip_sum(_mm_tn(dh4, xb, 512, "mm_dw_in_odd"), "w_in_odd")
            dxn, (r_in,) = _mm_nn_res(dh4, wt_in[layer], dz, 512, 512, "mm_dx_odd", comm=_RsIci([s_in]))
        gred[kind + "_w_out", j] = _rs_final(s_out, r_out, "rs_final_w_out")
        gred[kind + "_w_in", j] = jnp.transpose(_rs_final(s_in, r_in, "rs_final_w_in_" + kind))
    grad_x = dxn[None]

    out_g, out_d, out_m, out_v = {}, {}, {}, {}
    for nm in BIG:
        g = jnp.stack([gred[nm, 0], gred[nm, 1]])
        shp = wts[nm].shape
        d2, m2, v2 = _adamw(wts[nm].reshape(-1, shp[-1]), g.reshape(-1, shp[-1]), mom[nm].reshape(-1, shp[-1]),
                            var[nm].reshape(-1, shp[-1]), 512, f"adamw_{nm}")
        out_g[nm], out_d[nm], out_m[nm], out_v[nm] = g, d2.reshape(shp), m2.reshape(shp), v2.reshape(shp)

    rep_rows = [_rep_pack(jnp.stack(gsum[nm]).reshape(wts[nm].shape)) for nm in REP]
    sh_rows = [_sh_pack(jnp.stack(gsum[nm]).reshape(shape), axis) for nm, shape, axis in SH]
    n_rep = sum(p.shape[1] for p in rep_rows)
    packed = _pad_rows(jnp.concatenate(rep_rows + sh_rows, axis=1))
    red = _sum8(_exchange(packed, "rs_small"), 1 << 20, "sum_small")
    n_rep8 = n_rep + (-n_rep) % 8
    rep_all = _all_gather(_pad_rows(red[:n_rep]), "ag_small_grad")
    g_small = {}
    off = 0
    for nm, p in zip(REP, rep_rows):
        r = p.shape[1]
        g_small[nm] = _rep_unpack(rep_all[:, off:off + r, :], wts[nm].shape)
        off += r
    off = n_rep
    for (nm, shape, axis), p in zip(SH, sh_rows):
        r = p.shape[1]
        g_small[nm] = red[off:off + r].reshape(wts[nm].shape)
        off += r

    def rows(a):
        f = a.reshape(-1)
        pad = (-f.shape[0]) % 128
        if pad:
            f = jnp.concatenate([f, jnp.zeros((pad,), a.dtype)])
        return f.reshape(-1, 128)

    small = REP + [nm for nm, _, _ in SH]
    cat = lambda src: _pad_rows(jnp.concatenate([rows(src[nm]) for nm in small], axis=0))
    d2, m2, v2 = _adamw(cat(wts), cat(g_small), cat(mom), cat(var), 1 << 20, "adamw_small")
    off = 0
    for nm in small:
        n = wts[nm].size
        r = (n + 127) // 128
        shp = wts[nm].shape
        take = lambda a: a[off:off + r].reshape(-1)[:n].reshape(shp)
        out_g[nm], out_d[nm], out_m[nm], out_v[nm] = g_small[nm], take(d2), take(m2), take(v2)
        off += r

    return (loss, grad_x, *[out_g[nm] for nm in NAMES], *[out_d[nm] for nm in NAMES],
            *[out_m[nm] for nm in NAMES], *[out_v[nm] for nm in NAMES])
```

```python
import functools

import jax
import jax.numpy as jnp
from jax import lax
from jax.experimental import pallas as pl
from jax.experimental.pallas import tpu as pltpu

F32 = jnp.float32
BF = jnp.bfloat16
MESH = pl.DeviceIdType.MESH
ANY = pl.BlockSpec(memory_space=pl.ANY)

N_DEV = 8
D = 2048
W = 1024
EVEN_IN = 5376
ODD_IN = 4096
CHUNK = 128
ALPHA = (2 * 4) ** 0.25
LN_EPS = 1e-5
ROPE_THETA = 500000.0
LRU_C = 8.0
LR, B1, B2, ADAM_EPS, WD, STEP = 0.001, 0.9, 0.999, 1e-08, 0.01, 10
NEG = -1e30


def _cp(vmem_mb=48):
    return pltpu.CompilerParams(vmem_limit_bytes=vmem_mb * 1024 * 1024)


def _sig(x):
    return jax.nn.sigmoid(x)


def _silu_grad(x):
    s = _sig(x)
    return x * s, s * (1.0 + x * (1.0 - s))


def _dot(a, b):
    return jnp.dot(a, b, preferred_element_type=F32)


def _dot_nt(a, b):
    return lax.dot_general(a, b, (((1,), (1,)), ((), ())), preferred_element_type=F32)


def _dot_tn(a, b):
    return lax.dot_general(a, b, (((0,), (0,)), ((), ())), preferred_element_type=F32)


def _coords():
    return lax.axis_index("x"), lax.axis_index("y"), lax.axis_index("c")


def _chip(j):
    x, y, _ = _coords()
    return (1 - x if j & 2 else x), (1 - y if j & 1 else y)


class _Comm:
    def start(self, ins, outs, sems):
        for cp in self.copies(ins, outs, sems):
            cp.start()

    def wait(self, ins, outs, sems):
        for cp in self.copies(ins, outs, sems):
            cp.wait()


class _AgIci(_Comm):
    def __init__(self, blocks):
        n = len(blocks)
        self.inputs = list(blocks)
        self.out_shapes = [jax.ShapeDtypeStruct((N_DEV,) + b.shape, b.dtype) for b in blocks]
        self.sem_shapes = [pltpu.SemaphoreType.DMA((n, 3)), pltpu.SemaphoreType.DMA((n, 3))]

    def copies(self, ins, outs, sems):
        send, recv = sems
        x, y, c = _coords()
        me = 4 * x + 2 * y + c
        res = []
        for k, (src, dst) in enumerate(zip(ins, outs)):
            for j in (1, 2, 3):
                px, py = _chip(j)
                res.append(pltpu.make_async_remote_copy(
                    src_ref=src, dst_ref=dst.at[me], send_sem=send.at[k, j - 1], recv_sem=recv.at[k, j - 1],
                    device_id=(px, py, c), device_id_type=MESH))
        return res


class _RsIci(_Comm):
    def __init__(self, sums):
        n = len(sums)
        self.inputs = list(sums)
        self.out_shapes = [jax.ShapeDtypeStruct((3,) + s.shape[1:], s.dtype) for s in sums]
        self.sem_shapes = [pltpu.SemaphoreType.DMA((n, 3)), pltpu.SemaphoreType.DMA((n, 3))]

    def copies(self, ins, outs, sems):
        send, recv = sems
        _, _, c = _coords()
        res = []
        for k, (src, dst) in enumerate(zip(ins, outs)):
            for j in (1, 2, 3):
                px, py = _chip(j)
                res.append(pltpu.make_async_remote_copy(
                    src_ref=src.at[j], dst_ref=dst.at[j - 1], send_sem=send.at[k, j - 1], recv_sem=recv.at[k, j - 1],
                    device_id=(px, py, c), device_id_type=MESH))
        return res


def _pcall(body, *, grid, in_specs, out_specs, out_shape, name, scratch=(), vmem=48, comm=None):
    in_specs, out_specs, out_shape, scratch = list(in_specs), list(out_specs), list(out_shape), list(scratch)
    if comm is None:
        call = pl.pallas_call(body, grid=grid, in_specs=in_specs, out_specs=out_specs, out_shape=out_shape,
                              scratch_shapes=scratch, name=name, compiler_params=_cp(vmem))
        return lambda *args: (call(*args), [])
    n_in, n_out, n_scr = len(in_specs), len(out_specs), len(scratch)
    c_in, c_out = len(comm.inputs), len(comm.out_shapes)

    def wrapped(*refs):
        ins, cins = refs[:n_in], refs[n_in:n_in + c_in]
        o0 = n_in + c_in
        outs, couts = refs[o0:o0 + n_out], refs[o0 + n_out:o0 + n_out + c_out]
        s0 = o0 + n_out + c_out
        scr, sems = refs[s0:s0 + n_scr], refs[s0 + n_scr:]
        ids = [pl.program_id(a) for a in range(len(grid))]
        first = functools.reduce(jnp.logical_and, [i == 0 for i in ids])
        last = functools.reduce(jnp.logical_and, [i == g - 1 for i, g in zip(ids, grid)])

        @pl.when(first)
        def _():
            comm.start(cins, couts, sems)

        body(*ins, *outs, *scr)

        @pl.when(last)
        def _():
            comm.wait(cins, couts, sems)

    call = pl.pallas_call(wrapped, grid=grid, in_specs=in_specs + [ANY] * c_in, out_specs=out_specs + [ANY] * c_out,
                          out_shape=out_shape + list(comm.out_shapes), scratch_shapes=scratch + list(comm.sem_shapes),
                          name=name, compiler_params=_cp(vmem))

    def run(*args):
        res = call(*args, *comm.inputs)
        return res[:n_out], res[n_out:]

    return run


def _comm_only(comm, name):
    c_in, c_out = len(comm.inputs), len(comm.out_shapes)

    def body(*refs):
        cins, couts, sems = refs[:c_in], refs[c_in:c_in + c_out], refs[c_in + c_out:]
        comm.start(cins, couts, sems)
        comm.wait(cins, couts, sems)

    return pl.pallas_call(body, in_specs=[ANY] * c_in, out_specs=[ANY] * c_out, out_shape=list(comm.out_shapes),
                          scratch_shapes=list(comm.sem_shapes), name=name)(*comm.inputs)


def _ag_d2d(bufs, blocks, name):
    n = len(bufs)

    def body(*refs):
        mine, outs, (send, recv, loc) = refs[n:2 * n], refs[2 * n:3 * n], refs[3 * n:]
        x, y, c = _coords()
        me = 4 * x + 2 * y + c
        cps = []
        for k in range(n):
            cps.append(pltpu.make_async_copy(mine[k], outs[k].at[me], loc.at[k]))
            for j in range(4):
                px, py = _chip(j)
                blk = outs[k].at[4 * px + 2 * py + c]
                cps.append(pltpu.make_async_remote_copy(src_ref=mine[k] if j == 0 else blk, dst_ref=blk, send_sem=send.at[k, j],
                                                        recv_sem=recv.at[k, j], device_id=(x, y, 1 - c), device_id_type=MESH))
        for cp in cps:
            cp.start()
        for cp in cps:
            cp.wait()

    return pl.pallas_call(
        body, in_specs=[ANY] * n + [pl.BlockSpec(memory_space=pltpu.VMEM)] * n, out_specs=[ANY] * n,
        out_shape=[jax.ShapeDtypeStruct(b.shape, b.dtype) for b in bufs], input_output_aliases={k: k for k in range(n)},
        scratch_shapes=[pltpu.SemaphoreType.DMA((n, 4)), pltpu.SemaphoreType.DMA((n, 4)), pltpu.SemaphoreType.DMA((n,))],
        name=name, compiler_params=_cp())(*bufs, *blocks)


def _rs_d2d(parts, name):
    n = len(parts)

    def body(*refs):
        ins, got, (send, recv) = refs[:n], refs[n:2 * n], refs[2 * n:]
        x, y, c = _coords()
        cps = []
        for k in range(n):
            for j in range(4):
                px, py = _chip(j)
                cps.append(pltpu.make_async_remote_copy(
                    src_ref=ins[k].at[4 * px + 2 * py + 1 - c], dst_ref=got[k].at[j], send_sem=send.at[k, j],
                    recv_sem=recv.at[k, j], device_id=(x, y, 1 - c), device_id_type=MESH))
        for cp in cps:
            cp.start()
        for cp in cps:
            cp.wait()

    return pl.pallas_call(
        body, in_specs=[ANY] * n, out_specs=[ANY] * n,
        out_shape=[jax.ShapeDtypeStruct((4,) + p.shape[1:], p.dtype) for p in parts],
        scratch_shapes=[pltpu.SemaphoreType.DMA((n, 4)), pltpu.SemaphoreType.DMA((n, 4))], name=name)(*parts)


def _chip_blocks():
    _, _, c = _coords()
    return jnp.stack([4 * px + 2 * py + c for px, py in map(_chip, range(4))]).astype(jnp.int32)


def _add_pairs(g8, b4, name):
    _, R, C = b4.shape

    def body(idx_ref, a_ref, b_ref, o_ref):
        o_ref[...] = (a_ref[...].astype(F32) + b_ref[...].astype(F32)).astype(BF)

    blk = pl.BlockSpec((None, R, C), lambda j, idx: (j, 0, 0))
    grid_spec = pltpu.PrefetchScalarGridSpec(
        num_scalar_prefetch=1, grid=(4,),
        in_specs=[pl.BlockSpec((None, R, C), lambda j, idx: (idx[j], 0, 0)), blk], out_specs=blk)
    return pl.pallas_call(body, grid_spec=grid_spec, out_shape=jax.ShapeDtypeStruct(b4.shape, BF), name=name,
                          compiler_params=_cp())(_chip_blocks(), g8, b4)


def _rs_final(s4, r3, name):
    _, R, C = s4.shape
    tr = R // 2

    def body(s_ref, r_ref, o_ref):
        o_ref[...] = ((s_ref[...].astype(F32) + r_ref[0].astype(F32)) + r_ref[1].astype(F32)) + r_ref[2].astype(F32)

    return pl.pallas_call(
        body, grid=(2,),
        in_specs=[pl.BlockSpec((None, tr, C), lambda i: (0, i, 0)), pl.BlockSpec((3, tr, C), lambda i: (0, i, 0))],
        out_specs=pl.BlockSpec((tr, C), lambda i: (i, 0)), out_shape=jax.ShapeDtypeStruct((R, C), F32),
        name=name, compiler_params=_cp())(s4, r3)


def _mm_nt(a, w, tm, tn, name, out3=False, comm=None):
    M, K = a.shape
    N = w.shape[0]
    tm = min(tm, M)

    def body(a_ref, w_ref, o_ref):
        o_ref[...] = _dot_nt(a_ref[...], w_ref[...])

    if out3:
        per = W // tn
        out_shape = jax.ShapeDtypeStruct((N // W, M, W), F32)
        out_spec = pl.BlockSpec((None, tm, tn), lambda i, j: (j // per, i, j % per))
    else:
        out_shape = jax.ShapeDtypeStruct((M, N), F32)
        out_spec = pl.BlockSpec((tm, tn), lambda i, j: (i, j))
    (res,), extra = _pcall(
        body, grid=(M // tm, N // tn),
        in_specs=[pl.BlockSpec((tm, K), lambda i, j: (i, 0)), pl.BlockSpec((tn, K), lambda i, j: (j, 0))],
        out_specs=[out_spec], out_shape=[out_shape], name=name, comm=comm)(a, w)
    return res, extra


def _mm_tn(a, b, tm, name):
    K, N = b.shape
    if a.ndim == 3:
        M = a.shape[0] * W
        per = W // tm
        a_spec = pl.BlockSpec((None, K, tm), lambda i: (i // per, 0, i % per))
    else:
        M = a.shape[1]
        a_spec = pl.BlockSpec((K, tm), lambda i: (0, i))

    def body(a_ref, b_ref, o_ref):
        o_ref[...] = _dot_tn(a_ref[...], b_ref[...]).astype(BF)

    return pl.pallas_call(
        body, grid=(M // tm,),
        in_specs=[a_spec, pl.BlockSpec((K, N), lambda i: (0, 0))],
        out_specs=pl.BlockSpec((tm, N), lambda i: (i, 0)),
        out_shape=jax.ShapeDtypeStruct((M, N), BF), name=name, compiler_params=_cp(56),
    )(a, b)


def _mm_nn_res(a, w, res, tm, tk, name, comm=None):
    K, N = w.shape
    if a.ndim == 3:
        M = a.shape[1]
        tm = min(tm, M)
        per = W // tk
        a_spec = pl.BlockSpec((None, tm, tk), lambda i, k: (k // per, i, k % per))
    else:
        M = a.shape[0]
        tm = min(tm, M)
        a_spec = pl.BlockSpec((tm, tk), lambda i, k: (i, k))

    def body(a_ref, w_ref, r_ref, o_ref):
        k = pl.program_id(1)
        d = _dot(a_ref[...], w_ref[...])

        @pl.when(k == 0)
        def _():
            o_ref[...] = ALPHA * r_ref[...] + d

        @pl.when(k > 0)
        def _():
            o_ref[...] += d

    (out,), extra = _pcall(
        body, grid=(M // tm, K // tk),
        in_specs=[a_spec, pl.BlockSpec((tk, N), lambda i, k: (k, 0)), pl.BlockSpec((tm, N), lambda i, k: (i, 0))],
        out_specs=[pl.BlockSpec((tm, N), lambda i, k: (i, 0))],
        out_shape=[jax.ShapeDtypeStruct((M, N), F32)], name=name, comm=comm)(a, w, res)
    return out, extra


def _mm_out_ln(mix3, w_out, x, g, b, name, comm=None):
    S = x.shape[0]
    tm = min(256, S)

    def body(m_ref, w_ref, x_ref, g_ref, b_ref, z_ref, xn_ref, xb_ref):
        acc = _dot(m_ref[0], w_ref[0:W, :]) + _dot(m_ref[1], w_ref[W:2 * W, :])
        z = ALPHA * x_ref[...] + acc
        mu = jnp.mean(z, axis=1, keepdims=True)
        zc = z - mu
        var = jnp.mean(zc * zc, axis=1, keepdims=True)
        xn = zc * lax.rsqrt(var + LN_EPS) * g_ref[...] + b_ref[...]
        z_ref[...] = z
        xn_ref[...] = xn
        xb_ref[...] = xn.astype(BF)

    row = pl.BlockSpec((tm, D), lambda i: (i, 0))
    vec = pl.BlockSpec((1, D), lambda i: (0, 0))
    return _pcall(
        body, grid=(S // tm,),
        in_specs=[pl.BlockSpec((2, tm, W), lambda i: (0, i, 0)), pl.BlockSpec((D, D), lambda i: (0, 0)), row, vec, vec],
        out_specs=[row, row, row],
        out_shape=[jax.ShapeDtypeStruct((S, D), F32), jax.ShapeDtypeStruct((S, D), F32), jax.ShapeDtypeStruct((S, D), BF)],
        name=name, comm=comm)(mix3, w_out, x, g.reshape(1, D), b.reshape(1, D))


def _ln_bwd(dxn, z, g, name):
    S = z.shape[0]
    tm = min(256, S)

    def body(d_ref, z_ref, g_ref, dz_ref, dzb_ref, dg_ref, db_ref):
        i = pl.program_id(0)
        zz = z_ref[...]
        mu = jnp.mean(zz, axis=1, keepdims=True)
        zc = zz - mu
        var = jnp.mean(zc * zc, axis=1, keepdims=True)
        rstd = lax.rsqrt(var + LN_EPS)
        xhat = zc * rstd
        dy = d_ref[...]
        dyg = dy * g_ref[...]
        m1 = jnp.mean(dyg, axis=1, keepdims=True)
        m2 = jnp.mean(dyg * xhat, axis=1, keepdims=True)
        dz = rstd * (dyg - m1 - xhat * m2)
        dz_ref[...] = dz
        dzb_ref[...] = dz.astype(BF)

        @pl.when(i == 0)
        def _():
            dg_ref[...] = jnp.zeros_like(dg_ref)
            db_ref[...] = jnp.zeros_like(db_ref)

        dg_ref[...] += jnp.sum(dy * xhat, axis=0, keepdims=True)
        db_ref[...] += jnp.sum(dy, axis=0, keepdims=True)

    row = pl.BlockSpec((tm, D), lambda i: (i, 0))
    vec = pl.BlockSpec((1, D), lambda i: (0, 0))
    return pl.pallas_call(
        body, grid=(S // tm,), in_specs=[row, row, vec], out_specs=[row, row, vec, vec],
        out_shape=[jax.ShapeDtypeStruct((S, D), F32), jax.ShapeDtypeStruct((S, D), BF),
                   jax.ShapeDtypeStruct((1, D), F32), jax.ShapeDtypeStruct((1, D), F32)],
        name=name, compiler_params=_cp(),
    )(dxn, z, g.reshape(1, D))


def _loss_grad(xn, target):
    S = xn.shape[0]
    tm = min(256, S)

    def body(x_ref, t_ref, d_ref, p_ref):
        i = pl.program_id(0)
        e = x_ref[...] - t_ref[...]
        d_ref[...] = e * (1.0 / D)

        @pl.when(i == 0)
        def _():
            p_ref[...] = jnp.zeros_like(p_ref)

        p_ref[...] += jnp.sum(jnp.sum(e * e, axis=1, keepdims=True), axis=0, keepdims=True)

    row = pl.BlockSpec((tm, D), lambda i: (i, 0))
    return pl.pallas_call(
        body, grid=(S // tm,), in_specs=[row, row],
        out_specs=[row, pl.BlockSpec((8, 128), lambda i: (0, 0))],
        out_shape=[jax.ShapeDtypeStruct((S, D), F32), jax.ShapeDtypeStruct((8, 128), F32)],
        name="loss_grad", compiler_params=_cp(),
    )(xn, target)


def _rope_fwd(t, r_ref):
    return (t * r_ref[:, 0:128] + pltpu.roll(t, 120, 1) * r_ref[:, 128:256]
            + pltpu.roll(t, 8, 1) * r_ref[:, 256:384])


def _rope_bwd(g, r_ref):
    return (g * r_ref[:, 0:128] + pltpu.roll(g * r_ref[:, 128:256], 8, 1)
            + pltpu.roll(g * r_ref[:, 256:384], 120, 1))


def _dup_heads(kb):
    lo = lax.broadcasted_iota(jnp.int32, kb.shape, 1) < 64
    sw = pltpu.roll(kb, 64, 1)
    return [jnp.where(lo, kb, sw).astype(BF), jnp.where(lo, sw, kb).astype(BF)]


def _even_fwd(h, rope, lng, lnb, ws, bsb, sinks, name, comm=None):
    S = h.shape[0]
    nb = S // CHUNK

    def body(h_ref, hp_ref, rc_ref, rp_ref, lng_ref, lnb_ref, ws_ref, bsb_ref, sink_ref, mix_ref, o_ref, l_ref):
        n = pl.program_id(0)
        lane = lax.broadcasted_iota(jnp.int32, (128, 128), 1)
        rowi = lax.broadcasted_iota(jnp.int32, (128, 128), 0)
        tri = rowi >= lane
        lane_lo = lane < 64
        v = h_ref[:, W:2 * W]
        mu = jnp.mean(v, axis=1, keepdims=True)
        vc = v - mu
        var = jnp.mean(vc * vc, axis=1, keepdims=True)
        vn = vc * lax.rsqrt(var + LN_EPS) * lng_ref[...] + lnb_ref[...]
        for g in range(8):
            sl = slice(g * 128, (g + 1) * 128)
            w = jnp.where(tri, ws_ref[g], 0.0).astype(BF)
            m = _dot(w, vn[:, sl].astype(BF)) + bsb_ref[g]
            ag = h_ref[:, 2 * W + g * 128:2 * W + (g + 1) * 128]
            mix_ref[0, :, sl] = (h_ref[:, sl] * m * (ag * _sig(ag))).astype(BF)
        kb = jnp.concatenate([_rope_fwd(hp_ref[:, 0:128], rp_ref), _rope_fwd(h_ref[:, 4096:4224], rc_ref)], axis=0)
        vb = jnp.concatenate([hp_ref[:, 128:256], h_ref[:, 4224:4352]], axis=0)
        k2 = _dup_heads(kb)
        v2 = _dup_heads(vb)
        qi = lax.broadcasted_iota(jnp.int32, (128, 256), 0)
        kj = lax.broadcasted_iota(jnp.int32, (128, 256), 1)
        diff = qi + 128 - kj
        valid = (diff >= 0) & (diff < 128) & ((n > 0) | (kj >= 128))
        lacc = jnp.zeros((128, 128), F32)
        for j in range(8):
            hk = j // 4
            cs = slice(j * 128, (j + 1) * 128)
            qc = _rope_fwd(h_ref[:, 3072 + j * 128:3072 + (j + 1) * 128], rc_ref)
            ocol = jnp.zeros((128, 128), F32)
            for half in range(2):
                hq = 2 * j + half
                hm = lane_lo if half == 0 else jnp.logical_not(lane_lo)
                qm = jnp.where(hm, qc, 0.0).astype(BF)
                s = jnp.where(valid, _dot_nt(qm, k2[hk]) * 0.125, NEG)
                sk = sink_ref[hq]
                mx = jnp.maximum(jnp.max(s, axis=1, keepdims=True), sk)
                p = jnp.exp(s - mx)
                den = jnp.sum(p, axis=1, keepdims=True) + jnp.exp(sk - mx)
                oh = _dot((p / den).astype(BF), v2[hk])
                ocol = jnp.where(hm, oh, ocol)
                lacc = jnp.where(lane == hq, mx + jnp.log(den), lacc)
            bg = h_ref[:, 4352 + j * 128:4352 + (j + 1) * 128]
            o_ref[:, cs] = ocol
            mix_ref[1, :, cs] = (ocol * (bg * _sig(bg))).astype(BF)
        l_ref[...] = lacc

    prev = lambda n: jnp.maximum(n - 1, 0)
    full = lambda shape: pl.BlockSpec(shape, lambda n: (0,) * len(shape))
    return _pcall(
        body, grid=(nb,),
        in_specs=[pl.BlockSpec((CHUNK, EVEN_IN), lambda n: (n, 0)),
                  pl.BlockSpec((CHUNK, 256), lambda n: (prev(n), 16)),
                  pl.BlockSpec((CHUNK, 384), lambda n: (n, 0)),
                  pl.BlockSpec((CHUNK, 384), lambda n: (prev(n), 0)),
                  full((1, W)), full((1, W)), full((8, 128, 128)), full((8, 128, 128)),
                  pl.BlockSpec(memory_space=pltpu.SMEM)],
        out_specs=[pl.BlockSpec((2, CHUNK, W), lambda n: (0, n, 0)),
                   pl.BlockSpec((CHUNK, W), lambda n: (n, 0)),
                   pl.BlockSpec((CHUNK, 128), lambda n: (n, 0))],
        out_shape=[jax.ShapeDtypeStruct((2, S, W), BF), jax.ShapeDtypeStruct((S, W), F32),
                   jax.ShapeDtypeStruct((S, 128), F32)],
        name=name, comm=comm)(h, h, rope, rope, lng.reshape(1, W), lnb.reshape(1, W), ws, bsb, sinks)


def _even_bwd(h, dmix3, o, l, rope, lng, lnb, ws, wst, bsb, sinks, name, comm=None):
    S = h.shape[0]
    nb = S // CHUNK

    def body(h_ref, hp_ref, hn_ref, dm_ref, dmn_ref, o_ref, on_ref, l_ref, ln_ref, rc_ref, rp_ref, rn_ref,
             lng_ref, lnb_ref, ws_ref, wst_ref, bsb_ref, sink_ref,
             dh_ref, dws_ref, dbs_ref, dlng_ref, dlnb_ref, dsink_ref, dvn_ref):
        n = pl.program_id(0)

        @pl.when(n == 0)
        def _():
            dws_ref[...] = jnp.zeros_like(dws_ref)
            dbs_ref[...] = jnp.zeros_like(dbs_ref)
            dlng_ref[...] = jnp.zeros_like(dlng_ref)
            dlnb_ref[...] = jnp.zeros_like(dlnb_ref)
            dsink_ref[...] = jnp.zeros_like(dsink_ref)

        lane = lax.broadcasted_iota(jnp.int32, (128, 128), 1)
        rowi = lax.broadcasted_iota(jnp.int32, (128, 128), 0)
        lane1 = lax.broadcasted_iota(jnp.int32, (1, 128), 1)
        tri = rowi >= lane
        tri_t = lane >= rowi
        lane_lo = lane < 64
        v = h_ref[:, W:2 * W]
        mu = jnp.mean(v, axis=1, keepdims=True)
        vc = v - mu
        var = jnp.mean(vc * vc, axis=1, keepdims=True)
        rstd = lax.rsqrt(var + LN_EPS)
        vhat = vc * rstd
        vn = vhat * lng_ref[...] + lnb_ref[...]
        dbs_acc = jnp.zeros((128, 128), F32)
        for g in range(8):
            sl = slice(g * 128, (g + 1) * 128)
            w = jnp.where(tri, ws_ref[g], 0.0).astype(BF)
            wt = jnp.where(tri_t, wst_ref[g], 0.0).astype(BF)
            vng = vn[:, sl].astype(BF)
            m = _dot(w, vng) + bsb_ref[g]
            ag = h_ref[:, 2 * W + g * 128:2 * W + (g + 1) * 128]
            sg, dsg = _silu_grad(ag)
            u = h_ref[:, sl]
            da = dm_ref[0, :, sl]
            dmm = da * u * sg
            dh_ref[:, sl] = (da * m * sg).astype(BF)
            dh_ref[:, 2 * W + g * 128:2 * W + (g + 1) * 128] = (da * u * m * dsg).astype(BF)
            dmb = dmm.astype(BF)
            dvn_ref[:, sl] = _dot(wt, dmb)
            dws_ref[g] += jnp.where(tri, _dot_nt(dmb, vng), 0.0)
            dbs_acc = jnp.where(lane == g, jnp.sum(dmm, axis=1, keepdims=True), dbs_acc)
        dbs_ref[...] += dbs_acc
        dvn = dvn_ref[...]
        dlng_ref[...] += jnp.sum(dvn * vhat, axis=0, keepdims=True)
        dlnb_ref[...] += jnp.sum(dvn, axis=0, keepdims=True)
        dyg = dvn * lng_ref[...]
        m1 = jnp.mean(dyg, axis=1, keepdims=True)
        m2 = jnp.mean(dyg * vhat, axis=1, keepdims=True)
        dh_ref[:, W:2 * W] = (rstd * (dyg - m1 - vhat * m2)).astype(BF)
        kcur = _rope_fwd(h_ref[:, 4096:4224], rc_ref)
        kb = jnp.concatenate([_rope_fwd(hp_ref[:, 0:128], rp_ref), kcur], axis=0)
        vb = jnp.concatenate([hp_ref[:, 128:256], h_ref[:, 4224:4352]], axis=0)
        k2 = _dup_heads(kb)
        v2 = _dup_heads(vb)
        kc2 = _dup_heads(kcur)
        vc2 = _dup_heads(h_ref[:, 4224:4352])
        qi = lax.broadcasted_iota(jnp.int32, (128, 256), 0)
        kj = lax.broadcasted_iota(jnp.int32, (128, 256), 1)
        diff = qi + 128 - kj
        valid = (diff >= 0) & (diff < 128) & ((n > 0) | (kj >= 128))
        validn = (lane > rowi) & (n < nb - 1)
        lc = l_ref[...]
        lnx = ln_ref[...]
        dk = [jnp.zeros((128, 128), F32), jnp.zeros((128, 128), F32)]
        dv = [jnp.zeros((128, 128), F32), jnp.zeros((128, 128), F32)]
        dsk_acc = jnp.zeros((1, 128), F32)
        for j in range(8):
            hk = j // 4
            cs = slice(j * 128, (j + 1) * 128)
            qc = _rope_fwd(h_ref[:, 3072 + j * 128:3072 + (j + 1) * 128], rc_ref)
            qn = _rope_fwd(hn_ref[:, 3072 + j * 128:3072 + (j + 1) * 128], rn_ref)
            bg = h_ref[:, 4352 + j * 128:4352 + (j + 1) * 128]
            sgb, dsgb = _silu_grad(bg)
            db = dm_ref[1, :, cs]
            oc = o_ref[:, cs]
            do = db * sgb
            dh_ref[:, 4352 + j * 128:4352 + (j + 1) * 128] = (db * oc * dsgb).astype(BF)
            bgn = hn_ref[:, 4352 + j * 128:4352 + (j + 1) * 128]
            don = dmn_ref[1, :, cs] * (bgn * _sig(bgn))
            prod = do * oc
            prodn = don * on_ref[:, cs]
            dqcol = jnp.zeros((128, 128), F32)
            for half in range(2):
                hq = 2 * j + half
                hm = lane_lo if half == 0 else jnp.logical_not(lane_lo)
                dsum = jnp.sum(jnp.where(hm, prod, 0.0), axis=1, keepdims=True)
                dsumn = jnp.sum(jnp.where(hm, prodn, 0.0), axis=1, keepdims=True)
                lh = jnp.sum(jnp.where(lane == hq, lc, 0.0), axis=1, keepdims=True)
                lhn = jnp.sum(jnp.where(lane == hq, lnx, 0.0), axis=1, keepdims=True)
                qm = jnp.where(hm, qc, 0.0).astype(BF)
                dom = jnp.where(hm, do, 0.0).astype(BF)
                s = _dot_nt(qm, k2[hk]) * 0.125
                p = jnp.exp(jnp.where(valid, s - lh, NEG))
                ds = p * (_dot_nt(dom, v2[hk]) - dsum)
                dqcol = jnp.where(hm, _dot(ds.astype(BF), k2[hk]) * 0.125, dqcol)
                psink = jnp.exp(sink_ref[hq] - lh)
                dsk = -jnp.sum(psink * dsum, axis=0, keepdims=True)
                dsk_acc = jnp.where(lane1 == hq, dsk, dsk_acc)
                dv[hk] = dv[hk] + _dot(jnp.transpose(p[:, 128:256]).astype(BF), dom)
                dk[hk] = dk[hk] + _dot(jnp.transpose(ds[:, 128:256]).astype(BF), qm) * 0.125
                qnm = jnp.where(hm, qn, 0.0).astype(BF)
                donm = jnp.where(hm, don, 0.0).astype(BF)
                sn = _dot_nt(qnm, kc2[hk]) * 0.125
                pn = jnp.exp(jnp.where(validn, sn - lhn, NEG))
                dsn = pn * (_dot_nt(donm, vc2[hk]) - dsumn)
                dv[hk] = dv[hk] + _dot(jnp.transpose(pn).astype(BF), donm)
                dk[hk] = dk[hk] + _dot(jnp.transpose(dsn).astype(BF), qnm) * 0.125
            dh_ref[:, 3072 + j * 128:3072 + (j + 1) * 128] = _rope_bwd(dqcol, rc_ref).astype(BF)
        fold = lambda a: a + pltpu.roll(a, 64, 1)
        dh_ref[:, 4096:4224] = _rope_bwd(jnp.where(lane_lo, fold(dk[0]), fold(dk[1])), rc_ref).astype(BF)
        dh_ref[:, 4224:4352] = jnp.where(lane_lo, fold(dv[0]), fold(dv[1])).astype(BF)
        dsink_ref[...] += dsk_acc

    prev = lambda n: jnp.maximum(n - 1, 0)
    nxt = lambda n: jnp.minimum(n + 1, nb - 1)
    full = lambda shape: pl.BlockSpec(shape, lambda n: (0,) * len(shape))
    return _pcall(
        body, grid=(nb,),
        in_specs=[pl.BlockSpec((CHUNK, EVEN_IN), lambda n: (n, 0)),
                  pl.BlockSpec((CHUNK, 256), lambda n: (prev(n), 16)),
                  pl.BlockSpec((CHUNK, EVEN_IN), lambda n: (nxt(n), 0)),
                  pl.BlockSpec((2, CHUNK, W), lambda n: (0, n, 0)),
                  pl.BlockSpec((2, CHUNK, W), lambda n: (0, nxt(n), 0)),
                  pl.BlockSpec((CHUNK, W), lambda n: (n, 0)),
                  pl.BlockSpec((CHUNK, W), lambda n: (nxt(n), 0)),
                  pl.BlockSpec((CHUNK, 128), lambda n: (n, 0)),
                  pl.BlockSpec((CHUNK, 128), lambda n: (nxt(n), 0)),
                  pl.BlockSpec((CHUNK, 384), lambda n: (n, 0)),
                  pl.BlockSpec((CHUNK, 384), lambda n: (prev(n), 0)),
                  pl.BlockSpec((CHUNK, 384), lambda n: (nxt(n), 0)),
                  full((1, W)), full((1, W)), full((8, 128, 128)), full((8, 128, 128)), full((8, 128, 128)),
                  pl.BlockSpec(memory_space=pltpu.SMEM)],
        out_specs=[pl.BlockSpec((CHUNK, EVEN_IN), lambda n: (n, 0)),
                   full((8, 128, 128)), full((128, 128)), full((1, W)), full((1, W)), full((1, 128))],
        out_shape=[jax.ShapeDtypeStruct((S, EVEN_IN), BF), jax.ShapeDtypeStruct((8, 128, 128), F32),
                   jax.ShapeDtypeStruct((128, 128), F32), jax.ShapeDtypeStruct((1, W), F32),
                   jax.ShapeDtypeStruct((1, W), F32), jax.ShapeDtypeStruct((1, 128), F32)],
        scratch=[pltpu.VMEM((CHUNK, W), F32)], name=name, comm=comm,
    )(h, h, h, dmix3, dmix3, o, o, l, l, rope, rope, rope, lng.reshape(1, W), lnb.reshape(1, W), ws, wst, bsb, sinks)


def _expm1(x):
    ser = x * (1.0 + x * (0.5 + x * (1.0 / 6.0 + x * (1.0 / 24.0))))
    return jnp.where(jnp.abs(x) < 1e-2, ser, jnp.exp(x) - 1.0)


def _softplus_neg(lam):
    z = -lam
    e = jnp.exp(-jnp.abs(z))
    l1p = jnp.where(e < 1e-3, e * (1.0 - e * (0.5 - e * (1.0 / 3.0))), jnp.log(1.0 + e))
    return jnp.maximum(z, 0.0) + l1p


def _shift_down(x, k, row, fill=0.0):
    return jnp.where(row >= k, pltpu.roll(x, k, 0), fill)


def _shift_up(x, k, row, fill=0.0):
    S = x.shape[0]
    return jnp.where(row < S - k, pltpu.roll(x, S - k, 0), fill)


def _lru_gates(xc, row, cw_ref, cb_ref, wa_ref, wx_ref, ba_ref, bx_ref, lam_ref):
    xconv = (cw_ref[3:4, :] * xc + cw_ref[2:3, :] * _shift_down(xc, 1, row) + cw_ref[1:2, :] * _shift_down(xc, 2, row)
             + cw_ref[0:1, :] * _shift_down(xc, 3, row) + cb_ref[...])
    xb = xconv.astype(BF)
    r = _sig(_dot(xb, wa_ref[...]) + ba_ref[...])
    i = _sig(_dot(xb, wx_ref[...]) + bx_ref[...])
    sp = _softplus_neg(lam_ref[...])
    log_a = -LRU_C * r * sp
    a = jnp.exp(log_a)
    mult = jnp.sqrt(-_expm1(2.0 * log_a))
    return xconv, r, i, sp, a, mult


def _odd_c_fwd(h, cw, cb, wa, wx, ba, bx, lam, name, comm=None):
    S = h.shape[0]

    def body(xc_ref, cg_ref, cw_ref, cb_ref, wa_ref, wx_ref, ba_ref, bx_ref, lam_ref, mix_ref, hst_ref):
        row = lax.broadcasted_iota(jnp.int32, (S, 128), 0)
        xconv, r, i, sp, a, mult = _lru_gates(xc_ref[...], row, cw_ref, cb_ref, wa_ref, wx_ref, ba_ref, bx_ref, lam_ref)
        aa = a
        bb = mult * (i * xconv)
        k = 1
        while k < S:
            bb = aa * _shift_down(bb, k, row) + bb
            if 2 * k < S:
                aa = aa * _shift_down(aa, k, row, 1.0)
            k *= 2
        hst_ref[...] = bb
        cg = cg_ref[...]
        mix_ref[...] = (bb * (cg * _sig(cg))).astype(BF)

    col = lambda off: pl.BlockSpec((S, 128), lambda j: (0, off + j))
    vec = pl.BlockSpec((1, 128), lambda j: (0, j))
    mat = pl.BlockSpec((None, 128, 128), lambda j: (j, 0, 0))
    return _pcall(
        body, grid=(8,),
        in_specs=[col(0), col(8), pl.BlockSpec((4, 128), lambda j: (0, j)), vec, mat, mat, vec, vec, vec],
        out_specs=[pl.BlockSpec((None, S, 128), lambda j: (0, 0, j)), pl.BlockSpec((S, 128), lambda j: (0, j))],
        out_shape=[jax.ShapeDtypeStruct((2, S, W), BF), jax.ShapeDtypeStruct((S, W), F32)],
        name=name, comm=comm,
    )(h, h, cw, cb.reshape(1, W), wa, wx, ba.reshape(1, W), bx.reshape(1, W), lam.reshape(1, W))


def _pool_sums(x, g, row, shift):
    s2 = x + shift(x, 1, row)
    s4 = s2 + shift(s2, 2, row)
    s8 = s4 + shift(s4, 4, row)
    s16 = s8 + shift(s8, 8, row)
    return jnp.where(g == 0, s2, jnp.where(g == 1, s4, jnp.where(g == 2, s8, s16)))


def _odd_d_fwd(h, mix3, wp, dscale, name):
    S = h.shape[0]

    def body(xd_ref, dg_ref, wp_ref, ds_ref, mix_in, mix_ref):
        g = pl.program_id(0)
        row = lax.broadcasted_iota(jnp.int32, (S, 256), 0)
        xd = xd_ref[...]
        cnt = jnp.minimum(row + 1, jnp.left_shift(2, g)).astype(F32)
        pooled = _pool_sums(xd, g, row, _shift_down) / cnt - xd
        mixed = _dot(pooled.astype(BF), wp_ref[...])
        dg = dg_ref[...]
        mix_ref[...] = (mixed * ds_ref[...] * (dg * _sig(dg))).astype(BF)

    col = lambda off: pl.BlockSpec((S, 256), lambda g: (0, off + g))
    return pl.pallas_call(
        body, grid=(4,),
        in_specs=[col(8), col(12), pl.BlockSpec((None, 256, 256), lambda g: (g, 0, 0)),
                  pl.BlockSpec((1, 256), lambda g: (0, g)), ANY],
        out_specs=pl.BlockSpec((None, S, 256), lambda g: (1, 0, g)),
        out_shape=jax.ShapeDtypeStruct((2, S, W), BF), input_output_aliases={4: 0},
        name=name, compiler_params=_cp(),
    )(h, h, wp, dscale.reshape(1, W), mix3)


def _odd_c_bwd(h, hst, dmix3, cw, cb, wa, wx, wat, wxt, ba, bx, lam, name, comm=None):
    S = h.shape[0]

    def body(xc_ref, cg_ref, hst_ref, dc_ref, cw_ref, cb_ref, wa_ref, wx_ref, wat_ref, wxt_ref, ba_ref, bx_ref, lam_ref,
             dh_ref, dcw_ref, dcb_ref, dwa_ref, dwx_ref, dba_ref, dbx_ref, dlam_ref):
        row = lax.broadcasted_iota(jnp.int32, (S, 128), 0)
        xc = xc_ref[...]
        xconv, r, i, sp, a, mult = _lru_gates(xc, row, cw_ref, cb_ref, wa_ref, wx_ref, ba_ref, bx_ref, lam_ref)
        hst = hst_ref[...]
        cg = cg_ref[...]
        sg, dsg = _silu_grad(cg)
        dc = dc_ref[...]
        dh_ref[1] = (dc * hst * dsg).astype(BF)
        aa = _shift_up(a, 1, row)
        bb = dc * sg
        k = 1
        while k < S:
            bb = aa * _shift_up(bb, k, row) + bb
            if 2 * k < S:
                aa = aa * _shift_up(aa, k, row, 1.0)
            k *= 2
        lam_t = bb
        da = lam_t * _shift_down(hst, 1, row)
        ix = i * xconv
        dmult = lam_t * ix
        di = lam_t * mult * xconv
        dxconv = lam_t * mult * i
        dlog_a = da * a - dmult * (a * a / mult)
        dr = dlog_a * (-LRU_C * sp)
        dsp = jnp.sum(dlog_a * (-LRU_C * r), axis=0, keepdims=True)
        dlam_ref[...] = dsp * (-_sig(-lam_ref[...]))
        dpa = dr * r * (1.0 - r)
        dpx = di * i * (1.0 - i)
        dpab = dpa.astype(BF)
        dpxb = dpx.astype(BF)
        xb = xconv.astype(BF)
        dxconv = dxconv + _dot(dpab, wat_ref[...]) + _dot(dpxb, wxt_ref[...])
        dwa_ref[...] = _dot_tn(xb, dpab)
        dwx_ref[...] = _dot_tn(xb, dpxb)
        dba_ref[...] = jnp.sum(dpa, axis=0, keepdims=True)
        dbx_ref[...] = jnp.sum(dpx, axis=0, keepdims=True)
        dh_ref[0] = (cw_ref[3:4, :] * dxconv + cw_ref[2:3, :] * _shift_up(dxconv, 1, row)
                     + cw_ref[1:2, :] * _shift_up(dxconv, 2, row) + cw_ref[0:1, :] * _shift_up(dxconv, 3, row)).astype(BF)
        for j in range(4):
            src = xc if j == 3 else _shift_down(xc, 3 - j, row)
            dcw_ref[j:j + 1, :] = jnp.sum(dxconv * src, axis=0, keepdims=True)
        dcb_ref[...] = jnp.sum(dxconv, axis=0, keepdims=True)

    col = lambda off: pl.BlockSpec((S, 128), lambda j: (0, off + j))
    vec = pl.BlockSpec((1, 128), lambda j: (0, j))
    mat = pl.BlockSpec((None, 128, 128), lambda j: (j, 0, 0))
    vshape = jax.ShapeDtypeStruct((1, W), F32)
    mshape = jax.ShapeDtypeStruct((8, 128, 128), F32)
    return _pcall(
        body, grid=(8,),
        in_specs=[col(0), col(8), col(0), pl.BlockSpec((None, S, 128), lambda j: (0, 0, j)),
                  pl.BlockSpec((4, 128), lambda j: (0, j)), vec, mat, mat, mat, mat, vec, vec, vec],
        out_specs=[pl.BlockSpec((2, S, 128), lambda j: (0, 0, j)), pl.BlockSpec((4, 128), lambda j: (0, j)), vec,
                   mat, mat, vec, vec, vec],
        out_shape=[jax.ShapeDtypeStruct((4, S, W), BF), jax.ShapeDtypeStruct((4, W), F32), vshape, mshape, mshape,
                   vshape, vshape, vshape],
        name=name, vmem=56, comm=comm,
    )(h, h, hst, dmix3, cw, cb.reshape(1, W), wa, wx, wat, wxt, ba.reshape(1, W), bx.reshape(1, W), lam.reshape(1, W))


def _odd_d_bwd(h, dmix3, dh4, wp, wpt, dscale, name):
    S = h.shape[0]

    def body(xd_ref, dg_ref, dd_ref, wp_ref, wpt_ref, ds_ref, dh_in, dh_ref, dwp_ref, dds_ref):
        g = pl.program_id(0)
        row = lax.broadcasted_iota(jnp.int32, (S, 256), 0)
        xd = xd_ref[...]
        cnt = jnp.minimum(row + 1, jnp.left_shift(2, g)).astype(F32)
        pooled = _pool_sums(xd, g, row, _shift_down) / cnt - xd
        pb = pooled.astype(BF)
        mixed = _dot(pb, wp_ref[...])
        dg = dg_ref[...]
        sg, dsg = _silu_grad(dg)
        dd = dd_ref[...]
        dmixed = dd * ds_ref[...] * sg
        dds_ref[...] = jnp.sum(dd * mixed * sg, axis=0, keepdims=True)
        dh_ref[1] = (dd * mixed * ds_ref[...] * dsg).astype(BF)
        dmb = dmixed.astype(BF)
        dpooled = _dot(dmb, wpt_ref[...])
        dwp_ref[...] = _dot_tn(pb, dmb)
        dh_ref[0] = (_pool_sums(dpooled / cnt, g, row, _shift_up) - dpooled).astype(BF)

    col = lambda off: pl.BlockSpec((S, 256), lambda g: (0, off + g))
    mat = pl.BlockSpec((None, 256, 256), lambda g: (g, 0, 0))
    vec = pl.BlockSpec((1, 256), lambda g: (0, g))
    return pl.pallas_call(
        body, grid=(4,),
        in_specs=[col(8), col(12), pl.BlockSpec((None, S, 256), lambda g: (1, 0, g)), mat, mat, vec, ANY],
        out_specs=[pl.BlockSpec((2, S, 256), lambda g: (1, 0, g)), mat, vec],
        out_shape=[jax.ShapeDtypeStruct((4, S, W), BF), jax.ShapeDtypeStruct((4, 256, 256), F32),
                   jax.ShapeDtypeStruct((1, W), F32)],
        input_output_aliases={6: 0}, name=name, compiler_params=_cp(56),
    )(h, h, dmix3, wp, wpt, dscale.reshape(1, W), dh4)


def _peer(d):
    x, y, c = lax.axis_index("x"), lax.axis_index("y"), lax.axis_index("c")
    px = 1 - x if d & 4 else x
    py = 1 - y if d & 2 else y
    pc = 1 - c if d & 1 else c
    return (px, py, pc), 4 * px + 2 * py + pc


def _all_gather(xs, name):
    R, C = xs.shape

    def body(x_ref, out_ref, send_sems, recv_sems, local_sem):
        _, me = _peer(0)
        mine = pltpu.make_async_copy(x_ref, out_ref.at[me], local_sem)
        mine.start()
        copies = []
        for d in range(1, N_DEV):
            peer, _ = _peer(d)
            cp = pltpu.make_async_remote_copy(src_ref=x_ref, dst_ref=out_ref.at[me], send_sem=send_sems.at[d - 1],
                                              recv_sem=recv_sems.at[d - 1], device_id=peer, device_id_type=MESH)
            cp.start()
            copies.append(cp)
        for cp in copies:
            cp.wait()
        mine.wait()

    return pl.pallas_call(
        body, in_specs=[ANY], out_specs=ANY, out_shape=jax.ShapeDtypeStruct((N_DEV, R, C), xs.dtype),
        scratch_shapes=[pltpu.SemaphoreType.DMA((N_DEV - 1,)), pltpu.SemaphoreType.DMA((N_DEV - 1,)),
                        pltpu.SemaphoreType.DMA],
        name=name,
    )(xs)


def _exchange(g8, name):
    _, R, C = g8.shape

    def body(g_ref, out_ref, send_sems, recv_sems, local_sem):
        _, me = _peer(0)
        mine = pltpu.make_async_copy(g_ref.at[me], out_ref.at[0], local_sem)
        mine.start()
        copies = []
        for d in range(1, N_DEV):
            peer, pidx = _peer(d)
            cp = pltpu.make_async_remote_copy(src_ref=g_ref.at[pidx], dst_ref=out_ref.at[d], send_sem=send_sems.at[d - 1],
                                              recv_sem=recv_sems.at[d - 1], device_id=peer, device_id_type=MESH)
            cp.start()
            copies.append(cp)
        for cp in copies:
            cp.wait()
        mine.wait()

    return pl.pallas_call(
        body, in_specs=[ANY], out_specs=ANY, out_shape=jax.ShapeDtypeStruct((N_DEV, R, C), g8.dtype),
        scratch_shapes=[pltpu.SemaphoreType.DMA((N_DEV - 1,)), pltpu.SemaphoreType.DMA((N_DEV - 1,)),
                        pltpu.SemaphoreType.DMA],
        name=name,
    )(g8)


def _sum8(r8, tr, name):
    _, R, C = r8.shape
    tr = min(tr, R)
    assert R % tr == 0

    def body(r_ref, o_ref):
        acc = r_ref[0]
        for d in range(1, N_DEV):
            acc = acc + r_ref[d]
        o_ref[...] = acc

    return pl.pallas_call(
        body, grid=(R // tr,), in_specs=[pl.BlockSpec((N_DEV, tr, C), lambda i: (0, i, 0))],
        out_specs=pl.BlockSpec((tr, C), lambda i: (i, 0)), out_shape=jax.ShapeDtypeStruct((R, C), F32),
        name=name, compiler_params=_cp(),
    )(r8)


def _adamw(w, g, m, v, tr, name):
    R, C = w.shape
    tr = min(tr, R)

    def body(w_ref, g_ref, m_ref, v_ref, d_ref, m2_ref, v2_ref):
        gg = g_ref[...]
        m2 = B1 * m_ref[...] + (1.0 - B1) * gg
        v2 = B2 * v_ref[...] + (1.0 - B2) * (gg * gg)
        m_hat = m2 / (1.0 - B1 ** STEP)
        v_hat = v2 / (1.0 - B2 ** STEP)
        d_ref[...] = -LR * (m_hat / (jnp.sqrt(v_hat) + ADAM_EPS) + WD * w_ref[...])
        m2_ref[...] = m2
        v2_ref[...] = v2

    blk = pl.BlockSpec((tr, C), lambda i: (i, 0))
    shp = jax.ShapeDtypeStruct((R, C), F32)
    return pl.pallas_call(
        body, grid=(R // tr,), in_specs=[blk] * 4, out_specs=[blk] * 3, out_shape=[shp] * 3,
        name=name, compiler_params=_cp(),
    )(w, g, m, v)


def _rep_pack(a):
    n = a.size
    pad = (-n) % 1024
    f = a.reshape(-1)
    if pad:
        f = jnp.concatenate([f, jnp.zeros((pad,), a.dtype)])
    return f.reshape(N_DEV, -1, 128)


def _rep_unpack(p, shape):
    n = 1
    for s in shape:
        n *= s
    return p.reshape(-1)[:n].reshape(shape)


def _sh_pack(a, axis):
    shp = a.shape
    a = a.reshape(shp[:axis] + (N_DEV, shp[axis] // N_DEV) + shp[axis + 1:])
    return jnp.moveaxis(a, axis, 0).reshape(N_DEV, -1, 128)


def _sh_unpack(p, shape, axis):
    a = p.reshape((N_DEV,) + shape[:axis] + (shape[axis] // N_DEV,) + shape[axis + 1:])
    return jnp.moveaxis(a, 0, axis).reshape(shape)


def _pad_rows(a, mult=8):
    pad = (-a.shape[-2]) % mult
    if pad:
        a = jnp.concatenate([a, jnp.zeros(a.shape[:-2] + (pad, a.shape[-1]), a.dtype)], axis=-2)
    return a


REP = ["even_a_ln_g", "even_a_ln_b", "even_a_ws", "even_a_bs", "even_b_sinks", "even_ln_g", "even_ln_b",
       "odd_w_a", "odd_w_x"]
SH = [("odd_conv_w", (2, 4, W), 2), ("odd_conv_b", (2, W), 1), ("odd_b_a", (2, W), 1), ("odd_b_x", (2, W), 1),
      ("odd_lam", (2, W), 1), ("odd_w_pool", (2, 4, 256, 256), 2), ("odd_d_scale", (2, W), 1),
      ("odd_ln_g", (2, D), 1), ("odd_ln_b", (2, D), 1)]
BIG = ["even_w_in", "even_w_out", "odd_w_in", "odd_w_out"]
NAMES = ["even_w_in", "even_a_ln_g", "even_a_ln_b", "even_a_ws", "even_a_bs", "even_b_sinks", "even_w_out",
         "even_ln_g", "even_ln_b", "odd_w_in", "odd_conv_w", "odd_conv_b", "odd_w_a", "odd_b_a", "odd_w_x", "odd_b_x",
         "odd_lam", "odd_w_pool", "odd_d_scale", "odd_w_out", "odd_ln_g", "odd_ln_b"]


def _rope_table(positions):
    S = positions.shape[0]
    inv = ROPE_THETA ** (-jnp.arange(0, 16, 2, dtype=F32) / 16)
    ang = positions.astype(F32)[:, None] * inv
    cos, sin = jnp.cos(ang), jnp.sin(ang)
    one, zero = jnp.ones((S, 48), F32), jnp.zeros((S, 48), F32)
    z8 = jnp.zeros((S, 8), F32)
    c64 = jnp.concatenate([cos, cos, one], axis=1)
    s1 = jnp.concatenate([-sin, z8, zero], axis=1)
    s2 = jnp.concatenate([z8, sin, zero], axis=1)
    return jnp.concatenate([c64, c64, s1, s1, s2, s2], axis=1)


def kernel(x, positions, even_w_in, even_a_ln_g, even_a_ln_b, even_a_ws, even_a_bs, even_b_sinks, even_w_out, even_ln_g, even_ln_b, odd_w_in, odd_conv_w, odd_conv_b, odd_w_a, odd_b_a, odd_w_x, odd_b_x, odd_lam, odd_w_pool, odd_d_scale, odd_w_out, odd_ln_g, odd_ln_b, loss_target, m_even_w_in, m_even_a_ln_g, m_even_a_ln_b, m_even_a_ws, m_even_a_bs, m_even_b_sinks, m_even_w_out, m_even_ln_g, m_even_ln_b, m_odd_w_in, m_odd_conv_w, m_odd_conv_b, m_odd_w_a, m_odd_b_a, m_odd_w_x, m_odd_b_x, m_odd_lam, m_odd_w_pool, m_odd_d_scale, m_odd_w_out, m_odd_ln_g, m_odd_ln_b, v_even_w_in, v_even_a_ln_g, v_even_a_ln_b, v_even_a_ws, v_even_a_bs, v_even_b_sinks, v_even_w_out, v_even_ln_g, v_even_ln_b, v_odd_w_in, v_odd_conv_w, v_odd_conv_b, v_odd_w_a, v_odd_b_a, v_odd_w_x, v_odd_b_x, v_odd_lam, v_odd_w_pool, v_odd_d_scale, v_odd_w_out, v_odd_ln_g, v_odd_ln_b):
    args = (even_w_in, even_a_ln_g, even_a_ln_b, even_a_ws, even_a_bs, even_b_sinks, even_w_out, even_ln_g, even_ln_b,
            odd_w_in, odd_conv_w, odd_conv_b, odd_w_a, odd_b_a, odd_w_x, odd_b_x, odd_lam, odd_w_pool, odd_d_scale,
            odd_w_out, odd_ln_g, odd_ln_b)
    margs = (m_even_w_in, m_even_a_ln_g, m_even_a_ln_b, m_even_a_ws, m_even_a_bs, m_even_b_sinks, m_even_w_out,
             m_even_ln_g, m_even_ln_b, m_odd_w_in, m_odd_conv_w, m_odd_conv_b, m_odd_w_a, m_odd_b_a, m_odd_w_x,
             m_odd_b_x, m_odd_lam, m_odd_w_pool, m_odd_d_scale, m_odd_w_out, m_odd_ln_g, m_odd_ln_b)
    vargs = (v_even_w_in, v_even_a_ln_g, v_even_a_ln_b, v_even_a_ws, v_even_a_bs, v_even_b_sinks, v_even_w_out,
             v_even_ln_g, v_even_ln_b, v_odd_w_in, v_odd_conv_w, v_odd_conv_b, v_odd_w_a, v_odd_b_a, v_odd_w_x,
             v_odd_b_x, v_odd_lam, v_odd_w_pool, v_odd_d_scale, v_odd_w_out, v_odd_ln_g, v_odd_ln_b)
    wts = dict(zip(NAMES, args))
    mom = dict(zip(NAMES, margs))
    var = dict(zip(NAMES, vargs))
    S = x.shape[1]
    x0 = x[0]
    rope = _rope_table(positions[0])

    kinds = ("even", "odd", "even", "odd")
    blk_in = [jnp.transpose(wts[kinds[l] + "_w_in"][l // 2]).astype(BF) for l in range(4)]
    blk_out = [wts[kinds[l] + "_w_out"][l // 2].astype(BF) for l in range(4)]
    carry = {(0, "mm_h"): ("out", 0), (0, "branch"): ("in", 1), (0, "mm_out"): ("out", 1), (1, "mm_h"): ("in", 2),
             (1, "branch"): ("out", 2), (2, "mm_h"): ("in", 3), (2, "branch"): ("out", 3)}
    wt_in, w_out = [None] * 4, [None] * 4

    def fetch(layer, kernel_name):
        if (layer, kernel_name) not in carry:
            return None
        what, l = carry[layer, kernel_name]
        return _AgIci([blk_in[l] if what == "in" else blk_out[l]])

    def landed(layer, kernel_name, bufs):
        if (layer, kernel_name) not in carry:
            return
        what, l = carry[layer, kernel_name]
        (buf,) = _ag_d2d(list(bufs), [blk_in[l] if what == "in" else blk_out[l]], f"ag_d2d_{what}_{kinds[l]}")
        if what == "in":
            wt_in[l] = buf.reshape(-1, D)
        else:
            w_out[l] = buf.reshape(D, D)

    sh_local = _pad_rows(jnp.concatenate([wts[nm].reshape(-1, 128) for nm, _, _ in SH], axis=0))
    first = _comm_only(_AgIci([blk_in[0], sh_local]), "ag_first")
    first = _ag_d2d(list(first), [blk_in[0], sh_local], "ag_d2d_first")
    wt_in[0] = first[0].reshape(-1, D)
    sh_all = first[1]
    full = {}
    off = 0
    for nm, shape, axis in SH:
        r = wts[nm].size // 128
        full[nm] = _sh_unpack(sh_all[:, off:off + r, :], shape, axis)
        off += r
    for nm in REP:
        full[nm] = wts[nm]

    saved = []
    xf, xb = x0, x0.astype(BF)
    for layer in range(4):
        j = layer // 2
        kind = kinds[layer]
        h, got = _mm_nt(xb, wt_in[layer], 1024, 768 if kind == "even" else 512, "mm_h_" + kind,
                        comm=fetch(layer, "mm_h"))
        landed(layer, "mm_h", got)
        if kind == "even":
            bsb = jnp.broadcast_to(full["even_a_bs"][j][:, :, None], (8, 128, 128))
            (mix3, o, l), got = _even_fwd(h, rope, full["even_a_ln_g"][j], full["even_a_ln_b"][j], full["even_a_ws"][j],
                                          bsb, full["even_b_sinks"][j], "even_fwd", comm=fetch(layer, "branch"))
            extra = (o, l, bsb)
        else:
            wa, wx = full["odd_w_a"][j].astype(BF), full["odd_w_x"][j].astype(BF)
            wp = full["odd_w_pool"][j].astype(BF)
            (mix3, hst), got = _odd_c_fwd(h, full["odd_conv_w"][j], full["odd_conv_b"][j], wa, wx, full["odd_b_a"][j],
                                          full["odd_b_x"][j], full["odd_lam"][j], "odd_c_fwd", comm=fetch(layer, "branch"))
            mix3 = _odd_d_fwd(h, mix3, wp, full["odd_d_scale"][j], "odd_d_fwd")
            extra = (hst, wa, wx, wp)
        landed(layer, "branch", got)
        (z, xn, xnb), got = _mm_out_ln(mix3, w_out[layer], xf, full[kind + "_ln_g"][j], full[kind + "_ln_b"][j],
                                       "mm_out_ln", comm=fetch(layer, "mm_out"))
        landed(layer, "mm_out", got)
        saved.append((xb, h, mix3, z, extra))
        xf, xb = xn, xnb

    dxn, part = _loss_grad(xf, loss_target[0])
    loss = lax.psum(part[0, 0] * (0.5 / D), ("x", "y", "c"))

    gsum = {nm: [None, None] for nm in NAMES}
    gred = {}

    def chip_sum(g, tag):
        g8 = g.reshape(N_DEV, g.shape[0] // N_DEV, D)
        (got,) = _rs_d2d([g8], "rs_d2d_" + tag)
        return _add_pairs(g8, got, "rs_add_" + tag)

    for layer in (3, 2, 1, 0):
        j = layer // 2
        xb, h, mix3, z, extra = saved[layer]
        kind = kinds[layer]
        dz, dzb, dg, dbeta = _ln_bwd(dxn, z, full[kind + "_ln_g"][j], "ln_bwd")
        gsum[kind + "_ln_g"][j] = dg.reshape(D)
        gsum[kind + "_ln_b"][j] = dbeta.reshape(D)
        dmix3, _ = _mm_nt(dzb, w_out[layer], 1024, 512, "mm_dmix", out3=True)
        s_out = chip_sum(_mm_tn(mix3, dzb, 512, "mm_dw_out"), "w_out")
        if kind == "even":
            o, l, bsb = extra
            ws = full["even_a_ws"][j]
            (dh, dws, dbs, dlng, dlnb, dsink), (r_out,) = _even_bwd(
                h, dmix3, o, l, rope, full["even_a_ln_g"][j], full["even_a_ln_b"][j], ws, jnp.swapaxes(ws, 1, 2), bsb,
                full["even_b_sinks"][j], "even_bwd", comm=_RsIci([s_out]))
            gsum["even_a_ws"][j] = dws
            gsum["even_a_bs"][j] = jnp.transpose(dbs[:, :8])
            gsum["even_a_ln_g"][j] = dlng.reshape(W)
            gsum["even_a_ln_b"][j] = dlnb.reshape(W)
            gsum["even_b_sinks"][j] = dsink[0, :16]
            s_in = chip_sum(_mm_tn(dh, xb, 384, "mm_dw_in_even"), "w_in_even")
            dxn, (r_in,) = _mm_nn_res(dh, wt_in[layer], dz, 512, 768, "mm_dx_even", comm=_RsIci([s_in]))
        else:
            hst, wa, wx, wp = extra
            (dh4, dcw, dcb, dwa, dwx, dba, dbx, dlam), (r_out,) = _odd_c_bwd(
                h, hst, dmix3, full["odd_conv_w"][j], full["odd_conv_b"][j], wa, wx, jnp.swapaxes(wa, 1, 2),
                jnp.swapaxes(wx, 1, 2), full["odd_b_a"][j], full["odd_b_x"][j], full["odd_lam"][j], "odd_c_bwd",
                comm=_RsIci([s_out]))
            dh4, dwp, dds = _odd_d_bwd(h, dmix3, dh4, wp, jnp.swapaxes(wp, 1, 2), full["odd_d_scale"][j], "odd_d_bwd")
            gsum["odd_conv_w"][j], gsum["odd_conv_b"][j] = dcw, dcb.reshape(W)
            gsum["odd_w_a"][j], gsum["odd_w_x"][j] = dwa, dwx
            gsum["odd_b_a"][j], gsum["odd_b_x"][j], gsum["odd_lam"][j] = dba.reshape(W), dbx.reshape(W), dlam.reshape(W)
            gsum["odd_w_pool"][j], gsum["odd_d_scale"][j] = dwp, dds.reshape(W)
            s_in = chip_sum(_mm_tn(dh4, xb, 512, "mm_dw_in_odd"), "w_in_odd")
            dxn, (r_in,) = _mm_nn_res(dh4, wt_in[layer], dz, 512, 512, "mm_dx_odd", comm=_RsIci([s_in]))
        gred[kind + "_w_out", j] = _rs_final(s_out, r_out, "rs_final_w_out")
        gred[kind + "_w_in", j] = jnp.transpose(_rs_final(s_in, r_in, "rs_final_w_in_" + kind))
    grad_x = dxn[None]

    out_g, out_d, out_m, out_v = {}, {}, {}, {}
    for nm in BIG:
        g = jnp.stack([gred[nm, 0], gred[nm, 1]])
        shp = wts[nm].shape
        d2, m2, v2 = _adamw(wts[nm].reshape(-1, shp[-1]), g.reshape(-1, shp[-1]), mom[nm].reshape(-1, shp[-1]),
                            var[nm].reshape(-1, shp[-1]), 512, f"adamw_{nm}")
        out_g[nm], out_d[nm], out_m[nm], out_v[nm] = g, d2.reshape(shp), m2.reshape(shp), v2.reshape(shp)

    rep_rows = [_rep_pack(jnp.stack(gsum[nm]).reshape(wts[nm].shape)) for nm in REP]
    sh_rows = [_sh_pack(jnp.stack(gsum[nm]).reshape(shape), axis) for nm, shape, axis in SH]
    n_rep = sum(p.shape[1] for p in rep_rows)
    packed = _pad_rows(jnp.concatenate(rep_rows + sh_rows, axis=1))
    red = _sum8(_exchange(packed, "rs_small"), 1 << 20, "sum_small")
    n_rep8 = n_rep + (-n_rep) % 8
    rep_all = _all_gather(_pad_rows(red[:n_rep]), "ag_small_grad")
    g_small = {}
    off = 0
    for nm, p in zip(REP, rep_rows):
        r = p.shape[1]
        g_small[nm] = _rep_unpack(rep_all[:, off:off + r, :], wts[nm].shape)
        off += r
    off = n_rep
    for (nm, shape, axis), p in zip(SH, sh_rows):
        r = p.shape[1]
        g_small[nm] = red[off:off + r].reshape(wts[nm].shape)
        off += r

    def rows(a):
        f = a.reshape(-1)
        pad = (-f.shape[0]) % 128
        if pad:
            f = jnp.concatenate([f, jnp.zeros((pad,), a.dtype)])
        return f.reshape(-1, 128)

    small = REP + [nm for nm, _, _ in SH]
    cat = lambda src: _pad_rows(jnp.concatenate([rows(src[nm]) for nm in small], axis=0))
    d2, m2, v2 = _adamw(cat(wts), cat(g_small), cat(mom), cat(var), 1 << 20, "adamw_small")
    off = 0
    for nm in small:
        n = wts[nm].size
        r = (n + 127) // 128
        shp = wts[nm].shape
        take = lambda a: a[off:off + r].reshape(-1)[:n].reshape(shp)
        out_g[nm], out_d[nm], out_m[nm], out_v[nm] = g_small[nm], take(d2), take(m2), take(v2)
        off += r

    return (loss, grad_x, *[out_g[nm] for nm in NAMES], *[out_d[nm] for nm in NAMES],
            *[out_m[nm] for nm in NAMES], *[out_v[nm] for nm in NAMES])
```

```python
import functools

import jax
import jax.numpy as jnp
from jax import lax
from jax.experimental import pallas as pl
from jax.experimental.pallas import tpu as pltpu

F32 = jnp.float32
BF = jnp.bfloat16
MESH = pl.DeviceIdType.MESH
ANY = pl.BlockSpec(memory_space=pl.ANY)

N_DEV = 8
D = 2048
W = 1024
EVEN_IN = 5376
ODD_IN = 4096
CHUNK = 128
ALPHA = (2 * 4) ** 0.25
LN_EPS = 1e-5
ROPE_THETA = 500000.0
LRU_C = 8.0
LR, B1, B2, ADAM_EPS, WD, STEP = 0.001, 0.9, 0.999, 1e-08, 0.01, 10
NEG = -1e30


def _cp(vmem_mb=48):
    return pltpu.CompilerParams(vmem_limit_bytes=vmem_mb * 1024 * 1024)


def _sig(x):
    return jax.nn.sigmoid(x)


def _silu_grad(x):
    s = _sig(x)
    return x * s, s * (1.0 + x * (1.0 - s))


def _dot(a, b):
    return jnp.dot(a, b, preferred_element_type=F32)


def _dot_nt(a, b):
    return lax.dot_general(a, b, (((1,), (1,)), ((), ())), preferred_element_type=F32)


def _dot_tn(a, b):
    return lax.dot_general(a, b, (((0,), (0,)), ((), ())), preferred_element_type=F32)


def _coords():
    return lax.axis_index("x"), lax.axis_index("y"), lax.axis_index("c")


def _chip(j):
    x, y, _ = _coords()
    return (1 - x if j & 2 else x), (1 - y if j & 1 else y)


class _Comm:
    def start(self, ins, outs, sems):
        for cp in self.copies(ins, outs, sems):
            cp.start()

    def wait(self, ins, outs, sems):
        for cp in self.copies(ins, outs, sems):
            cp.wait()


ROWS_US = {"ag1": 0.104, "ag2": 0.052, "rs": 0.205}
ROW_CHUNK = {672: 224, 512: 128, 256: 128}
CARRY_US = {"mm_h_even": 58, "mm_h_odd": 47, "even_fwd": 61, "odd_c_fwd": 37, "mm_out_ln": 33, "ln_bwd": 23, "mm_dmix": 26,
            "mm_dw_out": 25, "even_bwd": 160, "odd_c_bwd": 70, "mm_dw_in_even": 56, "mm_dw_in_odd": 44, "mm_dx_even": 60,
            "mm_dx_odd": 50, "adamw_even_w_in": 30, "adamw_odd_w_in": 28, "adamw_even_w_out": 11, "adamw_odd_w_out": 11}
FWD_OVERBOOK = 1.15
FLUSH_EXTRA_US = 60.0


def _cost_us(task, reg):
    kind, src, _, lo, hi = task
    return ROWS_US[kind] * (hi - lo) * reg[src].shape[-1] * reg[src].dtype.itemsize / 4096.0


class _Copies(_Comm):
    def __init__(self, tasks, reg):
        self.tasks = list(tasks)
        self.out_names, self.in_names = [], []
        for kind, src, dst, lo, hi in self.tasks:
            if dst not in self.out_names:
                self.out_names.append(dst)
        for kind, src, dst, lo, hi in self.tasks:
            if src not in self.out_names and src not in self.in_names:
                self.in_names.append(src)
        self.out_shapes, self.aliases = [], {}
        for o, dst in enumerate(self.out_names):
            if dst in reg:
                self.aliases[len(self.in_names)] = o
                self.in_names.append(dst)
                self.out_shapes.append(jax.ShapeDtypeStruct(reg[dst].shape, reg[dst].dtype))
            else:
                kind, src = next((t[0], t[1]) for t in self.tasks if t[2] == dst)
                shape = (N_DEV,) + reg[src].shape if kind == "ag1" else (3,) + reg[src].shape[1:]
                self.out_shapes.append(jax.ShapeDtypeStruct(shape, reg[src].dtype))
        self.inputs = [reg[nm] for nm in self.in_names]
        n = sum(3 if t[0] == "rs" else 2 for t in self.tasks)
        self.sem_shapes = [pltpu.SemaphoreType.DMA((n,)), pltpu.SemaphoreType.DMA((n,))]

    def copies(self, ins, outs, sems):
        send, recv = sems
        x, y, c = _coords()
        me = 4 * x + 2 * y + c
        xn, yn = (1 - x, y, c), (x, 1 - y, c)
        at_xn, at_yn = 4 * (1 - x) + 2 * y + c, 4 * x + 2 * (1 - y) + c
        ref = dict(zip(self.in_names, ins))
        ref.update(zip(self.out_names, outs))
        res = []

        def copy(src, dst, to):
            i = len(res)
            res.append(pltpu.make_async_remote_copy(src_ref=src, dst_ref=dst, send_sem=send.at[i], recv_sem=recv.at[i],
                                                    device_id=to, device_id_type=MESH))

        for kind, src, dst, lo, hi in self.tasks:
            n = hi - lo
            if kind == "ag1":
                for to in (xn, yn):
                    copy(ref[src].at[pl.ds(lo, n)], ref[dst].at[me, pl.ds(lo, n)], to)
            elif kind == "ag2":
                h = n // 2
                first, second = ref[dst].at[at_xn, pl.ds(lo, h)], ref[dst].at[at_yn, pl.ds(lo + h, n - h)]
                copy(first, first, yn)
                copy(second, second, xn)
            else:
                for j in (1, 2, 3):
                    px, py = _chip(j)
                    copy(ref[src].at[j, pl.ds(lo, n)], ref[dst].at[j - 1, pl.ds(lo, n)], (px, py, c))
        return res


class _Sched:
    def __init__(self, reg):
        self.reg, self.queue, self.later = reg, [], []
        self.overhang = 0.5

    def add(self, tasks):
        self.queue.extend(tasks)

    def pending(self, dst):
        return any(t[2] == dst for t in self.queue + self.later)

    def take(self, budget_us, must=None, overhang=0.5):
        self.queue, self.later = self.later + self.queue, []
        picked, us = [], 0.0
        rest = []
        for t in self.queue:
            cost = _cost_us(t, self.reg)
            if (must is not None and t[2] == must) or us + (1.0 - overhang) * cost <= budget_us:
                picked.append(t)
                us += cost
                if t[0] == "ag1":
                    self.later.append(("ag2", t[2], t[2], t[3], t[4]))
            else:
                rest.append(t)
        self.queue = rest
        return _Copies(picked, self.reg) if picked else None

    def landed(self, comm, got):
        if comm is not None:
            for nm, a in zip(comm.out_names, got):
                self.reg[nm] = a

    def run(self, builder, budget_us, *args, **kw):
        comm = self.take(budget_us, overhang=self.overhang)
        res, got = builder(*args, comm=comm, **kw)
        self.landed(comm, got)
        return res

    def flush(self, dst, budget_us=0.0):
        while self.pending(dst):
            comm = self.take(budget_us, must=dst)
            self.landed(comm, _comm_only(comm, "flush_" + dst))


def _rows(name_src, name_dst, kind, n_rows, chunk):
    return [(kind, name_src, name_dst, lo, min(lo + chunk, n_rows)) for lo in range(0, n_rows, chunk)]


def _pcall(body, *, grid, in_specs, out_specs, out_shape, name, scratch=(), vmem=48, comm=None):
    in_specs, out_specs, out_shape, scratch = list(in_specs), list(out_specs), list(out_shape), list(scratch)
    if comm is None:
        call = pl.pallas_call(body, grid=grid, in_specs=in_specs, out_specs=out_specs, out_shape=out_shape,
                              scratch_shapes=scratch, name=name, compiler_params=_cp(vmem))
        return lambda *args: (call(*args), [])
    n_in, n_out, n_scr = len(in_specs), len(out_specs), len(scratch)
    c_in, c_out = len(comm.inputs), len(comm.out_shapes)
    aliases = {n_in + i: n_out + o for i, o in getattr(comm, "aliases", {}).items()}

    def wrapped(*refs):
        ins, cins = refs[:n_in], refs[n_in:n_in + c_in]
        o0 = n_in + c_in
        outs, couts = refs[o0:o0 + n_out], refs[o0 + n_out:o0 + n_out + c_out]
        s0 = o0 + n_out + c_out
        scr, sems = refs[s0:s0 + n_scr], refs[s0 + n_scr:]
        ids = [pl.program_id(a) for a in range(len(grid))]
        first = functools.reduce(jnp.logical_and, [i == 0 for i in ids])
        last = functools.reduce(jnp.logical_and, [i == g - 1 for i, g in zip(ids, grid)])

        @pl.when(first)
        def _():
            comm.start(cins, couts, sems)

        body(*ins, *outs, *scr)

        @pl.when(last)
        def _():
            comm.wait(cins, couts, sems)

    call = pl.pallas_call(wrapped, grid=grid, in_specs=in_specs + [ANY] * c_in, out_specs=out_specs + [ANY] * c_out,
                          out_shape=out_shape + list(comm.out_shapes), scratch_shapes=scratch + list(comm.sem_shapes),
                          input_output_aliases=aliases, name=name, compiler_params=_cp(vmem))

    def run(*args):
        res = call(*args, *comm.inputs)
        return res[:n_out], res[n_out:]

    return run


def _comm_only(comm, name):
    c_in, c_out = len(comm.inputs), len(comm.out_shapes)

    def body(*refs):
        cins, couts, sems = refs[:c_in], refs[c_in:c_in + c_out], refs[c_in + c_out:]
        comm.start(cins, couts, sems)
        comm.wait(cins, couts, sems)

    return pl.pallas_call(body, in_specs=[ANY] * c_in, out_specs=[ANY] * c_out, out_shape=list(comm.out_shapes),
                          scratch_shapes=list(comm.sem_shapes), input_output_aliases=dict(getattr(comm, "aliases", {})),
                          name=name)(*comm.inputs)


def _ag_d2d(bufs, blocks, name):
    n = len(bufs)

    def body(*refs):
        mine, outs, (send, recv, loc) = refs[n:2 * n], refs[2 * n:3 * n], refs[3 * n:]
        x, y, c = _coords()
        me = 4 * x + 2 * y + c
        cps = []
        for k in range(n):
            cps.append(pltpu.make_async_copy(mine[k], outs[k].at[me], loc.at[k]))
            for j in range(4):
                px, py = _chip(j)
                blk = outs[k].at[4 * px + 2 * py + c]
                cps.append(pltpu.make_async_remote_copy(src_ref=mine[k] if j == 0 else blk, dst_ref=blk, send_sem=send.at[k, j],
                                                        recv_sem=recv.at[k, j], device_id=(x, y, 1 - c), device_id_type=MESH))
        for cp in cps:
            cp.start()
        for cp in cps:
            cp.wait()

    return pl.pallas_call(
        body, in_specs=[ANY] * n + [pl.BlockSpec(memory_space=pltpu.VMEM)] * n, out_specs=[ANY] * n,
        out_shape=[jax.ShapeDtypeStruct(b.shape, b.dtype) for b in bufs], input_output_aliases={k: k for k in range(n)},
        scratch_shapes=[pltpu.SemaphoreType.DMA((n, 4)), pltpu.SemaphoreType.DMA((n, 4)), pltpu.SemaphoreType.DMA((n,))],
        name=name, compiler_params=_cp())(*bufs, *blocks)


def _rs_d2d(parts, name):
    n = len(parts)

    def body(*refs):
        ins, got, (send, recv) = refs[:n], refs[n:2 * n], refs[2 * n:]
        x, y, c = _coords()
        cps = []
        for k in range(n):
            for j in range(4):
                px, py = _chip(j)
                cps.append(pltpu.make_async_remote_copy(
                    src_ref=ins[k].at[4 * px + 2 * py + 1 - c], dst_ref=got[k].at[j], send_sem=send.at[k, j],
                    recv_sem=recv.at[k, j], device_id=(x, y, 1 - c), device_id_type=MESH))
        for cp in cps:
            cp.start()
        for cp in cps:
            cp.wait()

    return pl.pallas_call(
        body, in_specs=[ANY] * n, out_specs=[ANY] * n,
        out_shape=[jax.ShapeDtypeStruct((4,) + p.shape[1:], p.dtype) for p in parts],
        scratch_shapes=[pltpu.SemaphoreType.DMA((n, 4)), pltpu.SemaphoreType.DMA((n, 4))], name=name)(*parts)


def _chip_blocks():
    _, _, c = _coords()
    return jnp.stack([4 * px + 2 * py + c for px, py in map(_chip, range(4))]).astype(jnp.int32)


def _add_pairs(g8, b4, name):
    _, R, C = b4.shape

    def body(idx_ref, a_ref, b_ref, o_ref):
        o_ref[...] = (a_ref[...].astype(F32) + b_ref[...].astype(F32)).astype(BF)

    blk = pl.BlockSpec((None, R, C), lambda j, idx: (j, 0, 0))
    grid_spec = pltpu.PrefetchScalarGridSpec(
        num_scalar_prefetch=1, grid=(4,),
        in_specs=[pl.BlockSpec((None, R, C), lambda j, idx: (idx[j], 0, 0)), blk], out_specs=blk)
    return pl.pallas_call(body, grid_spec=grid_spec, out_shape=jax.ShapeDtypeStruct(b4.shape, BF), name=name,
                          compiler_params=_cp())(_chip_blocks(), g8, b4)


def _rs_final(s4, r3, name):
    _, R, C = s4.shape
    tr = R // 2

    def body(s_ref, r_ref, o_ref):
        o_ref[...] = ((s_ref[...].astype(F32) + r_ref[0].astype(F32)) + r_ref[1].astype(F32)) + r_ref[2].astype(F32)

    return pl.pallas_call(
        body, grid=(2,),
        in_specs=[pl.BlockSpec((None, tr, C), lambda i: (0, i, 0)), pl.BlockSpec((3, tr, C), lambda i: (0, i, 0))],
        out_specs=pl.BlockSpec((tr, C), lambda i: (i, 0)), out_shape=jax.ShapeDtypeStruct((R, C), F32),
        name=name, compiler_params=_cp())(s4, r3)


def _mm_nt(a, w, tm, tn, name, out3=False, comm=None):
    M, K = a.shape
    N = w.shape[0]
    tm = min(tm, M)

    def body(a_ref, w_ref, o_ref):
        o_ref[...] = _dot_nt(a_ref[...], w_ref[...])

    if out3:
        per = W // tn
        out_shape = jax.ShapeDtypeStruct((N // W, M, W), F32)
        out_spec = pl.BlockSpec((None, tm, tn), lambda i, j: (j // per, i, j % per))
    else:
        out_shape = jax.ShapeDtypeStruct((M, N), F32)
        out_spec = pl.BlockSpec((tm, tn), lambda i, j: (i, j))
    (res,), extra = _pcall(
        body, grid=(M // tm, N // tn),
        in_specs=[pl.BlockSpec((tm, K), lambda i, j: (i, 0)), pl.BlockSpec((tn, K), lambda i, j: (j, 0))],
        out_specs=[out_spec], out_shape=[out_shape], name=name, comm=comm)(a, w)
    return res, extra


def _mm_tn(a, b, tm, name, comm=None):
    K, N = b.shape
    if a.ndim == 3:
        M = a.shape[0] * W
        per = W // tm
        a_spec = pl.BlockSpec((None, K, tm), lambda i: (i // per, 0, i % per))
    else:
        M = a.shape[1]
        a_spec = pl.BlockSpec((K, tm), lambda i: (0, i))

    def body(a_ref, b_ref, o_ref):
        o_ref[...] = _dot_tn(a_ref[...], b_ref[...]).astype(BF)

    (out,), extra = _pcall(
        body, grid=(M // tm,),
        in_specs=[a_spec, pl.BlockSpec((K, N), lambda i: (0, 0))],
        out_specs=[pl.BlockSpec((tm, N), lambda i: (i, 0))],
        out_shape=[jax.ShapeDtypeStruct((M, N), BF)], name=name, vmem=56, comm=comm)(a, b)
    return out, extra


def _mm_nn_res(a, w, res, tm, tn, name, comm=None):
    K, N = w.shape
    if a.ndim == 3:
        P, M = a.shape[0], a.shape[1]
        tm = min(tm, M)
        a_spec = pl.BlockSpec((P, tm, W), lambda i, j: (0, i, 0))
    else:
        P, M = 0, a.shape[0]
        tm = min(tm, M)
        a_spec = pl.BlockSpec((tm, K), lambda i, j: (i, 0))

    def body(a_ref, w_ref, r_ref, o_ref):
        if P:
            d = _dot(a_ref[0], w_ref[0:W, :])
            for p in range(1, P):
                d = d + _dot(a_ref[p], w_ref[p * W:(p + 1) * W, :])
        else:
            d = _dot(a_ref[...], w_ref[...])
        o_ref[...] = ALPHA * r_ref[...] + d

    (out,), extra = _pcall(
        body, grid=(M // tm, N // tn),
        in_specs=[a_spec, pl.BlockSpec((K, tn), lambda i, j: (0, j)), pl.BlockSpec((tm, tn), lambda i, j: (i, j))],
        out_specs=[pl.BlockSpec((tm, tn), lambda i, j: (i, j))],
        out_shape=[jax.ShapeDtypeStruct((M, N), F32)], name=name, comm=comm)(a, w, res)
    return out, extra


def _mm_out_ln(mix3, w_out, x, g, b, name, comm=None):
    S = x.shape[0]
    tm = min(256, S)

    def body(m_ref, w_ref, x_ref, g_ref, b_ref, z_ref, xn_ref, xb_ref):
        acc = _dot(m_ref[0], w_ref[0:W, :]) + _dot(m_ref[1], w_ref[W:2 * W, :])
        z = ALPHA * x_ref[...] + acc
        mu = jnp.mean(z, axis=1, keepdims=True)
        zc = z - mu
        var = jnp.mean(zc * zc, axis=1, keepdims=True)
        xn = zc * lax.rsqrt(var + LN_EPS) * g_ref[...] + b_ref[...]
        z_ref[...] = z
        xn_ref[...] = xn
        xb_ref[...] = xn.astype(BF)

    row = pl.BlockSpec((tm, D), lambda i: (i, 0))
    vec = pl.BlockSpec((1, D), lambda i: (0, 0))
    return _pcall(
        body, grid=(S // tm,),
        in_specs=[pl.BlockSpec((2, tm, W), lambda i: (0, i, 0)), pl.BlockSpec((D, D), lambda i: (0, 0)), row, vec, vec],
        out_specs=[row, row, row],
        out_shape=[jax.ShapeDtypeStruct((S, D), F32), jax.ShapeDtypeStruct((S, D), F32), jax.ShapeDtypeStruct((S, D), BF)],
        name=name, comm=comm)(mix3, w_out, x, g.reshape(1, D), b.reshape(1, D))


def _ln_bwd(dxn, z, g, name, comm=None):
    S = z.shape[0]
    tm = min(256, S)

    def body(d_ref, z_ref, g_ref, dz_ref, dzb_ref, dg_ref, db_ref):
        i = pl.program_id(0)
        zz = z_ref[...]
        mu = jnp.mean(zz, axis=1, keepdims=True)
        zc = zz - mu
        var = jnp.mean(zc * zc, axis=1, keepdims=True)
        rstd = lax.rsqrt(var + LN_EPS)
        xhat = zc * rstd
        dy = d_ref[...]
        dyg = dy * g_ref[...]
        m1 = jnp.mean(dyg, axis=1, keepdims=True)
        m2 = jnp.mean(dyg * xhat, axis=1, keepdims=True)
        dz = rstd * (dyg - m1 - xhat * m2)
        dz_ref[...] = dz
        dzb_ref[...] = dz.astype(BF)

        @pl.when(i == 0)
        def _():
            dg_ref[...] = jnp.zeros_like(dg_ref)
            db_ref[...] = jnp.zeros_like(db_ref)

        dg_ref[...] += jnp.sum(dy * xhat, axis=0, keepdims=True)
        db_ref[...] += jnp.sum(dy, axis=0, keepdims=True)

    row = pl.BlockSpec((tm, D), lambda i: (i, 0))
    vec = pl.BlockSpec((1, D), lambda i: (0, 0))
    return _pcall(
        body, grid=(S // tm,), in_specs=[row, row, vec], out_specs=[row, row, vec, vec],
        out_shape=[jax.ShapeDtypeStruct((S, D), F32), jax.ShapeDtypeStruct((S, D), BF),
                   jax.ShapeDtypeStruct((1, D), F32), jax.ShapeDtypeStruct((1, D), F32)],
        name=name, comm=comm)(dxn, z, g.reshape(1, D))


def _loss_grad(xn, target):
    S = xn.shape[0]
    tm = min(256, S)

    def body(x_ref, t_ref, d_ref, p_ref):
        i = pl.program_id(0)
        e = x_ref[...] - t_ref[...]
        d_ref[...] = e * (1.0 / D)

        @pl.when(i == 0)
        def _():
            p_ref[...] = jnp.zeros_like(p_ref)

        p_ref[...] += jnp.sum(jnp.sum(e * e, axis=1, keepdims=True), axis=0, keepdims=True)

    row = pl.BlockSpec((tm, D), lambda i: (i, 0))
    return pl.pallas_call(
        body, grid=(S // tm,), in_specs=[row, row],
        out_specs=[row, pl.BlockSpec((8, 128), lambda i: (0, 0))],
        out_shape=[jax.ShapeDtypeStruct((S, D), F32), jax.ShapeDtypeStruct((8, 128), F32)],
        name="loss_grad", compiler_params=_cp(),
    )(xn, target)


def _rope_fwd(t, r_ref):
    return (t * r_ref[:, 0:128] + pltpu.roll(t, 120, 1) * r_ref[:, 128:256]
            + pltpu.roll(t, 8, 1) * r_ref[:, 256:384])


def _rope_bwd(g, r_ref):
    return (g * r_ref[:, 0:128] + pltpu.roll(g * r_ref[:, 128:256], 8, 1)
            + pltpu.roll(g * r_ref[:, 256:384], 120, 1))


def _dup_heads(kb):
    lo = lax.broadcasted_iota(jnp.int32, kb.shape, 1) < 64
    sw = pltpu.roll(kb, 64, 1)
    return [jnp.where(lo, kb, sw).astype(BF), jnp.where(lo, sw, kb).astype(BF)]


def _even_fwd(h, rope, lng, lnb, ws, bsb, sinks, name, comm=None):
    S = h.shape[0]
    nb = S // CHUNK

    def body(h_ref, hp_ref, rc_ref, rp_ref, lng_ref, lnb_ref, ws_ref, bsb_ref, sink_ref, mix_ref, o_ref, l_ref):
        n = pl.program_id(0)
        lane = lax.broadcasted_iota(jnp.int32, (128, 128), 1)
        rowi = lax.broadcasted_iota(jnp.int32, (128, 128), 0)
        tri = rowi >= lane
        lane_lo = lane < 64
        v = h_ref[:, W:2 * W]
        mu = jnp.mean(v, axis=1, keepdims=True)
        vc = v - mu
        var = jnp.mean(vc * vc, axis=1, keepdims=True)
        vn = vc * lax.rsqrt(var + LN_EPS) * lng_ref[...] + lnb_ref[...]
        for g in range(8):
            sl = slice(g * 128, (g + 1) * 128)
            w = jnp.where(tri, ws_ref[g], 0.0).astype(BF)
            m = _dot(w, vn[:, sl].astype(BF)) + bsb_ref[g]
            ag = h_ref[:, 2 * W + g * 128:2 * W + (g + 1) * 128]
            mix_ref[0, :, sl] = (h_ref[:, sl] * m * (ag * _sig(ag))).astype(BF)
        kb = jnp.concatenate([_rope_fwd(hp_ref[:, 0:128], rp_ref), _rope_fwd(h_ref[:, 4096:4224], rc_ref)], axis=0)
        vb = jnp.concatenate([hp_ref[:, 128:256], h_ref[:, 4224:4352]], axis=0)
        k2 = _dup_heads(kb)
        v2 = _dup_heads(vb)
        qi = lax.broadcasted_iota(jnp.int32, (128, 256), 0)
        kj = lax.broadcasted_iota(jnp.int32, (128, 256), 1)
        diff = qi + 128 - kj
        valid = (diff >= 0) & (diff < 128) & ((n > 0) | (kj >= 128))
        lacc = jnp.zeros((128, 128), F32)
        for j in range(8):
            hk = j // 4
            cs = slice(j * 128, (j + 1) * 128)
            qc = _rope_fwd(h_ref[:, 3072 + j * 128:3072 + (j + 1) * 128], rc_ref)
            ocol = jnp.zeros((128, 128), F32)
            for half in range(2):
                hq = 2 * j + half
                hm = lane_lo if half == 0 else jnp.logical_not(lane_lo)
                qm = jnp.where(hm, qc, 0.0).astype(BF)
                s = jnp.where(valid, _dot_nt(qm, k2[hk]) * 0.125, NEG)
                sk = sink_ref[hq]
                mx = jnp.maximum(jnp.max(s, axis=1, keepdims=True), sk)
                p = jnp.exp(s - mx)
                den = jnp.sum(p, axis=1, keepdims=True) + jnp.exp(sk - mx)
                oh = _dot((p / den).astype(BF), v2[hk])
                ocol = jnp.where(hm, oh, ocol)
                lacc = jnp.where(lane == hq, mx + jnp.log(den), lacc)
            bg = h_ref[:, 4352 + j * 128:4352 + (j + 1) * 128]
            o_ref[:, cs] = ocol
            mix_ref[1, :, cs] = (ocol * (bg * _sig(bg))).astype(BF)
        l_ref[...] = lacc

    prev = lambda n: jnp.maximum(n - 1, 0)
    full = lambda shape: pl.BlockSpec(shape, lambda n: (0,) * len(shape))
    return _pcall(
        body, grid=(nb,),
        in_specs=[pl.BlockSpec((CHUNK, EVEN_IN), lambda n: (n, 0)),
                  pl.BlockSpec((CHUNK, 256), lambda n: (prev(n), 16)),
                  pl.BlockSpec((CHUNK, 384), lambda n: (n, 0)),
                  pl.BlockSpec((CHUNK, 384), lambda n: (prev(n), 0)),
                  full((1, W)), full((1, W)), full((8, 128, 128)), full((8, 128, 128)),
                  pl.BlockSpec(memory_space=pltpu.SMEM)],
        out_specs=[pl.BlockSpec((2, CHUNK, W), lambda n: (0, n, 0)),
                   pl.BlockSpec((CHUNK, W), lambda n: (n, 0)),
                   pl.BlockSpec((CHUNK, 128), lambda n: (n, 0))],
        out_shape=[jax.ShapeDtypeStruct((2, S, W), BF), jax.ShapeDtypeStruct((S, W), F32),
                   jax.ShapeDtypeStruct((S, 128), F32)],
        name=name, comm=comm)(h, h, rope, rope, lng.reshape(1, W), lnb.reshape(1, W), ws, bsb, sinks)


def _even_bwd(h, dmix3, o, l, rope, lng, lnb, ws, wst, bsb, sinks, name, comm=None):
    S = h.shape[0]
    nb = S // CHUNK

    def body(h_ref, hp_ref, hn_ref, dm_ref, dmn_ref, o_ref, on_ref, l_ref, ln_ref, rc_ref, rp_ref, rn_ref,
             lng_ref, lnb_ref, ws_ref, wst_ref, bsb_ref, sink_ref,
             dh_ref, dws_ref, dbs_ref, dlng_ref, dlnb_ref, dsink_ref, dvn_ref):
        n = pl.program_id(0)

        @pl.when(n == 0)
        def _():
            dws_ref[...] = jnp.zeros_like(dws_ref)
            dbs_ref[...] = jnp.zeros_like(dbs_ref)
            dlng_ref[...] = jnp.zeros_like(dlng_ref)
            dlnb_ref[...] = jnp.zeros_like(dlnb_ref)
            dsink_ref[...] = jnp.zeros_like(dsink_ref)

        lane = lax.broadcasted_iota(jnp.int32, (128, 128), 1)
        rowi = lax.broadcasted_iota(jnp.int32, (128, 128), 0)
        lane1 = lax.broadcasted_iota(jnp.int32, (1, 128), 1)
        tri = rowi >= lane
        tri_t = lane >= rowi
        lane_lo = lane < 64
        v = h_ref[:, W:2 * W]
        mu = jnp.mean(v, axis=1, keepdims=True)
        vc = v - mu
        var = jnp.mean(vc * vc, axis=1, keepdims=True)
        rstd = lax.rsqrt(var + LN_EPS)
        vhat = vc * rstd
        vn = vhat * lng_ref[...] + lnb_ref[...]
        dbs_acc = jnp.zeros((128, 128), F32)
        for g in range(8):
            sl = slice(g * 128, (g + 1) * 128)
            w = jnp.where(tri, ws_ref[g], 0.0).astype(BF)
            wt = jnp.where(tri_t, wst_ref[g], 0.0).astype(BF)
            vng = vn[:, sl].astype(BF)
            m = _dot(w, vng) + bsb_ref[g]
            ag = h_ref[:, 2 * W + g * 128:2 * W + (g + 1) * 128]
            sg, dsg = _silu_grad(ag)
            u = h_ref[:, sl]
            da = dm_ref[0, :, sl]
            dmm = da * u * sg
            dh_ref[:, sl] = (da * m * sg).astype(BF)
            dh_ref[:, 2 * W + g * 128:2 * W + (g + 1) * 128] = (da * u * m * dsg).astype(BF)
            dmb = dmm.astype(BF)
            dvn_ref[:, sl] = _dot(wt, dmb)
            dws_ref[g] += jnp.where(tri, _dot_nt(dmb, vng), 0.0)
            dbs_acc = jnp.where(lane == g, jnp.sum(dmm, axis=1, keepdims=True), dbs_acc)
        dbs_ref[...] += dbs_acc
        dvn = dvn_ref[...]
        dlng_ref[...] += jnp.sum(dvn * vhat, axis=0, keepdims=True)
        dlnb_ref[...] += jnp.sum(dvn, axis=0, keepdims=True)
        dyg = dvn * lng_ref[...]
        m1 = jnp.mean(dyg, axis=1, keepdims=True)
        m2 = jnp.mean(dyg * vhat, axis=1, keepdims=True)
        dh_ref[:, W:2 * W] = (rstd * (dyg - m1 - vhat * m2)).astype(BF)
        kcur = _rope_fwd(h_ref[:, 4096:4224], rc_ref)
        kb = jnp.concatenate([_rope_fwd(hp_ref[:, 0:128], rp_ref), kcur], axis=0)
        vb = jnp.concatenate([hp_ref[:, 128:256], h_ref[:, 4224:4352]], axis=0)
        k2 = _dup_heads(kb)
        v2 = _dup_heads(vb)
        kc2 = _dup_heads(kcur)
        vc2 = _dup_heads(h_ref[:, 4224:4352])
        qi = lax.broadcasted_iota(jnp.int32, (128, 256), 0)
        kj = lax.broadcasted_iota(jnp.int32, (128, 256), 1)
        diff = qi + 128 - kj
        valid = (diff >= 0) & (diff < 128) & ((n > 0) | (kj >= 128))
        validn = (lane > rowi) & (n < nb - 1)
        lc = l_ref[...]
        lnx = ln_ref[...]
        dk = [jnp.zeros((128, 128), F32), jnp.zeros((128, 128), F32)]
        dv = [jnp.zeros((128, 128), F32), jnp.zeros((128, 128), F32)]
        dsk_acc = jnp.zeros((1, 128), F32)
        for j in range(8):
            hk = j // 4
            cs = slice(j * 128, (j + 1) * 128)
            qc = _rope_fwd(h_ref[:, 3072 + j * 128:3072 + (j + 1) * 128], rc_ref)
            qn = _rope_fwd(hn_ref[:, 3072 + j * 128:3072 + (j + 1) * 128], rn_ref)
            bg = h_ref[:, 4352 + j * 128:4352 + (j + 1) * 128]
            sgb, dsgb = _silu_grad(bg)
            db = dm_ref[1, :, cs]
            oc = o_ref[:, cs]
            do = db * sgb
            dh_ref[:, 4352 + j * 128:4352 + (j + 1) * 128] = (db * oc * dsgb).astype(BF)
            bgn = hn_ref[:, 4352 + j * 128:4352 + (j + 1) * 128]
            don = dmn_ref[1, :, cs] * (bgn * _sig(bgn))
            prod = do * oc
            prodn = don * on_ref[:, cs]
            dqcol = jnp.zeros((128, 128), F32)
            for half in range(2):
                hq = 2 * j + half
                hm = lane_lo if half == 0 else jnp.logical_not(lane_lo)
                dsum = jnp.sum(jnp.where(hm, prod, 0.0), axis=1, keepdims=True)
                dsumn = jnp.sum(jnp.where(hm, prodn, 0.0), axis=1, keepdims=True)
                lh = jnp.sum(jnp.where(lane == hq, lc, 0.0), axis=1, keepdims=True)
                lhn = jnp.sum(jnp.where(lane == hq, lnx, 0.0), axis=1, keepdims=True)
                qm = jnp.where(hm, qc, 0.0).astype(BF)
                dom = jnp.where(hm, do, 0.0).astype(BF)
                s = _dot_nt(qm, k2[hk]) * 0.125
                p = jnp.exp(jnp.where(valid, s - lh, NEG))
                ds = p * (_dot_nt(dom, v2[hk]) - dsum)
                dqcol = jnp.where(hm, _dot(ds.astype(BF), k2[hk]) * 0.125, dqcol)
                psink = jnp.exp(sink_ref[hq] - lh)
                dsk = -jnp.sum(psink * dsum, axis=0, keepdims=True)
                dsk_acc = jnp.where(lane1 == hq, dsk, dsk_acc)
                dv[hk] = dv[hk] + _dot(jnp.transpose(p[:, 128:256]).astype(BF), dom)
                dk[hk] = dk[hk] + _dot(jnp.transpose(ds[:, 128:256]).astype(BF), qm) * 0.125
                qnm = jnp.where(hm, qn, 0.0).astype(BF)
                donm = jnp.where(hm, don, 0.0).astype(BF)
                sn = _dot_nt(qnm, kc2[hk]) * 0.125
                pn = jnp.exp(jnp.where(validn, sn - lhn, NEG))
                dsn = pn * (_dot_nt(donm, vc2[hk]) - dsumn)
                dv[hk] = dv[hk] + _dot(jnp.transpose(pn).astype(BF), donm)
                dk[hk] = dk[hk] + _dot(jnp.transpose(dsn).astype(BF), qnm) * 0.125
            dh_ref[:, 3072 + j * 128:3072 + (j + 1) * 128] = _rope_bwd(dqcol, rc_ref).astype(BF)
        fold = lambda a: a + pltpu.roll(a, 64, 1)
        dh_ref[:, 4096:4224] = _rope_bwd(jnp.where(lane_lo, fold(dk[0]), fold(dk[1])), rc_ref).astype(BF)
        dh_ref[:, 4224:4352] = jnp.where(lane_lo, fold(dv[0]), fold(dv[1])).astype(BF)
        dsink_ref[...] += dsk_acc

    prev = lambda n: jnp.maximum(n - 1, 0)
    nxt = lambda n: jnp.minimum(n + 1, nb - 1)
    full = lambda shape: pl.BlockSpec(shape, lambda n: (0,) * len(shape))
    return _pcall(
        body, grid=(nb,),
        in_specs=[pl.BlockSpec((CHUNK, EVEN_IN), lambda n: (n, 0)),
                  pl.BlockSpec((CHUNK, 256), lambda n: (prev(n), 16)),
                  pl.BlockSpec((CHUNK, EVEN_IN), lambda n: (nxt(n), 0)),
                  pl.BlockSpec((2, CHUNK, W), lambda n: (0, n, 0)),
                  pl.BlockSpec((2, CHUNK, W), lambda n: (0, nxt(n), 0)),
                  pl.BlockSpec((CHUNK, W), lambda n: (n, 0)),
                  pl.BlockSpec((CHUNK, W), lambda n: (nxt(n), 0)),
                  pl.BlockSpec((CHUNK, 128), lambda n: (n, 0)),
                  pl.BlockSpec((CHUNK, 128), lambda n: (nxt(n), 0)),
                  pl.BlockSpec((CHUNK, 384), lambda n: (n, 0)),
                  pl.BlockSpec((CHUNK, 384), lambda n: (prev(n), 0)),
                  pl.BlockSpec((CHUNK, 384), lambda n: (nxt(n), 0)),
                  full((1, W)), full((1, W)), full((8, 128, 128)), full((8, 128, 128)), full((8, 128, 128)),
                  pl.BlockSpec(memory_space=pltpu.SMEM)],
        out_specs=[pl.BlockSpec((CHUNK, EVEN_IN), lambda n: (n, 0)),
                   full((8, 128, 128)), full((128, 128)), full((1, W)), full((1, W)), full((1, 128))],
        out_shape=[jax.ShapeDtypeStruct((S, EVEN_IN), BF), jax.ShapeDtypeStruct((8, 128, 128), F32),
                   jax.ShapeDtypeStruct((128, 128), F32), jax.ShapeDtypeStruct((1, W), F32),
                   jax.ShapeDtypeStruct((1, W), F32), jax.ShapeDtypeStruct((1, 128), F32)],
        scratch=[pltpu.VMEM((CHUNK, W), F32)], name=name, comm=comm,
    )(h, h, h, dmix3, dmix3, o, o, l, l, rope, rope, rope, lng.reshape(1, W), lnb.reshape(1, W), ws, wst, bsb, sinks)


def _expm1(x):
    ser = x * (1.0 + x * (0.5 + x * (1.0 / 6.0 + x * (1.0 / 24.0))))
    return jnp.where(jnp.abs(x) < 1e-2, ser, jnp.exp(x) - 1.0)


def _softplus_neg(lam):
    z = -lam
    e = jnp.exp(-jnp.abs(z))
    l1p = jnp.where(e < 1e-3, e * (1.0 - e * (0.5 - e * (1.0 / 3.0))), jnp.log(1.0 + e))
    return jnp.maximum(z, 0.0) + l1p


def _shift_down(x, k, row, fill=0.0):
    return jnp.where(row >= k, pltpu.roll(x, k, 0), fill)


def _shift_up(x, k, row, fill=0.0):
    S = x.shape[0]
    return jnp.where(row < S - k, pltpu.roll(x, S - k, 0), fill)


def _lru_gates(xc, row, cw_ref, cb_ref, wa_ref, wx_ref, ba_ref, bx_ref, lam_ref):
    xconv = (cw_ref[3:4, :] * xc + cw_ref[2:3, :] * _shift_down(xc, 1, row) + cw_ref[1:2, :] * _shift_down(xc, 2, row)
             + cw_ref[0:1, :] * _shift_down(xc, 3, row) + cb_ref[...])
    xb = xconv.astype(BF)
    r = _sig(_dot(xb, wa_ref[...]) + ba_ref[...])
    i = _sig(_dot(xb, wx_ref[...]) + bx_ref[...])
    sp = _softplus_neg(lam_ref[...])
    log_a = -LRU_C * r * sp
    a = jnp.exp(log_a)
    mult = jnp.sqrt(-_expm1(2.0 * log_a))
    return xconv, r, i, sp, a, mult


def _odd_c_fwd(h, cw, cb, wa, wx, ba, bx, lam, name, comm=None):
    S = h.shape[0]

    def body(xc_ref, cg_ref, cw_ref, cb_ref, wa_ref, wx_ref, ba_ref, bx_ref, lam_ref, mix_ref, hst_ref):
        row = lax.broadcasted_iota(jnp.int32, (S, 128), 0)
        xconv, r, i, sp, a, mult = _lru_gates(xc_ref[...], row, cw_ref, cb_ref, wa_ref, wx_ref, ba_ref, bx_ref, lam_ref)
        aa = a
        bb = mult * (i * xconv)
        k = 1
        while k < S:
            bb = aa * _shift_down(bb, k, row) + bb
            if 2 * k < S:
                aa = aa * _shift_down(aa, k, row, 1.0)
            k *= 2
        hst_ref[...] = bb
        cg = cg_ref[...]
        mix_ref[...] = (bb * (cg * _sig(cg))).astype(BF)

    col = lambda off: pl.BlockSpec((S, 128), lambda j: (0, off + j))
    vec = pl.BlockSpec((1, 128), lambda j: (0, j))
    mat = pl.BlockSpec((None, 128, 128), lambda j: (j, 0, 0))
    return _pcall(
        body, grid=(8,),
        in_specs=[col(0), col(8), pl.BlockSpec((4, 128), lambda j: (0, j)), vec, mat, mat, vec, vec, vec],
        out_specs=[pl.BlockSpec((None, S, 128), lambda j: (0, 0, j)), pl.BlockSpec((S, 128), lambda j: (0, j))],
        out_shape=[jax.ShapeDtypeStruct((2, S, W), BF), jax.ShapeDtypeStruct((S, W), F32)],
        name=name, comm=comm,
    )(h, h, cw, cb.reshape(1, W), wa, wx, ba.reshape(1, W), bx.reshape(1, W), lam.reshape(1, W))


def _pool_sums(x, g, row, shift):
    s2 = x + shift(x, 1, row)
    s4 = s2 + shift(s2, 2, row)
    s8 = s4 + shift(s4, 4, row)
    s16 = s8 + shift(s8, 8, row)
    return jnp.where(g == 0, s2, jnp.where(g == 1, s4, jnp.where(g == 2, s8, s16)))


def _odd_d_fwd(h, mix3, wp, dscale, name):
    S = h.shape[0]

    def body(xd_ref, dg_ref, wp_ref, ds_ref, mix_in, mix_ref):
        g = pl.program_id(0)
        row = lax.broadcasted_iota(jnp.int32, (S, 256), 0)
        xd = xd_ref[...]
        cnt = jnp.minimum(row + 1, jnp.left_shift(2, g)).astype(F32)
        pooled = _pool_sums(xd, g, row, _shift_down) / cnt - xd
        mixed = _dot(pooled.astype(BF), wp_ref[...])
        dg = dg_ref[...]
        mix_ref[...] = (mixed * ds_ref[...] * (dg * _sig(dg))).astype(BF)

    col = lambda off: pl.BlockSpec((S, 256), lambda g: (0, off + g))
    return pl.pallas_call(
        body, grid=(4,),
        in_specs=[col(8), col(12), pl.BlockSpec((None, 256, 256), lambda g: (g, 0, 0)),
                  pl.BlockSpec((1, 256), lambda g: (0, g)), ANY],
        out_specs=pl.BlockSpec((None, S, 256), lambda g: (1, 0, g)),
        out_shape=jax.ShapeDtypeStruct((2, S, W), BF), input_output_aliases={4: 0},
        name=name, compiler_params=_cp(),
    )(h, h, wp, dscale.reshape(1, W), mix3)


def _odd_c_bwd(h, hst, dmix3, cw, cb, wa, wx, wat, wxt, ba, bx, lam, name, comm=None):
    S = h.shape[0]

    def body(xc_ref, cg_ref, hst_ref, dc_ref, cw_ref, cb_ref, wa_ref, wx_ref, wat_ref, wxt_ref, ba_ref, bx_ref, lam_ref,
             dh_ref, dcw_ref, dcb_ref, dwa_ref, dwx_ref, dba_ref, dbx_ref, dlam_ref):
        row = lax.broadcasted_iota(jnp.int32, (S, 128), 0)
        xc = xc_ref[...]
        xconv, r, i, sp, a, mult = _lru_gates(xc, row, cw_ref, cb_ref, wa_ref, wx_ref, ba_ref, bx_ref, lam_ref)
        hst = hst_ref[...]
        cg = cg_ref[...]
        sg, dsg = _silu_grad(cg)
        dc = dc_ref[...]
        dh_ref[1] = (dc * hst * dsg).astype(BF)
        aa = _shift_up(a, 1, row)
        bb = dc * sg
        k = 1
        while k < S:
            bb = aa * _shift_up(bb, k, row) + bb
            if 2 * k < S:
                aa = aa * _shift_up(aa, k, row, 1.0)
            k *= 2
        lam_t = bb
        da = lam_t * _shift_down(hst, 1, row)
        ix = i * xconv
        dmult = lam_t * ix
        di = lam_t * mult * xconv
        dxconv = lam_t * mult * i
        dlog_a = da * a - dmult * (a * a / mult)
        dr = dlog_a * (-LRU_C * sp)
        dsp = jnp.sum(dlog_a * (-LRU_C * r), axis=0, keepdims=True)
        dlam_ref[...] = dsp * (-_sig(-lam_ref[...]))
        dpa = dr * r * (1.0 - r)
        dpx = di * i * (1.0 - i)
        dpab = dpa.astype(BF)
        dpxb = dpx.astype(BF)
        xb = xconv.astype(BF)
        dxconv = dxconv + _dot(dpab, wat_ref[...]) + _dot(dpxb, wxt_ref[...])
        dwa_ref[...] = _dot_tn(xb, dpab)
        dwx_ref[...] = _dot_tn(xb, dpxb)
        dba_ref[...] = jnp.sum(dpa, axis=0, keepdims=True)
        dbx_ref[...] = jnp.sum(dpx, axis=0, keepdims=True)
        dh_ref[0] = (cw_ref[3:4, :] * dxconv + cw_ref[2:3, :] * _shift_up(dxconv, 1, row)
                     + cw_ref[1:2, :] * _shift_up(dxconv, 2, row) + cw_ref[0:1, :] * _shift_up(dxconv, 3, row)).astype(BF)
        for j in range(4):
            src = xc if j == 3 else _shift_down(xc, 3 - j, row)
            dcw_ref[j:j + 1, :] = jnp.sum(dxconv * src, axis=0, keepdims=True)
        dcb_ref[...] = jnp.sum(dxconv, axis=0, keepdims=True)

    col = lambda off: pl.BlockSpec((S, 128), lambda j: (0, off + j))
    vec = pl.BlockSpec((1, 128), lambda j: (0, j))
    mat = pl.BlockSpec((None, 128, 128), lambda j: (j, 0, 0))
    vshape = jax.ShapeDtypeStruct((1, W), F32)
    mshape = jax.ShapeDtypeStruct((8, 128, 128), F32)
    return _pcall(
        body, grid=(8,),
        in_specs=[col(0), col(8), col(0), pl.BlockSpec((None, S, 128), lambda j: (0, 0, j)),
                  pl.BlockSpec((4, 128), lambda j: (0, j)), vec, mat, mat, mat, mat, vec, vec, vec],
        out_specs=[pl.BlockSpec((2, S, 128), lambda j: (0, 0, j)), pl.BlockSpec((4, 128), lambda j: (0, j)), vec,
                   mat, mat, vec, vec, vec],
        out_shape=[jax.ShapeDtypeStruct((4, S, W), BF), jax.ShapeDtypeStruct((4, W), F32), vshape, mshape, mshape,
                   vshape, vshape, vshape],
        name=name, vmem=56, comm=comm,
    )(h, h, hst, dmix3, cw, cb.reshape(1, W), wa, wx, wat, wxt, ba.reshape(1, W), bx.reshape(1, W), lam.reshape(1, W))


def _odd_d_bwd(h, dmix3, dh4, wp, wpt, dscale, name):
    S = h.shape[0]

    def body(xd_ref, dg_ref, dd_ref, wp_ref, wpt_ref, ds_ref, dh_in, dh_ref, dwp_ref, dds_ref):
        g = pl.program_id(0)
        row = lax.broadcasted_iota(jnp.int32, (S, 256), 0)
        xd = xd_ref[...]
        cnt = jnp.minimum(row + 1, jnp.left_shift(2, g)).astype(F32)
        pooled = _pool_sums(xd, g, row, _shift_down) / cnt - xd
        pb = pooled.astype(BF)
        mixed = _dot(pb, wp_ref[...])
        dg = dg_ref[...]
        sg, dsg = _silu_grad(dg)
        dd = dd_ref[...]
        dmixed = dd * ds_ref[...] * sg
        dds_ref[...] = jnp.sum(dd * mixed * sg, axis=0, keepdims=True)
        dh_ref[1] = (dd * mixed * ds_ref[...] * dsg).astype(BF)
        dmb = dmixed.astype(BF)
        dpooled = _dot(dmb, wpt_ref[...])
        dwp_ref[...] = _dot_tn(pb, dmb)
        dh_ref[0] = (_pool_sums(dpooled / cnt, g, row, _shift_up) - dpooled).astype(BF)

    col = lambda off: pl.BlockSpec((S, 256), lambda g: (0, off + g))
    mat = pl.BlockSpec((None, 256, 256), lambda g: (g, 0, 0))
    vec = pl.BlockSpec((1, 256), lambda g: (0, g))
    return pl.pallas_call(
        body, grid=(4,),
        in_specs=[col(8), col(12), pl.BlockSpec((None, S, 256), lambda g: (1, 0, g)), mat, mat, vec, ANY],
        out_specs=[pl.BlockSpec((2, S, 256), lambda g: (1, 0, g)), mat, vec],
        out_shape=[jax.ShapeDtypeStruct((4, S, W), BF), jax.ShapeDtypeStruct((4, 256, 256), F32),
                   jax.ShapeDtypeStruct((1, W), F32)],
        input_output_aliases={6: 0}, name=name, compiler_params=_cp(56),
    )(h, h, dmix3, wp, wpt, dscale.reshape(1, W), dh4)


def _peer(d):
    x, y, c = lax.axis_index("x"), lax.axis_index("y"), lax.axis_index("c")
    px = 1 - x if d & 4 else x
    py = 1 - y if d & 2 else y
    pc = 1 - c if d & 1 else c
    return (px, py, pc), 4 * px + 2 * py + pc


class _GatherAll(_Comm):
    def __init__(self, xs):
        self.inputs = [xs]
        self.out_shapes = [jax.ShapeDtypeStruct((N_DEV,) + xs.shape, xs.dtype)]
        self.sem_shapes = [pltpu.SemaphoreType.DMA((N_DEV - 1,)), pltpu.SemaphoreType.DMA((N_DEV - 1,)),
                           pltpu.SemaphoreType.DMA]

    def copies(self, ins, outs, sems):
        (x_ref,), (out_ref,), (send, recv, loc) = ins, outs, sems
        _, me = _peer(0)
        res = [pltpu.make_async_copy(x_ref, out_ref.at[me], loc)]
        for d in range(1, N_DEV):
            peer, _ = _peer(d)
            res.append(pltpu.make_async_remote_copy(src_ref=x_ref, dst_ref=out_ref.at[me], send_sem=send.at[d - 1],
                                                    recv_sem=recv.at[d - 1], device_id=peer, device_id_type=MESH))
        return res


class _ExchangeAll(_Comm):
    def __init__(self, g8):
        self.inputs = [g8]
        self.out_shapes = [jax.ShapeDtypeStruct(g8.shape, g8.dtype)]
        self.sem_shapes = [pltpu.SemaphoreType.DMA((N_DEV - 1,)), pltpu.SemaphoreType.DMA((N_DEV - 1,)),
                           pltpu.SemaphoreType.DMA]

    def copies(self, ins, outs, sems):
        (g_ref,), (out_ref,), (send, recv, loc) = ins, outs, sems
        _, me = _peer(0)
        res = [pltpu.make_async_copy(g_ref.at[me], out_ref.at[0], loc)]
        for d in range(1, N_DEV):
            peer, pidx = _peer(d)
            res.append(pltpu.make_async_remote_copy(src_ref=g_ref.at[pidx], dst_ref=out_ref.at[d], send_sem=send.at[d - 1],
                                                    recv_sem=recv.at[d - 1], device_id=peer, device_id_type=MESH))
        return res


def _sum8(r8, tr, name):
    _, R, C = r8.shape
    tr = min(tr, R)
    assert R % tr == 0

    def body(r_ref, o_ref):
        acc = r_ref[0]
        for d in range(1, N_DEV):
            acc = acc + r_ref[d]
        o_ref[...] = acc

    return pl.pallas_call(
        body, grid=(R // tr,), in_specs=[pl.BlockSpec((N_DEV, tr, C), lambda i: (0, i, 0))],
        out_specs=pl.BlockSpec((tr, C), lambda i: (i, 0)), out_shape=jax.ShapeDtypeStruct((R, C), F32),
        name=name, compiler_params=_cp(),
    )(r8)


def _adamw(w, g, m, v, tr, name, comm=None):
    R, C = w.shape
    tr = min(tr, R)

    def body(w_ref, g_ref, m_ref, v_ref, d_ref, m2_ref, v2_ref):
        gg = g_ref[...]
        m2 = B1 * m_ref[...] + (1.0 - B1) * gg
        v2 = B2 * v_ref[...] + (1.0 - B2) * (gg * gg)
        m_hat = m2 / (1.0 - B1 ** STEP)
        v_hat = v2 / (1.0 - B2 ** STEP)
        d_ref[...] = -LR * (m_hat / (jnp.sqrt(v_hat) + ADAM_EPS) + WD * w_ref[...])
        m2_ref[...] = m2
        v2_ref[...] = v2

    blk = pl.BlockSpec((tr, C), lambda i: (i, 0))
    shp = jax.ShapeDtypeStruct((R, C), F32)
    return _pcall(body, grid=(R // tr,), in_specs=[blk] * 4, out_specs=[blk] * 3, out_shape=[shp] * 3,
                  name=name, comm=comm)(w, g, m, v)


def _rep_pack(a):
    n = a.size
    pad = (-n) % 1024
    f = a.reshape(-1)
    if pad:
        f = jnp.concatenate([f, jnp.zeros((pad,), a.dtype)])
    return f.reshape(N_DEV, -1, 128)


def _rep_unpack(p, shape):
    n = 1
    for s in shape:
        n *= s
    return p.reshape(-1)[:n].reshape(shape)


def _sh_pack(a, axis):
    shp = a.shape
    a = a.reshape(shp[:axis] + (N_DEV, shp[axis] // N_DEV) + shp[axis + 1:])
    return jnp.moveaxis(a, axis, 0).reshape(N_DEV, -1, 128)


def _sh_unpack(p, shape, axis):
    a = p.reshape((N_DEV,) + shape[:axis] + (shape[axis] // N_DEV,) + shape[axis + 1:])
    return jnp.moveaxis(a, 0, axis).reshape(shape)


def _pad_rows(a, mult=8):
    pad = (-a.shape[-2]) % mult
    if pad:
        a = jnp.concatenate([a, jnp.zeros(a.shape[:-2] + (pad, a.shape[-1]), a.dtype)], axis=-2)
    return a


REP = ["even_a_ln_g", "even_a_ln_b", "even_a_ws", "even_a_bs", "even_b_sinks", "even_ln_g", "even_ln_b",
       "odd_w_a", "odd_w_x"]
SH = [("odd_conv_w", (2, 4, W), 2), ("odd_conv_b", (2, W), 1), ("odd_b_a", (2, W), 1), ("odd_b_x", (2, W), 1),
      ("odd_lam", (2, W), 1), ("odd_w_pool", (2, 4, 256, 256), 2), ("odd_d_scale", (2, W), 1),
      ("odd_ln_g", (2, D), 1), ("odd_ln_b", (2, D), 1)]
BIG = ["even_w_in", "even_w_out", "odd_w_in", "odd_w_out"]
NAMES = ["even_w_in", "even_a_ln_g", "even_a_ln_b", "even_a_ws", "even_a_bs", "even_b_sinks", "even_w_out",
         "even_ln_g", "even_ln_b", "odd_w_in", "odd_conv_w", "odd_conv_b", "odd_w_a", "odd_b_a", "odd_w_x", "odd_b_x",
         "odd_lam", "odd_w_pool", "odd_d_scale", "odd_w_out", "odd_ln_g", "odd_ln_b"]


def _rope_table(positions):
    S = positions.shape[0]
    inv = ROPE_THETA ** (-jnp.arange(0, 16, 2, dtype=F32) / 16)
    ang = positions.astype(F32)[:, None] * inv
    cos, sin = jnp.cos(ang), jnp.sin(ang)
    one, zero = jnp.ones((S, 48), F32), jnp.zeros((S, 48), F32)
    z8 = jnp.zeros((S, 8), F32)
    c64 = jnp.concatenate([cos, cos, one], axis=1)
    s1 = jnp.concatenate([-sin, z8, zero], axis=1)
    s2 = jnp.concatenate([z8, sin, zero], axis=1)
    return jnp.concatenate([c64, c64, s1, s1, s2, s2], axis=1)


def kernel(x, positions, even_w_in, even_a_ln_g, even_a_ln_b, even_a_ws, even_a_bs, even_b_sinks, even_w_out, even_ln_g, even_ln_b, odd_w_in, odd_conv_w, odd_conv_b, odd_w_a, odd_b_a, odd_w_x, odd_b_x, odd_lam, odd_w_pool, odd_d_scale, odd_w_out, odd_ln_g, odd_ln_b, loss_target, m_even_w_in, m_even_a_ln_g, m_even_a_ln_b, m_even_a_ws, m_even_a_bs, m_even_b_sinks, m_even_w_out, m_even_ln_g, m_even_ln_b, m_odd_w_in, m_odd_conv_w, m_odd_conv_b, m_odd_w_a, m_odd_b_a, m_odd_w_x, m_odd_b_x, m_odd_lam, m_odd_w_pool, m_odd_d_scale, m_odd_w_out, m_odd_ln_g, m_odd_ln_b, v_even_w_in, v_even_a_ln_g, v_even_a_ln_b, v_even_a_ws, v_even_a_bs, v_even_b_sinks, v_even_w_out, v_even_ln_g, v_even_ln_b, v_odd_w_in, v_odd_conv_w, v_odd_conv_b, v_odd_w_a, v_odd_b_a, v_odd_w_x, v_odd_b_x, v_odd_lam, v_odd_w_pool, v_odd_d_scale, v_odd_w_out, v_odd_ln_g, v_odd_ln_b):
    args = (even_w_in, even_a_ln_g, even_a_ln_b, even_a_ws, even_a_bs, even_b_sinks, even_w_out, even_ln_g, even_ln_b,
            odd_w_in, odd_conv_w, odd_conv_b, odd_w_a, odd_b_a, odd_w_x, odd_b_x, odd_lam, odd_w_pool, odd_d_scale,
            odd_w_out, odd_ln_g, odd_ln_b)
    margs = (m_even_w_in, m_even_a_ln_g, m_even_a_ln_b, m_even_a_ws, m_even_a_bs, m_even_b_sinks, m_even_w_out,
             m_even_ln_g, m_even_ln_b, m_odd_w_in, m_odd_conv_w, m_odd_conv_b, m_odd_w_a, m_odd_b_a, m_odd_w_x,
             m_odd_b_x, m_odd_lam, m_odd_w_pool, m_odd_d_scale, m_odd_w_out, m_odd_ln_g, m_odd_ln_b)
    vargs = (v_even_w_in, v_even_a_ln_g, v_even_a_ln_b, v_even_a_ws, v_even_a_bs, v_even_b_sinks, v_even_w_out,
             v_even_ln_g, v_even_ln_b, v_odd_w_in, v_odd_conv_w, v_odd_conv_b, v_odd_w_a, v_odd_b_a, v_odd_w_x,
             v_odd_b_x, v_odd_lam, v_odd_w_pool, v_odd_d_scale, v_odd_w_out, v_odd_ln_g, v_odd_ln_b)
    wts = dict(zip(NAMES, args))
    mom = dict(zip(NAMES, margs))
    var = dict(zip(NAMES, vargs))
    S = x.shape[1]
    x0 = x[0]
    rope = _rope_table(positions[0])

    kinds = ("even", "odd", "even", "odd")
    blk_in = [jnp.transpose(wts[kinds[l] + "_w_in"][l // 2]).astype(BF) for l in range(4)]
    blk_out = [wts[kinds[l] + "_w_out"][l // 2].astype(BF) for l in range(4)]
    sh_local = _pad_rows(jnp.concatenate([wts[nm].reshape(-1, 128) for nm, _, _ in SH], axis=0), 16)
    reg = {"blk_small": sh_local}
    sched = _Sched(reg)
    for l in range(4):
        reg[f"blk_in{l}"], reg[f"blk_out{l}"] = blk_in[l], blk_out[l]
    sched.add(_rows("blk_in0", "w_in0", "ag1", blk_in[0].shape[0], ROW_CHUNK[blk_in[0].shape[0]]))
    sched.add(_rows("blk_small", "w_small", "ag1", sh_local.shape[0], sh_local.shape[0]))
    for l in range(4):
        sched.add(_rows(f"blk_out{l}", f"w_out{l}", "ag1", D // N_DEV, ROW_CHUNK[D // N_DEV]))
        if l < 3:
            r = blk_in[l + 1].shape[0]
            sched.add(_rows(f"blk_in{l + 1}", f"w_in{l + 1}", "ag1", r, ROW_CHUNK[r]))

    def gathered(dst, blk):
        sched.flush(dst, FLUSH_EXTRA_US)
        (buf,) = _ag_d2d([reg.pop(dst)], [blk], "ag_d2d_" + dst)
        return buf

    wt_in0 = gathered("w_in0", blk_in[0]).reshape(-1, D)
    full = {nm: wts[nm] for nm in REP}

    def gather_small():
        sh_all = gathered("w_small", sh_local)
        off = 0
        for nm, shape, axis in SH:
            r = wts[nm].size // 128
            full[nm] = _sh_unpack(sh_all[:, off:off + r, :], shape, axis)
            off += r

    saved = []
    wt_in, w_out = [wt_in0, None, None, None], [None] * 4
    xf, xb = x0, x0.astype(BF)
    fwd = lambda name: FWD_OVERBOOK * CARRY_US[name]
    for layer in range(4):
        j = layer // 2
        kind = kinds[layer]
        if wt_in[layer] is None:
            wt_in[layer] = gathered(f"w_in{layer}", blk_in[layer]).reshape(-1, D)
        h = sched.run(_mm_nt, fwd("mm_h_" + kind), xb, wt_in[layer], 1024, 768 if kind == "even" else 512, "mm_h_" + kind)
        if kind == "even":
            bsb = jnp.broadcast_to(full["even_a_bs"][j][:, :, None], (8, 128, 128))
            mix3, o, l = sched.run(_even_fwd, fwd("even_fwd"), h, rope, full["even_a_ln_g"][j], full["even_a_ln_b"][j],
                                   full["even_a_ws"][j], bsb, full["even_b_sinks"][j], "even_fwd")
            extra = (o, l, bsb)
        else:
            if "odd_lam" not in full:
                gather_small()
            wa, wx = full["odd_w_a"][j].astype(BF), full["odd_w_x"][j].astype(BF)
            wp = full["odd_w_pool"][j].astype(BF)
            mix3, hst = sched.run(_odd_c_fwd, fwd("odd_c_fwd"), h, full["odd_conv_w"][j], full["odd_conv_b"][j], wa, wx,
                                  full["odd_b_a"][j], full["odd_b_x"][j], full["odd_lam"][j], "odd_c_fwd")
            mix3 = _odd_d_fwd(h, mix3, wp, full["odd_d_scale"][j], "odd_d_fwd")
            extra = (hst, wa, wx, wp)
        w_out[layer] = gathered(f"w_out{layer}", blk_out[layer]).reshape(D, D)
        z, xn, xnb = sched.run(_mm_out_ln, fwd("mm_out_ln"), mix3, w_out[layer], xf, full[kind + "_ln_g"][j],
                               full[kind + "_ln_b"][j], "mm_out_ln")
        saved.append((xb, h, mix3, z, extra))
        xf, xb = xn, xnb

    dxn, part = _loss_grad(xf, loss_target[0])
    loss = lax.psum(part[0, 0] * (0.5 / D), ("x", "y", "c"))

    gsum = {nm: [None, None] for nm in NAMES}

    chip_sums = {}
    sched.overhang = 0.15

    def chip_sum(g, tag, key):
        r = g.shape[0] // N_DEV
        g8 = g.reshape(N_DEV, r, D)
        (got,) = _rs_d2d([g8], "rs_d2d_" + tag)
        chip_sums[key] = reg["s_" + key] = _add_pairs(g8, got, "rs_add_" + tag)
        sched.add(_rows("s_" + key, "r_" + key, "rs", r, ROW_CHUNK[r] // 2))

    def reduced(key, name):
        sched.flush("r_" + key, FLUSH_EXTRA_US)
        return _rs_final(chip_sums[key], reg.pop("r_" + key), name)

    for layer in (3, 2, 1, 0):
        j = layer // 2
        xb, h, mix3, z, extra = saved[layer]
        kind = kinds[layer]
        dz, dzb, dg, dbeta = sched.run(_ln_bwd, CARRY_US["ln_bwd"], dxn, z, full[kind + "_ln_g"][j], "ln_bwd")
        gsum[kind + "_ln_g"][j] = dg.reshape(D)
        gsum[kind + "_ln_b"][j] = dbeta.reshape(D)
        dmix3 = sched.run(_mm_nt, CARRY_US["mm_dmix"], dzb, w_out[layer], 1024, 512, "mm_dmix", out3=True)
        chip_sum(sched.run(_mm_tn, CARRY_US["mm_dw_out"], mix3, dzb, 512, "mm_dw_out"), "w_out", f"out{layer}")
        if kind == "even":
            o, l, bsb = extra
            ws = full["even_a_ws"][j]
            dh, dws, dbs, dlng, dlnb, dsink = sched.run(
                _even_bwd, CARRY_US["even_bwd"], h, dmix3, o, l, rope, full["even_a_ln_g"][j], full["even_a_ln_b"][j],
                ws, jnp.swapaxes(ws, 1, 2), bsb, full["even_b_sinks"][j], "even_bwd")
            gsum["even_a_ws"][j] = dws
            gsum["even_a_bs"][j] = jnp.transpose(dbs[:, :8])
            gsum["even_a_ln_g"][j] = dlng.reshape(W)
            gsum["even_a_ln_b"][j] = dlnb.reshape(W)
            gsum["even_b_sinks"][j] = dsink[0, :16]
            if layer == 0:
                rep_rows = [_rep_pack(jnp.stack(gsum[nm]).reshape(wts[nm].shape)) for nm in REP]
                sh_rows = [_sh_pack(jnp.stack(gsum[nm]).reshape(shape), axis) for nm, shape, axis in SH]
                packed = _pad_rows(jnp.concatenate(rep_rows + sh_rows, axis=1))
                gw, (small8,) = _mm_tn(dh, xb, 384, "mm_dw_in_even", comm=_ExchangeAll(packed))
            else:
                gw = sched.run(_mm_tn, CARRY_US["mm_dw_in_even"], dh, xb, 384, "mm_dw_in_even")
            chip_sum(gw, "w_in_even", f"in{layer}")
            dxn = sched.run(_mm_nn_res, CARRY_US["mm_dx_even"], dh, wt_in[layer], dz, 512, 512, "mm_dx_even")
        else:
            hst, wa, wx, wp = extra
            dh4, dcw, dcb, dwa, dwx, dba, dbx, dlam = sched.run(
                _odd_c_bwd, CARRY_US["odd_c_bwd"], h, hst, dmix3, full["odd_conv_w"][j], full["odd_conv_b"][j], wa, wx,
                jnp.swapaxes(wa, 1, 2), jnp.swapaxes(wx, 1, 2), full["odd_b_a"][j], full["odd_b_x"][j], full["odd_lam"][j],
                "odd_c_bwd")
            dh4, dwp, dds = _odd_d_bwd(h, dmix3, dh4, wp, jnp.swapaxes(wp, 1, 2), full["odd_d_scale"][j], "odd_d_bwd")
            gsum["odd_conv_w"][j], gsum["odd_conv_b"][j] = dcw, dcb.reshape(W)
            gsum["odd_w_a"][j], gsum["odd_w_x"][j] = dwa, dwx
            gsum["odd_b_a"][j], gsum["odd_b_x"][j], gsum["odd_lam"][j] = dba.reshape(W), dbx.reshape(W), dlam.reshape(W)
            gsum["odd_w_pool"][j], gsum["odd_d_scale"][j] = dwp, dds.reshape(W)
            chip_sum(sched.run(_mm_tn, CARRY_US["mm_dw_in_odd"], dh4, xb, 512, "mm_dw_in_odd"), "w_in_odd", f"in{layer}")
            dxn = sched.run(_mm_nn_res, CARRY_US["mm_dx_odd"], dh4, wt_in[layer], dz, 512, 512, "mm_dx_odd")
    grad_x = dxn[None]

    n_rep = sum(p.shape[1] for p in rep_rows)
    red = _sum8(small8, 1 << 20, "sum_small")
    out_g, out_d, out_m, out_v = {}, {}, {}, {}
    for nm, kind, what, layers in (("odd_w_out", "odd", "out", (1, 3)), ("even_w_out", "even", "out", (0, 2)),
                                   ("odd_w_in", "odd", "in", (1, 3)), ("even_w_in", "even", "in", (0, 2))):
        gl = [reduced(f"{what}{l}", f"rs_final_w_{what}_{kind}") for l in layers]
        g = jnp.stack([jnp.transpose(a) for a in gl] if what == "in" else gl)
        shp = wts[nm].shape
        operands = (wts[nm].reshape(-1, shp[-1]), g.reshape(-1, shp[-1]), mom[nm].reshape(-1, shp[-1]),
                    var[nm].reshape(-1, shp[-1]), 512, f"adamw_{nm}")
        if nm == "even_w_in":
            (d2, m2, v2), (rep_all,) = _adamw(*operands, comm=_GatherAll(_pad_rows(red[:n_rep])))
        else:
            d2, m2, v2 = sched.run(_adamw, CARRY_US["adamw_" + nm], *operands)
        out_g[nm], out_d[nm], out_m[nm], out_v[nm] = g, d2.reshape(shp), m2.reshape(shp), v2.reshape(shp)

    g_small = {}
    off = 0
    for nm, p in zip(REP, rep_rows):
        r = p.shape[1]
        g_small[nm] = _rep_unpack(rep_all[:, off:off + r, :], wts[nm].shape)
        off += r
    off = n_rep
    for (nm, shape, axis), p in zip(SH, sh_rows):
        r = p.shape[1]
        g_small[nm] = red[off:off + r].reshape(wts[nm].shape)
        off += r

    def rows(a):
        f = a.reshape(-1)
        pad = (-f.shape[0]) % 128
        if pad:
            f = jnp.concatenate([f, jnp.zeros((pad,), a.dtype)])
        return f.reshape(-1, 128)

    small = REP + [nm for nm, _, _ in SH]
    cat = lambda src: _pad_rows(jnp.concatenate([rows(src[nm]) for nm in small], axis=0))
    (d2, m2, v2), _ = _adamw(cat(wts), cat(g_small), cat(mom), cat(var), 1 << 20, "adamw_small")
    off = 0
    for nm in small:
        n = wts[nm].size
        r = (n + 127) // 128
        shp = wts[nm].shape
        take = lambda a: a[off:off + r].reshape(-1)[:n].reshape(shp)
        out_g[nm], out_d[nm], out_m[nm], out_v[nm] = g_small[nm], take(d2), take(m2), take(v2)
        off += r

    return (loss, grad_x, *[out_g[nm] for nm in NAMES], *[out_d[nm] for nm in NAMES],
            *[out_m[nm] for nm in NAMES], *[out_v[nm] for nm in NAMES])
```

```python
import functools

import jax
import jax.numpy as jnp
from jax import lax
from jax.experimental import pallas as pl
from jax.experimental.pallas import tpu as pltpu

F32 = jnp.float32
BF = jnp.bfloat16
MESH = pl.DeviceIdType.MESH
ANY = pl.BlockSpec(memory_space=pl.ANY)

N_DEV = 8
D = 2048
W = 1024
EVEN_IN = 5376
ODD_IN = 4096
CHUNK = 128
ALPHA = (2 * 4) ** 0.25
LN_EPS = 1e-5
ROPE_THETA = 500000.0
LRU_C = 8.0
LR, B1, B2, ADAM_EPS, WD, STEP = 0.001, 0.9, 0.999, 1e-08, 0.01, 10
NEG = -1e30


def _cp(vmem_mb=48):
    return pltpu.CompilerParams(vmem_limit_bytes=vmem_mb * 1024 * 1024)


def _sig(x):
    return jax.nn.sigmoid(x)


def _silu_grad(x):
    s = _sig(x)
    return x * s, s * (1.0 + x * (1.0 - s))


def _dot(a, b):
    return jnp.dot(a, b, preferred_element_type=F32)


def _dot_nt(a, b):
    return lax.dot_general(a, b, (((1,), (1,)), ((), ())), preferred_element_type=F32)


def _dot_tn(a, b):
    return lax.dot_general(a, b, (((0,), (0,)), ((), ())), preferred_element_type=F32)


def _coords():
    return lax.axis_index("x"), lax.axis_index("y"), lax.axis_index("c")


def _chip(j):
    x, y, _ = _coords()
    return (1 - x if j & 2 else x), (1 - y if j & 1 else y)


class _Comm:
    def start(self, ins, outs, sems):
        for cp in self.copies(ins, outs, sems):
            cp.start()

    def wait(self, ins, outs, sems):
        for cp in self.copies(ins, outs, sems):
            cp.wait()


ROWS_US = {"ag1": 0.104, "ag2": 0.052, "agd": 0.027, "rsd": 0.027, "rs": 0.205}
N_COPIES = {"ag1": 2, "ag2": 2, "agd": 4, "rsd": 4, "rs": 3}
ROW_CHUNK = {672: 224, 512: 128, 256: 128}
CARRY_US = {"mm_h_even": 58, "mm_h_odd": 47, "even_fwd": 61, "odd_c_fwd": 37, "mm_out_ln": 33, "ln_bwd": 23, "mm_dmix": 26,
            "mm_dw_out": 25, "even_bwd": 160, "odd_c_bwd": 70, "mm_dw_in_even": 56, "mm_dw_in_odd": 44, "mm_dx_even": 60,
            "mm_dx_odd": 50, "adamw_even_w_in": 30, "adamw_odd_w_in": 28, "adamw_even_w_out": 11, "adamw_odd_w_out": 11}
FWD_OVERBOOK = 1.15
FLUSH_EXTRA_US = 60.0


def _cost_us(task, reg):
    kind, src, _, lo, hi = task
    return ROWS_US[kind] * (hi - lo) * reg[src].shape[-1] * reg[src].dtype.itemsize / 4096.0


class _Copies(_Comm):
    def __init__(self, tasks, reg):
        self.tasks = list(tasks)
        self.out_names, self.in_names = [], []
        for kind, src, dst, lo, hi in self.tasks:
            if dst not in self.out_names:
                self.out_names.append(dst)
        for kind, src, dst, lo, hi in self.tasks:
            if src not in self.out_names and src not in self.in_names:
                self.in_names.append(src)
        self.out_shapes, self.aliases = [], {}
        for o, dst in enumerate(self.out_names):
            if dst in reg:
                self.aliases[len(self.in_names)] = o
                self.in_names.append(dst)
                self.out_shapes.append(jax.ShapeDtypeStruct(reg[dst].shape, reg[dst].dtype))
            else:
                kind, src = next((t[0], t[1]) for t in self.tasks if t[2] == dst)
                shape = ({"rsd": 4, "rs": 3}[kind],) + reg[src].shape[1:]
                self.out_shapes.append(jax.ShapeDtypeStruct(shape, reg[src].dtype))
        self.inputs = [reg[nm] for nm in self.in_names]
        n = sum(N_COPIES[t[0]] for t in self.tasks)
        self.sem_shapes = [pltpu.SemaphoreType.DMA((n,)), pltpu.SemaphoreType.DMA((n,))]

    def copies(self, ins, outs, sems):
        send, recv = sems
        x, y, c = _coords()
        me = 4 * x + 2 * y + c
        xn, yn = (1 - x, y, c), (x, 1 - y, c)
        at_xn, at_yn = 4 * (1 - x) + 2 * y + c, 4 * x + 2 * (1 - y) + c
        ref = dict(zip(self.in_names, ins))
        ref.update(zip(self.out_names, outs))
        res = []

        def copy(src, dst, to):
            i = len(res)
            res.append(pltpu.make_async_remote_copy(src_ref=src, dst_ref=dst, send_sem=send.at[i], recv_sem=recv.at[i],
                                                    device_id=to, device_id_type=MESH))

        for kind, src, dst, lo, hi in self.tasks:
            n = hi - lo
            if kind == "ag1":
                for to in (xn, yn):
                    copy(ref[src].at[pl.ds(lo, n)], ref[dst].at[me, pl.ds(lo, n)], to)
            elif kind == "ag2":
                h = n // 2
                first, second = ref[dst].at[at_xn, pl.ds(lo, h)], ref[dst].at[at_yn, pl.ds(lo + h, n - h)]
                copy(first, first, yn)
                copy(second, second, xn)
            elif kind == "agd":
                for j in range(4):
                    px, py = _chip(j)
                    rows = ref[dst].at[4 * px + 2 * py + c, pl.ds(lo, n)]
                    copy(rows, rows, (x, y, 1 - c))
            elif kind == "rsd":
                for j in range(4):
                    px, py = _chip(j)
                    copy(ref[src].at[4 * px + 2 * py + 1 - c, pl.ds(lo, n)], ref[dst].at[j, pl.ds(lo, n)], (x, y, 1 - c))
            else:
                for j in (1, 2, 3):
                    px, py = _chip(j)
                    copy(ref[src].at[j, pl.ds(lo, n)], ref[dst].at[j - 1, pl.ds(lo, n)], (px, py, c))
        return res


class _Sched:
    def __init__(self, reg):
        self.reg, self.queue, self.later = reg, [], []
        self.overhang = 0.5
        self.after_landing = None

    def add(self, tasks, first=False):
        self.queue = list(tasks) + self.queue if first else self.queue + list(tasks)

    def pending(self, dst):
        return any(t[2] == dst for t in self.queue + self.later)

    def take(self, budget_us, must=None, overhang=0.5):
        self.queue, self.later = self.later + self.queue, []
        picked, us = [], 0.0
        rest = []
        for t in self.queue:
            cost = _cost_us(t, self.reg)
            if (must is not None and t[2] == must) or us + (1.0 - overhang) * cost <= budget_us:
                picked.append(t)
                us += cost
                if t[0] in ("ag1", "ag2"):
                    self.later.append(({"ag1": "ag2", "ag2": "agd"}[t[0]], t[2], t[2], t[3], t[4]))
            else:
                rest.append(t)
        self.queue = rest
        return _Copies(picked, self.reg) if picked else None

    def landed(self, comm, got):
        if comm is not None:
            for nm, a in zip(comm.out_names, got):
                self.reg[nm] = a
        if self.after_landing is not None:
            self.after_landing()

    def run(self, builder, budget_us, *args, **kw):
        comm = self.take(budget_us, overhang=self.overhang)
        res, got = builder(*args, comm=comm, **kw)
        self.landed(comm, got)
        return res

    def flush(self, dst, budget_us=0.0):
        while self.pending(dst):
            comm = self.take(budget_us, must=dst)
            self.landed(comm, _comm_only(comm, "flush_" + dst))


def _rows(name_src, name_dst, kind, n_rows, chunk):
    return [(kind, name_src, name_dst, lo, min(lo + chunk, n_rows)) for lo in range(0, n_rows, chunk)]


def _pcall(body, *, grid, in_specs, out_specs, out_shape, name, scratch=(), vmem=48, comm=None):
    in_specs, out_specs, out_shape, scratch = list(in_specs), list(out_specs), list(out_shape), list(scratch)
    if comm is None:
        call = pl.pallas_call(body, grid=grid, in_specs=in_specs, out_specs=out_specs, out_shape=out_shape,
                              scratch_shapes=scratch, name=name, compiler_params=_cp(vmem))
        return lambda *args: (call(*args), [])
    n_in, n_out, n_scr = len(in_specs), len(out_specs), len(scratch)
    c_in, c_out = len(comm.inputs), len(comm.out_shapes)
    aliases = {n_in + i: n_out + o for i, o in getattr(comm, "aliases", {}).items()}

    def wrapped(*refs):
        ins, cins = refs[:n_in], refs[n_in:n_in + c_in]
        o0 = n_in + c_in
        outs, couts = refs[o0:o0 + n_out], refs[o0 + n_out:o0 + n_out + c_out]
        s0 = o0 + n_out + c_out
        scr, sems = refs[s0:s0 + n_scr], refs[s0 + n_scr:]
        ids = [pl.program_id(a) for a in range(len(grid))]
        first = functools.reduce(jnp.logical_and, [i == 0 for i in ids])
        last = functools.reduce(jnp.logical_and, [i == g - 1 for i, g in zip(ids, grid)])

        @pl.when(first)
        def _():
            comm.start(cins, couts, sems)

        body(*ins, *outs, *scr)

        @pl.when(last)
        def _():
            comm.wait(cins, couts, sems)

    call = pl.pallas_call(wrapped, grid=grid, in_specs=in_specs + [ANY] * c_in, out_specs=out_specs + [ANY] * c_out,
                          out_shape=out_shape + list(comm.out_shapes), scratch_shapes=scratch + list(comm.sem_shapes),
                          input_output_aliases=aliases, name=name, compiler_params=_cp(vmem))

    def run(*args):
        res = call(*args, *comm.inputs)
        return res[:n_out], res[n_out:]

    return run


def _comm_only(comm, name):
    c_in, c_out = len(comm.inputs), len(comm.out_shapes)

    def body(*refs):
        cins, couts, sems = refs[:c_in], refs[c_in:c_in + c_out], refs[c_in + c_out:]
        comm.start(cins, couts, sems)
        comm.wait(cins, couts, sems)

    return pl.pallas_call(body, in_specs=[ANY] * c_in, out_specs=[ANY] * c_out, out_shape=list(comm.out_shapes),
                          scratch_shapes=list(comm.sem_shapes), input_output_aliases=dict(getattr(comm, "aliases", {})),
                          name=name)(*comm.inputs)


def _chip_blocks():
    _, _, c = _coords()
    return jnp.stack([4 * px + 2 * py + c for px, py in map(_chip, range(4))]).astype(jnp.int32)


def _add_pairs(g8, b4, name):
    _, R, C = b4.shape

    def body(idx_ref, a_ref, b_ref, o_ref):
        o_ref[...] = (a_ref[...].astype(F32) + b_ref[...].astype(F32)).astype(BF)

    blk = pl.BlockSpec((None, R, C), lambda j, idx: (j, 0, 0))
    grid_spec = pltpu.PrefetchScalarGridSpec(
        num_scalar_prefetch=1, grid=(4,),
        in_specs=[pl.BlockSpec((None, R, C), lambda j, idx: (idx[j], 0, 0)), blk], out_specs=blk)
    return pl.pallas_call(body, grid_spec=grid_spec, out_shape=jax.ShapeDtypeStruct(b4.shape, BF), name=name,
                          compiler_params=_cp())(_chip_blocks(), g8, b4)


def _rs_final(s4, r3, name):
    _, R, C = s4.shape
    tr = R // 2

    def body(s_ref, r_ref, o_ref):
        o_ref[...] = ((s_ref[...].astype(F32) + r_ref[0].astype(F32)) + r_ref[1].astype(F32)) + r_ref[2].astype(F32)

    return pl.pallas_call(
        body, grid=(2,),
        in_specs=[pl.BlockSpec((None, tr, C), lambda i: (0, i, 0)), pl.BlockSpec((3, tr, C), lambda i: (0, i, 0))],
        out_specs=pl.BlockSpec((tr, C), lambda i: (i, 0)), out_shape=jax.ShapeDtypeStruct((R, C), F32),
        name=name, compiler_params=_cp())(s4, r3)


def _mm_nt(a, w, tm, tn, name, out3=False, comm=None):
    M, K = a.shape
    N = w.shape[0]
    tm = min(tm, M)

    def body(a_ref, w_ref, o_ref):
        o_ref[...] = _dot_nt(a_ref[...], w_ref[...])

    if out3:
        per = W // tn
        out_shape = jax.ShapeDtypeStruct((N // W, M, W), F32)
        out_spec = pl.BlockSpec((None, tm, tn), lambda i, j: (j // per, i, j % per))
    else:
        out_shape = jax.ShapeDtypeStruct((M, N), F32)
        out_spec = pl.BlockSpec((tm, tn), lambda i, j: (i, j))
    (res,), extra = _pcall(
        body, grid=(M // tm, N // tn),
        in_specs=[pl.BlockSpec((tm, K), lambda i, j: (i, 0)), pl.BlockSpec((tn, K), lambda i, j: (j, 0))],
        out_specs=[out_spec], out_shape=[out_shape], name=name, comm=comm)(a, w)
    return res, extra


def _mm_tn(a, b, tm, name, comm=None):
    K, N = b.shape
    if a.ndim == 3:
        M = a.shape[0] * W
        per = W // tm
        a_spec = pl.BlockSpec((None, K, tm), lambda i: (i // per, 0, i % per))
    else:
        M = a.shape[1]
        a_spec = pl.BlockSpec((K, tm), lambda i: (0, i))

    def body(a_ref, b_ref, o_ref):
        o_ref[...] = _dot_tn(a_ref[...], b_ref[...]).astype(BF)

    (out,), extra = _pcall(
        body, grid=(M // tm,),
        in_specs=[a_spec, pl.BlockSpec((K, N), lambda i: (0, 0))],
        out_specs=[pl.BlockSpec((tm, N), lambda i: (i, 0))],
        out_shape=[jax.ShapeDtypeStruct((M, N), BF)], name=name, vmem=56, comm=comm)(a, b)
    return out, extra


def _mm_nn_res(a, w, res, tm, tn, name, comm=None):
    K, N = w.shape
    if a.ndim == 3:
        P, M = a.shape[0], a.shape[1]
        tm = min(tm, M)
        a_spec = pl.BlockSpec((P, tm, W), lambda i, j: (0, i, 0))
    else:
        P, M = 0, a.shape[0]
        tm = min(tm, M)
        a_spec = pl.BlockSpec((tm, K), lambda i, j: (i, 0))

    def body(a_ref, w_ref, r_ref, o_ref):
        if P:
            d = _dot(a_ref[0], w_ref[0:W, :])
            for p in range(1, P):
                d = d + _dot(a_ref[p], w_ref[p * W:(p + 1) * W, :])
        else:
            d = _dot(a_ref[...], w_ref[...])
        o_ref[...] = ALPHA * r_ref[...] + d

    (out,), extra = _pcall(
        body, grid=(M // tm, N // tn),
        in_specs=[a_spec, pl.BlockSpec((K, tn), lambda i, j: (0, j)), pl.BlockSpec((tm, tn), lambda i, j: (i, j))],
        out_specs=[pl.BlockSpec((tm, tn), lambda i, j: (i, j))],
        out_shape=[jax.ShapeDtypeStruct((M, N), F32)], name=name, comm=comm)(a, w, res)
    return out, extra


def _mm_out_ln(mix3, w_out, x, g, b, name, comm=None):
    S = x.shape[0]
    tm = min(256, S)

    def body(m_ref, w_ref, x_ref, g_ref, b_ref, z_ref, xn_ref, xb_ref):
        acc = _dot(m_ref[0], w_ref[0:W, :]) + _dot(m_ref[1], w_ref[W:2 * W, :])
        z = ALPHA * x_ref[...] + acc
        mu = jnp.mean(z, axis=1, keepdims=True)
        zc = z - mu
        var = jnp.mean(zc * zc, axis=1, keepdims=True)
        xn = zc * lax.rsqrt(var + LN_EPS) * g_ref[...] + b_ref[...]
        z_ref[...] = z
        xn_ref[...] = xn
        xb_ref[...] = xn.astype(BF)

    row = pl.BlockSpec((tm, D), lambda i: (i, 0))
    vec = pl.BlockSpec((1, D), lambda i: (0, 0))
    return _pcall(
        body, grid=(S // tm,),
        in_specs=[pl.BlockSpec((2, tm, W), lambda i: (0, i, 0)), pl.BlockSpec((D, D), lambda i: (0, 0)), row, vec, vec],
        out_specs=[row, row, row],
        out_shape=[jax.ShapeDtypeStruct((S, D), F32), jax.ShapeDtypeStruct((S, D), F32), jax.ShapeDtypeStruct((S, D), BF)],
        name=name, comm=comm)(mix3, w_out, x, g.reshape(1, D), b.reshape(1, D))


def _ln_bwd(dxn, z, g, name, comm=None):
    S = z.shape[0]
    tm = min(256, S)

    def body(d_ref, z_ref, g_ref, dz_ref, dzb_ref, dg_ref, db_ref):
        i = pl.program_id(0)
        zz = z_ref[...]
        mu = jnp.mean(zz, axis=1, keepdims=True)
        zc = zz - mu
        var = jnp.mean(zc * zc, axis=1, keepdims=True)
        rstd = lax.rsqrt(var + LN_EPS)
        xhat = zc * rstd
        dy = d_ref[...]
        dyg = dy * g_ref[...]
        m1 = jnp.mean(dyg, axis=1, keepdims=True)
        m2 = jnp.mean(dyg * xhat, axis=1, keepdims=True)
        dz = rstd * (dyg - m1 - xhat * m2)
        dz_ref[...] = dz
        dzb_ref[...] = dz.astype(BF)

        @pl.when(i == 0)
        def _():
            dg_ref[...] = jnp.zeros_like(dg_ref)
            db_ref[...] = jnp.zeros_like(db_ref)

        dg_ref[...] += jnp.sum(dy * xhat, axis=0, keepdims=True)
        db_ref[...] += jnp.sum(dy, axis=0, keepdims=True)

    row = pl.BlockSpec((tm, D), lambda i: (i, 0))
    vec = pl.BlockSpec((1, D), lambda i: (0, 0))
    return _pcall(
        body, grid=(S // tm,), in_specs=[row, row, vec], out_specs=[row, row, vec, vec],
        out_shape=[jax.ShapeDtypeStruct((S, D), F32), jax.ShapeDtypeStruct((S, D), BF),
                   jax.ShapeDtypeStruct((1, D), F32), jax.ShapeDtypeStruct((1, D), F32)],
        name=name, comm=comm)(dxn, z, g.reshape(1, D))


def _loss_grad(xn, target):
    S = xn.shape[0]
    tm = min(256, S)

    def body(x_ref, t_ref, d_ref, p_ref):
        i = pl.program_id(0)
        e = x_ref[...] - t_ref[...]
        d_ref[...] = e * (1.0 / D)

        @pl.when(i == 0)
        def _():
            p_ref[...] = jnp.zeros_like(p_ref)

        p_ref[...] += jnp.sum(jnp.sum(e * e, axis=1, keepdims=True), axis=0, keepdims=True)

    row = pl.BlockSpec((tm, D), lambda i: (i, 0))
    return pl.pallas_call(
        body, grid=(S // tm,), in_specs=[row, row],
        out_specs=[row, pl.BlockSpec((8, 128), lambda i: (0, 0))],
        out_shape=[jax.ShapeDtypeStruct((S, D), F32), jax.ShapeDtypeStruct((8, 128), F32)],
        name="loss_grad", compiler_params=_cp(),
    )(xn, target)


def _rope_fwd(t, r_ref):
    return (t * r_ref[:, 0:128] + pltpu.roll(t, 120, 1) * r_ref[:, 128:256]
            + pltpu.roll(t, 8, 1) * r_ref[:, 256:384])


def _rope_bwd(g, r_ref):
    return (g * r_ref[:, 0:128] + pltpu.roll(g * r_ref[:, 128:256], 8, 1)
            + pltpu.roll(g * r_ref[:, 256:384], 120, 1))


def _dup_heads(kb):
    lo = lax.broadcasted_iota(jnp.int32, kb.shape, 1) < 64
    sw = pltpu.roll(kb, 64, 1)
    return [jnp.where(lo, kb, sw).astype(BF), jnp.where(lo, sw, kb).astype(BF)]


def _even_fwd(h, rope, lng, lnb, ws, bsb, sinks, name, comm=None):
    S = h.shape[0]
    nb = S // CHUNK

    def body(h_ref, hp_ref, rc_ref, rp_ref, lng_ref, lnb_ref, ws_ref, bsb_ref, sink_ref, mix_ref, o_ref, l_ref):
        n = pl.program_id(0)
        lane = lax.broadcasted_iota(jnp.int32, (128, 128), 1)
        rowi = lax.broadcasted_iota(jnp.int32, (128, 128), 0)
        tri = rowi >= lane
        lane_lo = lane < 64
        v = h_ref[:, W:2 * W]
        mu = jnp.mean(v, axis=1, keepdims=True)
        vc = v - mu
        var = jnp.mean(vc * vc, axis=1, keepdims=True)
        vn = vc * lax.rsqrt(var + LN_EPS) * lng_ref[...] + lnb_ref[...]
        for g in range(8):
            sl = slice(g * 128, (g + 1) * 128)
            w = jnp.where(tri, ws_ref[g], 0.0).astype(BF)
            m = _dot(w, vn[:, sl].astype(BF)) + bsb_ref[g]
            ag = h_ref[:, 2 * W + g * 128:2 * W + (g + 1) * 128]
            mix_ref[0, :, sl] = (h_ref[:, sl] * m * (ag * _sig(ag))).astype(BF)
        kb = jnp.concatenate([_rope_fwd(hp_ref[:, 0:128], rp_ref), _rope_fwd(h_ref[:, 4096:4224], rc_ref)], axis=0)
        vb = jnp.concatenate([hp_ref[:, 128:256], h_ref[:, 4224:4352]], axis=0)
        k2 = _dup_heads(kb)
        v2 = _dup_heads(vb)
        qi = lax.broadcasted_iota(jnp.int32, (128, 256), 0)
        kj = lax.broadcasted_iota(jnp.int32, (128, 256), 1)
        diff = qi + 128 - kj
        valid = (diff >= 0) & (diff < 128) & ((n > 0) | (kj >= 128))
        lacc = jnp.zeros((128, 128), F32)
        for j in range(8):
            hk = j // 4
            cs = slice(j * 128, (j + 1) * 128)
            qc = _rope_fwd(h_ref[:, 3072 + j * 128:3072 + (j + 1) * 128], rc_ref)
            ocol = jnp.zeros((128, 128), F32)
            for half in range(2):
                hq = 2 * j + half
                hm = lane_lo if half == 0 else jnp.logical_not(lane_lo)
                qm = jnp.where(hm, qc, 0.0).astype(BF)
                s = jnp.where(valid, _dot_nt(qm, k2[hk]) * 0.125, NEG)
                sk = sink_ref[hq]
                mx = jnp.maximum(jnp.max(s, axis=1, keepdims=True), sk)
                p = jnp.exp(s - mx)
                den = jnp.sum(p, axis=1, keepdims=True) + jnp.exp(sk - mx)
                oh = _dot((p / den).astype(BF), v2[hk])
                ocol = jnp.where(hm, oh, ocol)
                lacc = jnp.where(lane == hq, mx + jnp.log(den), lacc)
            bg = h_ref[:, 4352 + j * 128:4352 + (j + 1) * 128]
            o_ref[:, cs] = ocol
            mix_ref[1, :, cs] = (ocol * (bg * _sig(bg))).astype(BF)
        l_ref[...] = lacc

    prev = lambda n: jnp.maximum(n - 1, 0)
    full = lambda shape: pl.BlockSpec(shape, lambda n: (0,) * len(shape))
    return _pcall(
        body, grid=(nb,),
        in_specs=[pl.BlockSpec((CHUNK, EVEN_IN), lambda n: (n, 0)),
                  pl.BlockSpec((CHUNK, 256), lambda n: (prev(n), 16)),
                  pl.BlockSpec((CHUNK, 384), lambda n: (n, 0)),
                  pl.BlockSpec((CHUNK, 384), lambda n: (prev(n), 0)),
                  full((1, W)), full((1, W)), full((8, 128, 128)), full((8, 128, 128)),
                  pl.BlockSpec(memory_space=pltpu.SMEM)],
        out_specs=[pl.BlockSpec((2, CHUNK, W), lambda n: (0, n, 0)),
                   pl.BlockSpec((CHUNK, W), lambda n: (n, 0)),
                   pl.BlockSpec((CHUNK, 128), lambda n: (n, 0))],
        out_shape=[jax.ShapeDtypeStruct((2, S, W), BF), jax.ShapeDtypeStruct((S, W), F32),
                   jax.ShapeDtypeStruct((S, 128), F32)],
        name=name, comm=comm)(h, h, rope, rope, lng.reshape(1, W), lnb.reshape(1, W), ws, bsb, sinks)


def _even_bwd(h, dmix3, o, l, rope, lng, lnb, ws, wst, bsb, sinks, name, comm=None):
    S = h.shape[0]
    nb = S // CHUNK

    def body(h_ref, hp_ref, hn_ref, dm_ref, dmn_ref, o_ref, on_ref, l_ref, ln_ref, rc_ref, rp_ref, rn_ref,
             lng_ref, lnb_ref, ws_ref, wst_ref, bsb_ref, sink_ref,
             dh_ref, dws_ref, dbs_ref, dlng_ref, dlnb_ref, dsink_ref, dvn_ref):
        n = pl.program_id(0)

        @pl.when(n == 0)
        def _():
            dws_ref[...] = jnp.zeros_like(dws_ref)
            dbs_ref[...] = jnp.zeros_like(dbs_ref)
            dlng_ref[...] = jnp.zeros_like(dlng_ref)
            dlnb_ref[...] = jnp.zeros_like(dlnb_ref)
            dsink_ref[...] = jnp.zeros_like(dsink_ref)

        lane = lax.broadcasted_iota(jnp.int32, (128, 128), 1)
        rowi = lax.broadcasted_iota(jnp.int32, (128, 128), 0)
        lane1 = lax.broadcasted_iota(jnp.int32, (1, 128), 1)
        tri = rowi >= lane
        tri_t = lane >= rowi
        lane_lo = lane < 64
        v = h_ref[:, W:2 * W]
        mu = jnp.mean(v, axis=1, keepdims=True)
        vc = v - mu
        var = jnp.mean(vc * vc, axis=1, keepdims=True)
        rstd = lax.rsqrt(var + LN_EPS)
        vhat = vc * rstd
        vn = vhat * lng_ref[...] + lnb_ref[...]
        dbs_acc = jnp.zeros((128, 128), F32)
        for g in range(8):
            sl = slice(g * 128, (g + 1) * 128)
            w = jnp.where(tri, ws_ref[g], 0.0).astype(BF)
            wt = jnp.where(tri_t, wst_ref[g], 0.0).astype(BF)
            vng = vn[:, sl].astype(BF)
            m = _dot(w, vng) + bsb_ref[g]
            ag = h_ref[:, 2 * W + g * 128:2 * W + (g + 1) * 128]
            sg, dsg = _silu_grad(ag)
            u = h_ref[:, sl]
            da = dm_ref[0, :, sl]
            dmm = da * u * sg
            dh_ref[:, sl] = (da * m * sg).astype(BF)
            dh_ref[:, 2 * W + g * 128:2 * W + (g + 1) * 128] = (da * u * m * dsg).astype(BF)
            dmb = dmm.astype(BF)
            dvn_ref[:, sl] = _dot(wt, dmb)
            dws_ref[g] += jnp.where(tri, _dot_nt(dmb, vng), 0.0)
            dbs_acc = jnp.where(lane == g, jnp.sum(dmm, axis=1, keepdims=True), dbs_acc)
        dbs_ref[...] += dbs_acc
        dvn = dvn_ref[...]
        dlng_ref[...] += jnp.sum(dvn * vhat, axis=0, keepdims=True)
        dlnb_ref[...] += jnp.sum(dvn, axis=0, keepdims=True)
        dyg = dvn * lng_ref[...]
        m1 = jnp.mean(dyg, axis=1, keepdims=True)
        m2 = jnp.mean(dyg * vhat, axis=1, keepdims=True)
        dh_ref[:, W:2 * W] = (rstd * (dyg - m1 - vhat * m2)).astype(BF)
        kcur = _rope_fwd(h_ref[:, 4096:4224], rc_ref)
        kb = jnp.concatenate([_rope_fwd(hp_ref[:, 0:128], rp_ref), kcur], axis=0)
        vb = jnp.concatenate([hp_ref[:, 128:256], h_ref[:, 4224:4352]], axis=0)
        k2 = _dup_heads(kb)
        v2 = _dup_heads(vb)
        kc2 = _dup_heads(kcur)
        vc2 = _dup_heads(h_ref[:, 4224:4352])
        qi = lax.broadcasted_iota(jnp.int32, (128, 256), 0)
        kj = lax.broadcasted_iota(jnp.int32, (128, 256), 1)
        diff = qi + 128 - kj
        valid = (diff >= 0) & (diff < 128) & ((n > 0) | (kj >= 128))
        validn = (lane > rowi) & (n < nb - 1)
        lc = l_ref[...]
        lnx = ln_ref[...]
        dk = [jnp.zeros((128, 128), F32), jnp.zeros((128, 128), F32)]
        dv = [jnp.zeros((128, 128), F32), jnp.zeros((128, 128), F32)]
        dsk_acc = jnp.zeros((1, 128), F32)
        for j in range(8):
            hk = j // 4
            cs = slice(j * 128, (j + 1) * 128)
            qc = _rope_fwd(h_ref[:, 3072 + j * 128:3072 + (j + 1) * 128], rc_ref)
            qn = _rope_fwd(hn_ref[:, 3072 + j * 128:3072 + (j + 1) * 128], rn_ref)
            bg = h_ref[:, 4352 + j * 128:4352 + (j + 1) * 128]
            sgb, dsgb = _silu_grad(bg)
            db = dm_ref[1, :, cs]
            oc = o_ref[:, cs]
            do = db * sgb
            dh_ref[:, 4352 + j * 128:4352 + (j + 1) * 128] = (db * oc * dsgb).astype(BF)
            bgn = hn_ref[:, 4352 + j * 128:4352 + (j + 1) * 128]
            don = dmn_ref[1, :, cs] * (bgn * _sig(bgn))
            prod = do * oc
            prodn = don * on_ref[:, cs]
            dqcol = jnp.zeros((128, 128), F32)
            for half in range(2):
                hq = 2 * j + half
                hm = lane_lo if half == 0 else jnp.logical_not(lane_lo)
                dsum = jnp.sum(jnp.where(hm, prod, 0.0), axis=1, keepdims=True)
                dsumn = jnp.sum(jnp.where(hm, prodn, 0.0), axis=1, keepdims=True)
                lh = jnp.sum(jnp.where(lane == hq, lc, 0.0), axis=1, keepdims=True)
                lhn = jnp.sum(jnp.where(lane == hq, lnx, 0.0), axis=1, keepdims=True)
                qm = jnp.where(hm, qc, 0.0).astype(BF)
                dom = jnp.where(hm, do, 0.0).astype(BF)
                s = _dot_nt(qm, k2[hk]) * 0.125
                p = jnp.exp(jnp.where(valid, s - lh, NEG))
                ds = p * (_dot_nt(dom, v2[hk]) - dsum)
                dqcol = jnp.where(hm, _dot(ds.astype(BF), k2[hk]) * 0.125, dqcol)
                psink = jnp.exp(sink_ref[hq] - lh)
                dsk = -jnp.sum(psink * dsum, axis=0, keepdims=True)
                dsk_acc = jnp.where(lane1 == hq, dsk, dsk_acc)
                dv[hk] = dv[hk] + _dot(jnp.transpose(p[:, 128:256]).astype(BF), dom)
                dk[hk] = dk[hk] + _dot(jnp.transpose(ds[:, 128:256]).astype(BF), qm) * 0.125
                qnm = jnp.where(hm, qn, 0.0).astype(BF)
                donm = jnp.where(hm, don, 0.0).astype(BF)
                sn = _dot_nt(qnm, kc2[hk]) * 0.125
                pn = jnp.exp(jnp.where(validn, sn - lhn, NEG))
                dsn = pn * (_dot_nt(donm, vc2[hk]) - dsumn)
                dv[hk] = dv[hk] + _dot(jnp.transpose(pn).astype(BF), donm)
                dk[hk] = dk[hk] + _dot(jnp.transpose(dsn).astype(BF), qnm) * 0.125
            dh_ref[:, 3072 + j * 128:3072 + (j + 1) * 128] = _rope_bwd(dqcol, rc_ref).astype(BF)
        fold = lambda a: a + pltpu.roll(a, 64, 1)
        dh_ref[:, 4096:4224] = _rope_bwd(jnp.where(lane_lo, fold(dk[0]), fold(dk[1])), rc_ref).astype(BF)
        dh_ref[:, 4224:4352] = jnp.where(lane_lo, fold(dv[0]), fold(dv[1])).astype(BF)
        dsink_ref[...] += dsk_acc

    prev = lambda n: jnp.maximum(n - 1, 0)
    nxt = lambda n: jnp.minimum(n + 1, nb - 1)
    full = lambda shape: pl.BlockSpec(shape, lambda n: (0,) * len(shape))
    return _pcall(
        body, grid=(nb,),
        in_specs=[pl.BlockSpec((CHUNK, EVEN_IN), lambda n: (n, 0)),
                  pl.BlockSpec((CHUNK, 256), lambda n: (prev(n), 16)),
                  pl.BlockSpec((CHUNK, EVEN_IN), lambda n: (nxt(n), 0)),
                  pl.BlockSpec((2, CHUNK, W), lambda n: (0, n, 0)),
                  pl.BlockSpec((2, CHUNK, W), lambda n: (0, nxt(n), 0)),
                  pl.BlockSpec((CHUNK, W), lambda n: (n, 0)),
                  pl.BlockSpec((CHUNK, W), lambda n: (nxt(n), 0)),
                  pl.BlockSpec((CHUNK, 128), lambda n: (n, 0)),
                  pl.BlockSpec((CHUNK, 128), lambda n: (nxt(n), 0)),
                  pl.BlockSpec((CHUNK, 384), lambda n: (n, 0)),
                  pl.BlockSpec((CHUNK, 384), lambda n: (prev(n), 0)),
                  pl.BlockSpec((CHUNK, 384), lambda n: (nxt(n), 0)),
                  full((1, W)), full((1, W)), full((8, 128, 128)), full((8, 128, 128)), full((8, 128, 128)),
                  pl.BlockSpec(memory_space=pltpu.SMEM)],
        out_specs=[pl.BlockSpec((CHUNK, EVEN_IN), lambda n: (n, 0)),
                   full((8, 128, 128)), full((128, 128)), full((1, W)), full((1, W)), full((1, 128))],
        out_shape=[jax.ShapeDtypeStruct((S, EVEN_IN), BF), jax.ShapeDtypeStruct((8, 128, 128), F32),
                   jax.ShapeDtypeStruct((128, 128), F32), jax.ShapeDtypeStruct((1, W), F32),
                   jax.ShapeDtypeStruct((1, W), F32), jax.ShapeDtypeStruct((1, 128), F32)],
        scratch=[pltpu.VMEM((CHUNK, W), F32)], name=name, comm=comm,
    )(h, h, h, dmix3, dmix3, o, o, l, l, rope, rope, rope, lng.reshape(1, W), lnb.reshape(1, W), ws, wst, bsb, sinks)


def _expm1(x):
    ser = x * (1.0 + x * (0.5 + x * (1.0 / 6.0 + x * (1.0 / 24.0))))
    return jnp.where(jnp.abs(x) < 1e-2, ser, jnp.exp(x) - 1.0)


def _softplus_neg(lam):
    z = -lam
    e = jnp.exp(-jnp.abs(z))
    l1p = jnp.where(e < 1e-3, e * (1.0 - e * (0.5 - e * (1.0 / 3.0))), jnp.log(1.0 + e))
    return jnp.maximum(z, 0.0) + l1p


def _shift_down(x, k, row, fill=0.0):
    return jnp.where(row >= k, pltpu.roll(x, k, 0), fill)


def _shift_up(x, k, row, fill=0.0):
    S = x.shape[0]
    return jnp.where(row < S - k, pltpu.roll(x, S - k, 0), fill)


def _lru_gates(xc, row, cw_ref, cb_ref, wa_ref, wx_ref, ba_ref, bx_ref, lam_ref):
    xconv = (cw_ref[3:4, :] * xc + cw_ref[2:3, :] * _shift_down(xc, 1, row) + cw_ref[1:2, :] * _shift_down(xc, 2, row)
             + cw_ref[0:1, :] * _shift_down(xc, 3, row) + cb_ref[...])
    xb = xconv.astype(BF)
    r = _sig(_dot(xb, wa_ref[...]) + ba_ref[...])
    i = _sig(_dot(xb, wx_ref[...]) + bx_ref[...])
    sp = _softplus_neg(lam_ref[...])
    log_a = -LRU_C * r * sp
    a = jnp.exp(log_a)
    mult = jnp.sqrt(-_expm1(2.0 * log_a))
    return xconv, r, i, sp, a, mult


def _odd_c_fwd(h, cw, cb, wa, wx, ba, bx, lam, name, comm=None):
    S = h.shape[0]

    def body(xc_ref, cg_ref, cw_ref, cb_ref, wa_ref, wx_ref, ba_ref, bx_ref, lam_ref, mix_ref, hst_ref):
        row = lax.broadcasted_iota(jnp.int32, (S, 128), 0)
        xconv, r, i, sp, a, mult = _lru_gates(xc_ref[...], row, cw_ref, cb_ref, wa_ref, wx_ref, ba_ref, bx_ref, lam_ref)
        aa = a
        bb = mult * (i * xconv)
        k = 1
        while k < S:
            bb = aa * _shift_down(bb, k, row) + bb
            if 2 * k < S:
                aa = aa * _shift_down(aa, k, row, 1.0)
            k *= 2
        hst_ref[...] = bb
        cg = cg_ref[...]
        mix_ref[...] = (bb * (cg * _sig(cg))).astype(BF)

    col = lambda off: pl.BlockSpec((S, 128), lambda j: (0, off + j))
    vec = pl.BlockSpec((1, 128), lambda j: (0, j))
    mat = pl.BlockSpec((None, 128, 128), lambda j: (j, 0, 0))
    return _pcall(
        body, grid=(8,),
        in_specs=[col(0), col(8), pl.BlockSpec((4, 128), lambda j: (0, j)), vec, mat, mat, vec, vec, vec],
        out_specs=[pl.BlockSpec((None, S, 128), lambda j: (0, 0, j)), pl.BlockSpec((S, 128), lambda j: (0, j))],
        out_shape=[jax.ShapeDtypeStruct((2, S, W), BF), jax.ShapeDtypeStruct((S, W), F32)],
        name=name, comm=comm,
    )(h, h, cw, cb.reshape(1, W), wa, wx, ba.reshape(1, W), bx.reshape(1, W), lam.reshape(1, W))


def _pool_sums(x, g, row, shift):
    s2 = x + shift(x, 1, row)
    s4 = s2 + shift(s2, 2, row)
    s8 = s4 + shift(s4, 4, row)
    s16 = s8 + shift(s8, 8, row)
    return jnp.where(g == 0, s2, jnp.where(g == 1, s4, jnp.where(g == 2, s8, s16)))


def _odd_d_fwd(h, mix3, wp, dscale, name):
    S = h.shape[0]

    def body(xd_ref, dg_ref, wp_ref, ds_ref, mix_in, mix_ref):
        g = pl.program_id(0)
        row = lax.broadcasted_iota(jnp.int32, (S, 256), 0)
        xd = xd_ref[...]
        cnt = jnp.minimum(row + 1, jnp.left_shift(2, g)).astype(F32)
        pooled = _pool_sums(xd, g, row, _shift_down) / cnt - xd
        mixed = _dot(pooled.astype(BF), wp_ref[...])
        dg = dg_ref[...]
        mix_ref[...] = (mixed * ds_ref[...] * (dg * _sig(dg))).astype(BF)

    col = lambda off: pl.BlockSpec((S, 256), lambda g: (0, off + g))
    return pl.pallas_call(
        body, grid=(4,),
        in_specs=[col(8), col(12), pl.BlockSpec((None, 256, 256), lambda g: (g, 0, 0)),
                  pl.BlockSpec((1, 256), lambda g: (0, g)), ANY],
        out_specs=pl.BlockSpec((None, S, 256), lambda g: (1, 0, g)),
        out_shape=jax.ShapeDtypeStruct((2, S, W), BF), input_output_aliases={4: 0},
        name=name, compiler_params=_cp(),
    )(h, h, wp, dscale.reshape(1, W), mix3)


def _odd_c_bwd(h, hst, dmix3, cw, cb, wa, wx, wat, wxt, ba, bx, lam, name, comm=None):
    S = h.shape[0]

    def body(xc_ref, cg_ref, hst_ref, dc_ref, cw_ref, cb_ref, wa_ref, wx_ref, wat_ref, wxt_ref, ba_ref, bx_ref, lam_ref,
             dh_ref, dcw_ref, dcb_ref, dwa_ref, dwx_ref, dba_ref, dbx_ref, dlam_ref):
        row = lax.broadcasted_iota(jnp.int32, (S, 128), 0)
        xc = xc_ref[...]
        xconv, r, i, sp, a, mult = _lru_gates(xc, row, cw_ref, cb_ref, wa_ref, wx_ref, ba_ref, bx_ref, lam_ref)
        hst = hst_ref[...]
        cg = cg_ref[...]
        sg, dsg = _silu_grad(cg)
        dc = dc_ref[...]
        dh_ref[1] = (dc * hst * dsg).astype(BF)
        aa = _shift_up(a, 1, row)
        bb = dc * sg
        k = 1
        while k < S:
            bb = aa * _shift_up(bb, k, row) + bb
            if 2 * k < S:
                aa = aa * _shift_up(aa, k, row, 1.0)
            k *= 2
        lam_t = bb
        da = lam_t * _shift_down(hst, 1, row)
        ix = i * xconv
        dmult = lam_t * ix
        di = lam_t * mult * xconv
        dxconv = lam_t * mult * i
        dlog_a = da * a - dmult * (a * a / mult)
        dr = dlog_a * (-LRU_C * sp)
        dsp = jnp.sum(dlog_a * (-LRU_C * r), axis=0, keepdims=True)
        dlam_ref[...] = dsp * (-_sig(-lam_ref[...]))
        dpa = dr * r * (1.0 - r)
        dpx = di * i * (1.0 - i)
        dpab = dpa.astype(BF)
        dpxb = dpx.astype(BF)
        xb = xconv.astype(BF)
        dxconv = dxconv + _dot(dpab, wat_ref[...]) + _dot(dpxb, wxt_ref[...])
        dwa_ref[...] = _dot_tn(xb, dpab)
        dwx_ref[...] = _dot_tn(xb, dpxb)
        dba_ref[...] = jnp.sum(dpa, axis=0, keepdims=True)
        dbx_ref[...] = jnp.sum(dpx, axis=0, keepdims=True)
        dh_ref[0] = (cw_ref[3:4, :] * dxconv + cw_ref[2:3, :] * _shift_up(dxconv, 1, row)
                     + cw_ref[1:2, :] * _shift_up(dxconv, 2, row) + cw_ref[0:1, :] * _shift_up(dxconv, 3, row)).astype(BF)
        for j in range(4):
            src = xc if j == 3 else _shift_down(xc, 3 - j, row)
            dcw_ref[j:j + 1, :] = jnp.sum(dxconv * src, axis=0, keepdims=True)
        dcb_ref[...] = jnp.sum(dxconv, axis=0, keepdims=True)

    col = lambda off: pl.BlockSpec((S, 128), lambda j: (0, off + j))
    vec = pl.BlockSpec((1, 128), lambda j: (0, j))
    mat = pl.BlockSpec((None, 128, 128), lambda j: (j, 0, 0))
    vshape = jax.ShapeDtypeStruct((1, W), F32)
    mshape = jax.ShapeDtypeStruct((8, 128, 128), F32)
    return _pcall(
        body, grid=(8,),
        in_specs=[col(0), col(8), col(0), pl.BlockSpec((None, S, 128), lambda j: (0, 0, j)),
                  pl.BlockSpec((4, 128), lambda j: (0, j)), vec, mat, mat, mat, mat, vec, vec, vec],
        out_specs=[pl.BlockSpec((2, S, 128), lambda j: (0, 0, j)), pl.BlockSpec((4, 128), lambda j: (0, j)), vec,
                   mat, mat, vec, vec, vec],
        out_shape=[jax.ShapeDtypeStruct((4, S, W), BF), jax.ShapeDtypeStruct((4, W), F32), vshape, mshape, mshape,
                   vshape, vshape, vshape],
        name=name, vmem=56, comm=comm,
    )(h, h, hst, dmix3, cw, cb.reshape(1, W), wa, wx, wat, wxt, ba.reshape(1, W), bx.reshape(1, W), lam.reshape(1, W))


def _odd_d_bwd(h, dmix3, dh4, wp, wpt, dscale, name):
    S = h.shape[0]

    def body(xd_ref, dg_ref, dd_ref, wp_ref, wpt_ref, ds_ref, dh_in, dh_ref, dwp_ref, dds_ref):
        g = pl.program_id(0)
        row = lax.broadcasted_iota(jnp.int32, (S, 256), 0)
        xd = xd_ref[...]
        cnt = jnp.minimum(row + 1, jnp.left_shift(2, g)).astype(F32)
        pooled = _pool_sums(xd, g, row, _shift_down) / cnt - xd
        pb = pooled.astype(BF)
        mixed = _dot(pb, wp_ref[...])
        dg = dg_ref[...]
        sg, dsg = _silu_grad(dg)
        dd = dd_ref[...]
        dmixed = dd * ds_ref[...] * sg
        dds_ref[...] = jnp.sum(dd * mixed * sg, axis=0, keepdims=True)
        dh_ref[1] = (dd * mixed * ds_ref[...] * dsg).astype(BF)
        dmb = dmixed.astype(BF)
        dpooled = _dot(dmb, wpt_ref[...])
        dwp_ref[...] = _dot_tn(pb, dmb)
        dh_ref[0] = (_pool_sums(dpooled / cnt, g, row, _shift_up) - dpooled).astype(BF)

    col = lambda off: pl.BlockSpec((S, 256), lambda g: (0, off + g))
    mat = pl.BlockSpec((None, 256, 256), lambda g: (g, 0, 0))
    vec = pl.BlockSpec((1, 256), lambda g: (0, g))
    return pl.pallas_call(
        body, grid=(4,),
        in_specs=[col(8), col(12), pl.BlockSpec((None, S, 256), lambda g: (1, 0, g)), mat, mat, vec, ANY],
        out_specs=[pl.BlockSpec((2, S, 256), lambda g: (1, 0, g)), mat, vec],
        out_shape=[jax.ShapeDtypeStruct((4, S, W), BF), jax.ShapeDtypeStruct((4, 256, 256), F32),
                   jax.ShapeDtypeStruct((1, W), F32)],
        input_output_aliases={6: 0}, name=name, compiler_params=_cp(56),
    )(h, h, dmix3, wp, wpt, dscale.reshape(1, W), dh4)


def _peer(d):
    x, y, c = lax.axis_index("x"), lax.axis_index("y"), lax.axis_index("c")
    px = 1 - x if d & 4 else x
    py = 1 - y if d & 2 else y
    pc = 1 - c if d & 1 else c
    return (px, py, pc), 4 * px + 2 * py + pc


class _GatherAll(_Comm):
    def __init__(self, xs):
        self.inputs = [xs]
        self.out_shapes = [jax.ShapeDtypeStruct((N_DEV,) + xs.shape, xs.dtype)]
        self.sem_shapes = [pltpu.SemaphoreType.DMA((N_DEV - 1,)), pltpu.SemaphoreType.DMA((N_DEV - 1,)),
                           pltpu.SemaphoreType.DMA]

    def copies(self, ins, outs, sems):
        (x_ref,), (out_ref,), (send, recv, loc) = ins, outs, sems
        _, me = _peer(0)
        res = [pltpu.make_async_copy(x_ref, out_ref.at[me], loc)]
        for d in range(1, N_DEV):
            peer, _ = _peer(d)
            res.append(pltpu.make_async_remote_copy(src_ref=x_ref, dst_ref=out_ref.at[me], send_sem=send.at[d - 1],
                                                    recv_sem=recv.at[d - 1], device_id=peer, device_id_type=MESH))
        return res


class _ExchangeAll(_Comm):
    def __init__(self, g8):
        self.inputs = [g8]
        self.out_shapes = [jax.ShapeDtypeStruct(g8.shape, g8.dtype)]
        self.sem_shapes = [pltpu.SemaphoreType.DMA((N_DEV - 1,)), pltpu.SemaphoreType.DMA((N_DEV - 1,)),
                           pltpu.SemaphoreType.DMA]

    def copies(self, ins, outs, sems):
        (g_ref,), (out_ref,), (send, recv, loc) = ins, outs, sems
        _, me = _peer(0)
        res = [pltpu.make_async_copy(g_ref.at[me], out_ref.at[0], loc)]
        for d in range(1, N_DEV):
            peer, pidx = _peer(d)
            res.append(pltpu.make_async_remote_copy(src_ref=g_ref.at[pidx], dst_ref=out_ref.at[d], send_sem=send.at[d - 1],
                                                    recv_sem=recv.at[d - 1], device_id=peer, device_id_type=MESH))
        return res


def _sum8(r8, tr, name):
    _, R, C = r8.shape
    tr = min(tr, R)
    assert R % tr == 0

    def body(r_ref, o_ref):
        acc = r_ref[0]
        for d in range(1, N_DEV):
            acc = acc + r_ref[d]
        o_ref[...] = acc

    return pl.pallas_call(
        body, grid=(R // tr,), in_specs=[pl.BlockSpec((N_DEV, tr, C), lambda i: (0, i, 0))],
        out_specs=pl.BlockSpec((tr, C), lambda i: (i, 0)), out_shape=jax.ShapeDtypeStruct((R, C), F32),
        name=name, compiler_params=_cp(),
    )(r8)


def _adamw_math(w, g, m, v):
    m2 = B1 * m + (1.0 - B1) * g
    v2 = B2 * v + (1.0 - B2) * (g * g)
    m_hat = m2 / (1.0 - B1 ** STEP)
    v_hat = v2 / (1.0 - B2 ** STEP)
    return -LR * (m_hat / (jnp.sqrt(v_hat) + ADAM_EPS) + WD * w), m2, v2


def _adamw_many(ws, gs, ms, vs, name):
    n = len(ws)

    def body(*refs):
        for i in range(n):
            d, m2, v2 = _adamw_math(refs[i][...], refs[n + i][...], refs[2 * n + i][...], refs[3 * n + i][...])
            refs[4 * n + i][...] = d
            refs[5 * n + i][...] = m2
            refs[6 * n + i][...] = v2

    vmem = pl.BlockSpec(memory_space=pltpu.VMEM)
    shapes = [jax.ShapeDtypeStruct(w.shape, F32) for w in ws]
    res = pl.pallas_call(body, in_specs=[vmem] * (4 * n), out_specs=[vmem] * (3 * n), out_shape=shapes * 3, name=name,
                         compiler_params=_cp())(*ws, *gs, *ms, *vs)
    return res[:n], res[n:2 * n], res[2 * n:]


def _adamw(w, g, m, v, tr, name, comm=None):
    R, C = w.shape
    tr = min(tr, R)

    def body(w_ref, g_ref, m_ref, v_ref, d_ref, m2_ref, v2_ref):
        d_ref[...], m2_ref[...], v2_ref[...] = _adamw_math(w_ref[...], g_ref[...], m_ref[...], v_ref[...])

    blk = pl.BlockSpec((tr, C), lambda i: (i, 0))
    shp = jax.ShapeDtypeStruct((R, C), F32)
    return _pcall(body, grid=(R // tr,), in_specs=[blk] * 4, out_specs=[blk] * 3, out_shape=[shp] * 3,
                  name=name, comm=comm)(w, g, m, v)


def _rep_pack(a):
    n = a.size
    pad = (-n) % 1024
    f = a.reshape(-1)
    if pad:
        f = jnp.concatenate([f, jnp.zeros((pad,), a.dtype)])
    return f.reshape(N_DEV, -1, 128)


def _rep_unpack(p, shape):
    n = 1
    for s in shape:
        n *= s
    return p.reshape(-1)[:n].reshape(shape)


def _sh_pack(a, axis):
    shp = a.shape
    a = a.reshape(shp[:axis] + (N_DEV, shp[axis] // N_DEV) + shp[axis + 1:])
    return jnp.moveaxis(a, axis, 0).reshape(N_DEV, -1, 128)


def _sh_unpack(p, shape, axis):
    a = p.reshape((N_DEV,) + shape[:axis] + (shape[axis] // N_DEV,) + shape[axis + 1:])
    return jnp.moveaxis(a, 0, axis).reshape(shape)


def _pad_rows(a, mult=8):
    pad = (-a.shape[-2]) % mult
    if pad:
        a = jnp.concatenate([a, jnp.zeros(a.shape[:-2] + (pad, a.shape[-1]), a.dtype)], axis=-2)
    return a


REP = ["even_a_ln_g", "even_a_ln_b", "even_a_ws", "even_a_bs", "even_b_sinks", "even_ln_g", "even_ln_b",
       "odd_w_a", "odd_w_x"]
SH = [("odd_conv_w", (2, 4, W), 2), ("odd_conv_b", (2, W), 1), ("odd_b_a", (2, W), 1), ("odd_b_x", (2, W), 1),
      ("odd_lam", (2, W), 1), ("odd_w_pool", (2, 4, 256, 256), 2), ("odd_d_scale", (2, W), 1),
      ("odd_ln_g", (2, D), 1), ("odd_ln_b", (2, D), 1)]
BIG = ["even_w_in", "even_w_out", "odd_w_in", "odd_w_out"]
NAMES = ["even_w_in", "even_a_ln_g", "even_a_ln_b", "even_a_ws", "even_a_bs", "even_b_sinks", "even_w_out",
         "even_ln_g", "even_ln_b", "odd_w_in", "odd_conv_w", "odd_conv_b", "odd_w_a", "odd_b_a", "odd_w_x", "odd_b_x",
         "odd_lam", "odd_w_pool", "odd_d_scale", "odd_w_out", "odd_ln_g", "odd_ln_b"]


def _rope_table(positions):
    S = positions.shape[0]
    inv = ROPE_THETA ** (-jnp.arange(0, 16, 2, dtype=F32) / 16)
    ang = positions.astype(F32)[:, None] * inv
    cos, sin = jnp.cos(ang), jnp.sin(ang)
    one, zero = jnp.ones((S, 48), F32), jnp.zeros((S, 48), F32)
    z8 = jnp.zeros((S, 8), F32)
    c64 = jnp.concatenate([cos, cos, one], axis=1)
    s1 = jnp.concatenate([-sin, z8, zero], axis=1)
    s2 = jnp.concatenate([z8, sin, zero], axis=1)
    return jnp.concatenate([c64, c64, s1, s1, s2, s2], axis=1)


def kernel(x, positions, even_w_in, even_a_ln_g, even_a_ln_b, even_a_ws, even_a_bs, even_b_sinks, even_w_out, even_ln_g, even_ln_b, odd_w_in, odd_conv_w, odd_conv_b, odd_w_a, odd_b_a, odd_w_x, odd_b_x, odd_lam, odd_w_pool, odd_d_scale, odd_w_out, odd_ln_g, odd_ln_b, loss_target, m_even_w_in, m_even_a_ln_g, m_even_a_ln_b, m_even_a_ws, m_even_a_bs, m_even_b_sinks, m_even_w_out, m_even_ln_g, m_even_ln_b, m_odd_w_in, m_odd_conv_w, m_odd_conv_b, m_odd_w_a, m_odd_b_a, m_odd_w_x, m_odd_b_x, m_odd_lam, m_odd_w_pool, m_odd_d_scale, m_odd_w_out, m_odd_ln_g, m_odd_ln_b, v_even_w_in, v_even_a_ln_g, v_even_a_ln_b, v_even_a_ws, v_even_a_bs, v_even_b_sinks, v_even_w_out, v_even_ln_g, v_even_ln_b, v_odd_w_in, v_odd_conv_w, v_odd_conv_b, v_odd_w_a, v_odd_b_a, v_odd_w_x, v_odd_b_x, v_odd_lam, v_odd_w_pool, v_odd_d_scale, v_odd_w_out, v_odd_ln_g, v_odd_ln_b):
    args = (even_w_in, even_a_ln_g, even_a_ln_b, even_a_ws, even_a_bs, even_b_sinks, even_w_out, even_ln_g, even_ln_b,
            odd_w_in, odd_conv_w, odd_conv_b, odd_w_a, odd_b_a, odd_w_x, odd_b_x, odd_lam, odd_w_pool, odd_d_scale,
            odd_w_out, odd_ln_g, odd_ln_b)
    margs = (m_even_w_in, m_even_a_ln_g, m_even_a_ln_b, m_even_a_ws, m_even_a_bs, m_even_b_sinks, m_even_w_out,
             m_even_ln_g, m_even_ln_b, m_odd_w_in, m_odd_conv_w, m_odd_conv_b, m_odd_w_a, m_odd_b_a, m_odd_w_x,
             m_odd_b_x, m_odd_lam, m_odd_w_pool, m_odd_d_scale, m_odd_w_out, m_odd_ln_g, m_odd_ln_b)
    vargs = (v_even_w_in, v_even_a_ln_g, v_even_a_ln_b, v_even_a_ws, v_even_a_bs, v_even_b_sinks, v_even_w_out,
             v_even_ln_g, v_even_ln_b, v_odd_w_in, v_odd_conv_w, v_odd_conv_b, v_odd_w_a, v_odd_b_a, v_odd_w_x,
             v_odd_b_x, v_odd_lam, v_odd_w_pool, v_odd_d_scale, v_odd_w_out, v_odd_ln_g, v_odd_ln_b)
    wts = dict(zip(NAMES, args))
    mom = dict(zip(NAMES, margs))
    var = dict(zip(NAMES, vargs))
    S = x.shape[1]
    x0 = x[0]
    rope = _rope_table(positions[0])

    kinds = ("even", "odd", "even", "odd")
    blk_in = [jnp.transpose(wts[kinds[l] + "_w_in"][l // 2]).astype(BF) for l in range(4)]
    blk_out = [wts[kinds[l] + "_w_out"][l // 2].astype(BF) for l in range(4)]
    sh_local = _pad_rows(jnp.concatenate([wts[nm].reshape(-1, 128) for nm, _, _ in SH], axis=0), 16)
    me = 4 * lax.axis_index("x") + 2 * lax.axis_index("y") + lax.axis_index("c")
    own_slot = lambda blk: lax.dynamic_update_slice(lax.empty((N_DEV,) + blk.shape, blk.dtype), blk[None], (me, 0, 0))
    reg = {"blk_small": sh_local, "w_small": own_slot(sh_local)}
    sched = _Sched(reg)
    for l in range(4):
        reg[f"blk_in{l}"], reg[f"blk_out{l}"] = blk_in[l], blk_out[l]
        reg[f"w_in{l}"], reg[f"w_out{l}"] = own_slot(blk_in[l]), own_slot(blk_out[l])
    sched.add(_rows("blk_in0", "w_in0", "ag1", blk_in[0].shape[0], ROW_CHUNK[blk_in[0].shape[0]]))
    sched.add(_rows("blk_small", "w_small", "ag1", sh_local.shape[0], sh_local.shape[0]))
    for l in range(4):
        sched.add(_rows(f"blk_out{l}", f"w_out{l}", "ag1", D // N_DEV, ROW_CHUNK[D // N_DEV]))
        if l < 3:
            r = blk_in[l + 1].shape[0]
            sched.add(_rows(f"blk_in{l + 1}", f"w_in{l + 1}", "ag1", r, ROW_CHUNK[r]))

    def gathered(dst, blk):
        sched.flush(dst, FLUSH_EXTRA_US)
        return reg.pop(dst)

    wt_in0 = gathered("w_in0", blk_in[0]).reshape(-1, D)
    full = {nm: wts[nm] for nm in REP}

    def gather_small():
        sh_all = gathered("w_small", sh_local)
        off = 0
        for nm, shape, axis in SH:
            r = wts[nm].size // 128
            full[nm] = _sh_unpack(sh_all[:, off:off + r, :], shape, axis)
            off += r

    saved = []
    wt_in, w_out = [wt_in0, None, None, None], [None] * 4
    xf, xb = x0, x0.astype(BF)
    fwd = lambda name: FWD_OVERBOOK * CARRY_US[name]
    for layer in range(4):
        j = layer // 2
        kind = kinds[layer]
        if wt_in[layer] is None:
            wt_in[layer] = gathered(f"w_in{layer}", blk_in[layer]).reshape(-1, D)
        h = sched.run(_mm_nt, fwd("mm_h_" + kind), xb, wt_in[layer], 1024, 768 if kind == "even" else 512, "mm_h_" + kind)
        if kind == "even":
            bsb = jnp.broadcast_to(full["even_a_bs"][j][:, :, None], (8, 128, 128))
            mix3, o, l = sched.run(_even_fwd, fwd("even_fwd"), h, rope, full["even_a_ln_g"][j], full["even_a_ln_b"][j],
                                   full["even_a_ws"][j], bsb, full["even_b_sinks"][j], "even_fwd")
            extra = (o, l, bsb)
        else:
            if "odd_lam" not in full:
                gather_small()
            wa, wx = full["odd_w_a"][j].astype(BF), full["odd_w_x"][j].astype(BF)
            wp = full["odd_w_pool"][j].astype(BF)
            mix3, hst = sched.run(_odd_c_fwd, fwd("odd_c_fwd"), h, full["odd_conv_w"][j], full["odd_conv_b"][j], wa, wx,
                                  full["odd_b_a"][j], full["odd_b_x"][j], full["odd_lam"][j], "odd_c_fwd")
            mix3 = _odd_d_fwd(h, mix3, wp, full["odd_d_scale"][j], "odd_d_fwd")
            extra = (hst, wa, wx, wp)
        w_out[layer] = gathered(f"w_out{layer}", blk_out[layer]).reshape(D, D)
        z, xn, xnb = sched.run(_mm_out_ln, fwd("mm_out_ln"), mix3, w_out[layer], xf, full[kind + "_ln_g"][j],
                               full[kind + "_ln_b"][j], "mm_out_ln")
        saved.append((xb, h, mix3, z, extra))
        xf, xb = xn, xnb

    dxn, part = _loss_grad(xf, loss_target[0])
    loss = lax.psum(part[0, 0] * (0.5 / D), ("x", "y", "c"))

    gsum = {nm: [None, None] for nm in NAMES}

    chip_sums = {}
    sched.overhang = 0.15

    waiting = []

    def chip_sum(g, tag, key):
        r = g.shape[0] // N_DEV
        reg["g_" + key] = g.reshape(N_DEV, r, D)
        sched.add(_rows("g_" + key, "d_" + key, "rsd", r, r), first=True)
        waiting.append((key, tag))

    def add_arrived():
        for key, tag in list(waiting):
            if "d_" + key in reg and not sched.pending("d_" + key):
                waiting.remove((key, tag))
                g8 = reg.pop("g_" + key)
                chip_sums[key] = reg["s_" + key] = _add_pairs(g8, reg.pop("d_" + key), "rs_add_" + tag)
                sched.add(_rows("s_" + key, "r_" + key, "rs", g8.shape[1], ROW_CHUNK[g8.shape[1]] // 2))

    sched.after_landing = add_arrived

    def reduced(key, name):
        sched.flush("d_" + key, FLUSH_EXTRA_US)
        sched.flush("r_" + key, FLUSH_EXTRA_US)
        return _rs_final(chip_sums[key], reg.pop("r_" + key), name)

    for layer in (3, 2, 1, 0):
        j = layer // 2
        xb, h, mix3, z, extra = saved[layer]
        kind = kinds[layer]
        dz, dzb, dg, dbeta = sched.run(_ln_bwd, CARRY_US["ln_bwd"], dxn, z, full[kind + "_ln_g"][j], "ln_bwd")
        gsum[kind + "_ln_g"][j] = dg.reshape(D)
        gsum[kind + "_ln_b"][j] = dbeta.reshape(D)
        chip_sum(sched.run(_mm_tn, CARRY_US["mm_dw_out"], mix3, dzb, 512, "mm_dw_out"), "w_out", f"out{layer}")
        dmix3 = sched.run(_mm_nt, CARRY_US["mm_dmix"], dzb, w_out[layer], 1024, 512, "mm_dmix", out3=True)
        if kind == "even":
            o, l, bsb = extra
            ws = full["even_a_ws"][j]
            dh, dws, dbs, dlng, dlnb, dsink = sched.run(
                _even_bwd, CARRY_US["even_bwd"], h, dmix3, o, l, rope, full["even_a_ln_g"][j], full["even_a_ln_b"][j],
                ws, jnp.swapaxes(ws, 1, 2), bsb, full["even_b_sinks"][j], "even_bwd")
            gsum["even_a_ws"][j] = dws
            gsum["even_a_bs"][j] = jnp.transpose(dbs[:, :8])
            gsum["even_a_ln_g"][j] = dlng.reshape(W)
            gsum["even_a_ln_b"][j] = dlnb.reshape(W)
            gsum["even_b_sinks"][j] = dsink[0, :16]
            if layer == 0:
                rep_rows = [_rep_pack(jnp.stack(gsum[nm]).reshape(wts[nm].shape)) for nm in REP]
                sh_rows = [_sh_pack(jnp.stack(gsum[nm]).reshape(shape), axis) for nm, shape, axis in SH]
                packed = _pad_rows(jnp.concatenate(rep_rows + sh_rows, axis=1))
                gw, (small8,) = _mm_tn(dh, xb, 384, "mm_dw_in_even", comm=_ExchangeAll(packed))
            else:
                gw = sched.run(_mm_tn, CARRY_US["mm_dw_in_even"], dh, xb, 384, "mm_dw_in_even")
            chip_sum(gw, "w_in_even", f"in{layer}")
            if layer == 0:
                sched.flush("d_in0", FLUSH_EXTRA_US)
                sched.overhang = 0.6
            dxn = sched.run(_mm_nn_res, CARRY_US["mm_dx_even"], dh, wt_in[layer], dz, 512, 512, "mm_dx_even")
        else:
            hst, wa, wx, wp = extra
            dh4, dcw, dcb, dwa, dwx, dba, dbx, dlam = sched.run(
                _odd_c_bwd, CARRY_US["odd_c_bwd"], h, hst, dmix3, full["odd_conv_w"][j], full["odd_conv_b"][j], wa, wx,
                jnp.swapaxes(wa, 1, 2), jnp.swapaxes(wx, 1, 2), full["odd_b_a"][j], full["odd_b_x"][j], full["odd_lam"][j],
                "odd_c_bwd")
            dh4, dwp, dds = _odd_d_bwd(h, dmix3, dh4, wp, jnp.swapaxes(wp, 1, 2), full["odd_d_scale"][j], "odd_d_bwd")
            gsum["odd_conv_w"][j], gsum["odd_conv_b"][j] = dcw, dcb.reshape(W)
            gsum["odd_w_a"][j], gsum["odd_w_x"][j] = dwa, dwx
            gsum["odd_b_a"][j], gsum["odd_b_x"][j], gsum["odd_lam"][j] = dba.reshape(W), dbx.reshape(W), dlam.reshape(W)
            gsum["odd_w_pool"][j], gsum["odd_d_scale"][j] = dwp, dds.reshape(W)
            chip_sum(sched.run(_mm_tn, CARRY_US["mm_dw_in_odd"], dh4, xb, 512, "mm_dw_in_odd"), "w_in_odd", f"in{layer}")
            dxn = sched.run(_mm_nn_res, CARRY_US["mm_dx_odd"], dh4, wt_in[layer], dz, 512, 512, "mm_dx_odd")
    grad_x = dxn[None]

    n_rep = sum(p.shape[1] for p in rep_rows)
    red = _sum8(small8, 1 << 20, "sum_small")
    out_g, out_d, out_m, out_v = {}, {}, {}, {}
    for nm, kind, what, layers in (("odd_w_out", "odd", "out", (1, 3)), ("even_w_out", "even", "out", (0, 2)),
                                   ("odd_w_in", "odd", "in", (1, 3)), ("even_w_in", "even", "in", (0, 2))):
        gl = [reduced(f"{what}{l}", f"rs_final_w_{what}_{kind}") for l in layers]
        g = jnp.stack([jnp.transpose(a) for a in gl] if what == "in" else gl)
        shp = wts[nm].shape
        operands = (wts[nm].reshape(-1, shp[-1]), g.reshape(-1, shp[-1]), mom[nm].reshape(-1, shp[-1]),
                    var[nm].reshape(-1, shp[-1]), 512, f"adamw_{nm}")
        if nm == "even_w_in":
            (d2, m2, v2), (rep_all,) = _adamw(*operands, comm=_GatherAll(_pad_rows(red[:n_rep])))
        else:
            d2, m2, v2 = sched.run(_adamw, CARRY_US["adamw_" + nm], *operands)
        out_g[nm], out_d[nm], out_m[nm], out_v[nm] = g, d2.reshape(shp), m2.reshape(shp), v2.reshape(shp)

    g_small = {}
    off = 0
    for nm, p in zip(REP, rep_rows):
        r = p.shape[1]
        g_small[nm] = _rep_unpack(rep_all[:, off:off + r, :], wts[nm].shape)
        off += r
    off = n_rep
    for (nm, shape, axis), p in zip(SH, sh_rows):
        r = p.shape[1]
        g_small[nm] = red[off:off + r].reshape(wts[nm].shape)
        off += r

    def rows(a):
        f = a.reshape(-1)
        pad = (-f.shape[0]) % 128
        if pad:
            f = jnp.concatenate([f, jnp.zeros((pad,), a.dtype)])
        return f.reshape(-1, 128)

    small = REP + [nm for nm, _, _ in SH]
    each = lambda src: [rows(src[nm]) for nm in small]
    d2, m2, v2 = _adamw_many(each(wts), each(g_small), each(mom), each(var), "adamw_small")
    for i, nm in enumerate(small):
        n, shp = wts[nm].size, wts[nm].shape
        take = lambda a: a.reshape(-1)[:n].reshape(shp)
        out_g[nm], out_d[nm], out_m[nm], out_v[nm] = g_small[nm], take(d2[i]), take(m2[i]), take(v2[i])

    return (loss, grad_x, *[out_g[nm] for nm in NAMES], *[out_d[nm] for nm in NAMES],
            *[out_m[nm] for nm in NAMES], *[out_v[nm] for nm in NAMES])
```

```python
import functools

import jax
import jax.numpy as jnp
from jax import lax
from jax.experimental import pallas as pl
from jax.experimental.pallas import tpu as pltpu

F32 = jnp.float32
BF = jnp.bfloat16
MESH = pl.DeviceIdType.MESH
ANY = pl.BlockSpec(memory_space=pl.ANY)

N_DEV = 8
D = 2048
W = 1024
EVEN_IN = 5376
ODD_IN = 4096
CHUNK = 128
ALPHA = (2 * 4) ** 0.25
LN_EPS = 1e-5
ROPE_THETA = 500000.0
LRU_C = 8.0
LR, B1, B2, ADAM_EPS, WD, STEP = 0.001, 0.9, 0.999, 1e-08, 0.01, 10
NEG = -1e30
HEAD_COLS = 4


def _cp(vmem_mb=48):
    return pltpu.CompilerParams(vmem_limit_bytes=vmem_mb * 1024 * 1024)


def _sig(x):
    return jax.nn.sigmoid(x)


def _silu_grad(x):
    s = _sig(x)
    return x * s, s * (1.0 + x * (1.0 - s))


def _dot(a, b):
    return jnp.dot(a, b, preferred_element_type=F32)


def _dot_nt(a, b):
    return lax.dot_general(a, b, (((1,), (1,)), ((), ())), preferred_element_type=F32)


def _dot_tn(a, b):
    return lax.dot_general(a, b, (((0,), (0,)), ((), ())), preferred_element_type=F32)


def _coords():
    return lax.axis_index("x"), lax.axis_index("y"), lax.axis_index("c")


def _chip(j):
    x, y, _ = _coords()
    return (1 - x if j & 2 else x), (1 - y if j & 1 else y)


class _Comm:
    def start(self, ins, outs, sems):
        for cp in self.copies(ins, outs, sems):
            cp.start()

    def wait(self, ins, outs, sems):
        for cp in self.copies(ins, outs, sems):
            cp.wait()


ROWS_US = {"ag1": 0.104, "ag2": 0.052, "agd": 0.027, "rsd": 0.027, "rs": 0.205}
N_COPIES = {"ag1": 2, "ag2": 2, "agd": 4, "rsd": 4, "rs": 3}
ROW_CHUNK = {672: 224, 512: 128, 256: 128}
CARRY_US = {"mm_h_even": 58, "mm_h_odd": 47, "even_fwd": 42, "odd_c_fwd": 37, "mm_out_ln": 33, "ln_bwd": 23, "mm_dmix": 26,
            "mm_dw_out": 25, "even_bwd": 95, "odd_c_bwd": 70, "mm_dw_in_even": 56, "mm_dw_in_odd": 44, "mm_dx_even": 60,
            "mm_dx_odd": 50, "adamw_even_w_in": 30, "adamw_odd_w_in": 28, "adamw_even_w_out": 11, "adamw_odd_w_out": 11}
FWD_OVERBOOK = 1.15
FLUSH_EXTRA_US = 60.0


def _cost_us(task, reg):
    kind, src, _, lo, hi = task
    return ROWS_US[kind] * (hi - lo) * reg[src].shape[-1] * reg[src].dtype.itemsize / 4096.0


class _Copies(_Comm):
    def __init__(self, tasks, reg):
        self.tasks = list(tasks)
        self.out_names, self.in_names = [], []
        for kind, src, dst, lo, hi in self.tasks:
            if dst not in self.out_names:
                self.out_names.append(dst)
        for kind, src, dst, lo, hi in self.tasks:
            if src not in self.out_names and src not in self.in_names:
                self.in_names.append(src)
        self.out_shapes, self.aliases = [], {}
        for o, dst in enumerate(self.out_names):
            if dst in reg:
                self.aliases[len(self.in_names)] = o
                self.in_names.append(dst)
                self.out_shapes.append(jax.ShapeDtypeStruct(reg[dst].shape, reg[dst].dtype))
            else:
                kind, src = next((t[0], t[1]) for t in self.tasks if t[2] == dst)
                shape = ({"rsd": 4, "rs": 3}[kind],) + reg[src].shape[1:]
                self.out_shapes.append(jax.ShapeDtypeStruct(shape, reg[src].dtype))
        self.inputs = [reg[nm] for nm in self.in_names]
        n = sum(N_COPIES[t[0]] for t in self.tasks)
        self.sem_shapes = [pltpu.SemaphoreType.DMA((n,)), pltpu.SemaphoreType.DMA((n,))]

    def copies(self, ins, outs, sems):
        send, recv = sems
        x, y, c = _coords()
        me = 4 * x + 2 * y + c
        xn, yn = (1 - x, y, c), (x, 1 - y, c)
        at_xn, at_yn = 4 * (1 - x) + 2 * y + c, 4 * x + 2 * (1 - y) + c
        ref = dict(zip(self.in_names, ins))
        ref.update(zip(self.out_names, outs))
        res = []

        def copy(src, dst, to):
            i = len(res)
            res.append(pltpu.make_async_remote_copy(src_ref=src, dst_ref=dst, send_sem=send.at[i], recv_sem=recv.at[i],
                                                    device_id=to, device_id_type=MESH))

        for kind, src, dst, lo, hi in self.tasks:
            n = hi - lo
            if kind == "ag1":
                for to in (xn, yn):
                    copy(ref[src].at[pl.ds(lo, n)], ref[dst].at[me, pl.ds(lo, n)], to)
            elif kind == "ag2":
                h = n // 2
                first, second = ref[dst].at[at_xn, pl.ds(lo, h)], ref[dst].at[at_yn, pl.ds(lo + h, n - h)]
                copy(first, first, yn)
                copy(second, second, xn)
            elif kind == "agd":
                for j in range(4):
                    px, py = _chip(j)
                    rows = ref[dst].at[4 * px + 2 * py + c, pl.ds(lo, n)]
                    copy(rows, rows, (x, y, 1 - c))
            elif kind == "rsd":
                for j in range(4):
                    px, py = _chip(j)
                    copy(ref[src].at[4 * px + 2 * py + 1 - c, pl.ds(lo, n)], ref[dst].at[j, pl.ds(lo, n)], (x, y, 1 - c))
            else:
                for j in (1, 2, 3):
                    px, py = _chip(j)
                    copy(ref[src].at[j, pl.ds(lo, n)], ref[dst].at[j - 1, pl.ds(lo, n)], (px, py, c))
        return res


class _Sched:
    def __init__(self, reg):
        self.reg, self.queue, self.later = reg, [], []
        self.overhang = 0.5
        self.after_landing = None

    def add(self, tasks, first=False):
        self.queue = list(tasks) + self.queue if first else self.queue + list(tasks)

    def pending(self, dst):
        return any(t[2] == dst for t in self.queue + self.later)

    def take(self, budget_us, must=None, overhang=0.5):
        self.queue, self.later = self.later + self.queue, []
        picked, us = [], 0.0
        rest = []
        for t in self.queue:
            cost = _cost_us(t, self.reg)
            if (must is not None and t[2] == must) or us + (1.0 - overhang) * cost <= budget_us:
                picked.append(t)
                us += cost
                if t[0] in ("ag1", "ag2"):
                    self.later.append(({"ag1": "ag2", "ag2": "agd"}[t[0]], t[2], t[2], t[3], t[4]))
            else:
                rest.append(t)
        self.queue = rest
        return _Copies(picked, self.reg) if picked else None

    def landed(self, comm, got):
        if comm is not None:
            for nm, a in zip(comm.out_names, got):
                self.reg[nm] = a
        if self.after_landing is not None:
            self.after_landing()

    def run(self, builder, budget_us, *args, **kw):
        comm = self.take(budget_us, overhang=self.overhang)
        res, got = builder(*args, comm=comm, **kw)
        self.landed(comm, got)
        return res

    def flush(self, dst, budget_us=0.0):
        while self.pending(dst):
            comm = self.take(budget_us, must=dst)
            self.landed(comm, _comm_only(comm, "flush_" + dst))


def _rows(name_src, name_dst, kind, n_rows, chunk):
    return [(kind, name_src, name_dst, lo, min(lo + chunk, n_rows)) for lo in range(0, n_rows, chunk)]


def _pcall(body, *, grid, in_specs, out_specs, out_shape, name, scratch=(), vmem=48, comm=None):
    in_specs, out_specs, out_shape, scratch = list(in_specs), list(out_specs), list(out_shape), list(scratch)
    if comm is None:
        call = pl.pallas_call(body, grid=grid, in_specs=in_specs, out_specs=out_specs, out_shape=out_shape,
                              scratch_shapes=scratch, name=name, compiler_params=_cp(vmem))
        return lambda *args: (call(*args), [])
    n_in, n_out, n_scr = len(in_specs), len(out_specs), len(scratch)
    c_in, c_out = len(comm.inputs), len(comm.out_shapes)
    aliases = {n_in + i: n_out + o for i, o in getattr(comm, "aliases", {}).items()}

    def wrapped(*refs):
        ins, cins = refs[:n_in], refs[n_in:n_in + c_in]
        o0 = n_in + c_in
        outs, couts = refs[o0:o0 + n_out], refs[o0 + n_out:o0 + n_out + c_out]
        s0 = o0 + n_out + c_out
        scr, sems = refs[s0:s0 + n_scr], refs[s0 + n_scr:]
        ids = [pl.program_id(a) for a in range(len(grid))]
        first = functools.reduce(jnp.logical_and, [i == 0 for i in ids])
        last = functools.reduce(jnp.logical_and, [i == g - 1 for i, g in zip(ids, grid)])

        @pl.when(first)
        def _():
            comm.start(cins, couts, sems)

        body(*ins, *outs, *scr)

        @pl.when(last)
        def _():
            comm.wait(cins, couts, sems)

    call = pl.pallas_call(wrapped, grid=grid, in_specs=in_specs + [ANY] * c_in, out_specs=out_specs + [ANY] * c_out,
                          out_shape=out_shape + list(comm.out_shapes), scratch_shapes=scratch + list(comm.sem_shapes),
                          input_output_aliases=aliases, name=name, compiler_params=_cp(vmem))

    def run(*args):
        res = call(*args, *comm.inputs)
        return res[:n_out], res[n_out:]

    return run


def _comm_only(comm, name):
    c_in, c_out = len(comm.inputs), len(comm.out_shapes)

    def body(*refs):
        cins, couts, sems = refs[:c_in], refs[c_in:c_in + c_out], refs[c_in + c_out:]
        comm.start(cins, couts, sems)
        comm.wait(cins, couts, sems)

    return pl.pallas_call(body, in_specs=[ANY] * c_in, out_specs=[ANY] * c_out, out_shape=list(comm.out_shapes),
                          scratch_shapes=list(comm.sem_shapes), input_output_aliases=dict(getattr(comm, "aliases", {})),
                          name=name)(*comm.inputs)


def _chip_blocks():
    _, _, c = _coords()
    return jnp.stack([4 * px + 2 * py + c for px, py in map(_chip, range(4))]).astype(jnp.int32)


def _add_pairs(g8, b4, name):
    _, R, C = b4.shape

    def body(idx_ref, a_ref, b_ref, o_ref):
        o_ref[...] = (a_ref[...].astype(F32) + b_ref[...].astype(F32)).astype(BF)

    blk = pl.BlockSpec((None, R, C), lambda j, idx: (j, 0, 0))
    grid_spec = pltpu.PrefetchScalarGridSpec(
        num_scalar_prefetch=1, grid=(4,),
        in_specs=[pl.BlockSpec((None, R, C), lambda j, idx: (idx[j], 0, 0)), blk], out_specs=blk)
    return pl.pallas_call(body, grid_spec=grid_spec, out_shape=jax.ShapeDtypeStruct(b4.shape, BF), name=name,
                          compiler_params=_cp())(_chip_blocks(), g8, b4)


def _rs_final(s4, r3, name):
    _, R, C = s4.shape
    tr = R // 2

    def body(s_ref, r_ref, o_ref):
        o_ref[...] = ((s_ref[...].astype(F32) + r_ref[0].astype(F32)) + r_ref[1].astype(F32)) + r_ref[2].astype(F32)

    return pl.pallas_call(
        body, grid=(2,),
        in_specs=[pl.BlockSpec((None, tr, C), lambda i: (0, i, 0)), pl.BlockSpec((3, tr, C), lambda i: (0, i, 0))],
        out_specs=pl.BlockSpec((tr, C), lambda i: (i, 0)), out_shape=jax.ShapeDtypeStruct((R, C), F32),
        name=name, compiler_params=_cp())(s4, r3)


def _mm_nt(a, w, tm, tn, name, out3=False, comm=None):
    M, K = a.shape
    N = w.shape[0]
    tm = min(tm, M)

    def body(a_ref, w_ref, o_ref):
        o_ref[...] = _dot_nt(a_ref[...], w_ref[...])

    if out3:
        per = W // tn
        out_shape = jax.ShapeDtypeStruct((N // W, M, W), F32)
        out_spec = pl.BlockSpec((None, tm, tn), lambda i, j: (j // per, i, j % per))
    else:
        out_shape = jax.ShapeDtypeStruct((M, N), F32)
        out_spec = pl.BlockSpec((tm, tn), lambda i, j: (i, j))
    (res,), extra = _pcall(
        body, grid=(M // tm, N // tn),
        in_specs=[pl.BlockSpec((tm, K), lambda i, j: (i, 0)), pl.BlockSpec((tn, K), lambda i, j: (j, 0))],
        out_specs=[out_spec], out_shape=[out_shape], name=name, comm=comm)(a, w)
    return res, extra


def _mm_tn(a, b, tm, name, comm=None):
    K, N = b.shape
    if a.ndim == 3:
        M = a.shape[0] * W
        per = W // tm
        a_spec = pl.BlockSpec((None, K, tm), lambda i: (i // per, 0, i % per))
    else:
        M = a.shape[1]
        a_spec = pl.BlockSpec((K, tm), lambda i: (0, i))

    def body(a_ref, b_ref, o_ref):
        o_ref[...] = _dot_tn(a_ref[...], b_ref[...]).astype(BF)

    (out,), extra = _pcall(
        body, grid=(M // tm,),
        in_specs=[a_spec, pl.BlockSpec((K, N), lambda i: (0, 0))],
        out_specs=[pl.BlockSpec((tm, N), lambda i: (i, 0))],
        out_shape=[jax.ShapeDtypeStruct((M, N), BF)], name=name, vmem=56, comm=comm)(a, b)
    return out, extra


def _mm_nn_res(a, w, res, tm, tn, name, comm=None):
    K, N = w.shape
    if a.ndim == 3:
        P, M = a.shape[0], a.shape[1]
        tm = min(tm, M)
        a_spec = pl.BlockSpec((P, tm, W), lambda i, j: (0, i, 0))
    else:
        P, M = 0, a.shape[0]
        tm = min(tm, M)
        a_spec = pl.BlockSpec((tm, K), lambda i, j: (i, 0))

    def body(a_ref, w_ref, r_ref, o_ref):
        if P:
            d = _dot(a_ref[0], w_ref[0:W, :])
            for p in range(1, P):
                d = d + _dot(a_ref[p], w_ref[p * W:(p + 1) * W, :])
        else:
            d = _dot(a_ref[...], w_ref[...])
        o_ref[...] = ALPHA * r_ref[...] + d

    (out,), extra = _pcall(
        body, grid=(M // tm, N // tn),
        in_specs=[a_spec, pl.BlockSpec((K, tn), lambda i, j: (0, j)), pl.BlockSpec((tm, tn), lambda i, j: (i, j))],
        out_specs=[pl.BlockSpec((tm, tn), lambda i, j: (i, j))],
        out_shape=[jax.ShapeDtypeStruct((M, N), F32)], name=name, comm=comm)(a, w, res)
    return out, extra


def _mm_out_ln(mix3, w_out, x, g, b, name, comm=None):
    S = x.shape[0]
    tm = min(256, S)

    def body(m_ref, w_ref, x_ref, g_ref, b_ref, z_ref, xn_ref, xb_ref):
        acc = _dot(m_ref[0], w_ref[0:W, :]) + _dot(m_ref[1], w_ref[W:2 * W, :])
        z = ALPHA * x_ref[...] + acc
        mu = jnp.mean(z, axis=1, keepdims=True)
        zc = z - mu
        var = jnp.mean(zc * zc, axis=1, keepdims=True)
        xn = zc * lax.rsqrt(var + LN_EPS) * g_ref[...] + b_ref[...]
        z_ref[...] = z
        xn_ref[...] = xn
        xb_ref[...] = xn.astype(BF)

    row = pl.BlockSpec((tm, D), lambda i: (i, 0))
    vec = pl.BlockSpec((1, D), lambda i: (0, 0))
    return _pcall(
        body, grid=(S // tm,),
        in_specs=[pl.BlockSpec((2, tm, W), lambda i: (0, i, 0)), pl.BlockSpec((D, D), lambda i: (0, 0)), row, vec, vec],
        out_specs=[row, row, row],
        out_shape=[jax.ShapeDtypeStruct((S, D), F32), jax.ShapeDtypeStruct((S, D), F32), jax.ShapeDtypeStruct((S, D), BF)],
        name=name, comm=comm)(mix3, w_out, x, g.reshape(1, D), b.reshape(1, D))


def _ln_bwd(dxn, z, g, name, comm=None):
    S = z.shape[0]
    tm = min(256, S)

    def body(d_ref, z_ref, g_ref, dz_ref, dzb_ref, dg_ref, db_ref):
        i = pl.program_id(0)
        zz = z_ref[...]
        mu = jnp.mean(zz, axis=1, keepdims=True)
        zc = zz - mu
        var = jnp.mean(zc * zc, axis=1, keepdims=True)
        rstd = lax.rsqrt(var + LN_EPS)
        xhat = zc * rstd
        dy = d_ref[...]
        dyg = dy * g_ref[...]
        m1 = jnp.mean(dyg, axis=1, keepdims=True)
        m2 = jnp.mean(dyg * xhat, axis=1, keepdims=True)
        dz = rstd * (dyg - m1 - xhat * m2)
        dz_ref[...] = dz
        dzb_ref[...] = dz.astype(BF)

        @pl.when(i == 0)
        def _():
            dg_ref[...] = jnp.zeros_like(dg_ref)
            db_ref[...] = jnp.zeros_like(db_ref)

        dg_ref[...] += jnp.sum(dy * xhat, axis=0, keepdims=True)
        db_ref[...] += jnp.sum(dy, axis=0, keepdims=True)

    row = pl.BlockSpec((tm, D), lambda i: (i, 0))
    vec = pl.BlockSpec((1, D), lambda i: (0, 0))
    return _pcall(
        body, grid=(S // tm,), in_specs=[row, row, vec], out_specs=[row, row, vec, vec],
        out_shape=[jax.ShapeDtypeStruct((S, D), F32), jax.ShapeDtypeStruct((S, D), BF),
                   jax.ShapeDtypeStruct((1, D), F32), jax.ShapeDtypeStruct((1, D), F32)],
        name=name, comm=comm)(dxn, z, g.reshape(1, D))


def _loss_grad(xn, target):
    S = xn.shape[0]
    tm = min(256, S)

    def body(x_ref, t_ref, d_ref, p_ref):
        i = pl.program_id(0)
        e = x_ref[...] - t_ref[...]
        d_ref[...] = e * (1.0 / D)

        @pl.when(i == 0)
        def _():
            p_ref[...] = jnp.zeros_like(p_ref)

        p_ref[...] += jnp.sum(jnp.sum(e * e, axis=1, keepdims=True), axis=0, keepdims=True)

    row = pl.BlockSpec((tm, D), lambda i: (i, 0))
    return pl.pallas_call(
        body, grid=(S // tm,), in_specs=[row, row],
        out_specs=[row, pl.BlockSpec((8, 128), lambda i: (0, 0))],
        out_shape=[jax.ShapeDtypeStruct((S, D), F32), jax.ShapeDtypeStruct((8, 128), F32)],
        name="loss_grad", compiler_params=_cp(),
    )(xn, target)


def _rope_fwd(t, r_ref):
    return (t * r_ref[:, 0:128] + pltpu.roll(t, 120, 1) * r_ref[:, 128:256]
            + pltpu.roll(t, 8, 1) * r_ref[:, 256:384])


def _rope_bwd(g, r_ref):
    return (g * r_ref[:, 0:128] + pltpu.roll(g * r_ref[:, 128:256], 8, 1)
            + pltpu.roll(g * r_ref[:, 256:384], 120, 1))


def _dup_heads(kb):
    lo = lax.broadcasted_iota(jnp.int32, kb.shape, 1) < 64
    sw = pltpu.roll(kb, 64, 1)
    return [jnp.where(lo, kb, sw).astype(BF), jnp.where(lo, sw, kb).astype(BF)]


def _even_fwd(h, rope, lng, lnb, ws, bsb, sinks, name, comm=None):
    S = h.shape[0]
    nb = S // CHUNK

    def body(h_ref, hp_ref, rc_ref, rp_ref, lng_ref, lnb_ref, ws_ref, bsb_ref, sink_ref, mix_ref, o_ref, l_ref):
        n = pl.program_id(0)
        lane = lax.broadcasted_iota(jnp.int32, (128, 128), 1)
        rowi = lax.broadcasted_iota(jnp.int32, (128, 128), 0)
        tri = rowi >= lane
        lane_lo = lane < 64
        v = h_ref[:, W:2 * W]
        mu = jnp.mean(v, axis=1, keepdims=True)
        vc = v - mu
        var = jnp.mean(vc * vc, axis=1, keepdims=True)
        vn = vc * lax.rsqrt(var + LN_EPS) * lng_ref[...] + lnb_ref[...]
        ms = [_dot(jnp.where(tri, ws_ref[g], 0.0).astype(BF), vn[:, g * 128:(g + 1) * 128].astype(BF)) for g in range(8)]
        for g in range(8):
            sl = slice(g * 128, (g + 1) * 128)
            ag = h_ref[:, 2 * W + g * 128:2 * W + (g + 1) * 128]
            mix_ref[0, :, sl] = (h_ref[:, sl] * (ms[g] + bsb_ref[g]) * (ag * _sig(ag))).astype(BF)
        kb = jnp.concatenate([_rope_fwd(hp_ref[:, 0:128], rp_ref), _rope_fwd(h_ref[:, 4096:4224], rc_ref)], axis=0)
        vb = jnp.concatenate([hp_ref[:, 128:256], h_ref[:, 4224:4352]], axis=0)
        k2 = _dup_heads(kb)
        v2 = _dup_heads(vb)
        qi = lax.broadcasted_iota(jnp.int32, (128, 256), 0)
        kj = lax.broadcasted_iota(jnp.int32, (128, 256), 1)
        diff = qi + 128 - kj
        valid = (diff >= 0) & (diff < 128) & ((n > 0) | (kj >= 128))
        lacc = jnp.zeros((128, 128), F32)
        for j0 in range(0, 8, HEAD_COLS):
            heads = [(j, half) for j in range(j0, j0 + HEAD_COLS) for half in range(2)]
            sc, pr, oh = {}, {}, {}
            for j in range(j0, j0 + HEAD_COLS):
                qc = _rope_fwd(h_ref[:, 3072 + j * 128:3072 + (j + 1) * 128], rc_ref)
                sc[j, 0] = _dot_nt(jnp.where(lane_lo, qc, 0.0).astype(BF), k2[j // 4])
                sc[j, 1] = _dot_nt(jnp.where(lane_lo, 0.0, qc).astype(BF), k2[j // 4])
            for j, half in heads:
                hq = 2 * j + half
                s = jnp.where(valid, sc[j, half] * 0.125, NEG)
                sk = sink_ref[hq]
                mx = jnp.maximum(jnp.max(s, axis=1, keepdims=True), sk)
                p = jnp.exp(s - mx)
                den = jnp.sum(p, axis=1, keepdims=True) + jnp.exp(sk - mx)
                pr[j, half] = (p / den).astype(BF)
                lacc = jnp.where(lane == hq, mx + jnp.log(den), lacc)
            for j, half in heads:
                oh[j, half] = _dot(pr[j, half], v2[j // 4])
            for j in range(j0, j0 + HEAD_COLS):
                cs = slice(j * 128, (j + 1) * 128)
                ocol = jnp.where(lane_lo, oh[j, 0], oh[j, 1])
                bg = h_ref[:, 4352 + j * 128:4352 + (j + 1) * 128]
                o_ref[:, cs] = ocol
                mix_ref[1, :, cs] = (ocol * (bg * _sig(bg))).astype(BF)
        l_ref[...] = lacc

    prev = lambda n: jnp.maximum(n - 1, 0)
    full = lambda shape: pl.BlockSpec(shape, lambda n: (0,) * len(shape))
    return _pcall(
        body, grid=(nb,),
        in_specs=[pl.BlockSpec((CHUNK, EVEN_IN), lambda n: (n, 0)),
                  pl.BlockSpec((CHUNK, 256), lambda n: (prev(n), 16)),
                  pl.BlockSpec((CHUNK, 384), lambda n: (n, 0)),
                  pl.BlockSpec((CHUNK, 384), lambda n: (prev(n), 0)),
                  full((1, W)), full((1, W)), full((8, 128, 128)), full((8, 128, 128)),
                  pl.BlockSpec(memory_space=pltpu.SMEM)],
        out_specs=[pl.BlockSpec((2, CHUNK, W), lambda n: (0, n, 0)),
                   pl.BlockSpec((CHUNK, W), lambda n: (n, 0)),
                   pl.BlockSpec((CHUNK, 128), lambda n: (n, 0))],
        out_shape=[jax.ShapeDtypeStruct((2, S, W), BF), jax.ShapeDtypeStruct((S, W), F32),
                   jax.ShapeDtypeStruct((S, 128), F32)],
        name=name, comm=comm)(h, h, rope, rope, lng.reshape(1, W), lnb.reshape(1, W), ws, bsb, sinks)


def _even_bwd(h, dmix3, o, l, rope, lng, lnb, ws, wst, bsb, sinks, name, comm=None):
    S = h.shape[0]
    nb = S // CHUNK

    def body(h_ref, hp_ref, hn_ref, dm_ref, dmn_ref, o_ref, on_ref, l_ref, ln_ref, rc_ref, rp_ref, rn_ref,
             lng_ref, lnb_ref, ws_ref, wst_ref, bsb_ref, sink_ref,
             dh_ref, dws_ref, dbs_ref, dlng_ref, dlnb_ref, dsink_ref, dvn_ref):
        n = pl.program_id(0)

        @pl.when(n == 0)
        def _():
            dws_ref[...] = jnp.zeros_like(dws_ref)
            dbs_ref[...] = jnp.zeros_like(dbs_ref)
            dlng_ref[...] = jnp.zeros_like(dlng_ref)
            dlnb_ref[...] = jnp.zeros_like(dlnb_ref)
            dsink_ref[...] = jnp.zeros_like(dsink_ref)

        lane = lax.broadcasted_iota(jnp.int32, (128, 128), 1)
        rowi = lax.broadcasted_iota(jnp.int32, (128, 128), 0)
        lane1 = lax.broadcasted_iota(jnp.int32, (1, 128), 1)
        tri = rowi >= lane
        tri_t = lane >= rowi
        lane_lo = lane < 64
        v = h_ref[:, W:2 * W]
        mu = jnp.mean(v, axis=1, keepdims=True)
        vc = v - mu
        var = jnp.mean(vc * vc, axis=1, keepdims=True)
        rstd = lax.rsqrt(var + LN_EPS)
        vhat = vc * rstd
        vn = vhat * lng_ref[...] + lnb_ref[...]
        dbs_acc = jnp.zeros((128, 128), F32)
        vng = [vn[:, g * 128:(g + 1) * 128].astype(BF) for g in range(8)]
        ms = [_dot(jnp.where(tri, ws_ref[g], 0.0).astype(BF), vng[g]) for g in range(8)]
        dmb = []
        for g in range(8):
            sl = slice(g * 128, (g + 1) * 128)
            m = ms[g] + bsb_ref[g]
            ag = h_ref[:, 2 * W + g * 128:2 * W + (g + 1) * 128]
            sg, dsg = _silu_grad(ag)
            u = h_ref[:, sl]
            da = dm_ref[0, :, sl]
            dmm = da * u * sg
            dh_ref[:, sl] = (da * m * sg).astype(BF)
            dh_ref[:, 2 * W + g * 128:2 * W + (g + 1) * 128] = (da * u * m * dsg).astype(BF)
            dmb.append(dmm.astype(BF))
            dbs_acc = jnp.where(lane == g, jnp.sum(dmm, axis=1, keepdims=True), dbs_acc)
        dvs = [_dot(jnp.where(tri_t, wst_ref[g], 0.0).astype(BF), dmb[g]) for g in range(8)]
        dwss = [_dot_nt(dmb[g], vng[g]) for g in range(8)]
        for g in range(8):
            dvn_ref[:, g * 128:(g + 1) * 128] = dvs[g]
            dws_ref[g] += jnp.where(tri, dwss[g], 0.0)
        dbs_ref[...] += dbs_acc
        dvn = dvn_ref[...]
        dlng_ref[...] += jnp.sum(dvn * vhat, axis=0, keepdims=True)
        dlnb_ref[...] += jnp.sum(dvn, axis=0, keepdims=True)
        dyg = dvn * lng_ref[...]
        m1 = jnp.mean(dyg, axis=1, keepdims=True)
        m2 = jnp.mean(dyg * vhat, axis=1, keepdims=True)
        dh_ref[:, W:2 * W] = (rstd * (dyg - m1 - vhat * m2)).astype(BF)
        kcur = _rope_fwd(h_ref[:, 4096:4224], rc_ref)
        kb = jnp.concatenate([_rope_fwd(hp_ref[:, 0:128], rp_ref), kcur], axis=0)
        vb = jnp.concatenate([hp_ref[:, 128:256], h_ref[:, 4224:4352]], axis=0)
        k2 = _dup_heads(kb)
        v2 = _dup_heads(vb)
        kc2 = _dup_heads(kcur)
        vc2 = _dup_heads(h_ref[:, 4224:4352])
        qi = lax.broadcasted_iota(jnp.int32, (128, 256), 0)
        kj = lax.broadcasted_iota(jnp.int32, (128, 256), 1)
        diff = qi + 128 - kj
        valid = (diff >= 0) & (diff < 128) & ((n > 0) | (kj >= 128))
        validn = (lane > rowi) & (n < nb - 1)
        lc = l_ref[...]
        lnx = ln_ref[...]
        dk = [jnp.zeros((128, 128), F32), jnp.zeros((128, 128), F32)]
        dv = [jnp.zeros((128, 128), F32), jnp.zeros((128, 128), F32)]
        dsk_acc = jnp.zeros((1, 128), F32)
        for j0 in range(0, 8, HEAD_COLS):
            heads = [(j, half) for j in range(j0, j0 + HEAD_COLS) for half in range(2)]
            t = {}
            for j in range(j0, j0 + HEAD_COLS):
                cs = slice(j * 128, (j + 1) * 128)
                qc = _rope_fwd(h_ref[:, 3072 + j * 128:3072 + (j + 1) * 128], rc_ref)
                qn = _rope_fwd(hn_ref[:, 3072 + j * 128:3072 + (j + 1) * 128], rn_ref)
                bg = h_ref[:, 4352 + j * 128:4352 + (j + 1) * 128]
                sgb, dsgb = _silu_grad(bg)
                db = dm_ref[1, :, cs]
                oc = o_ref[:, cs]
                do = db * sgb
                dh_ref[:, 4352 + j * 128:4352 + (j + 1) * 128] = (db * oc * dsgb).astype(BF)
                bgn = hn_ref[:, 4352 + j * 128:4352 + (j + 1) * 128]
                don = dmn_ref[1, :, cs] * (bgn * _sig(bgn))
                prod = do * oc
                prodn = don * on_ref[:, cs]
                for half in range(2):
                    hq = 2 * j + half
                    hm = lane_lo if half == 0 else jnp.logical_not(lane_lo)
                    t[j, half] = dict(
                        dsum=jnp.sum(jnp.where(hm, prod, 0.0), axis=1, keepdims=True),
                        dsumn=jnp.sum(jnp.where(hm, prodn, 0.0), axis=1, keepdims=True),
                        lh=jnp.sum(jnp.where(lane == hq, lc, 0.0), axis=1, keepdims=True),
                        lhn=jnp.sum(jnp.where(lane == hq, lnx, 0.0), axis=1, keepdims=True),
                        qm=jnp.where(hm, qc, 0.0).astype(BF), dom=jnp.where(hm, do, 0.0).astype(BF),
                        qnm=jnp.where(hm, qn, 0.0).astype(BF), donm=jnp.where(hm, don, 0.0).astype(BF))
            for j, half in heads:
                e, hk = t[j, half], j // 4
                e["s"], e["dp"] = _dot_nt(e["qm"], k2[hk]), _dot_nt(e["dom"], v2[hk])
                e["sn"], e["dpn"] = _dot_nt(e["qnm"], kc2[hk]), _dot_nt(e["donm"], vc2[hk])
            for j, half in heads:
                e, hq = t[j, half], 2 * j + half
                p = jnp.exp(jnp.where(valid, e["s"] * 0.125 - e["lh"], NEG))
                ds = p * (e["dp"] - e["dsum"])
                pn = jnp.exp(jnp.where(validn, e["sn"] * 0.125 - e["lhn"], NEG))
                dsn = pn * (e["dpn"] - e["dsumn"])
                psink = jnp.exp(sink_ref[hq] - e["lh"])
                dsk_acc = jnp.where(lane1 == hq, -jnp.sum(psink * e["dsum"], axis=0, keepdims=True), dsk_acc)
                e["ds"] = ds.astype(BF)
                e["pt"], e["dst"] = jnp.transpose(p[:, 128:256]).astype(BF), jnp.transpose(ds[:, 128:256]).astype(BF)
                e["pnt"], e["dsnt"] = jnp.transpose(pn).astype(BF), jnp.transpose(dsn).astype(BF)
            for j, half in heads:
                e, hk = t[j, half], j // 4
                e["dq"] = _dot(e["ds"], k2[hk])
                e["dv"] = _dot(e["pt"], e["dom"]) + _dot(e["pnt"], e["donm"])
                e["dk"] = _dot(e["dst"], e["qm"]) + _dot(e["dsnt"], e["qnm"])
            for j in range(j0, j0 + HEAD_COLS):
                hk = j // 4
                dqcol = jnp.where(lane_lo, t[j, 0]["dq"], t[j, 1]["dq"]) * 0.125
                dh_ref[:, 3072 + j * 128:3072 + (j + 1) * 128] = _rope_bwd(dqcol, rc_ref).astype(BF)
                dv[hk] = dv[hk] + t[j, 0]["dv"] + t[j, 1]["dv"]
                dk[hk] = dk[hk] + (t[j, 0]["dk"] + t[j, 1]["dk"]) * 0.125
        fold = lambda a: a + pltpu.roll(a, 64, 1)
        dh_ref[:, 4096:4224] = _rope_bwd(jnp.where(lane_lo, fold(dk[0]), fold(dk[1])), rc_ref).astype(BF)
        dh_ref[:, 4224:4352] = jnp.where(lane_lo, fold(dv[0]), fold(dv[1])).astype(BF)
        dsink_ref[...] += dsk_acc

    prev = lambda n: jnp.maximum(n - 1, 0)
    nxt = lambda n: jnp.minimum(n + 1, nb - 1)
    full = lambda shape: pl.BlockSpec(shape, lambda n: (0,) * len(shape))
    return _pcall(
        body, grid=(nb,),
        in_specs=[pl.BlockSpec((CHUNK, EVEN_IN), lambda n: (n, 0)),
                  pl.BlockSpec((CHUNK, 256), lambda n: (prev(n), 16)),
                  pl.BlockSpec((CHUNK, EVEN_IN), lambda n: (nxt(n), 0)),
                  pl.BlockSpec((2, CHUNK, W), lambda n: (0, n, 0)),
                  pl.BlockSpec((2, CHUNK, W), lambda n: (0, nxt(n), 0)),
                  pl.BlockSpec((CHUNK, W), lambda n: (n, 0)),
                  pl.BlockSpec((CHUNK, W), lambda n: (nxt(n), 0)),
                  pl.BlockSpec((CHUNK, 128), lambda n: (n, 0)),
                  pl.BlockSpec((CHUNK, 128), lambda n: (nxt(n), 0)),
                  pl.BlockSpec((CHUNK, 384), lambda n: (n, 0)),
                  pl.BlockSpec((CHUNK, 384), lambda n: (prev(n), 0)),
                  pl.BlockSpec((CHUNK, 384), lambda n: (nxt(n), 0)),
                  full((1, W)), full((1, W)), full((8, 128, 128)), full((8, 128, 128)), full((8, 128, 128)),
                  pl.BlockSpec(memory_space=pltpu.SMEM)],
        out_specs=[pl.BlockSpec((CHUNK, EVEN_IN), lambda n: (n, 0)),
                   full((8, 128, 128)), full((128, 128)), full((1, W)), full((1, W)), full((1, 128))],
        out_shape=[jax.ShapeDtypeStruct((S, EVEN_IN), BF), jax.ShapeDtypeStruct((8, 128, 128), F32),
                   jax.ShapeDtypeStruct((128, 128), F32), jax.ShapeDtypeStruct((1, W), F32),
                   jax.ShapeDtypeStruct((1, W), F32), jax.ShapeDtypeStruct((1, 128), F32)],
        scratch=[pltpu.VMEM((CHUNK, W), F32)], name=name, comm=comm,
    )(h, h, h, dmix3, dmix3, o, o, l, l, rope, rope, rope, lng.reshape(1, W), lnb.reshape(1, W), ws, wst, bsb, sinks)


def _expm1(x):
    ser = x * (1.0 + x * (0.5 + x * (1.0 / 6.0 + x * (1.0 / 24.0))))
    return jnp.where(jnp.abs(x) < 1e-2, ser, jnp.exp(x) - 1.0)


def _softplus_neg(lam):
    z = -lam
    e = jnp.exp(-jnp.abs(z))
    l1p = jnp.where(e < 1e-3, e * (1.0 - e * (0.5 - e * (1.0 / 3.0))), jnp.log(1.0 + e))
    return jnp.maximum(z, 0.0) + l1p


def _shift_down(x, k, row, fill=0.0):
    return jnp.where(row >= k, pltpu.roll(x, k, 0), fill)


def _shift_up(x, k, row, fill=0.0):
    S = x.shape[0]
    return jnp.where(row < S - k, pltpu.roll(x, S - k, 0), fill)


def _lru_gates(xc, row, cw_ref, cb_ref, wa_ref, wx_ref, ba_ref, bx_ref, lam_ref):
    xconv = (cw_ref[3:4, :] * xc + cw_ref[2:3, :] * _shift_down(xc, 1, row) + cw_ref[1:2, :] * _shift_down(xc, 2, row)
             + cw_ref[0:1, :] * _shift_down(xc, 3, row) + cb_ref[...])
    xb = xconv.astype(BF)
    r = _sig(_dot(xb, wa_ref[...]) + ba_ref[...])
    i = _sig(_dot(xb, wx_ref[...]) + bx_ref[...])
    sp = _softplus_neg(lam_ref[...])
    log_a = -LRU_C * r * sp
    a = jnp.exp(log_a)
    mult = jnp.sqrt(-_expm1(2.0 * log_a))
    return xconv, r, i, sp, a, mult


def _odd_c_fwd(h, cw, cb, wa, wx, ba, bx, lam, name, comm=None):
    S = h.shape[0]

    def body(xc_ref, cg_ref, cw_ref, cb_ref, wa_ref, wx_ref, ba_ref, bx_ref, lam_ref, mix_ref, hst_ref):
        row = lax.broadcasted_iota(jnp.int32, (S, 128), 0)
        xconv, r, i, sp, a, mult = _lru_gates(xc_ref[...], row, cw_ref, cb_ref, wa_ref, wx_ref, ba_ref, bx_ref, lam_ref)
        aa = a
        bb = mult * (i * xconv)
        k = 1
        while k < S:
            bb = aa * _shift_down(bb, k, row) + bb
            if 2 * k < S:
                aa = aa * _shift_down(aa, k, row, 1.0)
            k *= 2
        hst_ref[...] = bb
        cg = cg_ref[...]
        mix_ref[...] = (bb * (cg * _sig(cg))).astype(BF)

    col = lambda off: pl.BlockSpec((S, 128), lambda j: (0, off + j))
    vec = pl.BlockSpec((1, 128), lambda j: (0, j))
    mat = pl.BlockSpec((None, 128, 128), lambda j: (j, 0, 0))
    return _pcall(
        body, grid=(8,),
        in_specs=[col(0), col(8), pl.BlockSpec((4, 128), lambda j: (0, j)), vec, mat, mat, vec, vec, vec],
        out_specs=[pl.BlockSpec((None, S, 128), lambda j: (0, 0, j)), pl.BlockSpec((S, 128), lambda j: (0, j))],
        out_shape=[jax.ShapeDtypeStruct((2, S, W), BF), jax.ShapeDtypeStruct((S, W), F32)],
        name=name, comm=comm,
    )(h, h, cw, cb.reshape(1, W), wa, wx, ba.reshape(1, W), bx.reshape(1, W), lam.reshape(1, W))


def _pool_sums(x, g, row, shift):
    s2 = x + shift(x, 1, row)
    s4 = s2 + shift(s2, 2, row)
    s8 = s4 + shift(s4, 4, row)
    s16 = s8 + shift(s8, 8, row)
    return jnp.where(g == 0, s2, jnp.where(g == 1, s4, jnp.where(g == 2, s8, s16)))


def _odd_d_fwd(h, mix3, wp, dscale, name):
    S = h.shape[0]

    def body(xd_ref, dg_ref, wp_ref, ds_ref, mix_in, mix_ref):
        g = pl.program_id(0)
        row = lax.broadcasted_iota(jnp.int32, (S, 256), 0)
        xd = xd_ref[...]
        cnt = jnp.minimum(row + 1, jnp.left_shift(2, g)).astype(F32)
        pooled = _pool_sums(xd, g, row, _shift_down) / cnt - xd
        mixed = _dot(pooled.astype(BF), wp_ref[...])
        dg = dg_ref[...]
        mix_ref[...] = (mixed * ds_ref[...] * (dg * _sig(dg))).astype(BF)

    col = lambda off: pl.BlockSpec((S, 256), lambda g: (0, off + g))
    return pl.pallas_call(
        body, grid=(4,),
        in_specs=[col(8), col(12), pl.BlockSpec((None, 256, 256), lambda g: (g, 0, 0)),
                  pl.BlockSpec((1, 256), lambda g: (0, g)), ANY],
        out_specs=pl.BlockSpec((None, S, 256), lambda g: (1, 0, g)),
        out_shape=jax.ShapeDtypeStruct((2, S, W), BF), input_output_aliases={4: 0},
        name=name, compiler_params=_cp(),
    )(h, h, wp, dscale.reshape(1, W), mix3)


def _odd_c_bwd(h, hst, dmix3, cw, cb, wa, wx, wat, wxt, ba, bx, lam, name, comm=None):
    S = h.shape[0]

    def body(xc_ref, cg_ref, hst_ref, dc_ref, cw_ref, cb_ref, wa_ref, wx_ref, wat_ref, wxt_ref, ba_ref, bx_ref, lam_ref,
             dh_ref, dcw_ref, dcb_ref, dwa_ref, dwx_ref, dba_ref, dbx_ref, dlam_ref):
        row = lax.broadcasted_iota(jnp.int32, (S, 128), 0)
        xc = xc_ref[...]
        xconv, r, i, sp, a, mult = _lru_gates(xc, row, cw_ref, cb_ref, wa_ref, wx_ref, ba_ref, bx_ref, lam_ref)
        hst = hst_ref[...]
        cg = cg_ref[...]
        sg, dsg = _silu_grad(cg)
        dc = dc_ref[...]
        dh_ref[1] = (dc * hst * dsg).astype(BF)
        aa = _shift_up(a, 1, row)
        bb = dc * sg
        k = 1
        while k < S:
            bb = aa * _shift_up(bb, k, row) + bb
            if 2 * k < S:
                aa = aa * _shift_up(aa, k, row, 1.0)
            k *= 2
        lam_t = bb
        da = lam_t * _shift_down(hst, 1, row)
        ix = i * xconv
        dmult = lam_t * ix
        di = lam_t * mult * xconv
        dxconv = lam_t * mult * i
        dlog_a = da * a - dmult * (a * a / mult)
        dr = dlog_a * (-LRU_C * sp)
        dsp = jnp.sum(dlog_a * (-LRU_C * r), axis=0, keepdims=True)
        dlam_ref[...] = dsp * (-_sig(-lam_ref[...]))
        dpa = dr * r * (1.0 - r)
        dpx = di * i * (1.0 - i)
        dpab = dpa.astype(BF)
        dpxb = dpx.astype(BF)
        xb = xconv.astype(BF)
        dxconv = dxconv + _dot(dpab, wat_ref[...]) + _dot(dpxb, wxt_ref[...])
        dwa_ref[...] = _dot_tn(xb, dpab)
        dwx_ref[...] = _dot_tn(xb, dpxb)
        dba_ref[...] = jnp.sum(dpa, axis=0, keepdims=True)
        dbx_ref[...] = jnp.sum(dpx, axis=0, keepdims=True)
        dh_ref[0] = (cw_ref[3:4, :] * dxconv + cw_ref[2:3, :] * _shift_up(dxconv, 1, row)
                     + cw_ref[1:2, :] * _shift_up(dxconv, 2, row) + cw_ref[0:1, :] * _shift_up(dxconv, 3, row)).astype(BF)
        for j in range(4):
            src = xc if j == 3 else _shift_down(xc, 3 - j, row)
            dcw_ref[j:j + 1, :] = jnp.sum(dxconv * src, axis=0, keepdims=True)
        dcb_ref[...] = jnp.sum(dxconv, axis=0, keepdims=True)

    col = lambda off: pl.BlockSpec((S, 128), lambda j: (0, off + j))
    vec = pl.BlockSpec((1, 128), lambda j: (0, j))
    mat = pl.BlockSpec((None, 128, 128), lambda j: (j, 0, 0))
    vshape = jax.ShapeDtypeStruct((1, W), F32)
    mshape = jax.ShapeDtypeStruct((8, 128, 128), F32)
    return _pcall(
        body, grid=(8,),
        in_specs=[col(0), col(8), col(0), pl.BlockSpec((None, S, 128), lambda j: (0, 0, j)),
                  pl.BlockSpec((4, 128), lambda j: (0, j)), vec, mat, mat, mat, mat, vec, vec, vec],
        out_specs=[pl.BlockSpec((2, S, 128), lambda j: (0, 0, j)), pl.BlockSpec((4, 128), lambda j: (0, j)), vec,
                   mat, mat, vec, vec, vec],
        out_shape=[jax.ShapeDtypeStruct((4, S, W), BF), jax.ShapeDtypeStruct((4, W), F32), vshape, mshape, mshape,
                   vshape, vshape, vshape],
        name=name, vmem=56, comm=comm,
    )(h, h, hst, dmix3, cw, cb.reshape(1, W), wa, wx, wat, wxt, ba.reshape(1, W), bx.reshape(1, W), lam.reshape(1, W))


def _odd_d_bwd(h, dmix3, dh4, wp, wpt, dscale, name):
    S = h.shape[0]

    def body(xd_ref, dg_ref, dd_ref, wp_ref, wpt_ref, ds_ref, dh_in, dh_ref, dwp_ref, dds_ref):
        g = pl.program_id(0)
        row = lax.broadcasted_iota(jnp.int32, (S, 256), 0)
        xd = xd_ref[...]
        cnt = jnp.minimum(row + 1, jnp.left_shift(2, g)).astype(F32)
        pooled = _pool_sums(xd, g, row, _shift_down) / cnt - xd
        pb = pooled.astype(BF)
        mixed = _dot(pb, wp_ref[...])
        dg = dg_ref[...]
        sg, dsg = _silu_grad(dg)
        dd = dd_ref[...]
        dmixed = dd * ds_ref[...] * sg
        dds_ref[...] = jnp.sum(dd * mixed * sg, axis=0, keepdims=True)
        dh_ref[1] = (dd * mixed * ds_ref[...] * dsg).astype(BF)
        dmb = dmixed.astype(BF)
        dpooled = _dot(dmb, wpt_ref[...])
        dwp_ref[...] = _dot_tn(pb, dmb)
        dh_ref[0] = (_pool_sums(dpooled / cnt, g, row, _shift_up) - dpooled).astype(BF)

    col = lambda off: pl.BlockSpec((S, 256), lambda g: (0, off + g))
    mat = pl.BlockSpec((None, 256, 256), lambda g: (g, 0, 0))
    vec = pl.BlockSpec((1, 256), lambda g: (0, g))
    return pl.pallas_call(
        body, grid=(4,),
        in_specs=[col(8), col(12), pl.BlockSpec((None, S, 256), lambda g: (1, 0, g)), mat, mat, vec, ANY],
        out_specs=[pl.BlockSpec((2, S, 256), lambda g: (1, 0, g)), mat, vec],
        out_shape=[jax.ShapeDtypeStruct((4, S, W), BF), jax.ShapeDtypeStruct((4, 256, 256), F32),
                   jax.ShapeDtypeStruct((1, W), F32)],
        input_output_aliases={6: 0}, name=name, compiler_params=_cp(56),
    )(h, h, dmix3, wp, wpt, dscale.reshape(1, W), dh4)


def _peer(d):
    x, y, c = lax.axis_index("x"), lax.axis_index("y"), lax.axis_index("c")
    px = 1 - x if d & 4 else x
    py = 1 - y if d & 2 else y
    pc = 1 - c if d & 1 else c
    return (px, py, pc), 4 * px + 2 * py + pc


class _GatherAll(_Comm):
    def __init__(self, xs):
        self.inputs = [xs]
        self.out_shapes = [jax.ShapeDtypeStruct((N_DEV,) + xs.shape, xs.dtype)]
        self.sem_shapes = [pltpu.SemaphoreType.DMA((N_DEV - 1,)), pltpu.SemaphoreType.DMA((N_DEV - 1,)),
                           pltpu.SemaphoreType.DMA]

    def copies(self, ins, outs, sems):
        (x_ref,), (out_ref,), (send, recv, loc) = ins, outs, sems
        _, me = _peer(0)
        res = [pltpu.make_async_copy(x_ref, out_ref.at[me], loc)]
        for d in range(1, N_DEV):
            peer, _ = _peer(d)
            res.append(pltpu.make_async_remote_copy(src_ref=x_ref, dst_ref=out_ref.at[me], send_sem=send.at[d - 1],
                                                    recv_sem=recv.at[d - 1], device_id=peer, device_id_type=MESH))
        return res


class _ExchangeAll(_Comm):
    def __init__(self, g8):
        self.inputs = [g8]
        self.out_shapes = [jax.ShapeDtypeStruct(g8.shape, g8.dtype)]
        self.sem_shapes = [pltpu.SemaphoreType.DMA((N_DEV - 1,)), pltpu.SemaphoreType.DMA((N_DEV - 1,)),
                           pltpu.SemaphoreType.DMA]

    def copies(self, ins, outs, sems):
        (g_ref,), (out_ref,), (send, recv, loc) = ins, outs, sems
        _, me = _peer(0)
        res = [pltpu.make_async_copy(g_ref.at[me], out_ref.at[0], loc)]
        for d in range(1, N_DEV):
            peer, pidx = _peer(d)
            res.append(pltpu.make_async_remote_copy(src_ref=g_ref.at[pidx], dst_ref=out_ref.at[d], send_sem=send.at[d - 1],
                                                    recv_sem=recv.at[d - 1], device_id=peer, device_id_type=MESH))
        return res


def _sum8(r8, tr, name):
    _, R, C = r8.shape
    tr = min(tr, R)
    assert R % tr == 0

    def body(r_ref, o_ref):
        acc = r_ref[0]
        for d in range(1, N_DEV):
            acc = acc + r_ref[d]
        o_ref[...] = acc

    return pl.pallas_call(
        body, grid=(R // tr,), in_specs=[pl.BlockSpec((N_DEV, tr, C), lambda i: (0, i, 0))],
        out_specs=pl.BlockSpec((tr, C), lambda i: (i, 0)), out_shape=jax.ShapeDtypeStruct((R, C), F32),
        name=name, compiler_params=_cp(),
    )(r8)


def _adamw_math(w, g, m, v):
    m2 = B1 * m + (1.0 - B1) * g
    v2 = B2 * v + (1.0 - B2) * (g * g)
    m_hat = m2 / (1.0 - B1 ** STEP)
    v_hat = v2 / (1.0 - B2 ** STEP)
    return -LR * (m_hat / (jnp.sqrt(v_hat) + ADAM_EPS) + WD * w), m2, v2


def _adamw_many(ws, gs, ms, vs, name):
    n = len(ws)

    def body(*refs):
        for i in range(n):
            d, m2, v2 = _adamw_math(refs[i][...], refs[n + i][...], refs[2 * n + i][...], refs[3 * n + i][...])
            refs[4 * n + i][...] = d
            refs[5 * n + i][...] = m2
            refs[6 * n + i][...] = v2

    vmem = pl.BlockSpec(memory_space=pltpu.VMEM)
    shapes = [jax.ShapeDtypeStruct(w.shape, F32) for w in ws]
    res = pl.pallas_call(body, in_specs=[vmem] * (4 * n), out_specs=[vmem] * (3 * n), out_shape=shapes * 3, name=name,
                         compiler_params=_cp())(*ws, *gs, *ms, *vs)
    return res[:n], res[n:2 * n], res[2 * n:]


def _adamw(w, g, m, v, tr, name, comm=None):
    R, C = w.shape
    tr = min(tr, R)

    def body(w_ref, g_ref, m_ref, v_ref, d_ref, m2_ref, v2_ref):
        d_ref[...], m2_ref[...], v2_ref[...] = _adamw_math(w_ref[...], g_ref[...], m_ref[...], v_ref[...])

    blk = pl.BlockSpec((tr, C), lambda i: (i, 0))
    shp = jax.ShapeDtypeStruct((R, C), F32)
    return _pcall(body, grid=(R // tr,), in_specs=[blk] * 4, out_specs=[blk] * 3, out_shape=[shp] * 3,
                  name=name, comm=comm)(w, g, m, v)


def _rep_pack(a):
    n = a.size
    pad = (-n) % 1024
    f = a.reshape(-1)
    if pad:
        f = jnp.concatenate([f, jnp.zeros((pad,), a.dtype)])
    return f.reshape(N_DEV, -1, 128)


def _rep_unpack(p, shape):
    n = 1
    for s in shape:
        n *= s
    return p.reshape(-1)[:n].reshape(shape)


def _sh_pack(a, axis):
    shp = a.shape
    a = a.reshape(shp[:axis] + (N_DEV, shp[axis] // N_DEV) + shp[axis + 1:])
    return jnp.moveaxis(a, axis, 0).reshape(N_DEV, -1, 128)


def _sh_unpack(p, shape, axis):
    a = p.reshape((N_DEV,) + shape[:axis] + (shape[axis] // N_DEV,) + shape[axis + 1:])
    return jnp.moveaxis(a, 0, axis).reshape(shape)


def _pad_rows(a, mult=8):
    pad = (-a.shape[-2]) % mult
    if pad:
        a = jnp.concatenate([a, jnp.zeros(a.shape[:-2] + (pad, a.shape[-1]), a.dtype)], axis=-2)
    return a


REP = ["even_a_ln_g", "even_a_ln_b", "even_a_ws", "even_a_bs", "even_b_sinks", "even_ln_g", "even_ln_b",
       "odd_w_a", "odd_w_x"]
SH = [("odd_conv_w", (2, 4, W), 2), ("odd_conv_b", (2, W), 1), ("odd_b_a", (2, W), 1), ("odd_b_x", (2, W), 1),
      ("odd_lam", (2, W), 1), ("odd_w_pool", (2, 4, 256, 256), 2), ("odd_d_scale", (2, W), 1),
      ("odd_ln_g", (2, D), 1), ("odd_ln_b", (2, D), 1)]
BIG = ["even_w_in", "even_w_out", "odd_w_in", "odd_w_out"]
NAMES = ["even_w_in", "even_a_ln_g", "even_a_ln_b", "even_a_ws", "even_a_bs", "even_b_sinks", "even_w_out",
         "even_ln_g", "even_ln_b", "odd_w_in", "odd_conv_w", "odd_conv_b", "odd_w_a", "odd_b_a", "odd_w_x", "odd_b_x",
         "odd_lam", "odd_w_pool", "odd_d_scale", "odd_w_out", "odd_ln_g", "odd_ln_b"]


def _rope_table(positions):
    S = positions.shape[0]
    inv = ROPE_THETA ** (-jnp.arange(0, 16, 2, dtype=F32) / 16)
    ang = positions.astype(F32)[:, None] * inv
    cos, sin = jnp.cos(ang), jnp.sin(ang)
    one, zero = jnp.ones((S, 48), F32), jnp.zeros((S, 48), F32)
    z8 = jnp.zeros((S, 8), F32)
    c64 = jnp.concatenate([cos, cos, one], axis=1)
    s1 = jnp.concatenate([-sin, z8, zero], axis=1)
    s2 = jnp.concatenate([z8, sin, zero], axis=1)
    return jnp.concatenate([c64, c64, s1, s1, s2, s2], axis=1)


def kernel(x, positions, even_w_in, even_a_ln_g, even_a_ln_b, even_a_ws, even_a_bs, even_b_sinks, even_w_out, even_ln_g, even_ln_b, odd_w_in, odd_conv_w, odd_conv_b, odd_w_a, odd_b_a, odd_w_x, odd_b_x, odd_lam, odd_w_pool, odd_d_scale, odd_w_out, odd_ln_g, odd_ln_b, loss_target, m_even_w_in, m_even_a_ln_g, m_even_a_ln_b, m_even_a_ws, m_even_a_bs, m_even_b_sinks, m_even_w_out, m_even_ln_g, m_even_ln_b, m_odd_w_in, m_odd_conv_w, m_odd_conv_b, m_odd_w_a, m_odd_b_a, m_odd_w_x, m_odd_b_x, m_odd_lam, m_odd_w_pool, m_odd_d_scale, m_odd_w_out, m_odd_ln_g, m_odd_ln_b, v_even_w_in, v_even_a_ln_g, v_even_a_ln_b, v_even_a_ws, v_even_a_bs, v_even_b_sinks, v_even_w_out, v_even_ln_g, v_even_ln_b, v_odd_w_in, v_odd_conv_w, v_odd_conv_b, v_odd_w_a, v_odd_b_a, v_odd_w_x, v_odd_b_x, v_odd_lam, v_odd_w_pool, v_odd_d_scale, v_odd_w_out, v_odd_ln_g, v_odd_ln_b):
    args = (even_w_in, even_a_ln_g, even_a_ln_b, even_a_ws, even_a_bs, even_b_sinks, even_w_out, even_ln_g, even_ln_b,
            odd_w_in, odd_conv_w, odd_conv_b, odd_w_a, odd_b_a, odd_w_x, odd_b_x, odd_lam, odd_w_pool, odd_d_scale,
            odd_w_out, odd_ln_g, odd_ln_b)
    margs = (m_even_w_in, m_even_a_ln_g, m_even_a_ln_b, m_even_a_ws, m_even_a_bs, m_even_b_sinks, m_even_w_out,
             m_even_ln_g, m_even_ln_b, m_odd_w_in, m_odd_conv_w, m_odd_conv_b, m_odd_w_a, m_odd_b_a, m_odd_w_x,
             m_odd_b_x, m_odd_lam, m_odd_w_pool, m_odd_d_scale, m_odd_w_out, m_odd_ln_g, m_odd_ln_b)
    vargs = (v_even_w_in, v_even_a_ln_g, v_even_a_ln_b, v_even_a_ws, v_even_a_bs, v_even_b_sinks, v_even_w_out,
             v_even_ln_g, v_even_ln_b, v_odd_w_in, v_odd_conv_w, v_odd_conv_b, v_odd_w_a, v_odd_b_a, v_odd_w_x,
             v_odd_b_x, v_odd_lam, v_odd_w_pool, v_odd_d_scale, v_odd_w_out, v_odd_ln_g, v_odd_ln_b)
    wts = dict(zip(NAMES, args))
    mom = dict(zip(NAMES, margs))
    var = dict(zip(NAMES, vargs))
    S = x.shape[1]
    x0 = x[0]
    rope = _rope_table(positions[0])

    kinds = ("even", "odd", "even", "odd")
    blk_in = [jnp.transpose(wts[kinds[l] + "_w_in"][l // 2]).astype(BF) for l in range(4)]
    blk_out = [wts[kinds[l] + "_w_out"][l // 2].astype(BF) for l in range(4)]
    sh_local = _pad_rows(jnp.concatenate([wts[nm].reshape(-1, 128) for nm, _, _ in SH], axis=0), 16)
    me = 4 * lax.axis_index("x") + 2 * lax.axis_index("y") + lax.axis_index("c")
    own_slot = lambda blk: lax.dynamic_update_slice(lax.empty((N_DEV,) + blk.shape, blk.dtype), blk[None], (me, 0, 0))
    reg = {"blk_small": sh_local, "w_small": own_slot(sh_local)}
    sched = _Sched(reg)
    for l in range(4):
        reg[f"blk_in{l}"], reg[f"blk_out{l}"] = blk_in[l], blk_out[l]
        reg[f"w_in{l}"], reg[f"w_out{l}"] = own_slot(blk_in[l]), own_slot(blk_out[l])
    sched.add(_rows("blk_in0", "w_in0", "ag1", blk_in[0].shape[0], ROW_CHUNK[blk_in[0].shape[0]]))
    sched.add(_rows("blk_small", "w_small", "ag1", sh_local.shape[0], sh_local.shape[0]))
    for l in range(4):
        sched.add(_rows(f"blk_out{l}", f"w_out{l}", "ag1", D // N_DEV, ROW_CHUNK[D // N_DEV]))
        if l < 3:
            r = blk_in[l + 1].shape[0]
            sched.add(_rows(f"blk_in{l + 1}", f"w_in{l + 1}", "ag1", r, ROW_CHUNK[r]))

    def gathered(dst, blk):
        sched.flush(dst, FLUSH_EXTRA_US)
        return reg.pop(dst)

    wt_in0 = gathered("w_in0", blk_in[0]).reshape(-1, D)
    full = {nm: wts[nm] for nm in REP}

    def gather_small():
        sh_all = gathered("w_small", sh_local)
        off = 0
        for nm, shape, axis in SH:
            r = wts[nm].size // 128
            full[nm] = _sh_unpack(sh_all[:, off:off + r, :], shape, axis)
            off += r

    saved = []
    wt_in, w_out = [wt_in0, None, None, None], [None] * 4
    xf, xb = x0, x0.astype(BF)
    fwd = lambda name: FWD_OVERBOOK * CARRY_US[name]
    for layer in range(4):
        j = layer // 2
        kind = kinds[layer]
        if wt_in[layer] is None:
            wt_in[layer] = gathered(f"w_in{layer}", blk_in[layer]).reshape(-1, D)
        h = sched.run(_mm_nt, fwd("mm_h_" + kind), xb, wt_in[layer], 1024, 768 if kind == "even" else 512, "mm_h_" + kind)
        if kind == "even":
            bsb = jnp.broadcast_to(full["even_a_bs"][j][:, :, None], (8, 128, 128))
            mix3, o, l = sched.run(_even_fwd, fwd("even_fwd"), h, rope, full["even_a_ln_g"][j], full["even_a_ln_b"][j],
                                   full["even_a_ws"][j], bsb, full["even_b_sinks"][j], "even_fwd")
            extra = (o, l, bsb)
        else:
            if "odd_lam" not in full:
                gather_small()
            wa, wx = full["odd_w_a"][j].astype(BF), full["odd_w_x"][j].astype(BF)
            wp = full["odd_w_pool"][j].astype(BF)
            mix3, hst = sched.run(_odd_c_fwd, fwd("odd_c_fwd"), h, full["odd_conv_w"][j], full["odd_conv_b"][j], wa, wx,
                                  full["odd_b_a"][j], full["odd_b_x"][j], full["odd_lam"][j], "odd_c_fwd")
            mix3 = _odd_d_fwd(h, mix3, wp, full["odd_d_scale"][j], "odd_d_fwd")
            extra = (hst, wa, wx, wp)
        w_out[layer] = gathered(f"w_out{layer}", blk_out[layer]).reshape(D, D)
        z, xn, xnb = sched.run(_mm_out_ln, fwd("mm_out_ln"), mix3, w_out[layer], xf, full[kind + "_ln_g"][j],
                               full[kind + "_ln_b"][j], "mm_out_ln")
        saved.append((xb, h, mix3, z, extra))
        xf, xb = xn, xnb

    dxn, part = _loss_grad(xf, loss_target[0])
    loss = lax.psum(part[0, 0] * (0.5 / D), ("x", "y", "c"))

    gsum = {nm: [None, None] for nm in NAMES}

    chip_sums = {}
    sched.overhang = 0.15

    waiting = []

    def chip_sum(g, tag, key):
        r = g.shape[0] // N_DEV
        reg["g_" + key] = g.reshape(N_DEV, r, D)
        sched.add(_rows("g_" + key, "d_" + key, "rsd", r, r), first=True)
        waiting.append((key, tag))

    def add_arrived():
        for key, tag in list(waiting):
            if "d_" + key in reg and not sched.pending("d_" + key):
                waiting.remove((key, tag))
                g8 = reg.pop("g_" + key)
                chip_sums[key] = reg["s_" + key] = _add_pairs(g8, reg.pop("d_" + key), "rs_add_" + tag)
                sched.add(_rows("s_" + key, "r_" + key, "rs", g8.shape[1], ROW_CHUNK[g8.shape[1]] // 2))

    sched.after_landing = add_arrived

    def reduced(key, name):
        sched.flush("d_" + key, FLUSH_EXTRA_US)
        sched.flush("r_" + key, FLUSH_EXTRA_US)
        return _rs_final(chip_sums[key], reg.pop("r_" + key), name)

    for layer in (3, 2, 1, 0):
        j = layer // 2
        xb, h, mix3, z, extra = saved[layer]
        kind = kinds[layer]
        dz, dzb, dg, dbeta = sched.run(_ln_bwd, CARRY_US["ln_bwd"], dxn, z, full[kind + "_ln_g"][j], "ln_bwd")
        gsum[kind + "_ln_g"][j] = dg.reshape(D)
        gsum[kind + "_ln_b"][j] = dbeta.reshape(D)
        chip_sum(sched.run(_mm_tn, CARRY_US["mm_dw_out"], mix3, dzb, 512, "mm_dw_out"), "w_out", f"out{layer}")
        dmix3 = sched.run(_mm_nt, CARRY_US["mm_dmix"], dzb, w_out[layer], 1024, 512, "mm_dmix", out3=True)
        if kind == "even":
            o, l, bsb = extra
            ws = full["even_a_ws"][j]
            dh, dws, dbs, dlng, dlnb, dsink = sched.run(
                _even_bwd, CARRY_US["even_bwd"], h, dmix3, o, l, rope, full["even_a_ln_g"][j], full["even_a_ln_b"][j],
                ws, jnp.swapaxes(ws, 1, 2), bsb, full["even_b_sinks"][j], "even_bwd")
            gsum["even_a_ws"][j] = dws
            gsum["even_a_bs"][j] = jnp.transpose(dbs[:, :8])
            gsum["even_a_ln_g"][j] = dlng.reshape(W)
            gsum["even_a_ln_b"][j] = dlnb.reshape(W)
            gsum["even_b_sinks"][j] = dsink[0, :16]
            if layer == 0:
                rep_rows = [_rep_pack(jnp.stack(gsum[nm]).reshape(wts[nm].shape)) for nm in REP]
                sh_rows = [_sh_pack(jnp.stack(gsum[nm]).reshape(shape), axis) for nm, shape, axis in SH]
                packed = _pad_rows(jnp.concatenate(rep_rows + sh_rows, axis=1))
                gw, (small8,) = _mm_tn(dh, xb, 384, "mm_dw_in_even", comm=_ExchangeAll(packed))
            else:
                gw = sched.run(_mm_tn, CARRY_US["mm_dw_in_even"], dh, xb, 384, "mm_dw_in_even")
            chip_sum(gw, "w_in_even", f"in{layer}")
            if layer == 0:
                sched.flush("d_in0", FLUSH_EXTRA_US)
                sched.overhang = 0.6
            dxn = sched.run(_mm_nn_res, CARRY_US["mm_dx_even"], dh, wt_in[layer], dz, 512, 512, "mm_dx_even")
        else:
            hst, wa, wx, wp = extra
            dh4, dcw, dcb, dwa, dwx, dba, dbx, dlam = sched.run(
                _odd_c_bwd, CARRY_US["odd_c_bwd"], h, hst, dmix3, full["odd_conv_w"][j], full["odd_conv_b"][j], wa, wx,
                jnp.swapaxes(wa, 1, 2), jnp.swapaxes(wx, 1, 2), full["odd_b_a"][j], full["odd_b_x"][j], full["odd_lam"][j],
                "odd_c_bwd")
            dh4, dwp, dds = _odd_d_bwd(h, dmix3, dh4, wp, jnp.swapaxes(wp, 1, 2), full["odd_d_scale"][j], "odd_d_bwd")
            gsum["odd_conv_w"][j], gsum["odd_conv_b"][j] = dcw, dcb.reshape(W)
            gsum["odd_w_a"][j], gsum["odd_w_x"][j] = dwa, dwx
            gsum["odd_b_a"][j], gsum["odd_b_x"][j], gsum["odd_lam"][j] = dba.reshape(W), dbx.reshape(W), dlam.reshape(W)
            gsum["odd_w_pool"][j], gsum["odd_d_scale"][j] = dwp, dds.reshape(W)
            chip_sum(sched.run(_mm_tn, CARRY_US["mm_dw_in_odd"], dh4, xb, 512, "mm_dw_in_odd"), "w_in_odd", f"in{layer}")
            dxn = sched.run(_mm_nn_res, CARRY_US["mm_dx_odd"], dh4, wt_in[layer], dz, 512, 512, "mm_dx_odd")
    grad_x = dxn[None]

    n_rep = sum(p.shape[1] for p in rep_rows)
    red = _sum8(small8, 1 << 20, "sum_small")
    out_g, out_d, out_m, out_v = {}, {}, {}, {}
    for nm, kind, what, layers in (("odd_w_out", "odd", "out", (1, 3)), ("even_w_out", "even", "out", (0, 2)),
                                   ("odd_w_in", "odd", "in", (1, 3)), ("even_w_in", "even", "in", (0, 2))):
        gl = [reduced(f"{what}{l}", f"rs_final_w_{what}_{kind}") for l in layers]
        g = jnp.stack([jnp.transpose(a) for a in gl] if what == "in" else gl)
        shp = wts[nm].shape
        operands = (wts[nm].reshape(-1, shp[-1]), g.reshape(-1, shp[-1]), mom[nm].reshape(-1, shp[-1]),
                    var[nm].reshape(-1, shp[-1]), 512, f"adamw_{nm}")
        if nm == "even_w_in":
            (d2, m2, v2), (rep_all,) = _adamw(*operands, comm=_GatherAll(_pad_rows(red[:n_rep])))
        else:
            d2, m2, v2 = sched.run(_adamw, CARRY_US["adamw_" + nm], *operands)
        out_g[nm], out_d[nm], out_m[nm], out_v[nm] = g, d2.reshape(shp), m2.reshape(shp), v2.reshape(shp)

    g_small = {}
    off = 0
    for nm, p in zip(REP, rep_rows):
        r = p.shape[1]
        g_small[nm] = _rep_unpack(rep_all[:, off:off + r, :], wts[nm].shape)
        off += r
    off = n_rep
    for (nm, shape, axis), p in zip(SH, sh_rows):
        r = p.shape[1]
        g_small[nm] = red[off:off + r].reshape(wts[nm].shape)
        off += r

    def rows(a):
        f = a.reshape(-1)
        pad = (-f.shape[0]) % 128
        if pad:
            f = jnp.concatenate([f, jnp.zeros((pad,), a.dtype)])
        return f.reshape(-1, 128)

    small = REP + [nm for nm, _, _ in SH]
    each = lambda src: [rows(src[nm]) for nm in small]
    d2, m2, v2 = _adamw_many(each(wts), each(g_small), each(mom), each(var), "adamw_small")
    for i, nm in enumerate(small):
        n, shp = wts[nm].size, wts[nm].shape
        take = lambda a: a.reshape(-1)[:n].reshape(shp)
        out_g[nm], out_d[nm], out_m[nm], out_v[nm] = g_small[nm], take(d2[i]), take(m2[i]), take(v2[i])

    return (loss, grad_x, *[out_g[nm] for nm in NAMES], *[out_d[nm] for nm in NAMES],
            *[out_m[nm] for nm in NAMES], *[out_v[nm] for nm in NAMES])
```

```python
import functools

import jax
import jax.numpy as jnp
from jax import lax
from jax.experimental import pallas as pl
from jax.experimental.pallas import tpu as pltpu

F32 = jnp.float32
BF = jnp.bfloat16
MESH = pl.DeviceIdType.MESH
ANY = pl.BlockSpec(memory_space=pl.ANY)

N_DEV = 8
D = 2048
W = 1024
EVEN_IN = 5376
ODD_IN = 4096
CHUNK = 128
ALPHA = (2 * 4) ** 0.25
LN_EPS = 1e-5
ROPE_THETA = 500000.0
LRU_C = 8.0
LR, B1, B2, ADAM_EPS, WD, STEP = 0.001, 0.9, 0.999, 1e-08, 0.01, 10
NEG = -1e30
HEAD_COLS = 4


def _cp(vmem_mb=48):
    return pltpu.CompilerParams(vmem_limit_bytes=vmem_mb * 1024 * 1024)


def _sig(x):
    return jax.nn.sigmoid(x)


def _silu_grad(x):
    s = _sig(x)
    return x * s, s * (1.0 + x * (1.0 - s))


def _dot(a, b):
    return jnp.dot(a, b, preferred_element_type=F32)


def _dot_nt(a, b):
    return lax.dot_general(a, b, (((1,), (1,)), ((), ())), preferred_element_type=F32)


def _dot_tn(a, b):
    return lax.dot_general(a, b, (((0,), (0,)), ((), ())), preferred_element_type=F32)


def _coords():
    return lax.axis_index("x"), lax.axis_index("y"), lax.axis_index("c")


def _chip(j):
    x, y, _ = _coords()
    return (1 - x if j & 2 else x), (1 - y if j & 1 else y)


class _Comm:
    def start(self, ins, outs, sems):
        for cp in self.copies(ins, outs, sems):
            cp.start()

    def wait(self, ins, outs, sems):
        for cp in self.copies(ins, outs, sems):
            cp.wait()


class _Join(_Comm):
    def __init__(self, parts):
        self.parts = list(parts)
        self.inputs = [a for p in self.parts for a in p.inputs]
        self.out_shapes = [s for p in self.parts for s in p.out_shapes]
        self.sem_shapes = [s for p in self.parts for s in p.sem_shapes]
        self.aliases = {}
        i0 = o0 = 0
        for p in self.parts:
            for i, o in getattr(p, "aliases", {}).items():
                self.aliases[i0 + i] = o0 + o
            i0, o0 = i0 + len(p.inputs), o0 + len(p.out_shapes)

    def copies(self, ins, outs, sems):
        res = []
        i0 = o0 = s0 = 0
        for p in self.parts:
            ni, no, ns = len(p.inputs), len(p.out_shapes), len(p.sem_shapes)
            res += p.copies(ins[i0:i0 + ni], outs[o0:o0 + no], sems[s0:s0 + ns])
            i0, o0, s0 = i0 + ni, o0 + no, s0 + ns
        return res


ROWS_US = {"ag1": 0.104, "ag2": 0.052, "agd": 0.027, "rsd": 0.027, "rs": 0.205}
N_COPIES = {"ag1": 2, "ag2": 2, "agd": 4, "rsd": 4, "rs": 3}
ROW_CHUNK = {672: 224, 512: 128, 256: 128}
CARRY_US = {"mm_h_even": 58, "mm_h_odd": 47, "even_fwd": 42, "odd_c_fwd": 37, "mm_out_ln": 33, "ln_bwd": 23, "mm_dmix": 26,
            "mm_dw_out": 25, "even_bwd": 95, "odd_c_bwd": 70, "mm_dw_in_even": 56, "mm_dw_in_odd": 44, "mm_dx_even": 60,
            "mm_dx_odd": 50, "adamw_even_w_in": 30, "adamw_odd_w_in": 28, "adamw_even_w_out": 11, "adamw_odd_w_out": 11}
FWD_OVERBOOK = 1.15
FLUSH_EXTRA_US = 60.0


def _cost_us(task, reg):
    kind, src, _, lo, hi = task
    return ROWS_US[kind] * (hi - lo) * reg[src].shape[-1] * reg[src].dtype.itemsize / 4096.0


class _Copies(_Comm):
    def __init__(self, tasks, reg):
        self.tasks = list(tasks)
        self.out_names, self.in_names = [], []
        for kind, src, dst, lo, hi in self.tasks:
            if dst not in self.out_names:
                self.out_names.append(dst)
        for kind, src, dst, lo, hi in self.tasks:
            if src not in self.out_names and src not in self.in_names:
                self.in_names.append(src)
        self.out_shapes, self.aliases = [], {}
        for o, dst in enumerate(self.out_names):
            if dst in reg:
                self.aliases[len(self.in_names)] = o
                self.in_names.append(dst)
                self.out_shapes.append(jax.ShapeDtypeStruct(reg[dst].shape, reg[dst].dtype))
            else:
                kind, src = next((t[0], t[1]) for t in self.tasks if t[2] == dst)
                shape = ({"rsd": 4, "rs": 3}[kind],) + reg[src].shape[1:]
                self.out_shapes.append(jax.ShapeDtypeStruct(shape, reg[src].dtype))
        self.inputs = [reg[nm] for nm in self.in_names]
        n = sum(N_COPIES[t[0]] for t in self.tasks)
        self.sem_shapes = [pltpu.SemaphoreType.DMA((n,)), pltpu.SemaphoreType.DMA((n,))]

    def copies(self, ins, outs, sems):
        send, recv = sems
        x, y, c = _coords()
        me = 4 * x + 2 * y + c
        xn, yn = (1 - x, y, c), (x, 1 - y, c)
        at_xn, at_yn = 4 * (1 - x) + 2 * y + c, 4 * x + 2 * (1 - y) + c
        ref = dict(zip(self.in_names, ins))
        ref.update(zip(self.out_names, outs))
        res = []

        def copy(src, dst, to):
            i = len(res)
            res.append(pltpu.make_async_remote_copy(src_ref=src, dst_ref=dst, send_sem=send.at[i], recv_sem=recv.at[i],
                                                    device_id=to, device_id_type=MESH))

        for kind, src, dst, lo, hi in self.tasks:
            n = hi - lo
            if kind == "ag1":
                for to in (xn, yn):
                    copy(ref[src].at[pl.ds(lo, n)], ref[dst].at[me, pl.ds(lo, n)], to)
            elif kind == "ag2":
                h = n // 2
                first, second = ref[dst].at[at_xn, pl.ds(lo, h)], ref[dst].at[at_yn, pl.ds(lo + h, n - h)]
                copy(first, first, yn)
                copy(second, second, xn)
            elif kind == "agd":
                for j in range(4):
                    px, py = _chip(j)
                    rows = ref[dst].at[4 * px + 2 * py + c, pl.ds(lo, n)]
                    copy(rows, rows, (x, y, 1 - c))
            elif kind == "rsd":
                for j in range(4):
                    px, py = _chip(j)
                    copy(ref[src].at[4 * px + 2 * py + 1 - c, pl.ds(lo, n)], ref[dst].at[j, pl.ds(lo, n)], (x, y, 1 - c))
            else:
                for j in (1, 2, 3):
                    px, py = _chip(j)
                    copy(ref[src].at[j, pl.ds(lo, n)], ref[dst].at[j - 1, pl.ds(lo, n)], (px, py, c))
        return res


class _Sched:
    def __init__(self, reg):
        self.reg, self.queue, self.later = reg, [], []
        self.overhang = 0.5
        self.after_landing = None

    def add(self, tasks, first=False):
        self.queue = list(tasks) + self.queue if first else self.queue + list(tasks)

    def pending(self, dst):
        return any(t[2] == dst for t in self.queue + self.later)

    def take(self, budget_us, must=None, overhang=0.5):
        self.queue, self.later = self.later + self.queue, []
        picked, us = [], 0.0
        rest = []
        for t in self.queue:
            cost = _cost_us(t, self.reg)
            if (must is not None and t[2] == must) or us + (1.0 - overhang) * cost <= budget_us:
                picked.append(t)
                us += cost
                if t[0] in ("ag1", "ag2"):
                    self.later.append(({"ag1": "ag2", "ag2": "agd"}[t[0]], t[2], t[2], t[3], t[4]))
            else:
                rest.append(t)
        self.queue = rest
        return _Copies(picked, self.reg) if picked else None

    def landed(self, comm, got):
        if comm is not None:
            for nm, a in zip(comm.out_names, got):
                self.reg[nm] = a
        if self.after_landing is not None:
            self.after_landing()

    def run(self, builder, budget_us, *args, **kw):
        comm = self.take(budget_us, overhang=self.overhang)
        res, got = builder(*args, comm=comm, **kw)
        self.landed(comm, got)
        return res

    def flush(self, dst, budget_us=0.0, beside=None):
        res = []
        while self.pending(dst):
            comm = self.take(budget_us, must=dst)
            got = _comm_only(comm if beside is None else _Join([comm, beside]), "flush_" + dst)
            res, beside = got[len(comm.out_shapes):], None
            self.landed(comm, got[:len(comm.out_shapes)])
        return res


def _rows(name_src, name_dst, kind, n_rows, chunk):
    return [(kind, name_src, name_dst, lo, min(lo + chunk, n_rows)) for lo in range(0, n_rows, chunk)]


def _pcall(body, *, grid, in_specs, out_specs, out_shape, name, scratch=(), vmem=48, comm=None):
    in_specs, out_specs, out_shape, scratch = list(in_specs), list(out_specs), list(out_shape), list(scratch)
    if comm is None:
        call = pl.pallas_call(body, grid=grid, in_specs=in_specs, out_specs=out_specs, out_shape=out_shape,
                              scratch_shapes=scratch, name=name, compiler_params=_cp(vmem))
        return lambda *args: (call(*args), [])
    n_in, n_out, n_scr = len(in_specs), len(out_specs), len(scratch)
    c_in, c_out = len(comm.inputs), len(comm.out_shapes)
    aliases = {n_in + i: n_out + o for i, o in getattr(comm, "aliases", {}).items()}

    def wrapped(*refs):
        ins, cins = refs[:n_in], refs[n_in:n_in + c_in]
        o0 = n_in + c_in
        outs, couts = refs[o0:o0 + n_out], refs[o0 + n_out:o0 + n_out + c_out]
        s0 = o0 + n_out + c_out
        scr, sems = refs[s0:s0 + n_scr], refs[s0 + n_scr:]
        ids = [pl.program_id(a) for a in range(len(grid))]
        first = functools.reduce(jnp.logical_and, [i == 0 for i in ids])
        last = functools.reduce(jnp.logical_and, [i == g - 1 for i, g in zip(ids, grid)])

        @pl.when(first)
        def _():
            comm.start(cins, couts, sems)

        body(*ins, *outs, *scr)

        @pl.when(last)
        def _():
            comm.wait(cins, couts, sems)

    call = pl.pallas_call(wrapped, grid=grid, in_specs=in_specs + [ANY] * c_in, out_specs=out_specs + [ANY] * c_out,
                          out_shape=out_shape + list(comm.out_shapes), scratch_shapes=scratch + list(comm.sem_shapes),
                          input_output_aliases=aliases, name=name, compiler_params=_cp(vmem))

    def run(*args):
        res = call(*args, *comm.inputs)
        return res[:n_out], res[n_out:]

    return run


def _comm_only(comm, name):
    c_in, c_out = len(comm.inputs), len(comm.out_shapes)

    def body(*refs):
        cins, couts, sems = refs[:c_in], refs[c_in:c_in + c_out], refs[c_in + c_out:]
        comm.start(cins, couts, sems)
        comm.wait(cins, couts, sems)

    return pl.pallas_call(body, in_specs=[ANY] * c_in, out_specs=[ANY] * c_out, out_shape=list(comm.out_shapes),
                          scratch_shapes=list(comm.sem_shapes), input_output_aliases=dict(getattr(comm, "aliases", {})),
                          name=name)(*comm.inputs)


def _chip_blocks():
    _, _, c = _coords()
    return jnp.stack([4 * px + 2 * py + c for px, py in map(_chip, range(4))]).astype(jnp.int32)


def _add_pairs(g8, b4, name):
    _, R, C = b4.shape

    def body(idx_ref, a_ref, b_ref, o_ref):
        o_ref[...] = (a_ref[...].astype(F32) + b_ref[...].astype(F32)).astype(BF)

    blk = pl.BlockSpec((None, R, C), lambda j, idx: (j, 0, 0))
    grid_spec = pltpu.PrefetchScalarGridSpec(
        num_scalar_prefetch=1, grid=(4,),
        in_specs=[pl.BlockSpec((None, R, C), lambda j, idx: (idx[j], 0, 0)), blk], out_specs=blk)
    return pl.pallas_call(body, grid_spec=grid_spec, out_shape=jax.ShapeDtypeStruct(b4.shape, BF), name=name,
                          compiler_params=_cp())(_chip_blocks(), g8, b4)


def _rs_final(s4, r3, name):
    _, R, C = s4.shape
    tr = R // 2

    def body(s_ref, r_ref, o_ref):
        o_ref[...] = ((s_ref[...].astype(F32) + r_ref[0].astype(F32)) + r_ref[1].astype(F32)) + r_ref[2].astype(F32)

    return pl.pallas_call(
        body, grid=(2,),
        in_specs=[pl.BlockSpec((None, tr, C), lambda i: (0, i, 0)), pl.BlockSpec((3, tr, C), lambda i: (0, i, 0))],
        out_specs=pl.BlockSpec((tr, C), lambda i: (i, 0)), out_shape=jax.ShapeDtypeStruct((R, C), F32),
        name=name, compiler_params=_cp())(s4, r3)


def _mm_nt(a, w, tm, tn, name, out3=False, comm=None):
    M, K = a.shape
    N = w.shape[0]
    tm = min(tm, M)

    def body(a_ref, w_ref, o_ref):
        o_ref[...] = _dot_nt(a_ref[...], w_ref[...])

    if out3:
        per = W // tn
        out_shape = jax.ShapeDtypeStruct((N // W, M, W), F32)
        out_spec = pl.BlockSpec((None, tm, tn), lambda i, j: (j // per, i, j % per))
    else:
        out_shape = jax.ShapeDtypeStruct((M, N), F32)
        out_spec = pl.BlockSpec((tm, tn), lambda i, j: (i, j))
    (res,), extra = _pcall(
        body, grid=(M // tm, N // tn),
        in_specs=[pl.BlockSpec((tm, K), lambda i, j: (i, 0)), pl.BlockSpec((tn, K), lambda i, j: (j, 0))],
        out_specs=[out_spec], out_shape=[out_shape], name=name, comm=comm)(a, w)
    return res, extra


def _mm_tn(a, b, tm, name, comm=None):
    K, N = b.shape
    if a.ndim == 3:
        M = a.shape[0] * W
        per = W // tm
        a_spec = pl.BlockSpec((None, K, tm), lambda i: (i // per, 0, i % per))
    else:
        M = a.shape[1]
        a_spec = pl.BlockSpec((K, tm), lambda i: (0, i))

    def body(a_ref, b_ref, o_ref):
        o_ref[...] = _dot_tn(a_ref[...], b_ref[...]).astype(BF)

    (out,), extra = _pcall(
        body, grid=(M // tm,),
        in_specs=[a_spec, pl.BlockSpec((K, N), lambda i: (0, 0))],
        out_specs=[pl.BlockSpec((tm, N), lambda i: (i, 0))],
        out_shape=[jax.ShapeDtypeStruct((M, N), BF)], name=name, vmem=56, comm=comm)(a, b)
    return out, extra


def _mm_nn_res(a, w, res, tm, tn, name, comm=None):
    K, N = w.shape
    if a.ndim == 3:
        P, M = a.shape[0], a.shape[1]
        tm = min(tm, M)
        a_spec = pl.BlockSpec((P, tm, W), lambda i, j: (0, i, 0))
    else:
        P, M = 0, a.shape[0]
        tm = min(tm, M)
        a_spec = pl.BlockSpec((tm, K), lambda i, j: (i, 0))

    def body(a_ref, w_ref, r_ref, o_ref):
        if P:
            d = _dot(a_ref[0], w_ref[0:W, :])
            for p in range(1, P):
                d = d + _dot(a_ref[p], w_ref[p * W:(p + 1) * W, :])
        else:
            d = _dot(a_ref[...], w_ref[...])
        o_ref[...] = ALPHA * r_ref[...] + d

    (out,), extra = _pcall(
        body, grid=(M // tm, N // tn),
        in_specs=[a_spec, pl.BlockSpec((K, tn), lambda i, j: (0, j)), pl.BlockSpec((tm, tn), lambda i, j: (i, j))],
        out_specs=[pl.BlockSpec((tm, tn), lambda i, j: (i, j))],
        out_shape=[jax.ShapeDtypeStruct((M, N), F32)], name=name, comm=comm)(a, w, res)
    return out, extra


def _mm_out_ln(mix3, w_out, x, g, b, name, comm=None):
    S = x.shape[0]
    tm = min(256, S)

    def body(m_ref, w_ref, x_ref, g_ref, b_ref, z_ref, xn_ref, xb_ref):
        acc = _dot(m_ref[0], w_ref[0:W, :]) + _dot(m_ref[1], w_ref[W:2 * W, :])
        z = ALPHA * x_ref[...] + acc
        mu = jnp.mean(z, axis=1, keepdims=True)
        zc = z - mu
        var = jnp.mean(zc * zc, axis=1, keepdims=True)
        xn = zc * lax.rsqrt(var + LN_EPS) * g_ref[...] + b_ref[...]
        z_ref[...] = z
        xn_ref[...] = xn
        xb_ref[...] = xn.astype(BF)

    row = pl.BlockSpec((tm, D), lambda i: (i, 0))
    vec = pl.BlockSpec((1, D), lambda i: (0, 0))
    return _pcall(
        body, grid=(S // tm,),
        in_specs=[pl.BlockSpec((2, tm, W), lambda i: (0, i, 0)), pl.BlockSpec((D, D), lambda i: (0, 0)), row, vec, vec],
        out_specs=[row, row, row],
        out_shape=[jax.ShapeDtypeStruct((S, D), F32), jax.ShapeDtypeStruct((S, D), F32), jax.ShapeDtypeStruct((S, D), BF)],
        name=name, comm=comm)(mix3, w_out, x, g.reshape(1, D), b.reshape(1, D))


def _ln_bwd(dxn, z, g, name, comm=None):
    S = z.shape[0]
    tm = min(256, S)

    def body(d_ref, z_ref, g_ref, dz_ref, dzb_ref, dg_ref, db_ref):
        i = pl.program_id(0)
        zz = z_ref[...]
        mu = jnp.mean(zz, axis=1, keepdims=True)
        zc = zz - mu
        var = jnp.mean(zc * zc, axis=1, keepdims=True)
        rstd = lax.rsqrt(var + LN_EPS)
        xhat = zc * rstd
        dy = d_ref[...]
        dyg = dy * g_ref[...]
        m1 = jnp.mean(dyg, axis=1, keepdims=True)
        m2 = jnp.mean(dyg * xhat, axis=1, keepdims=True)
        dz = rstd * (dyg - m1 - xhat * m2)
        dz_ref[...] = dz
        dzb_ref[...] = dz.astype(BF)

        @pl.when(i == 0)
        def _():
            dg_ref[...] = jnp.zeros_like(dg_ref)
            db_ref[...] = jnp.zeros_like(db_ref)

        dg_ref[...] += jnp.sum(dy * xhat, axis=0, keepdims=True)
        db_ref[...] += jnp.sum(dy, axis=0, keepdims=True)

    row = pl.BlockSpec((tm, D), lambda i: (i, 0))
    vec = pl.BlockSpec((1, D), lambda i: (0, 0))
    return _pcall(
        body, grid=(S // tm,), in_specs=[row, row, vec], out_specs=[row, row, vec, vec],
        out_shape=[jax.ShapeDtypeStruct((S, D), F32), jax.ShapeDtypeStruct((S, D), BF),
                   jax.ShapeDtypeStruct((1, D), F32), jax.ShapeDtypeStruct((1, D), F32)],
        name=name, comm=comm)(dxn, z, g.reshape(1, D))


def _loss_grad(xn, target):
    S = xn.shape[0]
    tm = min(256, S)

    def body(x_ref, t_ref, d_ref, p_ref):
        i = pl.program_id(0)
        e = x_ref[...] - t_ref[...]
        d_ref[...] = e * (1.0 / D)

        @pl.when(i == 0)
        def _():
            p_ref[...] = jnp.zeros_like(p_ref)

        p_ref[...] += jnp.sum(jnp.sum(e * e, axis=1, keepdims=True), axis=0, keepdims=True)

    row = pl.BlockSpec((tm, D), lambda i: (i, 0))
    return pl.pallas_call(
        body, grid=(S // tm,), in_specs=[row, row],
        out_specs=[row, pl.BlockSpec((8, 128), lambda i: (0, 0))],
        out_shape=[jax.ShapeDtypeStruct((S, D), F32), jax.ShapeDtypeStruct((8, 128), F32)],
        name="loss_grad", compiler_params=_cp(),
    )(xn, target)


def _rope_fwd(t, r_ref):
    return (t * r_ref[:, 0:128] + pltpu.roll(t, 120, 1) * r_ref[:, 128:256]
            + pltpu.roll(t, 8, 1) * r_ref[:, 256:384])


def _rope_bwd(g, r_ref):
    return (g * r_ref[:, 0:128] + pltpu.roll(g * r_ref[:, 128:256], 8, 1)
            + pltpu.roll(g * r_ref[:, 256:384], 120, 1))


def _dup_heads(kb):
    lo = lax.broadcasted_iota(jnp.int32, kb.shape, 1) < 64
    sw = pltpu.roll(kb, 64, 1)
    return [jnp.where(lo, kb, sw).astype(BF), jnp.where(lo, sw, kb).astype(BF)]


def _even_fwd(h, rope, lng, lnb, ws, bsb, sinks, name, comm=None):
    S = h.shape[0]
    nb = S // CHUNK

    def body(h_ref, hp_ref, rc_ref, rp_ref, lng_ref, lnb_ref, ws_ref, bsb_ref, sink_ref, mix_ref, o_ref, l_ref):
        n = pl.program_id(0)
        lane = lax.broadcasted_iota(jnp.int32, (128, 128), 1)
        rowi = lax.broadcasted_iota(jnp.int32, (128, 128), 0)
        tri = rowi >= lane
        lane_lo = lane < 64
        v = h_ref[:, W:2 * W]
        mu = jnp.mean(v, axis=1, keepdims=True)
        vc = v - mu
        var = jnp.mean(vc * vc, axis=1, keepdims=True)
        vn = vc * lax.rsqrt(var + LN_EPS) * lng_ref[...] + lnb_ref[...]
        ms = [_dot(jnp.where(tri, ws_ref[g], 0.0).astype(BF), vn[:, g * 128:(g + 1) * 128].astype(BF)) for g in range(8)]
        for g in range(8):
            sl = slice(g * 128, (g + 1) * 128)
            ag = h_ref[:, 2 * W + g * 128:2 * W + (g + 1) * 128]
            mix_ref[0, :, sl] = (h_ref[:, sl] * (ms[g] + bsb_ref[g]) * (ag * _sig(ag))).astype(BF)
        kb = jnp.concatenate([_rope_fwd(hp_ref[:, 0:128], rp_ref), _rope_fwd(h_ref[:, 4096:4224], rc_ref)], axis=0)
        vb = jnp.concatenate([hp_ref[:, 128:256], h_ref[:, 4224:4352]], axis=0)
        k2 = _dup_heads(kb)
        v2 = _dup_heads(vb)
        qi = lax.broadcasted_iota(jnp.int32, (128, 256), 0)
        kj = lax.broadcasted_iota(jnp.int32, (128, 256), 1)
        diff = qi + 128 - kj
        valid = (diff >= 0) & (diff < 128) & ((n > 0) | (kj >= 128))
        lacc = jnp.zeros((128, 128), F32)
        for j0 in range(0, 8, HEAD_COLS):
            heads = [(j, half) for j in range(j0, j0 + HEAD_COLS) for half in range(2)]
            sc, pr, oh = {}, {}, {}
            for j in range(j0, j0 + HEAD_COLS):
                qc = _rope_fwd(h_ref[:, 3072 + j * 128:3072 + (j + 1) * 128], rc_ref)
                sc[j, 0] = _dot_nt(jnp.where(lane_lo, qc, 0.0).astype(BF), k2[j // 4])
                sc[j, 1] = _dot_nt(jnp.where(lane_lo, 0.0, qc).astype(BF), k2[j // 4])
            for j, half in heads:
                hq = 2 * j + half
                s = jnp.where(valid, sc[j, half] * 0.125, NEG)
                sk = sink_ref[hq]
                mx = jnp.maximum(jnp.max(s, axis=1, keepdims=True), sk)
                p = jnp.exp(s - mx)
                den = jnp.sum(p, axis=1, keepdims=True) + jnp.exp(sk - mx)
                pr[j, half] = (p / den).astype(BF)
                lacc = jnp.where(lane == hq, mx + jnp.log(den), lacc)
            for j, half in heads:
                oh[j, half] = _dot(pr[j, half], v2[j // 4])
            for j in range(j0, j0 + HEAD_COLS):
                cs = slice(j * 128, (j + 1) * 128)
                ocol = jnp.where(lane_lo, oh[j, 0], oh[j, 1])
                bg = h_ref[:, 4352 + j * 128:4352 + (j + 1) * 128]
                o_ref[:, cs] = ocol
                mix_ref[1, :, cs] = (ocol * (bg * _sig(bg))).astype(BF)
        l_ref[...] = lacc

    prev = lambda n: jnp.maximum(n - 1, 0)
    full = lambda shape: pl.BlockSpec(shape, lambda n: (0,) * len(shape))
    return _pcall(
        body, grid=(nb,),
        in_specs=[pl.BlockSpec((CHUNK, EVEN_IN), lambda n: (n, 0)),
                  pl.BlockSpec((CHUNK, 256), lambda n: (prev(n), 16)),
                  pl.BlockSpec((CHUNK, 384), lambda n: (n, 0)),
                  pl.BlockSpec((CHUNK, 384), lambda n: (prev(n), 0)),
                  full((1, W)), full((1, W)), full((8, 128, 128)), full((8, 128, 128)),
                  pl.BlockSpec(memory_space=pltpu.SMEM)],
        out_specs=[pl.BlockSpec((2, CHUNK, W), lambda n: (0, n, 0)),
                   pl.BlockSpec((CHUNK, W), lambda n: (n, 0)),
                   pl.BlockSpec((CHUNK, 128), lambda n: (n, 0))],
        out_shape=[jax.ShapeDtypeStruct((2, S, W), BF), jax.ShapeDtypeStruct((S, W), F32),
                   jax.ShapeDtypeStruct((S, 128), F32)],
        name=name, comm=comm)(h, h, rope, rope, lng.reshape(1, W), lnb.reshape(1, W), ws, bsb, sinks)


def _even_bwd(h, dmix3, o, l, rope, lng, lnb, ws, wst, bsb, sinks, name, comm=None):
    S = h.shape[0]
    nb = S // CHUNK

    def body(h_ref, hp_ref, hn_ref, dm_ref, dmn_ref, o_ref, on_ref, l_ref, ln_ref, rc_ref, rp_ref, rn_ref,
             lng_ref, lnb_ref, ws_ref, wst_ref, bsb_ref, sink_ref,
             dh_ref, dws_ref, dbs_ref, dlng_ref, dlnb_ref, dsink_ref, dvn_ref):
        n = pl.program_id(0)

        @pl.when(n == 0)
        def _():
            dws_ref[...] = jnp.zeros_like(dws_ref)
            dbs_ref[...] = jnp.zeros_like(dbs_ref)
            dlng_ref[...] = jnp.zeros_like(dlng_ref)
            dlnb_ref[...] = jnp.zeros_like(dlnb_ref)
            dsink_ref[...] = jnp.zeros_like(dsink_ref)

        lane = lax.broadcasted_iota(jnp.int32, (128, 128), 1)
        rowi = lax.broadcasted_iota(jnp.int32, (128, 128), 0)
        lane1 = lax.broadcasted_iota(jnp.int32, (1, 128), 1)
        tri = rowi >= lane
        tri_t = lane >= rowi
        lane_lo = lane < 64
        v = h_ref[:, W:2 * W]
        mu = jnp.mean(v, axis=1, keepdims=True)
        vc = v - mu
        var = jnp.mean(vc * vc, axis=1, keepdims=True)
        rstd = lax.rsqrt(var + LN_EPS)
        vhat = vc * rstd
        vn = vhat * lng_ref[...] + lnb_ref[...]
        dbs_acc = jnp.zeros((128, 128), F32)
        vng = [vn[:, g * 128:(g + 1) * 128].astype(BF) for g in range(8)]
        ms = [_dot(jnp.where(tri, ws_ref[g], 0.0).astype(BF), vng[g]) for g in range(8)]
        dmb = []
        for g in range(8):
            sl = slice(g * 128, (g + 1) * 128)
            m = ms[g] + bsb_ref[g]
            ag = h_ref[:, 2 * W + g * 128:2 * W + (g + 1) * 128]
            sg, dsg = _silu_grad(ag)
            u = h_ref[:, sl]
            da = dm_ref[0, :, sl]
            dmm = da * u * sg
            dh_ref[:, sl] = (da * m * sg).astype(BF)
            dh_ref[:, 2 * W + g * 128:2 * W + (g + 1) * 128] = (da * u * m * dsg).astype(BF)
            dmb.append(dmm.astype(BF))
            dbs_acc = jnp.where(lane == g, jnp.sum(dmm, axis=1, keepdims=True), dbs_acc)
        dvs = [_dot(jnp.where(tri_t, wst_ref[g], 0.0).astype(BF), dmb[g]) for g in range(8)]
        dwss = [_dot_nt(dmb[g], vng[g]) for g in range(8)]
        for g in range(8):
            dvn_ref[:, g * 128:(g + 1) * 128] = dvs[g]
            dws_ref[g] += jnp.where(tri, dwss[g], 0.0)
        dbs_ref[...] += dbs_acc
        dvn = dvn_ref[...]
        dlng_ref[...] += jnp.sum(dvn * vhat, axis=0, keepdims=True)
        dlnb_ref[...] += jnp.sum(dvn, axis=0, keepdims=True)
        dyg = dvn * lng_ref[...]
        m1 = jnp.mean(dyg, axis=1, keepdims=True)
        m2 = jnp.mean(dyg * vhat, axis=1, keepdims=True)
        dh_ref[:, W:2 * W] = (rstd * (dyg - m1 - vhat * m2)).astype(BF)
        kcur = _rope_fwd(h_ref[:, 4096:4224], rc_ref)
        kb = jnp.concatenate([_rope_fwd(hp_ref[:, 0:128], rp_ref), kcur], axis=0)
        vb = jnp.concatenate([hp_ref[:, 128:256], h_ref[:, 4224:4352]], axis=0)
        k2 = _dup_heads(kb)
        v2 = _dup_heads(vb)
        kc2 = _dup_heads(kcur)
        vc2 = _dup_heads(h_ref[:, 4224:4352])
        qi = lax.broadcasted_iota(jnp.int32, (128, 256), 0)
        kj = lax.broadcasted_iota(jnp.int32, (128, 256), 1)
        diff = qi + 128 - kj
        valid = (diff >= 0) & (diff < 128) & ((n > 0) | (kj >= 128))
        validn = (lane > rowi) & (n < nb - 1)
        lc = l_ref[...]
        lnx = ln_ref[...]
        dk = [jnp.zeros((128, 128), F32), jnp.zeros((128, 128), F32)]
        dv = [jnp.zeros((128, 128), F32), jnp.zeros((128, 128), F32)]
        dsk_acc = jnp.zeros((1, 128), F32)
        for j0 in range(0, 8, HEAD_COLS):
            heads = [(j, half) for j in range(j0, j0 + HEAD_COLS) for half in range(2)]
            t = {}
            for j in range(j0, j0 + HEAD_COLS):
                cs = slice(j * 128, (j + 1) * 128)
                qc = _rope_fwd(h_ref[:, 3072 + j * 128:3072 + (j + 1) * 128], rc_ref)
                qn = _rope_fwd(hn_ref[:, 3072 + j * 128:3072 + (j + 1) * 128], rn_ref)
                bg = h_ref[:, 4352 + j * 128:4352 + (j + 1) * 128]
                sgb, dsgb = _silu_grad(bg)
                db = dm_ref[1, :, cs]
                oc = o_ref[:, cs]
                do = db * sgb
                dh_ref[:, 4352 + j * 128:4352 + (j + 1) * 128] = (db * oc * dsgb).astype(BF)
                bgn = hn_ref[:, 4352 + j * 128:4352 + (j + 1) * 128]
                don = dmn_ref[1, :, cs] * (bgn * _sig(bgn))
                prod = do * oc
                prodn = don * on_ref[:, cs]
                for half in range(2):
                    hq = 2 * j + half
                    hm = lane_lo if half == 0 else jnp.logical_not(lane_lo)
                    t[j, half] = dict(
                        dsum=jnp.sum(jnp.where(hm, prod, 0.0), axis=1, keepdims=True),
                        dsumn=jnp.sum(jnp.where(hm, prodn, 0.0), axis=1, keepdims=True),
                        lh=jnp.sum(jnp.where(lane == hq, lc, 0.0), axis=1, keepdims=True),
                        lhn=jnp.sum(jnp.where(lane == hq, lnx, 0.0), axis=1, keepdims=True),
                        qm=jnp.where(hm, qc, 0.0).astype(BF), dom=jnp.where(hm, do, 0.0).astype(BF),
                        qnm=jnp.where(hm, qn, 0.0).astype(BF), donm=jnp.where(hm, don, 0.0).astype(BF))
            for j, half in heads:
                e, hk = t[j, half], j // 4
                e["s"], e["dp"] = _dot_nt(e["qm"], k2[hk]), _dot_nt(e["dom"], v2[hk])
                e["sn"], e["dpn"] = _dot_nt(e["qnm"], kc2[hk]), _dot_nt(e["donm"], vc2[hk])
            for j, half in heads:
                e, hq = t[j, half], 2 * j + half
                p = jnp.exp(jnp.where(valid, e["s"] * 0.125 - e["lh"], NEG))
                ds = p * (e["dp"] - e["dsum"])
                pn = jnp.exp(jnp.where(validn, e["sn"] * 0.125 - e["lhn"], NEG))
                dsn = pn * (e["dpn"] - e["dsumn"])
                psink = jnp.exp(sink_ref[hq] - e["lh"])
                dsk_acc = jnp.where(lane1 == hq, -jnp.sum(psink * e["dsum"], axis=0, keepdims=True), dsk_acc)
                e["ds"] = ds.astype(BF)
                e["pt"], e["dst"] = jnp.transpose(p[:, 128:256]).astype(BF), jnp.transpose(ds[:, 128:256]).astype(BF)
                e["pnt"], e["dsnt"] = jnp.transpose(pn).astype(BF), jnp.transpose(dsn).astype(BF)
            for j, half in heads:
                e, hk = t[j, half], j // 4
                e["dq"] = _dot(e["ds"], k2[hk])
                e["dv"] = _dot(e["pt"], e["dom"]) + _dot(e["pnt"], e["donm"])
                e["dk"] = _dot(e["dst"], e["qm"]) + _dot(e["dsnt"], e["qnm"])
            for j in range(j0, j0 + HEAD_COLS):
                hk = j // 4
                dqcol = jnp.where(lane_lo, t[j, 0]["dq"], t[j, 1]["dq"]) * 0.125
                dh_ref[:, 3072 + j * 128:3072 + (j + 1) * 128] = _rope_bwd(dqcol, rc_ref).astype(BF)
                dv[hk] = dv[hk] + t[j, 0]["dv"] + t[j, 1]["dv"]
                dk[hk] = dk[hk] + (t[j, 0]["dk"] + t[j, 1]["dk"]) * 0.125
        fold = lambda a: a + pltpu.roll(a, 64, 1)
        dh_ref[:, 4096:4224] = _rope_bwd(jnp.where(lane_lo, fold(dk[0]), fold(dk[1])), rc_ref).astype(BF)
        dh_ref[:, 4224:4352] = jnp.where(lane_lo, fold(dv[0]), fold(dv[1])).astype(BF)
        dsink_ref[...] += dsk_acc

    prev = lambda n: jnp.maximum(n - 1, 0)
    nxt = lambda n: jnp.minimum(n + 1, nb - 1)
    full = lambda shape: pl.BlockSpec(shape, lambda n: (0,) * len(shape))
    return _pcall(
        body, grid=(nb,),
        in_specs=[pl.BlockSpec((CHUNK, EVEN_IN), lambda n: (n, 0)),
                  pl.BlockSpec((CHUNK, 256), lambda n: (prev(n), 16)),
                  pl.BlockSpec((CHUNK, EVEN_IN), lambda n: (nxt(n), 0)),
                  pl.BlockSpec((2, CHUNK, W), lambda n: (0, n, 0)),
                  pl.BlockSpec((2, CHUNK, W), lambda n: (0, nxt(n), 0)),
                  pl.BlockSpec((CHUNK, W), lambda n: (n, 0)),
                  pl.BlockSpec((CHUNK, W), lambda n: (nxt(n), 0)),
                  pl.BlockSpec((CHUNK, 128), lambda n: (n, 0)),
                  pl.BlockSpec((CHUNK, 128), lambda n: (nxt(n), 0)),
                  pl.BlockSpec((CHUNK, 384), lambda n: (n, 0)),
                  pl.BlockSpec((CHUNK, 384), lambda n: (prev(n), 0)),
                  pl.BlockSpec((CHUNK, 384), lambda n: (nxt(n), 0)),
                  full((1, W)), full((1, W)), full((8, 128, 128)), full((8, 128, 128)), full((8, 128, 128)),
                  pl.BlockSpec(memory_space=pltpu.SMEM)],
        out_specs=[pl.BlockSpec((CHUNK, EVEN_IN), lambda n: (n, 0)),
                   full((8, 128, 128)), full((128, 128)), full((1, W)), full((1, W)), full((1, 128))],
        out_shape=[jax.ShapeDtypeStruct((S, EVEN_IN), BF), jax.ShapeDtypeStruct((8, 128, 128), F32),
                   jax.ShapeDtypeStruct((128, 128), F32), jax.ShapeDtypeStruct((1, W), F32),
                   jax.ShapeDtypeStruct((1, W), F32), jax.ShapeDtypeStruct((1, 128), F32)],
        scratch=[pltpu.VMEM((CHUNK, W), F32)], name=name, comm=comm,
    )(h, h, h, dmix3, dmix3, o, o, l, l, rope, rope, rope, lng.reshape(1, W), lnb.reshape(1, W), ws, wst, bsb, sinks)


def _expm1(x):
    ser = x * (1.0 + x * (0.5 + x * (1.0 / 6.0 + x * (1.0 / 24.0))))
    return jnp.where(jnp.abs(x) < 1e-2, ser, jnp.exp(x) - 1.0)


def _softplus_neg(lam):
    z = -lam
    e = jnp.exp(-jnp.abs(z))
    l1p = jnp.where(e < 1e-3, e * (1.0 - e * (0.5 - e * (1.0 / 3.0))), jnp.log(1.0 + e))
    return jnp.maximum(z, 0.0) + l1p


def _shift_down(x, k, row, fill=0.0):
    return jnp.where(row >= k, pltpu.roll(x, k, 0), fill)


def _shift_up(x, k, row, fill=0.0):
    S = x.shape[0]
    return jnp.where(row < S - k, pltpu.roll(x, S - k, 0), fill)


def _lru_gates(xc, row, cw_ref, cb_ref, wa_ref, wx_ref, ba_ref, bx_ref, lam_ref):
    xconv = (cw_ref[3:4, :] * xc + cw_ref[2:3, :] * _shift_down(xc, 1, row) + cw_ref[1:2, :] * _shift_down(xc, 2, row)
             + cw_ref[0:1, :] * _shift_down(xc, 3, row) + cb_ref[...])
    xb = xconv.astype(BF)
    r = _sig(_dot(xb, wa_ref[...]) + ba_ref[...])
    i = _sig(_dot(xb, wx_ref[...]) + bx_ref[...])
    sp = _softplus_neg(lam_ref[...])
    log_a = -LRU_C * r * sp
    a = jnp.exp(log_a)
    mult = jnp.sqrt(-_expm1(2.0 * log_a))
    return xconv, r, i, sp, a, mult


def _odd_c_fwd(h, cw, cb, wa, wx, ba, bx, lam, name, comm=None):
    S = h.shape[0]

    def body(xc_ref, cg_ref, cw_ref, cb_ref, wa_ref, wx_ref, ba_ref, bx_ref, lam_ref, mix_ref, hst_ref):
        row = lax.broadcasted_iota(jnp.int32, (S, 128), 0)
        xconv, r, i, sp, a, mult = _lru_gates(xc_ref[...], row, cw_ref, cb_ref, wa_ref, wx_ref, ba_ref, bx_ref, lam_ref)
        aa = a
        bb = mult * (i * xconv)
        k = 1
        while k < S:
            bb = aa * _shift_down(bb, k, row) + bb
            if 2 * k < S:
                aa = aa * _shift_down(aa, k, row, 1.0)
            k *= 2
        hst_ref[...] = bb
        cg = cg_ref[...]
        mix_ref[...] = (bb * (cg * _sig(cg))).astype(BF)

    col = lambda off: pl.BlockSpec((S, 128), lambda j: (0, off + j))
    vec = pl.BlockSpec((1, 128), lambda j: (0, j))
    mat = pl.BlockSpec((None, 128, 128), lambda j: (j, 0, 0))
    return _pcall(
        body, grid=(8,),
        in_specs=[col(0), col(8), pl.BlockSpec((4, 128), lambda j: (0, j)), vec, mat, mat, vec, vec, vec],
        out_specs=[pl.BlockSpec((None, S, 128), lambda j: (0, 0, j)), pl.BlockSpec((S, 128), lambda j: (0, j))],
        out_shape=[jax.ShapeDtypeStruct((2, S, W), BF), jax.ShapeDtypeStruct((S, W), F32)],
        name=name, comm=comm,
    )(h, h, cw, cb.reshape(1, W), wa, wx, ba.reshape(1, W), bx.reshape(1, W), lam.reshape(1, W))


def _pool_sums(x, g, row, shift):
    s2 = x + shift(x, 1, row)
    s4 = s2 + shift(s2, 2, row)
    s8 = s4 + shift(s4, 4, row)
    s16 = s8 + shift(s8, 8, row)
    return jnp.where(g == 0, s2, jnp.where(g == 1, s4, jnp.where(g == 2, s8, s16)))


def _odd_d_fwd(h, mix3, wp, dscale, name):
    S = h.shape[0]

    def body(xd_ref, dg_ref, wp_ref, ds_ref, mix_in, mix_ref):
        g = pl.program_id(0)
        row = lax.broadcasted_iota(jnp.int32, (S, 256), 0)
        xd = xd_ref[...]
        cnt = jnp.minimum(row + 1, jnp.left_shift(2, g)).astype(F32)
        pooled = _pool_sums(xd, g, row, _shift_down) / cnt - xd
        mixed = _dot(pooled.astype(BF), wp_ref[...])
        dg = dg_ref[...]
        mix_ref[...] = (mixed * ds_ref[...] * (dg * _sig(dg))).astype(BF)

    col = lambda off: pl.BlockSpec((S, 256), lambda g: (0, off + g))
    return pl.pallas_call(
        body, grid=(4,),
        in_specs=[col(8), col(12), pl.BlockSpec((None, 256, 256), lambda g: (g, 0, 0)),
                  pl.BlockSpec((1, 256), lambda g: (0, g)), ANY],
        out_specs=pl.BlockSpec((None, S, 256), lambda g: (1, 0, g)),
        out_shape=jax.ShapeDtypeStruct((2, S, W), BF), input_output_aliases={4: 0},
        name=name, compiler_params=_cp(),
    )(h, h, wp, dscale.reshape(1, W), mix3)


def _odd_c_bwd(h, hst, dmix3, cw, cb, wa, wx, wat, wxt, ba, bx, lam, name, comm=None):
    S = h.shape[0]

    def body(xc_ref, cg_ref, hst_ref, dc_ref, cw_ref, cb_ref, wa_ref, wx_ref, wat_ref, wxt_ref, ba_ref, bx_ref, lam_ref,
             dh_ref, dcw_ref, dcb_ref, dwa_ref, dwx_ref, dba_ref, dbx_ref, dlam_ref):
        row = lax.broadcasted_iota(jnp.int32, (S, 128), 0)
        xc = xc_ref[...]
        xconv, r, i, sp, a, mult = _lru_gates(xc, row, cw_ref, cb_ref, wa_ref, wx_ref, ba_ref, bx_ref, lam_ref)
        hst = hst_ref[...]
        cg = cg_ref[...]
        sg, dsg = _silu_grad(cg)
        dc = dc_ref[...]
        dh_ref[1] = (dc * hst * dsg).astype(BF)
        aa = _shift_up(a, 1, row)
        bb = dc * sg
        k = 1
        while k < S:
            bb = aa * _shift_up(bb, k, row) + bb
            if 2 * k < S:
                aa = aa * _shift_up(aa, k, row, 1.0)
            k *= 2
        lam_t = bb
        da = lam_t * _shift_down(hst, 1, row)
        ix = i * xconv
        dmult = lam_t * ix
        di = lam_t * mult * xconv
        dxconv = lam_t * mult * i
        dlog_a = da * a - dmult * (a * a / mult)
        dr = dlog_a * (-LRU_C * sp)
        dsp = jnp.sum(dlog_a * (-LRU_C * r), axis=0, keepdims=True)
        dlam_ref[...] = dsp * (-_sig(-lam_ref[...]))
        dpa = dr * r * (1.0 - r)
        dpx = di * i * (1.0 - i)
        dpab = dpa.astype(BF)
        dpxb = dpx.astype(BF)
        xb = xconv.astype(BF)
        dxconv = dxconv + _dot(dpab, wat_ref[...]) + _dot(dpxb, wxt_ref[...])
        dwa_ref[...] = _dot_tn(xb, dpab)
        dwx_ref[...] = _dot_tn(xb, dpxb)
        dba_ref[...] = jnp.sum(dpa, axis=0, keepdims=True)
        dbx_ref[...] = jnp.sum(dpx, axis=0, keepdims=True)
        dh_ref[0] = (cw_ref[3:4, :] * dxconv + cw_ref[2:3, :] * _shift_up(dxconv, 1, row)
                     + cw_ref[1:2, :] * _shift_up(dxconv, 2, row) + cw_ref[0:1, :] * _shift_up(dxconv, 3, row)).astype(BF)
        for j in range(4):
            src = xc if j == 3 else _shift_down(xc, 3 - j, row)
            dcw_ref[j:j + 1, :] = jnp.sum(dxconv * src, axis=0, keepdims=True)
        dcb_ref[...] = jnp.sum(dxconv, axis=0, keepdims=True)

    col = lambda off: pl.BlockSpec((S, 128), lambda j: (0, off + j))
    vec = pl.BlockSpec((1, 128), lambda j: (0, j))
    mat = pl.BlockSpec((None, 128, 128), lambda j: (j, 0, 0))
    vshape = jax.ShapeDtypeStruct((1, W), F32)
    mshape = jax.ShapeDtypeStruct((8, 128, 128), F32)
    return _pcall(
        body, grid=(8,),
        in_specs=[col(0), col(8), col(0), pl.BlockSpec((None, S, 128), lambda j: (0, 0, j)),
                  pl.BlockSpec((4, 128), lambda j: (0, j)), vec, mat, mat, mat, mat, vec, vec, vec],
        out_specs=[pl.BlockSpec((2, S, 128), lambda j: (0, 0, j)), pl.BlockSpec((4, 128), lambda j: (0, j)), vec,
                   mat, mat, vec, vec, vec],
        out_shape=[jax.ShapeDtypeStruct((4, S, W), BF), jax.ShapeDtypeStruct((4, W), F32), vshape, mshape, mshape,
                   vshape, vshape, vshape],
        name=name, vmem=56, comm=comm,
    )(h, h, hst, dmix3, cw, cb.reshape(1, W), wa, wx, wat, wxt, ba.reshape(1, W), bx.reshape(1, W), lam.reshape(1, W))


def _odd_d_bwd(h, dmix3, dh4, wp, wpt, dscale, name):
    S = h.shape[0]

    def body(xd_ref, dg_ref, dd_ref, wp_ref, wpt_ref, ds_ref, dh_in, dh_ref, dwp_ref, dds_ref):
        g = pl.program_id(0)
        row = lax.broadcasted_iota(jnp.int32, (S, 256), 0)
        xd = xd_ref[...]
        cnt = jnp.minimum(row + 1, jnp.left_shift(2, g)).astype(F32)
        pooled = _pool_sums(xd, g, row, _shift_down) / cnt - xd
        pb = pooled.astype(BF)
        mixed = _dot(pb, wp_ref[...])
        dg = dg_ref[...]
        sg, dsg = _silu_grad(dg)
        dd = dd_ref[...]
        dmixed = dd * ds_ref[...] * sg
        dds_ref[...] = jnp.sum(dd * mixed * sg, axis=0, keepdims=True)
        dh_ref[1] = (dd * mixed * ds_ref[...] * dsg).astype(BF)
        dmb = dmixed.astype(BF)
        dpooled = _dot(dmb, wpt_ref[...])
        dwp_ref[...] = _dot_tn(pb, dmb)
        dh_ref[0] = (_pool_sums(dpooled / cnt, g, row, _shift_up) - dpooled).astype(BF)

    col = lambda off: pl.BlockSpec((S, 256), lambda g: (0, off + g))
    mat = pl.BlockSpec((None, 256, 256), lambda g: (g, 0, 0))
    vec = pl.BlockSpec((1, 256), lambda g: (0, g))
    return pl.pallas_call(
        body, grid=(4,),
        in_specs=[col(8), col(12), pl.BlockSpec((None, S, 256), lambda g: (1, 0, g)), mat, mat, vec, ANY],
        out_specs=[pl.BlockSpec((2, S, 256), lambda g: (1, 0, g)), mat, vec],
        out_shape=[jax.ShapeDtypeStruct((4, S, W), BF), jax.ShapeDtypeStruct((4, 256, 256), F32),
                   jax.ShapeDtypeStruct((1, W), F32)],
        input_output_aliases={6: 0}, name=name, compiler_params=_cp(56),
    )(h, h, dmix3, wp, wpt, dscale.reshape(1, W), dh4)


def _peer(d):
    x, y, c = lax.axis_index("x"), lax.axis_index("y"), lax.axis_index("c")
    px = 1 - x if d & 4 else x
    py = 1 - y if d & 2 else y
    pc = 1 - c if d & 1 else c
    return (px, py, pc), 4 * px + 2 * py + pc


class _GatherAll(_Comm):
    def __init__(self, xs):
        self.inputs = [xs]
        self.out_shapes = [jax.ShapeDtypeStruct((N_DEV,) + xs.shape, xs.dtype)]
        self.sem_shapes = [pltpu.SemaphoreType.DMA((N_DEV - 1,)), pltpu.SemaphoreType.DMA((N_DEV - 1,)),
                           pltpu.SemaphoreType.DMA]

    def copies(self, ins, outs, sems):
        (x_ref,), (out_ref,), (send, recv, loc) = ins, outs, sems
        _, me = _peer(0)
        res = [pltpu.make_async_copy(x_ref, out_ref.at[me], loc)]
        for d in range(1, N_DEV):
            peer, _ = _peer(d)
            res.append(pltpu.make_async_remote_copy(src_ref=x_ref, dst_ref=out_ref.at[me], send_sem=send.at[d - 1],
                                                    recv_sem=recv.at[d - 1], device_id=peer, device_id_type=MESH))
        return res


class _ExchangeAll(_Comm):
    def __init__(self, g8):
        self.inputs = [g8]
        self.out_shapes = [jax.ShapeDtypeStruct(g8.shape, g8.dtype)]
        self.sem_shapes = [pltpu.SemaphoreType.DMA((N_DEV - 1,)), pltpu.SemaphoreType.DMA((N_DEV - 1,)),
                           pltpu.SemaphoreType.DMA]

    def copies(self, ins, outs, sems):
        (g_ref,), (out_ref,), (send, recv, loc) = ins, outs, sems
        _, me = _peer(0)
        res = [pltpu.make_async_copy(g_ref.at[me], out_ref.at[0], loc)]
        for d in range(1, N_DEV):
            peer, pidx = _peer(d)
            res.append(pltpu.make_async_remote_copy(src_ref=g_ref.at[pidx], dst_ref=out_ref.at[d], send_sem=send.at[d - 1],
                                                    recv_sem=recv.at[d - 1], device_id=peer, device_id_type=MESH))
        return res


def _sum8(r8, tr, name):
    _, R, C = r8.shape
    tr = min(tr, R)
    assert R % tr == 0

    def body(r_ref, o_ref):
        acc = r_ref[0]
        for d in range(1, N_DEV):
            acc = acc + r_ref[d]
        o_ref[...] = acc

    return pl.pallas_call(
        body, grid=(R // tr,), in_specs=[pl.BlockSpec((N_DEV, tr, C), lambda i: (0, i, 0))],
        out_specs=pl.BlockSpec((tr, C), lambda i: (i, 0)), out_shape=jax.ShapeDtypeStruct((R, C), F32),
        name=name, compiler_params=_cp(),
    )(r8)


def _adamw_math(w, g, m, v):
    m2 = B1 * m + (1.0 - B1) * g
    v2 = B2 * v + (1.0 - B2) * (g * g)
    m_hat = m2 / (1.0 - B1 ** STEP)
    v_hat = v2 / (1.0 - B2 ** STEP)
    return -LR * (m_hat / (jnp.sqrt(v_hat) + ADAM_EPS) + WD * w), m2, v2


def _adamw_many(ws, gs, ms, vs, name):
    n = len(ws)

    def body(*refs):
        for i in range(n):
            d, m2, v2 = _adamw_math(refs[i][...], refs[n + i][...], refs[2 * n + i][...], refs[3 * n + i][...])
            refs[4 * n + i][...] = d
            refs[5 * n + i][...] = m2
            refs[6 * n + i][...] = v2

    vmem = pl.BlockSpec(memory_space=pltpu.VMEM)
    shapes = [jax.ShapeDtypeStruct(w.shape, F32) for w in ws]
    res = pl.pallas_call(body, in_specs=[vmem] * (4 * n), out_specs=[vmem] * (3 * n), out_shape=shapes * 3, name=name,
                         compiler_params=_cp())(*ws, *gs, *ms, *vs)
    return res[:n], res[n:2 * n], res[2 * n:]


def _adamw(w3, gs, m3, v3, tr, name, comm=None):
    _, R, C = w3.shape

    def body(w_ref, g0_ref, g1_ref, m_ref, v_ref, d_ref, m2_ref, v2_ref, g_ref):
        g = jnp.where(pl.program_id(0) == 0, g0_ref[...], g1_ref[...])
        d_ref[...], m2_ref[...], v2_ref[...] = _adamw_math(w_ref[...], g, m_ref[...], v_ref[...])
        g_ref[...] = g

    blk = pl.BlockSpec((None, tr, C), lambda j, i: (j, i, 0))
    grad = lambda layer: pl.BlockSpec((tr, C), lambda j, i: (jnp.where(j == layer, i, 0), 0))
    shp = jax.ShapeDtypeStruct((2, R, C), F32)
    return _pcall(body, grid=(2, R // tr), in_specs=[blk, grad(0), grad(1), blk, blk], out_specs=[blk] * 4,
                  out_shape=[shp] * 4, name=name, comm=comm)(w3, gs[0], gs[1], m3, v3)


def _rep_pack(a):
    n = a.size
    pad = (-n) % 1024
    f = a.reshape(-1)
    if pad:
        f = jnp.concatenate([f, jnp.zeros((pad,), a.dtype)])
    return f.reshape(N_DEV, -1, 128)


def _rep_unpack(p, shape):
    n = 1
    for s in shape:
        n *= s
    return p.reshape(-1)[:n].reshape(shape)


def _sh_pack(a, axis):
    shp = a.shape
    a = a.reshape(shp[:axis] + (N_DEV, shp[axis] // N_DEV) + shp[axis + 1:])
    return jnp.moveaxis(a, axis, 0).reshape(N_DEV, -1, 128)


def _sh_unpack(p, shape, axis):
    a = p.reshape((N_DEV,) + shape[:axis] + (shape[axis] // N_DEV,) + shape[axis + 1:])
    return jnp.moveaxis(a, 0, axis).reshape(shape)


def _pad_rows(a, mult=8):
    pad = (-a.shape[-2]) % mult
    if pad:
        a = jnp.concatenate([a, jnp.zeros(a.shape[:-2] + (pad, a.shape[-1]), a.dtype)], axis=-2)
    return a


REP = ["even_a_ln_g", "even_a_ln_b", "even_a_ws", "even_a_bs", "even_b_sinks", "even_ln_g", "even_ln_b",
       "odd_w_a", "odd_w_x"]
SH = [("odd_conv_w", (2, 4, W), 2), ("odd_conv_b", (2, W), 1), ("odd_b_a", (2, W), 1), ("odd_b_x", (2, W), 1),
      ("odd_lam", (2, W), 1), ("odd_w_pool", (2, 4, 256, 256), 2), ("odd_d_scale", (2, W), 1),
      ("odd_ln_g", (2, D), 1), ("odd_ln_b", (2, D), 1)]
BIG = ["even_w_in", "even_w_out", "odd_w_in", "odd_w_out"]
NAMES = ["even_w_in", "even_a_ln_g", "even_a_ln_b", "even_a_ws", "even_a_bs", "even_b_sinks", "even_w_out",
         "even_ln_g", "even_ln_b", "odd_w_in", "odd_conv_w", "odd_conv_b", "odd_w_a", "odd_b_a", "odd_w_x", "odd_b_x",
         "odd_lam", "odd_w_pool", "odd_d_scale", "odd_w_out", "odd_ln_g", "odd_ln_b"]


def _rope_table(positions):
    S = positions.shape[0]
    inv = ROPE_THETA ** (-jnp.arange(0, 16, 2, dtype=F32) / 16)
    ang = positions.astype(F32)[:, None] * inv
    cos, sin = jnp.cos(ang), jnp.sin(ang)
    one, zero = jnp.ones((S, 48), F32), jnp.zeros((S, 48), F32)
    z8 = jnp.zeros((S, 8), F32)
    c64 = jnp.concatenate([cos, cos, one], axis=1)
    s1 = jnp.concatenate([-sin, z8, zero], axis=1)
    s2 = jnp.concatenate([z8, sin, zero], axis=1)
    return jnp.concatenate([c64, c64, s1, s1, s2, s2], axis=1)


def kernel(x, positions, even_w_in, even_a_ln_g, even_a_ln_b, even_a_ws, even_a_bs, even_b_sinks, even_w_out, even_ln_g, even_ln_b, odd_w_in, odd_conv_w, odd_conv_b, odd_w_a, odd_b_a, odd_w_x, odd_b_x, odd_lam, odd_w_pool, odd_d_scale, odd_w_out, odd_ln_g, odd_ln_b, loss_target, m_even_w_in, m_even_a_ln_g, m_even_a_ln_b, m_even_a_ws, m_even_a_bs, m_even_b_sinks, m_even_w_out, m_even_ln_g, m_even_ln_b, m_odd_w_in, m_odd_conv_w, m_odd_conv_b, m_odd_w_a, m_odd_b_a, m_odd_w_x, m_odd_b_x, m_odd_lam, m_odd_w_pool, m_odd_d_scale, m_odd_w_out, m_odd_ln_g, m_odd_ln_b, v_even_w_in, v_even_a_ln_g, v_even_a_ln_b, v_even_a_ws, v_even_a_bs, v_even_b_sinks, v_even_w_out, v_even_ln_g, v_even_ln_b, v_odd_w_in, v_odd_conv_w, v_odd_conv_b, v_odd_w_a, v_odd_b_a, v_odd_w_x, v_odd_b_x, v_odd_lam, v_odd_w_pool, v_odd_d_scale, v_odd_w_out, v_odd_ln_g, v_odd_ln_b):
    args = (even_w_in, even_a_ln_g, even_a_ln_b, even_a_ws, even_a_bs, even_b_sinks, even_w_out, even_ln_g, even_ln_b,
            odd_w_in, odd_conv_w, odd_conv_b, odd_w_a, odd_b_a, odd_w_x, odd_b_x, odd_lam, odd_w_pool, odd_d_scale,
            odd_w_out, odd_ln_g, odd_ln_b)
    margs = (m_even_w_in, m_even_a_ln_g, m_even_a_ln_b, m_even_a_ws, m_even_a_bs, m_even_b_sinks, m_even_w_out,
             m_even_ln_g, m_even_ln_b, m_odd_w_in, m_odd_conv_w, m_odd_conv_b, m_odd_w_a, m_odd_b_a, m_odd_w_x,
             m_odd_b_x, m_odd_lam, m_odd_w_pool, m_odd_d_scale, m_odd_w_out, m_odd_ln_g, m_odd_ln_b)
    vargs = (v_even_w_in, v_even_a_ln_g, v_even_a_ln_b, v_even_a_ws, v_even_a_bs, v_even_b_sinks, v_even_w_out,
             v_even_ln_g, v_even_ln_b, v_odd_w_in, v_odd_conv_w, v_odd_conv_b, v_odd_w_a, v_odd_b_a, v_odd_w_x,
             v_odd_b_x, v_odd_lam, v_odd_w_pool, v_odd_d_scale, v_odd_w_out, v_odd_ln_g, v_odd_ln_b)
    wts = dict(zip(NAMES, args))
    mom = dict(zip(NAMES, margs))
    var = dict(zip(NAMES, vargs))
    S = x.shape[1]
    x0 = x[0]
    rope = _rope_table(positions[0])

    kinds = ("even", "odd", "even", "odd")
    blk_in = [jnp.transpose(wts[kinds[l] + "_w_in"][l // 2]).astype(BF) for l in range(4)]
    blk_out = [wts[kinds[l] + "_w_out"][l // 2].astype(BF) for l in range(4)]
    sh_local = _pad_rows(jnp.concatenate([wts[nm].reshape(-1, 128) for nm, _, _ in SH], axis=0), 16)
    me = 4 * lax.axis_index("x") + 2 * lax.axis_index("y") + lax.axis_index("c")
    own_slot = lambda blk: lax.dynamic_update_slice(lax.empty((N_DEV,) + blk.shape, blk.dtype), blk[None], (me, 0, 0))
    reg = {"blk_small": sh_local, "w_small": own_slot(sh_local)}
    sched = _Sched(reg)
    for l in range(4):
        reg[f"blk_in{l}"], reg[f"blk_out{l}"] = blk_in[l], blk_out[l]
        reg[f"w_in{l}"], reg[f"w_out{l}"] = own_slot(blk_in[l]), own_slot(blk_out[l])
    sched.add(_rows("blk_in0", "w_in0", "ag1", blk_in[0].shape[0], ROW_CHUNK[blk_in[0].shape[0]]))
    sched.add(_rows("blk_small", "w_small", "ag1", sh_local.shape[0], sh_local.shape[0]))
    for l in range(4):
        sched.add(_rows(f"blk_out{l}", f"w_out{l}", "ag1", D // N_DEV, ROW_CHUNK[D // N_DEV]))
        if l < 3:
            r = blk_in[l + 1].shape[0]
            sched.add(_rows(f"blk_in{l + 1}", f"w_in{l + 1}", "ag1", r, ROW_CHUNK[r]))

    def gathered(dst, blk):
        sched.flush(dst, FLUSH_EXTRA_US)
        return reg.pop(dst)

    wt_in0 = gathered("w_in0", blk_in[0]).reshape(-1, D)
    full = {nm: wts[nm] for nm in REP}

    def gather_small():
        sh_all = gathered("w_small", sh_local)
        off = 0
        for nm, shape, axis in SH:
            r = wts[nm].size // 128
            full[nm] = _sh_unpack(sh_all[:, off:off + r, :], shape, axis)
            off += r

    saved = []
    wt_in, w_out = [wt_in0, None, None, None], [None] * 4
    xf, xb = x0, x0.astype(BF)
    fwd = lambda name: FWD_OVERBOOK * CARRY_US[name]
    for layer in range(4):
        j = layer // 2
        kind = kinds[layer]
        if wt_in[layer] is None:
            wt_in[layer] = gathered(f"w_in{layer}", blk_in[layer]).reshape(-1, D)
        h = sched.run(_mm_nt, fwd("mm_h_" + kind), xb, wt_in[layer], 1024, 768 if kind == "even" else 512, "mm_h_" + kind)
        if kind == "even":
            bsb = jnp.broadcast_to(full["even_a_bs"][j][:, :, None], (8, 128, 128))
            mix3, o, l = sched.run(_even_fwd, fwd("even_fwd"), h, rope, full["even_a_ln_g"][j], full["even_a_ln_b"][j],
                                   full["even_a_ws"][j], bsb, full["even_b_sinks"][j], "even_fwd")
            extra = (o, l, bsb)
        else:
            if "odd_lam" not in full:
                gather_small()
            wa, wx = full["odd_w_a"][j].astype(BF), full["odd_w_x"][j].astype(BF)
            wp = full["odd_w_pool"][j].astype(BF)
            mix3, hst = sched.run(_odd_c_fwd, fwd("odd_c_fwd"), h, full["odd_conv_w"][j], full["odd_conv_b"][j], wa, wx,
                                  full["odd_b_a"][j], full["odd_b_x"][j], full["odd_lam"][j], "odd_c_fwd")
            mix3 = _odd_d_fwd(h, mix3, wp, full["odd_d_scale"][j], "odd_d_fwd")
            extra = (hst, wa, wx, wp)
        w_out[layer] = gathered(f"w_out{layer}", blk_out[layer]).reshape(D, D)
        z, xn, xnb = sched.run(_mm_out_ln, fwd("mm_out_ln"), mix3, w_out[layer], xf, full[kind + "_ln_g"][j],
                               full[kind + "_ln_b"][j], "mm_out_ln")
        saved.append((xb, h, mix3, z, extra))
        xf, xb = xn, xnb

    dxn, part = _loss_grad(xf, loss_target[0])
    loss = lax.psum(part[0, 0] * (0.5 / D), ("x", "y", "c"))

    gsum = {nm: [None, None] for nm in NAMES}

    chip_sums = {}
    sched.overhang = 0.15

    waiting = []

    def chip_sum(g, tag, key):
        r = g.shape[0] // N_DEV
        reg["g_" + key] = g.reshape(N_DEV, r, D)
        sched.add(_rows("g_" + key, "d_" + key, "rsd", r, r), first=True)
        waiting.append((key, tag))

    def add_arrived():
        for key, tag in list(waiting):
            if "d_" + key in reg and not sched.pending("d_" + key):
                waiting.remove((key, tag))
                g8 = reg.pop("g_" + key)
                chip_sums[key] = reg["s_" + key] = _add_pairs(g8, reg.pop("d_" + key), "rs_add_" + tag)
                sched.add(_rows("s_" + key, "r_" + key, "rs", g8.shape[1], ROW_CHUNK[g8.shape[1]] // 2))

    sched.after_landing = add_arrived

    def reduced(key, name):
        sched.flush("d_" + key, FLUSH_EXTRA_US)
        sched.flush("r_" + key, FLUSH_EXTRA_US)
        return _rs_final(chip_sums[key], reg.pop("r_" + key), name)

    for layer in (3, 2, 1, 0):
        j = layer // 2
        xb, h, mix3, z, extra = saved[layer]
        kind = kinds[layer]
        dz, dzb, dg, dbeta = sched.run(_ln_bwd, CARRY_US["ln_bwd"], dxn, z, full[kind + "_ln_g"][j], "ln_bwd")
        gsum[kind + "_ln_g"][j] = dg.reshape(D)
        gsum[kind + "_ln_b"][j] = dbeta.reshape(D)
        chip_sum(sched.run(_mm_tn, CARRY_US["mm_dw_out"], mix3, dzb, 512, "mm_dw_out"), "w_out", f"out{layer}")
        dmix3 = sched.run(_mm_nt, CARRY_US["mm_dmix"], dzb, w_out[layer], 1024, 512, "mm_dmix", out3=True)
        if kind == "even":
            o, l, bsb = extra
            ws = full["even_a_ws"][j]
            dh, dws, dbs, dlng, dlnb, dsink = sched.run(
                _even_bwd, CARRY_US["even_bwd"], h, dmix3, o, l, rope, full["even_a_ln_g"][j], full["even_a_ln_b"][j],
                ws, jnp.swapaxes(ws, 1, 2), bsb, full["even_b_sinks"][j], "even_bwd")
            gsum["even_a_ws"][j] = dws
            gsum["even_a_bs"][j] = jnp.transpose(dbs[:, :8])
            gsum["even_a_ln_g"][j] = dlng.reshape(W)
            gsum["even_a_ln_b"][j] = dlnb.reshape(W)
            gsum["even_b_sinks"][j] = dsink[0, :16]
            if layer == 0:
                rep_rows = [_rep_pack(jnp.stack(gsum[nm]).reshape(wts[nm].shape)) for nm in REP]
                sh_rows = [_sh_pack(jnp.stack(gsum[nm]).reshape(shape), axis) for nm, shape, axis in SH]
                packed = _pad_rows(jnp.concatenate(rep_rows + sh_rows, axis=1))
                gw, (small8,) = _mm_tn(dh, xb, 384, "mm_dw_in_even", comm=_ExchangeAll(packed))
            else:
                gw = sched.run(_mm_tn, CARRY_US["mm_dw_in_even"], dh, xb, 384, "mm_dw_in_even")
            chip_sum(gw, "w_in_even", f"in{layer}")
            if layer == 0:
                n_rep = sum(p.shape[1] for p in rep_rows)
                red = _sum8(small8, 1 << 20, "sum_small")
                (rep_all,) = sched.flush("d_in0", FLUSH_EXTRA_US, beside=_GatherAll(_pad_rows(red[:n_rep])))
                sched.overhang = 0.6
            dxn = sched.run(_mm_nn_res, CARRY_US["mm_dx_even"], dh, wt_in[layer], dz, 512, 512, "mm_dx_even")
        else:
            hst, wa, wx, wp = extra
            dh4, dcw, dcb, dwa, dwx, dba, dbx, dlam = sched.run(
                _odd_c_bwd, CARRY_US["odd_c_bwd"], h, hst, dmix3, full["odd_conv_w"][j], full["odd_conv_b"][j], wa, wx,
                jnp.swapaxes(wa, 1, 2), jnp.swapaxes(wx, 1, 2), full["odd_b_a"][j], full["odd_b_x"][j], full["odd_lam"][j],
                "odd_c_bwd")
            dh4, dwp, dds = _odd_d_bwd(h, dmix3, dh4, wp, jnp.swapaxes(wp, 1, 2), full["odd_d_scale"][j], "odd_d_bwd")
            gsum["odd_conv_w"][j], gsum["odd_conv_b"][j] = dcw, dcb.reshape(W)
            gsum["odd_w_a"][j], gsum["odd_w_x"][j] = dwa, dwx
            gsum["odd_b_a"][j], gsum["odd_b_x"][j], gsum["odd_lam"][j] = dba.reshape(W), dbx.reshape(W), dlam.reshape(W)
            gsum["odd_w_pool"][j], gsum["odd_d_scale"][j] = dwp, dds.reshape(W)
            chip_sum(sched.run(_mm_tn, CARRY_US["mm_dw_in_odd"], dh4, xb, 512, "mm_dw_in_odd"), "w_in_odd", f"in{layer}")
            dxn = sched.run(_mm_nn_res, CARRY_US["mm_dx_odd"], dh4, wt_in[layer], dz, 512, 512, "mm_dx_odd")
    grad_x = dxn[None]

    out_g, out_d, out_m, out_v = {}, {}, {}, {}
    for nm, kind, what, layers in (("odd_w_out", "odd", "out", (1, 3)), ("even_w_out", "even", "out", (0, 2)),
                                   ("odd_w_in", "odd", "in", (1, 3)), ("even_w_in", "even", "in", (0, 2))):
        gl = [reduced(f"{what}{l}", f"rs_final_w_{what}_{kind}") for l in layers]
        if nm == "even_w_in":
            view = lambda a: jnp.transpose(a, (0, 2, 1))
            res, _ = _adamw(view(wts[nm]), gl, view(mom[nm]), view(var[nm]), 168, f"adamw_{nm}")
            res = [view(a) for a in res]
        elif what == "in":
            res, _ = _adamw(wts[nm], [jnp.transpose(a) for a in gl], mom[nm], var[nm], 512, f"adamw_{nm}")
        else:
            res = sched.run(_adamw, CARRY_US["adamw_" + nm], wts[nm], gl, mom[nm], var[nm], 128, f"adamw_{nm}")
        out_d[nm], out_m[nm], out_v[nm], out_g[nm] = res

    g_small = {}
    off = 0
    for nm, p in zip(REP, rep_rows):
        r = p.shape[1]
        g_small[nm] = _rep_unpack(rep_all[:, off:off + r, :], wts[nm].shape)
        off += r
    off = n_rep
    for (nm, shape, axis), p in zip(SH, sh_rows):
        r = p.shape[1]
        g_small[nm] = red[off:off + r].reshape(wts[nm].shape)
        off += r

    def rows(a):
        f = a.reshape(-1)
        pad = (-f.shape[0]) % 128
        if pad:
            f = jnp.concatenate([f, jnp.zeros((pad,), a.dtype)])
        return f.reshape(-1, 128)

    small = REP + [nm for nm, _, _ in SH]
    each = lambda src: [rows(src[nm]) for nm in small]
    d2, m2, v2 = _adamw_many(each(wts), each(g_small), each(mom), each(var), "adamw_small")
    for i, nm in enumerate(small):
        n, shp = wts[nm].size, wts[nm].shape
        take = lambda a: a.reshape(-1)[:n].reshape(shp)
        out_g[nm], out_d[nm], out_m[nm], out_v[nm] = g_small[nm], take(d2[i]), take(m2[i]), take(v2[i])

    return (loss, grad_x, *[out_g[nm] for nm in NAMES], *[out_d[nm] for nm in NAMES],
            *[out_m[nm] for nm in NAMES], *[out_v[nm] for nm in NAMES])
```

```python
import functools

import jax
import jax.numpy as jnp
from jax import lax
from jax.experimental import pallas as pl
from jax.experimental.pallas import tpu as pltpu

F32 = jnp.float32
BF = jnp.bfloat16
MESH = pl.DeviceIdType.MESH
ANY = pl.BlockSpec(memory_space=pl.ANY)

N_DEV = 8
D = 2048
W = 1024
EVEN_IN = 5376
ODD_IN = 4096
CHUNK = 128
ALPHA = (2 * 4) ** 0.25
LN_EPS = 1e-5
ROPE_THETA = 500000.0
LRU_C = 8.0
LR, B1, B2, ADAM_EPS, WD, STEP = 0.001, 0.9, 0.999, 1e-08, 0.01, 10
NEG = -1e30
HEAD_COLS = 4


def _cp(vmem_mb=48, collective_id=None):
    return pltpu.CompilerParams(vmem_limit_bytes=vmem_mb * 1024 * 1024, collective_id=collective_id)


def _sig(x):
    return jax.nn.sigmoid(x)


def _silu_grad(x):
    s = _sig(x)
    return x * s, s * (1.0 + x * (1.0 - s))


def _dot(a, b):
    return jnp.dot(a, b, preferred_element_type=F32)


def _dot_nt(a, b):
    return lax.dot_general(a, b, (((1,), (1,)), ((), ())), preferred_element_type=F32)


def _dot_tn(a, b):
    return lax.dot_general(a, b, (((0,), (0,)), ((), ())), preferred_element_type=F32)


def _coords():
    return lax.axis_index("x"), lax.axis_index("y"), lax.axis_index("c")


def _chip(j):
    x, y, _ = _coords()
    return (1 - x if j & 2 else x), (1 - y if j & 1 else y)


X_NB, Y_NB, DIAG, SIB = 4, 2, 6, 1
EVERYONE = frozenset(range(1, N_DEV))
BARRIER_IDS = {}


class _Comm:
    def collective_id(self):
        return BARRIER_IDS.setdefault(frozenset(self.peers), len(BARRIER_IDS))

    def start(self, ins, outs, sems):
        barrier = pltpu.get_barrier_semaphore()
        for d in sorted(self.peers):
            pl.semaphore_signal(barrier, inc=1, device_id=_peer(d)[0], device_id_type=MESH)
        pl.semaphore_wait(barrier, len(self.peers))
        for cp in self.copies(ins, outs, sems):
            cp.start()

    def wait(self, ins, outs, sems):
        for cp in self.copies(ins, outs, sems):
            cp.wait()


class _Join(_Comm):
    def __init__(self, parts):
        self.parts = list(parts)
        self.peers = frozenset().union(*[p.peers for p in self.parts])
        self.inputs = [a for p in self.parts for a in p.inputs]
        self.out_shapes = [s for p in self.parts for s in p.out_shapes]
        self.sem_shapes = [s for p in self.parts for s in p.sem_shapes]
        self.aliases = {}
        i0 = o0 = 0
        for p in self.parts:
            for i, o in getattr(p, "aliases", {}).items():
                self.aliases[i0 + i] = o0 + o
            i0, o0 = i0 + len(p.inputs), o0 + len(p.out_shapes)

    def copies(self, ins, outs, sems):
        res = []
        i0 = o0 = s0 = 0
        for p in self.parts:
            ni, no, ns = len(p.inputs), len(p.out_shapes), len(p.sem_shapes)
            res += p.copies(ins[i0:i0 + ni], outs[o0:o0 + no], sems[s0:s0 + ns])
            i0, o0, s0 = i0 + ni, o0 + no, s0 + ns
        return res


ROWS_US = {"ag1": 0.104, "ag2": 0.052, "agd": 0.027, "rsd": 0.027, "rs": 0.205}
N_COPIES = {"ag1": 2, "ag2": 2, "agd": 4, "rsd": 4, "rs": 3}
TASK_PEERS = {"ag1": {X_NB, Y_NB}, "ag2": {X_NB, Y_NB}, "agd": {SIB}, "rsd": {SIB}, "rs": {X_NB, Y_NB, DIAG}}
ROW_CHUNK = {672: 224, 512: 128, 256: 128}
CARRY_US = {"mm_h_even": 58, "mm_h_odd": 47, "even_fwd": 42, "odd_c_fwd": 37, "mm_out_ln": 33, "ln_bwd": 23, "mm_dmix": 26,
            "mm_dw_out": 25, "even_bwd": 95, "odd_c_bwd": 70, "mm_dw_in_even": 56, "mm_dw_in_odd": 44, "mm_dx_even": 60,
            "mm_dx_odd": 50, "adamw_even_w_in": 30, "adamw_odd_w_in": 28, "adamw_even_w_out": 11, "adamw_odd_w_out": 11}
FWD_OVERBOOK = 1.15
FLUSH_EXTRA_US = 60.0


def _cost_us(task, reg):
    kind, src, _, lo, hi = task
    return ROWS_US[kind] * (hi - lo) * reg[src].shape[-1] * reg[src].dtype.itemsize / 4096.0


class _Copies(_Comm):
    def __init__(self, tasks, reg):
        self.tasks = list(tasks)
        self.out_names, self.in_names = [], []
        for kind, src, dst, lo, hi in self.tasks:
            if dst not in self.out_names:
                self.out_names.append(dst)
        for kind, src, dst, lo, hi in self.tasks:
            if src not in self.out_names and src not in self.in_names:
                self.in_names.append(src)
        self.out_shapes, self.aliases = [], {}
        for o, dst in enumerate(self.out_names):
            if dst in reg:
                self.aliases[len(self.in_names)] = o
                self.in_names.append(dst)
                self.out_shapes.append(jax.ShapeDtypeStruct(reg[dst].shape, reg[dst].dtype))
            else:
                kind, src = next((t[0], t[1]) for t in self.tasks if t[2] == dst)
                shape = ({"rsd": 4, "rs": 3}[kind],) + reg[src].shape[1:]
                self.out_shapes.append(jax.ShapeDtypeStruct(shape, reg[src].dtype))
        self.inputs = [reg[nm] for nm in self.in_names]
        n = sum(N_COPIES[t[0]] for t in self.tasks)
        self.sem_shapes = [pltpu.SemaphoreType.DMA((n,)), pltpu.SemaphoreType.DMA((n,))]
        self.peers = frozenset().union(*[TASK_PEERS[t[0]] for t in self.tasks])

    def copies(self, ins, outs, sems):
        send, recv = sems
        x, y, c = _coords()
        me = 4 * x + 2 * y + c
        xn, yn = (1 - x, y, c), (x, 1 - y, c)
        at_xn, at_yn = 4 * (1 - x) + 2 * y + c, 4 * x + 2 * (1 - y) + c
        ref = dict(zip(self.in_names, ins))
        ref.update(zip(self.out_names, outs))
        res = []

        def copy(src, dst, to):
            i = len(res)
            res.append(pltpu.make_async_remote_copy(src_ref=src, dst_ref=dst, send_sem=send.at[i], recv_sem=recv.at[i],
                                                    device_id=to, device_id_type=MESH))

        for kind, src, dst, lo, hi in self.tasks:
            n = hi - lo
            if kind == "ag1":
                for to in (xn, yn):
                    copy(ref[src].at[pl.ds(lo, n)], ref[dst].at[me, pl.ds(lo, n)], to)
            elif kind == "ag2":
                h = n // 2
                first, second = ref[dst].at[at_xn, pl.ds(lo, h)], ref[dst].at[at_yn, pl.ds(lo + h, n - h)]
                copy(first, first, yn)
                copy(second, second, xn)
            elif kind == "agd":
                for j in range(4):
                    px, py = _chip(j)
                    rows = ref[dst].at[4 * px + 2 * py + c, pl.ds(lo, n)]
                    copy(rows, rows, (x, y, 1 - c))
            elif kind == "rsd":
                for j in range(4):
                    px, py = _chip(j)
                    copy(ref[src].at[4 * px + 2 * py + 1 - c, pl.ds(lo, n)], ref[dst].at[j, pl.ds(lo, n)], (x, y, 1 - c))
            else:
                for j in (1, 2, 3):
                    px, py = _chip(j)
                    copy(ref[src].at[j, pl.ds(lo, n)], ref[dst].at[j - 1, pl.ds(lo, n)], (px, py, c))
        return res


class _Sched:
    def __init__(self, reg):
        self.reg, self.queue, self.later = reg, [], []
        self.overhang = 0.5
        self.after_landing = None

    def add(self, tasks, first=False):
        self.queue = list(tasks) + self.queue if first else self.queue + list(tasks)

    def pending(self, dst):
        return any(t[2] == dst for t in self.queue + self.later)

    def take(self, budget_us, must=None, overhang=0.5):
        self.queue, self.later = self.later + self.queue, []
        picked, us = [], 0.0
        rest = []
        for t in self.queue:
            cost = _cost_us(t, self.reg)
            if (must is not None and t[2] == must) or us + (1.0 - overhang) * cost <= budget_us:
                picked.append(t)
                us += cost
                if t[0] in ("ag1", "ag2"):
                    self.later.append(({"ag1": "ag2", "ag2": "agd"}[t[0]], t[2], t[2], t[3], t[4]))
            else:
                rest.append(t)
        self.queue = rest
        return _Copies(picked, self.reg) if picked else None

    def landed(self, comm, got):
        if comm is not None:
            for nm, a in zip(comm.out_names, got):
                self.reg[nm] = a
        if self.after_landing is not None:
            self.after_landing()

    def run(self, builder, budget_us, *args, **kw):
        comm = self.take(budget_us, overhang=self.overhang)
        res, got = builder(*args, comm=comm, **kw)
        self.landed(comm, got)
        return res

    def flush(self, dst, budget_us=0.0, beside=None):
        res = []
        while self.pending(dst):
            comm = self.take(budget_us, must=dst)
            got = _comm_only(comm if beside is None else _Join([comm, beside]), "flush_" + dst)
            res, beside = got[len(comm.out_shapes):], None
            self.landed(comm, got[:len(comm.out_shapes)])
        return res


def _rows(name_src, name_dst, kind, n_rows, chunk):
    return [(kind, name_src, name_dst, lo, min(lo + chunk, n_rows)) for lo in range(0, n_rows, chunk)]


def _pcall(body, *, grid, in_specs, out_specs, out_shape, name, scratch=(), vmem=48, comm=None):
    in_specs, out_specs, out_shape, scratch = list(in_specs), list(out_specs), list(out_shape), list(scratch)
    if comm is None:
        call = pl.pallas_call(body, grid=grid, in_specs=in_specs, out_specs=out_specs, out_shape=out_shape,
                              scratch_shapes=scratch, name=name, compiler_params=_cp(vmem))
        return lambda *args: (call(*args), [])
    n_in, n_out, n_scr = len(in_specs), len(out_specs), len(scratch)
    c_in, c_out = len(comm.inputs), len(comm.out_shapes)
    aliases = {n_in + i: n_out + o for i, o in getattr(comm, "aliases", {}).items()}

    def wrapped(*refs):
        ins, cins = refs[:n_in], refs[n_in:n_in + c_in]
        o0 = n_in + c_in
        outs, couts = refs[o0:o0 + n_out], refs[o0 + n_out:o0 + n_out + c_out]
        s0 = o0 + n_out + c_out
        scr, sems = refs[s0:s0 + n_scr], refs[s0 + n_scr:]
        ids = [pl.program_id(a) for a in range(len(grid))]
        first = functools.reduce(jnp.logical_and, [i == 0 for i in ids])
        last = functools.reduce(jnp.logical_and, [i == g - 1 for i, g in zip(ids, grid)])

        @pl.when(first)
        def _():
            comm.start(cins, couts, sems)

        body(*ins, *outs, *scr)

        @pl.when(last)
        def _():
            comm.wait(cins, couts, sems)

    call = pl.pallas_call(wrapped, grid=grid, in_specs=in_specs + [ANY] * c_in, out_specs=out_specs + [ANY] * c_out,
                          out_shape=out_shape + list(comm.out_shapes), scratch_shapes=scratch + list(comm.sem_shapes),
                          input_output_aliases=aliases, name=name, compiler_params=_cp(vmem, comm.collective_id()))

    def run(*args):
        res = call(*args, *comm.inputs)
        return res[:n_out], res[n_out:]

    return run


def _comm_only(comm, name):
    c_in, c_out = len(comm.inputs), len(comm.out_shapes)

    def body(*refs):
        cins, couts, sems = refs[:c_in], refs[c_in:c_in + c_out], refs[c_in + c_out:]
        comm.start(cins, couts, sems)
        comm.wait(cins, couts, sems)

    return pl.pallas_call(body, in_specs=[ANY] * c_in, out_specs=[ANY] * c_out, out_shape=list(comm.out_shapes),
                          scratch_shapes=list(comm.sem_shapes), input_output_aliases=dict(getattr(comm, "aliases", {})),
                          name=name, compiler_params=pltpu.CompilerParams(collective_id=comm.collective_id()))(*comm.inputs)


def _chip_blocks():
    _, _, c = _coords()
    return jnp.stack([4 * px + 2 * py + c for px, py in map(_chip, range(4))]).astype(jnp.int32)


def _add_pairs(g8, b4, name):
    _, R, C = b4.shape

    def body(idx_ref, a_ref, b_ref, o_ref):
        o_ref[...] = (a_ref[...].astype(F32) + b_ref[...].astype(F32)).astype(BF)

    blk = pl.BlockSpec((None, R, C), lambda j, idx: (j, 0, 0))
    grid_spec = pltpu.PrefetchScalarGridSpec(
        num_scalar_prefetch=1, grid=(4,),
        in_specs=[pl.BlockSpec((None, R, C), lambda j, idx: (idx[j], 0, 0)), blk], out_specs=blk)
    return pl.pallas_call(body, grid_spec=grid_spec, out_shape=jax.ShapeDtypeStruct(b4.shape, BF), name=name,
                          compiler_params=_cp())(_chip_blocks(), g8, b4)


def _rs_final(s4, r3, name):
    _, R, C = s4.shape
    tr = R // 2

    def body(s_ref, r_ref, o_ref):
        o_ref[...] = ((s_ref[...].astype(F32) + r_ref[0].astype(F32)) + r_ref[1].astype(F32)) + r_ref[2].astype(F32)

    return pl.pallas_call(
        body, grid=(2,),
        in_specs=[pl.BlockSpec((None, tr, C), lambda i: (0, i, 0)), pl.BlockSpec((3, tr, C), lambda i: (0, i, 0))],
        out_specs=pl.BlockSpec((tr, C), lambda i: (i, 0)), out_shape=jax.ShapeDtypeStruct((R, C), F32),
        name=name, compiler_params=_cp())(s4, r3)


def _mm_nt(a, w, tm, tn, name, out3=False, comm=None):
    M, K = a.shape
    N = w.shape[0]
    tm = min(tm, M)

    def body(a_ref, w_ref, o_ref):
        o_ref[...] = _dot_nt(a_ref[...], w_ref[...])

    if out3:
        per = W // tn
        out_shape = jax.ShapeDtypeStruct((N // W, M, W), F32)
        out_spec = pl.BlockSpec((None, tm, tn), lambda i, j: (j // per, i, j % per))
    else:
        out_shape = jax.ShapeDtypeStruct((M, N), F32)
        out_spec = pl.BlockSpec((tm, tn), lambda i, j: (i, j))
    (res,), extra = _pcall(
        body, grid=(M // tm, N // tn),
        in_specs=[pl.BlockSpec((tm, K), lambda i, j: (i, 0)), pl.BlockSpec((tn, K), lambda i, j: (j, 0))],
        out_specs=[out_spec], out_shape=[out_shape], name=name, comm=comm)(a, w)
    return res, extra


def _mm_tn(a, b, tm, name, comm=None):
    K, N = b.shape
    if a.ndim == 3:
        M = a.shape[0] * W
        per = W // tm
        a_spec = pl.BlockSpec((None, K, tm), lambda i: (i // per, 0, i % per))
    else:
        M = a.shape[1]
        a_spec = pl.BlockSpec((K, tm), lambda i: (0, i))

    def body(a_ref, b_ref, o_ref):
        o_ref[...] = _dot_tn(a_ref[...], b_ref[...]).astype(BF)

    (out,), extra = _pcall(
        body, grid=(M // tm,),
        in_specs=[a_spec, pl.BlockSpec((K, N), lambda i: (0, 0))],
        out_specs=[pl.BlockSpec((tm, N), lambda i: (i, 0))],
        out_shape=[jax.ShapeDtypeStruct((M, N), BF)], name=name, vmem=56, comm=comm)(a, b)
    return out, extra


def _mm_nn_res(a, w, res, tm, tn, name, comm=None):
    K, N = w.shape
    if a.ndim == 3:
        P, M = a.shape[0], a.shape[1]
        tm = min(tm, M)
        a_spec = pl.BlockSpec((P, tm, W), lambda i, j: (0, i, 0))
    else:
        P, M = 0, a.shape[0]
        tm = min(tm, M)
        a_spec = pl.BlockSpec((tm, K), lambda i, j: (i, 0))

    def body(a_ref, w_ref, r_ref, o_ref):
        if P:
            d = _dot(a_ref[0], w_ref[0:W, :])
            for p in range(1, P):
                d = d + _dot(a_ref[p], w_ref[p * W:(p + 1) * W, :])
        else:
            d = _dot(a_ref[...], w_ref[...])
        o_ref[...] = ALPHA * r_ref[...] + d

    (out,), extra = _pcall(
        body, grid=(M // tm, N // tn),
        in_specs=[a_spec, pl.BlockSpec((K, tn), lambda i, j: (0, j)), pl.BlockSpec((tm, tn), lambda i, j: (i, j))],
        out_specs=[pl.BlockSpec((tm, tn), lambda i, j: (i, j))],
        out_shape=[jax.ShapeDtypeStruct((M, N), F32)], name=name, comm=comm)(a, w, res)
    return out, extra


def _mm_out_ln(mix3, w_out, x, g, b, name, comm=None):
    S = x.shape[0]
    tm = min(256, S)

    def body(m_ref, w_ref, x_ref, g_ref, b_ref, z_ref, xn_ref, xb_ref):
        acc = _dot(m_ref[0], w_ref[0:W, :]) + _dot(m_ref[1], w_ref[W:2 * W, :])
        z = ALPHA * x_ref[...] + acc
        mu = jnp.mean(z, axis=1, keepdims=True)
        zc = z - mu
        var = jnp.mean(zc * zc, axis=1, keepdims=True)
        xn = zc * lax.rsqrt(var + LN_EPS) * g_ref[...] + b_ref[...]
        z_ref[...] = z
        xn_ref[...] = xn
        xb_ref[...] = xn.astype(BF)

    row = pl.BlockSpec((tm, D), lambda i: (i, 0))
    vec = pl.BlockSpec((1, D), lambda i: (0, 0))
    return _pcall(
        body, grid=(S // tm,),
        in_specs=[pl.BlockSpec((2, tm, W), lambda i: (0, i, 0)), pl.BlockSpec((D, D), lambda i: (0, 0)), row, vec, vec],
        out_specs=[row, row, row],
        out_shape=[jax.ShapeDtypeStruct((S, D), F32), jax.ShapeDtypeStruct((S, D), F32), jax.ShapeDtypeStruct((S, D), BF)],
        name=name, comm=comm)(mix3, w_out, x, g.reshape(1, D), b.reshape(1, D))


def _ln_bwd(dxn, z, g, name, comm=None):
    S = z.shape[0]
    tm = min(256, S)

    def body(d_ref, z_ref, g_ref, dz_ref, dzb_ref, dg_ref, db_ref):
        i = pl.program_id(0)
        zz = z_ref[...]
        mu = jnp.mean(zz, axis=1, keepdims=True)
        zc = zz - mu
        var = jnp.mean(zc * zc, axis=1, keepdims=True)
        rstd = lax.rsqrt(var + LN_EPS)
        xhat = zc * rstd
        dy = d_ref[...]
        dyg = dy * g_ref[...]
        m1 = jnp.mean(dyg, axis=1, keepdims=True)
        m2 = jnp.mean(dyg * xhat, axis=1, keepdims=True)
        dz = rstd * (dyg - m1 - xhat * m2)
        dz_ref[...] = dz
        dzb_ref[...] = dz.astype(BF)

        @pl.when(i == 0)
        def _():
            dg_ref[...] = jnp.zeros_like(dg_ref)
            db_ref[...] = jnp.zeros_like(db_ref)

        dg_ref[...] += jnp.sum(dy * xhat, axis=0, keepdims=True)
        db_ref[...] += jnp.sum(dy, axis=0, keepdims=True)

    row = pl.BlockSpec((tm, D), lambda i: (i, 0))
    vec = pl.BlockSpec((1, D), lambda i: (0, 0))
    return _pcall(
        body, grid=(S // tm,), in_specs=[row, row, vec], out_specs=[row, row, vec, vec],
        out_shape=[jax.ShapeDtypeStruct((S, D), F32), jax.ShapeDtypeStruct((S, D), BF),
                   jax.ShapeDtypeStruct((1, D), F32), jax.ShapeDtypeStruct((1, D), F32)],
        name=name, comm=comm)(dxn, z, g.reshape(1, D))


def _loss_grad(xn, target):
    S = xn.shape[0]
    tm = min(256, S)

    def body(x_ref, t_ref, d_ref, p_ref):
        i = pl.program_id(0)
        e = x_ref[...] - t_ref[...]
        d_ref[...] = e * (1.0 / D)

        @pl.when(i == 0)
        def _():
            p_ref[...] = jnp.zeros_like(p_ref)

        p_ref[...] += jnp.sum(jnp.sum(e * e, axis=1, keepdims=True), axis=0, keepdims=True)

    row = pl.BlockSpec((tm, D), lambda i: (i, 0))
    return pl.pallas_call(
        body, grid=(S // tm,), in_specs=[row, row],
        out_specs=[row, pl.BlockSpec((8, 128), lambda i: (0, 0))],
        out_shape=[jax.ShapeDtypeStruct((S, D), F32), jax.ShapeDtypeStruct((8, 128), F32)],
        name="loss_grad", compiler_params=_cp(),
    )(xn, target)


def _rope_fwd(t, r_ref):
    return (t * r_ref[:, 0:128] + pltpu.roll(t, 120, 1) * r_ref[:, 128:256]
            + pltpu.roll(t, 8, 1) * r_ref[:, 256:384])


def _rope_bwd(g, r_ref):
    return (g * r_ref[:, 0:128] + pltpu.roll(g * r_ref[:, 128:256], 8, 1)
            + pltpu.roll(g * r_ref[:, 256:384], 120, 1))


def _dup_heads(kb):
    lo = lax.broadcasted_iota(jnp.int32, kb.shape, 1) < 64
    sw = pltpu.roll(kb, 64, 1)
    return [jnp.where(lo, kb, sw).astype(BF), jnp.where(lo, sw, kb).astype(BF)]


def _even_fwd(h, rope, lng, lnb, ws, bsb, sinks, name, comm=None):
    S = h.shape[0]
    nb = S // CHUNK

    def body(h_ref, hp_ref, rc_ref, rp_ref, lng_ref, lnb_ref, ws_ref, bsb_ref, sink_ref, mix_ref, o_ref, l_ref):
        n = pl.program_id(0)
        lane = lax.broadcasted_iota(jnp.int32, (128, 128), 1)
        rowi = lax.broadcasted_iota(jnp.int32, (128, 128), 0)
        tri = rowi >= lane
        lane_lo = lane < 64
        v = h_ref[:, W:2 * W]
        mu = jnp.mean(v, axis=1, keepdims=True)
        vc = v - mu
        var = jnp.mean(vc * vc, axis=1, keepdims=True)
        vn = vc * lax.rsqrt(var + LN_EPS) * lng_ref[...] + lnb_ref[...]
        ms = [_dot(jnp.where(tri, ws_ref[g], 0.0).astype(BF), vn[:, g * 128:(g + 1) * 128].astype(BF)) for g in range(8)]
        for g in range(8):
            sl = slice(g * 128, (g + 1) * 128)
            ag = h_ref[:, 2 * W + g * 128:2 * W + (g + 1) * 128]
            mix_ref[0, :, sl] = (h_ref[:, sl] * (ms[g] + bsb_ref[g]) * (ag * _sig(ag))).astype(BF)
        kb = jnp.concatenate([_rope_fwd(hp_ref[:, 0:128], rp_ref), _rope_fwd(h_ref[:, 4096:4224], rc_ref)], axis=0)
        vb = jnp.concatenate([hp_ref[:, 128:256], h_ref[:, 4224:4352]], axis=0)
        k2 = _dup_heads(kb)
        v2 = _dup_heads(vb)
        qi = lax.broadcasted_iota(jnp.int32, (128, 256), 0)
        kj = lax.broadcasted_iota(jnp.int32, (128, 256), 1)
        diff = qi + 128 - kj
        valid = (diff >= 0) & (diff < 128) & ((n > 0) | (kj >= 128))
        lacc = jnp.zeros((128, 128), F32)
        for j0 in range(0, 8, HEAD_COLS):
            heads = [(j, half) for j in range(j0, j0 + HEAD_COLS) for half in range(2)]
            sc, pr, oh = {}, {}, {}
            for j in range(j0, j0 + HEAD_COLS):
                qc = _rope_fwd(h_ref[:, 3072 + j * 128:3072 + (j + 1) * 128], rc_ref)
                sc[j, 0] = _dot_nt(jnp.where(lane_lo, qc, 0.0).astype(BF), k2[j // 4])
                sc[j, 1] = _dot_nt(jnp.where(lane_lo, 0.0, qc).astype(BF), k2[j // 4])
            for j, half in heads:
                hq = 2 * j + half
                s = jnp.where(valid, sc[j, half] * 0.125, NEG)
                sk = sink_ref[hq]
                mx = jnp.maximum(jnp.max(s, axis=1, keepdims=True), sk)
                p = jnp.exp(s - mx)
                den = jnp.sum(p, axis=1, keepdims=True) + jnp.exp(sk - mx)
                pr[j, half] = (p / den).astype(BF)
                lacc = jnp.where(lane == hq, mx + jnp.log(den), lacc)
            for j, half in heads:
                oh[j, half] = _dot(pr[j, half], v2[j // 4])
            for j in range(j0, j0 + HEAD_COLS):
                cs = slice(j * 128, (j + 1) * 128)
                ocol = jnp.where(lane_lo, oh[j, 0], oh[j, 1])
                bg = h_ref[:, 4352 + j * 128:4352 + (j + 1) * 128]
                o_ref[:, cs] = ocol
                mix_ref[1, :, cs] = (ocol * (bg * _sig(bg))).astype(BF)
        l_ref[...] = lacc

    prev = lambda n: jnp.maximum(n - 1, 0)
    full = lambda shape: pl.BlockSpec(shape, lambda n: (0,) * len(shape))
    return _pcall(
        body, grid=(nb,),
        in_specs=[pl.BlockSpec((CHUNK, EVEN_IN), lambda n: (n, 0)),
                  pl.BlockSpec((CHUNK, 256), lambda n: (prev(n), 16)),
                  pl.BlockSpec((CHUNK, 384), lambda n: (n, 0)),
                  pl.BlockSpec((CHUNK, 384), lambda n: (prev(n), 0)),
                  full((1, W)), full((1, W)), full((8, 128, 128)), full((8, 128, 128)),
                  pl.BlockSpec(memory_space=pltpu.SMEM)],
        out_specs=[pl.BlockSpec((2, CHUNK, W), lambda n: (0, n, 0)),
                   pl.BlockSpec((CHUNK, W), lambda n: (n, 0)),
                   pl.BlockSpec((CHUNK, 128), lambda n: (n, 0))],
        out_shape=[jax.ShapeDtypeStruct((2, S, W), BF), jax.ShapeDtypeStruct((S, W), F32),
                   jax.ShapeDtypeStruct((S, 128), F32)],
        name=name, comm=comm)(h, h, rope, rope, lng.reshape(1, W), lnb.reshape(1, W), ws, bsb, sinks)


def _even_bwd(h, dmix3, o, l, rope, lng, lnb, ws, wst, bsb, sinks, name, comm=None):
    S = h.shape[0]
    nb = S // CHUNK

    def body(h_ref, hp_ref, hn_ref, dm_ref, dmn_ref, o_ref, on_ref, l_ref, ln_ref, rc_ref, rp_ref, rn_ref,
             lng_ref, lnb_ref, ws_ref, wst_ref, bsb_ref, sink_ref,
             dh_ref, dws_ref, dbs_ref, dlng_ref, dlnb_ref, dsink_ref, dvn_ref):
        n = pl.program_id(0)

        @pl.when(n == 0)
        def _():
            dws_ref[...] = jnp.zeros_like(dws_ref)
            dbs_ref[...] = jnp.zeros_like(dbs_ref)
            dlng_ref[...] = jnp.zeros_like(dlng_ref)
            dlnb_ref[...] = jnp.zeros_like(dlnb_ref)
            dsink_ref[...] = jnp.zeros_like(dsink_ref)

        lane = lax.broadcasted_iota(jnp.int32, (128, 128), 1)
        rowi = lax.broadcasted_iota(jnp.int32, (128, 128), 0)
        lane1 = lax.broadcasted_iota(jnp.int32, (1, 128), 1)
        tri = rowi >= lane
        tri_t = lane >= rowi
        lane_lo = lane < 64
        v = h_ref[:, W:2 * W]
        mu = jnp.mean(v, axis=1, keepdims=True)
        vc = v - mu
        var = jnp.mean(vc * vc, axis=1, keepdims=True)
        rstd = lax.rsqrt(var + LN_EPS)
        vhat = vc * rstd
        vn = vhat * lng_ref[...] + lnb_ref[...]
        dbs_acc = jnp.zeros((128, 128), F32)
        vng = [vn[:, g * 128:(g + 1) * 128].astype(BF) for g in range(8)]
        ms = [_dot(jnp.where(tri, ws_ref[g], 0.0).astype(BF), vng[g]) for g in range(8)]
        dmb = []
        for g in range(8):
            sl = slice(g * 128, (g + 1) * 128)
            m = ms[g] + bsb_ref[g]
            ag = h_ref[:, 2 * W + g * 128:2 * W + (g + 1) * 128]
            sg, dsg = _silu_grad(ag)
            u = h_ref[:, sl]
            da = dm_ref[0, :, sl]
            dmm = da * u * sg
            dh_ref[:, sl] = (da * m * sg).astype(BF)
            dh_ref[:, 2 * W + g * 128:2 * W + (g + 1) * 128] = (da * u * m * dsg).astype(BF)
            dmb.append(dmm.astype(BF))
            dbs_acc = jnp.where(lane == g, jnp.sum(dmm, axis=1, keepdims=True), dbs_acc)
        dvs = [_dot(jnp.where(tri_t, wst_ref[g], 0.0).astype(BF), dmb[g]) for g in range(8)]
        dwss = [_dot_nt(dmb[g], vng[g]) for g in range(8)]
        for g in range(8):
            dvn_ref[:, g * 128:(g + 1) * 128] = dvs[g]
            dws_ref[g] += jnp.where(tri, dwss[g], 0.0)
        dbs_ref[...] += dbs_acc
        dvn = dvn_ref[...]
        dlng_ref[...] += jnp.sum(dvn * vhat, axis=0, keepdims=True)
        dlnb_ref[...] += jnp.sum(dvn, axis=0, keepdims=True)
        dyg = dvn * lng_ref[...]
        m1 = jnp.mean(dyg, axis=1, keepdims=True)
        m2 = jnp.mean(dyg * vhat, axis=1, keepdims=True)
        dh_ref[:, W:2 * W] = (rstd * (dyg - m1 - vhat * m2)).astype(BF)
        kcur = _rope_fwd(h_ref[:, 4096:4224], rc_ref)
        kb = jnp.concatenate([_rope_fwd(hp_ref[:, 0:128], rp_ref), kcur], axis=0)
        vb = jnp.concatenate([hp_ref[:, 128:256], h_ref[:, 4224:4352]], axis=0)
        k2 = _dup_heads(kb)
        v2 = _dup_heads(vb)
        kc2 = _dup_heads(kcur)
        vc2 = _dup_heads(h_ref[:, 4224:4352])
        qi = lax.broadcasted_iota(jnp.int32, (128, 256), 0)
        kj = lax.broadcasted_iota(jnp.int32, (128, 256), 1)
        diff = qi + 128 - kj
        valid = (diff >= 0) & (diff < 128) & ((n > 0) | (kj >= 128))
        validn = (lane > rowi) & (n < nb - 1)
        lc = l_ref[...]
        lnx = ln_ref[...]
        dk = [jnp.zeros((128, 128), F32), jnp.zeros((128, 128), F32)]
        dv = [jnp.zeros((128, 128), F32), jnp.zeros((128, 128), F32)]
        dsk_acc = jnp.zeros((1, 128), F32)
        for j0 in range(0, 8, HEAD_COLS):
            heads = [(j, half) for j in range(j0, j0 + HEAD_COLS) for half in range(2)]
            t = {}
            for j in range(j0, j0 + HEAD_COLS):
                cs = slice(j * 128, (j + 1) * 128)
                qc = _rope_fwd(h_ref[:, 3072 + j * 128:3072 + (j + 1) * 128], rc_ref)
                qn = _rope_fwd(hn_ref[:, 3072 + j * 128:3072 + (j + 1) * 128], rn_ref)
                bg = h_ref[:, 4352 + j * 128:4352 + (j + 1) * 128]
                sgb, dsgb = _silu_grad(bg)
                db = dm_ref[1, :, cs]
                oc = o_ref[:, cs]
                do = db * sgb
                dh_ref[:, 4352 + j * 128:4352 + (j + 1) * 128] = (db * oc * dsgb).astype(BF)
                bgn = hn_ref[:, 4352 + j * 128:4352 + (j + 1) * 128]
                don = dmn_ref[1, :, cs] * (bgn * _sig(bgn))
                prod = do * oc
                prodn = don * on_ref[:, cs]
                for half in range(2):
                    hq = 2 * j + half
                    hm = lane_lo if half == 0 else jnp.logical_not(lane_lo)
                    t[j, half] = dict(
                        dsum=jnp.sum(jnp.where(hm, prod, 0.0), axis=1, keepdims=True),
                        dsumn=jnp.sum(jnp.where(hm, prodn, 0.0), axis=1, keepdims=True),
                        lh=jnp.sum(jnp.where(lane == hq, lc, 0.0), axis=1, keepdims=True),
                        lhn=jnp.sum(jnp.where(lane == hq, lnx, 0.0), axis=1, keepdims=True),
                        qm=jnp.where(hm, qc, 0.0).astype(BF), dom=jnp.where(hm, do, 0.0).astype(BF),
                        qnm=jnp.where(hm, qn, 0.0).astype(BF), donm=jnp.where(hm, don, 0.0).astype(BF))
            for j, half in heads:
                e, hk = t[j, half], j // 4
                e["s"], e["dp"] = _dot_nt(e["qm"], k2[hk]), _dot_nt(e["dom"], v2[hk])
                e["sn"], e["dpn"] = _dot_nt(e["qnm"], kc2[hk]), _dot_nt(e["donm"], vc2[hk])
            for j, half in heads:
                e, hq = t[j, half], 2 * j + half
                p = jnp.exp(jnp.where(valid, e["s"] * 0.125 - e["lh"], NEG))
                ds = p * (e["dp"] - e["dsum"])
                pn = jnp.exp(jnp.where(validn, e["sn"] * 0.125 - e["lhn"], NEG))
                dsn = pn * (e["dpn"] - e["dsumn"])
                psink = jnp.exp(sink_ref[hq] - e["lh"])
                dsk_acc = jnp.where(lane1 == hq, -jnp.sum(psink * e["dsum"], axis=0, keepdims=True), dsk_acc)
                e["ds"] = ds.astype(BF)
                e["pt"], e["dst"] = jnp.transpose(p[:, 128:256]).astype(BF), jnp.transpose(ds[:, 128:256]).astype(BF)
                e["pnt"], e["dsnt"] = jnp.transpose(pn).astype(BF), jnp.transpose(dsn).astype(BF)
            for j, half in heads:
                e, hk = t[j, half], j // 4
                e["dq"] = _dot(e["ds"], k2[hk])
                e["dv"] = _dot(e["pt"], e["dom"]) + _dot(e["pnt"], e["donm"])
                e["dk"] = _dot(e["dst"], e["qm"]) + _dot(e["dsnt"], e["qnm"])
            for j in range(j0, j0 + HEAD_COLS):
                hk = j // 4
                dqcol = jnp.where(lane_lo, t[j, 0]["dq"], t[j, 1]["dq"]) * 0.125
                dh_ref[:, 3072 + j * 128:3072 + (j + 1) * 128] = _rope_bwd(dqcol, rc_ref).astype(BF)
                dv[hk] = dv[hk] + t[j, 0]["dv"] + t[j, 1]["dv"]
                dk[hk] = dk[hk] + (t[j, 0]["dk"] + t[j, 1]["dk"]) * 0.125
        fold = lambda a: a + pltpu.roll(a, 64, 1)
        dh_ref[:, 4096:4224] = _rope_bwd(jnp.where(lane_lo, fold(dk[0]), fold(dk[1])), rc_ref).astype(BF)
        dh_ref[:, 4224:4352] = jnp.where(lane_lo, fold(dv[0]), fold(dv[1])).astype(BF)
        dsink_ref[...] += dsk_acc

    prev = lambda n: jnp.maximum(n - 1, 0)
    nxt = lambda n: jnp.minimum(n + 1, nb - 1)
    full = lambda shape: pl.BlockSpec(shape, lambda n: (0,) * len(shape))
    return _pcall(
        body, grid=(nb,),
        in_specs=[pl.BlockSpec((CHUNK, EVEN_IN), lambda n: (n, 0)),
                  pl.BlockSpec((CHUNK, 256), lambda n: (prev(n), 16)),
                  pl.BlockSpec((CHUNK, EVEN_IN), lambda n: (nxt(n), 0)),
                  pl.BlockSpec((2, CHUNK, W), lambda n: (0, n, 0)),
                  pl.BlockSpec((2, CHUNK, W), lambda n: (0, nxt(n), 0)),
                  pl.BlockSpec((CHUNK, W), lambda n: (n, 0)),
                  pl.BlockSpec((CHUNK, W), lambda n: (nxt(n), 0)),
                  pl.BlockSpec((CHUNK, 128), lambda n: (n, 0)),
                  pl.BlockSpec((CHUNK, 128), lambda n: (nxt(n), 0)),
                  pl.BlockSpec((CHUNK, 384), lambda n: (n, 0)),
                  pl.BlockSpec((CHUNK, 384), lambda n: (prev(n), 0)),
                  pl.BlockSpec((CHUNK, 384), lambda n: (nxt(n), 0)),
                  full((1, W)), full((1, W)), full((8, 128, 128)), full((8, 128, 128)), full((8, 128, 128)),
                  pl.BlockSpec(memory_space=pltpu.SMEM)],
        out_specs=[pl.BlockSpec((CHUNK, EVEN_IN), lambda n: (n, 0)),
                   full((8, 128, 128)), full((128, 128)), full((1, W)), full((1, W)), full((1, 128))],
        out_shape=[jax.ShapeDtypeStruct((S, EVEN_IN), BF), jax.ShapeDtypeStruct((8, 128, 128), F32),
                   jax.ShapeDtypeStruct((128, 128), F32), jax.ShapeDtypeStruct((1, W), F32),
                   jax.ShapeDtypeStruct((1, W), F32), jax.ShapeDtypeStruct((1, 128), F32)],
        scratch=[pltpu.VMEM((CHUNK, W), F32)], name=name, comm=comm,
    )(h, h, h, dmix3, dmix3, o, o, l, l, rope, rope, rope, lng.reshape(1, W), lnb.reshape(1, W), ws, wst, bsb, sinks)


def _expm1(x):
    ser = x * (1.0 + x * (0.5 + x * (1.0 / 6.0 + x * (1.0 / 24.0))))
    return jnp.where(jnp.abs(x) < 1e-2, ser, jnp.exp(x) - 1.0)


def _softplus_neg(lam):
    z = -lam
    e = jnp.exp(-jnp.abs(z))
    l1p = jnp.where(e < 1e-3, e * (1.0 - e * (0.5 - e * (1.0 / 3.0))), jnp.log(1.0 + e))
    return jnp.maximum(z, 0.0) + l1p


def _shift_down(x, k, row, fill=0.0):
    return jnp.where(row >= k, pltpu.roll(x, k, 0), fill)


def _shift_up(x, k, row, fill=0.0):
    S = x.shape[0]
    return jnp.where(row < S - k, pltpu.roll(x, S - k, 0), fill)


def _lru_gates(xc, row, cw_ref, cb_ref, wa_ref, wx_ref, ba_ref, bx_ref, lam_ref):
    xconv = (cw_ref[3:4, :] * xc + cw_ref[2:3, :] * _shift_down(xc, 1, row) + cw_ref[1:2, :] * _shift_down(xc, 2, row)
             + cw_ref[0:1, :] * _shift_down(xc, 3, row) + cb_ref[...])
    xb = xconv.astype(BF)
    r = _sig(_dot(xb, wa_ref[...]) + ba_ref[...])
    i = _sig(_dot(xb, wx_ref[...]) + bx_ref[...])
    sp = _softplus_neg(lam_ref[...])
    log_a = -LRU_C * r * sp
    a = jnp.exp(log_a)
    mult = jnp.sqrt(-_expm1(2.0 * log_a))
    return xconv, r, i, sp, a, mult


def _odd_c_fwd(h, cw, cb, wa, wx, ba, bx, lam, name, comm=None):
    S = h.shape[0]

    def body(xc_ref, cg_ref, cw_ref, cb_ref, wa_ref, wx_ref, ba_ref, bx_ref, lam_ref, mix_ref, hst_ref):
        row = lax.broadcasted_iota(jnp.int32, (S, 128), 0)
        xconv, r, i, sp, a, mult = _lru_gates(xc_ref[...], row, cw_ref, cb_ref, wa_ref, wx_ref, ba_ref, bx_ref, lam_ref)
        aa = a
        bb = mult * (i * xconv)
        k = 1
        while k < S:
            bb = aa * _shift_down(bb, k, row) + bb
            if 2 * k < S:
                aa = aa * _shift_down(aa, k, row, 1.0)
            k *= 2
        hst_ref[...] = bb
        cg = cg_ref[...]
        mix_ref[...] = (bb * (cg * _sig(cg))).astype(BF)

    col = lambda off: pl.BlockSpec((S, 128), lambda j: (0, off + j))
    vec = pl.BlockSpec((1, 128), lambda j: (0, j))
    mat = pl.BlockSpec((None, 128, 128), lambda j: (j, 0, 0))
    return _pcall(
        body, grid=(8,),
        in_specs=[col(0), col(8), pl.BlockSpec((4, 128), lambda j: (0, j)), vec, mat, mat, vec, vec, vec],
        out_specs=[pl.BlockSpec((None, S, 128), lambda j: (0, 0, j)), pl.BlockSpec((S, 128), lambda j: (0, j))],
        out_shape=[jax.ShapeDtypeStruct((2, S, W), BF), jax.ShapeDtypeStruct((S, W), F32)],
        name=name, comm=comm,
    )(h, h, cw, cb.reshape(1, W), wa, wx, ba.reshape(1, W), bx.reshape(1, W), lam.reshape(1, W))


def _pool_sums(x, g, row, shift):
    s2 = x + shift(x, 1, row)
    s4 = s2 + shift(s2, 2, row)
    s8 = s4 + shift(s4, 4, row)
    s16 = s8 + shift(s8, 8, row)
    return jnp.where(g == 0, s2, jnp.where(g == 1, s4, jnp.where(g == 2, s8, s16)))


def _odd_d_fwd(h, mix3, wp, dscale, name):
    S = h.shape[0]

    def body(xd_ref, dg_ref, wp_ref, ds_ref, mix_in, mix_ref):
        g = pl.program_id(0)
        row = lax.broadcasted_iota(jnp.int32, (S, 256), 0)
        xd = xd_ref[...]
        cnt = jnp.minimum(row + 1, jnp.left_shift(2, g)).astype(F32)
        pooled = _pool_sums(xd, g, row, _shift_down) / cnt - xd
        mixed = _dot(pooled.astype(BF), wp_ref[...])
        dg = dg_ref[...]
        mix_ref[...] = (mixed * ds_ref[...] * (dg * _sig(dg))).astype(BF)

    col = lambda off: pl.BlockSpec((S, 256), lambda g: (0, off + g))
    return pl.pallas_call(
        body, grid=(4,),
        in_specs=[col(8), col(12), pl.BlockSpec((None, 256, 256), lambda g: (g, 0, 0)),
                  pl.BlockSpec((1, 256), lambda g: (0, g)), ANY],
        out_specs=pl.BlockSpec((None, S, 256), lambda g: (1, 0, g)),
        out_shape=jax.ShapeDtypeStruct((2, S, W), BF), input_output_aliases={4: 0},
        name=name, compiler_params=_cp(),
    )(h, h, wp, dscale.reshape(1, W), mix3)


def _odd_c_bwd(h, hst, dmix3, cw, cb, wa, wx, wat, wxt, ba, bx, lam, name, comm=None):
    S = h.shape[0]

    def body(xc_ref, cg_ref, hst_ref, dc_ref, cw_ref, cb_ref, wa_ref, wx_ref, wat_ref, wxt_ref, ba_ref, bx_ref, lam_ref,
             dh_ref, dcw_ref, dcb_ref, dwa_ref, dwx_ref, dba_ref, dbx_ref, dlam_ref):
        row = lax.broadcasted_iota(jnp.int32, (S, 128), 0)
        xc = xc_ref[...]
        xconv, r, i, sp, a, mult = _lru_gates(xc, row, cw_ref, cb_ref, wa_ref, wx_ref, ba_ref, bx_ref, lam_ref)
        hst = hst_ref[...]
        cg = cg_ref[...]
        sg, dsg = _silu_grad(cg)
        dc = dc_ref[...]
        dh_ref[1] = (dc * hst * dsg).astype(BF)
        aa = _shift_up(a, 1, row)
        bb = dc * sg
        k = 1
        while k < S:
            bb = aa * _shift_up(bb, k, row) + bb
            if 2 * k < S:
                aa = aa * _shift_up(aa, k, row, 1.0)
            k *= 2
        lam_t = bb
        da = lam_t * _shift_down(hst, 1, row)
        ix = i * xconv
        dmult = lam_t * ix
        di = lam_t * mult * xconv
        dxconv = lam_t * mult * i
        dlog_a = da * a - dmult * (a * a / mult)
        dr = dlog_a * (-LRU_C * sp)
        dsp = jnp.sum(dlog_a * (-LRU_C * r), axis=0, keepdims=True)
        dlam_ref[...] = dsp * (-_sig(-lam_ref[...]))
        dpa = dr * r * (1.0 - r)
        dpx = di * i * (1.0 - i)
        dpab = dpa.astype(BF)
        dpxb = dpx.astype(BF)
        xb = xconv.astype(BF)
        dxconv = dxconv + _dot(dpab, wat_ref[...]) + _dot(dpxb, wxt_ref[...])
        dwa_ref[...] = _dot_tn(xb, dpab)
        dwx_ref[...] = _dot_tn(xb, dpxb)
        dba_ref[...] = jnp.sum(dpa, axis=0, keepdims=True)
        dbx_ref[...] = jnp.sum(dpx, axis=0, keepdims=True)
        dh_ref[0] = (cw_ref[3:4, :] * dxconv + cw_ref[2:3, :] * _shift_up(dxconv, 1, row)
                     + cw_ref[1:2, :] * _shift_up(dxconv, 2, row) + cw_ref[0:1, :] * _shift_up(dxconv, 3, row)).astype(BF)
        for j in range(4):
            src = xc if j == 3 else _shift_down(xc, 3 - j, row)
            dcw_ref[j:j + 1, :] = jnp.sum(dxconv * src, axis=0, keepdims=True)
        dcb_ref[...] = jnp.sum(dxconv, axis=0, keepdims=True)

    col = lambda off: pl.BlockSpec((S, 128), lambda j: (0, off + j))
    vec = pl.BlockSpec((1, 128), lambda j: (0, j))
    mat = pl.BlockSpec((None, 128, 128), lambda j: (j, 0, 0))
    vshape = jax.ShapeDtypeStruct((1, W), F32)
    mshape = jax.ShapeDtypeStruct((8, 128, 128), F32)
    return _pcall(
        body, grid=(8,),
        in_specs=[col(0), col(8), col(0), pl.BlockSpec((None, S, 128), lambda j: (0, 0, j)),
                  pl.BlockSpec((4, 128), lambda j: (0, j)), vec, mat, mat, mat, mat, vec, vec, vec],
        out_specs=[pl.BlockSpec((2, S, 128), lambda j: (0, 0, j)), pl.BlockSpec((4, 128), lambda j: (0, j)), vec,
                   mat, mat, vec, vec, vec],
        out_shape=[jax.ShapeDtypeStruct((4, S, W), BF), jax.ShapeDtypeStruct((4, W), F32), vshape, mshape, mshape,
                   vshape, vshape, vshape],
        name=name, vmem=56, comm=comm,
    )(h, h, hst, dmix3, cw, cb.reshape(1, W), wa, wx, wat, wxt, ba.reshape(1, W), bx.reshape(1, W), lam.reshape(1, W))


def _odd_d_bwd(h, dmix3, dh4, wp, wpt, dscale, name):
    S = h.shape[0]

    def body(xd_ref, dg_ref, dd_ref, wp_ref, wpt_ref, ds_ref, dh_in, dh_ref, dwp_ref, dds_ref):
        g = pl.program_id(0)
        row = lax.broadcasted_iota(jnp.int32, (S, 256), 0)
        xd = xd_ref[...]
        cnt = jnp.minimum(row + 1, jnp.left_shift(2, g)).astype(F32)
        pooled = _pool_sums(xd, g, row, _shift_down) / cnt - xd
        pb = pooled.astype(BF)
        mixed = _dot(pb, wp_ref[...])
        dg = dg_ref[...]
        sg, dsg = _silu_grad(dg)
        dd = dd_ref[...]
        dmixed = dd * ds_ref[...] * sg
        dds_ref[...] = jnp.sum(dd * mixed * sg, axis=0, keepdims=True)
        dh_ref[1] = (dd * mixed * ds_ref[...] * dsg).astype(BF)
        dmb = dmixed.astype(BF)
        dpooled = _dot(dmb, wpt_ref[...])
        dwp_ref[...] = _dot_tn(pb, dmb)
        dh_ref[0] = (_pool_sums(dpooled / cnt, g, row, _shift_up) - dpooled).astype(BF)

    col = lambda off: pl.BlockSpec((S, 256), lambda g: (0, off + g))
    mat = pl.BlockSpec((None, 256, 256), lambda g: (g, 0, 0))
    vec = pl.BlockSpec((1, 256), lambda g: (0, g))
    return pl.pallas_call(
        body, grid=(4,),
        in_specs=[col(8), col(12), pl.BlockSpec((None, S, 256), lambda g: (1, 0, g)), mat, mat, vec, ANY],
        out_specs=[pl.BlockSpec((2, S, 256), lambda g: (1, 0, g)), mat, vec],
        out_shape=[jax.ShapeDtypeStruct((4, S, W), BF), jax.ShapeDtypeStruct((4, 256, 256), F32),
                   jax.ShapeDtypeStruct((1, W), F32)],
        input_output_aliases={6: 0}, name=name, compiler_params=_cp(56),
    )(h, h, dmix3, wp, wpt, dscale.reshape(1, W), dh4)


def _peer(d):
    x, y, c = lax.axis_index("x"), lax.axis_index("y"), lax.axis_index("c")
    px = 1 - x if d & 4 else x
    py = 1 - y if d & 2 else y
    pc = 1 - c if d & 1 else c
    return (px, py, pc), 4 * px + 2 * py + pc


class _GatherAll(_Comm):
    def __init__(self, xs):
        self.peers = EVERYONE
        self.inputs = [xs]
        self.out_shapes = [jax.ShapeDtypeStruct((N_DEV,) + xs.shape, xs.dtype)]
        self.sem_shapes = [pltpu.SemaphoreType.DMA((N_DEV - 1,)), pltpu.SemaphoreType.DMA((N_DEV - 1,)),
                           pltpu.SemaphoreType.DMA]

    def copies(self, ins, outs, sems):
        (x_ref,), (out_ref,), (send, recv, loc) = ins, outs, sems
        _, me = _peer(0)
        res = [pltpu.make_async_copy(x_ref, out_ref.at[me], loc)]
        for d in range(1, N_DEV):
            peer, _ = _peer(d)
            res.append(pltpu.make_async_remote_copy(src_ref=x_ref, dst_ref=out_ref.at[me], send_sem=send.at[d - 1],
                                                    recv_sem=recv.at[d - 1], device_id=peer, device_id_type=MESH))
        return res


class _ExchangeAll(_Comm):
    def __init__(self, g8):
        self.peers = EVERYONE
        self.inputs = [g8]
        self.out_shapes = [jax.ShapeDtypeStruct(g8.shape, g8.dtype)]
        self.sem_shapes = [pltpu.SemaphoreType.DMA((N_DEV - 1,)), pltpu.SemaphoreType.DMA((N_DEV - 1,)),
                           pltpu.SemaphoreType.DMA]

    def copies(self, ins, outs, sems):
        (g_ref,), (out_ref,), (send, recv, loc) = ins, outs, sems
        _, me = _peer(0)
        res = [pltpu.make_async_copy(g_ref.at[me], out_ref.at[0], loc)]
        for d in range(1, N_DEV):
            peer, pidx = _peer(d)
            res.append(pltpu.make_async_remote_copy(src_ref=g_ref.at[pidx], dst_ref=out_ref.at[d], send_sem=send.at[d - 1],
                                                    recv_sem=recv.at[d - 1], device_id=peer, device_id_type=MESH))
        return res


def _sum8(r8, tr, name):
    _, R, C = r8.shape
    tr = min(tr, R)
    assert R % tr == 0

    def body(r_ref, o_ref):
        acc = r_ref[0]
        for d in range(1, N_DEV):
            acc = acc + r_ref[d]
        o_ref[...] = acc

    return pl.pallas_call(
        body, grid=(R // tr,), in_specs=[pl.BlockSpec((N_DEV, tr, C), lambda i: (0, i, 0))],
        out_specs=pl.BlockSpec((tr, C), lambda i: (i, 0)), out_shape=jax.ShapeDtypeStruct((R, C), F32),
        name=name, compiler_params=_cp(),
    )(r8)


def _adamw_math(w, g, m, v):
    m2 = B1 * m + (1.0 - B1) * g
    v2 = B2 * v + (1.0 - B2) * (g * g)
    m_hat = m2 / (1.0 - B1 ** STEP)
    v_hat = v2 / (1.0 - B2 ** STEP)
    return -LR * (m_hat / (jnp.sqrt(v_hat) + ADAM_EPS) + WD * w), m2, v2


def _adamw_many(ws, gs, ms, vs, name):
    n = len(ws)

    def body(*refs):
        for i in range(n):
            d, m2, v2 = _adamw_math(refs[i][...], refs[n + i][...], refs[2 * n + i][...], refs[3 * n + i][...])
            refs[4 * n + i][...] = d
            refs[5 * n + i][...] = m2
            refs[6 * n + i][...] = v2

    vmem = pl.BlockSpec(memory_space=pltpu.VMEM)
    shapes = [jax.ShapeDtypeStruct(w.shape, F32) for w in ws]
    res = pl.pallas_call(body, in_specs=[vmem] * (4 * n), out_specs=[vmem] * (3 * n), out_shape=shapes * 3, name=name,
                         compiler_params=_cp())(*ws, *gs, *ms, *vs)
    return res[:n], res[n:2 * n], res[2 * n:]


def _adamw(w3, gs, m3, v3, tr, name, comm=None):
    _, R, C = w3.shape

    def body(w_ref, g0_ref, g1_ref, m_ref, v_ref, d_ref, m2_ref, v2_ref, g_ref):
        g = jnp.where(pl.program_id(0) == 0, g0_ref[...], g1_ref[...])
        d_ref[...], m2_ref[...], v2_ref[...] = _adamw_math(w_ref[...], g, m_ref[...], v_ref[...])
        g_ref[...] = g

    blk = pl.BlockSpec((None, tr, C), lambda j, i: (j, i, 0))
    grad = lambda layer: pl.BlockSpec((tr, C), lambda j, i: (jnp.where(j == layer, i, 0), 0))
    shp = jax.ShapeDtypeStruct((2, R, C), F32)
    return _pcall(body, grid=(2, R // tr), in_specs=[blk, grad(0), grad(1), blk, blk], out_specs=[blk] * 4,
                  out_shape=[shp] * 4, name=name, comm=comm)(w3, gs[0], gs[1], m3, v3)


def _rep_pack(a):
    n = a.size
    pad = (-n) % 1024
    f = a.reshape(-1)
    if pad:
        f = jnp.concatenate([f, jnp.zeros((pad,), a.dtype)])
    return f.reshape(N_DEV, -1, 128)


def _rep_unpack(p, shape):
    n = 1
    for s in shape:
        n *= s
    return p.reshape(-1)[:n].reshape(shape)


def _sh_pack(a, axis):
    shp = a.shape
    a = a.reshape(shp[:axis] + (N_DEV, shp[axis] // N_DEV) + shp[axis + 1:])
    return jnp.moveaxis(a, axis, 0).reshape(N_DEV, -1, 128)


def _sh_unpack(p, shape, axis):
    a = p.reshape((N_DEV,) + shape[:axis] + (shape[axis] // N_DEV,) + shape[axis + 1:])
    return jnp.moveaxis(a, 0, axis).reshape(shape)


def _pad_rows(a, mult=8):
    pad = (-a.shape[-2]) % mult
    if pad:
        a = jnp.concatenate([a, jnp.zeros(a.shape[:-2] + (pad, a.shape[-1]), a.dtype)], axis=-2)
    return a


REP = ["even_a_ln_g", "even_a_ln_b", "even_a_ws", "even_a_bs", "even_b_sinks", "even_ln_g", "even_ln_b",
       "odd_w_a", "odd_w_x"]
SH = [("odd_conv_w", (2, 4, W), 2), ("odd_conv_b", (2, W), 1), ("odd_b_a", (2, W), 1), ("odd_b_x", (2, W), 1),
      ("odd_lam", (2, W), 1), ("odd_w_pool", (2, 4, 256, 256), 2), ("odd_d_scale", (2, W), 1),
      ("odd_ln_g", (2, D), 1), ("odd_ln_b", (2, D), 1)]
BIG = ["even_w_in", "even_w_out", "odd_w_in", "odd_w_out"]
NAMES = ["even_w_in", "even_a_ln_g", "even_a_ln_b", "even_a_ws", "even_a_bs", "even_b_sinks", "even_w_out",
         "even_ln_g", "even_ln_b", "odd_w_in", "odd_conv_w", "odd_conv_b", "odd_w_a", "odd_b_a", "odd_w_x", "odd_b_x",
         "odd_lam", "odd_w_pool", "odd_d_scale", "odd_w_out", "odd_ln_g", "odd_ln_b"]


def _rope_table(positions):
    inv = ROPE_THETA ** (-jnp.arange(0, 16, 2, dtype=F32) / 16)
    f = jnp.arange(128) % 64
    ang = positions.astype(F32)[:, None] * inv[f % 8][None, :]
    cos, sin = jnp.cos(ang), jnp.sin(ang)
    return jnp.concatenate([jnp.where(f < 16, cos, 1.0), jnp.where(f < 8, -sin, 0.0),
                            jnp.where((f >= 8) & (f < 16), sin, 0.0)], axis=1)


def kernel(x, positions, even_w_in, even_a_ln_g, even_a_ln_b, even_a_ws, even_a_bs, even_b_sinks, even_w_out, even_ln_g, even_ln_b, odd_w_in, odd_conv_w, odd_conv_b, odd_w_a, odd_b_a, odd_w_x, odd_b_x, odd_lam, odd_w_pool, odd_d_scale, odd_w_out, odd_ln_g, odd_ln_b, loss_target, m_even_w_in, m_even_a_ln_g, m_even_a_ln_b, m_even_a_ws, m_even_a_bs, m_even_b_sinks, m_even_w_out, m_even_ln_g, m_even_ln_b, m_odd_w_in, m_odd_conv_w, m_odd_conv_b, m_odd_w_a, m_odd_b_a, m_odd_w_x, m_odd_b_x, m_odd_lam, m_odd_w_pool, m_odd_d_scale, m_odd_w_out, m_odd_ln_g, m_odd_ln_b, v_even_w_in, v_even_a_ln_g, v_even_a_ln_b, v_even_a_ws, v_even_a_bs, v_even_b_sinks, v_even_w_out, v_even_ln_g, v_even_ln_b, v_odd_w_in, v_odd_conv_w, v_odd_conv_b, v_odd_w_a, v_odd_b_a, v_odd_w_x, v_odd_b_x, v_odd_lam, v_odd_w_pool, v_odd_d_scale, v_odd_w_out, v_odd_ln_g, v_odd_ln_b):
    args = (even_w_in, even_a_ln_g, even_a_ln_b, even_a_ws, even_a_bs, even_b_sinks, even_w_out, even_ln_g, even_ln_b,
            odd_w_in, odd_conv_w, odd_conv_b, odd_w_a, odd_b_a, odd_w_x, odd_b_x, odd_lam, odd_w_pool, odd_d_scale,
            odd_w_out, odd_ln_g, odd_ln_b)
    margs = (m_even_w_in, m_even_a_ln_g, m_even_a_ln_b, m_even_a_ws, m_even_a_bs, m_even_b_sinks, m_even_w_out,
             m_even_ln_g, m_even_ln_b, m_odd_w_in, m_odd_conv_w, m_odd_conv_b, m_odd_w_a, m_odd_b_a, m_odd_w_x,
             m_odd_b_x, m_odd_lam, m_odd_w_pool, m_odd_d_scale, m_odd_w_out, m_odd_ln_g, m_odd_ln_b)
    vargs = (v_even_w_in, v_even_a_ln_g, v_even_a_ln_b, v_even_a_ws, v_even_a_bs, v_even_b_sinks, v_even_w_out,
             v_even_ln_g, v_even_ln_b, v_odd_w_in, v_odd_conv_w, v_odd_conv_b, v_odd_w_a, v_odd_b_a, v_odd_w_x,
             v_odd_b_x, v_odd_lam, v_odd_w_pool, v_odd_d_scale, v_odd_w_out, v_odd_ln_g, v_odd_ln_b)
    wts = dict(zip(NAMES, args))
    mom = dict(zip(NAMES, margs))
    var = dict(zip(NAMES, vargs))
    S = x.shape[1]
    x0 = x[0]
    rope = _rope_table(positions[0])

    kinds = ("even", "odd", "even", "odd")
    blk_in = [jnp.transpose(wts[kinds[l] + "_w_in"][l // 2]).astype(BF) for l in range(4)]
    blk_out = [wts[kinds[l] + "_w_out"][l // 2].astype(BF) for l in range(4)]
    sh_local = _pad_rows(jnp.concatenate([wts[nm].reshape(-1, 128) for nm, _, _ in SH], axis=0), 16)
    me = 4 * lax.axis_index("x") + 2 * lax.axis_index("y") + lax.axis_index("c")
    own_slot = lambda blk: lax.dynamic_update_slice(lax.empty((N_DEV,) + blk.shape, blk.dtype), blk[None], (me, 0, 0))
    reg = {"blk_small": sh_local, "w_small": own_slot(sh_local)}
    sched = _Sched(reg)
    for l in range(4):
        reg[f"blk_in{l}"], reg[f"blk_out{l}"] = blk_in[l], blk_out[l]
        reg[f"w_in{l}"], reg[f"w_out{l}"] = own_slot(blk_in[l]), own_slot(blk_out[l])
    sched.add(_rows("blk_in0", "w_in0", "ag1", blk_in[0].shape[0], ROW_CHUNK[blk_in[0].shape[0]]))
    sched.add(_rows("blk_small", "w_small", "ag1", sh_local.shape[0], sh_local.shape[0]))
    for l in range(4):
        sched.add(_rows(f"blk_out{l}", f"w_out{l}", "ag1", D // N_DEV, ROW_CHUNK[D // N_DEV]))
        if l < 3:
            r = blk_in[l + 1].shape[0]
            sched.add(_rows(f"blk_in{l + 1}", f"w_in{l + 1}", "ag1", r, ROW_CHUNK[r]))

    def gathered(dst, blk):
        sched.flush(dst, FLUSH_EXTRA_US)
        return reg.pop(dst)

    wt_in0 = gathered("w_in0", blk_in[0]).reshape(-1, D)
    full = {nm: wts[nm] for nm in REP}

    def gather_small():
        sh_all = gathered("w_small", sh_local)
        off = 0
        for nm, shape, axis in SH:
            r = wts[nm].size // 128
            full[nm] = _sh_unpack(sh_all[:, off:off + r, :], shape, axis)
            off += r

    saved = []
    wt_in, w_out = [wt_in0, None, None, None], [None] * 4
    xf, xb = x0, x0.astype(BF)
    fwd = lambda name: FWD_OVERBOOK * CARRY_US[name]
    for layer in range(4):
        j = layer // 2
        kind = kinds[layer]
        if wt_in[layer] is None:
            wt_in[layer] = gathered(f"w_in{layer}", blk_in[layer]).reshape(-1, D)
        h = sched.run(_mm_nt, fwd("mm_h_" + kind), xb, wt_in[layer], 1024, 768 if kind == "even" else 512, "mm_h_" + kind)
        if kind == "even":
            bsb = jnp.broadcast_to(full["even_a_bs"][j][:, :, None], (8, 128, 128))
            mix3, o, l = sched.run(_even_fwd, fwd("even_fwd"), h, rope, full["even_a_ln_g"][j], full["even_a_ln_b"][j],
                                   full["even_a_ws"][j], bsb, full["even_b_sinks"][j], "even_fwd")
            extra = (o, l, bsb)
        else:
            if "odd_lam" not in full:
                gather_small()
            wa, wx = full["odd_w_a"][j].astype(BF), full["odd_w_x"][j].astype(BF)
            wp = full["odd_w_pool"][j].astype(BF)
            mix3, hst = sched.run(_odd_c_fwd, fwd("odd_c_fwd"), h, full["odd_conv_w"][j], full["odd_conv_b"][j], wa, wx,
                                  full["odd_b_a"][j], full["odd_b_x"][j], full["odd_lam"][j], "odd_c_fwd")
            mix3 = _odd_d_fwd(h, mix3, wp, full["odd_d_scale"][j], "odd_d_fwd")
            extra = (hst, wa, wx, wp)
        w_out[layer] = gathered(f"w_out{layer}", blk_out[layer]).reshape(D, D)
        z, xn, xnb = sched.run(_mm_out_ln, fwd("mm_out_ln"), mix3, w_out[layer], xf, full[kind + "_ln_g"][j],
                               full[kind + "_ln_b"][j], "mm_out_ln")
        saved.append((xb, h, mix3, z, extra))
        xf, xb = xn, xnb

    dxn, part = _loss_grad(xf, loss_target[0])
    loss = lax.psum(part[0, 0] * (0.5 / D), ("x", "y", "c"))

    gsum = {nm: [None, None] for nm in NAMES}

    chip_sums = {}
    sched.overhang = 0.15

    waiting = []

    def chip_sum(g, tag, key):
        r = g.shape[0] // N_DEV
        reg["g_" + key] = g.reshape(N_DEV, r, D)
        sched.add(_rows("g_" + key, "d_" + key, "rsd", r, r), first=True)
        waiting.append((key, tag))

    def add_arrived():
        for key, tag in list(waiting):
            if "d_" + key in reg and not sched.pending("d_" + key):
                waiting.remove((key, tag))
                g8 = reg.pop("g_" + key)
                chip_sums[key] = reg["s_" + key] = _add_pairs(g8, reg.pop("d_" + key), "rs_add_" + tag)
                sched.add(_rows("s_" + key, "r_" + key, "rs", g8.shape[1], ROW_CHUNK[g8.shape[1]] // 2))

    sched.after_landing = add_arrived

    def reduced(key, name):
        sched.flush("d_" + key, FLUSH_EXTRA_US)
        sched.flush("r_" + key, FLUSH_EXTRA_US)
        return _rs_final(chip_sums[key], reg.pop("r_" + key), name)

    for layer in (3, 2, 1, 0):
        j = layer // 2
        xb, h, mix3, z, extra = saved[layer]
        kind = kinds[layer]
        dz, dzb, dg, dbeta = sched.run(_ln_bwd, CARRY_US["ln_bwd"], dxn, z, full[kind + "_ln_g"][j], "ln_bwd")
        gsum[kind + "_ln_g"][j] = dg.reshape(D)
        gsum[kind + "_ln_b"][j] = dbeta.reshape(D)
        chip_sum(sched.run(_mm_tn, CARRY_US["mm_dw_out"], mix3, dzb, 512, "mm_dw_out"), "w_out", f"out{layer}")
        dmix3 = sched.run(_mm_nt, CARRY_US["mm_dmix"], dzb, w_out[layer], 1024, 512, "mm_dmix", out3=True)
        if kind == "even":
            o, l, bsb = extra
            ws = full["even_a_ws"][j]
            dh, dws, dbs, dlng, dlnb, dsink = sched.run(
                _even_bwd, CARRY_US["even_bwd"], h, dmix3, o, l, rope, full["even_a_ln_g"][j], full["even_a_ln_b"][j],
                ws, jnp.swapaxes(ws, 1, 2), bsb, full["even_b_sinks"][j], "even_bwd")
            gsum["even_a_ws"][j] = dws
            gsum["even_a_bs"][j] = jnp.transpose(dbs[:, :8])
            gsum["even_a_ln_g"][j] = dlng.reshape(W)
            gsum["even_a_ln_b"][j] = dlnb.reshape(W)
            gsum["even_b_sinks"][j] = dsink[0, :16]
            if layer == 0:
                rep_rows = [_rep_pack(jnp.stack(gsum[nm]).reshape(wts[nm].shape)) for nm in REP]
                sh_rows = [_sh_pack(jnp.stack(gsum[nm]).reshape(shape), axis) for nm, shape, axis in SH]
                packed = _pad_rows(jnp.concatenate(rep_rows + sh_rows, axis=1))
                gw, (small8,) = _mm_tn(dh, xb, 384, "mm_dw_in_even", comm=_ExchangeAll(packed))
            else:
                gw = sched.run(_mm_tn, CARRY_US["mm_dw_in_even"], dh, xb, 384, "mm_dw_in_even")
            chip_sum(gw, "w_in_even", f"in{layer}")
            if layer == 0:
                n_rep = sum(p.shape[1] for p in rep_rows)
                red = _sum8(small8, 1 << 20, "sum_small")
                (rep_all,) = sched.flush("d_in0", FLUSH_EXTRA_US, beside=_GatherAll(_pad_rows(red[:n_rep])))
                sched.overhang = 0.6
            dxn = sched.run(_mm_nn_res, CARRY_US["mm_dx_even"], dh, wt_in[layer], dz, 512, 512, "mm_dx_even")
        else:
            hst, wa, wx, wp = extra
            dh4, dcw, dcb, dwa, dwx, dba, dbx, dlam = sched.run(
                _odd_c_bwd, CARRY_US["odd_c_bwd"], h, hst, dmix3, full["odd_conv_w"][j], full["odd_conv_b"][j], wa, wx,
                jnp.swapaxes(wa, 1, 2), jnp.swapaxes(wx, 1, 2), full["odd_b_a"][j], full["odd_b_x"][j], full["odd_lam"][j],
                "odd_c_bwd")
            dh4, dwp, dds = _odd_d_bwd(h, dmix3, dh4, wp, jnp.swapaxes(wp, 1, 2), full["odd_d_scale"][j], "odd_d_bwd")
            gsum["odd_conv_w"][j], gsum["odd_conv_b"][j] = dcw, dcb.reshape(W)
            gsum["odd_w_a"][j], gsum["odd_w_x"][j] = dwa, dwx
            gsum["odd_b_a"][j], gsum["odd_b_x"][j], gsum["odd_lam"][j] = dba.reshape(W), dbx.reshape(W), dlam.reshape(W)
            gsum["odd_w_pool"][j], gsum["odd_d_scale"][j] = dwp, dds.reshape(W)
            chip_sum(sched.run(_mm_tn, CARRY_US["mm_dw_in_odd"], dh4, xb, 512, "mm_dw_in_odd"), "w_in_odd", f"in{layer}")
            dxn = sched.run(_mm_nn_res, CARRY_US["mm_dx_odd"], dh4, wt_in[layer], dz, 512, 512, "mm_dx_odd")
    grad_x = dxn[None]

    out_g, out_d, out_m, out_v = {}, {}, {}, {}
    for nm, kind, what, layers in (("odd_w_out", "odd", "out", (1, 3)), ("even_w_out", "even", "out", (0, 2)),
                                   ("odd_w_in", "odd", "in", (1, 3)), ("even_w_in", "even", "in", (0, 2))):
        gl = [reduced(f"{what}{l}", f"rs_final_w_{what}_{kind}") for l in layers]
        if nm == "even_w_in":
            view = lambda a: jnp.transpose(a, (0, 2, 1))
            res, _ = _adamw(view(wts[nm]), gl, view(mom[nm]), view(var[nm]), 168, f"adamw_{nm}")
            res = [view(a) for a in res]
        elif what == "in":
            res, _ = _adamw(wts[nm], [jnp.transpose(a) for a in gl], mom[nm], var[nm], 512, f"adamw_{nm}")
        else:
            res = sched.run(_adamw, CARRY_US["adamw_" + nm], wts[nm], gl, mom[nm], var[nm], 128, f"adamw_{nm}")
        out_d[nm], out_m[nm], out_v[nm], out_g[nm] = res

    g_small = {}
    off = 0
    for nm, p in zip(REP, rep_rows):
        r = p.shape[1]
        g_small[nm] = _rep_unpack(rep_all[:, off:off + r, :], wts[nm].shape)
        off += r
    off = n_rep
    for (nm, shape, axis), p in zip(SH, sh_rows):
        r = p.shape[1]
        g_small[nm] = red[off:off + r].reshape(wts[nm].shape)
        off += r

    def rows(a):
        f = a.reshape(-1)
        pad = (-f.shape[0]) % 128
        if pad:
            f = jnp.concatenate([f, jnp.zeros((pad,), a.dtype)])
        return f.reshape(-1, 128)

    small = REP + [nm for nm, _, _ in SH]
    each = lambda src: [rows(src[nm]) for nm in small]
    d2, m2, v2 = _adamw_many(each(wts), each(g_small), each(mom), each(var), "adamw_small")
    for i, nm in enumerate(small):
        n, shp = wts[nm].size, wts[nm].shape
        take = lambda a: a.reshape(-1)[:n].reshape(shp)
        out_g[nm], out_d[nm], out_m[nm], out_v[nm] = g_small[nm], take(d2[i]), take(m2[i]), take(v2[i])

    return (loss, grad_x, *[out_g[nm] for nm in NAMES], *[out_d[nm] for nm in NAMES],
            *[out_m[nm] for nm in NAMES], *[out_v[nm] for nm in NAMES])
```

```python
import functools

import jax
import jax.numpy as jnp
from jax import lax
from jax.experimental import pallas as pl
from jax.experimental.pallas import tpu as pltpu

F32 = jnp.float32
BF = jnp.bfloat16
MESH = pl.DeviceIdType.MESH
ANY = pl.BlockSpec(memory_space=pl.ANY)

N_DEV = 8
D = 2048
W = 1024
EVEN_IN = 5376
ODD_IN = 4096
CHUNK = 128
ALPHA = (2 * 4) ** 0.25
LN_EPS = 1e-5
ROPE_THETA = 500000.0
LRU_C = 8.0
LR, B1, B2, ADAM_EPS, WD, STEP = 0.001, 0.9, 0.999, 1e-08, 0.01, 10
NEG = -1e30
HEAD_COLS = 4


def _cp(vmem_mb=48, collective_id=None):
    return pltpu.CompilerParams(vmem_limit_bytes=vmem_mb * 1024 * 1024, collective_id=collective_id)


def _sig(x):
    return jax.nn.sigmoid(x)


def _silu_grad(x):
    s = _sig(x)
    return x * s, s * (1.0 + x * (1.0 - s))


def _dot(a, b):
    return jnp.dot(a, b, preferred_element_type=F32)


def _dot_nt(a, b):
    return lax.dot_general(a, b, (((1,), (1,)), ((), ())), preferred_element_type=F32)


def _dot_tn(a, b):
    return lax.dot_general(a, b, (((0,), (0,)), ((), ())), preferred_element_type=F32)


def _coords():
    return lax.axis_index("x"), lax.axis_index("y"), lax.axis_index("c")


def _chip(j):
    x, y, _ = _coords()
    return (1 - x if j & 2 else x), (1 - y if j & 1 else y)


X_NB, Y_NB, DIAG, SIB = 4, 2, 6, 1
EVERYONE = frozenset(range(1, N_DEV))
BARRIER_IDS = {}


class _Comm:
    def collective_id(self):
        return BARRIER_IDS.setdefault(frozenset(self.peers), len(BARRIER_IDS))

    def start(self, ins, outs, sems):
        barrier = pltpu.get_barrier_semaphore()
        for d in sorted(self.peers):
            pl.semaphore_signal(barrier, inc=1, device_id=_peer(d)[0], device_id_type=MESH)
        pl.semaphore_wait(barrier, len(self.peers))
        for cp in self.copies(ins, outs, sems):
            cp.start()

    def wait(self, ins, outs, sems):
        for cp in self.copies(ins, outs, sems):
            cp.wait()


class _Join(_Comm):
    def __init__(self, parts):
        self.parts = list(parts)
        self.peers = frozenset().union(*[p.peers for p in self.parts])
        self.inputs = [a for p in self.parts for a in p.inputs]
        self.out_shapes = [s for p in self.parts for s in p.out_shapes]
        self.sem_shapes = [s for p in self.parts for s in p.sem_shapes]
        self.aliases = {}
        i0 = o0 = 0
        for p in self.parts:
            for i, o in getattr(p, "aliases", {}).items():
                self.aliases[i0 + i] = o0 + o
            i0, o0 = i0 + len(p.inputs), o0 + len(p.out_shapes)

    def copies(self, ins, outs, sems):
        res = []
        i0 = o0 = s0 = 0
        for p in self.parts:
            ni, no, ns = len(p.inputs), len(p.out_shapes), len(p.sem_shapes)
            res += p.copies(ins[i0:i0 + ni], outs[o0:o0 + no], sems[s0:s0 + ns])
            i0, o0, s0 = i0 + ni, o0 + no, s0 + ns
        return res


ROWS_US = {"ag1": 0.104, "ag2": 0.052, "agd": 0.027, "rsd": 0.027, "rs": 0.205}
N_COPIES = {"ag1": 2, "ag2": 2, "agd": 4, "rsd": 4, "rs": 3}
TASK_PEERS = {"ag1": {X_NB, Y_NB}, "ag2": {X_NB, Y_NB}, "agd": {SIB}, "rsd": {SIB}, "rs": {X_NB, Y_NB, DIAG}}
ROW_CHUNK = {672: 224, 512: 128, 256: 128}
CARRY_US = {"mm_h_even": 58, "mm_h_odd": 47, "even_fwd": 42, "odd_c_fwd": 37, "mm_out_ln": 33, "ln_bwd": 23, "mm_dmix": 26,
            "mm_dw_out": 25, "even_bwd": 95, "odd_c_bwd": 70, "mm_dw_in_even": 56, "mm_dw_in_odd": 44, "mm_dx_even": 60,
            "mm_dx_odd": 50, "adamw_even_w_in": 30, "adamw_odd_w_in": 28, "adamw_even_w_out": 11, "adamw_odd_w_out": 11}
FWD_OVERBOOK = 1.15
FLUSH_EXTRA_US = 60.0


def _cost_us(task, reg):
    kind, src, _, lo, hi = task
    return ROWS_US[kind] * (hi - lo) * reg[src].shape[-1] * reg[src].dtype.itemsize / 4096.0


class _Copies(_Comm):
    def __init__(self, tasks, reg):
        self.tasks = list(tasks)
        self.out_names, self.in_names = [], []
        for kind, src, dst, lo, hi in self.tasks:
            if dst not in self.out_names:
                self.out_names.append(dst)
        for kind, src, dst, lo, hi in self.tasks:
            if src not in self.out_names and src not in self.in_names:
                self.in_names.append(src)
        self.out_shapes, self.aliases = [], {}
        for o, dst in enumerate(self.out_names):
            if dst in reg:
                self.aliases[len(self.in_names)] = o
                self.in_names.append(dst)
                self.out_shapes.append(jax.ShapeDtypeStruct(reg[dst].shape, reg[dst].dtype))
            else:
                kind, src = next((t[0], t[1]) for t in self.tasks if t[2] == dst)
                shape = ({"rsd": 4, "rs": 3}[kind],) + reg[src].shape[1:]
                self.out_shapes.append(jax.ShapeDtypeStruct(shape, reg[src].dtype))
        self.inputs = [reg[nm] for nm in self.in_names]
        n = sum(N_COPIES[t[0]] for t in self.tasks)
        self.sem_shapes = [pltpu.SemaphoreType.DMA((n,)), pltpu.SemaphoreType.DMA((n,))]
        self.peers = frozenset().union(*[TASK_PEERS[t[0]] for t in self.tasks])

    def copies(self, ins, outs, sems):
        send, recv = sems
        x, y, c = _coords()
        me = 4 * x + 2 * y + c
        xn, yn = (1 - x, y, c), (x, 1 - y, c)
        at_xn, at_yn = 4 * (1 - x) + 2 * y + c, 4 * x + 2 * (1 - y) + c
        ref = dict(zip(self.in_names, ins))
        ref.update(zip(self.out_names, outs))
        res = []

        def copy(src, dst, to):
            i = len(res)
            res.append(pltpu.make_async_remote_copy(src_ref=src, dst_ref=dst, send_sem=send.at[i], recv_sem=recv.at[i],
                                                    device_id=to, device_id_type=MESH))

        for kind, src, dst, lo, hi in self.tasks:
            n = hi - lo
            if kind == "ag1":
                for to in (xn, yn):
                    copy(ref[src].at[pl.ds(lo, n)], ref[dst].at[me, pl.ds(lo, n)], to)
            elif kind == "ag2":
                h = n // 2
                first, second = ref[dst].at[at_xn, pl.ds(lo, h)], ref[dst].at[at_yn, pl.ds(lo + h, n - h)]
                copy(first, first, yn)
                copy(second, second, xn)
            elif kind == "agd":
                for j in range(4):
                    px, py = _chip(j)
                    rows = ref[dst].at[4 * px + 2 * py + c, pl.ds(lo, n)]
                    copy(rows, rows, (x, y, 1 - c))
            elif kind == "rsd":
                for j in range(4):
                    px, py = _chip(j)
                    copy(ref[src].at[4 * px + 2 * py + 1 - c, pl.ds(lo, n)], ref[dst].at[j, pl.ds(lo, n)], (x, y, 1 - c))
            else:
                for j in (1, 2, 3):
                    px, py = _chip(j)
                    copy(ref[src].at[j, pl.ds(lo, n)], ref[dst].at[j - 1, pl.ds(lo, n)], (px, py, c))
        return res


class _Sched:
    def __init__(self, reg):
        self.reg, self.queue, self.later = reg, [], []
        self.overhang = 0.5
        self.after_landing = None

    def add(self, tasks, first=False):
        self.queue = list(tasks) + self.queue if first else self.queue + list(tasks)

    def pending(self, dst):
        return any(t[2] == dst for t in self.queue + self.later)

    def take(self, budget_us, must=None, overhang=0.5):
        self.queue, self.later = self.later + self.queue, []
        picked, us = [], 0.0
        rest = []
        for t in self.queue:
            cost = _cost_us(t, self.reg)
            if (must is not None and t[2] == must) or us + (1.0 - overhang) * cost <= budget_us:
                picked.append(t)
                us += cost
                if t[0] in ("ag1", "ag2"):
                    self.later.append(({"ag1": "ag2", "ag2": "agd"}[t[0]], t[2], t[2], t[3], t[4]))
            else:
                rest.append(t)
        self.queue = rest
        return _Copies(picked, self.reg) if picked else None

    def landed(self, comm, got):
        if comm is not None:
            for nm, a in zip(comm.out_names, got):
                self.reg[nm] = a
        if self.after_landing is not None:
            self.after_landing()

    def run(self, builder, budget_us, *args, **kw):
        comm = self.take(budget_us, overhang=self.overhang)
        res, got = builder(*args, comm=comm, **kw)
        self.landed(comm, got)
        return res

    def flush(self, dst, budget_us=0.0, beside=None):
        res = []
        while self.pending(dst):
            comm = self.take(budget_us, must=dst)
            got = _comm_only(comm if beside is None else _Join([comm, beside]), "flush_" + dst)
            res, beside = got[len(comm.out_shapes):], None
            self.landed(comm, got[:len(comm.out_shapes)])
        return res


def _rows(name_src, name_dst, kind, n_rows, chunk):
    return [(kind, name_src, name_dst, lo, min(lo + chunk, n_rows)) for lo in range(0, n_rows, chunk)]


def _pcall(body, *, grid, in_specs, out_specs, out_shape, name, scratch=(), vmem=48, comm=None):
    in_specs, out_specs, out_shape, scratch = list(in_specs), list(out_specs), list(out_shape), list(scratch)
    if comm is None:
        call = pl.pallas_call(body, grid=grid, in_specs=in_specs, out_specs=out_specs, out_shape=out_shape,
                              scratch_shapes=scratch, name=name, compiler_params=_cp(vmem))
        return lambda *args: (call(*args), [])
    n_in, n_out, n_scr = len(in_specs), len(out_specs), len(scratch)
    c_in, c_out = len(comm.inputs), len(comm.out_shapes)
    aliases = {n_in + i: n_out + o for i, o in getattr(comm, "aliases", {}).items()}

    def wrapped(*refs):
        ins, cins = refs[:n_in], refs[n_in:n_in + c_in]
        o0 = n_in + c_in
        outs, couts = refs[o0:o0 + n_out], refs[o0 + n_out:o0 + n_out + c_out]
        s0 = o0 + n_out + c_out
        scr, sems = refs[s0:s0 + n_scr], refs[s0 + n_scr:]
        ids = [pl.program_id(a) for a in range(len(grid))]
        first = functools.reduce(jnp.logical_and, [i == 0 for i in ids])
        last = functools.reduce(jnp.logical_and, [i == g - 1 for i, g in zip(ids, grid)])

        @pl.when(first)
        def _():
            comm.start(cins, couts, sems)

        body(*ins, *outs, *scr)

        @pl.when(last)
        def _():
            comm.wait(cins, couts, sems)

    call = pl.pallas_call(wrapped, grid=grid, in_specs=in_specs + [ANY] * c_in, out_specs=out_specs + [ANY] * c_out,
                          out_shape=out_shape + list(comm.out_shapes), scratch_shapes=scratch + list(comm.sem_shapes),
                          input_output_aliases=aliases, name=name, compiler_params=_cp(vmem, comm.collective_id()))

    def run(*args):
        res = call(*args, *comm.inputs)
        return res[:n_out], res[n_out:]

    return run


def _comm_only(comm, name):
    c_in, c_out = len(comm.inputs), len(comm.out_shapes)

    def body(*refs):
        cins, couts, sems = refs[:c_in], refs[c_in:c_in + c_out], refs[c_in + c_out:]
        comm.start(cins, couts, sems)
        comm.wait(cins, couts, sems)

    return pl.pallas_call(body, in_specs=[ANY] * c_in, out_specs=[ANY] * c_out, out_shape=list(comm.out_shapes),
                          scratch_shapes=list(comm.sem_shapes), input_output_aliases=dict(getattr(comm, "aliases", {})),
                          name=name, compiler_params=pltpu.CompilerParams(collective_id=comm.collective_id()))(*comm.inputs)


def _chip_blocks():
    _, _, c = _coords()
    return jnp.stack([4 * px + 2 * py + c for px, py in map(_chip, range(4))]).astype(jnp.int32)


def _add_pairs(g8, b4, name):
    _, R, C = b4.shape

    def body(idx_ref, a_ref, b_ref, o_ref):
        o_ref[...] = (a_ref[...].astype(F32) + b_ref[...].astype(F32)).astype(BF)

    blk = pl.BlockSpec((None, R, C), lambda j, idx: (j, 0, 0))
    grid_spec = pltpu.PrefetchScalarGridSpec(
        num_scalar_prefetch=1, grid=(4,),
        in_specs=[pl.BlockSpec((None, R, C), lambda j, idx: (idx[j], 0, 0)), blk], out_specs=blk)
    return pl.pallas_call(body, grid_spec=grid_spec, out_shape=jax.ShapeDtypeStruct(b4.shape, BF), name=name,
                          compiler_params=_cp())(_chip_blocks(), g8, b4)


def _rs_final(s4, r3, name):
    _, R, C = s4.shape
    tr = R // 2

    def body(s_ref, r_ref, o_ref):
        o_ref[...] = ((s_ref[...].astype(F32) + r_ref[0].astype(F32)) + r_ref[1].astype(F32)) + r_ref[2].astype(F32)

    return pl.pallas_call(
        body, grid=(2,),
        in_specs=[pl.BlockSpec((None, tr, C), lambda i: (0, i, 0)), pl.BlockSpec((3, tr, C), lambda i: (0, i, 0))],
        out_specs=pl.BlockSpec((tr, C), lambda i: (i, 0)), out_shape=jax.ShapeDtypeStruct((R, C), F32),
        name=name, compiler_params=_cp())(s4, r3)


def _mm_nt(a, w, tm, tn, name, out3=False, comm=None):
    M, K = a.shape
    N = w.shape[0]
    tm = min(tm, M)

    def body(a_ref, w_ref, o_ref):
        o_ref[...] = _dot_nt(a_ref[...], w_ref[...])

    if out3:
        per = W // tn
        out_shape = jax.ShapeDtypeStruct((N // W, M, W), F32)
        out_spec = pl.BlockSpec((None, tm, tn), lambda i, j: (j // per, i, j % per))
    else:
        out_shape = jax.ShapeDtypeStruct((M, N), F32)
        out_spec = pl.BlockSpec((tm, tn), lambda i, j: (i, j))
    (res,), extra = _pcall(
        body, grid=(M // tm, N // tn),
        in_specs=[pl.BlockSpec((tm, K), lambda i, j: (i, 0)), pl.BlockSpec((tn, K), lambda i, j: (j, 0))],
        out_specs=[out_spec], out_shape=[out_shape], name=name, comm=comm)(a, w)
    return res, extra


def _mm_tn(a, b, tm, name, comm=None):
    K, N = b.shape
    if a.ndim == 3:
        M = a.shape[0] * W
        per = W // tm
        a_spec = pl.BlockSpec((None, K, tm), lambda i: (i // per, 0, i % per))
    else:
        M = a.shape[1]
        a_spec = pl.BlockSpec((K, tm), lambda i: (0, i))

    def body(a_ref, b_ref, o_ref):
        o_ref[...] = _dot_tn(a_ref[...], b_ref[...]).astype(BF)

    (out,), extra = _pcall(
        body, grid=(M // tm,),
        in_specs=[a_spec, pl.BlockSpec((K, N), lambda i: (0, 0))],
        out_specs=[pl.BlockSpec((tm, N), lambda i: (i, 0))],
        out_shape=[jax.ShapeDtypeStruct((M, N), BF)], name=name, vmem=56, comm=comm)(a, b)
    return out, extra


def _mm_nn_res(a, w, res, tm, tn, name, comm=None):
    K, N = w.shape
    if a.ndim == 3:
        P, M = a.shape[0], a.shape[1]
        tm = min(tm, M)
        a_spec = pl.BlockSpec((P, tm, W), lambda j, i: (0, i, 0))
    else:
        P, M = 0, a.shape[0]
        tm = min(tm, M)
        a_spec = pl.BlockSpec((tm, K), lambda j, i: (i, 0))

    def body(a_ref, w_ref, r_ref, o_ref):
        if P:
            d = _dot(a_ref[0], w_ref[0:W, :])
            for p in range(1, P):
                d = d + _dot(a_ref[p], w_ref[p * W:(p + 1) * W, :])
        else:
            d = _dot(a_ref[...], w_ref[...])
        o_ref[...] = ALPHA * r_ref[...] + d

    (out,), extra = _pcall(
        body, grid=(N // tn, M // tm),
        in_specs=[a_spec, pl.BlockSpec((K, tn), lambda j, i: (0, j)), pl.BlockSpec((tm, tn), lambda j, i: (i, j))],
        out_specs=[pl.BlockSpec((tm, tn), lambda j, i: (i, j))],
        out_shape=[jax.ShapeDtypeStruct((M, N), F32)], name=name, comm=comm)(a, w, res)
    return out, extra


def _mm_out_ln(mix3, w_out, x, g, b, name, comm=None):
    S = x.shape[0]
    tm = min(256, S)

    def body(m_ref, w_ref, x_ref, g_ref, b_ref, z_ref, xn_ref, xb_ref):
        acc = _dot(m_ref[0], w_ref[0:W, :]) + _dot(m_ref[1], w_ref[W:2 * W, :])
        z = ALPHA * x_ref[...] + acc
        mu = jnp.mean(z, axis=1, keepdims=True)
        zc = z - mu
        var = jnp.mean(zc * zc, axis=1, keepdims=True)
        xn = zc * lax.rsqrt(var + LN_EPS) * g_ref[...] + b_ref[...]
        z_ref[...] = z
        xn_ref[...] = xn
        xb_ref[...] = xn.astype(BF)

    row = pl.BlockSpec((tm, D), lambda i: (i, 0))
    vec = pl.BlockSpec((1, D), lambda i: (0, 0))
    return _pcall(
        body, grid=(S // tm,),
        in_specs=[pl.BlockSpec((2, tm, W), lambda i: (0, i, 0)), pl.BlockSpec((D, D), lambda i: (0, 0)), row, vec, vec],
        out_specs=[row, row, row],
        out_shape=[jax.ShapeDtypeStruct((S, D), F32), jax.ShapeDtypeStruct((S, D), F32), jax.ShapeDtypeStruct((S, D), BF)],
        name=name, comm=comm)(mix3, w_out, x, g.reshape(1, D), b.reshape(1, D))


def _ln_bwd(dxn, z, g, name, comm=None, target=None):
    S = z.shape[0]
    tm = min(256, S)
    head = target is not None

    def body(*refs):
        if head:
            d_ref, t_ref, z_ref, g_ref, dz_ref, dzb_ref, dg_ref, db_ref, p_ref = refs
        else:
            d_ref, z_ref, g_ref, dz_ref, dzb_ref, dg_ref, db_ref = refs
        i = pl.program_id(0)
        zz = z_ref[...]
        mu = jnp.mean(zz, axis=1, keepdims=True)
        zc = zz - mu
        var = jnp.mean(zc * zc, axis=1, keepdims=True)
        rstd = lax.rsqrt(var + LN_EPS)
        xhat = zc * rstd
        dy = d_ref[...]
        if head:
            e = dy - t_ref[...]
            dy = e * (1.0 / D)

            @pl.when(i == 0)
            def _():
                p_ref[...] = jnp.zeros_like(p_ref)

            p_ref[...] += jnp.sum(jnp.sum(e * e, axis=1, keepdims=True), axis=0, keepdims=True)
        dyg = dy * g_ref[...]
        m1 = jnp.mean(dyg, axis=1, keepdims=True)
        m2 = jnp.mean(dyg * xhat, axis=1, keepdims=True)
        dz = rstd * (dyg - m1 - xhat * m2)
        dz_ref[...] = dz
        dzb_ref[...] = dz.astype(BF)

        @pl.when(i == 0)
        def _():
            dg_ref[...] = jnp.zeros_like(dg_ref)
            db_ref[...] = jnp.zeros_like(db_ref)

        dg_ref[...] += jnp.sum(dy * xhat, axis=0, keepdims=True)
        db_ref[...] += jnp.sum(dy, axis=0, keepdims=True)

    row = pl.BlockSpec((tm, D), lambda i: (i, 0))
    vec = pl.BlockSpec((1, D), lambda i: (0, 0))
    out_specs = [row, row, vec, vec] + ([pl.BlockSpec((8, 128), lambda i: (0, 0))] if head else [])
    out_shape = [jax.ShapeDtypeStruct((S, D), F32), jax.ShapeDtypeStruct((S, D), BF), jax.ShapeDtypeStruct((1, D), F32),
                 jax.ShapeDtypeStruct((1, D), F32)] + ([jax.ShapeDtypeStruct((8, 128), F32)] if head else [])
    operands = (dxn, target, z, g.reshape(1, D)) if head else (dxn, z, g.reshape(1, D))
    return _pcall(body, grid=(S // tm,), in_specs=[row] * (len(operands) - 1) + [vec], out_specs=out_specs,
                  out_shape=out_shape, name=name, comm=comm)(*operands)


def _rope_fwd(t, r_ref):
    return (t * r_ref[:, 0:128] + pltpu.roll(t, 120, 1) * r_ref[:, 128:256]
            + pltpu.roll(t, 8, 1) * r_ref[:, 256:384])


def _rope_bwd(g, r_ref):
    return (g * r_ref[:, 0:128] + pltpu.roll(g * r_ref[:, 128:256], 8, 1)
            + pltpu.roll(g * r_ref[:, 256:384], 120, 1))


def _dup_heads(kb):
    lo = lax.broadcasted_iota(jnp.int32, kb.shape, 1) < 64
    sw = pltpu.roll(kb, 64, 1)
    return [jnp.where(lo, kb, sw).astype(BF), jnp.where(lo, sw, kb).astype(BF)]


def _even_fwd(h, rope, lng, lnb, ws, bsb, sinks, name, comm=None):
    S = h.shape[0]
    nb = S // CHUNK

    def body(h_ref, hp_ref, rc_ref, rp_ref, lng_ref, lnb_ref, ws_ref, bsb_ref, sink_ref, mix_ref, o_ref, l_ref):
        n = pl.program_id(0)
        lane = lax.broadcasted_iota(jnp.int32, (128, 128), 1)
        rowi = lax.broadcasted_iota(jnp.int32, (128, 128), 0)
        tri = rowi >= lane
        lane_lo = lane < 64
        v = h_ref[:, W:2 * W]
        mu = jnp.mean(v, axis=1, keepdims=True)
        vc = v - mu
        var = jnp.mean(vc * vc, axis=1, keepdims=True)
        vn = vc * lax.rsqrt(var + LN_EPS) * lng_ref[...] + lnb_ref[...]
        ms = [_dot(jnp.where(tri, ws_ref[g], 0.0).astype(BF), vn[:, g * 128:(g + 1) * 128].astype(BF)) for g in range(8)]
        for g in range(8):
            sl = slice(g * 128, (g + 1) * 128)
            ag = h_ref[:, 2 * W + g * 128:2 * W + (g + 1) * 128]
            mix_ref[0, :, sl] = (h_ref[:, sl] * (ms[g] + bsb_ref[g]) * (ag * _sig(ag))).astype(BF)
        kb = jnp.concatenate([_rope_fwd(hp_ref[:, 0:128], rp_ref), _rope_fwd(h_ref[:, 4096:4224], rc_ref)], axis=0)
        vb = jnp.concatenate([hp_ref[:, 128:256], h_ref[:, 4224:4352]], axis=0)
        k2 = _dup_heads(kb)
        v2 = _dup_heads(vb)
        qi = lax.broadcasted_iota(jnp.int32, (128, 256), 0)
        kj = lax.broadcasted_iota(jnp.int32, (128, 256), 1)
        diff = qi + 128 - kj
        valid = (diff >= 0) & (diff < 128) & ((n > 0) | (kj >= 128))
        lacc = jnp.zeros((128, 128), F32)
        for j0 in range(0, 8, HEAD_COLS):
            heads = [(j, half) for j in range(j0, j0 + HEAD_COLS) for half in range(2)]
            sc, pr, oh = {}, {}, {}
            for j in range(j0, j0 + HEAD_COLS):
                qc = _rope_fwd(h_ref[:, 3072 + j * 128:3072 + (j + 1) * 128], rc_ref)
                sc[j, 0] = _dot_nt(jnp.where(lane_lo, qc, 0.0).astype(BF), k2[j // 4])
                sc[j, 1] = _dot_nt(jnp.where(lane_lo, 0.0, qc).astype(BF), k2[j // 4])
            for j, half in heads:
                hq = 2 * j + half
                s = jnp.where(valid, sc[j, half] * 0.125, NEG)
                sk = sink_ref[hq]
                mx = jnp.maximum(jnp.max(s, axis=1, keepdims=True), sk)
                p = jnp.exp(s - mx)
                den = jnp.sum(p, axis=1, keepdims=True) + jnp.exp(sk - mx)
                pr[j, half] = (p / den).astype(BF)
                lacc = jnp.where(lane == hq, mx + jnp.log(den), lacc)
            for j, half in heads:
                oh[j, half] = _dot(pr[j, half], v2[j // 4])
            for j in range(j0, j0 + HEAD_COLS):
                cs = slice(j * 128, (j + 1) * 128)
                ocol = jnp.where(lane_lo, oh[j, 0], oh[j, 1])
                bg = h_ref[:, 4352 + j * 128:4352 + (j + 1) * 128]
                o_ref[:, cs] = ocol
                mix_ref[1, :, cs] = (ocol * (bg * _sig(bg))).astype(BF)
        l_ref[...] = lacc

    prev = lambda n: jnp.maximum(n - 1, 0)
    full = lambda shape: pl.BlockSpec(shape, lambda n: (0,) * len(shape))
    return _pcall(
        body, grid=(nb,),
        in_specs=[pl.BlockSpec((CHUNK, EVEN_IN), lambda n: (n, 0)),
                  pl.BlockSpec((CHUNK, 256), lambda n: (prev(n), 16)),
                  pl.BlockSpec((CHUNK, 384), lambda n: (n, 0)),
                  pl.BlockSpec((CHUNK, 384), lambda n: (prev(n), 0)),
                  full((1, W)), full((1, W)), full((8, 128, 128)), full((8, 128, 128)),
                  pl.BlockSpec(memory_space=pltpu.SMEM)],
        out_specs=[pl.BlockSpec((2, CHUNK, W), lambda n: (0, n, 0)),
                   pl.BlockSpec((CHUNK, W), lambda n: (n, 0)),
                   pl.BlockSpec((CHUNK, 128), lambda n: (n, 0))],
        out_shape=[jax.ShapeDtypeStruct((2, S, W), BF), jax.ShapeDtypeStruct((S, W), F32),
                   jax.ShapeDtypeStruct((S, 128), F32)],
        name=name, comm=comm)(h, h, rope, rope, lng.reshape(1, W), lnb.reshape(1, W), ws, bsb, sinks)


def _even_bwd(h, dmix3, o, l, rope, lng, lnb, ws, wst, bsb, sinks, name, comm=None):
    S = h.shape[0]
    nb = S // CHUNK

    def body(h_ref, hp_ref, hn_ref, dm_ref, dmn_ref, o_ref, on_ref, l_ref, ln_ref, rc_ref, rp_ref, rn_ref,
             lng_ref, lnb_ref, ws_ref, wst_ref, bsb_ref, sink_ref,
             dh_ref, dws_ref, dbs_ref, dlng_ref, dlnb_ref, dsink_ref, dvn_ref):
        n = pl.program_id(0)

        @pl.when(n == 0)
        def _():
            dws_ref[...] = jnp.zeros_like(dws_ref)
            dbs_ref[...] = jnp.zeros_like(dbs_ref)
            dlng_ref[...] = jnp.zeros_like(dlng_ref)
            dlnb_ref[...] = jnp.zeros_like(dlnb_ref)
            dsink_ref[...] = jnp.zeros_like(dsink_ref)

        lane = lax.broadcasted_iota(jnp.int32, (128, 128), 1)
        rowi = lax.broadcasted_iota(jnp.int32, (128, 128), 0)
        lane1 = lax.broadcasted_iota(jnp.int32, (1, 128), 1)
        tri = rowi >= lane
        tri_t = lane >= rowi
        lane_lo = lane < 64
        v = h_ref[:, W:2 * W]
        mu = jnp.mean(v, axis=1, keepdims=True)
        vc = v - mu
        var = jnp.mean(vc * vc, axis=1, keepdims=True)
        rstd = lax.rsqrt(var + LN_EPS)
        vhat = vc * rstd
        vn = vhat * lng_ref[...] + lnb_ref[...]
        dbs_acc = jnp.zeros((128, 128), F32)
        vng = [vn[:, g * 128:(g + 1) * 128].astype(BF) for g in range(8)]
        ms = [_dot(jnp.where(tri, ws_ref[g], 0.0).astype(BF), vng[g]) for g in range(8)]
        dmb = []
        for g in range(8):
            sl = slice(g * 128, (g + 1) * 128)
            m = ms[g] + bsb_ref[g]
            ag = h_ref[:, 2 * W + g * 128:2 * W + (g + 1) * 128]
            sg, dsg = _silu_grad(ag)
            u = h_ref[:, sl]
            da = dm_ref[0, :, sl]
            dmm = da * u * sg
            dh_ref[:, sl] = (da * m * sg).astype(BF)
            dh_ref[:, 2 * W + g * 128:2 * W + (g + 1) * 128] = (da * u * m * dsg).astype(BF)
            dmb.append(dmm.astype(BF))
            dbs_acc = jnp.where(lane == g, jnp.sum(dmm, axis=1, keepdims=True), dbs_acc)
        dvs = [_dot(jnp.where(tri_t, wst_ref[g], 0.0).astype(BF), dmb[g]) for g in range(8)]
        dwss = [_dot_nt(dmb[g], vng[g]) for g in range(8)]
        for g in range(8):
            dvn_ref[:, g * 128:(g + 1) * 128] = dvs[g]
            dws_ref[g] += jnp.where(tri, dwss[g], 0.0)
        dbs_ref[...] += dbs_acc
        dvn = dvn_ref[...]
        dlng_ref[...] += jnp.sum(dvn * vhat, axis=0, keepdims=True)
        dlnb_ref[...] += jnp.sum(dvn, axis=0, keepdims=True)
        dyg = dvn * lng_ref[...]
        m1 = jnp.mean(dyg, axis=1, keepdims=True)
        m2 = jnp.mean(dyg * vhat, axis=1, keepdims=True)
        dh_ref[:, W:2 * W] = (rstd * (dyg - m1 - vhat * m2)).astype(BF)
        kcur = _rope_fwd(h_ref[:, 4096:4224], rc_ref)
        kb = jnp.concatenate([_rope_fwd(hp_ref[:, 0:128], rp_ref), kcur], axis=0)
        vb = jnp.concatenate([hp_ref[:, 128:256], h_ref[:, 4224:4352]], axis=0)
        k2 = _dup_heads(kb)
        v2 = _dup_heads(vb)
        kc2 = _dup_heads(kcur)
        vc2 = _dup_heads(h_ref[:, 4224:4352])
        qi = lax.broadcasted_iota(jnp.int32, (128, 256), 0)
        kj = lax.broadcasted_iota(jnp.int32, (128, 256), 1)
        diff = qi + 128 - kj
        valid = (diff >= 0) & (diff < 128) & ((n > 0) | (kj >= 128))
        validn = (lane > rowi) & (n < nb - 1)
        lc = l_ref[...]
        lnx = ln_ref[...]
        dk = [jnp.zeros((128, 128), F32), jnp.zeros((128, 128), F32)]
        dv = [jnp.zeros((128, 128), F32), jnp.zeros((128, 128), F32)]
        dsk_acc = jnp.zeros((1, 128), F32)
        for j0 in range(0, 8, HEAD_COLS):
            heads = [(j, half) for j in range(j0, j0 + HEAD_COLS) for half in range(2)]
            t = {}
            for j in range(j0, j0 + HEAD_COLS):
                cs = slice(j * 128, (j + 1) * 128)
                qc = _rope_fwd(h_ref[:, 3072 + j * 128:3072 + (j + 1) * 128], rc_ref)
                qn = _rope_fwd(hn_ref[:, 3072 + j * 128:3072 + (j + 1) * 128], rn_ref)
                bg = h_ref[:, 4352 + j * 128:4352 + (j + 1) * 128]
                sgb, dsgb = _silu_grad(bg)
                db = dm_ref[1, :, cs]
                oc = o_ref[:, cs]
                do = db * sgb
                dh_ref[:, 4352 + j * 128:4352 + (j + 1) * 128] = (db * oc * dsgb).astype(BF)
                bgn = hn_ref[:, 4352 + j * 128:4352 + (j + 1) * 128]
                don = dmn_ref[1, :, cs] * (bgn * _sig(bgn))
                prod = do * oc
                prodn = don * on_ref[:, cs]
                for half in range(2):
                    hq = 2 * j + half
                    hm = lane_lo if half == 0 else jnp.logical_not(lane_lo)
                    t[j, half] = dict(
                        dsum=jnp.sum(jnp.where(hm, prod, 0.0), axis=1, keepdims=True),
                        dsumn=jnp.sum(jnp.where(hm, prodn, 0.0), axis=1, keepdims=True),
                        lh=jnp.sum(jnp.where(lane == hq, lc, 0.0), axis=1, keepdims=True),
                        lhn=jnp.sum(jnp.where(lane == hq, lnx, 0.0), axis=1, keepdims=True),
                        qm=jnp.where(hm, qc, 0.0).astype(BF), dom=jnp.where(hm, do, 0.0).astype(BF),
                        qnm=jnp.where(hm, qn, 0.0).astype(BF), donm=jnp.where(hm, don, 0.0).astype(BF))
            for j, half in heads:
                e, hk = t[j, half], j // 4
                e["s"], e["dp"] = _dot_nt(e["qm"], k2[hk]), _dot_nt(e["dom"], v2[hk])
                e["sn"], e["dpn"] = _dot_nt(e["qnm"], kc2[hk]), _dot_nt(e["donm"], vc2[hk])
            for j, half in heads:
                e, hq = t[j, half], 2 * j + half
                p = jnp.exp(jnp.where(valid, e["s"] * 0.125 - e["lh"], NEG))
                ds = p * (e["dp"] - e["dsum"])
                pn = jnp.exp(jnp.where(validn, e["sn"] * 0.125 - e["lhn"], NEG))
                dsn = pn * (e["dpn"] - e["dsumn"])
                psink = jnp.exp(sink_ref[hq] - e["lh"])
                dsk_acc = jnp.where(lane1 == hq, -jnp.sum(psink * e["dsum"], axis=0, keepdims=True), dsk_acc)
                e["ds"] = ds.astype(BF)
                e["pt"], e["dst"] = jnp.transpose(p[:, 128:256]).astype(BF), jnp.transpose(ds[:, 128:256]).astype(BF)
                e["pnt"], e["dsnt"] = jnp.transpose(pn).astype(BF), jnp.transpose(dsn).astype(BF)
            for j, half in heads:
                e, hk = t[j, half], j // 4
                e["dq"] = _dot(e["ds"], k2[hk])
                e["dv"] = _dot(e["pt"], e["dom"]) + _dot(e["pnt"], e["donm"])
                e["dk"] = _dot(e["dst"], e["qm"]) + _dot(e["dsnt"], e["qnm"])
            for j in range(j0, j0 + HEAD_COLS):
                hk = j // 4
                dqcol = jnp.where(lane_lo, t[j, 0]["dq"], t[j, 1]["dq"]) * 0.125
                dh_ref[:, 3072 + j * 128:3072 + (j + 1) * 128] = _rope_bwd(dqcol, rc_ref).astype(BF)
                dv[hk] = dv[hk] + t[j, 0]["dv"] + t[j, 1]["dv"]
                dk[hk] = dk[hk] + (t[j, 0]["dk"] + t[j, 1]["dk"]) * 0.125
        fold = lambda a: a + pltpu.roll(a, 64, 1)
        dh_ref[:, 4096:4224] = _rope_bwd(jnp.where(lane_lo, fold(dk[0]), fold(dk[1])), rc_ref).astype(BF)
        dh_ref[:, 4224:4352] = jnp.where(lane_lo, fold(dv[0]), fold(dv[1])).astype(BF)
        dsink_ref[...] += dsk_acc

    prev = lambda n: jnp.maximum(n - 1, 0)
    nxt = lambda n: jnp.minimum(n + 1, nb - 1)
    full = lambda shape: pl.BlockSpec(shape, lambda n: (0,) * len(shape))
    return _pcall(
        body, grid=(nb,),
        in_specs=[pl.BlockSpec((CHUNK, EVEN_IN), lambda n: (n, 0)),
                  pl.BlockSpec((CHUNK, 256), lambda n: (prev(n), 16)),
                  pl.BlockSpec((CHUNK, EVEN_IN), lambda n: (nxt(n), 0)),
                  pl.BlockSpec((2, CHUNK, W), lambda n: (0, n, 0)),
                  pl.BlockSpec((2, CHUNK, W), lambda n: (0, nxt(n), 0)),
                  pl.BlockSpec((CHUNK, W), lambda n: (n, 0)),
                  pl.BlockSpec((CHUNK, W), lambda n: (nxt(n), 0)),
                  pl.BlockSpec((CHUNK, 128), lambda n: (n, 0)),
                  pl.BlockSpec((CHUNK, 128), lambda n: (nxt(n), 0)),
                  pl.BlockSpec((CHUNK, 384), lambda n: (n, 0)),
                  pl.BlockSpec((CHUNK, 384), lambda n: (prev(n), 0)),
                  pl.BlockSpec((CHUNK, 384), lambda n: (nxt(n), 0)),
                  full((1, W)), full((1, W)), full((8, 128, 128)), full((8, 128, 128)), full((8, 128, 128)),
                  pl.BlockSpec(memory_space=pltpu.SMEM)],
        out_specs=[pl.BlockSpec((CHUNK, EVEN_IN), lambda n: (n, 0)),
                   full((8, 128, 128)), full((128, 128)), full((1, W)), full((1, W)), full((1, 128))],
        out_shape=[jax.ShapeDtypeStruct((S, EVEN_IN), BF), jax.ShapeDtypeStruct((8, 128, 128), F32),
                   jax.ShapeDtypeStruct((128, 128), F32), jax.ShapeDtypeStruct((1, W), F32),
                   jax.ShapeDtypeStruct((1, W), F32), jax.ShapeDtypeStruct((1, 128), F32)],
        scratch=[pltpu.VMEM((CHUNK, W), F32)], name=name, comm=comm,
    )(h, h, h, dmix3, dmix3, o, o, l, l, rope, rope, rope, lng.reshape(1, W), lnb.reshape(1, W), ws, wst, bsb, sinks)


def _expm1(x):
    ser = x * (1.0 + x * (0.5 + x * (1.0 / 6.0 + x * (1.0 / 24.0))))
    return jnp.where(jnp.abs(x) < 1e-2, ser, jnp.exp(x) - 1.0)


def _softplus_neg(lam):
    z = -lam
    e = jnp.exp(-jnp.abs(z))
    l1p = jnp.where(e < 1e-3, e * (1.0 - e * (0.5 - e * (1.0 / 3.0))), jnp.log(1.0 + e))
    return jnp.maximum(z, 0.0) + l1p


def _shift_down(x, k, row, fill=0.0):
    return jnp.where(row >= k, pltpu.roll(x, k, 0), fill)


def _shift_up(x, k, row, fill=0.0):
    S = x.shape[0]
    return jnp.where(row < S - k, pltpu.roll(x, S - k, 0), fill)


def _lru_gates(xc, row, cw_ref, cb_ref, wa_ref, wx_ref, ba_ref, bx_ref, lam_ref):
    xconv = (cw_ref[3:4, :] * xc + cw_ref[2:3, :] * _shift_down(xc, 1, row) + cw_ref[1:2, :] * _shift_down(xc, 2, row)
             + cw_ref[0:1, :] * _shift_down(xc, 3, row) + cb_ref[...])
    xb = xconv.astype(BF)
    r = _sig(_dot(xb, wa_ref[...]) + ba_ref[...])
    i = _sig(_dot(xb, wx_ref[...]) + bx_ref[...])
    sp = _softplus_neg(lam_ref[...])
    log_a = -LRU_C * r * sp
    a = jnp.exp(log_a)
    mult = jnp.sqrt(-_expm1(2.0 * log_a))
    return xconv, r, i, sp, a, mult


def _odd_c_fwd(h, cw, cb, wa, wx, ba, bx, lam, name, comm=None):
    S = h.shape[0]

    def body(xc_ref, cg_ref, cw_ref, cb_ref, wa_ref, wx_ref, ba_ref, bx_ref, lam_ref, mix_ref, hst_ref):
        row = lax.broadcasted_iota(jnp.int32, (S, 128), 0)
        xconv, r, i, sp, a, mult = _lru_gates(xc_ref[...], row, cw_ref, cb_ref, wa_ref, wx_ref, ba_ref, bx_ref, lam_ref)
        aa = a
        bb = mult * (i * xconv)
        k = 1
        while k < S:
            bb = aa * _shift_down(bb, k, row) + bb
            if 2 * k < S:
                aa = aa * _shift_down(aa, k, row, 1.0)
            k *= 2
        hst_ref[...] = bb
        cg = cg_ref[...]
        mix_ref[...] = (bb * (cg * _sig(cg))).astype(BF)

    col = lambda off: pl.BlockSpec((S, 128), lambda j: (0, off + j))
    vec = pl.BlockSpec((1, 128), lambda j: (0, j))
    mat = pl.BlockSpec((None, 128, 128), lambda j: (j, 0, 0))
    return _pcall(
        body, grid=(8,),
        in_specs=[col(0), col(8), pl.BlockSpec((4, 128), lambda j: (0, j)), vec, mat, mat, vec, vec, vec],
        out_specs=[pl.BlockSpec((None, S, 128), lambda j: (0, 0, j)), pl.BlockSpec((S, 128), lambda j: (0, j))],
        out_shape=[jax.ShapeDtypeStruct((2, S, W), BF), jax.ShapeDtypeStruct((S, W), F32)],
        name=name, comm=comm,
    )(h, h, cw, cb.reshape(1, W), wa, wx, ba.reshape(1, W), bx.reshape(1, W), lam.reshape(1, W))


def _pool_sums(x, g, row, shift):
    s2 = x + shift(x, 1, row)
    s4 = s2 + shift(s2, 2, row)
    s8 = s4 + shift(s4, 4, row)
    s16 = s8 + shift(s8, 8, row)
    return jnp.where(g == 0, s2, jnp.where(g == 1, s4, jnp.where(g == 2, s8, s16)))


def _odd_d_fwd(h, mix3, wp, dscale, name):
    S = h.shape[0]

    def body(xd_ref, dg_ref, wp_ref, ds_ref, mix_in, mix_ref):
        g = pl.program_id(0)
        row = lax.broadcasted_iota(jnp.int32, (S, 256), 0)
        xd = xd_ref[...]
        cnt = jnp.minimum(row + 1, jnp.left_shift(2, g)).astype(F32)
        pooled = _pool_sums(xd, g, row, _shift_down) / cnt - xd
        mixed = _dot(pooled.astype(BF), wp_ref[...])
        dg = dg_ref[...]
        mix_ref[...] = (mixed * ds_ref[...] * (dg * _sig(dg))).astype(BF)

    col = lambda off: pl.BlockSpec((S, 256), lambda g: (0, off + g))
    return pl.pallas_call(
        body, grid=(4,),
        in_specs=[col(8), col(12), pl.BlockSpec((None, 256, 256), lambda g: (g, 0, 0)),
                  pl.BlockSpec((1, 256), lambda g: (0, g)), ANY],
        out_specs=pl.BlockSpec((None, S, 256), lambda g: (1, 0, g)),
        out_shape=jax.ShapeDtypeStruct((2, S, W), BF), input_output_aliases={4: 0},
        name=name, compiler_params=_cp(),
    )(h, h, wp, dscale.reshape(1, W), mix3)


def _odd_c_bwd(h, hst, dmix3, cw, cb, wa, wx, wat, wxt, ba, bx, lam, name, comm=None):
    S = h.shape[0]

    def body(xc_ref, cg_ref, hst_ref, dc_ref, cw_ref, cb_ref, wa_ref, wx_ref, wat_ref, wxt_ref, ba_ref, bx_ref, lam_ref,
             dh_ref, dcw_ref, dcb_ref, dwa_ref, dwx_ref, dba_ref, dbx_ref, dlam_ref):
        row = lax.broadcasted_iota(jnp.int32, (S, 128), 0)
        xc = xc_ref[...]
        xconv, r, i, sp, a, mult = _lru_gates(xc, row, cw_ref, cb_ref, wa_ref, wx_ref, ba_ref, bx_ref, lam_ref)
        hst = hst_ref[...]
        cg = cg_ref[...]
        sg, dsg = _silu_grad(cg)
        dc = dc_ref[...]
        dh_ref[1] = (dc * hst * dsg).astype(BF)
        aa = _shift_up(a, 1, row)
        bb = dc * sg
        k = 1
        while k < S:
            bb = aa * _shift_up(bb, k, row) + bb
            if 2 * k < S:
                aa = aa * _shift_up(aa, k, row, 1.0)
            k *= 2
        lam_t = bb
        da = lam_t * _shift_down(hst, 1, row)
        ix = i * xconv
        dmult = lam_t * ix
        di = lam_t * mult * xconv
        dxconv = lam_t * mult * i
        dlog_a = da * a - dmult * (a * a / mult)
        dr = dlog_a * (-LRU_C * sp)
        dsp = jnp.sum(dlog_a * (-LRU_C * r), axis=0, keepdims=True)
        dlam_ref[...] = dsp * (-_sig(-lam_ref[...]))
        dpa = dr * r * (1.0 - r)
        dpx = di * i * (1.0 - i)
        dpab = dpa.astype(BF)
        dpxb = dpx.astype(BF)
        xb = xconv.astype(BF)
        dxconv = dxconv + _dot(dpab, wat_ref[...]) + _dot(dpxb, wxt_ref[...])
        dwa_ref[...] = _dot_tn(xb, dpab)
        dwx_ref[...] = _dot_tn(xb, dpxb)
        dba_ref[...] = jnp.sum(dpa, axis=0, keepdims=True)
        dbx_ref[...] = jnp.sum(dpx, axis=0, keepdims=True)
        dh_ref[0] = (cw_ref[3:4, :] * dxconv + cw_ref[2:3, :] * _shift_up(dxconv, 1, row)
                     + cw_ref[1:2, :] * _shift_up(dxconv, 2, row) + cw_ref[0:1, :] * _shift_up(dxconv, 3, row)).astype(BF)
        for j in range(4):
            src = xc if j == 3 else _shift_down(xc, 3 - j, row)
            dcw_ref[j:j + 1, :] = jnp.sum(dxconv * src, axis=0, keepdims=True)
        dcb_ref[...] = jnp.sum(dxconv, axis=0, keepdims=True)

    col = lambda off: pl.BlockSpec((S, 128), lambda j: (0, off + j))
    vec = pl.BlockSpec((1, 128), lambda j: (0, j))
    mat = pl.BlockSpec((None, 128, 128), lambda j: (j, 0, 0))
    vshape = jax.ShapeDtypeStruct((1, W), F32)
    mshape = jax.ShapeDtypeStruct((8, 128, 128), F32)
    return _pcall(
        body, grid=(8,),
        in_specs=[col(0), col(8), col(0), pl.BlockSpec((None, S, 128), lambda j: (0, 0, j)),
                  pl.BlockSpec((4, 128), lambda j: (0, j)), vec, mat, mat, mat, mat, vec, vec, vec],
        out_specs=[pl.BlockSpec((2, S, 128), lambda j: (0, 0, j)), pl.BlockSpec((4, 128), lambda j: (0, j)), vec,
                   mat, mat, vec, vec, vec],
        out_shape=[jax.ShapeDtypeStruct((4, S, W), BF), jax.ShapeDtypeStruct((4, W), F32), vshape, mshape, mshape,
                   vshape, vshape, vshape],
        name=name, vmem=56, comm=comm,
    )(h, h, hst, dmix3, cw, cb.reshape(1, W), wa, wx, wat, wxt, ba.reshape(1, W), bx.reshape(1, W), lam.reshape(1, W))


def _odd_d_bwd(h, dmix3, dh4, wp, wpt, dscale, name):
    S = h.shape[0]

    def body(xd_ref, dg_ref, dd_ref, wp_ref, wpt_ref, ds_ref, dh_in, dh_ref, dwp_ref, dds_ref):
        g = pl.program_id(0)
        row = lax.broadcasted_iota(jnp.int32, (S, 256), 0)
        xd = xd_ref[...]
        cnt = jnp.minimum(row + 1, jnp.left_shift(2, g)).astype(F32)
        pooled = _pool_sums(xd, g, row, _shift_down) / cnt - xd
        pb = pooled.astype(BF)
        mixed = _dot(pb, wp_ref[...])
        dg = dg_ref[...]
        sg, dsg = _silu_grad(dg)
        dd = dd_ref[...]
        dmixed = dd * ds_ref[...] * sg
        dds_ref[...] = jnp.sum(dd * mixed * sg, axis=0, keepdims=True)
        dh_ref[1] = (dd * mixed * ds_ref[...] * dsg).astype(BF)
        dmb = dmixed.astype(BF)
        dpooled = _dot(dmb, wpt_ref[...])
        dwp_ref[...] = _dot_tn(pb, dmb)
        dh_ref[0] = (_pool_sums(dpooled / cnt, g, row, _shift_up) - dpooled).astype(BF)

    col = lambda off: pl.BlockSpec((S, 256), lambda g: (0, off + g))
    mat = pl.BlockSpec((None, 256, 256), lambda g: (g, 0, 0))
    vec = pl.BlockSpec((1, 256), lambda g: (0, g))
    return pl.pallas_call(
        body, grid=(4,),
        in_specs=[col(8), col(12), pl.BlockSpec((None, S, 256), lambda g: (1, 0, g)), mat, mat, vec, ANY],
        out_specs=[pl.BlockSpec((2, S, 256), lambda g: (1, 0, g)), mat, vec],
        out_shape=[jax.ShapeDtypeStruct((4, S, W), BF), jax.ShapeDtypeStruct((4, 256, 256), F32),
                   jax.ShapeDtypeStruct((1, W), F32)],
        input_output_aliases={6: 0}, name=name, compiler_params=_cp(56),
    )(h, h, dmix3, wp, wpt, dscale.reshape(1, W), dh4)


def _peer(d):
    x, y, c = lax.axis_index("x"), lax.axis_index("y"), lax.axis_index("c")
    px = 1 - x if d & 4 else x
    py = 1 - y if d & 2 else y
    pc = 1 - c if d & 1 else c
    return (px, py, pc), 4 * px + 2 * py + pc


class _GatherAll(_Comm):
    def __init__(self, xs):
        self.peers = EVERYONE
        self.inputs = [xs]
        self.out_shapes = [jax.ShapeDtypeStruct((N_DEV,) + xs.shape, xs.dtype)]
        self.sem_shapes = [pltpu.SemaphoreType.DMA((N_DEV - 1,)), pltpu.SemaphoreType.DMA((N_DEV - 1,)),
                           pltpu.SemaphoreType.DMA]

    def copies(self, ins, outs, sems):
        (x_ref,), (out_ref,), (send, recv, loc) = ins, outs, sems
        _, me = _peer(0)
        res = [pltpu.make_async_copy(x_ref, out_ref.at[me], loc)]
        for d in range(1, N_DEV):
            peer, _ = _peer(d)
            res.append(pltpu.make_async_remote_copy(src_ref=x_ref, dst_ref=out_ref.at[me], send_sem=send.at[d - 1],
                                                    recv_sem=recv.at[d - 1], device_id=peer, device_id_type=MESH))
        return res


class _ExchangeAll(_Comm):
    def __init__(self, g8):
        self.peers = EVERYONE
        self.inputs = [g8]
        self.out_shapes = [jax.ShapeDtypeStruct(g8.shape, g8.dtype)]
        self.sem_shapes = [pltpu.SemaphoreType.DMA((N_DEV - 1,)), pltpu.SemaphoreType.DMA((N_DEV - 1,)),
                           pltpu.SemaphoreType.DMA]

    def copies(self, ins, outs, sems):
        (g_ref,), (out_ref,), (send, recv, loc) = ins, outs, sems
        _, me = _peer(0)
        res = [pltpu.make_async_copy(g_ref.at[me], out_ref.at[0], loc)]
        for d in range(1, N_DEV):
            peer, pidx = _peer(d)
            res.append(pltpu.make_async_remote_copy(src_ref=g_ref.at[pidx], dst_ref=out_ref.at[d], send_sem=send.at[d - 1],
                                                    recv_sem=recv.at[d - 1], device_id=peer, device_id_type=MESH))
        return res


def _sum8(r8, tr, name):
    _, R, C = r8.shape
    tr = min(tr, R)
    assert R % tr == 0

    def body(r_ref, o_ref):
        acc = r_ref[0]
        for d in range(1, N_DEV):
            acc = acc + r_ref[d]
        o_ref[...] = acc

    return pl.pallas_call(
        body, grid=(R // tr,), in_specs=[pl.BlockSpec((N_DEV, tr, C), lambda i: (0, i, 0))],
        out_specs=pl.BlockSpec((tr, C), lambda i: (i, 0)), out_shape=jax.ShapeDtypeStruct((R, C), F32),
        name=name, compiler_params=_cp(),
    )(r8)


def _adamw_math(w, g, m, v):
    m2 = B1 * m + (1.0 - B1) * g
    v2 = B2 * v + (1.0 - B2) * (g * g)
    m_hat = m2 / (1.0 - B1 ** STEP)
    v_hat = v2 / (1.0 - B2 ** STEP)
    return -LR * (m_hat / (jnp.sqrt(v_hat) + ADAM_EPS) + WD * w), m2, v2


def _adamw_many(ws, gs, ms, vs, name):
    n = len(ws)

    def body(*refs):
        for i in range(n):
            d, m2, v2 = _adamw_math(refs[i][...], refs[n + i][...], refs[2 * n + i][...], refs[3 * n + i][...])
            refs[4 * n + i][...] = d
            refs[5 * n + i][...] = m2
            refs[6 * n + i][...] = v2

    vmem = pl.BlockSpec(memory_space=pltpu.VMEM)
    shapes = [jax.ShapeDtypeStruct(w.shape, F32) for w in ws]
    res = pl.pallas_call(body, in_specs=[vmem] * (4 * n), out_specs=[vmem] * (3 * n), out_shape=shapes * 3, name=name,
                         compiler_params=_cp())(*ws, *gs, *ms, *vs)
    return res[:n], res[n:2 * n], res[2 * n:]


def _adamw(w3, gs, m3, v3, tr, name, comm=None):
    _, R, C = w3.shape

    def body(w_ref, g0_ref, g1_ref, m_ref, v_ref, d_ref, m2_ref, v2_ref, g_ref):
        g = jnp.where(pl.program_id(0) == 0, g0_ref[...], g1_ref[...])
        d_ref[...], m2_ref[...], v2_ref[...] = _adamw_math(w_ref[...], g, m_ref[...], v_ref[...])
        g_ref[...] = g

    blk = pl.BlockSpec((None, tr, C), lambda j, i: (j, i, 0))
    grad = lambda layer: pl.BlockSpec((tr, C), lambda j, i: (jnp.where(j == layer, i, 0), 0))
    shp = jax.ShapeDtypeStruct((2, R, C), F32)
    return _pcall(body, grid=(2, R // tr), in_specs=[blk, grad(0), grad(1), blk, blk], out_specs=[blk] * 4,
                  out_shape=[shp] * 4, name=name, comm=comm)(w3, gs[0], gs[1], m3, v3)


def _rep_pack(a):
    n = a.size
    pad = (-n) % 1024
    f = a.reshape(-1)
    if pad:
        f = jnp.concatenate([f, jnp.zeros((pad,), a.dtype)])
    return f.reshape(N_DEV, -1, 128)


def _rep_unpack(p, shape):
    n = 1
    for s in shape:
        n *= s
    return p.reshape(-1)[:n].reshape(shape)


def _sh_pack(a, axis):
    shp = a.shape
    a = a.reshape(shp[:axis] + (N_DEV, shp[axis] // N_DEV) + shp[axis + 1:])
    return jnp.moveaxis(a, axis, 0).reshape(N_DEV, -1, 128)


def _sh_unpack(p, shape, axis):
    a = p.reshape((N_DEV,) + shape[:axis] + (shape[axis] // N_DEV,) + shape[axis + 1:])
    return jnp.moveaxis(a, 0, axis).reshape(shape)


def _pad_rows(a, mult=8):
    pad = (-a.shape[-2]) % mult
    if pad:
        a = jnp.concatenate([a, jnp.zeros(a.shape[:-2] + (pad, a.shape[-1]), a.dtype)], axis=-2)
    return a


REP = ["even_a_ln_g", "even_a_ln_b", "even_a_ws", "even_a_bs", "even_b_sinks", "even_ln_g", "even_ln_b",
       "odd_w_a", "odd_w_x"]
SH = [("odd_conv_w", (2, 4, W), 2), ("odd_conv_b", (2, W), 1), ("odd_b_a", (2, W), 1), ("odd_b_x", (2, W), 1),
      ("odd_lam", (2, W), 1), ("odd_w_pool", (2, 4, 256, 256), 2), ("odd_d_scale", (2, W), 1),
      ("odd_ln_g", (2, D), 1), ("odd_ln_b", (2, D), 1)]
BIG = ["even_w_in", "even_w_out", "odd_w_in", "odd_w_out"]
NAMES = ["even_w_in", "even_a_ln_g", "even_a_ln_b", "even_a_ws", "even_a_bs", "even_b_sinks", "even_w_out",
         "even_ln_g", "even_ln_b", "odd_w_in", "odd_conv_w", "odd_conv_b", "odd_w_a", "odd_b_a", "odd_w_x", "odd_b_x",
         "odd_lam", "odd_w_pool", "odd_d_scale", "odd_w_out", "odd_ln_g", "odd_ln_b"]


def _rope_table(positions):
    inv = ROPE_THETA ** (-jnp.arange(0, 16, 2, dtype=F32) / 16)
    f = jnp.arange(128) % 64
    ang = positions.astype(F32)[:, None] * inv[f % 8][None, :]
    cos, sin = jnp.cos(ang), jnp.sin(ang)
    return jnp.concatenate([jnp.where(f < 16, cos, 1.0), jnp.where(f < 8, -sin, 0.0),
                            jnp.where((f >= 8) & (f < 16), sin, 0.0)], axis=1)


def kernel(x, positions, even_w_in, even_a_ln_g, even_a_ln_b, even_a_ws, even_a_bs, even_b_sinks, even_w_out, even_ln_g, even_ln_b, odd_w_in, odd_conv_w, odd_conv_b, odd_w_a, odd_b_a, odd_w_x, odd_b_x, odd_lam, odd_w_pool, odd_d_scale, odd_w_out, odd_ln_g, odd_ln_b, loss_target, m_even_w_in, m_even_a_ln_g, m_even_a_ln_b, m_even_a_ws, m_even_a_bs, m_even_b_sinks, m_even_w_out, m_even_ln_g, m_even_ln_b, m_odd_w_in, m_odd_conv_w, m_odd_conv_b, m_odd_w_a, m_odd_b_a, m_odd_w_x, m_odd_b_x, m_odd_lam, m_odd_w_pool, m_odd_d_scale, m_odd_w_out, m_odd_ln_g, m_odd_ln_b, v_even_w_in, v_even_a_ln_g, v_even_a_ln_b, v_even_a_ws, v_even_a_bs, v_even_b_sinks, v_even_w_out, v_even_ln_g, v_even_ln_b, v_odd_w_in, v_odd_conv_w, v_odd_conv_b, v_odd_w_a, v_odd_b_a, v_odd_w_x, v_odd_b_x, v_odd_lam, v_odd_w_pool, v_odd_d_scale, v_odd_w_out, v_odd_ln_g, v_odd_ln_b):
    args = (even_w_in, even_a_ln_g, even_a_ln_b, even_a_ws, even_a_bs, even_b_sinks, even_w_out, even_ln_g, even_ln_b,
            odd_w_in, odd_conv_w, odd_conv_b, odd_w_a, odd_b_a, odd_w_x, odd_b_x, odd_lam, odd_w_pool, odd_d_scale,
            odd_w_out, odd_ln_g, odd_ln_b)
    margs = (m_even_w_in, m_even_a_ln_g, m_even_a_ln_b, m_even_a_ws, m_even_a_bs, m_even_b_sinks, m_even_w_out,
             m_even_ln_g, m_even_ln_b, m_odd_w_in, m_odd_conv_w, m_odd_conv_b, m_odd_w_a, m_odd_b_a, m_odd_w_x,
             m_odd_b_x, m_odd_lam, m_odd_w_pool, m_odd_d_scale, m_odd_w_out, m_odd_ln_g, m_odd_ln_b)
    vargs = (v_even_w_in, v_even_a_ln_g, v_even_a_ln_b, v_even_a_ws, v_even_a_bs, v_even_b_sinks, v_even_w_out,
             v_even_ln_g, v_even_ln_b, v_odd_w_in, v_odd_conv_w, v_odd_conv_b, v_odd_w_a, v_odd_b_a, v_odd_w_x,
             v_odd_b_x, v_odd_lam, v_odd_w_pool, v_odd_d_scale, v_odd_w_out, v_odd_ln_g, v_odd_ln_b)
    wts = dict(zip(NAMES, args))
    mom = dict(zip(NAMES, margs))
    var = dict(zip(NAMES, vargs))
    S = x.shape[1]
    x0 = x[0]
    rope = _rope_table(positions[0])

    kinds = ("even", "odd", "even", "odd")
    blk_in = [jnp.transpose(wts[kinds[l] + "_w_in"][l // 2]).astype(BF) for l in range(4)]
    blk_out = [wts[kinds[l] + "_w_out"][l // 2].astype(BF) for l in range(4)]
    sh_local = _pad_rows(jnp.concatenate([wts[nm].reshape(-1, 128) for nm, _, _ in SH], axis=0), 16)
    me = 4 * lax.axis_index("x") + 2 * lax.axis_index("y") + lax.axis_index("c")
    own_slot = lambda blk: lax.dynamic_update_slice(lax.empty((N_DEV,) + blk.shape, blk.dtype), blk[None], (me, 0, 0))
    reg = {"blk_small": sh_local, "w_small": own_slot(sh_local)}
    sched = _Sched(reg)
    for l in range(4):
        reg[f"blk_in{l}"], reg[f"blk_out{l}"] = blk_in[l], blk_out[l]
        reg[f"w_in{l}"], reg[f"w_out{l}"] = own_slot(blk_in[l]), own_slot(blk_out[l])
    sched.add(_rows("blk_in0", "w_in0", "ag1", blk_in[0].shape[0], ROW_CHUNK[blk_in[0].shape[0]]))
    sched.add(_rows("blk_small", "w_small", "ag1", sh_local.shape[0], sh_local.shape[0]))
    for l in range(4):
        sched.add(_rows(f"blk_out{l}", f"w_out{l}", "ag1", D // N_DEV, ROW_CHUNK[D // N_DEV]))
        if l < 3:
            r = blk_in[l + 1].shape[0]
            sched.add(_rows(f"blk_in{l + 1}", f"w_in{l + 1}", "ag1", r, ROW_CHUNK[r]))

    def gathered(dst, blk):
        sched.flush(dst, FLUSH_EXTRA_US)
        return reg.pop(dst)

    wt_in0 = gathered("w_in0", blk_in[0]).reshape(-1, D)
    full = {nm: wts[nm] for nm in REP}

    def gather_small():
        sh_all = gathered("w_small", sh_local)
        off = 0
        for nm, shape, axis in SH:
            r = wts[nm].size // 128
            full[nm] = _sh_unpack(sh_all[:, off:off + r, :], shape, axis)
            off += r

    saved = []
    wt_in, w_out = [wt_in0, None, None, None], [None] * 4
    xf, xb = x0, x0.astype(BF)
    fwd = lambda name: FWD_OVERBOOK * CARRY_US[name]
    for layer in range(4):
        j = layer // 2
        kind = kinds[layer]
        if wt_in[layer] is None:
            wt_in[layer] = gathered(f"w_in{layer}", blk_in[layer]).reshape(-1, D)
        h = sched.run(_mm_nt, fwd("mm_h_" + kind), xb, wt_in[layer], 1024, 768 if kind == "even" else 512, "mm_h_" + kind)
        if kind == "even":
            bsb = jnp.broadcast_to(full["even_a_bs"][j][:, :, None], (8, 128, 128))
            mix3, o, l = sched.run(_even_fwd, fwd("even_fwd"), h, rope, full["even_a_ln_g"][j], full["even_a_ln_b"][j],
                                   full["even_a_ws"][j], bsb, full["even_b_sinks"][j], "even_fwd")
            extra = (o, l, bsb)
        else:
            if "odd_lam" not in full:
                gather_small()
            wa, wx = full["odd_w_a"][j].astype(BF), full["odd_w_x"][j].astype(BF)
            wp = full["odd_w_pool"][j].astype(BF)
            mix3, hst = sched.run(_odd_c_fwd, fwd("odd_c_fwd"), h, full["odd_conv_w"][j], full["odd_conv_b"][j], wa, wx,
                                  full["odd_b_a"][j], full["odd_b_x"][j], full["odd_lam"][j], "odd_c_fwd")
            mix3 = _odd_d_fwd(h, mix3, wp, full["odd_d_scale"][j], "odd_d_fwd")
            extra = (hst, wa, wx, wp)
        w_out[layer] = gathered(f"w_out{layer}", blk_out[layer]).reshape(D, D)
        z, xn, xnb = sched.run(_mm_out_ln, fwd("mm_out_ln"), mix3, w_out[layer], xf, full[kind + "_ln_g"][j],
                               full[kind + "_ln_b"][j], "mm_out_ln")
        saved.append((xb, h, mix3, z, extra))
        xf, xb = xn, xnb

    dxn = xf

    gsum = {nm: [None, None] for nm in NAMES}

    chip_sums = {}
    sched.overhang = 0.15

    waiting = []

    def chip_sum(g, tag, key):
        r = g.shape[0] // N_DEV
        reg["g_" + key] = g.reshape(N_DEV, r, D)
        sched.add(_rows("g_" + key, "d_" + key, "rsd", r, r), first=True)
        waiting.append((key, tag))

    def add_arrived():
        for key, tag in list(waiting):
            if "d_" + key in reg and not sched.pending("d_" + key):
                waiting.remove((key, tag))
                g8 = reg.pop("g_" + key)
                chip_sums[key] = reg["s_" + key] = _add_pairs(g8, reg.pop("d_" + key), "rs_add_" + tag)
                sched.add(_rows("s_" + key, "r_" + key, "rs", g8.shape[1], ROW_CHUNK[g8.shape[1]] // 2))

    sched.after_landing = add_arrived

    def reduced(key, name):
        sched.flush("d_" + key, FLUSH_EXTRA_US)
        sched.flush("r_" + key, FLUSH_EXTRA_US)
        return _rs_final(chip_sums[key], reg.pop("r_" + key), name)

    for layer in (3, 2, 1, 0):
        j = layer // 2
        xb, h, mix3, z, extra = saved[layer]
        kind = kinds[layer]
        if layer == 3:
            dz, dzb, dg, dbeta, part = sched.run(_ln_bwd, CARRY_US["ln_bwd"], dxn, z, full[kind + "_ln_g"][j], "loss_ln_bwd",
                                                 target=loss_target[0])
            loss = lax.psum(part[0, 0] * (0.5 / D), ("x", "y", "c"))
        else:
            dz, dzb, dg, dbeta = sched.run(_ln_bwd, CARRY_US["ln_bwd"], dxn, z, full[kind + "_ln_g"][j], "ln_bwd")
        gsum[kind + "_ln_g"][j] = dg.reshape(D)
        gsum[kind + "_ln_b"][j] = dbeta.reshape(D)
        chip_sum(sched.run(_mm_tn, CARRY_US["mm_dw_out"], mix3, dzb, 512, "mm_dw_out"), "w_out", f"out{layer}")
        dmix3 = sched.run(_mm_nt, CARRY_US["mm_dmix"], dzb, w_out[layer], 1024, 512, "mm_dmix", out3=True)
        if kind == "even":
            o, l, bsb = extra
            ws = full["even_a_ws"][j]
            dh, dws, dbs, dlng, dlnb, dsink = sched.run(
                _even_bwd, CARRY_US["even_bwd"], h, dmix3, o, l, rope, full["even_a_ln_g"][j], full["even_a_ln_b"][j],
                ws, jnp.swapaxes(ws, 1, 2), bsb, full["even_b_sinks"][j], "even_bwd")
            gsum["even_a_ws"][j] = dws
            gsum["even_a_bs"][j] = jnp.transpose(dbs[:, :8])
            gsum["even_a_ln_g"][j] = dlng.reshape(W)
            gsum["even_a_ln_b"][j] = dlnb.reshape(W)
            gsum["even_b_sinks"][j] = dsink[0, :16]
            if layer == 0:
                rep_rows = [_rep_pack(jnp.stack(gsum[nm]).reshape(wts[nm].shape)) for nm in REP]
                sh_rows = [_sh_pack(jnp.stack(gsum[nm]).reshape(shape), axis) for nm, shape, axis in SH]
                packed = _pad_rows(jnp.concatenate(rep_rows + sh_rows, axis=1))
                gw, (small8,) = _mm_tn(dh, xb, 384, "mm_dw_in_even", comm=_ExchangeAll(packed))
            else:
                gw = sched.run(_mm_tn, CARRY_US["mm_dw_in_even"], dh, xb, 384, "mm_dw_in_even")
            chip_sum(gw, "w_in_even", f"in{layer}")
            if layer == 0:
                n_rep = sum(p.shape[1] for p in rep_rows)
                red = _sum8(small8, 1 << 20, "sum_small")
                (rep_all,) = sched.flush("d_in0", FLUSH_EXTRA_US, beside=_GatherAll(_pad_rows(red[:n_rep])))
                sched.overhang = 0.6
            dxn = sched.run(_mm_nn_res, CARRY_US["mm_dx_even"], dh, wt_in[layer], dz, 512, 512, "mm_dx_even")
        else:
            hst, wa, wx, wp = extra
            dh4, dcw, dcb, dwa, dwx, dba, dbx, dlam = sched.run(
                _odd_c_bwd, CARRY_US["odd_c_bwd"], h, hst, dmix3, full["odd_conv_w"][j], full["odd_conv_b"][j], wa, wx,
                jnp.swapaxes(wa, 1, 2), jnp.swapaxes(wx, 1, 2), full["odd_b_a"][j], full["odd_b_x"][j], full["odd_lam"][j],
                "odd_c_bwd")
            dh4, dwp, dds = _odd_d_bwd(h, dmix3, dh4, wp, jnp.swapaxes(wp, 1, 2), full["odd_d_scale"][j], "odd_d_bwd")
            gsum["odd_conv_w"][j], gsum["odd_conv_b"][j] = dcw, dcb.reshape(W)
            gsum["odd_w_a"][j], gsum["odd_w_x"][j] = dwa, dwx
            gsum["odd_b_a"][j], gsum["odd_b_x"][j], gsum["odd_lam"][j] = dba.reshape(W), dbx.reshape(W), dlam.reshape(W)
            gsum["odd_w_pool"][j], gsum["odd_d_scale"][j] = dwp, dds.reshape(W)
            chip_sum(sched.run(_mm_tn, CARRY_US["mm_dw_in_odd"], dh4, xb, 512, "mm_dw_in_odd"), "w_in_odd", f"in{layer}")
            dxn = sched.run(_mm_nn_res, CARRY_US["mm_dx_odd"], dh4, wt_in[layer], dz, 512, 512, "mm_dx_odd")
    grad_x = dxn[None]

    out_g, out_d, out_m, out_v = {}, {}, {}, {}
    for nm, kind, what, layers in (("odd_w_out", "odd", "out", (1, 3)), ("even_w_out", "even", "out", (0, 2)),
                                   ("odd_w_in", "odd", "in", (1, 3)), ("even_w_in", "even", "in", (0, 2))):
        gl = [reduced(f"{what}{l}", f"rs_final_w_{what}_{kind}") for l in layers]
        if nm == "even_w_in":
            view = lambda a: jnp.transpose(a, (0, 2, 1))
            res, _ = _adamw(view(wts[nm]), gl, view(mom[nm]), view(var[nm]), 168, f"adamw_{nm}")
            res = [view(a) for a in res]
        elif what == "in":
            res, _ = _adamw(wts[nm], [jnp.transpose(a) for a in gl], mom[nm], var[nm], 512, f"adamw_{nm}")
        else:
            res = sched.run(_adamw, CARRY_US["adamw_" + nm], wts[nm], gl, mom[nm], var[nm], 128, f"adamw_{nm}")
        out_d[nm], out_m[nm], out_v[nm], out_g[nm] = res

    g_small = {}
    off = 0
    for nm, p in zip(REP, rep_rows):
        r = p.shape[1]
        g_small[nm] = _rep_unpack(rep_all[:, off:off + r, :], wts[nm].shape)
        off += r
    off = n_rep
    for (nm, shape, axis), p in zip(SH, sh_rows):
        r = p.shape[1]
        g_small[nm] = red[off:off + r].reshape(wts[nm].shape)
        off += r

    def rows(a):
        f = a.reshape(-1)
        pad = (-f.shape[0]) % 128
        if pad:
            f = jnp.concatenate([f, jnp.zeros((pad,), a.dtype)])
        return f.reshape(-1, 128)

    small = REP + [nm for nm, _, _ in SH]
    each = lambda src: [rows(src[nm]) for nm in small]
    d2, m2, v2 = _adamw_many(each(wts), each(g_small), each(mom), each(var), "adamw_small")
    for i, nm in enumerate(small):
        n, shp = wts[nm].size, wts[nm].shape
        take = lambda a: a.reshape(-1)[:n].reshape(shp)
        out_g[nm], out_d[nm], out_m[nm], out_v[nm] = g_small[nm], take(d2[i]), take(m2[i]), take(v2[i])

    return (loss, grad_x, *[out_g[nm] for nm in NAMES], *[out_d[nm] for nm in NAMES],
            *[out_m[nm] for nm in NAMES], *[out_v[nm] for nm in NAMES])
```

```python
import functools

import jax
import jax.numpy as jnp
from jax import lax
from jax.experimental import pallas as pl
from jax.experimental.pallas import tpu as pltpu

F32 = jnp.float32
BF = jnp.bfloat16
MESH = pl.DeviceIdType.MESH
ANY = pl.BlockSpec(memory_space=pl.ANY)

N_DEV = 8
D = 2048
W = 1024
EVEN_IN = 5376
ODD_IN = 4096
CHUNK = 128
ALPHA = (2 * 4) ** 0.25
LN_EPS = 1e-5
ROPE_THETA = 500000.0
LRU_C = 8.0
LR, B1, B2, ADAM_EPS, WD, STEP = 0.001, 0.9, 0.999, 1e-08, 0.01, 10
NEG = -1e30
HEAD_COLS = 4


def _cp(vmem_mb=48, collective_id=None):
    return pltpu.CompilerParams(vmem_limit_bytes=vmem_mb * 1024 * 1024, collective_id=collective_id)


def _sig(x):
    return jax.nn.sigmoid(x)


def _silu_grad(x):
    s = _sig(x)
    return x * s, s * (1.0 + x * (1.0 - s))


def _dot(a, b):
    return jnp.dot(a, b, preferred_element_type=F32)


def _dot_nt(a, b):
    return lax.dot_general(a, b, (((1,), (1,)), ((), ())), preferred_element_type=F32)


def _dot_tn(a, b):
    return lax.dot_general(a, b, (((0,), (0,)), ((), ())), preferred_element_type=F32)


def _coords():
    return lax.axis_index("x"), lax.axis_index("y"), lax.axis_index("c")


def _chip(j):
    x, y, _ = _coords()
    return (1 - x if j & 2 else x), (1 - y if j & 1 else y)


X_NB, Y_NB, DIAG, SIB = 4, 2, 6, 1
EVERYONE = frozenset(range(1, N_DEV))
BARRIER_IDS = {}


class _Comm:
    def collective_id(self):
        return BARRIER_IDS.setdefault(frozenset(self.peers), len(BARRIER_IDS))

    def start(self, ins, outs, sems):
        barrier = pltpu.get_barrier_semaphore()
        for d in sorted(self.peers):
            pl.semaphore_signal(barrier, inc=1, device_id=_peer(d)[0], device_id_type=MESH)
        pl.semaphore_wait(barrier, len(self.peers))
        for cp in self.copies(ins, outs, sems):
            cp.start()

    def wait(self, ins, outs, sems):
        for cp in self.copies(ins, outs, sems):
            cp.wait()


class _Join(_Comm):
    def __init__(self, parts):
        self.parts = list(parts)
        self.peers = frozenset().union(*[p.peers for p in self.parts])
        self.inputs = [a for p in self.parts for a in p.inputs]
        self.out_shapes = [s for p in self.parts for s in p.out_shapes]
        self.sem_shapes = [s for p in self.parts for s in p.sem_shapes]
        self.aliases = {}
        i0 = o0 = 0
        for p in self.parts:
            for i, o in getattr(p, "aliases", {}).items():
                self.aliases[i0 + i] = o0 + o
            i0, o0 = i0 + len(p.inputs), o0 + len(p.out_shapes)

    def copies(self, ins, outs, sems):
        res = []
        i0 = o0 = s0 = 0
        for p in self.parts:
            ni, no, ns = len(p.inputs), len(p.out_shapes), len(p.sem_shapes)
            res += p.copies(ins[i0:i0 + ni], outs[o0:o0 + no], sems[s0:s0 + ns])
            i0, o0, s0 = i0 + ni, o0 + no, s0 + ns
        return res


ROWS_US = {"ag1": 0.104, "ag2": 0.052, "agd": 0.027, "rsd": 0.027, "rs": 0.205}
N_COPIES = {"ag1": 2, "ag2": 2, "agd": 4, "rsd": 4, "rs": 3}
TASK_PEERS = {"ag1": {X_NB, Y_NB}, "ag2": {X_NB, Y_NB}, "agd": {SIB}, "rsd": {SIB}, "rs": {X_NB, Y_NB, DIAG}}
ROW_CHUNK = {672: 224, 512: 128, 256: 128}
CARRY_US = {"mm_h_even": 58, "mm_h_odd": 47, "even_fwd": 42, "odd_c_fwd": 37, "mm_out_ln": 33, "ln_bwd": 23, "mm_dmix": 26,
            "mm_dw_out": 25, "even_bwd": 95, "odd_c_bwd": 70, "mm_dw_in_even": 56, "mm_dw_in_odd": 44, "mm_dx_even": 60,
            "mm_dx_odd": 50, "adamw_even_w_in": 30, "adamw_odd_w_in": 28, "adamw_even_w_out": 11, "adamw_odd_w_out": 11}
FWD_OVERBOOK = 1.15
FLUSH_EXTRA_US = 60.0


def _cost_us(task, reg):
    kind, src, _, lo, hi = task
    return ROWS_US[kind] * (hi - lo) * reg[src].shape[-1] * reg[src].dtype.itemsize / 4096.0


class _Copies(_Comm):
    def __init__(self, tasks, reg):
        self.tasks = list(tasks)
        self.out_names, self.in_names = [], []
        for kind, src, dst, lo, hi in self.tasks:
            if dst not in self.out_names:
                self.out_names.append(dst)
        for kind, src, dst, lo, hi in self.tasks:
            if src not in self.out_names and src not in self.in_names:
                self.in_names.append(src)
        self.out_shapes, self.aliases = [], {}
        for o, dst in enumerate(self.out_names):
            if dst in reg:
                self.aliases[len(self.in_names)] = o
                self.in_names.append(dst)
                self.out_shapes.append(jax.ShapeDtypeStruct(reg[dst].shape, reg[dst].dtype))
            else:
                kind, src = next((t[0], t[1]) for t in self.tasks if t[2] == dst)
                shape = ({"rsd": 4, "rs": 3}[kind],) + reg[src].shape[1:]
                self.out_shapes.append(jax.ShapeDtypeStruct(shape, reg[src].dtype))
        self.inputs = [reg[nm] for nm in self.in_names]
        n = sum(N_COPIES[t[0]] for t in self.tasks)
        self.sem_shapes = [pltpu.SemaphoreType.DMA((n,)), pltpu.SemaphoreType.DMA((n,))]
        self.peers = frozenset().union(*[TASK_PEERS[t[0]] for t in self.tasks])

    def copies(self, ins, outs, sems):
        send, recv = sems
        x, y, c = _coords()
        me = 4 * x + 2 * y + c
        xn, yn = (1 - x, y, c), (x, 1 - y, c)
        at_xn, at_yn = 4 * (1 - x) + 2 * y + c, 4 * x + 2 * (1 - y) + c
        ref = dict(zip(self.in_names, ins))
        ref.update(zip(self.out_names, outs))
        res = []

        def copy(src, dst, to):
            i = len(res)
            res.append(pltpu.make_async_remote_copy(src_ref=src, dst_ref=dst, send_sem=send.at[i], recv_sem=recv.at[i],
                                                    device_id=to, device_id_type=MESH))

        for kind, src, dst, lo, hi in self.tasks:
            n = hi - lo
            if kind == "ag1":
                for to in (xn, yn):
                    copy(ref[src].at[pl.ds(lo, n)], ref[dst].at[me, pl.ds(lo, n)], to)
            elif kind == "ag2":
                h = n // 2
                first, second = ref[dst].at[at_xn, pl.ds(lo, h)], ref[dst].at[at_yn, pl.ds(lo + h, n - h)]
                copy(first, first, yn)
                copy(second, second, xn)
            elif kind == "agd":
                for j in range(4):
                    px, py = _chip(j)
                    rows = ref[dst].at[4 * px + 2 * py + c, pl.ds(lo, n)]
                    copy(rows, rows, (x, y, 1 - c))
            elif kind == "rsd":
                for j in range(4):
                    px, py = _chip(j)
                    copy(ref[src].at[4 * px + 2 * py + 1 - c, pl.ds(lo, n)], ref[dst].at[j, pl.ds(lo, n)], (x, y, 1 - c))
            else:
                for j in (1, 2, 3):
                    px, py = _chip(j)
                    copy(ref[src].at[j, pl.ds(lo, n)], ref[dst].at[j - 1, pl.ds(lo, n)], (px, py, c))
        return res


class _Sched:
    def __init__(self, reg):
        self.reg, self.queue, self.later = reg, [], []
        self.overhang = 0.5
        self.after_landing = None

    def add(self, tasks, first=False):
        self.queue = list(tasks) + self.queue if first else self.queue + list(tasks)

    def pending(self, dst):
        return any(t[2] == dst for t in self.queue + self.later)

    def take(self, budget_us, must=None, overhang=0.5):
        self.queue, self.later = self.later + self.queue, []
        picked, us = [], 0.0
        rest = []
        for t in self.queue:
            cost = _cost_us(t, self.reg)
            if (must is not None and t[2] == must) or us + (1.0 - overhang) * cost <= budget_us:
                picked.append(t)
                us += cost
                if t[0] in ("ag1", "ag2"):
                    self.later.append(({"ag1": "ag2", "ag2": "agd"}[t[0]], t[2], t[2], t[3], t[4]))
            else:
                rest.append(t)
        self.queue = rest
        return _Copies(picked, self.reg) if picked else None

    def landed(self, comm, got):
        if comm is not None:
            for nm, a in zip(comm.out_names, got):
                self.reg[nm] = a
        if self.after_landing is not None:
            self.after_landing()

    def run(self, builder, budget_us, *args, **kw):
        comm = self.take(budget_us, overhang=self.overhang)
        res, got = builder(*args, comm=comm, **kw)
        self.landed(comm, got)
        return res

    def flush(self, dst, budget_us=0.0, beside=None):
        res = []
        while self.pending(dst):
            comm = self.take(budget_us, must=dst)
            got = _comm_only(comm if beside is None else _Join([comm, beside]), "flush_" + dst)
            res, beside = got[len(comm.out_shapes):], None
            self.landed(comm, got[:len(comm.out_shapes)])
        return res


def _rows(name_src, name_dst, kind, n_rows, chunk):
    return [(kind, name_src, name_dst, lo, min(lo + chunk, n_rows)) for lo in range(0, n_rows, chunk)]


def _pcall(body, *, grid, in_specs, out_specs, out_shape, name, scratch=(), vmem=48, comm=None):
    in_specs, out_specs, out_shape, scratch = list(in_specs), list(out_specs), list(out_shape), list(scratch)
    if comm is None:
        call = pl.pallas_call(body, grid=grid, in_specs=in_specs, out_specs=out_specs, out_shape=out_shape,
                              scratch_shapes=scratch, name=name, compiler_params=_cp(vmem))
        return lambda *args: (call(*args), [])
    n_in, n_out, n_scr = len(in_specs), len(out_specs), len(scratch)
    c_in, c_out = len(comm.inputs), len(comm.out_shapes)
    aliases = {n_in + i: n_out + o for i, o in getattr(comm, "aliases", {}).items()}

    def wrapped(*refs):
        ins, cins = refs[:n_in], refs[n_in:n_in + c_in]
        o0 = n_in + c_in
        outs, couts = refs[o0:o0 + n_out], refs[o0 + n_out:o0 + n_out + c_out]
        s0 = o0 + n_out + c_out
        scr, sems = refs[s0:s0 + n_scr], refs[s0 + n_scr:]
        ids = [pl.program_id(a) for a in range(len(grid))]
        first = functools.reduce(jnp.logical_and, [i == 0 for i in ids])
        last = functools.reduce(jnp.logical_and, [i == g - 1 for i, g in zip(ids, grid)])

        @pl.when(first)
        def _():
            comm.start(cins, couts, sems)

        body(*ins, *outs, *scr)

        @pl.when(last)
        def _():
            comm.wait(cins, couts, sems)

    call = pl.pallas_call(wrapped, grid=grid, in_specs=in_specs + [ANY] * c_in, out_specs=out_specs + [ANY] * c_out,
                          out_shape=out_shape + list(comm.out_shapes), scratch_shapes=scratch + list(comm.sem_shapes),
                          input_output_aliases=aliases, name=name, compiler_params=_cp(vmem, comm.collective_id()))

    def run(*args):
        res = call(*args, *comm.inputs)
        return res[:n_out], res[n_out:]

    return run


def _comm_only(comm, name):
    c_in, c_out = len(comm.inputs), len(comm.out_shapes)

    def body(*refs):
        cins, couts, sems = refs[:c_in], refs[c_in:c_in + c_out], refs[c_in + c_out:]
        comm.start(cins, couts, sems)
        comm.wait(cins, couts, sems)

    return pl.pallas_call(body, in_specs=[ANY] * c_in, out_specs=[ANY] * c_out, out_shape=list(comm.out_shapes),
                          scratch_shapes=list(comm.sem_shapes), input_output_aliases=dict(getattr(comm, "aliases", {})),
                          name=name, compiler_params=pltpu.CompilerParams(collective_id=comm.collective_id()))(*comm.inputs)


def _chip_blocks():
    _, _, c = _coords()
    return jnp.stack([4 * px + 2 * py + c for px, py in map(_chip, range(4))]).astype(jnp.int32)


def _add_pairs(g8, b4, name):
    _, R, C = b4.shape

    def body(idx_ref, a_ref, b_ref, o_ref):
        o_ref[...] = (a_ref[...].astype(F32) + b_ref[...].astype(F32)).astype(BF)

    blk = pl.BlockSpec((None, R, C), lambda j, idx: (j, 0, 0))
    grid_spec = pltpu.PrefetchScalarGridSpec(
        num_scalar_prefetch=1, grid=(4,),
        in_specs=[pl.BlockSpec((None, R, C), lambda j, idx: (idx[j], 0, 0)), blk], out_specs=blk)
    return pl.pallas_call(body, grid_spec=grid_spec, out_shape=jax.ShapeDtypeStruct(b4.shape, BF), name=name,
                          compiler_params=_cp())(_chip_blocks(), g8, b4)


def _rs_final(s4, r3, name):
    _, R, C = s4.shape
    tr = R // 2

    def body(s_ref, r_ref, o_ref):
        o_ref[...] = ((s_ref[...].astype(F32) + r_ref[0].astype(F32)) + r_ref[1].astype(F32)) + r_ref[2].astype(F32)

    return pl.pallas_call(
        body, grid=(2,),
        in_specs=[pl.BlockSpec((None, tr, C), lambda i: (0, i, 0)), pl.BlockSpec((3, tr, C), lambda i: (0, i, 0))],
        out_specs=pl.BlockSpec((tr, C), lambda i: (i, 0)), out_shape=jax.ShapeDtypeStruct((R, C), F32),
        name=name, compiler_params=_cp())(s4, r3)


def _mm_nt(a, w, tm, tn, name, out3=False, comm=None):
    M, K = a.shape
    N = w.shape[0]
    tm = min(tm, M)

    def body(a_ref, w_ref, o_ref):
        o_ref[...] = _dot_nt(a_ref[...], w_ref[...])

    if out3:
        per = W // tn
        out_shape = jax.ShapeDtypeStruct((N // W, M, W), F32)
        out_spec = pl.BlockSpec((None, tm, tn), lambda i, j: (j // per, i, j % per))
    else:
        out_shape = jax.ShapeDtypeStruct((M, N), F32)
        out_spec = pl.BlockSpec((tm, tn), lambda i, j: (i, j))
    (res,), extra = _pcall(
        body, grid=(M // tm, N // tn),
        in_specs=[pl.BlockSpec((tm, K), lambda i, j: (i, 0)), pl.BlockSpec((tn, K), lambda i, j: (j, 0))],
        out_specs=[out_spec], out_shape=[out_shape], name=name, comm=comm)(a, w)
    return res, extra


def _mm_tn(a, b, tm, name, comm=None):
    K, N = b.shape
    if a.ndim == 3:
        M = a.shape[0] * W
        per = W // tm
        a_spec = pl.BlockSpec((None, K, tm), lambda i: (i // per, 0, i % per))
    else:
        M = a.shape[1]
        a_spec = pl.BlockSpec((K, tm), lambda i: (0, i))

    def body(a_ref, b_ref, o_ref):
        o_ref[...] = _dot_tn(a_ref[...], b_ref[...]).astype(BF)

    (out,), extra = _pcall(
        body, grid=(M // tm,),
        in_specs=[a_spec, pl.BlockSpec((K, N), lambda i: (0, 0))],
        out_specs=[pl.BlockSpec((tm, N), lambda i: (i, 0))],
        out_shape=[jax.ShapeDtypeStruct((M, N), BF)], name=name, vmem=56, comm=comm)(a, b)
    return out, extra


def _mm_nn_res(a, w, res, tm, tn, name, comm=None):
    K, N = w.shape
    if a.ndim == 3:
        P, M = a.shape[0], a.shape[1]
        tm = min(tm, M)
        a_spec = pl.BlockSpec((P, tm, W), lambda j, i: (0, i, 0))
    else:
        P, M = 0, a.shape[0]
        tm = min(tm, M)
        a_spec = pl.BlockSpec((tm, K), lambda j, i: (i, 0))

    def body(a_ref, w_ref, r_ref, o_ref):
        if P:
            d = _dot(a_ref[0], w_ref[0:W, :])
            for p in range(1, P):
                d = d + _dot(a_ref[p], w_ref[p * W:(p + 1) * W, :])
        else:
            d = _dot(a_ref[...], w_ref[...])
        o_ref[...] = ALPHA * r_ref[...] + d

    (out,), extra = _pcall(
        body, grid=(N // tn, M // tm),
        in_specs=[a_spec, pl.BlockSpec((K, tn), lambda j, i: (0, j)), pl.BlockSpec((tm, tn), lambda j, i: (i, j))],
        out_specs=[pl.BlockSpec((tm, tn), lambda j, i: (i, j))],
        out_shape=[jax.ShapeDtypeStruct((M, N), F32)], name=name, comm=comm)(a, w, res)
    return out, extra


def _mm_out_ln(mix3, w_out, x, g, b, name, comm=None):
    S = x.shape[0]
    tm = min(256, S)

    def body(m_ref, w_ref, x_ref, g_ref, b_ref, z_ref, xn_ref, xb_ref):
        acc = _dot(m_ref[0], w_ref[0:W, :]) + _dot(m_ref[1], w_ref[W:2 * W, :])
        z = ALPHA * x_ref[...] + acc
        mu = jnp.mean(z, axis=1, keepdims=True)
        zc = z - mu
        var = jnp.mean(zc * zc, axis=1, keepdims=True)
        xn = zc * lax.rsqrt(var + LN_EPS) * g_ref[...] + b_ref[...]
        z_ref[...] = z
        xn_ref[...] = xn
        xb_ref[...] = xn.astype(BF)

    row = pl.BlockSpec((tm, D), lambda i: (i, 0))
    vec = pl.BlockSpec((1, D), lambda i: (0, 0))
    return _pcall(
        body, grid=(S // tm,),
        in_specs=[pl.BlockSpec((2, tm, W), lambda i: (0, i, 0)), pl.BlockSpec((D, D), lambda i: (0, 0)), row, vec, vec],
        out_specs=[row, row, row],
        out_shape=[jax.ShapeDtypeStruct((S, D), F32), jax.ShapeDtypeStruct((S, D), F32), jax.ShapeDtypeStruct((S, D), BF)],
        name=name, comm=comm)(mix3, w_out, x, g.reshape(1, D), b.reshape(1, D))


def _ln_bwd(dxn, z, g, name, comm=None, target=None):
    S = z.shape[0]
    tm = min(256, S)
    head = target is not None

    def body(*refs):
        if head:
            d_ref, t_ref, z_ref, g_ref, dz_ref, dzb_ref, dg_ref, db_ref, p_ref = refs
        else:
            d_ref, z_ref, g_ref, dz_ref, dzb_ref, dg_ref, db_ref = refs
        i = pl.program_id(0)
        zz = z_ref[...]
        mu = jnp.mean(zz, axis=1, keepdims=True)
        zc = zz - mu
        var = jnp.mean(zc * zc, axis=1, keepdims=True)
        rstd = lax.rsqrt(var + LN_EPS)
        xhat = zc * rstd
        dy = d_ref[...]
        if head:
            e = dy - t_ref[...]
            dy = e * (1.0 / D)

            @pl.when(i == 0)
            def _():
                p_ref[...] = jnp.zeros_like(p_ref)

            p_ref[...] += jnp.sum(jnp.sum(e * e, axis=1, keepdims=True), axis=0, keepdims=True)
        dyg = dy * g_ref[...]
        m1 = jnp.mean(dyg, axis=1, keepdims=True)
        m2 = jnp.mean(dyg * xhat, axis=1, keepdims=True)
        dz = rstd * (dyg - m1 - xhat * m2)
        dz_ref[...] = dz
        dzb_ref[...] = dz.astype(BF)

        @pl.when(i == 0)
        def _():
            dg_ref[...] = jnp.zeros_like(dg_ref)
            db_ref[...] = jnp.zeros_like(db_ref)

        dg_ref[...] += jnp.sum(dy * xhat, axis=0, keepdims=True)
        db_ref[...] += jnp.sum(dy, axis=0, keepdims=True)

    row = pl.BlockSpec((tm, D), lambda i: (i, 0))
    vec = pl.BlockSpec((1, D), lambda i: (0, 0))
    out_specs = [row, row, vec, vec] + ([pl.BlockSpec((8, 128), lambda i: (0, 0))] if head else [])
    out_shape = [jax.ShapeDtypeStruct((S, D), F32), jax.ShapeDtypeStruct((S, D), BF), jax.ShapeDtypeStruct((1, D), F32),
                 jax.ShapeDtypeStruct((1, D), F32)] + ([jax.ShapeDtypeStruct((8, 128), F32)] if head else [])
    operands = (dxn, target, z, g.reshape(1, D)) if head else (dxn, z, g.reshape(1, D))
    return _pcall(body, grid=(S // tm,), in_specs=[row] * (len(operands) - 1) + [vec], out_specs=out_specs,
                  out_shape=out_shape, name=name, comm=comm)(*operands)


def _rope_fwd(t, r_ref):
    return (t * r_ref[:, 0:128] + pltpu.roll(t, 120, 1) * r_ref[:, 128:256]
            + pltpu.roll(t, 8, 1) * r_ref[:, 256:384])


def _rope_bwd(g, r_ref):
    return (g * r_ref[:, 0:128] + pltpu.roll(g * r_ref[:, 128:256], 8, 1)
            + pltpu.roll(g * r_ref[:, 256:384], 120, 1))


def _dup_heads(kb):
    lo = lax.broadcasted_iota(jnp.int32, kb.shape, 1) < 64
    sw = pltpu.roll(kb, 64, 1)
    return [jnp.where(lo, kb, sw).astype(BF), jnp.where(lo, sw, kb).astype(BF)]


def _even_fwd(h, rope, lng, lnb, ws, bsb, sinks, name, comm=None):
    S = h.shape[0]
    nb = S // CHUNK

    def body(h_ref, hp_ref, rc_ref, rp_ref, lng_ref, lnb_ref, ws_ref, bsb_ref, sink_ref, mix_ref, o_ref, l_ref):
        n = pl.program_id(0)
        lane = lax.broadcasted_iota(jnp.int32, (128, 128), 1)
        rowi = lax.broadcasted_iota(jnp.int32, (128, 128), 0)
        tri = rowi >= lane
        lane_lo = lane < 64
        v = h_ref[:, W:2 * W]
        mu = jnp.mean(v, axis=1, keepdims=True)
        vc = v - mu
        var = jnp.mean(vc * vc, axis=1, keepdims=True)
        vn = vc * lax.rsqrt(var + LN_EPS) * lng_ref[...] + lnb_ref[...]
        ms = [_dot(jnp.where(tri, ws_ref[g], 0.0).astype(BF), vn[:, g * 128:(g + 1) * 128].astype(BF)) for g in range(8)]
        for g in range(8):
            sl = slice(g * 128, (g + 1) * 128)
            ag = h_ref[:, 2 * W + g * 128:2 * W + (g + 1) * 128]
            mix_ref[0, :, sl] = (h_ref[:, sl] * (ms[g] + bsb_ref[g]) * (ag * _sig(ag))).astype(BF)
        kb = jnp.concatenate([_rope_fwd(hp_ref[:, 0:128], rp_ref), _rope_fwd(h_ref[:, 4096:4224], rc_ref)], axis=0)
        vb = jnp.concatenate([hp_ref[:, 128:256], h_ref[:, 4224:4352]], axis=0)
        k2 = _dup_heads(kb)
        v2 = _dup_heads(vb)
        qi = lax.broadcasted_iota(jnp.int32, (128, 256), 0)
        kj = lax.broadcasted_iota(jnp.int32, (128, 256), 1)
        diff = qi + 128 - kj
        valid = (diff >= 0) & (diff < 128) & ((n > 0) | (kj >= 128))
        lacc = jnp.zeros((128, 128), F32)
        for j0 in range(0, 8, HEAD_COLS):
            heads = [(j, half) for j in range(j0, j0 + HEAD_COLS) for half in range(2)]
            sc, pr, oh = {}, {}, {}
            for j in range(j0, j0 + HEAD_COLS):
                qc = _rope_fwd(h_ref[:, 3072 + j * 128:3072 + (j + 1) * 128], rc_ref)
                sc[j, 0] = _dot_nt(jnp.where(lane_lo, qc, 0.0).astype(BF), k2[j // 4])
                sc[j, 1] = _dot_nt(jnp.where(lane_lo, 0.0, qc).astype(BF), k2[j // 4])
            for j, half in heads:
                hq = 2 * j + half
                s = jnp.where(valid, sc[j, half] * 0.125, NEG)
                sk = sink_ref[hq]
                mx = jnp.maximum(jnp.max(s, axis=1, keepdims=True), sk)
                p = jnp.exp(s - mx)
                den = jnp.sum(p, axis=1, keepdims=True) + jnp.exp(sk - mx)
                pr[j, half] = (p / den).astype(BF)
                lacc = jnp.where(lane == hq, mx + jnp.log(den), lacc)
            for j, half in heads:
                oh[j, half] = _dot(pr[j, half], v2[j // 4])
            for j in range(j0, j0 + HEAD_COLS):
                cs = slice(j * 128, (j + 1) * 128)
                ocol = jnp.where(lane_lo, oh[j, 0], oh[j, 1])
                bg = h_ref[:, 4352 + j * 128:4352 + (j + 1) * 128]
                o_ref[:, cs] = ocol
                mix_ref[1, :, cs] = (ocol * (bg * _sig(bg))).astype(BF)
        l_ref[...] = lacc

    prev = lambda n: jnp.maximum(n - 1, 0)
    full = lambda shape: pl.BlockSpec(shape, lambda n: (0,) * len(shape))
    return _pcall(
        body, grid=(nb,),
        in_specs=[pl.BlockSpec((CHUNK, EVEN_IN), lambda n: (n, 0)),
                  pl.BlockSpec((CHUNK, 256), lambda n: (prev(n), 16)),
                  pl.BlockSpec((CHUNK, 384), lambda n: (n, 0)),
                  pl.BlockSpec((CHUNK, 384), lambda n: (prev(n), 0)),
                  full((1, W)), full((1, W)), full((8, 128, 128)), full((8, 128, 128)),
                  pl.BlockSpec(memory_space=pltpu.SMEM)],
        out_specs=[pl.BlockSpec((2, CHUNK, W), lambda n: (0, n, 0)),
                   pl.BlockSpec((CHUNK, W), lambda n: (n, 0)),
                   pl.BlockSpec((CHUNK, 128), lambda n: (n, 0))],
        out_shape=[jax.ShapeDtypeStruct((2, S, W), BF), jax.ShapeDtypeStruct((S, W), F32),
                   jax.ShapeDtypeStruct((S, 128), F32)],
        name=name, comm=comm)(h, h, rope, rope, lng.reshape(1, W), lnb.reshape(1, W), ws, bsb, sinks)


def _even_bwd(h, dmix3, o, l, rope, lng, lnb, ws, wst, bsb, sinks, name, comm=None):
    S = h.shape[0]
    nb = S // CHUNK

    def body(h_ref, hp_ref, hn_ref, dm_ref, dmn_ref, o_ref, on_ref, l_ref, ln_ref, rc_ref, rp_ref, rn_ref,
             lng_ref, lnb_ref, ws_ref, wst_ref, bsb_ref, sink_ref,
             dh_ref, dws_ref, dbs_ref, dlng_ref, dlnb_ref, dsink_ref, dvn_ref):
        n = pl.program_id(0)

        @pl.when(n == 0)
        def _():
            dws_ref[...] = jnp.zeros_like(dws_ref)
            dbs_ref[...] = jnp.zeros_like(dbs_ref)
            dlng_ref[...] = jnp.zeros_like(dlng_ref)
            dlnb_ref[...] = jnp.zeros_like(dlnb_ref)
            dsink_ref[...] = jnp.zeros_like(dsink_ref)

        lane = lax.broadcasted_iota(jnp.int32, (128, 128), 1)
        rowi = lax.broadcasted_iota(jnp.int32, (128, 128), 0)
        lane1 = lax.broadcasted_iota(jnp.int32, (1, 128), 1)
        tri = rowi >= lane
        tri_t = lane >= rowi
        lane_lo = lane < 64
        v = h_ref[:, W:2 * W]
        mu = jnp.mean(v, axis=1, keepdims=True)
        vc = v - mu
        var = jnp.mean(vc * vc, axis=1, keepdims=True)
        rstd = lax.rsqrt(var + LN_EPS)
        vhat = vc * rstd
        vn = vhat * lng_ref[...] + lnb_ref[...]
        dbs_acc = jnp.zeros((128, 128), F32)
        vng = [vn[:, g * 128:(g + 1) * 128].astype(BF) for g in range(8)]
        ms = [_dot(jnp.where(tri, ws_ref[g], 0.0).astype(BF), vng[g]) for g in range(8)]
        dmb = []
        for g in range(8):
            sl = slice(g * 128, (g + 1) * 128)
            m = ms[g] + bsb_ref[g]
            ag = h_ref[:, 2 * W + g * 128:2 * W + (g + 1) * 128]
            sg, dsg = _silu_grad(ag)
            u = h_ref[:, sl]
            da = dm_ref[0, :, sl]
            dmm = da * u * sg
            dh_ref[:, sl] = (da * m * sg).astype(BF)
            dh_ref[:, 2 * W + g * 128:2 * W + (g + 1) * 128] = (da * u * m * dsg).astype(BF)
            dmb.append(dmm.astype(BF))
            dbs_acc = jnp.where(lane == g, jnp.sum(dmm, axis=1, keepdims=True), dbs_acc)
        dvs = [_dot(jnp.where(tri_t, wst_ref[g], 0.0).astype(BF), dmb[g]) for g in range(8)]
        dwss = [_dot_nt(dmb[g], vng[g]) for g in range(8)]
        for g in range(8):
            dvn_ref[:, g * 128:(g + 1) * 128] = dvs[g]
            dws_ref[g] += jnp.where(tri, dwss[g], 0.0)
        dbs_ref[...] += dbs_acc
        dvn = dvn_ref[...]
        dlng_ref[...] += jnp.sum(dvn * vhat, axis=0, keepdims=True)
        dlnb_ref[...] += jnp.sum(dvn, axis=0, keepdims=True)
        dyg = dvn * lng_ref[...]
        m1 = jnp.mean(dyg, axis=1, keepdims=True)
        m2 = jnp.mean(dyg * vhat, axis=1, keepdims=True)
        dh_ref[:, W:2 * W] = (rstd * (dyg - m1 - vhat * m2)).astype(BF)
        kcur = _rope_fwd(h_ref[:, 4096:4224], rc_ref)
        kb = jnp.concatenate([_rope_fwd(hp_ref[:, 0:128], rp_ref), kcur], axis=0)
        vb = jnp.concatenate([hp_ref[:, 128:256], h_ref[:, 4224:4352]], axis=0)
        k2 = _dup_heads(kb)
        v2 = _dup_heads(vb)
        kc2 = _dup_heads(kcur)
        vc2 = _dup_heads(h_ref[:, 4224:4352])
        qi = lax.broadcasted_iota(jnp.int32, (128, 256), 0)
        kj = lax.broadcasted_iota(jnp.int32, (128, 256), 1)
        diff = qi + 128 - kj
        valid = (diff >= 0) & (diff < 128) & ((n > 0) | (kj >= 128))
        validn = (lane > rowi) & (n < nb - 1)
        lc = l_ref[...]
        lnx = ln_ref[...]
        dk = [jnp.zeros((128, 128), F32), jnp.zeros((128, 128), F32)]
        dv = [jnp.zeros((128, 128), F32), jnp.zeros((128, 128), F32)]
        dsk_acc = jnp.zeros((1, 128), F32)
        for j0 in range(0, 8, HEAD_COLS):
            heads = [(j, half) for j in range(j0, j0 + HEAD_COLS) for half in range(2)]
            t = {}
            for j in range(j0, j0 + HEAD_COLS):
                cs = slice(j * 128, (j + 1) * 128)
                qc = _rope_fwd(h_ref[:, 3072 + j * 128:3072 + (j + 1) * 128], rc_ref)
                qn = _rope_fwd(hn_ref[:, 3072 + j * 128:3072 + (j + 1) * 128], rn_ref)
                bg = h_ref[:, 4352 + j * 128:4352 + (j + 1) * 128]
                sgb, dsgb = _silu_grad(bg)
                db = dm_ref[1, :, cs]
                oc = o_ref[:, cs]
                do = db * sgb
                dh_ref[:, 4352 + j * 128:4352 + (j + 1) * 128] = (db * oc * dsgb).astype(BF)
                bgn = hn_ref[:, 4352 + j * 128:4352 + (j + 1) * 128]
                don = dmn_ref[1, :, cs] * (bgn * _sig(bgn))
                prod = do * oc
                prodn = don * on_ref[:, cs]
                for half in range(2):
                    hq = 2 * j + half
                    hm = lane_lo if half == 0 else jnp.logical_not(lane_lo)
                    t[j, half] = dict(
                        dsum=jnp.sum(jnp.where(hm, prod, 0.0), axis=1, keepdims=True),
                        dsumn=jnp.sum(jnp.where(hm, prodn, 0.0), axis=1, keepdims=True),
                        lh=jnp.sum(jnp.where(lane == hq, lc, 0.0), axis=1, keepdims=True),
                        lhn=jnp.sum(jnp.where(lane == hq, lnx, 0.0), axis=1, keepdims=True),
                        qm=jnp.where(hm, qc, 0.0).astype(BF), dom=jnp.where(hm, do, 0.0).astype(BF),
                        qnm=jnp.where(hm, qn, 0.0).astype(BF), donm=jnp.where(hm, don, 0.0).astype(BF))
            for j, half in heads:
                e, hk = t[j, half], j // 4
                e["s"], e["dp"] = _dot_nt(e["qm"], k2[hk]), _dot_nt(e["dom"], v2[hk])
                e["sn"], e["dpn"] = _dot_nt(e["qnm"], kc2[hk]), _dot_nt(e["donm"], vc2[hk])
            for j, half in heads:
                e, hq = t[j, half], 2 * j + half
                p = jnp.exp(jnp.where(valid, e["s"] * 0.125 - e["lh"], NEG))
                ds = p * (e["dp"] - e["dsum"])
                pn = jnp.exp(jnp.where(validn, e["sn"] * 0.125 - e["lhn"], NEG))
                dsn = pn * (e["dpn"] - e["dsumn"])
                psink = jnp.exp(sink_ref[hq] - e["lh"])
                dsk_acc = jnp.where(lane1 == hq, -jnp.sum(psink * e["dsum"], axis=0, keepdims=True), dsk_acc)
                e["ds"] = ds.astype(BF)
                e["pt"], e["dst"] = jnp.transpose(p[:, 128:256]).astype(BF), jnp.transpose(ds[:, 128:256]).astype(BF)
                e["pnt"], e["dsnt"] = jnp.transpose(pn).astype(BF), jnp.transpose(dsn).astype(BF)
            for j, half in heads:
                e, hk = t[j, half], j // 4
                e["dq"] = _dot(e["ds"], k2[hk])
                e["dv"] = _dot(e["pt"], e["dom"]) + _dot(e["pnt"], e["donm"])
                e["dk"] = _dot(e["dst"], e["qm"]) + _dot(e["dsnt"], e["qnm"])
            for j in range(j0, j0 + HEAD_COLS):
                hk = j // 4
                dqcol = jnp.where(lane_lo, t[j, 0]["dq"], t[j, 1]["dq"]) * 0.125
                dh_ref[:, 3072 + j * 128:3072 + (j + 1) * 128] = _rope_bwd(dqcol, rc_ref).astype(BF)
                dv[hk] = dv[hk] + t[j, 0]["dv"] + t[j, 1]["dv"]
                dk[hk] = dk[hk] + (t[j, 0]["dk"] + t[j, 1]["dk"]) * 0.125
        fold = lambda a: a + pltpu.roll(a, 64, 1)
        dh_ref[:, 4096:4224] = _rope_bwd(jnp.where(lane_lo, fold(dk[0]), fold(dk[1])), rc_ref).astype(BF)
        dh_ref[:, 4224:4352] = jnp.where(lane_lo, fold(dv[0]), fold(dv[1])).astype(BF)
        dsink_ref[...] += dsk_acc

    prev = lambda n: jnp.maximum(n - 1, 0)
    nxt = lambda n: jnp.minimum(n + 1, nb - 1)
    full = lambda shape: pl.BlockSpec(shape, lambda n: (0,) * len(shape))
    return _pcall(
        body, grid=(nb,),
        in_specs=[pl.BlockSpec((CHUNK, EVEN_IN), lambda n: (n, 0)),
                  pl.BlockSpec((CHUNK, 256), lambda n: (prev(n), 16)),
                  pl.BlockSpec((CHUNK, EVEN_IN), lambda n: (nxt(n), 0)),
                  pl.BlockSpec((2, CHUNK, W), lambda n: (0, n, 0)),
                  pl.BlockSpec((2, CHUNK, W), lambda n: (0, nxt(n), 0)),
                  pl.BlockSpec((CHUNK, W), lambda n: (n, 0)),
                  pl.BlockSpec((CHUNK, W), lambda n: (nxt(n), 0)),
                  pl.BlockSpec((CHUNK, 128), lambda n: (n, 0)),
                  pl.BlockSpec((CHUNK, 128), lambda n: (nxt(n), 0)),
                  pl.BlockSpec((CHUNK, 384), lambda n: (n, 0)),
                  pl.BlockSpec((CHUNK, 384), lambda n: (prev(n), 0)),
                  pl.BlockSpec((CHUNK, 384), lambda n: (nxt(n), 0)),
                  full((1, W)), full((1, W)), full((8, 128, 128)), full((8, 128, 128)), full((8, 128, 128)),
                  pl.BlockSpec(memory_space=pltpu.SMEM)],
        out_specs=[pl.BlockSpec((CHUNK, EVEN_IN), lambda n: (n, 0)),
                   full((8, 128, 128)), full((128, 128)), full((1, W)), full((1, W)), full((1, 128))],
        out_shape=[jax.ShapeDtypeStruct((S, EVEN_IN), BF), jax.ShapeDtypeStruct((8, 128, 128), F32),
                   jax.ShapeDtypeStruct((128, 128), F32), jax.ShapeDtypeStruct((1, W), F32),
                   jax.ShapeDtypeStruct((1, W), F32), jax.ShapeDtypeStruct((1, 128), F32)],
        scratch=[pltpu.VMEM((CHUNK, W), F32)], name=name, comm=comm,
    )(h, h, h, dmix3, dmix3, o, o, l, l, rope, rope, rope, lng.reshape(1, W), lnb.reshape(1, W), ws, wst, bsb, sinks)


def _expm1(x):
    ser = x * (1.0 + x * (0.5 + x * (1.0 / 6.0 + x * (1.0 / 24.0))))
    return jnp.where(jnp.abs(x) < 1e-2, ser, jnp.exp(x) - 1.0)


def _softplus_neg(lam):
    z = -lam
    e = jnp.exp(-jnp.abs(z))
    l1p = jnp.where(e < 1e-3, e * (1.0 - e * (0.5 - e * (1.0 / 3.0))), jnp.log(1.0 + e))
    return jnp.maximum(z, 0.0) + l1p


def _shift_down(x, k, row, fill=0.0):
    return jnp.where(row >= k, pltpu.roll(x, k, 0), fill)


def _shift_up(x, k, row, fill=0.0):
    S = x.shape[0]
    return jnp.where(row < S - k, pltpu.roll(x, S - k, 0), fill)


def _lru_gates(xc, row, cw_ref, cb_ref, wa_ref, wx_ref, ba_ref, bx_ref, lam_ref):
    xconv = (cw_ref[3:4, :] * xc + cw_ref[2:3, :] * _shift_down(xc, 1, row) + cw_ref[1:2, :] * _shift_down(xc, 2, row)
             + cw_ref[0:1, :] * _shift_down(xc, 3, row) + cb_ref[...])
    xb = xconv.astype(BF)
    r = _sig(_dot(xb, wa_ref[...]) + ba_ref[...])
    i = _sig(_dot(xb, wx_ref[...]) + bx_ref[...])
    sp = _softplus_neg(lam_ref[...])
    log_a = -LRU_C * r * sp
    a = jnp.exp(log_a)
    mult = jnp.sqrt(-_expm1(2.0 * log_a))
    return xconv, r, i, sp, a, mult


ROWS_PER_TILE = 8


def _steps(a, b, shift, inside, products=True):
    n, k = inside.n, 1
    while k < n:
        b = a * jnp.where(inside(k), shift(b, k), 0.0) + b
        if products or 2 * k < n:
            a = a * jnp.where(inside(k), shift(a, k), 1.0)
        k *= 2
    return a, b


class _Inside:
    def __init__(self, pos, n, reverse):
        self.pos, self.n, self.reverse = pos, n, reverse

    def __call__(self, k):
        return self.pos < self.n - k if self.reverse else self.pos >= k


def _scan_rows(a, b, row, a_ref, b_ref, c_ref, reverse=False):
    S = a.shape[0]
    G = S // ROWS_PER_TILE
    if reverse:
        shift = lambda x, k: pltpu.roll(x, x.shape[0] - k, 0)
    else:
        shift = lambda x, k: pltpu.roll(x, k, 0)
    a, b = _steps(a, b, shift, _Inside(row % ROWS_PER_TILE, ROWS_PER_TILE, reverse))
    a_ref[...] = a
    b_ref[...] = b
    last = 0 if reverse else ROWS_PER_TILE - 1
    grow = lax.broadcasted_iota(jnp.int32, (G, a.shape[1]), 0)
    _, tot = _steps(a_ref[pl.ds(last, G, stride=ROWS_PER_TILE), :], b_ref[pl.ds(last, G, stride=ROWS_PER_TILE), :],
                    shift, _Inside(grow, G, reverse), products=False)
    enters = jnp.where(_Inside(grow, G, reverse)(1), shift(tot, 1), 0.0)
    for r in range(ROWS_PER_TILE):
        c_ref[pl.ds(r, G, stride=ROWS_PER_TILE), :] = enters
    return b + a * c_ref[...]


def _odd_c_fwd(h, cw, cb, wa, wx, ba, bx, lam, name, comm=None):
    S = h.shape[0]

    def body(xc_ref, cg_ref, cw_ref, cb_ref, wa_ref, wx_ref, ba_ref, bx_ref, lam_ref, mix_ref, hst_ref, sa_ref, sb_ref, sc_ref):
        row = lax.broadcasted_iota(jnp.int32, (S, 128), 0)
        xconv, r, i, sp, a, mult = _lru_gates(xc_ref[...], row, cw_ref, cb_ref, wa_ref, wx_ref, ba_ref, bx_ref, lam_ref)
        bb = _scan_rows(a, mult * (i * xconv), row, sa_ref, sb_ref, sc_ref)
        hst_ref[...] = bb
        cg = cg_ref[...]
        mix_ref[...] = (bb * (cg * _sig(cg))).astype(BF)

    col = lambda off: pl.BlockSpec((S, 128), lambda j: (0, off + j))
    vec = pl.BlockSpec((1, 128), lambda j: (0, j))
    mat = pl.BlockSpec((None, 128, 128), lambda j: (j, 0, 0))
    return _pcall(
        body, grid=(8,),
        in_specs=[col(0), col(8), pl.BlockSpec((4, 128), lambda j: (0, j)), vec, mat, mat, vec, vec, vec],
        out_specs=[pl.BlockSpec((None, S, 128), lambda j: (0, 0, j)), pl.BlockSpec((S, 128), lambda j: (0, j))],
        out_shape=[jax.ShapeDtypeStruct((2, S, W), BF), jax.ShapeDtypeStruct((S, W), F32)],
        scratch=[pltpu.VMEM((S, 128), F32)] * 3, name=name, comm=comm,
    )(h, h, cw, cb.reshape(1, W), wa, wx, ba.reshape(1, W), bx.reshape(1, W), lam.reshape(1, W))


def _pool_sums(x, g, row, shift):
    s2 = x + shift(x, 1, row)
    s4 = s2 + shift(s2, 2, row)
    s8 = s4 + shift(s4, 4, row)
    s16 = s8 + shift(s8, 8, row)
    return jnp.where(g == 0, s2, jnp.where(g == 1, s4, jnp.where(g == 2, s8, s16)))


def _odd_d_fwd(h, mix3, wp, dscale, name):
    S = h.shape[0]

    def body(xd_ref, dg_ref, wp_ref, ds_ref, mix_in, mix_ref):
        g = pl.program_id(0)
        row = lax.broadcasted_iota(jnp.int32, (S, 256), 0)
        xd = xd_ref[...]
        cnt = jnp.minimum(row + 1, jnp.left_shift(2, g)).astype(F32)
        pooled = _pool_sums(xd, g, row, _shift_down) / cnt - xd
        mixed = _dot(pooled.astype(BF), wp_ref[...])
        dg = dg_ref[...]
        mix_ref[...] = (mixed * ds_ref[...] * (dg * _sig(dg))).astype(BF)

    col = lambda off: pl.BlockSpec((S, 256), lambda g: (0, off + g))
    return pl.pallas_call(
        body, grid=(4,),
        in_specs=[col(8), col(12), pl.BlockSpec((None, 256, 256), lambda g: (g, 0, 0)),
                  pl.BlockSpec((1, 256), lambda g: (0, g)), ANY],
        out_specs=pl.BlockSpec((None, S, 256), lambda g: (1, 0, g)),
        out_shape=jax.ShapeDtypeStruct((2, S, W), BF), input_output_aliases={4: 0},
        name=name, compiler_params=_cp(),
    )(h, h, wp, dscale.reshape(1, W), mix3)


def _odd_c_bwd(h, hst, dmix3, cw, cb, wa, wx, wat, wxt, ba, bx, lam, name, comm=None):
    S = h.shape[0]

    def body(xc_ref, cg_ref, hst_ref, dc_ref, cw_ref, cb_ref, wa_ref, wx_ref, wat_ref, wxt_ref, ba_ref, bx_ref, lam_ref,
             dh_ref, dcw_ref, dcb_ref, dwa_ref, dwx_ref, dba_ref, dbx_ref, dlam_ref, sa_ref, sb_ref, sc_ref):
        row = lax.broadcasted_iota(jnp.int32, (S, 128), 0)
        xc = xc_ref[...]
        xconv, r, i, sp, a, mult = _lru_gates(xc, row, cw_ref, cb_ref, wa_ref, wx_ref, ba_ref, bx_ref, lam_ref)
        hst = hst_ref[...]
        cg = cg_ref[...]
        sg, dsg = _silu_grad(cg)
        dc = dc_ref[...]
        dh_ref[1] = (dc * hst * dsg).astype(BF)
        lam_t = _scan_rows(_shift_up(a, 1, row), dc * sg, row, sa_ref, sb_ref, sc_ref, reverse=True)
        da = lam_t * _shift_down(hst, 1, row)
        ix = i * xconv
        dmult = lam_t * ix
        di = lam_t * mult * xconv
        dxconv = lam_t * mult * i
        dlog_a = da * a - dmult * (a * a / mult)
        dr = dlog_a * (-LRU_C * sp)
        dsp = jnp.sum(dlog_a * (-LRU_C * r), axis=0, keepdims=True)
        dlam_ref[...] = dsp * (-_sig(-lam_ref[...]))
        dpa = dr * r * (1.0 - r)
        dpx = di * i * (1.0 - i)
        dpab = dpa.astype(BF)
        dpxb = dpx.astype(BF)
        xb = xconv.astype(BF)
        dxconv = dxconv + _dot(dpab, wat_ref[...]) + _dot(dpxb, wxt_ref[...])
        dwa_ref[...] = _dot_tn(xb, dpab)
        dwx_ref[...] = _dot_tn(xb, dpxb)
        dba_ref[...] = jnp.sum(dpa, axis=0, keepdims=True)
        dbx_ref[...] = jnp.sum(dpx, axis=0, keepdims=True)
        dh_ref[0] = (cw_ref[3:4, :] * dxconv + cw_ref[2:3, :] * _shift_up(dxconv, 1, row)
                     + cw_ref[1:2, :] * _shift_up(dxconv, 2, row) + cw_ref[0:1, :] * _shift_up(dxconv, 3, row)).astype(BF)
        for j in range(4):
            src = xc if j == 3 else _shift_down(xc, 3 - j, row)
            dcw_ref[j:j + 1, :] = jnp.sum(dxconv * src, axis=0, keepdims=True)
        dcb_ref[...] = jnp.sum(dxconv, axis=0, keepdims=True)

    col = lambda off: pl.BlockSpec((S, 128), lambda j: (0, off + j))
    vec = pl.BlockSpec((1, 128), lambda j: (0, j))
    mat = pl.BlockSpec((None, 128, 128), lambda j: (j, 0, 0))
    vshape = jax.ShapeDtypeStruct((1, W), F32)
    mshape = jax.ShapeDtypeStruct((8, 128, 128), F32)
    return _pcall(
        body, grid=(8,),
        in_specs=[col(0), col(8), col(0), pl.BlockSpec((None, S, 128), lambda j: (0, 0, j)),
                  pl.BlockSpec((4, 128), lambda j: (0, j)), vec, mat, mat, mat, mat, vec, vec, vec],
        out_specs=[pl.BlockSpec((2, S, 128), lambda j: (0, 0, j)), pl.BlockSpec((4, 128), lambda j: (0, j)), vec,
                   mat, mat, vec, vec, vec],
        out_shape=[jax.ShapeDtypeStruct((4, S, W), BF), jax.ShapeDtypeStruct((4, W), F32), vshape, mshape, mshape,
                   vshape, vshape, vshape],
        scratch=[pltpu.VMEM((S, 128), F32)] * 3, name=name, vmem=56, comm=comm,
    )(h, h, hst, dmix3, cw, cb.reshape(1, W), wa, wx, wat, wxt, ba.reshape(1, W), bx.reshape(1, W), lam.reshape(1, W))


def _odd_d_bwd(h, dmix3, dh4, wp, wpt, dscale, name):
    S = h.shape[0]

    def body(xd_ref, dg_ref, dd_ref, wp_ref, wpt_ref, ds_ref, dh_in, dh_ref, dwp_ref, dds_ref):
        g = pl.program_id(0)
        row = lax.broadcasted_iota(jnp.int32, (S, 256), 0)
        xd = xd_ref[...]
        cnt = jnp.minimum(row + 1, jnp.left_shift(2, g)).astype(F32)
        pooled = _pool_sums(xd, g, row, _shift_down) / cnt - xd
        pb = pooled.astype(BF)
        mixed = _dot(pb, wp_ref[...])
        dg = dg_ref[...]
        sg, dsg = _silu_grad(dg)
        dd = dd_ref[...]
        dmixed = dd * ds_ref[...] * sg
        dds_ref[...] = jnp.sum(dd * mixed * sg, axis=0, keepdims=True)
        dh_ref[1] = (dd * mixed * ds_ref[...] * dsg).astype(BF)
        dmb = dmixed.astype(BF)
        dpooled = _dot(dmb, wpt_ref[...])
        dwp_ref[...] = _dot_tn(pb, dmb)
        dh_ref[0] = (_pool_sums(dpooled / cnt, g, row, _shift_up) - dpooled).astype(BF)

    col = lambda off: pl.BlockSpec((S, 256), lambda g: (0, off + g))
    mat = pl.BlockSpec((None, 256, 256), lambda g: (g, 0, 0))
    vec = pl.BlockSpec((1, 256), lambda g: (0, g))
    return pl.pallas_call(
        body, grid=(4,),
        in_specs=[col(8), col(12), pl.BlockSpec((None, S, 256), lambda g: (1, 0, g)), mat, mat, vec, ANY],
        out_specs=[pl.BlockSpec((2, S, 256), lambda g: (1, 0, g)), mat, vec],
        out_shape=[jax.ShapeDtypeStruct((4, S, W), BF), jax.ShapeDtypeStruct((4, 256, 256), F32),
                   jax.ShapeDtypeStruct((1, W), F32)],
        input_output_aliases={6: 0}, name=name, compiler_params=_cp(56),
    )(h, h, dmix3, wp, wpt, dscale.reshape(1, W), dh4)


def _peer(d):
    x, y, c = lax.axis_index("x"), lax.axis_index("y"), lax.axis_index("c")
    px = 1 - x if d & 4 else x
    py = 1 - y if d & 2 else y
    pc = 1 - c if d & 1 else c
    return (px, py, pc), 4 * px + 2 * py + pc


class _GatherAll(_Comm):
    def __init__(self, xs):
        self.peers = EVERYONE
        self.inputs = [xs]
        self.out_shapes = [jax.ShapeDtypeStruct((N_DEV,) + xs.shape, xs.dtype)]
        self.sem_shapes = [pltpu.SemaphoreType.DMA((N_DEV - 1,)), pltpu.SemaphoreType.DMA((N_DEV - 1,)),
                           pltpu.SemaphoreType.DMA]

    def copies(self, ins, outs, sems):
        (x_ref,), (out_ref,), (send, recv, loc) = ins, outs, sems
        _, me = _peer(0)
        res = [pltpu.make_async_copy(x_ref, out_ref.at[me], loc)]
        for d in range(1, N_DEV):
            peer, _ = _peer(d)
            res.append(pltpu.make_async_remote_copy(src_ref=x_ref, dst_ref=out_ref.at[me], send_sem=send.at[d - 1],
                                                    recv_sem=recv.at[d - 1], device_id=peer, device_id_type=MESH))
        return res


class _ExchangeAll(_Comm):
    def __init__(self, g8):
        self.peers = EVERYONE
        self.inputs = [g8]
        self.out_shapes = [jax.ShapeDtypeStruct(g8.shape, g8.dtype)]
        self.sem_shapes = [pltpu.SemaphoreType.DMA((N_DEV - 1,)), pltpu.SemaphoreType.DMA((N_DEV - 1,)),
                           pltpu.SemaphoreType.DMA]

    def copies(self, ins, outs, sems):
        (g_ref,), (out_ref,), (send, recv, loc) = ins, outs, sems
        _, me = _peer(0)
        res = [pltpu.make_async_copy(g_ref.at[me], out_ref.at[0], loc)]
        for d in range(1, N_DEV):
            peer, pidx = _peer(d)
            res.append(pltpu.make_async_remote_copy(src_ref=g_ref.at[pidx], dst_ref=out_ref.at[d], send_sem=send.at[d - 1],
                                                    recv_sem=recv.at[d - 1], device_id=peer, device_id_type=MESH))
        return res


def _sum8(r8, tr, name):
    _, R, C = r8.shape
    tr = min(tr, R)
    assert R % tr == 0

    def body(r_ref, o_ref):
        acc = r_ref[0]
        for d in range(1, N_DEV):
            acc = acc + r_ref[d]
        o_ref[...] = acc

    return pl.pallas_call(
        body, grid=(R // tr,), in_specs=[pl.BlockSpec((N_DEV, tr, C), lambda i: (0, i, 0))],
        out_specs=pl.BlockSpec((tr, C), lambda i: (i, 0)), out_shape=jax.ShapeDtypeStruct((R, C), F32),
        name=name, compiler_params=_cp(),
    )(r8)


def _adamw_math(w, g, m, v):
    m2 = B1 * m + (1.0 - B1) * g
    v2 = B2 * v + (1.0 - B2) * (g * g)
    m_hat = m2 / (1.0 - B1 ** STEP)
    v_hat = v2 / (1.0 - B2 ** STEP)
    return -LR * (m_hat / (jnp.sqrt(v_hat) + ADAM_EPS) + WD * w), m2, v2


def _adamw_many(ws, gs, ms, vs, name):
    n = len(ws)

    def body(*refs):
        for i in range(n):
            d, m2, v2 = _adamw_math(refs[i][...], refs[n + i][...], refs[2 * n + i][...], refs[3 * n + i][...])
            refs[4 * n + i][...] = d
            refs[5 * n + i][...] = m2
            refs[6 * n + i][...] = v2

    vmem = pl.BlockSpec(memory_space=pltpu.VMEM)
    shapes = [jax.ShapeDtypeStruct(w.shape, F32) for w in ws]
    res = pl.pallas_call(body, in_specs=[vmem] * (4 * n), out_specs=[vmem] * (3 * n), out_shape=shapes * 3, name=name,
                         compiler_params=_cp())(*ws, *gs, *ms, *vs)
    return res[:n], res[n:2 * n], res[2 * n:]


def _adamw(w3, gs, m3, v3, tr, name, comm=None):
    _, R, C = w3.shape

    def body(w_ref, g0_ref, g1_ref, m_ref, v_ref, d_ref, m2_ref, v2_ref, g_ref):
        g = jnp.where(pl.program_id(0) == 0, g0_ref[...], g1_ref[...])
        d_ref[...], m2_ref[...], v2_ref[...] = _adamw_math(w_ref[...], g, m_ref[...], v_ref[...])
        g_ref[...] = g

    blk = pl.BlockSpec((None, tr, C), lambda j, i: (j, i, 0))
    grad = lambda layer: pl.BlockSpec((tr, C), lambda j, i: (jnp.where(j == layer, i, 0), 0))
    shp = jax.ShapeDtypeStruct((2, R, C), F32)
    return _pcall(body, grid=(2, R // tr), in_specs=[blk, grad(0), grad(1), blk, blk], out_specs=[blk] * 4,
                  out_shape=[shp] * 4, name=name, comm=comm)(w3, gs[0], gs[1], m3, v3)


def _rep_pack(a):
    n = a.size
    pad = (-n) % 1024
    f = a.reshape(-1)
    if pad:
        f = jnp.concatenate([f, jnp.zeros((pad,), a.dtype)])
    return f.reshape(N_DEV, -1, 128)


def _rep_unpack(p, shape):
    n = 1
    for s in shape:
        n *= s
    return p.reshape(-1)[:n].reshape(shape)


def _sh_pack(a, axis):
    shp = a.shape
    a = a.reshape(shp[:axis] + (N_DEV, shp[axis] // N_DEV) + shp[axis + 1:])
    return jnp.moveaxis(a, axis, 0).reshape(N_DEV, -1, 128)


def _sh_unpack(p, shape, axis):
    a = p.reshape((N_DEV,) + shape[:axis] + (shape[axis] // N_DEV,) + shape[axis + 1:])
    return jnp.moveaxis(a, 0, axis).reshape(shape)


def _pad_rows(a, mult=8):
    pad = (-a.shape[-2]) % mult
    if pad:
        a = jnp.concatenate([a, jnp.zeros(a.shape[:-2] + (pad, a.shape[-1]), a.dtype)], axis=-2)
    return a


REP = ["even_a_ln_g", "even_a_ln_b", "even_a_ws", "even_a_bs", "even_b_sinks", "even_ln_g", "even_ln_b",
       "odd_w_a", "odd_w_x"]
SH = [("odd_conv_w", (2, 4, W), 2), ("odd_conv_b", (2, W), 1), ("odd_b_a", (2, W), 1), ("odd_b_x", (2, W), 1),
      ("odd_lam", (2, W), 1), ("odd_w_pool", (2, 4, 256, 256), 2), ("odd_d_scale", (2, W), 1),
      ("odd_ln_g", (2, D), 1), ("odd_ln_b", (2, D), 1)]
BIG = ["even_w_in", "even_w_out", "odd_w_in", "odd_w_out"]
NAMES = ["even_w_in", "even_a_ln_g", "even_a_ln_b", "even_a_ws", "even_a_bs", "even_b_sinks", "even_w_out",
         "even_ln_g", "even_ln_b", "odd_w_in", "odd_conv_w", "odd_conv_b", "odd_w_a", "odd_b_a", "odd_w_x", "odd_b_x",
         "odd_lam", "odd_w_pool", "odd_d_scale", "odd_w_out", "odd_ln_g", "odd_ln_b"]


def _rope_table(positions):
    inv = ROPE_THETA ** (-jnp.arange(0, 16, 2, dtype=F32) / 16)
    f = jnp.arange(128) % 64
    ang = positions.astype(F32)[:, None] * inv[f % 8][None, :]
    cos, sin = jnp.cos(ang), jnp.sin(ang)
    return jnp.concatenate([jnp.where(f < 16, cos, 1.0), jnp.where(f < 8, -sin, 0.0),
                            jnp.where((f >= 8) & (f < 16), sin, 0.0)], axis=1)


def kernel(x, positions, even_w_in, even_a_ln_g, even_a_ln_b, even_a_ws, even_a_bs, even_b_sinks, even_w_out, even_ln_g, even_ln_b, odd_w_in, odd_conv_w, odd_conv_b, odd_w_a, odd_b_a, odd_w_x, odd_b_x, odd_lam, odd_w_pool, odd_d_scale, odd_w_out, odd_ln_g, odd_ln_b, loss_target, m_even_w_in, m_even_a_ln_g, m_even_a_ln_b, m_even_a_ws, m_even_a_bs, m_even_b_sinks, m_even_w_out, m_even_ln_g, m_even_ln_b, m_odd_w_in, m_odd_conv_w, m_odd_conv_b, m_odd_w_a, m_odd_b_a, m_odd_w_x, m_odd_b_x, m_odd_lam, m_odd_w_pool, m_odd_d_scale, m_odd_w_out, m_odd_ln_g, m_odd_ln_b, v_even_w_in, v_even_a_ln_g, v_even_a_ln_b, v_even_a_ws, v_even_a_bs, v_even_b_sinks, v_even_w_out, v_even_ln_g, v_even_ln_b, v_odd_w_in, v_odd_conv_w, v_odd_conv_b, v_odd_w_a, v_odd_b_a, v_odd_w_x, v_odd_b_x, v_odd_lam, v_odd_w_pool, v_odd_d_scale, v_odd_w_out, v_odd_ln_g, v_odd_ln_b):
    args = (even_w_in, even_a_ln_g, even_a_ln_b, even_a_ws, even_a_bs, even_b_sinks, even_w_out, even_ln_g, even_ln_b,
            odd_w_in, odd_conv_w, odd_conv_b, odd_w_a, odd_b_a, odd_w_x, odd_b_x, odd_lam, odd_w_pool, odd_d_scale,
            odd_w_out, odd_ln_g, odd_ln_b)
    margs = (m_even_w_in, m_even_a_ln_g, m_even_a_ln_b, m_even_a_ws, m_even_a_bs, m_even_b_sinks, m_even_w_out,
             m_even_ln_g, m_even_ln_b, m_odd_w_in, m_odd_conv_w, m_odd_conv_b, m_odd_w_a, m_odd_b_a, m_odd_w_x,
             m_odd_b_x, m_odd_lam, m_odd_w_pool, m_odd_d_scale, m_odd_w_out, m_odd_ln_g, m_odd_ln_b)
    vargs = (v_even_w_in, v_even_a_ln_g, v_even_a_ln_b, v_even_a_ws, v_even_a_bs, v_even_b_sinks, v_even_w_out,
             v_even_ln_g, v_even_ln_b, v_odd_w_in, v_odd_conv_w, v_odd_conv_b, v_odd_w_a, v_odd_b_a, v_odd_w_x,
             v_odd_b_x, v_odd_lam, v_odd_w_pool, v_odd_d_scale, v_odd_w_out, v_odd_ln_g, v_odd_ln_b)
    wts = dict(zip(NAMES, args))
    mom = dict(zip(NAMES, margs))
    var = dict(zip(NAMES, vargs))
    S = x.shape[1]
    x0 = x[0]
    rope = _rope_table(positions[0])

    kinds = ("even", "odd", "even", "odd")
    blk_in = [jnp.transpose(wts[kinds[l] + "_w_in"][l // 2]).astype(BF) for l in range(4)]
    blk_out = [wts[kinds[l] + "_w_out"][l // 2].astype(BF) for l in range(4)]
    sh_local = _pad_rows(jnp.concatenate([wts[nm].reshape(-1, 128) for nm, _, _ in SH], axis=0), 16)
    me = 4 * lax.axis_index("x") + 2 * lax.axis_index("y") + lax.axis_index("c")
    own_slot = lambda blk: lax.dynamic_update_slice(lax.empty((N_DEV,) + blk.shape, blk.dtype), blk[None], (me, 0, 0))
    reg = {"blk_small": sh_local, "w_small": own_slot(sh_local)}
    sched = _Sched(reg)
    for l in range(4):
        reg[f"blk_in{l}"], reg[f"blk_out{l}"] = blk_in[l], blk_out[l]
        reg[f"w_in{l}"], reg[f"w_out{l}"] = own_slot(blk_in[l]), own_slot(blk_out[l])
    sched.add(_rows("blk_in0", "w_in0", "ag1", blk_in[0].shape[0], ROW_CHUNK[blk_in[0].shape[0]]))
    sched.add(_rows("blk_small", "w_small", "ag1", sh_local.shape[0], sh_local.shape[0]))
    for l in range(4):
        sched.add(_rows(f"blk_out{l}", f"w_out{l}", "ag1", D // N_DEV, ROW_CHUNK[D // N_DEV]))
        if l < 3:
            r = blk_in[l + 1].shape[0]
            sched.add(_rows(f"blk_in{l + 1}", f"w_in{l + 1}", "ag1", r, ROW_CHUNK[r]))

    def gathered(dst, blk):
        sched.flush(dst, FLUSH_EXTRA_US)
        return reg.pop(dst)

    wt_in0 = gathered("w_in0", blk_in[0]).reshape(-1, D)
    full = {nm: wts[nm] for nm in REP}

    def gather_small():
        sh_all = gathered("w_small", sh_local)
        off = 0
        for nm, shape, axis in SH:
            r = wts[nm].size // 128
            full[nm] = _sh_unpack(sh_all[:, off:off + r, :], shape, axis)
            off += r

    saved = []
    wt_in, w_out = [wt_in0, None, None, None], [None] * 4
    xf, xb = x0, x0.astype(BF)
    fwd = lambda name: FWD_OVERBOOK * CARRY_US[name]
    for layer in range(4):
        j = layer // 2
        kind = kinds[layer]
        if wt_in[layer] is None:
            wt_in[layer] = gathered(f"w_in{layer}", blk_in[layer]).reshape(-1, D)
        h = sched.run(_mm_nt, fwd("mm_h_" + kind), xb, wt_in[layer], 1024, 768 if kind == "even" else 512, "mm_h_" + kind)
        if kind == "even":
            bsb = jnp.broadcast_to(full["even_a_bs"][j][:, :, None], (8, 128, 128))
            mix3, o, l = sched.run(_even_fwd, fwd("even_fwd"), h, rope, full["even_a_ln_g"][j], full["even_a_ln_b"][j],
                                   full["even_a_ws"][j], bsb, full["even_b_sinks"][j], "even_fwd")
            extra = (o, l, bsb)
        else:
            if "odd_lam" not in full:
                gather_small()
            wa, wx = full["odd_w_a"][j].astype(BF), full["odd_w_x"][j].astype(BF)
            wp = full["odd_w_pool"][j].astype(BF)
            mix3, hst = sched.run(_odd_c_fwd, fwd("odd_c_fwd"), h, full["odd_conv_w"][j], full["odd_conv_b"][j], wa, wx,
                                  full["odd_b_a"][j], full["odd_b_x"][j], full["odd_lam"][j], "odd_c_fwd")
            mix3 = _odd_d_fwd(h, mix3, wp, full["odd_d_scale"][j], "odd_d_fwd")
            extra = (hst, wa, wx, wp)
        w_out[layer] = gathered(f"w_out{layer}", blk_out[layer]).reshape(D, D)
        z, xn, xnb = sched.run(_mm_out_ln, fwd("mm_out_ln"), mix3, w_out[layer], xf, full[kind + "_ln_g"][j],
                               full[kind + "_ln_b"][j], "mm_out_ln")
        saved.append((xb, h, mix3, z, extra))
        xf, xb = xn, xnb

    dxn = xf

    gsum = {nm: [None, None] for nm in NAMES}

    chip_sums = {}
    sched.overhang = 0.15

    waiting = []

    def chip_sum(g, tag, key):
        r = g.shape[0] // N_DEV
        reg["g_" + key] = g.reshape(N_DEV, r, D)
        sched.add(_rows("g_" + key, "d_" + key, "rsd", r, r), first=True)
        waiting.append((key, tag))

    def add_arrived():
        for key, tag in list(waiting):
            if "d_" + key in reg and not sched.pending("d_" + key):
                waiting.remove((key, tag))
                g8 = reg.pop("g_" + key)
                chip_sums[key] = reg["s_" + key] = _add_pairs(g8, reg.pop("d_" + key), "rs_add_" + tag)
                sched.add(_rows("s_" + key, "r_" + key, "rs", g8.shape[1], ROW_CHUNK[g8.shape[1]] // 2))

    sched.after_landing = add_arrived

    def reduced(key, name):
        sched.flush("d_" + key, FLUSH_EXTRA_US)
        sched.flush("r_" + key, FLUSH_EXTRA_US)
        return _rs_final(chip_sums[key], reg.pop("r_" + key), name)

    for layer in (3, 2, 1, 0):
        j = layer // 2
        xb, h, mix3, z, extra = saved[layer]
        kind = kinds[layer]
        if layer == 3:
            dz, dzb, dg, dbeta, part = sched.run(_ln_bwd, CARRY_US["ln_bwd"], dxn, z, full[kind + "_ln_g"][j], "loss_ln_bwd",
                                                 target=loss_target[0])
            loss = lax.psum(part[0, 0] * (0.5 / D), ("x", "y", "c"))
        else:
            dz, dzb, dg, dbeta = sched.run(_ln_bwd, CARRY_US["ln_bwd"], dxn, z, full[kind + "_ln_g"][j], "ln_bwd")
        gsum[kind + "_ln_g"][j] = dg.reshape(D)
        gsum[kind + "_ln_b"][j] = dbeta.reshape(D)
        chip_sum(sched.run(_mm_tn, CARRY_US["mm_dw_out"], mix3, dzb, 512, "mm_dw_out"), "w_out", f"out{layer}")
        dmix3 = sched.run(_mm_nt, CARRY_US["mm_dmix"], dzb, w_out[layer], 1024, 512, "mm_dmix", out3=True)
        if kind == "even":
            o, l, bsb = extra
            ws = full["even_a_ws"][j]
            dh, dws, dbs, dlng, dlnb, dsink = sched.run(
                _even_bwd, CARRY_US["even_bwd"], h, dmix3, o, l, rope, full["even_a_ln_g"][j], full["even_a_ln_b"][j],
                ws, jnp.swapaxes(ws, 1, 2), bsb, full["even_b_sinks"][j], "even_bwd")
            gsum["even_a_ws"][j] = dws
            gsum["even_a_bs"][j] = jnp.transpose(dbs[:, :8])
            gsum["even_a_ln_g"][j] = dlng.reshape(W)
            gsum["even_a_ln_b"][j] = dlnb.reshape(W)
            gsum["even_b_sinks"][j] = dsink[0, :16]
            if layer == 0:
                rep_rows = [_rep_pack(jnp.stack(gsum[nm]).reshape(wts[nm].shape)) for nm in REP]
                sh_rows = [_sh_pack(jnp.stack(gsum[nm]).reshape(shape), axis) for nm, shape, axis in SH]
                packed = _pad_rows(jnp.concatenate(rep_rows + sh_rows, axis=1))
                gw, (small8,) = _mm_tn(dh, xb, 384, "mm_dw_in_even", comm=_ExchangeAll(packed))
            else:
                gw = sched.run(_mm_tn, CARRY_US["mm_dw_in_even"], dh, xb, 384, "mm_dw_in_even")
            chip_sum(gw, "w_in_even", f"in{layer}")
            if layer == 0:
                n_rep = sum(p.shape[1] for p in rep_rows)
                red = _sum8(small8, 1 << 20, "sum_small")
                (rep_all,) = sched.flush("d_in0", FLUSH_EXTRA_US, beside=_GatherAll(_pad_rows(red[:n_rep])))
                sched.overhang = 0.6
            dxn = sched.run(_mm_nn_res, CARRY_US["mm_dx_even"], dh, wt_in[layer], dz, 512, 512, "mm_dx_even")
        else:
            hst, wa, wx, wp = extra
            dh4, dcw, dcb, dwa, dwx, dba, dbx, dlam = sched.run(
                _odd_c_bwd, CARRY_US["odd_c_bwd"], h, hst, dmix3, full["odd_conv_w"][j], full["odd_conv_b"][j], wa, wx,
                jnp.swapaxes(wa, 1, 2), jnp.swapaxes(wx, 1, 2), full["odd_b_a"][j], full["odd_b_x"][j], full["odd_lam"][j],
                "odd_c_bwd")
            dh4, dwp, dds = _odd_d_bwd(h, dmix3, dh4, wp, jnp.swapaxes(wp, 1, 2), full["odd_d_scale"][j], "odd_d_bwd")
            gsum["odd_conv_w"][j], gsum["odd_conv_b"][j] = dcw, dcb.reshape(W)
            gsum["odd_w_a"][j], gsum["odd_w_x"][j] = dwa, dwx
            gsum["odd_b_a"][j], gsum["odd_b_x"][j], gsum["odd_lam"][j] = dba.reshape(W), dbx.reshape(W), dlam.reshape(W)
            gsum["odd_w_pool"][j], gsum["odd_d_scale"][j] = dwp, dds.reshape(W)
            chip_sum(sched.run(_mm_tn, CARRY_US["mm_dw_in_odd"], dh4, xb, 512, "mm_dw_in_odd"), "w_in_odd", f"in{layer}")
            dxn = sched.run(_mm_nn_res, CARRY_US["mm_dx_odd"], dh4, wt_in[layer], dz, 512, 512, "mm_dx_odd")
    grad_x = dxn[None]

    out_g, out_d, out_m, out_v = {}, {}, {}, {}
    for nm, kind, what, layers in (("odd_w_out", "odd", "out", (1, 3)), ("even_w_out", "even", "out", (0, 2)),
                                   ("odd_w_in", "odd", "in", (1, 3)), ("even_w_in", "even", "in", (0, 2))):
        gl = [reduced(f"{what}{l}", f"rs_final_w_{what}_{kind}") for l in layers]
        if nm == "even_w_in":
            view = lambda a: jnp.transpose(a, (0, 2, 1))
            res, _ = _adamw(view(wts[nm]), gl, view(mom[nm]), view(var[nm]), 168, f"adamw_{nm}")
            res = [view(a) for a in res]
        elif what == "in":
            res, _ = _adamw(wts[nm], [jnp.transpose(a) for a in gl], mom[nm], var[nm], 512, f"adamw_{nm}")
        else:
            res = sched.run(_adamw, CARRY_US["adamw_" + nm], wts[nm], gl, mom[nm], var[nm], 128, f"adamw_{nm}")
        out_d[nm], out_m[nm], out_v[nm], out_g[nm] = res

    g_small = {}
    off = 0
    for nm, p in zip(REP, rep_rows):
        r = p.shape[1]
        g_small[nm] = _rep_unpack(rep_all[:, off:off + r, :], wts[nm].shape)
        off += r
    off = n_rep
    for (nm, shape, axis), p in zip(SH, sh_rows):
        r = p.shape[1]
        g_small[nm] = red[off:off + r].reshape(wts[nm].shape)
        off += r

    def rows(a):
        f = a.reshape(-1)
        pad = (-f.shape[0]) % 128
        if pad:
            f = jnp.concatenate([f, jnp.zeros((pad,), a.dtype)])
        return f.reshape(-1, 128)

    small = REP + [nm for nm, _, _ in SH]
    each = lambda src: [rows(src[nm]) for nm in small]
    d2, m2, v2 = _adamw_many(each(wts), each(g_small), each(mom), each(var), "adamw_small")
    for i, nm in enumerate(small):
        n, shp = wts[nm].size, wts[nm].shape
        take = lambda a: a.reshape(-1)[:n].reshape(shp)
        out_g[nm], out_d[nm], out_m[nm], out_v[nm] = g_small[nm], take(d2[i]), take(m2[i]), take(v2[i])

    return (loss, grad_x, *[out_g[nm] for nm in NAMES], *[out_d[nm] for nm in NAMES],
            *[out_m[nm] for nm in NAMES], *[out_v[nm] for nm in NAMES])
```

```python
import functools

import jax
import jax.numpy as jnp
from jax import lax
from jax.experimental import pallas as pl
from jax.experimental.pallas import tpu as pltpu

F32 = jnp.float32
BF = jnp.bfloat16
MESH = pl.DeviceIdType.MESH
ANY = pl.BlockSpec(memory_space=pl.ANY)

N_DEV = 8
D = 2048
W = 1024
EVEN_IN = 5376
ODD_IN = 4096
CHUNK = 128
ALPHA = (2 * 4) ** 0.25
LN_EPS = 1e-5
ROPE_THETA = 500000.0
LRU_C = 8.0
LR, B1, B2, ADAM_EPS, WD, STEP = 0.001, 0.9, 0.999, 1e-08, 0.01, 10
NEG = -1e30
HEAD_COLS = 4


def _cp(vmem_mb=48, collective_id=None):
    return pltpu.CompilerParams(vmem_limit_bytes=vmem_mb * 1024 * 1024, collective_id=collective_id)


def _sig(x):
    return jax.nn.sigmoid(x)


def _silu_grad(x):
    s = _sig(x)
    return x * s, s * (1.0 + x * (1.0 - s))


def _dot(a, b):
    return jnp.dot(a, b, preferred_element_type=F32)


def _dot_nt(a, b):
    return lax.dot_general(a, b, (((1,), (1,)), ((), ())), preferred_element_type=F32)


def _dot_tn(a, b):
    return lax.dot_general(a, b, (((0,), (0,)), ((), ())), preferred_element_type=F32)


def _coords():
    return lax.axis_index("x"), lax.axis_index("y"), lax.axis_index("c")


def _chip(j):
    x, y, _ = _coords()
    return (1 - x if j & 2 else x), (1 - y if j & 1 else y)


X_NB, Y_NB, DIAG, SIB = 4, 2, 6, 1
EVERYONE = frozenset(range(1, N_DEV))
BARRIER_IDS = {}


class _Comm:
    def collective_id(self):
        return BARRIER_IDS.setdefault(frozenset(self.peers), len(BARRIER_IDS))

    def start(self, ins, outs, sems):
        barrier = pltpu.get_barrier_semaphore()
        for d in sorted(self.peers):
            pl.semaphore_signal(barrier, inc=1, device_id=_peer(d)[0], device_id_type=MESH)
        pl.semaphore_wait(barrier, len(self.peers))
        for cp in self.copies(ins, outs, sems):
            cp.start()

    def wait(self, ins, outs, sems):
        for cp in self.copies(ins, outs, sems):
            cp.wait()


class _Join(_Comm):
    def __init__(self, parts):
        self.parts = list(parts)
        self.peers = frozenset().union(*[p.peers for p in self.parts])
        self.inputs = [a for p in self.parts for a in p.inputs]
        self.out_shapes = [s for p in self.parts for s in p.out_shapes]
        self.sem_shapes = [s for p in self.parts for s in p.sem_shapes]
        self.aliases = {}
        i0 = o0 = 0
        for p in self.parts:
            for i, o in getattr(p, "aliases", {}).items():
                self.aliases[i0 + i] = o0 + o
            i0, o0 = i0 + len(p.inputs), o0 + len(p.out_shapes)

    def copies(self, ins, outs, sems):
        res = []
        i0 = o0 = s0 = 0
        for p in self.parts:
            ni, no, ns = len(p.inputs), len(p.out_shapes), len(p.sem_shapes)
            res += p.copies(ins[i0:i0 + ni], outs[o0:o0 + no], sems[s0:s0 + ns])
            i0, o0, s0 = i0 + ni, o0 + no, s0 + ns
        return res


ROWS_US = {"ag1": 0.104, "ag2": 0.052, "agd": 0.027, "rsd": 0.027, "rs": 0.205}
N_COPIES = {"ag1": 2, "ag2": 2, "agd": 4, "rsd": 4, "rs": 3}
TASK_PEERS = {"ag1": {X_NB, Y_NB}, "ag2": {X_NB, Y_NB}, "agd": {SIB}, "rsd": {SIB}, "rs": {X_NB, Y_NB, DIAG}}
ROW_CHUNK = {672: 224, 512: 128, 256: 128}
CARRY_US = {"mm_h_even": 58, "mm_h_odd": 47, "even_fwd": 42, "odd_c_fwd": 37, "mm_out_ln": 33, "ln_bwd": 23, "mm_dmix": 26,
            "mm_dw_out": 25, "even_bwd": 95, "odd_c_bwd": 70, "mm_dw_in_even": 56, "mm_dw_in_odd": 44, "mm_dx_even": 60,
            "mm_dx_odd": 50, "adamw_even_w_in": 30, "adamw_odd_w_in": 28, "adamw_even_w_out": 11, "adamw_odd_w_out": 11}
FWD_OVERBOOK = 1.15
FLUSH_EXTRA_US = 60.0


def _cost_us(task, reg):
    kind, src, _, lo, hi = task
    return ROWS_US[kind] * (hi - lo) * reg[src].shape[-1] * reg[src].dtype.itemsize / 4096.0


class _Copies(_Comm):
    def __init__(self, tasks, reg):
        self.tasks = list(tasks)
        self.out_names, self.in_names = [], []
        for kind, src, dst, lo, hi in self.tasks:
            if dst not in self.out_names:
                self.out_names.append(dst)
        for kind, src, dst, lo, hi in self.tasks:
            if src not in self.out_names and src not in self.in_names:
                self.in_names.append(src)
        self.out_shapes, self.aliases = [], {}
        for o, dst in enumerate(self.out_names):
            if dst in reg:
                self.aliases[len(self.in_names)] = o
                self.in_names.append(dst)
                self.out_shapes.append(jax.ShapeDtypeStruct(reg[dst].shape, reg[dst].dtype))
            else:
                kind, src = next((t[0], t[1]) for t in self.tasks if t[2] == dst)
                shape = ({"rsd": 4, "rs": 3}[kind],) + reg[src].shape[1:]
                self.out_shapes.append(jax.ShapeDtypeStruct(shape, reg[src].dtype))
        self.inputs = [reg[nm] for nm in self.in_names]
        n = sum(N_COPIES[t[0]] for t in self.tasks)
        self.sem_shapes = [pltpu.SemaphoreType.DMA((n,)), pltpu.SemaphoreType.DMA((n,))]
        self.peers = frozenset().union(*[TASK_PEERS[t[0]] for t in self.tasks])

    def copies(self, ins, outs, sems):
        send, recv = sems
        x, y, c = _coords()
        me = 4 * x + 2 * y + c
        xn, yn = (1 - x, y, c), (x, 1 - y, c)
        at_xn, at_yn = 4 * (1 - x) + 2 * y + c, 4 * x + 2 * (1 - y) + c
        ref = dict(zip(self.in_names, ins))
        ref.update(zip(self.out_names, outs))
        res = []

        def copy(src, dst, to):
            i = len(res)
            res.append(pltpu.make_async_remote_copy(src_ref=src, dst_ref=dst, send_sem=send.at[i], recv_sem=recv.at[i],
                                                    device_id=to, device_id_type=MESH))

        for kind, src, dst, lo, hi in self.tasks:
            n = hi - lo
            if kind == "ag1":
                for to in (xn, yn):
                    copy(ref[src].at[pl.ds(lo, n)], ref[dst].at[me, pl.ds(lo, n)], to)
            elif kind == "ag2":
                h = n // 2
                first, second = ref[dst].at[at_xn, pl.ds(lo, h)], ref[dst].at[at_yn, pl.ds(lo + h, n - h)]
                copy(first, first, yn)
                copy(second, second, xn)
            elif kind == "agd":
                for j in range(4):
                    px, py = _chip(j)
                    rows = ref[dst].at[4 * px + 2 * py + c, pl.ds(lo, n)]
                    copy(rows, rows, (x, y, 1 - c))
            elif kind == "rsd":
                for j in range(4):
                    px, py = _chip(j)
                    copy(ref[src].at[4 * px + 2 * py + 1 - c, pl.ds(lo, n)], ref[dst].at[j, pl.ds(lo, n)], (x, y, 1 - c))
            else:
                for j in (1, 2, 3):
                    px, py = _chip(j)
                    copy(ref[src].at[j, pl.ds(lo, n)], ref[dst].at[j - 1, pl.ds(lo, n)], (px, py, c))
        return res


class _Sched:
    def __init__(self, reg):
        self.reg, self.queue, self.later = reg, [], []
        self.overhang = 0.5
        self.after_landing = None

    def add(self, tasks, first=False):
        self.queue = list(tasks) + self.queue if first else self.queue + list(tasks)

    def pending(self, dst):
        return any(t[2] == dst for t in self.queue + self.later)

    def take(self, budget_us, must=None, overhang=0.5):
        self.queue, self.later = self.later + self.queue, []
        picked, us = [], 0.0
        rest = []
        for t in self.queue:
            cost = _cost_us(t, self.reg)
            if (must is not None and t[2] == must) or us + (1.0 - overhang) * cost <= budget_us:
                picked.append(t)
                us += cost
                if t[0] in ("ag1", "ag2"):
                    self.later.append(({"ag1": "ag2", "ag2": "agd"}[t[0]], t[2], t[2], t[3], t[4]))
            else:
                rest.append(t)
        self.queue = rest
        return _Copies(picked, self.reg) if picked else None

    def landed(self, comm, got):
        if comm is not None:
            for nm, a in zip(comm.out_names, got):
                self.reg[nm] = a
        if self.after_landing is not None:
            self.after_landing()

    def run(self, builder, budget_us, *args, **kw):
        comm = self.take(budget_us, overhang=self.overhang)
        res, got = builder(*args, comm=comm, **kw)
        self.landed(comm, got)
        return res

    def flush(self, dst, budget_us=0.0, beside=None):
        res = []
        while self.pending(dst):
            comm = self.take(budget_us, must=dst)
            got = _comm_only(comm if beside is None else _Join([comm, beside]), "flush_" + dst)
            res, beside = got[len(comm.out_shapes):], None
            self.landed(comm, got[:len(comm.out_shapes)])
        return res


def _rows(name_src, name_dst, kind, n_rows, chunk):
    return [(kind, name_src, name_dst, lo, min(lo + chunk, n_rows)) for lo in range(0, n_rows, chunk)]


def _pcall(body, *, grid, in_specs, out_specs, out_shape, name, scratch=(), vmem=48, comm=None):
    in_specs, out_specs, out_shape, scratch = list(in_specs), list(out_specs), list(out_shape), list(scratch)
    if comm is None:
        call = pl.pallas_call(body, grid=grid, in_specs=in_specs, out_specs=out_specs, out_shape=out_shape,
                              scratch_shapes=scratch, name=name, compiler_params=_cp(vmem))
        return lambda *args: (call(*args), [])
    n_in, n_out, n_scr = len(in_specs), len(out_specs), len(scratch)
    c_in, c_out = len(comm.inputs), len(comm.out_shapes)
    aliases = {n_in + i: n_out + o for i, o in getattr(comm, "aliases", {}).items()}

    def wrapped(*refs):
        ins, cins = refs[:n_in], refs[n_in:n_in + c_in]
        o0 = n_in + c_in
        outs, couts = refs[o0:o0 + n_out], refs[o0 + n_out:o0 + n_out + c_out]
        s0 = o0 + n_out + c_out
        scr, sems = refs[s0:s0 + n_scr], refs[s0 + n_scr:]
        ids = [pl.program_id(a) for a in range(len(grid))]
        first = functools.reduce(jnp.logical_and, [i == 0 for i in ids])
        last = functools.reduce(jnp.logical_and, [i == g - 1 for i, g in zip(ids, grid)])

        @pl.when(first)
        def _():
            comm.start(cins, couts, sems)

        body(*ins, *outs, *scr)

        @pl.when(last)
        def _():
            comm.wait(cins, couts, sems)

    call = pl.pallas_call(wrapped, grid=grid, in_specs=in_specs + [ANY] * c_in, out_specs=out_specs + [ANY] * c_out,
                          out_shape=out_shape + list(comm.out_shapes), scratch_shapes=scratch + list(comm.sem_shapes),
                          input_output_aliases=aliases, name=name, compiler_params=_cp(vmem, comm.collective_id()))

    def run(*args):
        res = call(*args, *comm.inputs)
        return res[:n_out], res[n_out:]

    return run


def _comm_only(comm, name):
    c_in, c_out = len(comm.inputs), len(comm.out_shapes)

    def body(*refs):
        cins, couts, sems = refs[:c_in], refs[c_in:c_in + c_out], refs[c_in + c_out:]
        comm.start(cins, couts, sems)
        comm.wait(cins, couts, sems)

    return pl.pallas_call(body, in_specs=[ANY] * c_in, out_specs=[ANY] * c_out, out_shape=list(comm.out_shapes),
                          scratch_shapes=list(comm.sem_shapes), input_output_aliases=dict(getattr(comm, "aliases", {})),
                          name=name, compiler_params=pltpu.CompilerParams(collective_id=comm.collective_id()))(*comm.inputs)


def _chip_blocks():
    _, _, c = _coords()
    return jnp.stack([4 * px + 2 * py + c for px, py in map(_chip, range(4))]).astype(jnp.int32)


def _add_pairs(g8, b4, name):
    _, R, C = b4.shape

    def body(idx_ref, a_ref, b_ref, o_ref):
        o_ref[...] = (a_ref[...].astype(F32) + b_ref[...].astype(F32)).astype(BF)

    blk = pl.BlockSpec((None, R, C), lambda j, idx: (j, 0, 0))
    grid_spec = pltpu.PrefetchScalarGridSpec(
        num_scalar_prefetch=1, grid=(4,),
        in_specs=[pl.BlockSpec((None, R, C), lambda j, idx: (idx[j], 0, 0)), blk], out_specs=blk)
    return pl.pallas_call(body, grid_spec=grid_spec, out_shape=jax.ShapeDtypeStruct(b4.shape, BF), name=name,
                          compiler_params=_cp())(_chip_blocks(), g8, b4)


def _rs_final(s4, r3, name):
    _, R, C = s4.shape
    tr = R // 2

    def body(s_ref, r_ref, o_ref):
        o_ref[...] = ((s_ref[...].astype(F32) + r_ref[0].astype(F32)) + r_ref[1].astype(F32)) + r_ref[2].astype(F32)

    return pl.pallas_call(
        body, grid=(2,),
        in_specs=[pl.BlockSpec((None, tr, C), lambda i: (0, i, 0)), pl.BlockSpec((3, tr, C), lambda i: (0, i, 0))],
        out_specs=pl.BlockSpec((tr, C), lambda i: (i, 0)), out_shape=jax.ShapeDtypeStruct((R, C), F32),
        name=name, compiler_params=_cp())(s4, r3)


def _mm_nt(a, w, tm, tn, name, out3=False, comm=None):
    M, K = a.shape
    N = w.shape[0]
    tm = min(tm, M)

    def body(a_ref, w_ref, o_ref):
        o_ref[...] = _dot_nt(a_ref[...], w_ref[...])

    if out3:
        per = W // tn
        out_shape = jax.ShapeDtypeStruct((N // W, M, W), F32)
        out_spec = pl.BlockSpec((None, tm, tn), lambda i, j: (j // per, i, j % per))
    else:
        out_shape = jax.ShapeDtypeStruct((M, N), F32)
        out_spec = pl.BlockSpec((tm, tn), lambda i, j: (i, j))
    (res,), extra = _pcall(
        body, grid=(M // tm, N // tn),
        in_specs=[pl.BlockSpec((tm, K), lambda i, j: (i, 0)), pl.BlockSpec((tn, K), lambda i, j: (j, 0))],
        out_specs=[out_spec], out_shape=[out_shape], name=name, comm=comm)(a, w)
    return res, extra


def _mm_tn(a, b, tm, name, comm=None):
    K, N = b.shape
    if a.ndim == 3:
        M = a.shape[0] * W
        per = W // tm
        a_spec = pl.BlockSpec((None, K, tm), lambda i: (i // per, 0, i % per))
    else:
        M = a.shape[1]
        a_spec = pl.BlockSpec((K, tm), lambda i: (0, i))

    def body(a_ref, b_ref, o_ref):
        o_ref[...] = _dot_tn(a_ref[...], b_ref[...]).astype(BF)

    (out,), extra = _pcall(
        body, grid=(M // tm,),
        in_specs=[a_spec, pl.BlockSpec((K, N), lambda i: (0, 0))],
        out_specs=[pl.BlockSpec((tm, N), lambda i: (i, 0))],
        out_shape=[jax.ShapeDtypeStruct((M, N), BF)], name=name, vmem=56, comm=comm)(a, b)
    return out, extra


def _mm_nn_res(a, w, res, tm, tn, name, comm=None):
    K, N = w.shape
    if a.ndim == 3:
        P, M = a.shape[0], a.shape[1]
        tm = min(tm, M)
        a_spec = pl.BlockSpec((P, tm, W), lambda j, i: (0, i, 0))
    else:
        P, M = 0, a.shape[0]
        tm = min(tm, M)
        a_spec = pl.BlockSpec((tm, K), lambda j, i: (i, 0))

    def body(a_ref, w_ref, r_ref, o_ref):
        if P:
            d = _dot(a_ref[0], w_ref[0:W, :])
            for p in range(1, P):
                d = d + _dot(a_ref[p], w_ref[p * W:(p + 1) * W, :])
        else:
            d = _dot(a_ref[...], w_ref[...])
        o_ref[...] = ALPHA * r_ref[...] + d

    (out,), extra = _pcall(
        body, grid=(N // tn, M // tm),
        in_specs=[a_spec, pl.BlockSpec((K, tn), lambda j, i: (0, j)), pl.BlockSpec((tm, tn), lambda j, i: (i, j))],
        out_specs=[pl.BlockSpec((tm, tn), lambda j, i: (i, j))],
        out_shape=[jax.ShapeDtypeStruct((M, N), F32)], name=name, comm=comm)(a, w, res)
    return out, extra


def _mm_out_ln(mix3, w_out, x, g, b, name, comm=None):
    S = x.shape[0]
    tm = min(256, S)

    def body(m_ref, w_ref, x_ref, g_ref, b_ref, z_ref, xn_ref, xb_ref):
        acc = _dot(m_ref[0], w_ref[0:W, :]) + _dot(m_ref[1], w_ref[W:2 * W, :])
        z = ALPHA * x_ref[...] + acc
        mu = jnp.mean(z, axis=1, keepdims=True)
        zc = z - mu
        var = jnp.mean(zc * zc, axis=1, keepdims=True)
        xn = zc * lax.rsqrt(var + LN_EPS) * g_ref[...] + b_ref[...]
        z_ref[...] = z
        xn_ref[...] = xn
        xb_ref[...] = xn.astype(BF)

    row = pl.BlockSpec((tm, D), lambda i: (i, 0))
    vec = pl.BlockSpec((1, D), lambda i: (0, 0))
    return _pcall(
        body, grid=(S // tm,),
        in_specs=[pl.BlockSpec((2, tm, W), lambda i: (0, i, 0)), pl.BlockSpec((D, D), lambda i: (0, 0)), row, vec, vec],
        out_specs=[row, row, row],
        out_shape=[jax.ShapeDtypeStruct((S, D), F32), jax.ShapeDtypeStruct((S, D), F32), jax.ShapeDtypeStruct((S, D), BF)],
        name=name, comm=comm)(mix3, w_out, x, g.reshape(1, D), b.reshape(1, D))


def _ln_bwd(dxn, z, g, name, comm=None, target=None):
    S = z.shape[0]
    tm = min(256, S)
    head = target is not None

    def body(*refs):
        if head:
            d_ref, t_ref, z_ref, g_ref, dz_ref, dzb_ref, dg_ref, db_ref, p_ref = refs
        else:
            d_ref, z_ref, g_ref, dz_ref, dzb_ref, dg_ref, db_ref = refs
        i = pl.program_id(0)
        zz = z_ref[...]
        mu = jnp.mean(zz, axis=1, keepdims=True)
        zc = zz - mu
        var = jnp.mean(zc * zc, axis=1, keepdims=True)
        rstd = lax.rsqrt(var + LN_EPS)
        xhat = zc * rstd
        dy = d_ref[...]
        if head:
            e = dy - t_ref[...]
            dy = e * (1.0 / D)

            @pl.when(i == 0)
            def _():
                p_ref[...] = jnp.zeros_like(p_ref)

            p_ref[...] += jnp.sum(jnp.sum(e * e, axis=1, keepdims=True), axis=0, keepdims=True)
        dyg = dy * g_ref[...]
        m1 = jnp.mean(dyg, axis=1, keepdims=True)
        m2 = jnp.mean(dyg * xhat, axis=1, keepdims=True)
        dz = rstd * (dyg - m1 - xhat * m2)
        dz_ref[...] = dz
        dzb_ref[...] = dz.astype(BF)

        @pl.when(i == 0)
        def _():
            dg_ref[...] = jnp.zeros_like(dg_ref)
            db_ref[...] = jnp.zeros_like(db_ref)

        dg_ref[...] += jnp.sum(dy * xhat, axis=0, keepdims=True)
        db_ref[...] += jnp.sum(dy, axis=0, keepdims=True)

    row = pl.BlockSpec((tm, D), lambda i: (i, 0))
    vec = pl.BlockSpec((1, D), lambda i: (0, 0))
    out_specs = [row, row, vec, vec] + ([pl.BlockSpec((8, 128), lambda i: (0, 0))] if head else [])
    out_shape = [jax.ShapeDtypeStruct((S, D), F32), jax.ShapeDtypeStruct((S, D), BF), jax.ShapeDtypeStruct((1, D), F32),
                 jax.ShapeDtypeStruct((1, D), F32)] + ([jax.ShapeDtypeStruct((8, 128), F32)] if head else [])
    operands = (dxn, target, z, g.reshape(1, D)) if head else (dxn, z, g.reshape(1, D))
    return _pcall(body, grid=(S // tm,), in_specs=[row] * (len(operands) - 1) + [vec], out_specs=out_specs,
                  out_shape=out_shape, name=name, comm=comm)(*operands)


def _rope_fwd(t, r_ref):
    return (t * r_ref[:, 0:128] + pltpu.roll(t, 120, 1) * r_ref[:, 128:256]
            + pltpu.roll(t, 8, 1) * r_ref[:, 256:384])


def _rope_bwd(g, r_ref):
    return (g * r_ref[:, 0:128] + pltpu.roll(g * r_ref[:, 128:256], 8, 1)
            + pltpu.roll(g * r_ref[:, 256:384], 120, 1))


def _dup_heads(kb):
    lo = lax.broadcasted_iota(jnp.int32, kb.shape, 1) < 64
    sw = pltpu.roll(kb, 64, 1)
    return [jnp.where(lo, kb, sw).astype(BF), jnp.where(lo, sw, kb).astype(BF)]


def _even_fwd(h, rope, lng, lnb, ws, bsb, sinks, name, comm=None):
    S = h.shape[0]
    nb = S // CHUNK

    def body(h_ref, hp_ref, rc_ref, rp_ref, lng_ref, lnb_ref, ws_ref, bsb_ref, sink_ref, mix_ref, o_ref, l_ref):
        n = pl.program_id(0)
        lane = lax.broadcasted_iota(jnp.int32, (128, 128), 1)
        rowi = lax.broadcasted_iota(jnp.int32, (128, 128), 0)
        tri = rowi >= lane
        lane_lo = lane < 64
        v = h_ref[:, W:2 * W]
        mu = jnp.mean(v, axis=1, keepdims=True)
        vc = v - mu
        var = jnp.mean(vc * vc, axis=1, keepdims=True)
        vn = vc * lax.rsqrt(var + LN_EPS) * lng_ref[...] + lnb_ref[...]
        ms = [_dot(jnp.where(tri, ws_ref[g], 0.0).astype(BF), vn[:, g * 128:(g + 1) * 128].astype(BF)) for g in range(8)]
        for g in range(8):
            sl = slice(g * 128, (g + 1) * 128)
            ag = h_ref[:, 2 * W + g * 128:2 * W + (g + 1) * 128]
            mix_ref[0, :, sl] = (h_ref[:, sl] * (ms[g] + bsb_ref[g]) * (ag * _sig(ag))).astype(BF)
        kb = jnp.concatenate([_rope_fwd(hp_ref[:, 0:128], rp_ref), _rope_fwd(h_ref[:, 4096:4224], rc_ref)], axis=0)
        vb = jnp.concatenate([hp_ref[:, 128:256], h_ref[:, 4224:4352]], axis=0)
        k2 = _dup_heads(kb)
        v2 = _dup_heads(vb)
        qi = lax.broadcasted_iota(jnp.int32, (128, 256), 0)
        kj = lax.broadcasted_iota(jnp.int32, (128, 256), 1)
        diff = qi + 128 - kj
        valid = (diff >= 0) & (diff < 128) & ((n > 0) | (kj >= 128))
        lacc = jnp.zeros((128, 128), F32)
        for j0 in range(0, 8, HEAD_COLS):
            heads = [(j, half) for j in range(j0, j0 + HEAD_COLS) for half in range(2)]
            sc, pr, oh = {}, {}, {}
            for j in range(j0, j0 + HEAD_COLS):
                qc = _rope_fwd(h_ref[:, 3072 + j * 128:3072 + (j + 1) * 128], rc_ref)
                sc[j, 0] = _dot_nt(jnp.where(lane_lo, qc, 0.0).astype(BF), k2[j // 4])
                sc[j, 1] = _dot_nt(jnp.where(lane_lo, 0.0, qc).astype(BF), k2[j // 4])
            for j, half in heads:
                hq = 2 * j + half
                s = jnp.where(valid, sc[j, half] * 0.125, NEG)
                sk = sink_ref[hq]
                mx = jnp.maximum(jnp.max(s, axis=1, keepdims=True), sk)
                p = jnp.exp(s - mx)
                den = jnp.sum(p, axis=1, keepdims=True) + jnp.exp(sk - mx)
                pr[j, half] = (p / den).astype(BF)
                lacc = jnp.where(lane == hq, mx + jnp.log(den), lacc)
            for j, half in heads:
                oh[j, half] = _dot(pr[j, half], v2[j // 4])
            for j in range(j0, j0 + HEAD_COLS):
                cs = slice(j * 128, (j + 1) * 128)
                ocol = jnp.where(lane_lo, oh[j, 0], oh[j, 1])
                bg = h_ref[:, 4352 + j * 128:4352 + (j + 1) * 128]
                o_ref[:, cs] = ocol
                mix_ref[1, :, cs] = (ocol * (bg * _sig(bg))).astype(BF)
        l_ref[...] = lacc

    prev = lambda n: jnp.maximum(n - 1, 0)
    full = lambda shape: pl.BlockSpec(shape, lambda n: (0,) * len(shape))
    return _pcall(
        body, grid=(nb,),
        in_specs=[pl.BlockSpec((CHUNK, EVEN_IN), lambda n: (n, 0)),
                  pl.BlockSpec((CHUNK, 256), lambda n: (prev(n), 16)),
                  pl.BlockSpec((CHUNK, 384), lambda n: (n, 0)),
                  pl.BlockSpec((CHUNK, 384), lambda n: (prev(n), 0)),
                  full((1, W)), full((1, W)), full((8, 128, 128)), full((8, 128, 128)),
                  pl.BlockSpec(memory_space=pltpu.SMEM)],
        out_specs=[pl.BlockSpec((2, CHUNK, W), lambda n: (0, n, 0)),
                   pl.BlockSpec((CHUNK, W), lambda n: (n, 0)),
                   pl.BlockSpec((CHUNK, 128), lambda n: (n, 0))],
        out_shape=[jax.ShapeDtypeStruct((2, S, W), BF), jax.ShapeDtypeStruct((S, W), F32),
                   jax.ShapeDtypeStruct((S, 128), F32)],
        name=name, comm=comm)(h, h, rope, rope, lng.reshape(1, W), lnb.reshape(1, W), ws, bsb, sinks)


def _even_bwd(h, dmix3, o, l, rope, lng, lnb, ws, wst, bsb, sinks, name, comm=None):
    S = h.shape[0]
    nb = S // CHUNK

    def body(h_ref, hp_ref, hn_ref, dm_ref, dmn_ref, o_ref, on_ref, l_ref, ln_ref, rc_ref, rp_ref, rn_ref,
             lng_ref, lnb_ref, ws_ref, wst_ref, bsb_ref, sink_ref,
             dh_ref, dws_ref, dbs_ref, dlng_ref, dlnb_ref, dsink_ref, dvn_ref):
        n = pl.program_id(0)

        @pl.when(n == 0)
        def _():
            dws_ref[...] = jnp.zeros_like(dws_ref)
            dbs_ref[...] = jnp.zeros_like(dbs_ref)
            dlng_ref[...] = jnp.zeros_like(dlng_ref)
            dlnb_ref[...] = jnp.zeros_like(dlnb_ref)
            dsink_ref[...] = jnp.zeros_like(dsink_ref)

        lane = lax.broadcasted_iota(jnp.int32, (128, 128), 1)
        rowi = lax.broadcasted_iota(jnp.int32, (128, 128), 0)
        lane1 = lax.broadcasted_iota(jnp.int32, (1, 128), 1)
        tri = rowi >= lane
        tri_t = lane >= rowi
        lane_lo = lane < 64
        v = h_ref[:, W:2 * W]
        mu = jnp.mean(v, axis=1, keepdims=True)
        vc = v - mu
        var = jnp.mean(vc * vc, axis=1, keepdims=True)
        rstd = lax.rsqrt(var + LN_EPS)
        vhat = vc * rstd
        vn = vhat * lng_ref[...] + lnb_ref[...]
        dbs_acc = jnp.zeros((128, 128), F32)
        vng = [vn[:, g * 128:(g + 1) * 128].astype(BF) for g in range(8)]
        ms = [_dot(jnp.where(tri, ws_ref[g], 0.0).astype(BF), vng[g]) for g in range(8)]
        dmb = []
        for g in range(8):
            sl = slice(g * 128, (g + 1) * 128)
            m = ms[g] + bsb_ref[g]
            ag = h_ref[:, 2 * W + g * 128:2 * W + (g + 1) * 128]
            sg, dsg = _silu_grad(ag)
            u = h_ref[:, sl]
            da = dm_ref[0, :, sl]
            dmm = da * u * sg
            dh_ref[:, sl] = (da * m * sg).astype(BF)
            dh_ref[:, 2 * W + g * 128:2 * W + (g + 1) * 128] = (da * u * m * dsg).astype(BF)
            dmb.append(dmm.astype(BF))
            dbs_acc = jnp.where(lane == g, jnp.sum(dmm, axis=1, keepdims=True), dbs_acc)
        dvs = [_dot(jnp.where(tri_t, wst_ref[g], 0.0).astype(BF), dmb[g]) for g in range(8)]
        dwss = [_dot_nt(dmb[g], vng[g]) for g in range(8)]
        for g in range(8):
            dvn_ref[:, g * 128:(g + 1) * 128] = dvs[g]
            dws_ref[g] += jnp.where(tri, dwss[g], 0.0)
        dbs_ref[...] += dbs_acc
        dvn = dvn_ref[...]
        dlng_ref[...] += jnp.sum(dvn * vhat, axis=0, keepdims=True)
        dlnb_ref[...] += jnp.sum(dvn, axis=0, keepdims=True)
        dyg = dvn * lng_ref[...]
        m1 = jnp.mean(dyg, axis=1, keepdims=True)
        m2 = jnp.mean(dyg * vhat, axis=1, keepdims=True)
        dh_ref[:, W:2 * W] = (rstd * (dyg - m1 - vhat * m2)).astype(BF)
        kcur = _rope_fwd(h_ref[:, 4096:4224], rc_ref)
        kb = jnp.concatenate([_rope_fwd(hp_ref[:, 0:128], rp_ref), kcur], axis=0)
        vb = jnp.concatenate([hp_ref[:, 128:256], h_ref[:, 4224:4352]], axis=0)
        k2 = _dup_heads(kb)
        v2 = _dup_heads(vb)
        kc2 = _dup_heads(kcur)
        vc2 = _dup_heads(h_ref[:, 4224:4352])
        qi = lax.broadcasted_iota(jnp.int32, (128, 256), 0)
        kj = lax.broadcasted_iota(jnp.int32, (128, 256), 1)
        diff = qi + 128 - kj
        valid = (diff >= 0) & (diff < 128) & ((n > 0) | (kj >= 128))
        validn = (lane > rowi) & (n < nb - 1)
        lc = l_ref[...]
        lnx = ln_ref[...]
        dk = [jnp.zeros((128, 128), F32), jnp.zeros((128, 128), F32)]
        dv = [jnp.zeros((128, 128), F32), jnp.zeros((128, 128), F32)]
        dsk_acc = jnp.zeros((1, 128), F32)
        for j0 in range(0, 8, HEAD_COLS):
            heads = [(j, half) for j in range(j0, j0 + HEAD_COLS) for half in range(2)]
            t = {}
            for j in range(j0, j0 + HEAD_COLS):
                cs = slice(j * 128, (j + 1) * 128)
                qc = _rope_fwd(h_ref[:, 3072 + j * 128:3072 + (j + 1) * 128], rc_ref)
                qn = _rope_fwd(hn_ref[:, 3072 + j * 128:3072 + (j + 1) * 128], rn_ref)
                bg = h_ref[:, 4352 + j * 128:4352 + (j + 1) * 128]
                sgb, dsgb = _silu_grad(bg)
                db = dm_ref[1, :, cs]
                oc = o_ref[:, cs]
                do = db * sgb
                dh_ref[:, 4352 + j * 128:4352 + (j + 1) * 128] = (db * oc * dsgb).astype(BF)
                bgn = hn_ref[:, 4352 + j * 128:4352 + (j + 1) * 128]
                don = dmn_ref[1, :, cs] * (bgn * _sig(bgn))
                prod = do * oc
                prodn = don * on_ref[:, cs]
                for half in range(2):
                    hq = 2 * j + half
                    hm = lane_lo if half == 0 else jnp.logical_not(lane_lo)
                    t[j, half] = dict(
                        dsum=jnp.sum(jnp.where(hm, prod, 0.0), axis=1, keepdims=True),
                        dsumn=jnp.sum(jnp.where(hm, prodn, 0.0), axis=1, keepdims=True),
                        lh=jnp.sum(jnp.where(lane == hq, lc, 0.0), axis=1, keepdims=True),
                        lhn=jnp.sum(jnp.where(lane == hq, lnx, 0.0), axis=1, keepdims=True),
                        qm=jnp.where(hm, qc, 0.0).astype(BF), dom=jnp.where(hm, do, 0.0).astype(BF),
                        qnm=jnp.where(hm, qn, 0.0).astype(BF), donm=jnp.where(hm, don, 0.0).astype(BF))
            for j, half in heads:
                e, hk = t[j, half], j // 4
                e["s"], e["dp"] = _dot_nt(e["qm"], k2[hk]), _dot_nt(e["dom"], v2[hk])
                e["sn"], e["dpn"] = _dot_nt(e["qnm"], kc2[hk]), _dot_nt(e["donm"], vc2[hk])
            for j, half in heads:
                e, hq = t[j, half], 2 * j + half
                p = jnp.exp(jnp.where(valid, e["s"] * 0.125 - e["lh"], NEG))
                ds = p * (e["dp"] - e["dsum"])
                pn = jnp.exp(jnp.where(validn, e["sn"] * 0.125 - e["lhn"], NEG))
                dsn = pn * (e["dpn"] - e["dsumn"])
                psink = jnp.exp(sink_ref[hq] - e["lh"])
                dsk_acc = jnp.where(lane1 == hq, -jnp.sum(psink * e["dsum"], axis=0, keepdims=True), dsk_acc)
                e["ds"] = ds.astype(BF)
                e["pt"], e["dst"] = jnp.transpose(p[:, 128:256]).astype(BF), jnp.transpose(ds[:, 128:256]).astype(BF)
                e["pnt"], e["dsnt"] = jnp.transpose(pn).astype(BF), jnp.transpose(dsn).astype(BF)
            for j, half in heads:
                e, hk = t[j, half], j // 4
                e["dq"] = _dot(e["ds"], k2[hk])
                e["dv"] = _dot(e["pt"], e["dom"]) + _dot(e["pnt"], e["donm"])
                e["dk"] = _dot(e["dst"], e["qm"]) + _dot(e["dsnt"], e["qnm"])
            for j in range(j0, j0 + HEAD_COLS):
                hk = j // 4
                dqcol = jnp.where(lane_lo, t[j, 0]["dq"], t[j, 1]["dq"]) * 0.125
                dh_ref[:, 3072 + j * 128:3072 + (j + 1) * 128] = _rope_bwd(dqcol, rc_ref).astype(BF)
                dv[hk] = dv[hk] + t[j, 0]["dv"] + t[j, 1]["dv"]
                dk[hk] = dk[hk] + (t[j, 0]["dk"] + t[j, 1]["dk"]) * 0.125
        fold = lambda a: a + pltpu.roll(a, 64, 1)
        dh_ref[:, 4096:4224] = _rope_bwd(jnp.where(lane_lo, fold(dk[0]), fold(dk[1])), rc_ref).astype(BF)
        dh_ref[:, 4224:4352] = jnp.where(lane_lo, fold(dv[0]), fold(dv[1])).astype(BF)
        dsink_ref[...] += dsk_acc

    prev = lambda n: jnp.maximum(n - 1, 0)
    nxt = lambda n: jnp.minimum(n + 1, nb - 1)
    full = lambda shape: pl.BlockSpec(shape, lambda n: (0,) * len(shape))
    return _pcall(
        body, grid=(nb,),
        in_specs=[pl.BlockSpec((CHUNK, EVEN_IN), lambda n: (n, 0)),
                  pl.BlockSpec((CHUNK, 256), lambda n: (prev(n), 16)),
                  pl.BlockSpec((CHUNK, EVEN_IN), lambda n: (nxt(n), 0)),
                  pl.BlockSpec((2, CHUNK, W), lambda n: (0, n, 0)),
                  pl.BlockSpec((2, CHUNK, W), lambda n: (0, nxt(n), 0)),
                  pl.BlockSpec((CHUNK, W), lambda n: (n, 0)),
                  pl.BlockSpec((CHUNK, W), lambda n: (nxt(n), 0)),
                  pl.BlockSpec((CHUNK, 128), lambda n: (n, 0)),
                  pl.BlockSpec((CHUNK, 128), lambda n: (nxt(n), 0)),
                  pl.BlockSpec((CHUNK, 384), lambda n: (n, 0)),
                  pl.BlockSpec((CHUNK, 384), lambda n: (prev(n), 0)),
                  pl.BlockSpec((CHUNK, 384), lambda n: (nxt(n), 0)),
                  full((1, W)), full((1, W)), full((8, 128, 128)), full((8, 128, 128)), full((8, 128, 128)),
                  pl.BlockSpec(memory_space=pltpu.SMEM)],
        out_specs=[pl.BlockSpec((CHUNK, EVEN_IN), lambda n: (n, 0)),
                   full((8, 128, 128)), full((128, 128)), full((1, W)), full((1, W)), full((1, 128))],
        out_shape=[jax.ShapeDtypeStruct((S, EVEN_IN), BF), jax.ShapeDtypeStruct((8, 128, 128), F32),
                   jax.ShapeDtypeStruct((128, 128), F32), jax.ShapeDtypeStruct((1, W), F32),
                   jax.ShapeDtypeStruct((1, W), F32), jax.ShapeDtypeStruct((1, 128), F32)],
        scratch=[pltpu.VMEM((CHUNK, W), F32)], name=name, comm=comm,
    )(h, h, h, dmix3, dmix3, o, o, l, l, rope, rope, rope, lng.reshape(1, W), lnb.reshape(1, W), ws, wst, bsb, sinks)


def _expm1(x):
    ser = x * (1.0 + x * (0.5 + x * (1.0 / 6.0 + x * (1.0 / 24.0))))
    return jnp.where(jnp.abs(x) < 1e-2, ser, jnp.exp(x) - 1.0)


def _softplus_neg(lam):
    z = -lam
    e = jnp.exp(-jnp.abs(z))
    l1p = jnp.where(e < 1e-3, e * (1.0 - e * (0.5 - e * (1.0 / 3.0))), jnp.log(1.0 + e))
    return jnp.maximum(z, 0.0) + l1p


def _shift_down(x, k, row, fill=0.0):
    return jnp.where(row >= k, pltpu.roll(x, k, 0), fill)


def _shift_up(x, k, row, fill=0.0):
    S = x.shape[0]
    return jnp.where(row < S - k, pltpu.roll(x, S - k, 0), fill)


def _lru_gates(xc, row, cw_ref, cb_ref, wa_ref, wx_ref, ba_ref, bx_ref, lam_ref):
    xconv = (cw_ref[3:4, :] * xc + cw_ref[2:3, :] * _shift_down(xc, 1, row) + cw_ref[1:2, :] * _shift_down(xc, 2, row)
             + cw_ref[0:1, :] * _shift_down(xc, 3, row) + cb_ref[...])
    xb = xconv.astype(BF)
    r = _sig(_dot(xb, wa_ref[...]) + ba_ref[...])
    i = _sig(_dot(xb, wx_ref[...]) + bx_ref[...])
    sp = _softplus_neg(lam_ref[...])
    log_a = -LRU_C * r * sp
    a = jnp.exp(log_a)
    mult = jnp.sqrt(-_expm1(2.0 * log_a))
    return xconv, r, i, sp, a, mult


ROWS_PER_TILE = 8


def _steps(a, b, shift, inside, products=True):
    n, k = inside.n, 1
    while k < n:
        b = a * jnp.where(inside(k), shift(b, k), 0.0) + b
        if products or 2 * k < n:
            a = a * jnp.where(inside(k), shift(a, k), 1.0)
        k *= 2
    return a, b


class _Inside:
    def __init__(self, pos, n, reverse):
        self.pos, self.n, self.reverse = pos, n, reverse

    def __call__(self, k):
        return self.pos < self.n - k if self.reverse else self.pos >= k


def _scan_rows(a, b, row, a_ref, b_ref, c_ref, reverse=False):
    S = a.shape[0]
    G = S // ROWS_PER_TILE
    if reverse:
        shift = lambda x, k: pltpu.roll(x, x.shape[0] - k, 0)
    else:
        shift = lambda x, k: pltpu.roll(x, k, 0)
    a, b = _steps(a, b, shift, _Inside(row % ROWS_PER_TILE, ROWS_PER_TILE, reverse))
    a_ref[...] = a
    b_ref[...] = b
    last = 0 if reverse else ROWS_PER_TILE - 1
    grow = lax.broadcasted_iota(jnp.int32, (G, a.shape[1]), 0)
    _, tot = _steps(a_ref[pl.ds(last, G, stride=ROWS_PER_TILE), :], b_ref[pl.ds(last, G, stride=ROWS_PER_TILE), :],
                    shift, _Inside(grow, G, reverse), products=False)
    enters = jnp.where(_Inside(grow, G, reverse)(1), shift(tot, 1), 0.0)
    for r in range(ROWS_PER_TILE):
        c_ref[pl.ds(r, G, stride=ROWS_PER_TILE), :] = enters
    return b + a * c_ref[...]


def _odd_c_fwd(h, cw, cb, wa, wx, ba, bx, lam, name, comm=None):
    S = h.shape[0]

    def body(xc_ref, cg_ref, cw_ref, cb_ref, wa_ref, wx_ref, ba_ref, bx_ref, lam_ref, mix_ref, hst_ref, sa_ref, sb_ref, sc_ref):
        row = lax.broadcasted_iota(jnp.int32, (S, 128), 0)
        xconv, r, i, sp, a, mult = _lru_gates(xc_ref[...], row, cw_ref, cb_ref, wa_ref, wx_ref, ba_ref, bx_ref, lam_ref)
        bb = _scan_rows(a, mult * (i * xconv), row, sa_ref, sb_ref, sc_ref)
        hst_ref[...] = bb
        cg = cg_ref[...]
        mix_ref[...] = (bb * (cg * _sig(cg))).astype(BF)

    col = lambda off: pl.BlockSpec((S, 128), lambda j: (0, off + j))
    vec = pl.BlockSpec((1, 128), lambda j: (0, j))
    mat = pl.BlockSpec((None, 128, 128), lambda j: (j, 0, 0))
    return _pcall(
        body, grid=(8,),
        in_specs=[col(0), col(8), pl.BlockSpec((4, 128), lambda j: (0, j)), vec, mat, mat, vec, vec, vec],
        out_specs=[pl.BlockSpec((None, S, 128), lambda j: (0, 0, j)), pl.BlockSpec((S, 128), lambda j: (0, j))],
        out_shape=[jax.ShapeDtypeStruct((2, S, W), BF), jax.ShapeDtypeStruct((S, W), F32)],
        scratch=[pltpu.VMEM((S, 128), F32)] * 3, name=name, comm=comm,
    )(h, h, cw, cb.reshape(1, W), wa, wx, ba.reshape(1, W), bx.reshape(1, W), lam.reshape(1, W))


def _pool_sums(x, g, row, shift):
    s2 = x + shift(x, 1, row)
    s4 = s2 + shift(s2, 2, row)
    s8 = s4 + shift(s4, 4, row)
    s16 = s8 + shift(s8, 8, row)
    return jnp.where(g == 0, s2, jnp.where(g == 1, s4, jnp.where(g == 2, s8, s16)))


def _odd_d_fwd(h, mix3, wp, dscale, name):
    S = h.shape[0]

    def body(xd_ref, dg_ref, wp_ref, ds_ref, mix_in, mix_ref):
        g = pl.program_id(0)
        row = lax.broadcasted_iota(jnp.int32, (S, 256), 0)
        xd = xd_ref[...]
        cnt = jnp.minimum(row + 1, jnp.left_shift(2, g)).astype(F32)
        pooled = _pool_sums(xd, g, row, _shift_down) / cnt - xd
        mixed = _dot(pooled.astype(BF), wp_ref[...])
        dg = dg_ref[...]
        mix_ref[...] = (mixed * ds_ref[...] * (dg * _sig(dg))).astype(BF)

    col = lambda off: pl.BlockSpec((S, 256), lambda g: (0, off + g))
    return pl.pallas_call(
        body, grid=(4,),
        in_specs=[col(8), col(12), pl.BlockSpec((None, 256, 256), lambda g: (g, 0, 0)),
                  pl.BlockSpec((1, 256), lambda g: (0, g)), ANY],
        out_specs=pl.BlockSpec((None, S, 256), lambda g: (1, 0, g)),
        out_shape=jax.ShapeDtypeStruct((2, S, W), BF), input_output_aliases={4: 0},
        name=name, compiler_params=_cp(),
    )(h, h, wp, dscale.reshape(1, W), mix3)


def _odd_c_bwd(h, hst, dmix3, cw, cb, wa, wx, wat, wxt, ba, bx, lam, name, comm=None):
    S = h.shape[0]

    def body(xc_ref, cg_ref, hst_ref, dc_ref, cw_ref, cb_ref, wa_ref, wx_ref, wat_ref, wxt_ref, ba_ref, bx_ref, lam_ref,
             dh_ref, dcw_ref, dcb_ref, dwa_ref, dwx_ref, dba_ref, dbx_ref, dlam_ref, sa_ref, sb_ref, sc_ref):
        row = lax.broadcasted_iota(jnp.int32, (S, 128), 0)
        xc = xc_ref[...]
        xconv, r, i, sp, a, mult = _lru_gates(xc, row, cw_ref, cb_ref, wa_ref, wx_ref, ba_ref, bx_ref, lam_ref)
        hst = hst_ref[...]
        cg = cg_ref[...]
        sg, dsg = _silu_grad(cg)
        dc = dc_ref[...]
        dh_ref[1] = (dc * hst * dsg).astype(BF)
        lam_t = _scan_rows(_shift_up(a, 1, row), dc * sg, row, sa_ref, sb_ref, sc_ref, reverse=True)
        da = lam_t * _shift_down(hst, 1, row)
        ix = i * xconv
        dmult = lam_t * ix
        di = lam_t * mult * xconv
        dxconv = lam_t * mult * i
        dlog_a = da * a - dmult * (a * a / mult)
        dr = dlog_a * (-LRU_C * sp)
        dsp = jnp.sum(dlog_a * (-LRU_C * r), axis=0, keepdims=True)
        dlam_ref[...] = dsp * (-_sig(-lam_ref[...]))
        dpa = dr * r * (1.0 - r)
        dpx = di * i * (1.0 - i)
        dpab = dpa.astype(BF)
        dpxb = dpx.astype(BF)
        xb = xconv.astype(BF)
        dxconv = dxconv + _dot(dpab, wat_ref[...]) + _dot(dpxb, wxt_ref[...])
        dwa_ref[...] = _dot_tn(xb, dpab)
        dwx_ref[...] = _dot_tn(xb, dpxb)
        dba_ref[...] = jnp.sum(dpa, axis=0, keepdims=True)
        dbx_ref[...] = jnp.sum(dpx, axis=0, keepdims=True)
        dh_ref[0] = (cw_ref[3:4, :] * dxconv + cw_ref[2:3, :] * _shift_up(dxconv, 1, row)
                     + cw_ref[1:2, :] * _shift_up(dxconv, 2, row) + cw_ref[0:1, :] * _shift_up(dxconv, 3, row)).astype(BF)
        for j in range(4):
            src = xc if j == 3 else _shift_down(xc, 3 - j, row)
            dcw_ref[j:j + 1, :] = jnp.sum(dxconv * src, axis=0, keepdims=True)
        dcb_ref[...] = jnp.sum(dxconv, axis=0, keepdims=True)

    col = lambda off: pl.BlockSpec((S, 128), lambda j: (0, off + j))
    vec = pl.BlockSpec((1, 128), lambda j: (0, j))
    mat = pl.BlockSpec((None, 128, 128), lambda j: (j, 0, 0))
    vshape = jax.ShapeDtypeStruct((1, W), F32)
    mshape = jax.ShapeDtypeStruct((8, 128, 128), F32)
    return _pcall(
        body, grid=(8,),
        in_specs=[col(0), col(8), col(0), pl.BlockSpec((None, S, 128), lambda j: (0, 0, j)),
                  pl.BlockSpec((4, 128), lambda j: (0, j)), vec, mat, mat, mat, mat, vec, vec, vec],
        out_specs=[pl.BlockSpec((2, S, 128), lambda j: (0, 0, j)), pl.BlockSpec((4, 128), lambda j: (0, j)), vec,
                   mat, mat, vec, vec, vec],
        out_shape=[jax.ShapeDtypeStruct((4, S, W), BF), jax.ShapeDtypeStruct((4, W), F32), vshape, mshape, mshape,
                   vshape, vshape, vshape],
        scratch=[pltpu.VMEM((S, 128), F32)] * 3, name=name, vmem=56, comm=comm,
    )(h, h, hst, dmix3, cw, cb.reshape(1, W), wa, wx, wat, wxt, ba.reshape(1, W), bx.reshape(1, W), lam.reshape(1, W))


def _odd_d_bwd(h, dmix3, dh4, wp, wpt, dscale, name):
    S = h.shape[0]

    def body(xd_ref, dg_ref, dd_ref, wp_ref, wpt_ref, ds_ref, dh_in, dh_ref, dwp_ref, dds_ref):
        g = pl.program_id(0)
        row = lax.broadcasted_iota(jnp.int32, (S, 256), 0)
        xd = xd_ref[...]
        cnt = jnp.minimum(row + 1, jnp.left_shift(2, g)).astype(F32)
        pooled = _pool_sums(xd, g, row, _shift_down) / cnt - xd
        pb = pooled.astype(BF)
        mixed = _dot(pb, wp_ref[...])
        dg = dg_ref[...]
        sg, dsg = _silu_grad(dg)
        dd = dd_ref[...]
        dmixed = dd * ds_ref[...] * sg
        dds_ref[...] = jnp.sum(dd * mixed * sg, axis=0, keepdims=True)
        dh_ref[1] = (dd * mixed * ds_ref[...] * dsg).astype(BF)
        dmb = dmixed.astype(BF)
        dpooled = _dot(dmb, wpt_ref[...])
        dwp_ref[...] = _dot_tn(pb, dmb)
        dh_ref[0] = (_pool_sums(dpooled / cnt, g, row, _shift_up) - dpooled).astype(BF)

    col = lambda off: pl.BlockSpec((S, 256), lambda g: (0, off + g))
    mat = pl.BlockSpec((None, 256, 256), lambda g: (g, 0, 0))
    vec = pl.BlockSpec((1, 256), lambda g: (0, g))
    return pl.pallas_call(
        body, grid=(4,),
        in_specs=[col(8), col(12), pl.BlockSpec((None, S, 256), lambda g: (1, 0, g)), mat, mat, vec, ANY],
        out_specs=[pl.BlockSpec((2, S, 256), lambda g: (1, 0, g)), mat, vec],
        out_shape=[jax.ShapeDtypeStruct((4, S, W), BF), jax.ShapeDtypeStruct((4, 256, 256), F32),
                   jax.ShapeDtypeStruct((1, W), F32)],
        input_output_aliases={6: 0}, name=name, compiler_params=_cp(56),
    )(h, h, dmix3, wp, wpt, dscale.reshape(1, W), dh4)


def _peer(d):
    x, y, c = lax.axis_index("x"), lax.axis_index("y"), lax.axis_index("c")
    px = 1 - x if d & 4 else x
    py = 1 - y if d & 2 else y
    pc = 1 - c if d & 1 else c
    return (px, py, pc), 4 * px + 2 * py + pc


class _GatherAll(_Comm):
    def __init__(self, xs):
        self.peers = EVERYONE
        self.inputs = [xs]
        self.out_shapes = [jax.ShapeDtypeStruct((N_DEV,) + xs.shape, xs.dtype)]
        self.sem_shapes = [pltpu.SemaphoreType.DMA((N_DEV - 1,)), pltpu.SemaphoreType.DMA((N_DEV - 1,)),
                           pltpu.SemaphoreType.DMA]

    def copies(self, ins, outs, sems):
        (x_ref,), (out_ref,), (send, recv, loc) = ins, outs, sems
        _, me = _peer(0)
        res = [pltpu.make_async_copy(x_ref, out_ref.at[me], loc)]
        for d in range(1, N_DEV):
            peer, _ = _peer(d)
            res.append(pltpu.make_async_remote_copy(src_ref=x_ref, dst_ref=out_ref.at[me], send_sem=send.at[d - 1],
                                                    recv_sem=recv.at[d - 1], device_id=peer, device_id_type=MESH))
        return res


class _ExchangeAll(_Comm):
    def __init__(self, g8):
        self.peers = EVERYONE
        self.inputs = [g8]
        self.out_shapes = [jax.ShapeDtypeStruct(g8.shape, g8.dtype)]
        self.sem_shapes = [pltpu.SemaphoreType.DMA((N_DEV - 1,)), pltpu.SemaphoreType.DMA((N_DEV - 1,)),
                           pltpu.SemaphoreType.DMA]

    def copies(self, ins, outs, sems):
        (g_ref,), (out_ref,), (send, recv, loc) = ins, outs, sems
        _, me = _peer(0)
        res = [pltpu.make_async_copy(g_ref.at[me], out_ref.at[0], loc)]
        for d in range(1, N_DEV):
            peer, pidx = _peer(d)
            res.append(pltpu.make_async_remote_copy(src_ref=g_ref.at[pidx], dst_ref=out_ref.at[d], send_sem=send.at[d - 1],
                                                    recv_sem=recv.at[d - 1], device_id=peer, device_id_type=MESH))
        return res


def _sum8(r8, tr, name):
    _, R, C = r8.shape
    tr = min(tr, R)
    assert R % tr == 0

    def body(r_ref, o_ref):
        acc = r_ref[0]
        for d in range(1, N_DEV):
            acc = acc + r_ref[d]
        o_ref[...] = acc

    return pl.pallas_call(
        body, grid=(R // tr,), in_specs=[pl.BlockSpec((N_DEV, tr, C), lambda i: (0, i, 0))],
        out_specs=pl.BlockSpec((tr, C), lambda i: (i, 0)), out_shape=jax.ShapeDtypeStruct((R, C), F32),
        name=name, compiler_params=_cp(),
    )(r8)


def _adamw_math(w, g, m, v):
    m2 = B1 * m + (1.0 - B1) * g
    v2 = B2 * v + (1.0 - B2) * (g * g)
    m_hat = m2 / (1.0 - B1 ** STEP)
    v_hat = v2 / (1.0 - B2 ** STEP)
    return -LR * (m_hat / (jnp.sqrt(v_hat) + ADAM_EPS) + WD * w), m2, v2


def _adamw_many(ws, gs, ms, vs, name):
    n = len(ws)

    def body(*refs):
        for i in range(n):
            d, m2, v2 = _adamw_math(refs[i][...], refs[n + i][...], refs[2 * n + i][...], refs[3 * n + i][...])
            refs[4 * n + i][...] = d
            refs[5 * n + i][...] = m2
            refs[6 * n + i][...] = v2

    vmem = pl.BlockSpec(memory_space=pltpu.VMEM)
    shapes = [jax.ShapeDtypeStruct(w.shape, F32) for w in ws]
    res = pl.pallas_call(body, in_specs=[vmem] * (4 * n), out_specs=[vmem] * (3 * n), out_shape=shapes * 3, name=name,
                         compiler_params=_cp())(*ws, *gs, *ms, *vs)
    return res[:n], res[n:2 * n], res[2 * n:]


def _adamw(w3, gs, m3, v3, tr, name, comm=None):
    _, R, C = w3.shape
    n = 2 if isinstance(gs[0], tuple) else 1

    def gradient(refs):
        if n == 1:
            return refs[0][...]
        s_ref, r_ref = refs
        return ((s_ref[...].astype(F32) + r_ref[0].astype(F32)) + r_ref[1].astype(F32)) + r_ref[2].astype(F32)

    def body(w_ref, *rest):
        g_refs, (m_ref, v_ref, d_ref, m2_ref, v2_ref, g_ref) = rest[:2 * n], rest[2 * n:]
        g = jnp.where(pl.program_id(0) == 0, gradient(g_refs[:n]), gradient(g_refs[n:]))
        d_ref[...], m2_ref[...], v2_ref[...] = _adamw_math(w_ref[...], g, m_ref[...], v_ref[...])
        g_ref[...] = g

    blk = pl.BlockSpec((None, tr, C), lambda j, i: (j, i, 0))

    def grad_specs(layer):
        at = lambda j, i: jnp.where(j == layer, i, 0)
        if n == 1:
            return [pl.BlockSpec((tr, C), lambda j, i: (at(j, i), 0))]
        return [pl.BlockSpec((None, tr, C), lambda j, i: (0, at(j, i), 0)), pl.BlockSpec((3, tr, C), lambda j, i: (0, at(j, i), 0))]

    flat = [a for g in gs for a in (g if n == 2 else (g,))]
    shp = jax.ShapeDtypeStruct((2, R, C), F32)
    return _pcall(body, grid=(2, R // tr), in_specs=[blk] + grad_specs(0) + grad_specs(1) + [blk, blk], out_specs=[blk] * 4,
                  out_shape=[shp] * 4, name=name, comm=comm)(w3, *flat, m3, v3)


def _rep_pack(a):
    n = a.size
    pad = (-n) % 1024
    f = a.reshape(-1)
    if pad:
        f = jnp.concatenate([f, jnp.zeros((pad,), a.dtype)])
    return f.reshape(N_DEV, -1, 128)


def _rep_unpack(p, shape):
    n = 1
    for s in shape:
        n *= s
    return p.reshape(-1)[:n].reshape(shape)


def _sh_pack(a, axis):
    shp = a.shape
    a = a.reshape(shp[:axis] + (N_DEV, shp[axis] // N_DEV) + shp[axis + 1:])
    return jnp.moveaxis(a, axis, 0).reshape(N_DEV, -1, 128)


def _sh_unpack(p, shape, axis):
    a = p.reshape((N_DEV,) + shape[:axis] + (shape[axis] // N_DEV,) + shape[axis + 1:])
    return jnp.moveaxis(a, 0, axis).reshape(shape)


def _pad_rows(a, mult=8):
    pad = (-a.shape[-2]) % mult
    if pad:
        a = jnp.concatenate([a, jnp.zeros(a.shape[:-2] + (pad, a.shape[-1]), a.dtype)], axis=-2)
    return a


REP = ["even_a_ln_g", "even_a_ln_b", "even_a_ws", "even_a_bs", "even_b_sinks", "even_ln_g", "even_ln_b",
       "odd_w_a", "odd_w_x"]
SH = [("odd_conv_w", (2, 4, W), 2), ("odd_conv_b", (2, W), 1), ("odd_b_a", (2, W), 1), ("odd_b_x", (2, W), 1),
      ("odd_lam", (2, W), 1), ("odd_w_pool", (2, 4, 256, 256), 2), ("odd_d_scale", (2, W), 1),
      ("odd_ln_g", (2, D), 1), ("odd_ln_b", (2, D), 1)]
BIG = ["even_w_in", "even_w_out", "odd_w_in", "odd_w_out"]
NAMES = ["even_w_in", "even_a_ln_g", "even_a_ln_b", "even_a_ws", "even_a_bs", "even_b_sinks", "even_w_out",
         "even_ln_g", "even_ln_b", "odd_w_in", "odd_conv_w", "odd_conv_b", "odd_w_a", "odd_b_a", "odd_w_x", "odd_b_x",
         "odd_lam", "odd_w_pool", "odd_d_scale", "odd_w_out", "odd_ln_g", "odd_ln_b"]


def _rope_table(positions):
    inv = ROPE_THETA ** (-jnp.arange(0, 16, 2, dtype=F32) / 16)
    f = jnp.arange(128) % 64
    ang = positions.astype(F32)[:, None] * inv[f % 8][None, :]
    cos, sin = jnp.cos(ang), jnp.sin(ang)
    return jnp.concatenate([jnp.where(f < 16, cos, 1.0), jnp.where(f < 8, -sin, 0.0),
                            jnp.where((f >= 8) & (f < 16), sin, 0.0)], axis=1)


def kernel(x, positions, even_w_in, even_a_ln_g, even_a_ln_b, even_a_ws, even_a_bs, even_b_sinks, even_w_out, even_ln_g, even_ln_b, odd_w_in, odd_conv_w, odd_conv_b, odd_w_a, odd_b_a, odd_w_x, odd_b_x, odd_lam, odd_w_pool, odd_d_scale, odd_w_out, odd_ln_g, odd_ln_b, loss_target, m_even_w_in, m_even_a_ln_g, m_even_a_ln_b, m_even_a_ws, m_even_a_bs, m_even_b_sinks, m_even_w_out, m_even_ln_g, m_even_ln_b, m_odd_w_in, m_odd_conv_w, m_odd_conv_b, m_odd_w_a, m_odd_b_a, m_odd_w_x, m_odd_b_x, m_odd_lam, m_odd_w_pool, m_odd_d_scale, m_odd_w_out, m_odd_ln_g, m_odd_ln_b, v_even_w_in, v_even_a_ln_g, v_even_a_ln_b, v_even_a_ws, v_even_a_bs, v_even_b_sinks, v_even_w_out, v_even_ln_g, v_even_ln_b, v_odd_w_in, v_odd_conv_w, v_odd_conv_b, v_odd_w_a, v_odd_b_a, v_odd_w_x, v_odd_b_x, v_odd_lam, v_odd_w_pool, v_odd_d_scale, v_odd_w_out, v_odd_ln_g, v_odd_ln_b):
    args = (even_w_in, even_a_ln_g, even_a_ln_b, even_a_ws, even_a_bs, even_b_sinks, even_w_out, even_ln_g, even_ln_b,
            odd_w_in, odd_conv_w, odd_conv_b, odd_w_a, odd_b_a, odd_w_x, odd_b_x, odd_lam, odd_w_pool, odd_d_scale,
            odd_w_out, odd_ln_g, odd_ln_b)
    margs = (m_even_w_in, m_even_a_ln_g, m_even_a_ln_b, m_even_a_ws, m_even_a_bs, m_even_b_sinks, m_even_w_out,
             m_even_ln_g, m_even_ln_b, m_odd_w_in, m_odd_conv_w, m_odd_conv_b, m_odd_w_a, m_odd_b_a, m_odd_w_x,
             m_odd_b_x, m_odd_lam, m_odd_w_pool, m_odd_d_scale, m_odd_w_out, m_odd_ln_g, m_odd_ln_b)
    vargs = (v_even_w_in, v_even_a_ln_g, v_even_a_ln_b, v_even_a_ws, v_even_a_bs, v_even_b_sinks, v_even_w_out,
             v_even_ln_g, v_even_ln_b, v_odd_w_in, v_odd_conv_w, v_odd_conv_b, v_odd_w_a, v_odd_b_a, v_odd_w_x,
             v_odd_b_x, v_odd_lam, v_odd_w_pool, v_odd_d_scale, v_odd_w_out, v_odd_ln_g, v_odd_ln_b)
    wts = dict(zip(NAMES, args))
    mom = dict(zip(NAMES, margs))
    var = dict(zip(NAMES, vargs))
    S = x.shape[1]
    x0 = x[0]
    rope = _rope_table(positions[0])

    kinds = ("even", "odd", "even", "odd")
    blk_in = [jnp.transpose(wts[kinds[l] + "_w_in"][l // 2]).astype(BF) for l in range(4)]
    blk_out = [wts[kinds[l] + "_w_out"][l // 2].astype(BF) for l in range(4)]
    sh_local = _pad_rows(jnp.concatenate([wts[nm].reshape(-1, 128) for nm, _, _ in SH], axis=0), 16)
    me = 4 * lax.axis_index("x") + 2 * lax.axis_index("y") + lax.axis_index("c")
    own_slot = lambda blk: lax.dynamic_update_slice(lax.empty((N_DEV,) + blk.shape, blk.dtype), blk[None], (me, 0, 0))
    reg = {"blk_small": sh_local, "w_small": own_slot(sh_local)}
    sched = _Sched(reg)
    for l in range(4):
        reg[f"blk_in{l}"], reg[f"blk_out{l}"] = blk_in[l], blk_out[l]
        reg[f"w_in{l}"], reg[f"w_out{l}"] = own_slot(blk_in[l]), own_slot(blk_out[l])
    sched.add(_rows("blk_in0", "w_in0", "ag1", blk_in[0].shape[0], ROW_CHUNK[blk_in[0].shape[0]]))
    sched.add(_rows("blk_small", "w_small", "ag1", sh_local.shape[0], sh_local.shape[0]))
    for l in range(4):
        sched.add(_rows(f"blk_out{l}", f"w_out{l}", "ag1", D // N_DEV, ROW_CHUNK[D // N_DEV]))
        if l < 3:
            r = blk_in[l + 1].shape[0]
            sched.add(_rows(f"blk_in{l + 1}", f"w_in{l + 1}", "ag1", r, ROW_CHUNK[r]))

    def gathered(dst, blk):
        sched.flush(dst, FLUSH_EXTRA_US)
        return reg.pop(dst)

    wt_in0 = gathered("w_in0", blk_in[0]).reshape(-1, D)
    full = {nm: wts[nm] for nm in REP}

    def gather_small():
        sh_all = gathered("w_small", sh_local)
        off = 0
        for nm, shape, axis in SH:
            r = wts[nm].size // 128
            full[nm] = _sh_unpack(sh_all[:, off:off + r, :], shape, axis)
            off += r

    saved = []
    wt_in, w_out = [wt_in0, None, None, None], [None] * 4
    xf, xb = x0, x0.astype(BF)
    fwd = lambda name: FWD_OVERBOOK * CARRY_US[name]
    for layer in range(4):
        j = layer // 2
        kind = kinds[layer]
        if wt_in[layer] is None:
            wt_in[layer] = gathered(f"w_in{layer}", blk_in[layer]).reshape(-1, D)
        h = sched.run(_mm_nt, fwd("mm_h_" + kind), xb, wt_in[layer], 1024, 768 if kind == "even" else 512, "mm_h_" + kind)
        if kind == "even":
            bsb = jnp.broadcast_to(full["even_a_bs"][j][:, :, None], (8, 128, 128))
            mix3, o, l = sched.run(_even_fwd, fwd("even_fwd"), h, rope, full["even_a_ln_g"][j], full["even_a_ln_b"][j],
                                   full["even_a_ws"][j], bsb, full["even_b_sinks"][j], "even_fwd")
            extra = (o, l, bsb)
        else:
            if "odd_lam" not in full:
                gather_small()
            wa, wx = full["odd_w_a"][j].astype(BF), full["odd_w_x"][j].astype(BF)
            wp = full["odd_w_pool"][j].astype(BF)
            mix3, hst = sched.run(_odd_c_fwd, fwd("odd_c_fwd"), h, full["odd_conv_w"][j], full["odd_conv_b"][j], wa, wx,
                                  full["odd_b_a"][j], full["odd_b_x"][j], full["odd_lam"][j], "odd_c_fwd")
            mix3 = _odd_d_fwd(h, mix3, wp, full["odd_d_scale"][j], "odd_d_fwd")
            extra = (hst, wa, wx, wp)
        w_out[layer] = gathered(f"w_out{layer}", blk_out[layer]).reshape(D, D)
        z, xn, xnb = sched.run(_mm_out_ln, fwd("mm_out_ln"), mix3, w_out[layer], xf, full[kind + "_ln_g"][j],
                               full[kind + "_ln_b"][j], "mm_out_ln")
        saved.append((xb, h, mix3, z, extra))
        xf, xb = xn, xnb

    dxn = xf

    gsum = {nm: [None, None] for nm in NAMES}

    chip_sums = {}
    sched.overhang = 0.15

    waiting = []

    def chip_sum(g, tag, key):
        r = g.shape[0] // N_DEV
        reg["g_" + key] = g.reshape(N_DEV, r, D)
        sched.add(_rows("g_" + key, "d_" + key, "rsd", r, r), first=True)
        waiting.append((key, tag))

    def add_arrived():
        for key, tag in list(waiting):
            if "d_" + key in reg and not sched.pending("d_" + key):
                waiting.remove((key, tag))
                g8 = reg.pop("g_" + key)
                chip_sums[key] = reg["s_" + key] = _add_pairs(g8, reg.pop("d_" + key), "rs_add_" + tag)
                sched.add(_rows("s_" + key, "r_" + key, "rs", g8.shape[1], ROW_CHUNK[g8.shape[1]] // 2))

    sched.after_landing = add_arrived

    def reduced(key):
        sched.flush("d_" + key, FLUSH_EXTRA_US)
        sched.flush("r_" + key, FLUSH_EXTRA_US)
        return chip_sums[key], reg.pop("r_" + key)

    for layer in (3, 2, 1, 0):
        j = layer // 2
        xb, h, mix3, z, extra = saved[layer]
        kind = kinds[layer]
        if layer == 3:
            dz, dzb, dg, dbeta, part = sched.run(_ln_bwd, CARRY_US["ln_bwd"], dxn, z, full[kind + "_ln_g"][j], "loss_ln_bwd",
                                                 target=loss_target[0])
            loss = lax.psum(part[0, 0] * (0.5 / D), ("x", "y", "c"))
        else:
            dz, dzb, dg, dbeta = sched.run(_ln_bwd, CARRY_US["ln_bwd"], dxn, z, full[kind + "_ln_g"][j], "ln_bwd")
        gsum[kind + "_ln_g"][j] = dg.reshape(D)
        gsum[kind + "_ln_b"][j] = dbeta.reshape(D)
        chip_sum(sched.run(_mm_tn, CARRY_US["mm_dw_out"], mix3, dzb, 512, "mm_dw_out"), "w_out", f"out{layer}")
        dmix3 = sched.run(_mm_nt, CARRY_US["mm_dmix"], dzb, w_out[layer], 1024, 512, "mm_dmix", out3=True)
        if kind == "even":
            o, l, bsb = extra
            ws = full["even_a_ws"][j]
            dh, dws, dbs, dlng, dlnb, dsink = sched.run(
                _even_bwd, CARRY_US["even_bwd"], h, dmix3, o, l, rope, full["even_a_ln_g"][j], full["even_a_ln_b"][j],
                ws, jnp.swapaxes(ws, 1, 2), bsb, full["even_b_sinks"][j], "even_bwd")
            gsum["even_a_ws"][j] = dws
            gsum["even_a_bs"][j] = jnp.transpose(dbs[:, :8])
            gsum["even_a_ln_g"][j] = dlng.reshape(W)
            gsum["even_a_ln_b"][j] = dlnb.reshape(W)
            gsum["even_b_sinks"][j] = dsink[0, :16]
            if layer == 0:
                rep_rows = [_rep_pack(jnp.stack(gsum[nm]).reshape(wts[nm].shape)) for nm in REP]
                sh_rows = [_sh_pack(jnp.stack(gsum[nm]).reshape(shape), axis) for nm, shape, axis in SH]
                packed = _pad_rows(jnp.concatenate(rep_rows + sh_rows, axis=1))
                gw, (small8,) = _mm_tn(dh, xb, 384, "mm_dw_in_even", comm=_ExchangeAll(packed))
            else:
                gw = sched.run(_mm_tn, CARRY_US["mm_dw_in_even"], dh, xb, 384, "mm_dw_in_even")
            chip_sum(gw, "w_in_even", f"in{layer}")
            if layer == 0:
                n_rep = sum(p.shape[1] for p in rep_rows)
                red = _sum8(small8, 1 << 20, "sum_small")
                (rep_all,) = sched.flush("d_in0", FLUSH_EXTRA_US, beside=_GatherAll(_pad_rows(red[:n_rep])))
                sched.overhang = 0.6
            dxn = sched.run(_mm_nn_res, CARRY_US["mm_dx_even"], dh, wt_in[layer], dz, 512, 512, "mm_dx_even")
        else:
            hst, wa, wx, wp = extra
            dh4, dcw, dcb, dwa, dwx, dba, dbx, dlam = sched.run(
                _odd_c_bwd, CARRY_US["odd_c_bwd"], h, hst, dmix3, full["odd_conv_w"][j], full["odd_conv_b"][j], wa, wx,
                jnp.swapaxes(wa, 1, 2), jnp.swapaxes(wx, 1, 2), full["odd_b_a"][j], full["odd_b_x"][j], full["odd_lam"][j],
                "odd_c_bwd")
            dh4, dwp, dds = _odd_d_bwd(h, dmix3, dh4, wp, jnp.swapaxes(wp, 1, 2), full["odd_d_scale"][j], "odd_d_bwd")
            gsum["odd_conv_w"][j], gsum["odd_conv_b"][j] = dcw, dcb.reshape(W)
            gsum["odd_w_a"][j], gsum["odd_w_x"][j] = dwa, dwx
            gsum["odd_b_a"][j], gsum["odd_b_x"][j], gsum["odd_lam"][j] = dba.reshape(W), dbx.reshape(W), dlam.reshape(W)
            gsum["odd_w_pool"][j], gsum["odd_d_scale"][j] = dwp, dds.reshape(W)
            chip_sum(sched.run(_mm_tn, CARRY_US["mm_dw_in_odd"], dh4, xb, 512, "mm_dw_in_odd"), "w_in_odd", f"in{layer}")
            dxn = sched.run(_mm_nn_res, CARRY_US["mm_dx_odd"], dh4, wt_in[layer], dz, 512, 512, "mm_dx_odd")
    grad_x = dxn[None]

    out_g, out_d, out_m, out_v = {}, {}, {}, {}
    for nm, kind, what, layers in (("odd_w_out", "odd", "out", (1, 3)), ("even_w_out", "even", "out", (0, 2)),
                                   ("odd_w_in", "odd", "in", (1, 3)), ("even_w_in", "even", "in", (0, 2))):
        gl = [reduced(f"{what}{l}") for l in layers]
        if nm == "even_w_in":
            view = lambda a: jnp.transpose(a, (0, 2, 1))
            res, _ = _adamw(view(wts[nm]), gl, view(mom[nm]), view(var[nm]), 112, f"adamw_{nm}")
            res = [view(a) for a in res]
        elif what == "in":
            gs = [jnp.transpose(_rs_final(s4, r3, "rs_final_w_in_odd")) for s4, r3 in gl]
            res, _ = _adamw(wts[nm], gs, mom[nm], var[nm], 512, f"adamw_{nm}")
        else:
            res = sched.run(_adamw, CARRY_US["adamw_" + nm], wts[nm], gl, mom[nm], var[nm], 128, f"adamw_{nm}")
        out_d[nm], out_m[nm], out_v[nm], out_g[nm] = res

    g_small = {}
    off = 0
    for nm, p in zip(REP, rep_rows):
        r = p.shape[1]
        g_small[nm] = _rep_unpack(rep_all[:, off:off + r, :], wts[nm].shape)
        off += r
    off = n_rep
    for (nm, shape, axis), p in zip(SH, sh_rows):
        r = p.shape[1]
        g_small[nm] = red[off:off + r].reshape(wts[nm].shape)
        off += r

    def rows(a):
        f = a.reshape(-1)
        pad = (-f.shape[0]) % 128
        if pad:
            f = jnp.concatenate([f, jnp.zeros((pad,), a.dtype)])
        return f.reshape(-1, 128)

    small = REP + [nm for nm, _, _ in SH]
    each = lambda src: [rows(src[nm]) for nm in small]
    d2, m2, v2 = _adamw_many(each(wts), each(g_small), each(mom), each(var), "adamw_small")
    for i, nm in enumerate(small):
        n, shp = wts[nm].size, wts[nm].shape
        take = lambda a: a.reshape(-1)[:n].reshape(shp)
        out_g[nm], out_d[nm], out_m[nm], out_v[nm] = g_small[nm], take(d2[i]), take(m2[i]), take(v2[i])

    return (loss, grad_x, *[out_g[nm] for nm in NAMES], *[out_d[nm] for nm in NAMES],
            *[out_m[nm] for nm in NAMES], *[out_v[nm] for nm in NAMES])
```

```python
import functools

import jax
import jax.numpy as jnp
from jax import lax
from jax.experimental import pallas as pl
from jax.experimental.pallas import tpu as pltpu

F32 = jnp.float32
BF = jnp.bfloat16
MESH = pl.DeviceIdType.MESH
ANY = pl.BlockSpec(memory_space=pl.ANY)

N_DEV = 8
D = 2048
W = 1024
EVEN_IN = 5376
ODD_IN = 4096
CHUNK = 128
ALPHA = (2 * 4) ** 0.25
LN_EPS = 1e-5
ROPE_THETA = 500000.0
LRU_C = 8.0
LR, B1, B2, ADAM_EPS, WD, STEP = 0.001, 0.9, 0.999, 1e-08, 0.01, 10
NEG = -1e30
HEAD_COLS = 4


def _cp(vmem_mb=48, collective_id=None):
    return pltpu.CompilerParams(vmem_limit_bytes=vmem_mb * 1024 * 1024, collective_id=collective_id)


def _sig(x):
    return jax.nn.sigmoid(x)


def _silu_grad(x):
    s = _sig(x)
    return x * s, s * (1.0 + x * (1.0 - s))


def _dot(a, b):
    return jnp.dot(a, b, preferred_element_type=F32)


def _dot_nt(a, b):
    return lax.dot_general(a, b, (((1,), (1,)), ((), ())), preferred_element_type=F32)


def _dot_tn(a, b):
    return lax.dot_general(a, b, (((0,), (0,)), ((), ())), preferred_element_type=F32)


def _coords():
    return lax.axis_index("x"), lax.axis_index("y"), lax.axis_index("c")


def _chip(j):
    x, y, _ = _coords()
    return (1 - x if j & 2 else x), (1 - y if j & 1 else y)


X_NB, Y_NB, DIAG, SIB = 4, 2, 6, 1
EVERYONE = frozenset(range(1, N_DEV))
BARRIER_IDS = {}


class _Comm:
    def collective_id(self):
        return BARRIER_IDS.setdefault(frozenset(self.peers), len(BARRIER_IDS))

    def start(self, ins, outs, sems):
        barrier = pltpu.get_barrier_semaphore()
        for d in sorted(self.peers):
            pl.semaphore_signal(barrier, inc=1, device_id=_peer(d)[0], device_id_type=MESH)
        pl.semaphore_wait(barrier, len(self.peers))
        for cp in self.copies(ins, outs, sems):
            cp.start()

    def wait(self, ins, outs, sems):
        for cp in self.copies(ins, outs, sems):
            cp.wait()


class _Join(_Comm):
    def __init__(self, parts):
        self.parts = list(parts)
        self.peers = frozenset().union(*[p.peers for p in self.parts])
        self.inputs = [a for p in self.parts for a in p.inputs]
        self.out_shapes = [s for p in self.parts for s in p.out_shapes]
        self.sem_shapes = [s for p in self.parts for s in p.sem_shapes]
        self.aliases = {}
        i0 = o0 = 0
        for p in self.parts:
            for i, o in getattr(p, "aliases", {}).items():
                self.aliases[i0 + i] = o0 + o
            i0, o0 = i0 + len(p.inputs), o0 + len(p.out_shapes)

    def copies(self, ins, outs, sems):
        res = []
        i0 = o0 = s0 = 0
        for p in self.parts:
            ni, no, ns = len(p.inputs), len(p.out_shapes), len(p.sem_shapes)
            res += p.copies(ins[i0:i0 + ni], outs[o0:o0 + no], sems[s0:s0 + ns])
            i0, o0, s0 = i0 + ni, o0 + no, s0 + ns
        return res


ROWS_US = {"ag1": 0.104, "ag2": 0.052, "agd": 0.027, "rsd": 0.027, "rs": 0.205}
N_COPIES = {"ag1": 2, "ag2": 2, "agd": 4, "rsd": 4, "rs": 3}
TASK_PEERS = {"ag1": {X_NB, Y_NB}, "ag2": {X_NB, Y_NB}, "agd": {SIB}, "rsd": {SIB}, "rs": {X_NB, Y_NB, DIAG}}
ROW_CHUNK = {672: 224, 512: 128, 256: 128}
CARRY_US = {"mm_h_even": 58, "mm_h_odd": 47, "even_fwd": 42, "odd_c_fwd": 37, "mm_out_ln": 33, "ln_bwd": 23, "mm_dmix": 26,
            "mm_dw_out": 25, "even_bwd": 95, "odd_c_bwd": 70, "mm_dw_in_even": 56, "mm_dw_in_odd": 44, "mm_dx_even": 60,
            "mm_dx_odd": 50, "adamw_even_w_in": 30, "adamw_odd_w_in": 28, "adamw_even_w_out": 11, "adamw_odd_w_out": 11}
FWD_OVERBOOK = 1.15
FLUSH_EXTRA_US = 60.0


def _cost_us(task, reg):
    kind, src, _, lo, hi = task
    return ROWS_US[kind] * (hi - lo) * reg[src].shape[-1] * reg[src].dtype.itemsize / 4096.0


class _Copies(_Comm):
    def __init__(self, tasks, reg):
        self.tasks = list(tasks)
        self.out_names, self.in_names = [], []
        for kind, src, dst, lo, hi in self.tasks:
            if dst not in self.out_names:
                self.out_names.append(dst)
        for kind, src, dst, lo, hi in self.tasks:
            if src not in self.out_names and src not in self.in_names:
                self.in_names.append(src)
        self.out_shapes, self.aliases = [], {}
        for o, dst in enumerate(self.out_names):
            if dst in reg:
                self.aliases[len(self.in_names)] = o
                self.in_names.append(dst)
                self.out_shapes.append(jax.ShapeDtypeStruct(reg[dst].shape, reg[dst].dtype))
            else:
                kind, src = next((t[0], t[1]) for t in self.tasks if t[2] == dst)
                shape = ({"rsd": 4, "rs": 3}[kind],) + reg[src].shape[1:]
                self.out_shapes.append(jax.ShapeDtypeStruct(shape, reg[src].dtype))
        self.inputs = [reg[nm] for nm in self.in_names]
        n = sum(N_COPIES[t[0]] for t in self.tasks)
        self.sem_shapes = [pltpu.SemaphoreType.DMA((n,)), pltpu.SemaphoreType.DMA((n,))]
        self.peers = frozenset().union(*[TASK_PEERS[t[0]] for t in self.tasks])

    def copies(self, ins, outs, sems):
        send, recv = sems
        x, y, c = _coords()
        me = 4 * x + 2 * y + c
        xn, yn = (1 - x, y, c), (x, 1 - y, c)
        at_xn, at_yn = 4 * (1 - x) + 2 * y + c, 4 * x + 2 * (1 - y) + c
        ref = dict(zip(self.in_names, ins))
        ref.update(zip(self.out_names, outs))
        res = []

        def copy(src, dst, to):
            i = len(res)
            res.append(pltpu.make_async_remote_copy(src_ref=src, dst_ref=dst, send_sem=send.at[i], recv_sem=recv.at[i],
                                                    device_id=to, device_id_type=MESH))

        for kind, src, dst, lo, hi in self.tasks:
            n = hi - lo
            if kind == "ag1":
                for to in (xn, yn):
                    copy(ref[src].at[pl.ds(lo, n)], ref[dst].at[me, pl.ds(lo, n)], to)
            elif kind == "ag2":
                h = n // 2
                first, second = ref[dst].at[at_xn, pl.ds(lo, h)], ref[dst].at[at_yn, pl.ds(lo + h, n - h)]
                copy(first, first, yn)
                copy(second, second, xn)
            elif kind == "agd":
                for j in range(4):
                    px, py = _chip(j)
                    rows = ref[dst].at[4 * px + 2 * py + c, pl.ds(lo, n)]
                    copy(rows, rows, (x, y, 1 - c))
            elif kind == "rsd":
                for j in range(4):
                    px, py = _chip(j)
                    copy(ref[src].at[4 * px + 2 * py + 1 - c, pl.ds(lo, n)], ref[dst].at[j, pl.ds(lo, n)], (x, y, 1 - c))
            else:
                for j in (1, 2, 3):
                    px, py = _chip(j)
                    copy(ref[src].at[j, pl.ds(lo, n)], ref[dst].at[j - 1, pl.ds(lo, n)], (px, py, c))
        return res


class _Sched:
    def __init__(self, reg):
        self.reg, self.queue, self.later = reg, [], []
        self.overhang = 0.5
        self.after_landing = None

    def add(self, tasks, first=False):
        self.queue = list(tasks) + self.queue if first else self.queue + list(tasks)

    def pending(self, dst):
        return any(t[2] == dst for t in self.queue + self.later)

    def take(self, budget_us, must=None, overhang=0.5):
        self.queue, self.later = self.later + self.queue, []
        picked, us = [], 0.0
        rest = []
        for t in self.queue:
            cost = _cost_us(t, self.reg)
            if (must is not None and t[2] == must) or us + (1.0 - overhang) * cost <= budget_us:
                picked.append(t)
                us += cost
                if t[0] in ("ag1", "ag2"):
                    self.later.append(({"ag1": "ag2", "ag2": "agd"}[t[0]], t[2], t[2], t[3], t[4]))
            else:
                rest.append(t)
        self.queue = rest
        return _Copies(picked, self.reg) if picked else None

    def landed(self, comm, got):
        if comm is not None:
            for nm, a in zip(comm.out_names, got):
                self.reg[nm] = a
        if self.after_landing is not None:
            self.after_landing()

    def run(self, builder, budget_us, *args, **kw):
        comm = self.take(budget_us, overhang=self.overhang)
        res, got = builder(*args, comm=comm, **kw)
        self.landed(comm, got)
        return res

    def flush(self, dst, budget_us=0.0, beside=None):
        res = []
        while self.pending(dst):
            comm = self.take(budget_us, must=dst)
            got = _comm_only(comm if beside is None else _Join([comm, beside]), "flush_" + dst)
            res, beside = got[len(comm.out_shapes):], None
            self.landed(comm, got[:len(comm.out_shapes)])
        return res


def _rows(name_src, name_dst, kind, n_rows, chunk):
    return [(kind, name_src, name_dst, lo, min(lo + chunk, n_rows)) for lo in range(0, n_rows, chunk)]


def _pcall(body, *, grid, in_specs, out_specs, out_shape, name, scratch=(), vmem=48, comm=None):
    in_specs, out_specs, out_shape, scratch = list(in_specs), list(out_specs), list(out_shape), list(scratch)
    if comm is None:
        call = pl.pallas_call(body, grid=grid, in_specs=in_specs, out_specs=out_specs, out_shape=out_shape,
                              scratch_shapes=scratch, name=name, compiler_params=_cp(vmem))
        return lambda *args: (call(*args), [])
    n_in, n_out, n_scr = len(in_specs), len(out_specs), len(scratch)
    c_in, c_out = len(comm.inputs), len(comm.out_shapes)
    aliases = {n_in + i: n_out + o for i, o in getattr(comm, "aliases", {}).items()}

    def wrapped(*refs):
        ins, cins = refs[:n_in], refs[n_in:n_in + c_in]
        o0 = n_in + c_in
        outs, couts = refs[o0:o0 + n_out], refs[o0 + n_out:o0 + n_out + c_out]
        s0 = o0 + n_out + c_out
        scr, sems = refs[s0:s0 + n_scr], refs[s0 + n_scr:]
        ids = [pl.program_id(a) for a in range(len(grid))]
        first = functools.reduce(jnp.logical_and, [i == 0 for i in ids])
        last = functools.reduce(jnp.logical_and, [i == g - 1 for i, g in zip(ids, grid)])

        @pl.when(first)
        def _():
            comm.start(cins, couts, sems)

        body(*ins, *outs, *scr)

        @pl.when(last)
        def _():
            comm.wait(cins, couts, sems)

    call = pl.pallas_call(wrapped, grid=grid, in_specs=in_specs + [ANY] * c_in, out_specs=out_specs + [ANY] * c_out,
                          out_shape=out_shape + list(comm.out_shapes), scratch_shapes=scratch + list(comm.sem_shapes),
                          input_output_aliases=aliases, name=name, compiler_params=_cp(vmem, comm.collective_id()))

    def run(*args):
        res = call(*args, *comm.inputs)
        return res[:n_out], res[n_out:]

    return run


def _comm_only(comm, name):
    c_in, c_out = len(comm.inputs), len(comm.out_shapes)

    def body(*refs):
        cins, couts, sems = refs[:c_in], refs[c_in:c_in + c_out], refs[c_in + c_out:]
        comm.start(cins, couts, sems)
        comm.wait(cins, couts, sems)

    return pl.pallas_call(body, in_specs=[ANY] * c_in, out_specs=[ANY] * c_out, out_shape=list(comm.out_shapes),
                          scratch_shapes=list(comm.sem_shapes), input_output_aliases=dict(getattr(comm, "aliases", {})),
                          name=name, compiler_params=pltpu.CompilerParams(collective_id=comm.collective_id()))(*comm.inputs)


def _chip_blocks():
    _, _, c = _coords()
    return jnp.stack([4 * px + 2 * py + c for px, py in map(_chip, range(4))]).astype(jnp.int32)


def _add_pairs(g8, b4, name):
    _, R, C = b4.shape

    def body(idx_ref, a_ref, b_ref, o_ref):
        o_ref[...] = (a_ref[...].astype(F32) + b_ref[...].astype(F32)).astype(BF)

    blk = pl.BlockSpec((None, R, C), lambda j, idx: (j, 0, 0))
    grid_spec = pltpu.PrefetchScalarGridSpec(
        num_scalar_prefetch=1, grid=(4,),
        in_specs=[pl.BlockSpec((None, R, C), lambda j, idx: (idx[j], 0, 0)), blk], out_specs=blk)
    return pl.pallas_call(body, grid_spec=grid_spec, out_shape=jax.ShapeDtypeStruct(b4.shape, BF), name=name,
                          compiler_params=_cp())(_chip_blocks(), g8, b4)


def _rs_final(s4, r3, name):
    _, R, C = s4.shape
    tr = R // 2

    def body(s_ref, r_ref, o_ref):
        o_ref[...] = ((s_ref[...].astype(F32) + r_ref[0].astype(F32)) + r_ref[1].astype(F32)) + r_ref[2].astype(F32)

    return pl.pallas_call(
        body, grid=(2,),
        in_specs=[pl.BlockSpec((None, tr, C), lambda i: (0, i, 0)), pl.BlockSpec((3, tr, C), lambda i: (0, i, 0))],
        out_specs=pl.BlockSpec((tr, C), lambda i: (i, 0)), out_shape=jax.ShapeDtypeStruct((R, C), F32),
        name=name, compiler_params=_cp())(s4, r3)


def _mm_nt(a, w, tm, tn, name, out3=False, comm=None):
    M, K = a.shape
    N = w.shape[0]
    tm = min(tm, M)

    def body(a_ref, w_ref, o_ref):
        o_ref[...] = _dot_nt(a_ref[...], w_ref[...])

    if out3:
        per = W // tn
        out_shape = jax.ShapeDtypeStruct((N // W, M, W), F32)
        out_spec = pl.BlockSpec((None, tm, tn), lambda i, j: (j // per, i, j % per))
    else:
        out_shape = jax.ShapeDtypeStruct((M, N), F32)
        out_spec = pl.BlockSpec((tm, tn), lambda i, j: (i, j))
    (res,), extra = _pcall(
        body, grid=(M // tm, N // tn),
        in_specs=[pl.BlockSpec((tm, K), lambda i, j: (i, 0)), pl.BlockSpec((tn, K), lambda i, j: (j, 0))],
        out_specs=[out_spec], out_shape=[out_shape], name=name, comm=comm)(a, w)
    return res, extra


def _mm_tn(a, b, tm, name, comm=None):
    K, N = b.shape
    if a.ndim == 3:
        M = a.shape[0] * W
        per = W // tm
        a_spec = pl.BlockSpec((None, K, tm), lambda i: (i // per, 0, i % per))
    else:
        M = a.shape[1]
        a_spec = pl.BlockSpec((K, tm), lambda i: (0, i))

    def body(a_ref, b_ref, o_ref):
        o_ref[...] = _dot_tn(a_ref[...], b_ref[...]).astype(BF)

    (out,), extra = _pcall(
        body, grid=(M // tm,),
        in_specs=[a_spec, pl.BlockSpec((K, N), lambda i: (0, 0))],
        out_specs=[pl.BlockSpec((tm, N), lambda i: (i, 0))],
        out_shape=[jax.ShapeDtypeStruct((M, N), BF)], name=name, vmem=56, comm=comm)(a, b)
    return out, extra


def _mm_nn_res(a, w, res, tm, tn, name, comm=None):
    K, N = w.shape
    if a.ndim == 3:
        P, M = a.shape[0], a.shape[1]
        tm = min(tm, M)
        a_spec = pl.BlockSpec((P, tm, W), lambda j, i: (0, i, 0))
    else:
        P, M = 0, a.shape[0]
        tm = min(tm, M)
        a_spec = pl.BlockSpec((tm, K), lambda j, i: (i, 0))

    def body(a_ref, w_ref, r_ref, o_ref):
        if P:
            d = _dot(a_ref[0], w_ref[0:W, :])
            for p in range(1, P):
                d = d + _dot(a_ref[p], w_ref[p * W:(p + 1) * W, :])
        else:
            d = _dot(a_ref[...], w_ref[...])
        o_ref[...] = ALPHA * r_ref[...] + d

    (out,), extra = _pcall(
        body, grid=(N // tn, M // tm),
        in_specs=[a_spec, pl.BlockSpec((K, tn), lambda j, i: (0, j)), pl.BlockSpec((tm, tn), lambda j, i: (i, j))],
        out_specs=[pl.BlockSpec((tm, tn), lambda j, i: (i, j))],
        out_shape=[jax.ShapeDtypeStruct((M, N), F32)], name=name, comm=comm)(a, w, res)
    return out, extra


def _mm_out_ln(mix3, w_out, x, g, b, name, comm=None):
    S = x.shape[0]
    tm = min(256, S)

    def body(m_ref, w_ref, x_ref, g_ref, b_ref, z_ref, xn_ref, xb_ref):
        acc = _dot(m_ref[0], w_ref[0:W, :]) + _dot(m_ref[1], w_ref[W:2 * W, :])
        z = ALPHA * x_ref[...] + acc
        mu = jnp.mean(z, axis=1, keepdims=True)
        zc = z - mu
        var = jnp.mean(zc * zc, axis=1, keepdims=True)
        xn = zc * lax.rsqrt(var + LN_EPS) * g_ref[...] + b_ref[...]
        z_ref[...] = z
        xn_ref[...] = xn
        xb_ref[...] = xn.astype(BF)

    row = pl.BlockSpec((tm, D), lambda i: (i, 0))
    vec = pl.BlockSpec((1, D), lambda i: (0, 0))
    return _pcall(
        body, grid=(S // tm,),
        in_specs=[pl.BlockSpec((2, tm, W), lambda i: (0, i, 0)), pl.BlockSpec((D, D), lambda i: (0, 0)), row, vec, vec],
        out_specs=[row, row, row],
        out_shape=[jax.ShapeDtypeStruct((S, D), F32), jax.ShapeDtypeStruct((S, D), F32), jax.ShapeDtypeStruct((S, D), BF)],
        name=name, comm=comm)(mix3, w_out, x, g.reshape(1, D), b.reshape(1, D))


def _ln_bwd(dxn, z, g, name, comm=None, target=None):
    S = z.shape[0]
    tm = min(256, S)
    head = target is not None

    def body(*refs):
        if head:
            d_ref, t_ref, z_ref, g_ref, dz_ref, dzb_ref, dg_ref, db_ref, p_ref = refs
        else:
            d_ref, z_ref, g_ref, dz_ref, dzb_ref, dg_ref, db_ref = refs
        i = pl.program_id(0)
        zz = z_ref[...]
        mu = jnp.mean(zz, axis=1, keepdims=True)
        zc = zz - mu
        var = jnp.mean(zc * zc, axis=1, keepdims=True)
        rstd = lax.rsqrt(var + LN_EPS)
        xhat = zc * rstd
        dy = d_ref[...]
        if head:
            e = dy - t_ref[...]
            dy = e * (1.0 / D)

            @pl.when(i == 0)
            def _():
                p_ref[...] = jnp.zeros_like(p_ref)

            p_ref[...] += jnp.sum(jnp.sum(e * e, axis=1, keepdims=True), axis=0, keepdims=True)
        dyg = dy * g_ref[...]
        m1 = jnp.mean(dyg, axis=1, keepdims=True)
        m2 = jnp.mean(dyg * xhat, axis=1, keepdims=True)
        dz = rstd * (dyg - m1 - xhat * m2)
        dz_ref[...] = dz
        dzb_ref[...] = dz.astype(BF)

        @pl.when(i == 0)
        def _():
            dg_ref[...] = jnp.zeros_like(dg_ref)
            db_ref[...] = jnp.zeros_like(db_ref)

        dg_ref[...] += jnp.sum(dy * xhat, axis=0, keepdims=True)
        db_ref[...] += jnp.sum(dy, axis=0, keepdims=True)

    row = pl.BlockSpec((tm, D), lambda i: (i, 0))
    vec = pl.BlockSpec((1, D), lambda i: (0, 0))
    out_specs = [row, row, vec, vec] + ([pl.BlockSpec((8, 128), lambda i: (0, 0))] if head else [])
    out_shape = [jax.ShapeDtypeStruct((S, D), F32), jax.ShapeDtypeStruct((S, D), BF), jax.ShapeDtypeStruct((1, D), F32),
                 jax.ShapeDtypeStruct((1, D), F32)] + ([jax.ShapeDtypeStruct((8, 128), F32)] if head else [])
    operands = (dxn, target, z, g.reshape(1, D)) if head else (dxn, z, g.reshape(1, D))
    return _pcall(body, grid=(S // tm,), in_specs=[row] * (len(operands) - 1) + [vec], out_specs=out_specs,
                  out_shape=out_shape, name=name, comm=comm)(*operands)


def _rope_fwd(t, r_ref):
    return (t * r_ref[:, 0:128] + pltpu.roll(t, 120, 1) * r_ref[:, 128:256]
            + pltpu.roll(t, 8, 1) * r_ref[:, 256:384])


def _rope_bwd(g, r_ref):
    return (g * r_ref[:, 0:128] + pltpu.roll(g * r_ref[:, 128:256], 8, 1)
            + pltpu.roll(g * r_ref[:, 256:384], 120, 1))


def _dup_heads(kb):
    lo = lax.broadcasted_iota(jnp.int32, kb.shape, 1) < 64
    sw = pltpu.roll(kb, 64, 1)
    return [jnp.where(lo, kb, sw).astype(BF), jnp.where(lo, sw, kb).astype(BF)]


def _even_fwd(h, rope, lng, lnb, ws, bsb, sinks, name, comm=None):
    S = h.shape[0]
    nb = S // CHUNK

    def body(h_ref, hp_ref, rc_ref, rp_ref, lng_ref, lnb_ref, ws_ref, bsb_ref, sink_ref, mix_ref, o_ref, l_ref):
        n = pl.program_id(0)
        lane = lax.broadcasted_iota(jnp.int32, (128, 128), 1)
        rowi = lax.broadcasted_iota(jnp.int32, (128, 128), 0)
        tri = rowi >= lane
        lane_lo = lane < 64
        v = h_ref[:, W:2 * W]
        mu = jnp.mean(v, axis=1, keepdims=True)
        vc = v - mu
        var = jnp.mean(vc * vc, axis=1, keepdims=True)
        vn = vc * lax.rsqrt(var + LN_EPS) * lng_ref[...] + lnb_ref[...]
        ms = [_dot(jnp.where(tri, ws_ref[g], 0.0).astype(BF), vn[:, g * 128:(g + 1) * 128].astype(BF)) for g in range(8)]
        for g in range(8):
            sl = slice(g * 128, (g + 1) * 128)
            ag = h_ref[:, 2 * W + g * 128:2 * W + (g + 1) * 128]
            mix_ref[0, :, sl] = (h_ref[:, sl] * (ms[g] + bsb_ref[g]) * (ag * _sig(ag))).astype(BF)
        kb = jnp.concatenate([_rope_fwd(hp_ref[:, 0:128], rp_ref), _rope_fwd(h_ref[:, 4096:4224], rc_ref)], axis=0)
        vb = jnp.concatenate([hp_ref[:, 128:256], h_ref[:, 4224:4352]], axis=0)
        k2 = _dup_heads(kb)
        v2 = _dup_heads(vb)
        qi = lax.broadcasted_iota(jnp.int32, (128, 256), 0)
        kj = lax.broadcasted_iota(jnp.int32, (128, 256), 1)
        diff = qi + 128 - kj
        valid = (diff >= 0) & (diff < 128) & ((n > 0) | (kj >= 128))
        lacc = jnp.zeros((128, 128), F32)
        for j0 in range(0, 8, HEAD_COLS):
            heads = [(j, half) for j in range(j0, j0 + HEAD_COLS) for half in range(2)]
            sc, pr, oh = {}, {}, {}
            for j in range(j0, j0 + HEAD_COLS):
                qc = _rope_fwd(h_ref[:, 3072 + j * 128:3072 + (j + 1) * 128], rc_ref)
                sc[j, 0] = _dot_nt(jnp.where(lane_lo, qc, 0.0).astype(BF), k2[j // 4])
                sc[j, 1] = _dot_nt(jnp.where(lane_lo, 0.0, qc).astype(BF), k2[j // 4])
            for j, half in heads:
                hq = 2 * j + half
                s = jnp.where(valid, sc[j, half] * 0.125, NEG)
                sk = sink_ref[hq]
                mx = jnp.maximum(jnp.max(s, axis=1, keepdims=True), sk)
                p = jnp.exp(s - mx)
                den = jnp.sum(p, axis=1, keepdims=True) + jnp.exp(sk - mx)
                pr[j, half] = (p / den).astype(BF)
                lacc = jnp.where(lane == hq, mx + jnp.log(den), lacc)
            for j, half in heads:
                oh[j, half] = _dot(pr[j, half], v2[j // 4])
            for j in range(j0, j0 + HEAD_COLS):
                cs = slice(j * 128, (j + 1) * 128)
                ocol = jnp.where(lane_lo, oh[j, 0], oh[j, 1])
                bg = h_ref[:, 4352 + j * 128:4352 + (j + 1) * 128]
                o_ref[:, cs] = ocol
                mix_ref[1, :, cs] = (ocol * (bg * _sig(bg))).astype(BF)
        l_ref[...] = lacc

    prev = lambda n: jnp.maximum(n - 1, 0)
    full = lambda shape: pl.BlockSpec(shape, lambda n: (0,) * len(shape))
    return _pcall(
        body, grid=(nb,),
        in_specs=[pl.BlockSpec((CHUNK, EVEN_IN), lambda n: (n, 0)),
                  pl.BlockSpec((CHUNK, 256), lambda n: (prev(n), 16)),
                  pl.BlockSpec((CHUNK, 384), lambda n: (n, 0)),
                  pl.BlockSpec((CHUNK, 384), lambda n: (prev(n), 0)),
                  full((1, W)), full((1, W)), full((8, 128, 128)), full((8, 128, 128)),
                  pl.BlockSpec(memory_space=pltpu.SMEM)],
        out_specs=[pl.BlockSpec((2, CHUNK, W), lambda n: (0, n, 0)),
                   pl.BlockSpec((CHUNK, W), lambda n: (n, 0)),
                   pl.BlockSpec((CHUNK, 128), lambda n: (n, 0))],
        out_shape=[jax.ShapeDtypeStruct((2, S, W), BF), jax.ShapeDtypeStruct((S, W), F32),
                   jax.ShapeDtypeStruct((S, 128), F32)],
        name=name, comm=comm)(h, h, rope, rope, lng.reshape(1, W), lnb.reshape(1, W), ws, bsb, sinks)


def _even_bwd(h, dmix3, o, l, rope, lng, lnb, ws, wst, bsb, sinks, name, comm=None):
    S = h.shape[0]
    nb = S // CHUNK

    def body(h_ref, hp_ref, hn_ref, dm_ref, dmn_ref, o_ref, on_ref, l_ref, ln_ref, rc_ref, rp_ref, rn_ref,
             lng_ref, lnb_ref, ws_ref, wst_ref, bsb_ref, sink_ref,
             dh_ref, dws_ref, dbs_ref, dlng_ref, dlnb_ref, dsink_ref, dvn_ref):
        n = pl.program_id(0)

        @pl.when(n == 0)
        def _():
            dws_ref[...] = jnp.zeros_like(dws_ref)
            dbs_ref[...] = jnp.zeros_like(dbs_ref)
            dlng_ref[...] = jnp.zeros_like(dlng_ref)
            dlnb_ref[...] = jnp.zeros_like(dlnb_ref)
            dsink_ref[...] = jnp.zeros_like(dsink_ref)

        lane = lax.broadcasted_iota(jnp.int32, (128, 128), 1)
        rowi = lax.broadcasted_iota(jnp.int32, (128, 128), 0)
        lane1 = lax.broadcasted_iota(jnp.int32, (1, 128), 1)
        tri = rowi >= lane
        tri_t = lane >= rowi
        lane_lo = lane < 64
        v = h_ref[:, W:2 * W]
        mu = jnp.mean(v, axis=1, keepdims=True)
        vc = v - mu
        var = jnp.mean(vc * vc, axis=1, keepdims=True)
        rstd = lax.rsqrt(var + LN_EPS)
        vhat = vc * rstd
        vn = vhat * lng_ref[...] + lnb_ref[...]
        dbs_acc = jnp.zeros((128, 128), F32)
        vng = [vn[:, g * 128:(g + 1) * 128].astype(BF) for g in range(8)]
        ms = [_dot(jnp.where(tri, ws_ref[g], 0.0).astype(BF), vng[g]) for g in range(8)]
        dmb = []
        for g in range(8):
            sl = slice(g * 128, (g + 1) * 128)
            m = ms[g] + bsb_ref[g]
            ag = h_ref[:, 2 * W + g * 128:2 * W + (g + 1) * 128]
            sg, dsg = _silu_grad(ag)
            u = h_ref[:, sl]
            da = dm_ref[0, :, sl]
            dmm = da * u * sg
            dh_ref[:, sl] = (da * m * sg).astype(BF)
            dh_ref[:, 2 * W + g * 128:2 * W + (g + 1) * 128] = (da * u * m * dsg).astype(BF)
            dmb.append(dmm.astype(BF))
            dbs_acc = jnp.where(lane == g, jnp.sum(dmm, axis=1, keepdims=True), dbs_acc)
        dvs = [_dot(jnp.where(tri_t, wst_ref[g], 0.0).astype(BF), dmb[g]) for g in range(8)]
        dwss = [_dot_nt(dmb[g], vng[g]) for g in range(8)]
        for g in range(8):
            dvn_ref[:, g * 128:(g + 1) * 128] = dvs[g]
            dws_ref[g] += jnp.where(tri, dwss[g], 0.0)
        dbs_ref[...] += dbs_acc
        dvn = dvn_ref[...]
        dlng_ref[...] += jnp.sum(dvn * vhat, axis=0, keepdims=True)
        dlnb_ref[...] += jnp.sum(dvn, axis=0, keepdims=True)
        dyg = dvn * lng_ref[...]
        m1 = jnp.mean(dyg, axis=1, keepdims=True)
        m2 = jnp.mean(dyg * vhat, axis=1, keepdims=True)
        dh_ref[:, W:2 * W] = (rstd * (dyg - m1 - vhat * m2)).astype(BF)
        kcur = _rope_fwd(h_ref[:, 4096:4224], rc_ref)
        kb = jnp.concatenate([_rope_fwd(hp_ref[:, 0:128], rp_ref), kcur], axis=0)
        vb = jnp.concatenate([hp_ref[:, 128:256], h_ref[:, 4224:4352]], axis=0)
        k2 = _dup_heads(kb)
        v2 = _dup_heads(vb)
        kc2 = _dup_heads(kcur)
        vc2 = _dup_heads(h_ref[:, 4224:4352])
        qi = lax.broadcasted_iota(jnp.int32, (128, 256), 0)
        kj = lax.broadcasted_iota(jnp.int32, (128, 256), 1)
        diff = qi + 128 - kj
        valid = (diff >= 0) & (diff < 128) & ((n > 0) | (kj >= 128))
        validn = (lane > rowi) & (n < nb - 1)
        lc = l_ref[...]
        lnx = ln_ref[...]
        dk = [jnp.zeros((128, 128), F32), jnp.zeros((128, 128), F32)]
        dv = [jnp.zeros((128, 128), F32), jnp.zeros((128, 128), F32)]
        dsk_acc = jnp.zeros((1, 128), F32)
        for j0 in range(0, 8, HEAD_COLS):
            heads = [(j, half) for j in range(j0, j0 + HEAD_COLS) for half in range(2)]
            t = {}
            for j in range(j0, j0 + HEAD_COLS):
                cs = slice(j * 128, (j + 1) * 128)
                qc = _rope_fwd(h_ref[:, 3072 + j * 128:3072 + (j + 1) * 128], rc_ref)
                qn = _rope_fwd(hn_ref[:, 3072 + j * 128:3072 + (j + 1) * 128], rn_ref)
                bg = h_ref[:, 4352 + j * 128:4352 + (j + 1) * 128]
                sgb, dsgb = _silu_grad(bg)
                db = dm_ref[1, :, cs]
                oc = o_ref[:, cs]
                do = db * sgb
                dh_ref[:, 4352 + j * 128:4352 + (j + 1) * 128] = (db * oc * dsgb).astype(BF)
                bgn = hn_ref[:, 4352 + j * 128:4352 + (j + 1) * 128]
                don = dmn_ref[1, :, cs] * (bgn * _sig(bgn))
                prod = do * oc
                prodn = don * on_ref[:, cs]
                for half in range(2):
                    hq = 2 * j + half
                    hm = lane_lo if half == 0 else jnp.logical_not(lane_lo)
                    t[j, half] = dict(
                        dsum=jnp.sum(jnp.where(hm, prod, 0.0), axis=1, keepdims=True),
                        dsumn=jnp.sum(jnp.where(hm, prodn, 0.0), axis=1, keepdims=True),
                        lh=jnp.sum(jnp.where(lane == hq, lc, 0.0), axis=1, keepdims=True),
                        lhn=jnp.sum(jnp.where(lane == hq, lnx, 0.0), axis=1, keepdims=True),
                        qm=jnp.where(hm, qc, 0.0).astype(BF), dom=jnp.where(hm, do, 0.0).astype(BF),
                        qnm=jnp.where(hm, qn, 0.0).astype(BF), donm=jnp.where(hm, don, 0.0).astype(BF))
            for j, half in heads:
                e, hk = t[j, half], j // 4
                e["s"], e["dp"] = _dot_nt(e["qm"], k2[hk]), _dot_nt(e["dom"], v2[hk])
                e["sn"], e["dpn"] = _dot_nt(e["qnm"], kc2[hk]), _dot_nt(e["donm"], vc2[hk])
            for j, half in heads:
                e, hq = t[j, half], 2 * j + half
                p = jnp.exp(jnp.where(valid, e["s"] * 0.125 - e["lh"], NEG))
                ds = p * (e["dp"] - e["dsum"])
                pn = jnp.exp(jnp.where(validn, e["sn"] * 0.125 - e["lhn"], NEG))
                dsn = pn * (e["dpn"] - e["dsumn"])
                psink = jnp.exp(sink_ref[hq] - e["lh"])
                dsk_acc = jnp.where(lane1 == hq, -jnp.sum(psink * e["dsum"], axis=0, keepdims=True), dsk_acc)
                e["ds"] = ds.astype(BF)
                e["pt"], e["dst"] = jnp.transpose(p[:, 128:256]).astype(BF), jnp.transpose(ds[:, 128:256]).astype(BF)
                e["pnt"], e["dsnt"] = jnp.transpose(pn).astype(BF), jnp.transpose(dsn).astype(BF)
            for j, half in heads:
                e, hk = t[j, half], j // 4
                e["dq"] = _dot(e["ds"], k2[hk])
                e["dv"] = _dot(e["pt"], e["dom"]) + _dot(e["pnt"], e["donm"])
                e["dk"] = _dot(e["dst"], e["qm"]) + _dot(e["dsnt"], e["qnm"])
            for j in range(j0, j0 + HEAD_COLS):
                hk = j // 4
                dqcol = jnp.where(lane_lo, t[j, 0]["dq"], t[j, 1]["dq"]) * 0.125
                dh_ref[:, 3072 + j * 128:3072 + (j + 1) * 128] = _rope_bwd(dqcol, rc_ref).astype(BF)
                dv[hk] = dv[hk] + t[j, 0]["dv"] + t[j, 1]["dv"]
                dk[hk] = dk[hk] + (t[j, 0]["dk"] + t[j, 1]["dk"]) * 0.125
        fold = lambda a: a + pltpu.roll(a, 64, 1)
        dh_ref[:, 4096:4224] = _rope_bwd(jnp.where(lane_lo, fold(dk[0]), fold(dk[1])), rc_ref).astype(BF)
        dh_ref[:, 4224:4352] = jnp.where(lane_lo, fold(dv[0]), fold(dv[1])).astype(BF)
        dsink_ref[...] += dsk_acc

    prev = lambda n: jnp.maximum(n - 1, 0)
    nxt = lambda n: jnp.minimum(n + 1, nb - 1)
    full = lambda shape: pl.BlockSpec(shape, lambda n: (0,) * len(shape))
    return _pcall(
        body, grid=(nb,),
        in_specs=[pl.BlockSpec((CHUNK, EVEN_IN), lambda n: (n, 0)),
                  pl.BlockSpec((CHUNK, 256), lambda n: (prev(n), 16)),
                  pl.BlockSpec((CHUNK, EVEN_IN), lambda n: (nxt(n), 0)),
                  pl.BlockSpec((2, CHUNK, W), lambda n: (0, n, 0)),
                  pl.BlockSpec((2, CHUNK, W), lambda n: (0, nxt(n), 0)),
                  pl.BlockSpec((CHUNK, W), lambda n: (n, 0)),
                  pl.BlockSpec((CHUNK, W), lambda n: (nxt(n), 0)),
                  pl.BlockSpec((CHUNK, 128), lambda n: (n, 0)),
                  pl.BlockSpec((CHUNK, 128), lambda n: (nxt(n), 0)),
                  pl.BlockSpec((CHUNK, 384), lambda n: (n, 0)),
                  pl.BlockSpec((CHUNK, 384), lambda n: (prev(n), 0)),
                  pl.BlockSpec((CHUNK, 384), lambda n: (nxt(n), 0)),
                  full((1, W)), full((1, W)), full((8, 128, 128)), full((8, 128, 128)), full((8, 128, 128)),
                  pl.BlockSpec(memory_space=pltpu.SMEM)],
        out_specs=[pl.BlockSpec((CHUNK, EVEN_IN), lambda n: (n, 0)),
                   full((8, 128, 128)), full((128, 128)), full((1, W)), full((1, W)), full((1, 128))],
        out_shape=[jax.ShapeDtypeStruct((S, EVEN_IN), BF), jax.ShapeDtypeStruct((8, 128, 128), F32),
                   jax.ShapeDtypeStruct((128, 128), F32), jax.ShapeDtypeStruct((1, W), F32),
                   jax.ShapeDtypeStruct((1, W), F32), jax.ShapeDtypeStruct((1, 128), F32)],
        scratch=[pltpu.VMEM((CHUNK, W), F32)], name=name, comm=comm,
    )(h, h, h, dmix3, dmix3, o, o, l, l, rope, rope, rope, lng.reshape(1, W), lnb.reshape(1, W), ws, wst, bsb, sinks)


def _expm1(x):
    ser = x * (1.0 + x * (0.5 + x * (1.0 / 6.0 + x * (1.0 / 24.0))))
    return jnp.where(jnp.abs(x) < 1e-2, ser, jnp.exp(x) - 1.0)


def _softplus_neg(lam):
    z = -lam
    e = jnp.exp(-jnp.abs(z))
    l1p = jnp.where(e < 1e-3, e * (1.0 - e * (0.5 - e * (1.0 / 3.0))), jnp.log(1.0 + e))
    return jnp.maximum(z, 0.0) + l1p


def _shift_down(x, k, row, fill=0.0):
    return jnp.where(row >= k, pltpu.roll(x, k, 0), fill)


def _shift_up(x, k, row, fill=0.0):
    S = x.shape[0]
    return jnp.where(row < S - k, pltpu.roll(x, S - k, 0), fill)


def _lru_gates(xc, row, cw_ref, cb_ref, wa_ref, wx_ref, ba_ref, bx_ref, lam_ref):
    xconv = (cw_ref[3:4, :] * xc + cw_ref[2:3, :] * _shift_down(xc, 1, row) + cw_ref[1:2, :] * _shift_down(xc, 2, row)
             + cw_ref[0:1, :] * _shift_down(xc, 3, row) + cb_ref[...])
    xb = xconv.astype(BF)
    r = _sig(_dot(xb, wa_ref[...]) + ba_ref[...])
    i = _sig(_dot(xb, wx_ref[...]) + bx_ref[...])
    sp = _softplus_neg(lam_ref[...])
    log_a = -LRU_C * r * sp
    a = jnp.exp(log_a)
    mult = jnp.sqrt(-_expm1(2.0 * log_a))
    return xconv, r, i, sp, a, mult


ROWS_PER_TILE = 8


def _steps(a, b, shift, inside, products=True):
    n, k = inside.n, 1
    while k < n:
        b = a * jnp.where(inside(k), shift(b, k), 0.0) + b
        if products or 2 * k < n:
            a = a * jnp.where(inside(k), shift(a, k), 1.0)
        k *= 2
    return a, b


class _Inside:
    def __init__(self, pos, n, reverse):
        self.pos, self.n, self.reverse = pos, n, reverse

    def __call__(self, k):
        return self.pos < self.n - k if self.reverse else self.pos >= k


def _scan_rows(a, b, row, a_ref, b_ref, c_ref, reverse=False):
    S = a.shape[0]
    G = S // ROWS_PER_TILE
    if reverse:
        shift = lambda x, k: pltpu.roll(x, x.shape[0] - k, 0)
    else:
        shift = lambda x, k: pltpu.roll(x, k, 0)
    a, b = _steps(a, b, shift, _Inside(row % ROWS_PER_TILE, ROWS_PER_TILE, reverse))
    a_ref[...] = a
    b_ref[...] = b
    last = 0 if reverse else ROWS_PER_TILE - 1
    grow = lax.broadcasted_iota(jnp.int32, (G, a.shape[1]), 0)
    _, tot = _steps(a_ref[pl.ds(last, G, stride=ROWS_PER_TILE), :], b_ref[pl.ds(last, G, stride=ROWS_PER_TILE), :],
                    shift, _Inside(grow, G, reverse), products=False)
    enters = jnp.where(_Inside(grow, G, reverse)(1), shift(tot, 1), 0.0)
    for r in range(ROWS_PER_TILE):
        c_ref[pl.ds(r, G, stride=ROWS_PER_TILE), :] = enters
    return b + a * c_ref[...]


def _odd_c_fwd(h, cw, cb, wa, wx, ba, bx, lam, name, comm=None):
    S = h.shape[0]

    def body(xc_ref, cg_ref, cw_ref, cb_ref, wa_ref, wx_ref, ba_ref, bx_ref, lam_ref, mix_ref, hst_ref, sa_ref, sb_ref, sc_ref):
        row = lax.broadcasted_iota(jnp.int32, (S, 128), 0)
        xconv, r, i, sp, a, mult = _lru_gates(xc_ref[...], row, cw_ref, cb_ref, wa_ref, wx_ref, ba_ref, bx_ref, lam_ref)
        bb = _scan_rows(a, mult * (i * xconv), row, sa_ref, sb_ref, sc_ref)
        hst_ref[...] = bb
        cg = cg_ref[...]
        mix_ref[...] = (bb * (cg * _sig(cg))).astype(BF)

    col = lambda off: pl.BlockSpec((S, 128), lambda j: (0, off + j))
    vec = pl.BlockSpec((1, 128), lambda j: (0, j))
    mat = pl.BlockSpec((None, 128, 128), lambda j: (j, 0, 0))
    return _pcall(
        body, grid=(8,),
        in_specs=[col(0), col(8), pl.BlockSpec((4, 128), lambda j: (0, j)), vec, mat, mat, vec, vec, vec],
        out_specs=[pl.BlockSpec((None, S, 128), lambda j: (0, 0, j)), pl.BlockSpec((S, 128), lambda j: (0, j))],
        out_shape=[jax.ShapeDtypeStruct((2, S, W), BF), jax.ShapeDtypeStruct((S, W), F32)],
        scratch=[pltpu.VMEM((S, 128), F32)] * 3, name=name, comm=comm,
    )(h, h, cw, cb.reshape(1, W), wa, wx, ba.reshape(1, W), bx.reshape(1, W), lam.reshape(1, W))


def _pool_sums(x, g, row, shift):
    s2 = x + shift(x, 1, row)
    s4 = s2 + shift(s2, 2, row)
    s8 = s4 + shift(s4, 4, row)
    s16 = s8 + shift(s8, 8, row)
    return jnp.where(g == 0, s2, jnp.where(g == 1, s4, jnp.where(g == 2, s8, s16)))


def _odd_d_fwd(h, mix3, wp, dscale, name):
    S = h.shape[0]

    def body(xd_ref, dg_ref, wp_ref, ds_ref, mix_in, mix_ref):
        g = pl.program_id(0)
        row = lax.broadcasted_iota(jnp.int32, (S, 256), 0)
        xd = xd_ref[...]
        cnt = jnp.minimum(row + 1, jnp.left_shift(2, g)).astype(F32)
        pooled = _pool_sums(xd, g, row, _shift_down) / cnt - xd
        mixed = _dot(pooled.astype(BF), wp_ref[...])
        dg = dg_ref[...]
        mix_ref[...] = (mixed * ds_ref[...] * (dg * _sig(dg))).astype(BF)

    col = lambda off: pl.BlockSpec((S, 256), lambda g: (0, off + g))
    return pl.pallas_call(
        body, grid=(4,),
        in_specs=[col(8), col(12), pl.BlockSpec((None, 256, 256), lambda g: (g, 0, 0)),
                  pl.BlockSpec((1, 256), lambda g: (0, g)), ANY],
        out_specs=pl.BlockSpec((None, S, 256), lambda g: (1, 0, g)),
        out_shape=jax.ShapeDtypeStruct((2, S, W), BF), input_output_aliases={4: 0},
        name=name, compiler_params=_cp(),
    )(h, h, wp, dscale.reshape(1, W), mix3)


def _odd_c_bwd(h, hst, dmix3, cw, cb, wa, wx, wat, wxt, ba, bx, lam, name, comm=None):
    S = h.shape[0]

    def body(xc_ref, cg_ref, hst_ref, dc_ref, cw_ref, cb_ref, wa_ref, wx_ref, wat_ref, wxt_ref, ba_ref, bx_ref, lam_ref,
             dh_ref, dcw_ref, dcb_ref, dwa_ref, dwx_ref, dba_ref, dbx_ref, dlam_ref, sa_ref, sb_ref, sc_ref):
        row = lax.broadcasted_iota(jnp.int32, (S, 128), 0)
        xc = xc_ref[...]
        xconv, r, i, sp, a, mult = _lru_gates(xc, row, cw_ref, cb_ref, wa_ref, wx_ref, ba_ref, bx_ref, lam_ref)
        hst = hst_ref[...]
        cg = cg_ref[...]
        sg, dsg = _silu_grad(cg)
        dc = dc_ref[...]
        dh_ref[1] = (dc * hst * dsg).astype(BF)
        lam_t = _scan_rows(_shift_up(a, 1, row), dc * sg, row, sa_ref, sb_ref, sc_ref, reverse=True)
        da = lam_t * _shift_down(hst, 1, row)
        ix = i * xconv
        dmult = lam_t * ix
        di = lam_t * mult * xconv
        dxconv = lam_t * mult * i
        dlog_a = da * a - dmult * (a * a / mult)
        dr = dlog_a * (-LRU_C * sp)
        dsp = jnp.sum(dlog_a * (-LRU_C * r), axis=0, keepdims=True)
        dlam_ref[...] = dsp * (-_sig(-lam_ref[...]))
        dpa = dr * r * (1.0 - r)
        dpx = di * i * (1.0 - i)
        dpab = dpa.astype(BF)
        dpxb = dpx.astype(BF)
        xb = xconv.astype(BF)
        dxconv = dxconv + _dot(dpab, wat_ref[...]) + _dot(dpxb, wxt_ref[...])
        dwa_ref[...] = _dot_tn(xb, dpab)
        dwx_ref[...] = _dot_tn(xb, dpxb)
        dba_ref[...] = jnp.sum(dpa, axis=0, keepdims=True)
        dbx_ref[...] = jnp.sum(dpx, axis=0, keepdims=True)
        dh_ref[0] = (cw_ref[3:4, :] * dxconv + cw_ref[2:3, :] * _shift_up(dxconv, 1, row)
                     + cw_ref[1:2, :] * _shift_up(dxconv, 2, row) + cw_ref[0:1, :] * _shift_up(dxconv, 3, row)).astype(BF)
        for j in range(4):
            src = xc if j == 3 else _shift_down(xc, 3 - j, row)
            dcw_ref[j:j + 1, :] = jnp.sum(dxconv * src, axis=0, keepdims=True)
        dcb_ref[...] = jnp.sum(dxconv, axis=0, keepdims=True)

    col = lambda off: pl.BlockSpec((S, 128), lambda j: (0, off + j))
    vec = pl.BlockSpec((1, 128), lambda j: (0, j))
    mat = pl.BlockSpec((None, 128, 128), lambda j: (j, 0, 0))
    vshape = jax.ShapeDtypeStruct((1, W), F32)
    mshape = jax.ShapeDtypeStruct((8, 128, 128), F32)
    return _pcall(
        body, grid=(8,),
        in_specs=[col(0), col(8), col(0), pl.BlockSpec((None, S, 128), lambda j: (0, 0, j)),
                  pl.BlockSpec((4, 128), lambda j: (0, j)), vec, mat, mat, mat, mat, vec, vec, vec],
        out_specs=[pl.BlockSpec((2, S, 128), lambda j: (0, 0, j)), pl.BlockSpec((4, 128), lambda j: (0, j)), vec,
                   mat, mat, vec, vec, vec],
        out_shape=[jax.ShapeDtypeStruct((4, S, W), BF), jax.ShapeDtypeStruct((4, W), F32), vshape, mshape, mshape,
                   vshape, vshape, vshape],
        scratch=[pltpu.VMEM((S, 128), F32)] * 3, name=name, vmem=56, comm=comm,
    )(h, h, hst, dmix3, cw, cb.reshape(1, W), wa, wx, wat, wxt, ba.reshape(1, W), bx.reshape(1, W), lam.reshape(1, W))


def _odd_d_bwd(h, dmix3, dh4, wp, wpt, dscale, name):
    S = h.shape[0]

    def body(xd_ref, dg_ref, dd_ref, wp_ref, wpt_ref, ds_ref, dh_in, dh_ref, dwp_ref, dds_ref):
        g = pl.program_id(0)
        row = lax.broadcasted_iota(jnp.int32, (S, 256), 0)
        xd = xd_ref[...]
        cnt = jnp.minimum(row + 1, jnp.left_shift(2, g)).astype(F32)
        pooled = _pool_sums(xd, g, row, _shift_down) / cnt - xd
        pb = pooled.astype(BF)
        mixed = _dot(pb, wp_ref[...])
        dg = dg_ref[...]
        sg, dsg = _silu_grad(dg)
        dd = dd_ref[...]
        dmixed = dd * ds_ref[...] * sg
        dds_ref[...] = jnp.sum(dd * mixed * sg, axis=0, keepdims=True)
        dh_ref[1] = (dd * mixed * ds_ref[...] * dsg).astype(BF)
        dmb = dmixed.astype(BF)
        dpooled = _dot(dmb, wpt_ref[...])
        dwp_ref[...] = _dot_tn(pb, dmb)
        dh_ref[0] = (_pool_sums(dpooled / cnt, g, row, _shift_up) - dpooled).astype(BF)

    col = lambda off: pl.BlockSpec((S, 256), lambda g: (0, off + g))
    mat = pl.BlockSpec((None, 256, 256), lambda g: (g, 0, 0))
    vec = pl.BlockSpec((1, 256), lambda g: (0, g))
    return pl.pallas_call(
        body, grid=(4,),
        in_specs=[col(8), col(12), pl.BlockSpec((None, S, 256), lambda g: (1, 0, g)), mat, mat, vec, ANY],
        out_specs=[pl.BlockSpec((2, S, 256), lambda g: (1, 0, g)), mat, vec],
        out_shape=[jax.ShapeDtypeStruct((4, S, W), BF), jax.ShapeDtypeStruct((4, 256, 256), F32),
                   jax.ShapeDtypeStruct((1, W), F32)],
        input_output_aliases={6: 0}, name=name, compiler_params=_cp(56),
    )(h, h, dmix3, wp, wpt, dscale.reshape(1, W), dh4)


def _peer(d):
    x, y, c = lax.axis_index("x"), lax.axis_index("y"), lax.axis_index("c")
    px = 1 - x if d & 4 else x
    py = 1 - y if d & 2 else y
    pc = 1 - c if d & 1 else c
    return (px, py, pc), 4 * px + 2 * py + pc


class _GatherAll(_Comm):
    def __init__(self, xs):
        self.peers = EVERYONE
        self.inputs = [xs]
        self.out_shapes = [jax.ShapeDtypeStruct((N_DEV,) + xs.shape, xs.dtype)]
        self.sem_shapes = [pltpu.SemaphoreType.DMA((N_DEV - 1,)), pltpu.SemaphoreType.DMA((N_DEV - 1,)),
                           pltpu.SemaphoreType.DMA]

    def copies(self, ins, outs, sems):
        (x_ref,), (out_ref,), (send, recv, loc) = ins, outs, sems
        _, me = _peer(0)
        res = [pltpu.make_async_copy(x_ref, out_ref.at[me], loc)]
        for d in range(1, N_DEV):
            peer, _ = _peer(d)
            res.append(pltpu.make_async_remote_copy(src_ref=x_ref, dst_ref=out_ref.at[me], send_sem=send.at[d - 1],
                                                    recv_sem=recv.at[d - 1], device_id=peer, device_id_type=MESH))
        return res


class _ExchangeAll(_Comm):
    def __init__(self, g8):
        self.peers = EVERYONE
        self.inputs = [g8]
        self.out_shapes = [jax.ShapeDtypeStruct(g8.shape, g8.dtype)]
        self.sem_shapes = [pltpu.SemaphoreType.DMA((N_DEV - 1,)), pltpu.SemaphoreType.DMA((N_DEV - 1,)),
                           pltpu.SemaphoreType.DMA]

    def copies(self, ins, outs, sems):
        (g_ref,), (out_ref,), (send, recv, loc) = ins, outs, sems
        _, me = _peer(0)
        res = [pltpu.make_async_copy(g_ref.at[me], out_ref.at[0], loc)]
        for d in range(1, N_DEV):
            peer, pidx = _peer(d)
            res.append(pltpu.make_async_remote_copy(src_ref=g_ref.at[pidx], dst_ref=out_ref.at[d], send_sem=send.at[d - 1],
                                                    recv_sem=recv.at[d - 1], device_id=peer, device_id_type=MESH))
        return res


def _sum8(r8, tr, name):
    _, R, C = r8.shape
    tr = min(tr, R)
    assert R % tr == 0

    def body(r_ref, o_ref):
        acc = r_ref[0]
        for d in range(1, N_DEV):
            acc = acc + r_ref[d]
        o_ref[...] = acc

    return pl.pallas_call(
        body, grid=(R // tr,), in_specs=[pl.BlockSpec((N_DEV, tr, C), lambda i: (0, i, 0))],
        out_specs=pl.BlockSpec((tr, C), lambda i: (i, 0)), out_shape=jax.ShapeDtypeStruct((R, C), F32),
        name=name, compiler_params=_cp(),
    )(r8)


def _adamw_math(w, g, m, v):
    m2 = B1 * m + (1.0 - B1) * g
    v2 = B2 * v + (1.0 - B2) * (g * g)
    m_hat = m2 / (1.0 - B1 ** STEP)
    v_hat = v2 / (1.0 - B2 ** STEP)
    return -LR * (m_hat / (jnp.sqrt(v_hat) + ADAM_EPS) + WD * w), m2, v2


def _adamw_many(ws, gs, ms, vs, name):
    n = len(ws)

    def body(*refs):
        for i in range(n):
            d, m2, v2 = _adamw_math(refs[i][...], refs[n + i][...], refs[2 * n + i][...], refs[3 * n + i][...])
            refs[4 * n + i][...] = d
            refs[5 * n + i][...] = m2
            refs[6 * n + i][...] = v2

    vmem = pl.BlockSpec(memory_space=pltpu.VMEM)
    shapes = [jax.ShapeDtypeStruct(w.shape, F32) for w in ws]
    res = pl.pallas_call(body, in_specs=[vmem] * (4 * n), out_specs=[vmem] * (3 * n), out_shape=shapes * 3, name=name,
                         compiler_params=_cp())(*ws, *gs, *ms, *vs)
    return res[:n], res[n:2 * n], res[2 * n:]


def _adamw(w3, gs, m3, v3, tr, name, comm=None):
    _, R, C = w3.shape
    n = 2 if isinstance(gs[0], tuple) else 1

    def gradient(refs):
        if n == 1:
            return refs[0][...]
        s_ref, r_ref = refs
        return ((s_ref[...].astype(F32) + r_ref[0].astype(F32)) + r_ref[1].astype(F32)) + r_ref[2].astype(F32)

    def body(w_ref, *rest):
        g_refs, (m_ref, v_ref, d_ref, m2_ref, v2_ref, g_ref) = rest[:2 * n], rest[2 * n:]
        g = jnp.where(pl.program_id(0) == 0, gradient(g_refs[:n]), gradient(g_refs[n:]))
        d_ref[...], m2_ref[...], v2_ref[...] = _adamw_math(w_ref[...], g, m_ref[...], v_ref[...])
        g_ref[...] = g

    blk = pl.BlockSpec((None, tr, C), lambda j, i: (j, i, 0))

    def grad_specs(layer):
        at = lambda j, i: jnp.where(j == layer, i, 0)
        if n == 1:
            return [pl.BlockSpec((tr, C), lambda j, i: (at(j, i), 0))]
        return [pl.BlockSpec((None, tr, C), lambda j, i: (0, at(j, i), 0)), pl.BlockSpec((3, tr, C), lambda j, i: (0, at(j, i), 0))]

    flat = [a for g in gs for a in (g if n == 2 else (g,))]
    shp = jax.ShapeDtypeStruct((2, R, C), F32)
    return _pcall(body, grid=(2, R // tr), in_specs=[blk] + grad_specs(0) + grad_specs(1) + [blk, blk], out_specs=[blk] * 4,
                  out_shape=[shp] * 4, name=name, comm=comm)(w3, *flat, m3, v3)


def _rep_pack(a):
    n = a.size
    pad = (-n) % 1024
    f = a.reshape(-1)
    if pad:
        f = jnp.concatenate([f, jnp.zeros((pad,), a.dtype)])
    return f.reshape(N_DEV, -1, 128)


def _rep_unpack(p, shape):
    n = 1
    for s in shape:
        n *= s
    return p.reshape(-1)[:n].reshape(shape)


def _sh_pack(a, axis):
    shp = a.shape
    a = a.reshape(shp[:axis] + (N_DEV, shp[axis] // N_DEV) + shp[axis + 1:])
    return jnp.moveaxis(a, axis, 0).reshape(N_DEV, -1, 128)


def _sh_unpack(p, shape, axis):
    a = p.reshape((N_DEV,) + shape[:axis] + (shape[axis] // N_DEV,) + shape[axis + 1:])
    return jnp.moveaxis(a, 0, axis).reshape(shape)


def _pad_rows(a, mult=8):
    pad = (-a.shape[-2]) % mult
    if pad:
        a = jnp.concatenate([a, jnp.zeros(a.shape[:-2] + (pad, a.shape[-1]), a.dtype)], axis=-2)
    return a


REP = ["even_a_ln_g", "even_a_ln_b", "even_a_ws", "even_a_bs", "even_b_sinks", "even_ln_g", "even_ln_b",
       "odd_w_a", "odd_w_x"]
SH = [("odd_conv_w", (2, 4, W), 2), ("odd_conv_b", (2, W), 1), ("odd_b_a", (2, W), 1), ("odd_b_x", (2, W), 1),
      ("odd_lam", (2, W), 1), ("odd_w_pool", (2, 4, 256, 256), 2), ("odd_d_scale", (2, W), 1),
      ("odd_ln_g", (2, D), 1), ("odd_ln_b", (2, D), 1)]
BIG = ["even_w_in", "even_w_out", "odd_w_in", "odd_w_out"]
NAMES = ["even_w_in", "even_a_ln_g", "even_a_ln_b", "even_a_ws", "even_a_bs", "even_b_sinks", "even_w_out",
         "even_ln_g", "even_ln_b", "odd_w_in", "odd_conv_w", "odd_conv_b", "odd_w_a", "odd_b_a", "odd_w_x", "odd_b_x",
         "odd_lam", "odd_w_pool", "odd_d_scale", "odd_w_out", "odd_ln_g", "odd_ln_b"]


def _rope_table(positions):
    inv = ROPE_THETA ** (-jnp.arange(0, 16, 2, dtype=F32) / 16)
    f = jnp.arange(128) % 64
    ang = positions.astype(F32)[:, None] * inv[f % 8][None, :]
    cos, sin = jnp.cos(ang), jnp.sin(ang)
    return jnp.concatenate([jnp.where(f < 16, cos, 1.0), jnp.where(f < 8, -sin, 0.0),
                            jnp.where((f >= 8) & (f < 16), sin, 0.0)], axis=1)


def kernel(x, positions, even_w_in, even_a_ln_g, even_a_ln_b, even_a_ws, even_a_bs, even_b_sinks, even_w_out, even_ln_g, even_ln_b, odd_w_in, odd_conv_w, odd_conv_b, odd_w_a, odd_b_a, odd_w_x, odd_b_x, odd_lam, odd_w_pool, odd_d_scale, odd_w_out, odd_ln_g, odd_ln_b, loss_target, m_even_w_in, m_even_a_ln_g, m_even_a_ln_b, m_even_a_ws, m_even_a_bs, m_even_b_sinks, m_even_w_out, m_even_ln_g, m_even_ln_b, m_odd_w_in, m_odd_conv_w, m_odd_conv_b, m_odd_w_a, m_odd_b_a, m_odd_w_x, m_odd_b_x, m_odd_lam, m_odd_w_pool, m_odd_d_scale, m_odd_w_out, m_odd_ln_g, m_odd_ln_b, v_even_w_in, v_even_a_ln_g, v_even_a_ln_b, v_even_a_ws, v_even_a_bs, v_even_b_sinks, v_even_w_out, v_even_ln_g, v_even_ln_b, v_odd_w_in, v_odd_conv_w, v_odd_conv_b, v_odd_w_a, v_odd_b_a, v_odd_w_x, v_odd_b_x, v_odd_lam, v_odd_w_pool, v_odd_d_scale, v_odd_w_out, v_odd_ln_g, v_odd_ln_b):
    args = (even_w_in, even_a_ln_g, even_a_ln_b, even_a_ws, even_a_bs, even_b_sinks, even_w_out, even_ln_g, even_ln_b,
            odd_w_in, odd_conv_w, odd_conv_b, odd_w_a, odd_b_a, odd_w_x, odd_b_x, odd_lam, odd_w_pool, odd_d_scale,
            odd_w_out, odd_ln_g, odd_ln_b)
    margs = (m_even_w_in, m_even_a_ln_g, m_even_a_ln_b, m_even_a_ws, m_even_a_bs, m_even_b_sinks, m_even_w_out,
             m_even_ln_g, m_even_ln_b, m_odd_w_in, m_odd_conv_w, m_odd_conv_b, m_odd_w_a, m_odd_b_a, m_odd_w_x,
             m_odd_b_x, m_odd_lam, m_odd_w_pool, m_odd_d_scale, m_odd_w_out, m_odd_ln_g, m_odd_ln_b)
    vargs = (v_even_w_in, v_even_a_ln_g, v_even_a_ln_b, v_even_a_ws, v_even_a_bs, v_even_b_sinks, v_even_w_out,
             v_even_ln_g, v_even_ln_b, v_odd_w_in, v_odd_conv_w, v_odd_conv_b, v_odd_w_a, v_odd_b_a, v_odd_w_x,
             v_odd_b_x, v_odd_lam, v_odd_w_pool, v_odd_d_scale, v_odd_w_out, v_odd_ln_g, v_odd_ln_b)
    wts = dict(zip(NAMES, args))
    mom = dict(zip(NAMES, margs))
    var = dict(zip(NAMES, vargs))
    S = x.shape[1]
    x0 = x[0]
    rope = _rope_table(positions[0])

    kinds = ("even", "odd", "even", "odd")
    blk_in = [jnp.transpose(wts[kinds[l] + "_w_in"][l // 2]).astype(BF) for l in range(4)]
    blk_out = [wts[kinds[l] + "_w_out"][l // 2].astype(BF) for l in range(4)]
    sh_local = _pad_rows(jnp.concatenate([wts[nm].reshape(-1, 128) for nm, _, _ in SH], axis=0), 16)
    me = 4 * lax.axis_index("x") + 2 * lax.axis_index("y") + lax.axis_index("c")
    own_slot = lambda blk: lax.dynamic_update_slice(lax.empty((N_DEV,) + blk.shape, blk.dtype), blk[None], (me, 0, 0))
    reg = {"blk_small": sh_local, "w_small": own_slot(sh_local)}
    sched = _Sched(reg)
    for l in range(4):
        reg[f"blk_in{l}"], reg[f"blk_out{l}"] = blk_in[l], blk_out[l]
        reg[f"w_in{l}"], reg[f"w_out{l}"] = own_slot(blk_in[l]), own_slot(blk_out[l])
    sched.add(_rows("blk_in0", "w_in0", "ag1", blk_in[0].shape[0], ROW_CHUNK[blk_in[0].shape[0]]))
    sched.add(_rows("blk_small", "w_small", "ag1", sh_local.shape[0], sh_local.shape[0]))
    for l in range(4):
        sched.add(_rows(f"blk_out{l}", f"w_out{l}", "ag1", D // N_DEV, ROW_CHUNK[D // N_DEV]))
        if l < 3:
            r = blk_in[l + 1].shape[0]
            sched.add(_rows(f"blk_in{l + 1}", f"w_in{l + 1}", "ag1", r, ROW_CHUNK[r]))

    def gathered(dst, blk):
        sched.flush(dst, FLUSH_EXTRA_US)
        return reg.pop(dst)

    wt_in0 = gathered("w_in0", blk_in[0]).reshape(-1, D)
    full = {nm: wts[nm] for nm in REP}

    def gather_small():
        sh_all = gathered("w_small", sh_local)
        off = 0
        for nm, shape, axis in SH:
            r = wts[nm].size // 128
            full[nm] = _sh_unpack(sh_all[:, off:off + r, :], shape, axis)
            off += r

    saved = []
    wt_in, w_out = [wt_in0, None, None, None], [None] * 4
    xf, xb = x0, x0.astype(BF)
    fwd = lambda name: FWD_OVERBOOK * CARRY_US[name]
    for layer in range(4):
        j = layer // 2
        kind = kinds[layer]
        if wt_in[layer] is None:
            wt_in[layer] = gathered(f"w_in{layer}", blk_in[layer]).reshape(-1, D)
        h = sched.run(_mm_nt, fwd("mm_h_" + kind), xb, wt_in[layer], 1024, 768 if kind == "even" else 512, "mm_h_" + kind)
        if kind == "even":
            bsb = jnp.broadcast_to(full["even_a_bs"][j][:, :, None], (8, 128, 128))
            mix3, o, l = sched.run(_even_fwd, fwd("even_fwd"), h, rope, full["even_a_ln_g"][j], full["even_a_ln_b"][j],
                                   full["even_a_ws"][j], bsb, full["even_b_sinks"][j], "even_fwd")
            extra = (o, l, bsb)
        else:
            if "odd_lam" not in full:
                gather_small()
            wa, wx = full["odd_w_a"][j].astype(BF), full["odd_w_x"][j].astype(BF)
            wp = full["odd_w_pool"][j].astype(BF)
            mix3, hst = sched.run(_odd_c_fwd, fwd("odd_c_fwd"), h, full["odd_conv_w"][j], full["odd_conv_b"][j], wa, wx,
                                  full["odd_b_a"][j], full["odd_b_x"][j], full["odd_lam"][j], "odd_c_fwd")
            mix3 = _odd_d_fwd(h, mix3, wp, full["odd_d_scale"][j], "odd_d_fwd")
            extra = (hst, wa, wx, wp)
        w_out[layer] = gathered(f"w_out{layer}", blk_out[layer]).reshape(D, D)
        z, xn, xnb = sched.run(_mm_out_ln, fwd("mm_out_ln"), mix3, w_out[layer], xf, full[kind + "_ln_g"][j],
                               full[kind + "_ln_b"][j], "mm_out_ln")
        saved.append((xb, h, mix3, z, extra))
        xf, xb = xn, xnb

    dxn = xf

    gsum = {nm: [None, None] for nm in NAMES}

    chip_sums = {}
    sched.overhang = 0.15

    waiting = []

    def chip_sum(g, tag, key):
        r = g.shape[0] // N_DEV
        reg["g_" + key] = g.reshape(N_DEV, r, D)
        sched.add(_rows("g_" + key, "d_" + key, "rsd", r, r), first=True)
        waiting.append((key, tag))

    def add_arrived():
        for key, tag in list(waiting):
            if "d_" + key in reg and not sched.pending("d_" + key):
                waiting.remove((key, tag))
                g8 = reg.pop("g_" + key)
                chip_sums[key] = reg["s_" + key] = _add_pairs(g8, reg.pop("d_" + key), "rs_add_" + tag)
                sched.add(_rows("s_" + key, "r_" + key, "rs", g8.shape[1], ROW_CHUNK[g8.shape[1]] // 2))

    sched.after_landing = add_arrived

    def reduced(key):
        sched.flush("d_" + key, FLUSH_EXTRA_US)
        sched.flush("r_" + key, FLUSH_EXTRA_US)
        return chip_sums[key], reg.pop("r_" + key)

    for layer in (3, 2, 1, 0):
        j = layer // 2
        xb, h, mix3, z, extra = saved[layer]
        kind = kinds[layer]
        if layer == 3:
            dz, dzb, dg, dbeta, part = sched.run(_ln_bwd, CARRY_US["ln_bwd"], dxn, z, full[kind + "_ln_g"][j], "loss_ln_bwd",
                                                 target=loss_target[0])
        else:
            dz, dzb, dg, dbeta = sched.run(_ln_bwd, CARRY_US["ln_bwd"], dxn, z, full[kind + "_ln_g"][j], "ln_bwd")
        gsum[kind + "_ln_g"][j] = dg.reshape(D)
        gsum[kind + "_ln_b"][j] = dbeta.reshape(D)
        chip_sum(sched.run(_mm_tn, CARRY_US["mm_dw_out"], mix3, dzb, 512, "mm_dw_out"), "w_out", f"out{layer}")
        dmix3 = sched.run(_mm_nt, CARRY_US["mm_dmix"], dzb, w_out[layer], 1024, 512, "mm_dmix", out3=True)
        if kind == "even":
            o, l, bsb = extra
            ws = full["even_a_ws"][j]
            dh, dws, dbs, dlng, dlnb, dsink = sched.run(
                _even_bwd, CARRY_US["even_bwd"], h, dmix3, o, l, rope, full["even_a_ln_g"][j], full["even_a_ln_b"][j],
                ws, jnp.swapaxes(ws, 1, 2), bsb, full["even_b_sinks"][j], "even_bwd")
            gsum["even_a_ws"][j] = dws
            gsum["even_a_bs"][j] = jnp.transpose(dbs[:, :8])
            gsum["even_a_ln_g"][j] = dlng.reshape(W)
            gsum["even_a_ln_b"][j] = dlnb.reshape(W)
            gsum["even_b_sinks"][j] = dsink[0, :16]
            if layer == 0:
                rep_rows = [_rep_pack(jnp.stack(gsum[nm]).reshape(wts[nm].shape)) for nm in REP]
                sh_rows = [_sh_pack(jnp.stack(gsum[nm]).reshape(shape), axis) for nm, shape, axis in SH]
                packed = _pad_rows(jnp.concatenate(rep_rows + sh_rows, axis=1))
                gw, (small8, parts) = _mm_tn(dh, xb, 384, "mm_dw_in_even", comm=_Join([_ExchangeAll(packed), _GatherAll(part)]))
                loss = jnp.sum(parts[:, 0, 0]) * (0.5 / D)
            else:
                gw = sched.run(_mm_tn, CARRY_US["mm_dw_in_even"], dh, xb, 384, "mm_dw_in_even")
            chip_sum(gw, "w_in_even", f"in{layer}")
            if layer == 0:
                n_rep = sum(p.shape[1] for p in rep_rows)
                red = _sum8(small8, 1 << 20, "sum_small")
                (rep_all,) = sched.flush("d_in0", FLUSH_EXTRA_US, beside=_GatherAll(_pad_rows(red[:n_rep])))
                sched.overhang = 0.6
            dxn = sched.run(_mm_nn_res, CARRY_US["mm_dx_even"], dh, wt_in[layer], dz, 512, 512, "mm_dx_even")
        else:
            hst, wa, wx, wp = extra
            dh4, dcw, dcb, dwa, dwx, dba, dbx, dlam = sched.run(
                _odd_c_bwd, CARRY_US["odd_c_bwd"], h, hst, dmix3, full["odd_conv_w"][j], full["odd_conv_b"][j], wa, wx,
                jnp.swapaxes(wa, 1, 2), jnp.swapaxes(wx, 1, 2), full["odd_b_a"][j], full["odd_b_x"][j], full["odd_lam"][j],
                "odd_c_bwd")
            dh4, dwp, dds = _odd_d_bwd(h, dmix3, dh4, wp, jnp.swapaxes(wp, 1, 2), full["odd_d_scale"][j], "odd_d_bwd")
            gsum["odd_conv_w"][j], gsum["odd_conv_b"][j] = dcw, dcb.reshape(W)
            gsum["odd_w_a"][j], gsum["odd_w_x"][j] = dwa, dwx
            gsum["odd_b_a"][j], gsum["odd_b_x"][j], gsum["odd_lam"][j] = dba.reshape(W), dbx.reshape(W), dlam.reshape(W)
            gsum["odd_w_pool"][j], gsum["odd_d_scale"][j] = dwp, dds.reshape(W)
            chip_sum(sched.run(_mm_tn, CARRY_US["mm_dw_in_odd"], dh4, xb, 512, "mm_dw_in_odd"), "w_in_odd", f"in{layer}")
            dxn = sched.run(_mm_nn_res, CARRY_US["mm_dx_odd"], dh4, wt_in[layer], dz, 512, 512, "mm_dx_odd")
    grad_x = dxn[None]

    out_g, out_d, out_m, out_v = {}, {}, {}, {}
    for nm, kind, what, layers in (("odd_w_out", "odd", "out", (1, 3)), ("even_w_out", "even", "out", (0, 2)),
                                   ("odd_w_in", "odd", "in", (1, 3)), ("even_w_in", "even", "in", (0, 2))):
        gl = [reduced(f"{what}{l}") for l in layers]
        if nm == "even_w_in":
            view = lambda a: jnp.transpose(a, (0, 2, 1))
            res, _ = _adamw(view(wts[nm]), gl, view(mom[nm]), view(var[nm]), 112, f"adamw_{nm}")
            res = [view(a) for a in res]
        elif what == "in":
            gs = [jnp.transpose(_rs_final(s4, r3, "rs_final_w_in_odd")) for s4, r3 in gl]
            res, _ = _adamw(wts[nm], gs, mom[nm], var[nm], 512, f"adamw_{nm}")
        else:
            res = sched.run(_adamw, CARRY_US["adamw_" + nm], wts[nm], gl, mom[nm], var[nm], 128, f"adamw_{nm}")
        out_d[nm], out_m[nm], out_v[nm], out_g[nm] = res

    g_small = {}
    off = 0
    for nm, p in zip(REP, rep_rows):
        r = p.shape[1]
        g_small[nm] = _rep_unpack(rep_all[:, off:off + r, :], wts[nm].shape)
        off += r
    off = n_rep
    for (nm, shape, axis), p in zip(SH, sh_rows):
        r = p.shape[1]
        g_small[nm] = red[off:off + r].reshape(wts[nm].shape)
        off += r

    def rows(a):
        f = a.reshape(-1)
        pad = (-f.shape[0]) % 128
        if pad:
            f = jnp.concatenate([f, jnp.zeros((pad,), a.dtype)])
        return f.reshape(-1, 128)

    small = REP + [nm for nm, _, _ in SH]
    each = lambda src: [rows(src[nm]) for nm in small]
    d2, m2, v2 = _adamw_many(each(wts), each(g_small), each(mom), each(var), "adamw_small")
    for i, nm in enumerate(small):
        n, shp = wts[nm].size, wts[nm].shape
        take = lambda a: a.reshape(-1)[:n].reshape(shp)
        out_g[nm], out_d[nm], out_m[nm], out_v[nm] = g_small[nm], take(d2[i]), take(m2[i]), take(v2[i])

    return (loss, grad_x, *[out_g[nm] for nm in NAMES], *[out_d[nm] for nm in NAMES],
            *[out_m[nm] for nm in NAMES], *[out_v[nm] for nm in NAMES])
```

```python
import functools

import jax
import jax.numpy as jnp
from jax import lax
from jax.experimental import pallas as pl
from jax.experimental.pallas import tpu as pltpu

F32 = jnp.float32
BF = jnp.bfloat16
MESH = pl.DeviceIdType.MESH
ANY = pl.BlockSpec(memory_space=pl.ANY)

N_DEV = 8
D = 2048
W = 1024
EVEN_IN = 5376
ODD_IN = 4096
CHUNK = 128
ALPHA = (2 * 4) ** 0.25
LN_EPS = 1e-5
ROPE_THETA = 500000.0
LRU_C = 8.0
LR, B1, B2, ADAM_EPS, WD, STEP = 0.001, 0.9, 0.999, 1e-08, 0.01, 10
NEG = -1e30
HEAD_COLS = 4


def _cp(vmem_mb=48, collective_id=None):
    return pltpu.CompilerParams(vmem_limit_bytes=vmem_mb * 1024 * 1024, collective_id=collective_id)


def _sig(x):
    return jax.nn.sigmoid(x)


def _silu_grad(x):
    s = _sig(x)
    return x * s, s * (1.0 + x * (1.0 - s))


def _dot(a, b):
    return jnp.dot(a, b, preferred_element_type=F32)


def _dot_nt(a, b):
    return lax.dot_general(a, b, (((1,), (1,)), ((), ())), preferred_element_type=F32)


def _dot_tn(a, b):
    return lax.dot_general(a, b, (((0,), (0,)), ((), ())), preferred_element_type=F32)


def _coords():
    return lax.axis_index("x"), lax.axis_index("y"), lax.axis_index("c")


def _chip(j):
    x, y, _ = _coords()
    return (1 - x if j & 2 else x), (1 - y if j & 1 else y)


X_NB, Y_NB, DIAG, SIB = 4, 2, 6, 1
EVERYONE = frozenset(range(1, N_DEV))
BARRIER_IDS = {}


class _Comm:
    def collective_id(self):
        return BARRIER_IDS.setdefault(frozenset(self.peers), len(BARRIER_IDS))

    def start(self, ins, outs, sems):
        barrier = pltpu.get_barrier_semaphore()
        for d in sorted(self.peers):
            pl.semaphore_signal(barrier, inc=1, device_id=_peer(d)[0], device_id_type=MESH)
        pl.semaphore_wait(barrier, len(self.peers))
        for cp in self.copies(ins, outs, sems):
            cp.start()

    def wait(self, ins, outs, sems):
        for cp in self.copies(ins, outs, sems):
            cp.wait()


class _Join(_Comm):
    def __init__(self, parts):
        self.parts = list(parts)
        self.peers = frozenset().union(*[p.peers for p in self.parts])
        self.inputs = [a for p in self.parts for a in p.inputs]
        self.out_shapes = [s for p in self.parts for s in p.out_shapes]
        self.sem_shapes = [s for p in self.parts for s in p.sem_shapes]
        self.aliases = {}
        i0 = o0 = 0
        for p in self.parts:
            for i, o in getattr(p, "aliases", {}).items():
                self.aliases[i0 + i] = o0 + o
            i0, o0 = i0 + len(p.inputs), o0 + len(p.out_shapes)

    def copies(self, ins, outs, sems):
        res = []
        i0 = o0 = s0 = 0
        for p in self.parts:
            ni, no, ns = len(p.inputs), len(p.out_shapes), len(p.sem_shapes)
            res += p.copies(ins[i0:i0 + ni], outs[o0:o0 + no], sems[s0:s0 + ns])
            i0, o0, s0 = i0 + ni, o0 + no, s0 + ns
        return res


ROWS_US = {"ag1": 0.104, "ag2": 0.052, "agd": 0.027, "rsd": 0.027, "rs": 0.205}
N_COPIES = {"ag1": 2, "ag2": 2, "agd": 4, "rsd": 4, "rs": 3}
TASK_PEERS = {"ag1": {X_NB, Y_NB}, "ag2": {X_NB, Y_NB}, "agd": {SIB}, "rsd": {SIB}, "rs": {X_NB, Y_NB, DIAG}}
ROW_CHUNK = {672: 224, 512: 128, 256: 128}
CARRY_US = {"mm_h_even": 58, "mm_h_odd": 47, "even_fwd": 42, "odd_c_fwd": 37, "mm_out_ln": 33, "ln_bwd": 23, "mm_dmix": 26,
            "mm_dw_out": 25, "even_bwd": 90, "odd_c_bwd": 58, "mm_dw_in_even": 58, "mm_dw_in_odd": 44, "mm_dx_even": 66,
            "mm_dx_odd": 55, "adamw_even_w_out": 11, "adamw_odd_w_out": 11}
FWD_OVERBOOK = 1.15
FLUSH_EXTRA_US = 60.0


def _cost_us(task, reg):
    kind, src, _, lo, hi = task
    return ROWS_US[kind] * (hi - lo) * reg[src].shape[-1] * reg[src].dtype.itemsize / 4096.0


class _Copies(_Comm):
    def __init__(self, tasks, reg):
        self.tasks = list(tasks)
        self.out_names, self.in_names = [], []
        for kind, src, dst, lo, hi in self.tasks:
            if dst not in self.out_names:
                self.out_names.append(dst)
        for kind, src, dst, lo, hi in self.tasks:
            if src not in self.out_names and src not in self.in_names:
                self.in_names.append(src)
        self.out_shapes, self.aliases = [], {}
        for o, dst in enumerate(self.out_names):
            if dst in reg:
                self.aliases[len(self.in_names)] = o
                self.in_names.append(dst)
                self.out_shapes.append(jax.ShapeDtypeStruct(reg[dst].shape, reg[dst].dtype))
            else:
                kind, src = next((t[0], t[1]) for t in self.tasks if t[2] == dst)
                shape = ({"rsd": 4, "rs": 3}[kind],) + reg[src].shape[1:]
                self.out_shapes.append(jax.ShapeDtypeStruct(shape, reg[src].dtype))
        self.inputs = [reg[nm] for nm in self.in_names]
        n = sum(N_COPIES[t[0]] for t in self.tasks)
        self.sem_shapes = [pltpu.SemaphoreType.DMA((n,)), pltpu.SemaphoreType.DMA((n,))]
        self.peers = frozenset().union(*[TASK_PEERS[t[0]] for t in self.tasks])

    def copies(self, ins, outs, sems):
        send, recv = sems
        x, y, c = _coords()
        me = 4 * x + 2 * y + c
        xn, yn = (1 - x, y, c), (x, 1 - y, c)
        at_xn, at_yn = 4 * (1 - x) + 2 * y + c, 4 * x + 2 * (1 - y) + c
        ref = dict(zip(self.in_names, ins))
        ref.update(zip(self.out_names, outs))
        res = []

        def copy(src, dst, to):
            i = len(res)
            res.append(pltpu.make_async_remote_copy(src_ref=src, dst_ref=dst, send_sem=send.at[i], recv_sem=recv.at[i],
                                                    device_id=to, device_id_type=MESH))

        for kind, src, dst, lo, hi in self.tasks:
            n = hi - lo
            if kind == "ag1":
                for to in (xn, yn):
                    copy(ref[src].at[pl.ds(lo, n)], ref[dst].at[me, pl.ds(lo, n)], to)
            elif kind == "ag2":
                h = n // 2
                first, second = ref[dst].at[at_xn, pl.ds(lo, h)], ref[dst].at[at_yn, pl.ds(lo + h, n - h)]
                copy(first, first, yn)
                copy(second, second, xn)
            elif kind == "agd":
                for j in range(4):
                    px, py = _chip(j)
                    rows = ref[dst].at[4 * px + 2 * py + c, pl.ds(lo, n)]
                    copy(rows, rows, (x, y, 1 - c))
            elif kind == "rsd":
                for j in range(4):
                    px, py = _chip(j)
                    copy(ref[src].at[4 * px + 2 * py + 1 - c, pl.ds(lo, n)], ref[dst].at[j, pl.ds(lo, n)], (x, y, 1 - c))
            else:
                for j in (1, 2, 3):
                    px, py = _chip(j)
                    copy(ref[src].at[j, pl.ds(lo, n)], ref[dst].at[j - 1, pl.ds(lo, n)], (px, py, c))
        return res


class _Sched:
    def __init__(self, reg):
        self.reg, self.queue, self.later = reg, [], []
        self.overhang = 0.5
        self.after_landing = None

    def add(self, tasks, first=False):
        self.queue = list(tasks) + self.queue if first else self.queue + list(tasks)

    def pending(self, dst):
        return any(t[2] == dst for t in self.queue + self.later)

    def take(self, budget_us, must=None, overhang=0.5):
        self.queue, self.later = self.later + self.queue, []
        picked, us = [], 0.0
        rest = []
        for t in self.queue:
            cost = _cost_us(t, self.reg)
            if (must is not None and t[2] == must) or us + (1.0 - overhang) * cost <= budget_us:
                picked.append(t)
                us += cost
                if t[0] in ("ag1", "ag2"):
                    self.later.append(({"ag1": "ag2", "ag2": "agd"}[t[0]], t[2], t[2], t[3], t[4]))
            else:
                rest.append(t)
        self.queue = rest
        return _Copies(picked, self.reg) if picked else None

    def landed(self, comm, got):
        if comm is not None:
            for nm, a in zip(comm.out_names, got):
                self.reg[nm] = a
        if self.after_landing is not None:
            self.after_landing()

    def run(self, builder, budget_us, *args, **kw):
        comm = self.take(budget_us, overhang=self.overhang)
        res, got = builder(*args, comm=comm, **kw)
        self.landed(comm, got)
        return res

    def flush(self, dst, budget_us=0.0, beside=None):
        res = []
        while self.pending(dst):
            comm = self.take(budget_us, must=dst)
            got = _comm_only(comm if beside is None else _Join([comm, beside]), "flush_" + dst)
            res, beside = got[len(comm.out_shapes):], None
            self.landed(comm, got[:len(comm.out_shapes)])
        return res


def _rows(name_src, name_dst, kind, n_rows, chunk):
    return [(kind, name_src, name_dst, lo, min(lo + chunk, n_rows)) for lo in range(0, n_rows, chunk)]


def _pcall(body, *, grid, in_specs, out_specs, out_shape, name, scratch=(), vmem=48, comm=None):
    in_specs, out_specs, out_shape, scratch = list(in_specs), list(out_specs), list(out_shape), list(scratch)
    if comm is None:
        call = pl.pallas_call(body, grid=grid, in_specs=in_specs, out_specs=out_specs, out_shape=out_shape,
                              scratch_shapes=scratch, name=name, compiler_params=_cp(vmem))
        return lambda *args: (call(*args), [])
    n_in, n_out, n_scr = len(in_specs), len(out_specs), len(scratch)
    c_in, c_out = len(comm.inputs), len(comm.out_shapes)
    aliases = {n_in + i: n_out + o for i, o in getattr(comm, "aliases", {}).items()}

    def wrapped(*refs):
        ins, cins = refs[:n_in], refs[n_in:n_in + c_in]
        o0 = n_in + c_in
        outs, couts = refs[o0:o0 + n_out], refs[o0 + n_out:o0 + n_out + c_out]
        s0 = o0 + n_out + c_out
        scr, sems = refs[s0:s0 + n_scr], refs[s0 + n_scr:]
        ids = [pl.program_id(a) for a in range(len(grid))]
        first = functools.reduce(jnp.logical_and, [i == 0 for i in ids])
        last = functools.reduce(jnp.logical_and, [i == g - 1 for i, g in zip(ids, grid)])

        @pl.when(first)
        def _():
            comm.start(cins, couts, sems)

        body(*ins, *outs, *scr)

        @pl.when(last)
        def _():
            comm.wait(cins, couts, sems)

    call = pl.pallas_call(wrapped, grid=grid, in_specs=in_specs + [ANY] * c_in, out_specs=out_specs + [ANY] * c_out,
                          out_shape=out_shape + list(comm.out_shapes), scratch_shapes=scratch + list(comm.sem_shapes),
                          input_output_aliases=aliases, name=name, compiler_params=_cp(vmem, comm.collective_id()))

    def run(*args):
        res = call(*args, *comm.inputs)
        return res[:n_out], res[n_out:]

    return run


def _comm_only(comm, name):
    c_in, c_out = len(comm.inputs), len(comm.out_shapes)

    def body(*refs):
        cins, couts, sems = refs[:c_in], refs[c_in:c_in + c_out], refs[c_in + c_out:]
        comm.start(cins, couts, sems)
        comm.wait(cins, couts, sems)

    return pl.pallas_call(body, in_specs=[ANY] * c_in, out_specs=[ANY] * c_out, out_shape=list(comm.out_shapes),
                          scratch_shapes=list(comm.sem_shapes), input_output_aliases=dict(getattr(comm, "aliases", {})),
                          name=name, compiler_params=pltpu.CompilerParams(collective_id=comm.collective_id()))(*comm.inputs)


def _chip_blocks():
    _, _, c = _coords()
    return jnp.stack([4 * px + 2 * py + c for px, py in map(_chip, range(4))]).astype(jnp.int32)


def _add_pairs(g8, b4, name):
    _, R, C = b4.shape

    def body(idx_ref, a_ref, b_ref, o_ref):
        o_ref[...] = (a_ref[...].astype(F32) + b_ref[...].astype(F32)).astype(BF)

    blk = pl.BlockSpec((None, R, C), lambda j, idx: (j, 0, 0))
    grid_spec = pltpu.PrefetchScalarGridSpec(
        num_scalar_prefetch=1, grid=(4,),
        in_specs=[pl.BlockSpec((None, R, C), lambda j, idx: (idx[j], 0, 0)), blk], out_specs=blk)
    return pl.pallas_call(body, grid_spec=grid_spec, out_shape=jax.ShapeDtypeStruct(b4.shape, BF), name=name,
                          compiler_params=_cp())(_chip_blocks(), g8, b4)


def _rs_final(s4, r3, name):
    _, R, C = s4.shape
    tr = R // 2

    def body(s_ref, r_ref, o_ref):
        o_ref[...] = ((s_ref[...].astype(F32) + r_ref[0].astype(F32)) + r_ref[1].astype(F32)) + r_ref[2].astype(F32)

    return pl.pallas_call(
        body, grid=(2,),
        in_specs=[pl.BlockSpec((None, tr, C), lambda i: (0, i, 0)), pl.BlockSpec((3, tr, C), lambda i: (0, i, 0))],
        out_specs=pl.BlockSpec((tr, C), lambda i: (i, 0)), out_shape=jax.ShapeDtypeStruct((R, C), F32),
        name=name, compiler_params=_cp())(s4, r3)


def _mm_nt(a, w, tm, tn, name, out3=False, comm=None):
    M, K = a.shape
    N = w.shape[0]
    tm = min(tm, M)

    def body(a_ref, w_ref, o_ref):
        o_ref[...] = _dot_nt(a_ref[...], w_ref[...])

    if out3:
        per = W // tn
        out_shape = jax.ShapeDtypeStruct((N // W, M, W), F32)
        out_spec = pl.BlockSpec((None, tm, tn), lambda i, j: (j // per, i, j % per))
    else:
        out_shape = jax.ShapeDtypeStruct((M, N), F32)
        out_spec = pl.BlockSpec((tm, tn), lambda i, j: (i, j))
    (res,), extra = _pcall(
        body, grid=(M // tm, N // tn),
        in_specs=[pl.BlockSpec((tm, K), lambda i, j: (i, 0)), pl.BlockSpec((tn, K), lambda i, j: (j, 0))],
        out_specs=[out_spec], out_shape=[out_shape], name=name, comm=comm)(a, w)
    return res, extra


def _mm_tn(a, b, tm, name, comm=None):
    K, N = b.shape
    if a.ndim == 3:
        M = a.shape[0] * W
        per = W // tm
        a_spec = pl.BlockSpec((None, K, tm), lambda i: (i // per, 0, i % per))
    else:
        M = a.shape[1]
        a_spec = pl.BlockSpec((K, tm), lambda i: (0, i))

    def body(a_ref, b_ref, o_ref):
        o_ref[...] = _dot_tn(a_ref[...], b_ref[...]).astype(BF)

    (out,), extra = _pcall(
        body, grid=(M // tm,),
        in_specs=[a_spec, pl.BlockSpec((K, N), lambda i: (0, 0))],
        out_specs=[pl.BlockSpec((tm, N), lambda i: (i, 0))],
        out_shape=[jax.ShapeDtypeStruct((M, N), BF)], name=name, vmem=56, comm=comm)(a, b)
    return out, extra


def _mm_nn_res(a, w, res, tm, tn, name, comm=None):
    K, N = w.shape
    if a.ndim == 3:
        P, M = a.shape[0], a.shape[1]
        tm = min(tm, M)
        a_spec = pl.BlockSpec((P, tm, W), lambda j, i: (0, i, 0))
    else:
        P, M = 0, a.shape[0]
        tm = min(tm, M)
        a_spec = pl.BlockSpec((tm, K), lambda j, i: (i, 0))

    def body(a_ref, w_ref, r_ref, o_ref):
        if P:
            d = _dot(a_ref[0], w_ref[0:W, :])
            for p in range(1, P):
                d = d + _dot(a_ref[p], w_ref[p * W:(p + 1) * W, :])
        else:
            d = _dot(a_ref[...], w_ref[...])
        o_ref[...] = ALPHA * r_ref[...] + d

    (out,), extra = _pcall(
        body, grid=(N // tn, M // tm),
        in_specs=[a_spec, pl.BlockSpec((K, tn), lambda j, i: (0, j)), pl.BlockSpec((tm, tn), lambda j, i: (i, j))],
        out_specs=[pl.BlockSpec((tm, tn), lambda j, i: (i, j))],
        out_shape=[jax.ShapeDtypeStruct((M, N), F32)], name=name, comm=comm)(a, w, res)
    return out, extra


def _mm_out_ln(mix3, w_out, x, g, b, name, comm=None):
    S = x.shape[0]
    tm = min(256, S)

    def body(m_ref, w_ref, x_ref, g_ref, b_ref, z_ref, xn_ref, xb_ref):
        acc = _dot(m_ref[0], w_ref[0:W, :]) + _dot(m_ref[1], w_ref[W:2 * W, :])
        z = ALPHA * x_ref[...] + acc
        mu = jnp.mean(z, axis=1, keepdims=True)
        zc = z - mu
        var = jnp.mean(zc * zc, axis=1, keepdims=True)
        xn = zc * lax.rsqrt(var + LN_EPS) * g_ref[...] + b_ref[...]
        z_ref[...] = z
        xn_ref[...] = xn
        xb_ref[...] = xn.astype(BF)

    row = pl.BlockSpec((tm, D), lambda i: (i, 0))
    vec = pl.BlockSpec((1, D), lambda i: (0, 0))
    return _pcall(
        body, grid=(S // tm,),
        in_specs=[pl.BlockSpec((2, tm, W), lambda i: (0, i, 0)), pl.BlockSpec((D, D), lambda i: (0, 0)), row, vec, vec],
        out_specs=[row, row, row],
        out_shape=[jax.ShapeDtypeStruct((S, D), F32), jax.ShapeDtypeStruct((S, D), F32), jax.ShapeDtypeStruct((S, D), BF)],
        name=name, comm=comm)(mix3, w_out, x, g.reshape(1, D), b.reshape(1, D))


def _ln_bwd(dxn, z, g, name, comm=None, target=None):
    S = z.shape[0]
    tm = min(256, S)
    head = target is not None

    def body(*refs):
        if head:
            d_ref, t_ref, z_ref, g_ref, dz_ref, dzb_ref, dg_ref, db_ref, p_ref = refs
        else:
            d_ref, z_ref, g_ref, dz_ref, dzb_ref, dg_ref, db_ref = refs
        i = pl.program_id(0)
        zz = z_ref[...]
        mu = jnp.mean(zz, axis=1, keepdims=True)
        zc = zz - mu
        var = jnp.mean(zc * zc, axis=1, keepdims=True)
        rstd = lax.rsqrt(var + LN_EPS)
        xhat = zc * rstd
        dy = d_ref[...]
        if head:
            e = dy - t_ref[...]
            dy = e * (1.0 / D)

            @pl.when(i == 0)
            def _():
                p_ref[...] = jnp.zeros_like(p_ref)

            p_ref[...] += jnp.sum(jnp.sum(e * e, axis=1, keepdims=True), axis=0, keepdims=True)
        dyg = dy * g_ref[...]
        m1 = jnp.mean(dyg, axis=1, keepdims=True)
        m2 = jnp.mean(dyg * xhat, axis=1, keepdims=True)
        dz = rstd * (dyg - m1 - xhat * m2)
        dz_ref[...] = dz
        dzb_ref[...] = dz.astype(BF)

        @pl.when(i == 0)
        def _():
            dg_ref[...] = jnp.zeros_like(dg_ref)
            db_ref[...] = jnp.zeros_like(db_ref)

        dg_ref[...] += jnp.sum(dy * xhat, axis=0, keepdims=True)
        db_ref[...] += jnp.sum(dy, axis=0, keepdims=True)

    row = pl.BlockSpec((tm, D), lambda i: (i, 0))
    vec = pl.BlockSpec((1, D), lambda i: (0, 0))
    out_specs = [row, row, vec, vec] + ([pl.BlockSpec((8, 128), lambda i: (0, 0))] if head else [])
    out_shape = [jax.ShapeDtypeStruct((S, D), F32), jax.ShapeDtypeStruct((S, D), BF), jax.ShapeDtypeStruct((1, D), F32),
                 jax.ShapeDtypeStruct((1, D), F32)] + ([jax.ShapeDtypeStruct((8, 128), F32)] if head else [])
    operands = (dxn, target, z, g.reshape(1, D)) if head else (dxn, z, g.reshape(1, D))
    return _pcall(body, grid=(S // tm,), in_specs=[row] * (len(operands) - 1) + [vec], out_specs=out_specs,
                  out_shape=out_shape, name=name, comm=comm)(*operands)


def _rope_fwd(t, r_ref):
    return (t * r_ref[:, 0:128] + pltpu.roll(t, 120, 1) * r_ref[:, 128:256]
            + pltpu.roll(t, 8, 1) * r_ref[:, 256:384])


def _rope_bwd(g, r_ref):
    return (g * r_ref[:, 0:128] + pltpu.roll(g * r_ref[:, 128:256], 8, 1)
            + pltpu.roll(g * r_ref[:, 256:384], 120, 1))


def _dup_heads(kb):
    lo = lax.broadcasted_iota(jnp.int32, kb.shape, 1) < 64
    sw = pltpu.roll(kb, 64, 1)
    return [jnp.where(lo, kb, sw).astype(BF), jnp.where(lo, sw, kb).astype(BF)]


def _even_fwd(h, rope, lng, lnb, ws, bsb, sinks, name, comm=None):
    S = h.shape[0]
    nb = S // CHUNK

    def body(h_ref, hp_ref, rc_ref, rp_ref, lng_ref, lnb_ref, ws_ref, bsb_ref, sink_ref, mix_ref, o_ref, l_ref):
        n = pl.program_id(0)
        lane = lax.broadcasted_iota(jnp.int32, (128, 128), 1)
        rowi = lax.broadcasted_iota(jnp.int32, (128, 128), 0)
        tri = rowi >= lane
        lane_lo = lane < 64
        v = h_ref[:, W:2 * W]
        mu = jnp.mean(v, axis=1, keepdims=True)
        vc = v - mu
        var = jnp.mean(vc * vc, axis=1, keepdims=True)
        vn = vc * lax.rsqrt(var + LN_EPS) * lng_ref[...] + lnb_ref[...]
        ms = [_dot(jnp.where(tri, ws_ref[g], 0.0).astype(BF), vn[:, g * 128:(g + 1) * 128].astype(BF)) for g in range(8)]
        for g in range(8):
            sl = slice(g * 128, (g + 1) * 128)
            ag = h_ref[:, 2 * W + g * 128:2 * W + (g + 1) * 128]
            mix_ref[0, :, sl] = (h_ref[:, sl] * (ms[g] + bsb_ref[g]) * (ag * _sig(ag))).astype(BF)
        kb = jnp.concatenate([_rope_fwd(hp_ref[:, 0:128], rp_ref), _rope_fwd(h_ref[:, 4096:4224], rc_ref)], axis=0)
        vb = jnp.concatenate([hp_ref[:, 128:256], h_ref[:, 4224:4352]], axis=0)
        k2 = _dup_heads(kb)
        v2 = _dup_heads(vb)
        qi = lax.broadcasted_iota(jnp.int32, (128, 256), 0)
        kj = lax.broadcasted_iota(jnp.int32, (128, 256), 1)
        diff = qi + 128 - kj
        valid = (diff >= 0) & (diff < 128) & ((n > 0) | (kj >= 128))
        lacc = jnp.zeros((128, 128), F32)
        for j0 in range(0, 8, HEAD_COLS):
            heads = [(j, half) for j in range(j0, j0 + HEAD_COLS) for half in range(2)]
            sc, pr, oh = {}, {}, {}
            for j in range(j0, j0 + HEAD_COLS):
                qc = _rope_fwd(h_ref[:, 3072 + j * 128:3072 + (j + 1) * 128], rc_ref)
                sc[j, 0] = _dot_nt(jnp.where(lane_lo, qc, 0.0).astype(BF), k2[j // 4])
                sc[j, 1] = _dot_nt(jnp.where(lane_lo, 0.0, qc).astype(BF), k2[j // 4])
            for j, half in heads:
                hq = 2 * j + half
                s = jnp.where(valid, sc[j, half] * 0.125, NEG)
                sk = sink_ref[hq]
                mx = jnp.maximum(jnp.max(s, axis=1, keepdims=True), sk)
                p = jnp.exp(s - mx)
                den = jnp.sum(p, axis=1, keepdims=True) + jnp.exp(sk - mx)
                pr[j, half] = (p / den).astype(BF)
                lacc = jnp.where(lane == hq, mx + jnp.log(den), lacc)
            for j, half in heads:
                oh[j, half] = _dot(pr[j, half], v2[j // 4])
            for j in range(j0, j0 + HEAD_COLS):
                cs = slice(j * 128, (j + 1) * 128)
                ocol = jnp.where(lane_lo, oh[j, 0], oh[j, 1])
                bg = h_ref[:, 4352 + j * 128:4352 + (j + 1) * 128]
                o_ref[:, cs] = ocol
                mix_ref[1, :, cs] = (ocol * (bg * _sig(bg))).astype(BF)
        l_ref[...] = lacc

    prev = lambda n: jnp.maximum(n - 1, 0)
    full = lambda shape: pl.BlockSpec(shape, lambda n: (0,) * len(shape))
    return _pcall(
        body, grid=(nb,),
        in_specs=[pl.BlockSpec((CHUNK, EVEN_IN), lambda n: (n, 0)),
                  pl.BlockSpec((CHUNK, 256), lambda n: (prev(n), 16)),
                  pl.BlockSpec((CHUNK, 384), lambda n: (n, 0)),
                  pl.BlockSpec((CHUNK, 384), lambda n: (prev(n), 0)),
                  full((1, W)), full((1, W)), full((8, 128, 128)), full((8, 128, 128)),
                  pl.BlockSpec(memory_space=pltpu.SMEM)],
        out_specs=[pl.BlockSpec((2, CHUNK, W), lambda n: (0, n, 0)),
                   pl.BlockSpec((CHUNK, W), lambda n: (n, 0)),
                   pl.BlockSpec((CHUNK, 128), lambda n: (n, 0))],
        out_shape=[jax.ShapeDtypeStruct((2, S, W), BF), jax.ShapeDtypeStruct((S, W), F32),
                   jax.ShapeDtypeStruct((S, 128), F32)],
        name=name, comm=comm)(h, h, rope, rope, lng.reshape(1, W), lnb.reshape(1, W), ws, bsb, sinks)


def _even_bwd(h, dmix3, o, l, rope, lng, lnb, ws, wst, bsb, sinks, name, comm=None):
    S = h.shape[0]
    nb = S // CHUNK

    def body(h_ref, hp_ref, hn_ref, dm_ref, dmn_ref, o_ref, on_ref, l_ref, ln_ref, rc_ref, rp_ref, rn_ref,
             lng_ref, lnb_ref, ws_ref, wst_ref, bsb_ref, sink_ref,
             dh_ref, dws_ref, dbs_ref, dlng_ref, dlnb_ref, dsink_ref, dvn_ref):
        n = pl.program_id(0)

        @pl.when(n == 0)
        def _():
            dws_ref[...] = jnp.zeros_like(dws_ref)
            dbs_ref[...] = jnp.zeros_like(dbs_ref)
            dlng_ref[...] = jnp.zeros_like(dlng_ref)
            dlnb_ref[...] = jnp.zeros_like(dlnb_ref)
            dsink_ref[...] = jnp.zeros_like(dsink_ref)

        lane = lax.broadcasted_iota(jnp.int32, (128, 128), 1)
        rowi = lax.broadcasted_iota(jnp.int32, (128, 128), 0)
        lane1 = lax.broadcasted_iota(jnp.int32, (1, 128), 1)
        tri = rowi >= lane
        tri_t = lane >= rowi
        lane_lo = lane < 64
        v = h_ref[:, W:2 * W]
        mu = jnp.mean(v, axis=1, keepdims=True)
        vc = v - mu
        var = jnp.mean(vc * vc, axis=1, keepdims=True)
        rstd = lax.rsqrt(var + LN_EPS)
        vhat = vc * rstd
        vn = vhat * lng_ref[...] + lnb_ref[...]
        dbs_acc = jnp.zeros((128, 128), F32)
        vng = [vn[:, g * 128:(g + 1) * 128].astype(BF) for g in range(8)]
        ms = [_dot(jnp.where(tri, ws_ref[g], 0.0).astype(BF), vng[g]) for g in range(8)]
        dmb = []
        for g in range(8):
            sl = slice(g * 128, (g + 1) * 128)
            m = ms[g] + bsb_ref[g]
            ag = h_ref[:, 2 * W + g * 128:2 * W + (g + 1) * 128]
            sg, dsg = _silu_grad(ag)
            u = h_ref[:, sl]
            da = dm_ref[0, :, sl]
            dmm = da * u * sg
            dh_ref[:, sl] = (da * m * sg).astype(BF)
            dh_ref[:, 2 * W + g * 128:2 * W + (g + 1) * 128] = (da * u * m * dsg).astype(BF)
            dmb.append(dmm.astype(BF))
            dbs_acc = jnp.where(lane == g, jnp.sum(dmm, axis=1, keepdims=True), dbs_acc)
        dvs = [_dot(jnp.where(tri_t, wst_ref[g], 0.0).astype(BF), dmb[g]) for g in range(8)]
        dwss = [_dot_nt(dmb[g], vng[g]) for g in range(8)]
        for g in range(8):
            dvn_ref[:, g * 128:(g + 1) * 128] = dvs[g]
            dws_ref[g] += jnp.where(tri, dwss[g], 0.0)
        dbs_ref[...] += dbs_acc
        dvn = dvn_ref[...]
        dlng_ref[...] += jnp.sum(dvn * vhat, axis=0, keepdims=True)
        dlnb_ref[...] += jnp.sum(dvn, axis=0, keepdims=True)
        dyg = dvn * lng_ref[...]
        m1 = jnp.mean(dyg, axis=1, keepdims=True)
        m2 = jnp.mean(dyg * vhat, axis=1, keepdims=True)
        dh_ref[:, W:2 * W] = (rstd * (dyg - m1 - vhat * m2)).astype(BF)
        kcur = _rope_fwd(h_ref[:, 4096:4224], rc_ref)
        kb = jnp.concatenate([_rope_fwd(hp_ref[:, 0:128], rp_ref), kcur], axis=0)
        vb = jnp.concatenate([hp_ref[:, 128:256], h_ref[:, 4224:4352]], axis=0)
        k2 = _dup_heads(kb)
        v2 = _dup_heads(vb)
        kc2 = _dup_heads(kcur)
        vc2 = _dup_heads(h_ref[:, 4224:4352])
        qi = lax.broadcasted_iota(jnp.int32, (128, 256), 0)
        kj = lax.broadcasted_iota(jnp.int32, (128, 256), 1)
        diff = qi + 128 - kj
        valid = (diff >= 0) & (diff < 128) & ((n > 0) | (kj >= 128))
        validn = (lane > rowi) & (n < nb - 1)
        lc = l_ref[...]
        lnx = ln_ref[...]
        dk = [jnp.zeros((128, 128), F32), jnp.zeros((128, 128), F32)]
        dv = [jnp.zeros((128, 128), F32), jnp.zeros((128, 128), F32)]
        dsk_acc = jnp.zeros((1, 128), F32)
        for j0 in range(0, 8, HEAD_COLS):
            heads = [(j, half) for j in range(j0, j0 + HEAD_COLS) for half in range(2)]
            t = {}
            for j in range(j0, j0 + HEAD_COLS):
                cs = slice(j * 128, (j + 1) * 128)
                qc = _rope_fwd(h_ref[:, 3072 + j * 128:3072 + (j + 1) * 128], rc_ref)
                qn = _rope_fwd(hn_ref[:, 3072 + j * 128:3072 + (j + 1) * 128], rn_ref)
                bg = h_ref[:, 4352 + j * 128:4352 + (j + 1) * 128]
                sgb, dsgb = _silu_grad(bg)
                db = dm_ref[1, :, cs]
                oc = o_ref[:, cs]
                do = db * sgb
                dh_ref[:, 4352 + j * 128:4352 + (j + 1) * 128] = (db * oc * dsgb).astype(BF)
                bgn = hn_ref[:, 4352 + j * 128:4352 + (j + 1) * 128]
                don = dmn_ref[1, :, cs] * (bgn * _sig(bgn))
                prod = do * oc
                prodn = don * on_ref[:, cs]
                for half in range(2):
                    hq = 2 * j + half
                    hm = lane_lo if half == 0 else jnp.logical_not(lane_lo)
                    t[j, half] = dict(
                        dsum=jnp.sum(jnp.where(hm, prod, 0.0), axis=1, keepdims=True),
                        dsumn=jnp.sum(jnp.where(hm, prodn, 0.0), axis=1, keepdims=True),
                        lh=jnp.sum(jnp.where(lane == hq, lc, 0.0), axis=1, keepdims=True),
                        lhn=jnp.sum(jnp.where(lane == hq, lnx, 0.0), axis=1, keepdims=True),
                        qm=jnp.where(hm, qc, 0.0).astype(BF), dom=jnp.where(hm, do, 0.0).astype(BF),
                        qnm=jnp.where(hm, qn, 0.0).astype(BF), donm=jnp.where(hm, don, 0.0).astype(BF))
            for j, half in heads:
                e, hk = t[j, half], j // 4
                e["s"], e["dp"] = _dot_nt(e["qm"], k2[hk]), _dot_nt(e["dom"], v2[hk])
                e["sn"], e["dpn"] = _dot_nt(e["qnm"], kc2[hk]), _dot_nt(e["donm"], vc2[hk])
            for j, half in heads:
                e, hq = t[j, half], 2 * j + half
                p = jnp.exp(jnp.where(valid, e["s"] * 0.125 - e["lh"], NEG))
                ds = p * (e["dp"] - e["dsum"])
                pn = jnp.exp(jnp.where(validn, e["sn"] * 0.125 - e["lhn"], NEG))
                dsn = pn * (e["dpn"] - e["dsumn"])
                psink = jnp.exp(sink_ref[hq] - e["lh"])
                dsk_acc = jnp.where(lane1 == hq, -jnp.sum(psink * e["dsum"], axis=0, keepdims=True), dsk_acc)
                e["ds"] = ds.astype(BF)
                e["pt"], e["dst"] = jnp.transpose(p[:, 128:256]).astype(BF), jnp.transpose(ds[:, 128:256]).astype(BF)
                e["pnt"], e["dsnt"] = jnp.transpose(pn).astype(BF), jnp.transpose(dsn).astype(BF)
            for j, half in heads:
                e, hk = t[j, half], j // 4
                e["dq"] = _dot(e["ds"], k2[hk])
                e["dv"] = _dot(e["pt"], e["dom"]) + _dot(e["pnt"], e["donm"])
                e["dk"] = _dot(e["dst"], e["qm"]) + _dot(e["dsnt"], e["qnm"])
            for j in range(j0, j0 + HEAD_COLS):
                hk = j // 4
                dqcol = jnp.where(lane_lo, t[j, 0]["dq"], t[j, 1]["dq"]) * 0.125
                dh_ref[:, 3072 + j * 128:3072 + (j + 1) * 128] = _rope_bwd(dqcol, rc_ref).astype(BF)
                dv[hk] = dv[hk] + t[j, 0]["dv"] + t[j, 1]["dv"]
                dk[hk] = dk[hk] + (t[j, 0]["dk"] + t[j, 1]["dk"]) * 0.125
        fold = lambda a: a + pltpu.roll(a, 64, 1)
        dh_ref[:, 4096:4224] = _rope_bwd(jnp.where(lane_lo, fold(dk[0]), fold(dk[1])), rc_ref).astype(BF)
        dh_ref[:, 4224:4352] = jnp.where(lane_lo, fold(dv[0]), fold(dv[1])).astype(BF)
        dsink_ref[...] += dsk_acc

    prev = lambda n: jnp.maximum(n - 1, 0)
    nxt = lambda n: jnp.minimum(n + 1, nb - 1)
    full = lambda shape: pl.BlockSpec(shape, lambda n: (0,) * len(shape))
    return _pcall(
        body, grid=(nb,),
        in_specs=[pl.BlockSpec((CHUNK, EVEN_IN), lambda n: (n, 0)),
                  pl.BlockSpec((CHUNK, 256), lambda n: (prev(n), 16)),
                  pl.BlockSpec((CHUNK, EVEN_IN), lambda n: (nxt(n), 0)),
                  pl.BlockSpec((2, CHUNK, W), lambda n: (0, n, 0)),
                  pl.BlockSpec((2, CHUNK, W), lambda n: (0, nxt(n), 0)),
                  pl.BlockSpec((CHUNK, W), lambda n: (n, 0)),
                  pl.BlockSpec((CHUNK, W), lambda n: (nxt(n), 0)),
                  pl.BlockSpec((CHUNK, 128), lambda n: (n, 0)),
                  pl.BlockSpec((CHUNK, 128), lambda n: (nxt(n), 0)),
                  pl.BlockSpec((CHUNK, 384), lambda n: (n, 0)),
                  pl.BlockSpec((CHUNK, 384), lambda n: (prev(n), 0)),
                  pl.BlockSpec((CHUNK, 384), lambda n: (nxt(n), 0)),
                  full((1, W)), full((1, W)), full((8, 128, 128)), full((8, 128, 128)), full((8, 128, 128)),
                  pl.BlockSpec(memory_space=pltpu.SMEM)],
        out_specs=[pl.BlockSpec((CHUNK, EVEN_IN), lambda n: (n, 0)),
                   full((8, 128, 128)), full((128, 128)), full((1, W)), full((1, W)), full((1, 128))],
        out_shape=[jax.ShapeDtypeStruct((S, EVEN_IN), BF), jax.ShapeDtypeStruct((8, 128, 128), F32),
                   jax.ShapeDtypeStruct((128, 128), F32), jax.ShapeDtypeStruct((1, W), F32),
                   jax.ShapeDtypeStruct((1, W), F32), jax.ShapeDtypeStruct((1, 128), F32)],
        scratch=[pltpu.VMEM((CHUNK, W), F32)], name=name, comm=comm,
    )(h, h, h, dmix3, dmix3, o, o, l, l, rope, rope, rope, lng.reshape(1, W), lnb.reshape(1, W), ws, wst, bsb, sinks)


def _expm1(x):
    ser = x * (1.0 + x * (0.5 + x * (1.0 / 6.0 + x * (1.0 / 24.0))))
    return jnp.where(jnp.abs(x) < 1e-2, ser, jnp.exp(x) - 1.0)


def _softplus_neg(lam):
    z = -lam
    e = jnp.exp(-jnp.abs(z))
    l1p = jnp.where(e < 1e-3, e * (1.0 - e * (0.5 - e * (1.0 / 3.0))), jnp.log(1.0 + e))
    return jnp.maximum(z, 0.0) + l1p


def _shift_down(x, k, row, fill=0.0):
    return jnp.where(row >= k, pltpu.roll(x, k, 0), fill)


def _shift_up(x, k, row, fill=0.0):
    S = x.shape[0]
    return jnp.where(row < S - k, pltpu.roll(x, S - k, 0), fill)


def _lru_gates(xc, row, cw_ref, cb_ref, wa_ref, wx_ref, ba_ref, bx_ref, lam_ref):
    xconv = (cw_ref[3:4, :] * xc + cw_ref[2:3, :] * _shift_down(xc, 1, row) + cw_ref[1:2, :] * _shift_down(xc, 2, row)
             + cw_ref[0:1, :] * _shift_down(xc, 3, row) + cb_ref[...])
    xb = xconv.astype(BF)
    r = _sig(_dot(xb, wa_ref[...]) + ba_ref[...])
    i = _sig(_dot(xb, wx_ref[...]) + bx_ref[...])
    sp = _softplus_neg(lam_ref[...])
    log_a = -LRU_C * r * sp
    a = jnp.exp(log_a)
    mult = jnp.sqrt(-_expm1(2.0 * log_a))
    return xconv, r, i, sp, a, mult


ROWS_PER_TILE = 8


def _steps(a, b, shift, inside, products=True):
    n, k = inside.n, 1
    while k < n:
        b = a * jnp.where(inside(k), shift(b, k), 0.0) + b
        if products or 2 * k < n:
            a = a * jnp.where(inside(k), shift(a, k), 1.0)
        k *= 2
    return a, b


class _Inside:
    def __init__(self, pos, n, reverse):
        self.pos, self.n, self.reverse = pos, n, reverse

    def __call__(self, k):
        return self.pos < self.n - k if self.reverse else self.pos >= k


def _scan_rows(a, b, row, a_ref, b_ref, c_ref, reverse=False):
    S = a.shape[0]
    G = S // ROWS_PER_TILE
    if reverse:
        shift = lambda x, k: pltpu.roll(x, x.shape[0] - k, 0)
    else:
        shift = lambda x, k: pltpu.roll(x, k, 0)
    a, b = _steps(a, b, shift, _Inside(row % ROWS_PER_TILE, ROWS_PER_TILE, reverse))
    a_ref[...] = a
    b_ref[...] = b
    last = 0 if reverse else ROWS_PER_TILE - 1
    grow = lax.broadcasted_iota(jnp.int32, (G, a.shape[1]), 0)
    _, tot = _steps(a_ref[pl.ds(last, G, stride=ROWS_PER_TILE), :], b_ref[pl.ds(last, G, stride=ROWS_PER_TILE), :],
                    shift, _Inside(grow, G, reverse), products=False)
    enters = jnp.where(_Inside(grow, G, reverse)(1), shift(tot, 1), 0.0)
    for r in range(ROWS_PER_TILE):
        c_ref[pl.ds(r, G, stride=ROWS_PER_TILE), :] = enters
    return b + a * c_ref[...]


def _odd_c_fwd(h, cw, cb, wa, wx, ba, bx, lam, name, comm=None):
    S = h.shape[0]

    def body(xc_ref, cg_ref, cw_ref, cb_ref, wa_ref, wx_ref, ba_ref, bx_ref, lam_ref, mix_ref, hst_ref, sa_ref, sb_ref, sc_ref):
        row = lax.broadcasted_iota(jnp.int32, (S, 128), 0)
        xconv, r, i, sp, a, mult = _lru_gates(xc_ref[...], row, cw_ref, cb_ref, wa_ref, wx_ref, ba_ref, bx_ref, lam_ref)
        bb = _scan_rows(a, mult * (i * xconv), row, sa_ref, sb_ref, sc_ref)
        hst_ref[...] = bb
        cg = cg_ref[...]
        mix_ref[...] = (bb * (cg * _sig(cg))).astype(BF)

    col = lambda off: pl.BlockSpec((S, 128), lambda j: (0, off + j))
    vec = pl.BlockSpec((1, 128), lambda j: (0, j))
    mat = pl.BlockSpec((None, 128, 128), lambda j: (j, 0, 0))
    return _pcall(
        body, grid=(8,),
        in_specs=[col(0), col(8), pl.BlockSpec((4, 128), lambda j: (0, j)), vec, mat, mat, vec, vec, vec],
        out_specs=[pl.BlockSpec((None, S, 128), lambda j: (0, 0, j)), pl.BlockSpec((S, 128), lambda j: (0, j))],
        out_shape=[jax.ShapeDtypeStruct((2, S, W), BF), jax.ShapeDtypeStruct((S, W), F32)],
        scratch=[pltpu.VMEM((S, 128), F32)] * 3, name=name, comm=comm,
    )(h, h, cw, cb.reshape(1, W), wa, wx, ba.reshape(1, W), bx.reshape(1, W), lam.reshape(1, W))


def _pool_sums(x, g, row, shift):
    s2 = x + shift(x, 1, row)
    s4 = s2 + shift(s2, 2, row)
    s8 = s4 + shift(s4, 4, row)
    s16 = s8 + shift(s8, 8, row)
    return jnp.where(g == 0, s2, jnp.where(g == 1, s4, jnp.where(g == 2, s8, s16)))


def _odd_d_fwd(h, mix3, wp, dscale, name):
    S = h.shape[0]

    def body(xd_ref, dg_ref, wp_ref, ds_ref, mix_in, mix_ref):
        g = pl.program_id(0)
        row = lax.broadcasted_iota(jnp.int32, (S, 256), 0)
        xd = xd_ref[...]
        cnt = jnp.minimum(row + 1, jnp.left_shift(2, g)).astype(F32)
        pooled = _pool_sums(xd, g, row, _shift_down) / cnt - xd
        mixed = _dot(pooled.astype(BF), wp_ref[...])
        dg = dg_ref[...]
        mix_ref[...] = (mixed * ds_ref[...] * (dg * _sig(dg))).astype(BF)

    col = lambda off: pl.BlockSpec((S, 256), lambda g: (0, off + g))
    return pl.pallas_call(
        body, grid=(4,),
        in_specs=[col(8), col(12), pl.BlockSpec((None, 256, 256), lambda g: (g, 0, 0)),
                  pl.BlockSpec((1, 256), lambda g: (0, g)), ANY],
        out_specs=pl.BlockSpec((None, S, 256), lambda g: (1, 0, g)),
        out_shape=jax.ShapeDtypeStruct((2, S, W), BF), input_output_aliases={4: 0},
        name=name, compiler_params=_cp(),
    )(h, h, wp, dscale.reshape(1, W), mix3)


def _odd_c_bwd(h, hst, dmix3, cw, cb, wa, wx, wat, wxt, ba, bx, lam, name, comm=None):
    S = h.shape[0]

    def body(xc_ref, cg_ref, hst_ref, dc_ref, cw_ref, cb_ref, wa_ref, wx_ref, wat_ref, wxt_ref, ba_ref, bx_ref, lam_ref,
             dh_ref, dcw_ref, dcb_ref, dwa_ref, dwx_ref, dba_ref, dbx_ref, dlam_ref, sa_ref, sb_ref, sc_ref):
        row = lax.broadcasted_iota(jnp.int32, (S, 128), 0)
        xc = xc_ref[...]
        xconv, r, i, sp, a, mult = _lru_gates(xc, row, cw_ref, cb_ref, wa_ref, wx_ref, ba_ref, bx_ref, lam_ref)
        hst = hst_ref[...]
        cg = cg_ref[...]
        sg, dsg = _silu_grad(cg)
        dc = dc_ref[...]
        dh_ref[1] = (dc * hst * dsg).astype(BF)
        lam_t = _scan_rows(_shift_up(a, 1, row), dc * sg, row, sa_ref, sb_ref, sc_ref, reverse=True)
        da = lam_t * _shift_down(hst, 1, row)
        ix = i * xconv
        dmult = lam_t * ix
        di = lam_t * mult * xconv
        dxconv = lam_t * mult * i
        dlog_a = da * a - dmult * (a * a / mult)
        dr = dlog_a * (-LRU_C * sp)
        dsp = jnp.sum(dlog_a * (-LRU_C * r), axis=0, keepdims=True)
        dlam_ref[...] = dsp * (-_sig(-lam_ref[...]))
        dpa = dr * r * (1.0 - r)
        dpx = di * i * (1.0 - i)
        dpab = dpa.astype(BF)
        dpxb = dpx.astype(BF)
        xb = xconv.astype(BF)
        dxconv = dxconv + _dot(dpab, wat_ref[...]) + _dot(dpxb, wxt_ref[...])
        dwa_ref[...] = _dot_tn(xb, dpab)
        dwx_ref[...] = _dot_tn(xb, dpxb)
        dba_ref[...] = jnp.sum(dpa, axis=0, keepdims=True)
        dbx_ref[...] = jnp.sum(dpx, axis=0, keepdims=True)
        dh_ref[0] = (cw_ref[3:4, :] * dxconv + cw_ref[2:3, :] * _shift_up(dxconv, 1, row)
                     + cw_ref[1:2, :] * _shift_up(dxconv, 2, row) + cw_ref[0:1, :] * _shift_up(dxconv, 3, row)).astype(BF)
        for j in range(4):
            src = xc if j == 3 else _shift_down(xc, 3 - j, row)
            dcw_ref[j:j + 1, :] = jnp.sum(dxconv * src, axis=0, keepdims=True)
        dcb_ref[...] = jnp.sum(dxconv, axis=0, keepdims=True)

    col = lambda off: pl.BlockSpec((S, 128), lambda j: (0, off + j))
    vec = pl.BlockSpec((1, 128), lambda j: (0, j))
    mat = pl.BlockSpec((None, 128, 128), lambda j: (j, 0, 0))
    vshape = jax.ShapeDtypeStruct((1, W), F32)
    mshape = jax.ShapeDtypeStruct((8, 128, 128), F32)
    return _pcall(
        body, grid=(8,),
        in_specs=[col(0), col(8), col(0), pl.BlockSpec((None, S, 128), lambda j: (0, 0, j)),
                  pl.BlockSpec((4, 128), lambda j: (0, j)), vec, mat, mat, mat, mat, vec, vec, vec],
        out_specs=[pl.BlockSpec((2, S, 128), lambda j: (0, 0, j)), pl.BlockSpec((4, 128), lambda j: (0, j)), vec,
                   mat, mat, vec, vec, vec],
        out_shape=[jax.ShapeDtypeStruct((4, S, W), BF), jax.ShapeDtypeStruct((4, W), F32), vshape, mshape, mshape,
                   vshape, vshape, vshape],
        scratch=[pltpu.VMEM((S, 128), F32)] * 3, name=name, vmem=56, comm=comm,
    )(h, h, hst, dmix3, cw, cb.reshape(1, W), wa, wx, wat, wxt, ba.reshape(1, W), bx.reshape(1, W), lam.reshape(1, W))


def _odd_d_bwd(h, dmix3, dh4, wp, wpt, dscale, name):
    S = h.shape[0]

    def body(xd_ref, dg_ref, dd_ref, wp_ref, wpt_ref, ds_ref, dh_in, dh_ref, dwp_ref, dds_ref):
        g = pl.program_id(0)
        row = lax.broadcasted_iota(jnp.int32, (S, 256), 0)
        xd = xd_ref[...]
        cnt = jnp.minimum(row + 1, jnp.left_shift(2, g)).astype(F32)
        pooled = _pool_sums(xd, g, row, _shift_down) / cnt - xd
        pb = pooled.astype(BF)
        mixed = _dot(pb, wp_ref[...])
        dg = dg_ref[...]
        sg, dsg = _silu_grad(dg)
        dd = dd_ref[...]
        dmixed = dd * ds_ref[...] * sg
        dds_ref[...] = jnp.sum(dd * mixed * sg, axis=0, keepdims=True)
        dh_ref[1] = (dd * mixed * ds_ref[...] * dsg).astype(BF)
        dmb = dmixed.astype(BF)
        dpooled = _dot(dmb, wpt_ref[...])
        dwp_ref[...] = _dot_tn(pb, dmb)
        dh_ref[0] = (_pool_sums(dpooled / cnt, g, row, _shift_up) - dpooled).astype(BF)

    col = lambda off: pl.BlockSpec((S, 256), lambda g: (0, off + g))
    mat = pl.BlockSpec((None, 256, 256), lambda g: (g, 0, 0))
    vec = pl.BlockSpec((1, 256), lambda g: (0, g))
    return pl.pallas_call(
        body, grid=(4,),
        in_specs=[col(8), col(12), pl.BlockSpec((None, S, 256), lambda g: (1, 0, g)), mat, mat, vec, ANY],
        out_specs=[pl.BlockSpec((2, S, 256), lambda g: (1, 0, g)), mat, vec],
        out_shape=[jax.ShapeDtypeStruct((4, S, W), BF), jax.ShapeDtypeStruct((4, 256, 256), F32),
                   jax.ShapeDtypeStruct((1, W), F32)],
        input_output_aliases={6: 0}, name=name, compiler_params=_cp(56),
    )(h, h, dmix3, wp, wpt, dscale.reshape(1, W), dh4)


def _peer(d):
    x, y, c = lax.axis_index("x"), lax.axis_index("y"), lax.axis_index("c")
    px = 1 - x if d & 4 else x
    py = 1 - y if d & 2 else y
    pc = 1 - c if d & 1 else c
    return (px, py, pc), 4 * px + 2 * py + pc


class _GatherAll(_Comm):
    def __init__(self, xs):
        self.peers = EVERYONE
        self.inputs = [xs]
        self.out_shapes = [jax.ShapeDtypeStruct((N_DEV,) + xs.shape, xs.dtype)]
        self.sem_shapes = [pltpu.SemaphoreType.DMA((N_DEV - 1,)), pltpu.SemaphoreType.DMA((N_DEV - 1,)),
                           pltpu.SemaphoreType.DMA]

    def copies(self, ins, outs, sems):
        (x_ref,), (out_ref,), (send, recv, loc) = ins, outs, sems
        _, me = _peer(0)
        res = [pltpu.make_async_copy(x_ref, out_ref.at[me], loc)]
        for d in range(1, N_DEV):
            peer, _ = _peer(d)
            res.append(pltpu.make_async_remote_copy(src_ref=x_ref, dst_ref=out_ref.at[me], send_sem=send.at[d - 1],
                                                    recv_sem=recv.at[d - 1], device_id=peer, device_id_type=MESH))
        return res


class _ExchangeAll(_Comm):
    def __init__(self, g8):
        self.peers = EVERYONE
        self.inputs = [g8]
        self.out_shapes = [jax.ShapeDtypeStruct(g8.shape, g8.dtype)]
        self.sem_shapes = [pltpu.SemaphoreType.DMA((N_DEV - 1,)), pltpu.SemaphoreType.DMA((N_DEV - 1,)),
                           pltpu.SemaphoreType.DMA]

    def copies(self, ins, outs, sems):
        (g_ref,), (out_ref,), (send, recv, loc) = ins, outs, sems
        _, me = _peer(0)
        res = [pltpu.make_async_copy(g_ref.at[me], out_ref.at[0], loc)]
        for d in range(1, N_DEV):
            peer, pidx = _peer(d)
            res.append(pltpu.make_async_remote_copy(src_ref=g_ref.at[pidx], dst_ref=out_ref.at[d], send_sem=send.at[d - 1],
                                                    recv_sem=recv.at[d - 1], device_id=peer, device_id_type=MESH))
        return res


def _sum8(r8, tr, name):
    _, R, C = r8.shape
    tr = min(tr, R)
    assert R % tr == 0

    def body(r_ref, o_ref):
        acc = r_ref[0]
        for d in range(1, N_DEV):
            acc = acc + r_ref[d]
        o_ref[...] = acc

    return pl.pallas_call(
        body, grid=(R // tr,), in_specs=[pl.BlockSpec((N_DEV, tr, C), lambda i: (0, i, 0))],
        out_specs=pl.BlockSpec((tr, C), lambda i: (i, 0)), out_shape=jax.ShapeDtypeStruct((R, C), F32),
        name=name, compiler_params=_cp(),
    )(r8)


def _adamw_math(w, g, m, v):
    m2 = B1 * m + (1.0 - B1) * g
    v2 = B2 * v + (1.0 - B2) * (g * g)
    m_hat = m2 / (1.0 - B1 ** STEP)
    v_hat = v2 / (1.0 - B2 ** STEP)
    return -LR * (m_hat / (jnp.sqrt(v_hat) + ADAM_EPS) + WD * w), m2, v2


def _adamw_many(ws, gs, ms, vs, name):
    n = len(ws)

    def body(*refs):
        for i in range(n):
            d, m2, v2 = _adamw_math(refs[i][...], refs[n + i][...], refs[2 * n + i][...], refs[3 * n + i][...])
            refs[4 * n + i][...] = d
            refs[5 * n + i][...] = m2
            refs[6 * n + i][...] = v2

    vmem = pl.BlockSpec(memory_space=pltpu.VMEM)
    shapes = [jax.ShapeDtypeStruct(w.shape, F32) for w in ws]
    res = pl.pallas_call(body, in_specs=[vmem] * (4 * n), out_specs=[vmem] * (3 * n), out_shape=shapes * 3, name=name,
                         compiler_params=_cp())(*ws, *gs, *ms, *vs)
    return res[:n], res[n:2 * n], res[2 * n:]


def _adamw(w3, gs, m3, v3, tr, name, comm=None):
    _, R, C = w3.shape
    n = 2 if isinstance(gs[0], tuple) else 1

    def gradient(refs):
        if n == 1:
            return refs[0][...]
        s_ref, r_ref = refs
        return ((s_ref[...].astype(F32) + r_ref[0].astype(F32)) + r_ref[1].astype(F32)) + r_ref[2].astype(F32)

    def body(w_ref, *rest):
        g_refs, (m_ref, v_ref, d_ref, m2_ref, v2_ref, g_ref) = rest[:2 * n], rest[2 * n:]
        g = jnp.where(pl.program_id(0) == 0, gradient(g_refs[:n]), gradient(g_refs[n:]))
        d_ref[...], m2_ref[...], v2_ref[...] = _adamw_math(w_ref[...], g, m_ref[...], v_ref[...])
        g_ref[...] = g

    blk = pl.BlockSpec((None, tr, C), lambda j, i: (j, i, 0))

    def grad_specs(layer):
        at = lambda j, i: jnp.where(j == layer, i, 0)
        if n == 1:
            return [pl.BlockSpec((tr, C), lambda j, i: (at(j, i), 0))]
        return [pl.BlockSpec((None, tr, C), lambda j, i: (0, at(j, i), 0)), pl.BlockSpec((3, tr, C), lambda j, i: (0, at(j, i), 0))]

    flat = [a for g in gs for a in (g if n == 2 else (g,))]
    shp = jax.ShapeDtypeStruct((2, R, C), F32)
    return _pcall(body, grid=(2, R // tr), in_specs=[blk] + grad_specs(0) + grad_specs(1) + [blk, blk], out_specs=[blk] * 4,
                  out_shape=[shp] * 4, name=name, comm=comm)(w3, *flat, m3, v3)


def _rep_pack(a):
    n = a.size
    pad = (-n) % 1024
    f = a.reshape(-1)
    if pad:
        f = jnp.concatenate([f, jnp.zeros((pad,), a.dtype)])
    return f.reshape(N_DEV, -1, 128)


def _rep_unpack(p, shape):
    n = 1
    for s in shape:
        n *= s
    return p.reshape(-1)[:n].reshape(shape)


def _sh_pack(a, axis):
    shp = a.shape
    a = a.reshape(shp[:axis] + (N_DEV, shp[axis] // N_DEV) + shp[axis + 1:])
    return jnp.moveaxis(a, axis, 0).reshape(N_DEV, -1, 128)


def _sh_unpack(p, shape, axis):
    a = p.reshape((N_DEV,) + shape[:axis] + (shape[axis] // N_DEV,) + shape[axis + 1:])
    return jnp.moveaxis(a, 0, axis).reshape(shape)


def _pad_rows(a, mult=8):
    pad = (-a.shape[-2]) % mult
    if pad:
        a = jnp.concatenate([a, jnp.zeros(a.shape[:-2] + (pad, a.shape[-1]), a.dtype)], axis=-2)
    return a


REP = ["even_a_ln_g", "even_a_ln_b", "even_a_ws", "even_a_bs", "even_b_sinks", "even_ln_g", "even_ln_b",
       "odd_w_a", "odd_w_x"]
SH = [("odd_conv_w", (2, 4, W), 2), ("odd_conv_b", (2, W), 1), ("odd_b_a", (2, W), 1), ("odd_b_x", (2, W), 1),
      ("odd_lam", (2, W), 1), ("odd_w_pool", (2, 4, 256, 256), 2), ("odd_d_scale", (2, W), 1),
      ("odd_ln_g", (2, D), 1), ("odd_ln_b", (2, D), 1)]
BIG = ["even_w_in", "even_w_out", "odd_w_in", "odd_w_out"]
NAMES = ["even_w_in", "even_a_ln_g", "even_a_ln_b", "even_a_ws", "even_a_bs", "even_b_sinks", "even_w_out",
         "even_ln_g", "even_ln_b", "odd_w_in", "odd_conv_w", "odd_conv_b", "odd_w_a", "odd_b_a", "odd_w_x", "odd_b_x",
         "odd_lam", "odd_w_pool", "odd_d_scale", "odd_w_out", "odd_ln_g", "odd_ln_b"]


def _rope_table(positions):
    inv = ROPE_THETA ** (-jnp.arange(0, 16, 2, dtype=F32) / 16)
    f = jnp.arange(128) % 64
    ang = positions.astype(F32)[:, None] * inv[f % 8][None, :]
    cos, sin = jnp.cos(ang), jnp.sin(ang)
    return jnp.concatenate([jnp.where(f < 16, cos, 1.0), jnp.where(f < 8, -sin, 0.0),
                            jnp.where((f >= 8) & (f < 16), sin, 0.0)], axis=1)


def kernel(x, positions, even_w_in, even_a_ln_g, even_a_ln_b, even_a_ws, even_a_bs, even_b_sinks, even_w_out, even_ln_g, even_ln_b, odd_w_in, odd_conv_w, odd_conv_b, odd_w_a, odd_b_a, odd_w_x, odd_b_x, odd_lam, odd_w_pool, odd_d_scale, odd_w_out, odd_ln_g, odd_ln_b, loss_target, m_even_w_in, m_even_a_ln_g, m_even_a_ln_b, m_even_a_ws, m_even_a_bs, m_even_b_sinks, m_even_w_out, m_even_ln_g, m_even_ln_b, m_odd_w_in, m_odd_conv_w, m_odd_conv_b, m_odd_w_a, m_odd_b_a, m_odd_w_x, m_odd_b_x, m_odd_lam, m_odd_w_pool, m_odd_d_scale, m_odd_w_out, m_odd_ln_g, m_odd_ln_b, v_even_w_in, v_even_a_ln_g, v_even_a_ln_b, v_even_a_ws, v_even_a_bs, v_even_b_sinks, v_even_w_out, v_even_ln_g, v_even_ln_b, v_odd_w_in, v_odd_conv_w, v_odd_conv_b, v_odd_w_a, v_odd_b_a, v_odd_w_x, v_odd_b_x, v_odd_lam, v_odd_w_pool, v_odd_d_scale, v_odd_w_out, v_odd_ln_g, v_odd_ln_b):
    args = (even_w_in, even_a_ln_g, even_a_ln_b, even_a_ws, even_a_bs, even_b_sinks, even_w_out, even_ln_g, even_ln_b,
            odd_w_in, odd_conv_w, odd_conv_b, odd_w_a, odd_b_a, odd_w_x, odd_b_x, odd_lam, odd_w_pool, odd_d_scale,
            odd_w_out, odd_ln_g, odd_ln_b)
    margs = (m_even_w_in, m_even_a_ln_g, m_even_a_ln_b, m_even_a_ws, m_even_a_bs, m_even_b_sinks, m_even_w_out,
             m_even_ln_g, m_even_ln_b, m_odd_w_in, m_odd_conv_w, m_odd_conv_b, m_odd_w_a, m_odd_b_a, m_odd_w_x,
             m_odd_b_x, m_odd_lam, m_odd_w_pool, m_odd_d_scale, m_odd_w_out, m_odd_ln_g, m_odd_ln_b)
    vargs = (v_even_w_in, v_even_a_ln_g, v_even_a_ln_b, v_even_a_ws, v_even_a_bs, v_even_b_sinks, v_even_w_out,
             v_even_ln_g, v_even_ln_b, v_odd_w_in, v_odd_conv_w, v_odd_conv_b, v_odd_w_a, v_odd_b_a, v_odd_w_x,
             v_odd_b_x, v_odd_lam, v_odd_w_pool, v_odd_d_scale, v_odd_w_out, v_odd_ln_g, v_odd_ln_b)
    wts = dict(zip(NAMES, args))
    mom = dict(zip(NAMES, margs))
    var = dict(zip(NAMES, vargs))
    S = x.shape[1]
    x0 = x[0]
    rope = _rope_table(positions[0])

    kinds = ("even", "odd", "even", "odd")
    blk_in = [jnp.transpose(wts[kinds[l] + "_w_in"][l // 2]).astype(BF) for l in range(4)]
    blk_out = [wts[kinds[l] + "_w_out"][l // 2].astype(BF) for l in range(4)]
    sh_local = _pad_rows(jnp.concatenate([wts[nm].reshape(-1, 128) for nm, _, _ in SH], axis=0), 16)
    me = 4 * lax.axis_index("x") + 2 * lax.axis_index("y") + lax.axis_index("c")
    own_slot = lambda blk: lax.dynamic_update_slice(lax.empty((N_DEV,) + blk.shape, blk.dtype), blk[None], (me, 0, 0))
    reg = {"blk_small": sh_local, "w_small": own_slot(sh_local)}
    sched = _Sched(reg)
    for l in range(4):
        reg[f"blk_in{l}"], reg[f"blk_out{l}"] = blk_in[l], blk_out[l]
        reg[f"w_in{l}"], reg[f"w_out{l}"] = own_slot(blk_in[l]), own_slot(blk_out[l])
    sched.add(_rows("blk_in0", "w_in0", "ag1", blk_in[0].shape[0], ROW_CHUNK[blk_in[0].shape[0]]))
    sched.add(_rows("blk_small", "w_small", "ag1", sh_local.shape[0], sh_local.shape[0]))
    for l in range(4):
        sched.add(_rows(f"blk_out{l}", f"w_out{l}", "ag1", D // N_DEV, ROW_CHUNK[D // N_DEV]))
        if l < 3:
            r = blk_in[l + 1].shape[0]
            sched.add(_rows(f"blk_in{l + 1}", f"w_in{l + 1}", "ag1", r, ROW_CHUNK[r]))

    def gathered(dst, blk):
        sched.flush(dst, FLUSH_EXTRA_US)
        return reg.pop(dst)

    wt_in0 = gathered("w_in0", blk_in[0]).reshape(-1, D)
    full = {nm: wts[nm] for nm in REP}

    def gather_small():
        sh_all = gathered("w_small", sh_local)
        off = 0
        for nm, shape, axis in SH:
            r = wts[nm].size // 128
            full[nm] = _sh_unpack(sh_all[:, off:off + r, :], shape, axis)
            off += r

    saved = []
    wt_in, w_out = [wt_in0, None, None, None], [None] * 4
    xf, xb = x0, x0.astype(BF)
    fwd = lambda name: FWD_OVERBOOK * CARRY_US[name]
    for layer in range(4):
        j = layer // 2
        kind = kinds[layer]
        if wt_in[layer] is None:
            wt_in[layer] = gathered(f"w_in{layer}", blk_in[layer]).reshape(-1, D)
        h = sched.run(_mm_nt, fwd("mm_h_" + kind), xb, wt_in[layer], 1024, 768 if kind == "even" else 512, "mm_h_" + kind)
        if kind == "even":
            bsb = jnp.broadcast_to(full["even_a_bs"][j][:, :, None], (8, 128, 128))
            mix3, o, l = sched.run(_even_fwd, fwd("even_fwd"), h, rope, full["even_a_ln_g"][j], full["even_a_ln_b"][j],
                                   full["even_a_ws"][j], bsb, full["even_b_sinks"][j], "even_fwd")
            extra = (o, l, bsb)
        else:
            if "odd_lam" not in full:
                gather_small()
            wa, wx = full["odd_w_a"][j].astype(BF), full["odd_w_x"][j].astype(BF)
            wp = full["odd_w_pool"][j].astype(BF)
            mix3, hst = sched.run(_odd_c_fwd, fwd("odd_c_fwd"), h, full["odd_conv_w"][j], full["odd_conv_b"][j], wa, wx,
                                  full["odd_b_a"][j], full["odd_b_x"][j], full["odd_lam"][j], "odd_c_fwd")
            mix3 = _odd_d_fwd(h, mix3, wp, full["odd_d_scale"][j], "odd_d_fwd")
            extra = (hst, wa, wx, wp)
        w_out[layer] = gathered(f"w_out{layer}", blk_out[layer]).reshape(D, D)
        z, xn, xnb = sched.run(_mm_out_ln, fwd("mm_out_ln"), mix3, w_out[layer], xf, full[kind + "_ln_g"][j],
                               full[kind + "_ln_b"][j], "mm_out_ln")
        saved.append((xb, h, mix3, z, extra))
        xf, xb = xn, xnb

    dxn = xf

    gsum = {nm: [None, None] for nm in NAMES}

    chip_sums = {}
    sched.overhang = 0.15

    waiting = []

    def chip_sum(g, tag, key):
        r = g.shape[0] // N_DEV
        reg["g_" + key] = g.reshape(N_DEV, r, D)
        sched.add(_rows("g_" + key, "d_" + key, "rsd", r, r), first=True)
        waiting.append((key, tag))

    def add_arrived():
        for key, tag in list(waiting):
            if "d_" + key in reg and not sched.pending("d_" + key):
                waiting.remove((key, tag))
                g8 = reg.pop("g_" + key)
                chip_sums[key] = reg["s_" + key] = _add_pairs(g8, reg.pop("d_" + key), "rs_add_" + tag)
                sched.add(_rows("s_" + key, "r_" + key, "rs", g8.shape[1], ROW_CHUNK[g8.shape[1]] // 2))

    sched.after_landing = add_arrived

    def reduced(key):
        sched.flush("d_" + key, FLUSH_EXTRA_US)
        sched.flush("r_" + key, FLUSH_EXTRA_US)
        return chip_sums[key], reg.pop("r_" + key)

    for layer in (3, 2, 1, 0):
        j = layer // 2
        xb, h, mix3, z, extra = saved[layer]
        kind = kinds[layer]
        if layer == 3:
            dz, dzb, dg, dbeta, part = sched.run(_ln_bwd, CARRY_US["ln_bwd"], dxn, z, full[kind + "_ln_g"][j], "loss_ln_bwd",
                                                 target=loss_target[0])
        else:
            dz, dzb, dg, dbeta = sched.run(_ln_bwd, CARRY_US["ln_bwd"], dxn, z, full[kind + "_ln_g"][j], "ln_bwd")
        gsum[kind + "_ln_g"][j] = dg.reshape(D)
        gsum[kind + "_ln_b"][j] = dbeta.reshape(D)
        chip_sum(sched.run(_mm_tn, CARRY_US["mm_dw_out"], mix3, dzb, 512, "mm_dw_out"), "w_out", f"out{layer}")
        dmix3 = sched.run(_mm_nt, CARRY_US["mm_dmix"], dzb, w_out[layer], 1024, 512, "mm_dmix", out3=True)
        if kind == "even":
            o, l, bsb = extra
            ws = full["even_a_ws"][j]
            dh, dws, dbs, dlng, dlnb, dsink = sched.run(
                _even_bwd, CARRY_US["even_bwd"], h, dmix3, o, l, rope, full["even_a_ln_g"][j], full["even_a_ln_b"][j],
                ws, jnp.swapaxes(ws, 1, 2), bsb, full["even_b_sinks"][j], "even_bwd")
            gsum["even_a_ws"][j] = dws
            gsum["even_a_bs"][j] = jnp.transpose(dbs[:, :8])
            gsum["even_a_ln_g"][j] = dlng.reshape(W)
            gsum["even_a_ln_b"][j] = dlnb.reshape(W)
            gsum["even_b_sinks"][j] = dsink[0, :16]
            if layer == 0:
                rep_rows = [_rep_pack(jnp.stack(gsum[nm]).reshape(wts[nm].shape)) for nm in REP]
                sh_rows = [_sh_pack(jnp.stack(gsum[nm]).reshape(shape), axis) for nm, shape, axis in SH]
                packed = _pad_rows(jnp.concatenate(rep_rows + sh_rows, axis=1))
                gw, (small8, parts) = _mm_tn(dh, xb, 384, "mm_dw_in_even", comm=_Join([_ExchangeAll(packed), _GatherAll(part)]))
                loss = jnp.sum(parts[:, 0, 0]) * (0.5 / D)
            else:
                gw = sched.run(_mm_tn, CARRY_US["mm_dw_in_even"], dh, xb, 384, "mm_dw_in_even")
            chip_sum(gw, "w_in_even", f"in{layer}")
            if layer == 0:
                n_rep = sum(p.shape[1] for p in rep_rows)
                red = _sum8(small8, 1 << 20, "sum_small")
                (rep_all,) = sched.flush("d_in0", FLUSH_EXTRA_US, beside=_GatherAll(_pad_rows(red[:n_rep])))
                sched.overhang = 0.6
            dxn = sched.run(_mm_nn_res, CARRY_US["mm_dx_even"], dh, wt_in[layer], dz, 512, 512, "mm_dx_even")
        else:
            hst, wa, wx, wp = extra
            dh4, dcw, dcb, dwa, dwx, dba, dbx, dlam = sched.run(
                _odd_c_bwd, CARRY_US["odd_c_bwd"], h, hst, dmix3, full["odd_conv_w"][j], full["odd_conv_b"][j], wa, wx,
                jnp.swapaxes(wa, 1, 2), jnp.swapaxes(wx, 1, 2), full["odd_b_a"][j], full["odd_b_x"][j], full["odd_lam"][j],
                "odd_c_bwd")
            dh4, dwp, dds = _odd_d_bwd(h, dmix3, dh4, wp, jnp.swapaxes(wp, 1, 2), full["odd_d_scale"][j], "odd_d_bwd")
            gsum["odd_conv_w"][j], gsum["odd_conv_b"][j] = dcw, dcb.reshape(W)
            gsum["odd_w_a"][j], gsum["odd_w_x"][j] = dwa, dwx
            gsum["odd_b_a"][j], gsum["odd_b_x"][j], gsum["odd_lam"][j] = dba.reshape(W), dbx.reshape(W), dlam.reshape(W)
            gsum["odd_w_pool"][j], gsum["odd_d_scale"][j] = dwp, dds.reshape(W)
            chip_sum(sched.run(_mm_tn, CARRY_US["mm_dw_in_odd"], dh4, xb, 512, "mm_dw_in_odd"), "w_in_odd", f"in{layer}")
            dxn = sched.run(_mm_nn_res, CARRY_US["mm_dx_odd"], dh4, wt_in[layer], dz, 512, 512, "mm_dx_odd")
    grad_x = dxn[None]

    out_g, out_d, out_m, out_v = {}, {}, {}, {}
    for nm, kind, what, layers in (("odd_w_out", "odd", "out", (1, 3)), ("even_w_out", "even", "out", (0, 2)),
                                   ("odd_w_in", "odd", "in", (1, 3)), ("even_w_in", "even", "in", (0, 2))):
        gl = [reduced(f"{what}{l}") for l in layers]
        if nm == "even_w_in":
            view = lambda a: jnp.transpose(a, (0, 2, 1))
            res, _ = _adamw(view(wts[nm]), gl, view(mom[nm]), view(var[nm]), 112, f"adamw_{nm}")
            res = [view(a) for a in res]
        elif what == "in":
            gs = [jnp.transpose(_rs_final(s4, r3, "rs_final_w_in_odd")) for s4, r3 in gl]
            res, _ = _adamw(wts[nm], gs, mom[nm], var[nm], 512, f"adamw_{nm}")
        else:
            res = sched.run(_adamw, CARRY_US["adamw_" + nm], wts[nm], gl, mom[nm], var[nm], 128, f"adamw_{nm}")
        out_d[nm], out_m[nm], out_v[nm], out_g[nm] = res

    g_small = {}
    off = 0
    for nm, p in zip(REP, rep_rows):
        r = p.shape[1]
        g_small[nm] = _rep_unpack(rep_all[:, off:off + r, :], wts[nm].shape)
        off += r
    off = n_rep
    for (nm, shape, axis), p in zip(SH, sh_rows):
        r = p.shape[1]
        g_small[nm] = red[off:off + r].reshape(wts[nm].shape)
        off += r

    def rows(a):
        f = a.reshape(-1)
        pad = (-f.shape[0]) % 128
        if pad:
            f = jnp.concatenate([f, jnp.zeros((pad,), a.dtype)])
        return f.reshape(-1, 128)

    small = REP + [nm for nm, _, _ in SH]
    each = lambda src: [rows(src[nm]) for nm in small]
    d2, m2, v2 = _adamw_many(each(wts), each(g_small), each(mom), each(var), "adamw_small")
    for i, nm in enumerate(small):
        n, shp = wts[nm].size, wts[nm].shape
        take = lambda a: a.reshape(-1)[:n].reshape(shp)
        out_g[nm], out_d[nm], out_m[nm], out_v[nm] = g_small[nm], take(d2[i]), take(m2[i]), take(v2[i])

    return (loss, grad_x, *[out_g[nm] for nm in NAMES], *[out_d[nm] for nm in NAMES],
            *[out_m[nm] for nm in NAMES], *[out_v[nm] for nm in NAMES])
```

```python
import functools

import jax
import jax.numpy as jnp
from jax import lax
from jax.experimental import pallas as pl
from jax.experimental.pallas import tpu as pltpu

F32 = jnp.float32
BF = jnp.bfloat16
MESH = pl.DeviceIdType.MESH
ANY = pl.BlockSpec(memory_space=pl.ANY)

N_DEV = 8
D = 2048
W = 1024
EVEN_IN = 5376
ODD_IN = 4096
CHUNK = 128
ALPHA = (2 * 4) ** 0.25
LN_EPS = 1e-5
ROPE_THETA = 500000.0
LRU_C = 8.0
LR, B1, B2, ADAM_EPS, WD, STEP = 0.001, 0.9, 0.999, 1e-08, 0.01, 10
NEG = -1e30
HEAD_COLS = 4


def _cp(vmem_mb=48, collective_id=None):
    return pltpu.CompilerParams(vmem_limit_bytes=vmem_mb * 1024 * 1024, collective_id=collective_id)


def _sig(x):
    return jax.nn.sigmoid(x)


def _silu_grad(x):
    s = _sig(x)
    return x * s, s * (1.0 + x * (1.0 - s))


def _dot(a, b):
    return jnp.dot(a, b, preferred_element_type=F32)


def _dot_nt(a, b):
    return lax.dot_general(a, b, (((1,), (1,)), ((), ())), preferred_element_type=F32)


def _dot_tn(a, b):
    return lax.dot_general(a, b, (((0,), (0,)), ((), ())), preferred_element_type=F32)


def _coords():
    return lax.axis_index("x"), lax.axis_index("y"), lax.axis_index("c")


def _chip(j):
    x, y, _ = _coords()
    return (1 - x if j & 2 else x), (1 - y if j & 1 else y)


X_NB, Y_NB, DIAG, SIB = 4, 2, 6, 1
EVERYONE = frozenset(range(1, N_DEV))
BARRIER_IDS = {}


class _Comm:
    def collective_id(self):
        return BARRIER_IDS.setdefault(frozenset(self.peers), len(BARRIER_IDS))

    def start(self, ins, outs, sems):
        barrier = pltpu.get_barrier_semaphore()
        for d in sorted(self.peers):
            pl.semaphore_signal(barrier, inc=1, device_id=_peer(d)[0], device_id_type=MESH)
        pl.semaphore_wait(barrier, len(self.peers))
        for cp in self.copies(ins, outs, sems):
            cp.start()

    def wait(self, ins, outs, sems):
        for cp in self.copies(ins, outs, sems):
            cp.wait()


class _Join(_Comm):
    def __init__(self, parts):
        self.parts = list(parts)
        self.peers = frozenset().union(*[p.peers for p in self.parts])
        self.inputs = [a for p in self.parts for a in p.inputs]
        self.out_shapes = [s for p in self.parts for s in p.out_shapes]
        self.sem_shapes = [s for p in self.parts for s in p.sem_shapes]
        self.aliases = {}
        i0 = o0 = 0
        for p in self.parts:
            for i, o in getattr(p, "aliases", {}).items():
                self.aliases[i0 + i] = o0 + o
            i0, o0 = i0 + len(p.inputs), o0 + len(p.out_shapes)

    def copies(self, ins, outs, sems):
        res = []
        i0 = o0 = s0 = 0
        for p in self.parts:
            ni, no, ns = len(p.inputs), len(p.out_shapes), len(p.sem_shapes)
            res += p.copies(ins[i0:i0 + ni], outs[o0:o0 + no], sems[s0:s0 + ns])
            i0, o0, s0 = i0 + ni, o0 + no, s0 + ns
        return res


ROWS_US = {"ag1": 0.104, "ag2": 0.052, "agd": 0.027, "rsd": 0.027, "rs": 0.205}
N_COPIES = {"ag1": 2, "ag2": 2, "agd": 4, "rsd": 4, "rs": 3}
TASK_PEERS = {"ag1": {X_NB, Y_NB}, "ag2": {X_NB, Y_NB}, "agd": {SIB}, "rsd": {SIB}, "rs": {X_NB, Y_NB, DIAG}}
ROW_CHUNK = {672: 224, 512: 128, 256: 128}
CARRY_US = {"mm_h_even": 58, "mm_h_odd": 47, "even_fwd": 42, "odd_c_fwd": 37, "mm_out_ln": 33, "ln_bwd": 23, "mm_dmix": 26,
            "mm_dw_out": 25, "even_bwd": 90, "odd_c_bwd": 58, "mm_dw_in_even": 58, "mm_dw_in_odd": 44, "mm_dx_even": 66,
            "mm_dx_odd": 55, "adamw_even_w_out": 11, "adamw_odd_w_out": 11}
FWD_OVERBOOK = 1.15
FLUSH_EXTRA_US = 60.0


def _cost_us(task, reg):
    kind, src, _, lo, hi = task
    return ROWS_US[kind] * (hi - lo) * reg[src].shape[-1] * reg[src].dtype.itemsize / 4096.0


class _Copies(_Comm):
    def __init__(self, tasks, reg):
        self.tasks = list(tasks)
        self.out_names, self.in_names = [], []
        for kind, src, dst, lo, hi in self.tasks:
            if dst not in self.out_names:
                self.out_names.append(dst)
        for kind, src, dst, lo, hi in self.tasks:
            if src not in self.out_names and src not in self.in_names:
                self.in_names.append(src)
        self.out_shapes, self.aliases = [], {}
        for o, dst in enumerate(self.out_names):
            if dst in reg:
                self.aliases[len(self.in_names)] = o
                self.in_names.append(dst)
                self.out_shapes.append(jax.ShapeDtypeStruct(reg[dst].shape, reg[dst].dtype))
            else:
                kind, src = next((t[0], t[1]) for t in self.tasks if t[2] == dst)
                shape = ({"rsd": 4, "rs": 3}[kind],) + reg[src].shape[1:]
                self.out_shapes.append(jax.ShapeDtypeStruct(shape, reg[src].dtype))
        self.inputs = [reg[nm] for nm in self.in_names]
        n = sum(N_COPIES[t[0]] for t in self.tasks)
        self.sem_shapes = [pltpu.SemaphoreType.DMA((n,)), pltpu.SemaphoreType.DMA((n,))]
        self.peers = frozenset().union(*[TASK_PEERS[t[0]] for t in self.tasks])

    def copies(self, ins, outs, sems):
        send, recv = sems
        x, y, c = _coords()
        me = 4 * x + 2 * y + c
        xn, yn = (1 - x, y, c), (x, 1 - y, c)
        at_xn, at_yn = 4 * (1 - x) + 2 * y + c, 4 * x + 2 * (1 - y) + c
        ref = dict(zip(self.in_names, ins))
        ref.update(zip(self.out_names, outs))
        res = []

        def copy(src, dst, to):
            i = len(res)
            res.append(pltpu.make_async_remote_copy(src_ref=src, dst_ref=dst, send_sem=send.at[i], recv_sem=recv.at[i],
                                                    device_id=to, device_id_type=MESH))

        for kind, src, dst, lo, hi in self.tasks:
            n = hi - lo
            if kind == "ag1":
                for to in (xn, yn):
                    copy(ref[src].at[pl.ds(lo, n)], ref[dst].at[me, pl.ds(lo, n)], to)
            elif kind == "ag2":
                h = n // 2
                first, second = ref[dst].at[at_xn, pl.ds(lo, h)], ref[dst].at[at_yn, pl.ds(lo + h, n - h)]
                copy(first, first, yn)
                copy(second, second, xn)
            elif kind == "agd":
                for j in range(4):
                    px, py = _chip(j)
                    rows = ref[dst].at[4 * px + 2 * py + c, pl.ds(lo, n)]
                    copy(rows, rows, (x, y, 1 - c))
            elif kind == "rsd":
                for j in range(4):
                    px, py = _chip(j)
                    copy(ref[src].at[4 * px + 2 * py + 1 - c, pl.ds(lo, n)], ref[dst].at[j, pl.ds(lo, n)], (x, y, 1 - c))
            else:
                for j in (1, 2, 3):
                    px, py = _chip(j)
                    copy(ref[src].at[j, pl.ds(lo, n)], ref[dst].at[j - 1, pl.ds(lo, n)], (px, py, c))
        return res


class _Sched:
    def __init__(self, reg):
        self.reg, self.queue, self.later = reg, [], []
        self.overhang = 0.5
        self.after_landing = None

    def add(self, tasks, first=False):
        self.queue = list(tasks) + self.queue if first else self.queue + list(tasks)

    def pending(self, dst):
        return any(t[2] == dst for t in self.queue + self.later)

    def take(self, budget_us, must=None, overhang=0.5):
        self.queue, self.later = self.later + self.queue, []
        picked, used = [], {True: 0.0, False: 0.0}
        rest = []
        for t in self.queue:
            cost, d2d = _cost_us(t, self.reg), t[0] in ("agd", "rsd")
            if (must is not None and t[2] == must) or used[d2d] + (1.0 - overhang) * cost <= budget_us:
                picked.append(t)
                used[d2d] += cost
                if t[0] in ("ag1", "ag2"):
                    self.later.append(({"ag1": "ag2", "ag2": "agd"}[t[0]], t[2], t[2], t[3], t[4]))
            else:
                rest.append(t)
        self.queue = rest
        return _Copies(picked, self.reg) if picked else None

    def landed(self, comm, got):
        if comm is not None:
            for nm, a in zip(comm.out_names, got):
                self.reg[nm] = a
        if self.after_landing is not None:
            self.after_landing()

    def run(self, builder, budget_us, *args, **kw):
        comm = self.take(budget_us, overhang=self.overhang)
        res, got = builder(*args, comm=comm, **kw)
        self.landed(comm, got)
        return res

    def flush(self, dst, budget_us=0.0, beside=None):
        res = []
        while self.pending(dst):
            comm = self.take(budget_us, must=dst)
            got = _comm_only(comm if beside is None else _Join([comm, beside]), "flush_" + dst)
            res, beside = got[len(comm.out_shapes):], None
            self.landed(comm, got[:len(comm.out_shapes)])
        return res


def _rows(name_src, name_dst, kind, n_rows, chunk):
    return [(kind, name_src, name_dst, lo, min(lo + chunk, n_rows)) for lo in range(0, n_rows, chunk)]


def _pcall(body, *, grid, in_specs, out_specs, out_shape, name, scratch=(), vmem=48, comm=None):
    in_specs, out_specs, out_shape, scratch = list(in_specs), list(out_specs), list(out_shape), list(scratch)
    if comm is None:
        call = pl.pallas_call(body, grid=grid, in_specs=in_specs, out_specs=out_specs, out_shape=out_shape,
                              scratch_shapes=scratch, name=name, compiler_params=_cp(vmem))
        return lambda *args: (call(*args), [])
    n_in, n_out, n_scr = len(in_specs), len(out_specs), len(scratch)
    c_in, c_out = len(comm.inputs), len(comm.out_shapes)
    aliases = {n_in + i: n_out + o for i, o in getattr(comm, "aliases", {}).items()}

    def wrapped(*refs):
        ins, cins = refs[:n_in], refs[n_in:n_in + c_in]
        o0 = n_in + c_in
        outs, couts = refs[o0:o0 + n_out], refs[o0 + n_out:o0 + n_out + c_out]
        s0 = o0 + n_out + c_out
        scr, sems = refs[s0:s0 + n_scr], refs[s0 + n_scr:]
        ids = [pl.program_id(a) for a in range(len(grid))]
        first = functools.reduce(jnp.logical_and, [i == 0 for i in ids])
        last = functools.reduce(jnp.logical_and, [i == g - 1 for i, g in zip(ids, grid)])

        @pl.when(first)
        def _():
            comm.start(cins, couts, sems)

        body(*ins, *outs, *scr)

        @pl.when(last)
        def _():
            comm.wait(cins, couts, sems)

    call = pl.pallas_call(wrapped, grid=grid, in_specs=in_specs + [ANY] * c_in, out_specs=out_specs + [ANY] * c_out,
                          out_shape=out_shape + list(comm.out_shapes), scratch_shapes=scratch + list(comm.sem_shapes),
                          input_output_aliases=aliases, name=name, compiler_params=_cp(vmem, comm.collective_id()))

    def run(*args):
        res = call(*args, *comm.inputs)
        return res[:n_out], res[n_out:]

    return run


def _comm_only(comm, name):
    c_in, c_out = len(comm.inputs), len(comm.out_shapes)

    def body(*refs):
        cins, couts, sems = refs[:c_in], refs[c_in:c_in + c_out], refs[c_in + c_out:]
        comm.start(cins, couts, sems)
        comm.wait(cins, couts, sems)

    return pl.pallas_call(body, in_specs=[ANY] * c_in, out_specs=[ANY] * c_out, out_shape=list(comm.out_shapes),
                          scratch_shapes=list(comm.sem_shapes), input_output_aliases=dict(getattr(comm, "aliases", {})),
                          name=name, compiler_params=pltpu.CompilerParams(collective_id=comm.collective_id()))(*comm.inputs)


def _chip_blocks():
    _, _, c = _coords()
    return jnp.stack([4 * px + 2 * py + c for px, py in map(_chip, range(4))]).astype(jnp.int32)


def _add_pairs(g8, b4, name):
    _, R, C = b4.shape

    def body(idx_ref, a_ref, b_ref, o_ref):
        o_ref[...] = (a_ref[...].astype(F32) + b_ref[...].astype(F32)).astype(BF)

    blk = pl.BlockSpec((None, R, C), lambda j, idx: (j, 0, 0))
    grid_spec = pltpu.PrefetchScalarGridSpec(
        num_scalar_prefetch=1, grid=(4,),
        in_specs=[pl.BlockSpec((None, R, C), lambda j, idx: (idx[j], 0, 0)), blk], out_specs=blk)
    return pl.pallas_call(body, grid_spec=grid_spec, out_shape=jax.ShapeDtypeStruct(b4.shape, BF), name=name,
                          compiler_params=_cp())(_chip_blocks(), g8, b4)


def _rs_final(s4, r3, name):
    _, R, C = s4.shape
    tr = R // 2

    def body(s_ref, r_ref, o_ref):
        o_ref[...] = ((s_ref[...].astype(F32) + r_ref[0].astype(F32)) + r_ref[1].astype(F32)) + r_ref[2].astype(F32)

    return pl.pallas_call(
        body, grid=(2,),
        in_specs=[pl.BlockSpec((None, tr, C), lambda i: (0, i, 0)), pl.BlockSpec((3, tr, C), lambda i: (0, i, 0))],
        out_specs=pl.BlockSpec((tr, C), lambda i: (i, 0)), out_shape=jax.ShapeDtypeStruct((R, C), F32),
        name=name, compiler_params=_cp())(s4, r3)


def _mm_nt(a, w, tm, tn, name, out3=False, comm=None):
    M, K = a.shape
    N = w.shape[0]
    tm = min(tm, M)

    def body(a_ref, w_ref, o_ref):
        o_ref[...] = _dot_nt(a_ref[...], w_ref[...])

    if out3:
        per = W // tn
        out_shape = jax.ShapeDtypeStruct((N // W, M, W), F32)
        out_spec = pl.BlockSpec((None, tm, tn), lambda i, j: (j // per, i, j % per))
    else:
        out_shape = jax.ShapeDtypeStruct((M, N), F32)
        out_spec = pl.BlockSpec((tm, tn), lambda i, j: (i, j))
    (res,), extra = _pcall(
        body, grid=(M // tm, N // tn),
        in_specs=[pl.BlockSpec((tm, K), lambda i, j: (i, 0)), pl.BlockSpec((tn, K), lambda i, j: (j, 0))],
        out_specs=[out_spec], out_shape=[out_shape], name=name, comm=comm)(a, w)
    return res, extra


def _mm_tn(a, b, tm, name, comm=None):
    K, N = b.shape
    if a.ndim == 3:
        M = a.shape[0] * W
        per = W // tm
        a_spec = pl.BlockSpec((None, K, tm), lambda i: (i // per, 0, i % per))
    else:
        M = a.shape[1]
        a_spec = pl.BlockSpec((K, tm), lambda i: (0, i))

    def body(a_ref, b_ref, o_ref):
        o_ref[...] = _dot_tn(a_ref[...], b_ref[...]).astype(BF)

    (out,), extra = _pcall(
        body, grid=(M // tm,),
        in_specs=[a_spec, pl.BlockSpec((K, N), lambda i: (0, 0))],
        out_specs=[pl.BlockSpec((tm, N), lambda i: (i, 0))],
        out_shape=[jax.ShapeDtypeStruct((M, N), BF)], name=name, vmem=56, comm=comm)(a, b)
    return out, extra


def _mm_nn_res(a, w, res, tm, tn, name, comm=None):
    K, N = w.shape
    if a.ndim == 3:
        P, M = a.shape[0], a.shape[1]
        tm = min(tm, M)
        a_spec = pl.BlockSpec((P, tm, W), lambda j, i: (0, i, 0))
    else:
        P, M = 0, a.shape[0]
        tm = min(tm, M)
        a_spec = pl.BlockSpec((tm, K), lambda j, i: (i, 0))

    def body(a_ref, w_ref, r_ref, o_ref):
        if P:
            d = _dot(a_ref[0], w_ref[0:W, :])
            for p in range(1, P):
                d = d + _dot(a_ref[p], w_ref[p * W:(p + 1) * W, :])
        else:
            d = _dot(a_ref[...], w_ref[...])
        o_ref[...] = ALPHA * r_ref[...] + d

    (out,), extra = _pcall(
        body, grid=(N // tn, M // tm),
        in_specs=[a_spec, pl.BlockSpec((K, tn), lambda j, i: (0, j)), pl.BlockSpec((tm, tn), lambda j, i: (i, j))],
        out_specs=[pl.BlockSpec((tm, tn), lambda j, i: (i, j))],
        out_shape=[jax.ShapeDtypeStruct((M, N), F32)], name=name, comm=comm)(a, w, res)
    return out, extra


def _mm_out_ln(mix3, w_out, x, g, b, name, comm=None):
    S = x.shape[0]
    tm = min(256, S)

    def body(m_ref, w_ref, x_ref, g_ref, b_ref, z_ref, xn_ref, xb_ref):
        acc = _dot(m_ref[0], w_ref[0:W, :]) + _dot(m_ref[1], w_ref[W:2 * W, :])
        z = ALPHA * x_ref[...] + acc
        mu = jnp.mean(z, axis=1, keepdims=True)
        zc = z - mu
        var = jnp.mean(zc * zc, axis=1, keepdims=True)
        xn = zc * lax.rsqrt(var + LN_EPS) * g_ref[...] + b_ref[...]
        z_ref[...] = z
        xn_ref[...] = xn
        xb_ref[...] = xn.astype(BF)

    row = pl.BlockSpec((tm, D), lambda i: (i, 0))
    vec = pl.BlockSpec((1, D), lambda i: (0, 0))
    return _pcall(
        body, grid=(S // tm,),
        in_specs=[pl.BlockSpec((2, tm, W), lambda i: (0, i, 0)), pl.BlockSpec((D, D), lambda i: (0, 0)), row, vec, vec],
        out_specs=[row, row, row],
        out_shape=[jax.ShapeDtypeStruct((S, D), F32), jax.ShapeDtypeStruct((S, D), F32), jax.ShapeDtypeStruct((S, D), BF)],
        name=name, comm=comm)(mix3, w_out, x, g.reshape(1, D), b.reshape(1, D))


def _ln_bwd(dxn, z, g, name, comm=None, target=None):
    S = z.shape[0]
    tm = min(256, S)
    head = target is not None

    def body(*refs):
        if head:
            d_ref, t_ref, z_ref, g_ref, dz_ref, dzb_ref, dg_ref, db_ref, p_ref = refs
        else:
            d_ref, z_ref, g_ref, dz_ref, dzb_ref, dg_ref, db_ref = refs
        i = pl.program_id(0)
        zz = z_ref[...]
        mu = jnp.mean(zz, axis=1, keepdims=True)
        zc = zz - mu
        var = jnp.mean(zc * zc, axis=1, keepdims=True)
        rstd = lax.rsqrt(var + LN_EPS)
        xhat = zc * rstd
        dy = d_ref[...]
        if head:
            e = dy - t_ref[...]
            dy = e * (1.0 / D)

            @pl.when(i == 0)
            def _():
                p_ref[...] = jnp.zeros_like(p_ref)

            p_ref[...] += jnp.sum(jnp.sum(e * e, axis=1, keepdims=True), axis=0, keepdims=True)
        dyg = dy * g_ref[...]
        m1 = jnp.mean(dyg, axis=1, keepdims=True)
        m2 = jnp.mean(dyg * xhat, axis=1, keepdims=True)
        dz = rstd * (dyg - m1 - xhat * m2)
        dz_ref[...] = dz
        dzb_ref[...] = dz.astype(BF)

        @pl.when(i == 0)
        def _():
            dg_ref[...] = jnp.zeros_like(dg_ref)
            db_ref[...] = jnp.zeros_like(db_ref)

        dg_ref[...] += jnp.sum(dy * xhat, axis=0, keepdims=True)
        db_ref[...] += jnp.sum(dy, axis=0, keepdims=True)

    row = pl.BlockSpec((tm, D), lambda i: (i, 0))
    vec = pl.BlockSpec((1, D), lambda i: (0, 0))
    out_specs = [row, row, vec, vec] + ([pl.BlockSpec((8, 128), lambda i: (0, 0))] if head else [])
    out_shape = [jax.ShapeDtypeStruct((S, D), F32), jax.ShapeDtypeStruct((S, D), BF), jax.ShapeDtypeStruct((1, D), F32),
                 jax.ShapeDtypeStruct((1, D), F32)] + ([jax.ShapeDtypeStruct((8, 128), F32)] if head else [])
    operands = (dxn, target, z, g.reshape(1, D)) if head else (dxn, z, g.reshape(1, D))
    return _pcall(body, grid=(S // tm,), in_specs=[row] * (len(operands) - 1) + [vec], out_specs=out_specs,
                  out_shape=out_shape, name=name, comm=comm)(*operands)


def _rope_fwd(t, r_ref):
    return (t * r_ref[:, 0:128] + pltpu.roll(t, 120, 1) * r_ref[:, 128:256]
            + pltpu.roll(t, 8, 1) * r_ref[:, 256:384])


def _rope_bwd(g, r_ref):
    return (g * r_ref[:, 0:128] + pltpu.roll(g * r_ref[:, 128:256], 8, 1)
            + pltpu.roll(g * r_ref[:, 256:384], 120, 1))


def _dup_heads(kb):
    lo = lax.broadcasted_iota(jnp.int32, kb.shape, 1) < 64
    sw = pltpu.roll(kb, 64, 1)
    return [jnp.where(lo, kb, sw).astype(BF), jnp.where(lo, sw, kb).astype(BF)]


def _even_fwd(h, rope, lng, lnb, ws, bsb, sinks, name, comm=None):
    S = h.shape[0]
    nb = S // CHUNK

    def body(h_ref, hp_ref, rc_ref, rp_ref, lng_ref, lnb_ref, ws_ref, bsb_ref, sink_ref, mix_ref, o_ref, l_ref):
        n = pl.program_id(0)
        lane = lax.broadcasted_iota(jnp.int32, (128, 128), 1)
        rowi = lax.broadcasted_iota(jnp.int32, (128, 128), 0)
        tri = rowi >= lane
        lane_lo = lane < 64
        v = h_ref[:, W:2 * W]
        mu = jnp.mean(v, axis=1, keepdims=True)
        vc = v - mu
        var = jnp.mean(vc * vc, axis=1, keepdims=True)
        vn = vc * lax.rsqrt(var + LN_EPS) * lng_ref[...] + lnb_ref[...]
        ms = [_dot(jnp.where(tri, ws_ref[g], 0.0).astype(BF), vn[:, g * 128:(g + 1) * 128].astype(BF)) for g in range(8)]
        for g in range(8):
            sl = slice(g * 128, (g + 1) * 128)
            ag = h_ref[:, 2 * W + g * 128:2 * W + (g + 1) * 128]
            mix_ref[0, :, sl] = (h_ref[:, sl] * (ms[g] + bsb_ref[g]) * (ag * _sig(ag))).astype(BF)
        kb = jnp.concatenate([_rope_fwd(hp_ref[:, 0:128], rp_ref), _rope_fwd(h_ref[:, 4096:4224], rc_ref)], axis=0)
        vb = jnp.concatenate([hp_ref[:, 128:256], h_ref[:, 4224:4352]], axis=0)
        k2 = _dup_heads(kb)
        v2 = _dup_heads(vb)
        qi = lax.broadcasted_iota(jnp.int32, (128, 256), 0)
        kj = lax.broadcasted_iota(jnp.int32, (128, 256), 1)
        diff = qi + 128 - kj
        valid = (diff >= 0) & (diff < 128) & ((n > 0) | (kj >= 128))
        lacc = jnp.zeros((128, 128), F32)
        for j0 in range(0, 8, HEAD_COLS):
            heads = [(j, half) for j in range(j0, j0 + HEAD_COLS) for half in range(2)]
            sc, pr, oh = {}, {}, {}
            for j in range(j0, j0 + HEAD_COLS):
                qc = _rope_fwd(h_ref[:, 3072 + j * 128:3072 + (j + 1) * 128], rc_ref)
                sc[j, 0] = _dot_nt(jnp.where(lane_lo, qc, 0.0).astype(BF), k2[j // 4])
                sc[j, 1] = _dot_nt(jnp.where(lane_lo, 0.0, qc).astype(BF), k2[j // 4])
            for j, half in heads:
                hq = 2 * j + half
                s = jnp.where(valid, sc[j, half] * 0.125, NEG)
                sk = sink_ref[hq]
                mx = jnp.maximum(jnp.max(s, axis=1, keepdims=True), sk)
                p = jnp.exp(s - mx)
                den = jnp.sum(p, axis=1, keepdims=True) + jnp.exp(sk - mx)
                pr[j, half] = (p / den).astype(BF)
                lacc = jnp.where(lane == hq, mx + jnp.log(den), lacc)
            for j, half in heads:
                oh[j, half] = _dot(pr[j, half], v2[j // 4])
            for j in range(j0, j0 + HEAD_COLS):
                cs = slice(j * 128, (j + 1) * 128)
                ocol = jnp.where(lane_lo, oh[j, 0], oh[j, 1])
                bg = h_ref[:, 4352 + j * 128:4352 + (j + 1) * 128]
                o_ref[:, cs] = ocol
                mix_ref[1, :, cs] = (ocol * (bg * _sig(bg))).astype(BF)
        l_ref[...] = lacc

    prev = lambda n: jnp.maximum(n - 1, 0)
    full = lambda shape: pl.BlockSpec(shape, lambda n: (0,) * len(shape))
    return _pcall(
        body, grid=(nb,),
        in_specs=[pl.BlockSpec((CHUNK, EVEN_IN), lambda n: (n, 0)),
                  pl.BlockSpec((CHUNK, 256), lambda n: (prev(n), 16)),
                  pl.BlockSpec((CHUNK, 384), lambda n: (n, 0)),
                  pl.BlockSpec((CHUNK, 384), lambda n: (prev(n), 0)),
                  full((1, W)), full((1, W)), full((8, 128, 128)), full((8, 128, 128)),
                  pl.BlockSpec(memory_space=pltpu.SMEM)],
        out_specs=[pl.BlockSpec((2, CHUNK, W), lambda n: (0, n, 0)),
                   pl.BlockSpec((CHUNK, W), lambda n: (n, 0)),
                   pl.BlockSpec((CHUNK, 128), lambda n: (n, 0))],
        out_shape=[jax.ShapeDtypeStruct((2, S, W), BF), jax.ShapeDtypeStruct((S, W), F32),
                   jax.ShapeDtypeStruct((S, 128), F32)],
        name=name, comm=comm)(h, h, rope, rope, lng.reshape(1, W), lnb.reshape(1, W), ws, bsb, sinks)


def _even_bwd(h, dmix3, o, l, rope, lng, lnb, ws, wst, bsb, sinks, name, comm=None):
    S = h.shape[0]
    nb = S // CHUNK

    def body(h_ref, hp_ref, hn_ref, dm_ref, dmn_ref, o_ref, on_ref, l_ref, ln_ref, rc_ref, rp_ref, rn_ref,
             lng_ref, lnb_ref, ws_ref, wst_ref, bsb_ref, sink_ref,
             dh_ref, dws_ref, dbs_ref, dlng_ref, dlnb_ref, dsink_ref, dvn_ref):
        n = pl.program_id(0)

        @pl.when(n == 0)
        def _():
            dws_ref[...] = jnp.zeros_like(dws_ref)
            dbs_ref[...] = jnp.zeros_like(dbs_ref)
            dlng_ref[...] = jnp.zeros_like(dlng_ref)
            dlnb_ref[...] = jnp.zeros_like(dlnb_ref)
            dsink_ref[...] = jnp.zeros_like(dsink_ref)

        lane = lax.broadcasted_iota(jnp.int32, (128, 128), 1)
        rowi = lax.broadcasted_iota(jnp.int32, (128, 128), 0)
        lane1 = lax.broadcasted_iota(jnp.int32, (1, 128), 1)
        tri = rowi >= lane
        tri_t = lane >= rowi
        lane_lo = lane < 64
        v = h_ref[:, W:2 * W]
        mu = jnp.mean(v, axis=1, keepdims=True)
        vc = v - mu
        var = jnp.mean(vc * vc, axis=1, keepdims=True)
        rstd = lax.rsqrt(var + LN_EPS)
        vhat = vc * rstd
        vn = vhat * lng_ref[...] + lnb_ref[...]
        dbs_acc = jnp.zeros((128, 128), F32)
        vng = [vn[:, g * 128:(g + 1) * 128].astype(BF) for g in range(8)]
        ms = [_dot(jnp.where(tri, ws_ref[g], 0.0).astype(BF), vng[g]) for g in range(8)]
        dmb = []
        for g in range(8):
            sl = slice(g * 128, (g + 1) * 128)
            m = ms[g] + bsb_ref[g]
            ag = h_ref[:, 2 * W + g * 128:2 * W + (g + 1) * 128]
            sg, dsg = _silu_grad(ag)
            u = h_ref[:, sl]
            da = dm_ref[0, :, sl]
            dmm = da * u * sg
            dh_ref[:, sl] = (da * m * sg).astype(BF)
            dh_ref[:, 2 * W + g * 128:2 * W + (g + 1) * 128] = (da * u * m * dsg).astype(BF)
            dmb.append(dmm.astype(BF))
            dbs_acc = jnp.where(lane == g, jnp.sum(dmm, axis=1, keepdims=True), dbs_acc)
        dvs = [_dot(jnp.where(tri_t, wst_ref[g], 0.0).astype(BF), dmb[g]) for g in range(8)]
        dwss = [_dot_nt(dmb[g], vng[g]) for g in range(8)]
        for g in range(8):
            dvn_ref[:, g * 128:(g + 1) * 128] = dvs[g]
            dws_ref[g] += jnp.where(tri, dwss[g], 0.0)
        dbs_ref[...] += dbs_acc
        dvn = dvn_ref[...]
        dlng_ref[...] += jnp.sum(dvn * vhat, axis=0, keepdims=True)
        dlnb_ref[...] += jnp.sum(dvn, axis=0, keepdims=True)
        dyg = dvn * lng_ref[...]
        m1 = jnp.mean(dyg, axis=1, keepdims=True)
        m2 = jnp.mean(dyg * vhat, axis=1, keepdims=True)
        dh_ref[:, W:2 * W] = (rstd * (dyg - m1 - vhat * m2)).astype(BF)
        kcur = _rope_fwd(h_ref[:, 4096:4224], rc_ref)
        kb = jnp.concatenate([_rope_fwd(hp_ref[:, 0:128], rp_ref), kcur], axis=0)
        vb = jnp.concatenate([hp_ref[:, 128:256], h_ref[:, 4224:4352]], axis=0)
        k2 = _dup_heads(kb)
        v2 = _dup_heads(vb)
        kc2 = _dup_heads(kcur)
        vc2 = _dup_heads(h_ref[:, 4224:4352])
        qi = lax.broadcasted_iota(jnp.int32, (128, 256), 0)
        kj = lax.broadcasted_iota(jnp.int32, (128, 256), 1)
        diff = qi + 128 - kj
        valid = (diff >= 0) & (diff < 128) & ((n > 0) | (kj >= 128))
        validn = (lane > rowi) & (n < nb - 1)
        lc = l_ref[...]
        lnx = ln_ref[...]
        dk = [jnp.zeros((128, 128), F32), jnp.zeros((128, 128), F32)]
        dv = [jnp.zeros((128, 128), F32), jnp.zeros((128, 128), F32)]
        dsk_acc = jnp.zeros((1, 128), F32)
        for j0 in range(0, 8, HEAD_COLS):
            heads = [(j, half) for j in range(j0, j0 + HEAD_COLS) for half in range(2)]
            t = {}
            for j in range(j0, j0 + HEAD_COLS):
                cs = slice(j * 128, (j + 1) * 128)
                qc = _rope_fwd(h_ref[:, 3072 + j * 128:3072 + (j + 1) * 128], rc_ref)
                qn = _rope_fwd(hn_ref[:, 3072 + j * 128:3072 + (j + 1) * 128], rn_ref)
                bg = h_ref[:, 4352 + j * 128:4352 + (j + 1) * 128]
                sgb, dsgb = _silu_grad(bg)
                db = dm_ref[1, :, cs]
                oc = o_ref[:, cs]
                do = db * sgb
                dh_ref[:, 4352 + j * 128:4352 + (j + 1) * 128] = (db * oc * dsgb).astype(BF)
                bgn = hn_ref[:, 4352 + j * 128:4352 + (j + 1) * 128]
                don = dmn_ref[1, :, cs] * (bgn * _sig(bgn))
                prod = do * oc
                prodn = don * on_ref[:, cs]
                for half in range(2):
                    hq = 2 * j + half
                    hm = lane_lo if half == 0 else jnp.logical_not(lane_lo)
                    t[j, half] = dict(
                        dsum=jnp.sum(jnp.where(hm, prod, 0.0), axis=1, keepdims=True),
                        dsumn=jnp.sum(jnp.where(hm, prodn, 0.0), axis=1, keepdims=True),
                        lh=jnp.sum(jnp.where(lane == hq, lc, 0.0), axis=1, keepdims=True),
                        lhn=jnp.sum(jnp.where(lane == hq, lnx, 0.0), axis=1, keepdims=True),
                        qm=jnp.where(hm, qc, 0.0).astype(BF), dom=jnp.where(hm, do, 0.0).astype(BF),
                        qnm=jnp.where(hm, qn, 0.0).astype(BF), donm=jnp.where(hm, don, 0.0).astype(BF))
            for j, half in heads:
                e, hk = t[j, half], j // 4
                e["s"], e["dp"] = _dot_nt(e["qm"], k2[hk]), _dot_nt(e["dom"], v2[hk])
                e["sn"], e["dpn"] = _dot_nt(e["qnm"], kc2[hk]), _dot_nt(e["donm"], vc2[hk])
            for j, half in heads:
                e, hq = t[j, half], 2 * j + half
                p = jnp.exp(jnp.where(valid, e["s"] * 0.125 - e["lh"], NEG))
                ds = p * (e["dp"] - e["dsum"])
                pn = jnp.exp(jnp.where(validn, e["sn"] * 0.125 - e["lhn"], NEG))
                dsn = pn * (e["dpn"] - e["dsumn"])
                psink = jnp.exp(sink_ref[hq] - e["lh"])
                dsk_acc = jnp.where(lane1 == hq, -jnp.sum(psink * e["dsum"], axis=0, keepdims=True), dsk_acc)
                e["ds"] = ds.astype(BF)
                e["pt"], e["dst"] = jnp.transpose(p[:, 128:256]).astype(BF), jnp.transpose(ds[:, 128:256]).astype(BF)
                e["pnt"], e["dsnt"] = jnp.transpose(pn).astype(BF), jnp.transpose(dsn).astype(BF)
            for j, half in heads:
                e, hk = t[j, half], j // 4
                e["dq"] = _dot(e["ds"], k2[hk])
                e["dv"] = _dot(e["pt"], e["dom"]) + _dot(e["pnt"], e["donm"])
                e["dk"] = _dot(e["dst"], e["qm"]) + _dot(e["dsnt"], e["qnm"])
            for j in range(j0, j0 + HEAD_COLS):
                hk = j // 4
                dqcol = jnp.where(lane_lo, t[j, 0]["dq"], t[j, 1]["dq"]) * 0.125
                dh_ref[:, 3072 + j * 128:3072 + (j + 1) * 128] = _rope_bwd(dqcol, rc_ref).astype(BF)
                dv[hk] = dv[hk] + t[j, 0]["dv"] + t[j, 1]["dv"]
                dk[hk] = dk[hk] + (t[j, 0]["dk"] + t[j, 1]["dk"]) * 0.125
        fold = lambda a: a + pltpu.roll(a, 64, 1)
        dh_ref[:, 4096:4224] = _rope_bwd(jnp.where(lane_lo, fold(dk[0]), fold(dk[1])), rc_ref).astype(BF)
        dh_ref[:, 4224:4352] = jnp.where(lane_lo, fold(dv[0]), fold(dv[1])).astype(BF)
        dsink_ref[...] += dsk_acc

    prev = lambda n: jnp.maximum(n - 1, 0)
    nxt = lambda n: jnp.minimum(n + 1, nb - 1)
    full = lambda shape: pl.BlockSpec(shape, lambda n: (0,) * len(shape))
    return _pcall(
        body, grid=(nb,),
        in_specs=[pl.BlockSpec((CHUNK, EVEN_IN), lambda n: (n, 0)),
                  pl.BlockSpec((CHUNK, 256), lambda n: (prev(n), 16)),
                  pl.BlockSpec((CHUNK, EVEN_IN), lambda n: (nxt(n), 0)),
                  pl.BlockSpec((2, CHUNK, W), lambda n: (0, n, 0)),
                  pl.BlockSpec((2, CHUNK, W), lambda n: (0, nxt(n), 0)),
                  pl.BlockSpec((CHUNK, W), lambda n: (n, 0)),
                  pl.BlockSpec((CHUNK, W), lambda n: (nxt(n), 0)),
                  pl.BlockSpec((CHUNK, 128), lambda n: (n, 0)),
                  pl.BlockSpec((CHUNK, 128), lambda n: (nxt(n), 0)),
                  pl.BlockSpec((CHUNK, 384), lambda n: (n, 0)),
                  pl.BlockSpec((CHUNK, 384), lambda n: (prev(n), 0)),
                  pl.BlockSpec((CHUNK, 384), lambda n: (nxt(n), 0)),
                  full((1, W)), full((1, W)), full((8, 128, 128)), full((8, 128, 128)), full((8, 128, 128)),
                  pl.BlockSpec(memory_space=pltpu.SMEM)],
        out_specs=[pl.BlockSpec((CHUNK, EVEN_IN), lambda n: (n, 0)),
                   full((8, 128, 128)), full((128, 128)), full((1, W)), full((1, W)), full((1, 128))],
        out_shape=[jax.ShapeDtypeStruct((S, EVEN_IN), BF), jax.ShapeDtypeStruct((8, 128, 128), F32),
                   jax.ShapeDtypeStruct((128, 128), F32), jax.ShapeDtypeStruct((1, W), F32),
                   jax.ShapeDtypeStruct((1, W), F32), jax.ShapeDtypeStruct((1, 128), F32)],
        scratch=[pltpu.VMEM((CHUNK, W), F32)], name=name, comm=comm,
    )(h, h, h, dmix3, dmix3, o, o, l, l, rope, rope, rope, lng.reshape(1, W), lnb.reshape(1, W), ws, wst, bsb, sinks)


def _expm1(x):
    ser = x * (1.0 + x * (0.5 + x * (1.0 / 6.0 + x * (1.0 / 24.0))))
    return jnp.where(jnp.abs(x) < 1e-2, ser, jnp.exp(x) - 1.0)


def _softplus_neg(lam):
    z = -lam
    e = jnp.exp(-jnp.abs(z))
    l1p = jnp.where(e < 1e-3, e * (1.0 - e * (0.5 - e * (1.0 / 3.0))), jnp.log(1.0 + e))
    return jnp.maximum(z, 0.0) + l1p


def _shift_down(x, k, row, fill=0.0):
    return jnp.where(row >= k, pltpu.roll(x, k, 0), fill)


def _shift_up(x, k, row, fill=0.0):
    S = x.shape[0]
    return jnp.where(row < S - k, pltpu.roll(x, S - k, 0), fill)


def _lru_gates(xc, row, cw_ref, cb_ref, wa_ref, wx_ref, ba_ref, bx_ref, lam_ref):
    xconv = (cw_ref[3:4, :] * xc + cw_ref[2:3, :] * _shift_down(xc, 1, row) + cw_ref[1:2, :] * _shift_down(xc, 2, row)
             + cw_ref[0:1, :] * _shift_down(xc, 3, row) + cb_ref[...])
    xb = xconv.astype(BF)
    r = _sig(_dot(xb, wa_ref[...]) + ba_ref[...])
    i = _sig(_dot(xb, wx_ref[...]) + bx_ref[...])
    sp = _softplus_neg(lam_ref[...])
    log_a = -LRU_C * r * sp
    a = jnp.exp(log_a)
    mult = jnp.sqrt(-_expm1(2.0 * log_a))
    return xconv, r, i, sp, a, mult


ROWS_PER_TILE = 8


def _steps(a, b, shift, inside, products=True):
    n, k = inside.n, 1
    while k < n:
        b = a * jnp.where(inside(k), shift(b, k), 0.0) + b
        if products or 2 * k < n:
            a = a * jnp.where(inside(k), shift(a, k), 1.0)
        k *= 2
    return a, b


class _Inside:
    def __init__(self, pos, n, reverse):
        self.pos, self.n, self.reverse = pos, n, reverse

    def __call__(self, k):
        return self.pos < self.n - k if self.reverse else self.pos >= k


def _scan_rows(a, b, row, a_ref, b_ref, c_ref, reverse=False):
    S = a.shape[0]
    G = S // ROWS_PER_TILE
    if reverse:
        shift = lambda x, k: pltpu.roll(x, x.shape[0] - k, 0)
    else:
        shift = lambda x, k: pltpu.roll(x, k, 0)
    a, b = _steps(a, b, shift, _Inside(row % ROWS_PER_TILE, ROWS_PER_TILE, reverse))
    a_ref[...] = a
    b_ref[...] = b
    last = 0 if reverse else ROWS_PER_TILE - 1
    grow = lax.broadcasted_iota(jnp.int32, (G, a.shape[1]), 0)
    _, tot = _steps(a_ref[pl.ds(last, G, stride=ROWS_PER_TILE), :], b_ref[pl.ds(last, G, stride=ROWS_PER_TILE), :],
                    shift, _Inside(grow, G, reverse), products=False)
    enters = jnp.where(_Inside(grow, G, reverse)(1), shift(tot, 1), 0.0)
    for r in range(ROWS_PER_TILE):
        c_ref[pl.ds(r, G, stride=ROWS_PER_TILE), :] = enters
    return b + a * c_ref[...]


def _odd_c_fwd(h, cw, cb, wa, wx, ba, bx, lam, name, comm=None):
    S = h.shape[0]

    def body(xc_ref, cg_ref, cw_ref, cb_ref, wa_ref, wx_ref, ba_ref, bx_ref, lam_ref, mix_ref, hst_ref, sa_ref, sb_ref, sc_ref):
        row = lax.broadcasted_iota(jnp.int32, (S, 128), 0)
        xconv, r, i, sp, a, mult = _lru_gates(xc_ref[...], row, cw_ref, cb_ref, wa_ref, wx_ref, ba_ref, bx_ref, lam_ref)
        bb = _scan_rows(a, mult * (i * xconv), row, sa_ref, sb_ref, sc_ref)
        hst_ref[...] = bb
        cg = cg_ref[...]
        mix_ref[...] = (bb * (cg * _sig(cg))).astype(BF)

    col = lambda off: pl.BlockSpec((S, 128), lambda j: (0, off + j))
    vec = pl.BlockSpec((1, 128), lambda j: (0, j))
    mat = pl.BlockSpec((None, 128, 128), lambda j: (j, 0, 0))
    return _pcall(
        body, grid=(8,),
        in_specs=[col(0), col(8), pl.BlockSpec((4, 128), lambda j: (0, j)), vec, mat, mat, vec, vec, vec],
        out_specs=[pl.BlockSpec((None, S, 128), lambda j: (0, 0, j)), pl.BlockSpec((S, 128), lambda j: (0, j))],
        out_shape=[jax.ShapeDtypeStruct((2, S, W), BF), jax.ShapeDtypeStruct((S, W), F32)],
        scratch=[pltpu.VMEM((S, 128), F32)] * 3, name=name, comm=comm,
    )(h, h, cw, cb.reshape(1, W), wa, wx, ba.reshape(1, W), bx.reshape(1, W), lam.reshape(1, W))


def _pool_sums(x, g, row, shift):
    s2 = x + shift(x, 1, row)
    s4 = s2 + shift(s2, 2, row)
    s8 = s4 + shift(s4, 4, row)
    s16 = s8 + shift(s8, 8, row)
    return jnp.where(g == 0, s2, jnp.where(g == 1, s4, jnp.where(g == 2, s8, s16)))


def _odd_d_fwd(h, mix3, wp, dscale, name):
    S = h.shape[0]

    def body(xd_ref, dg_ref, wp_ref, ds_ref, mix_in, mix_ref):
        g = pl.program_id(0)
        row = lax.broadcasted_iota(jnp.int32, (S, 256), 0)
        xd = xd_ref[...]
        cnt = jnp.minimum(row + 1, jnp.left_shift(2, g)).astype(F32)
        pooled = _pool_sums(xd, g, row, _shift_down) / cnt - xd
        mixed = _dot(pooled.astype(BF), wp_ref[...])
        dg = dg_ref[...]
        mix_ref[...] = (mixed * ds_ref[...] * (dg * _sig(dg))).astype(BF)

    col = lambda off: pl.BlockSpec((S, 256), lambda g: (0, off + g))
    return pl.pallas_call(
        body, grid=(4,),
        in_specs=[col(8), col(12), pl.BlockSpec((None, 256, 256), lambda g: (g, 0, 0)),
                  pl.BlockSpec((1, 256), lambda g: (0, g)), ANY],
        out_specs=pl.BlockSpec((None, S, 256), lambda g: (1, 0, g)),
        out_shape=jax.ShapeDtypeStruct((2, S, W), BF), input_output_aliases={4: 0},
        name=name, compiler_params=_cp(),
    )(h, h, wp, dscale.reshape(1, W), mix3)


def _odd_c_bwd(h, hst, dmix3, cw, cb, wa, wx, wat, wxt, ba, bx, lam, name, comm=None):
    S = h.shape[0]

    def body(xc_ref, cg_ref, hst_ref, dc_ref, cw_ref, cb_ref, wa_ref, wx_ref, wat_ref, wxt_ref, ba_ref, bx_ref, lam_ref,
             dh_ref, dcw_ref, dcb_ref, dwa_ref, dwx_ref, dba_ref, dbx_ref, dlam_ref, sa_ref, sb_ref, sc_ref):
        row = lax.broadcasted_iota(jnp.int32, (S, 128), 0)
        xc = xc_ref[...]
        xconv, r, i, sp, a, mult = _lru_gates(xc, row, cw_ref, cb_ref, wa_ref, wx_ref, ba_ref, bx_ref, lam_ref)
        hst = hst_ref[...]
        cg = cg_ref[...]
        sg, dsg = _silu_grad(cg)
        dc = dc_ref[...]
        dh_ref[1] = (dc * hst * dsg).astype(BF)
        lam_t = _scan_rows(_shift_up(a, 1, row), dc * sg, row, sa_ref, sb_ref, sc_ref, reverse=True)
        da = lam_t * _shift_down(hst, 1, row)
        ix = i * xconv
        dmult = lam_t * ix
        di = lam_t * mult * xconv
        dxconv = lam_t * mult * i
        dlog_a = da * a - dmult * (a * a / mult)
        dr = dlog_a * (-LRU_C * sp)
        dsp = jnp.sum(dlog_a * (-LRU_C * r), axis=0, keepdims=True)
        dlam_ref[...] = dsp * (-_sig(-lam_ref[...]))
        dpa = dr * r * (1.0 - r)
        dpx = di * i * (1.0 - i)
        dpab = dpa.astype(BF)
        dpxb = dpx.astype(BF)
        xb = xconv.astype(BF)
        dxconv = dxconv + _dot(dpab, wat_ref[...]) + _dot(dpxb, wxt_ref[...])
        dwa_ref[...] = _dot_tn(xb, dpab)
        dwx_ref[...] = _dot_tn(xb, dpxb)
        dba_ref[...] = jnp.sum(dpa, axis=0, keepdims=True)
        dbx_ref[...] = jnp.sum(dpx, axis=0, keepdims=True)
        dh_ref[0] = (cw_ref[3:4, :] * dxconv + cw_ref[2:3, :] * _shift_up(dxconv, 1, row)
                     + cw_ref[1:2, :] * _shift_up(dxconv, 2, row) + cw_ref[0:1, :] * _shift_up(dxconv, 3, row)).astype(BF)
        for j in range(4):
            src = xc if j == 3 else _shift_down(xc, 3 - j, row)
            dcw_ref[j:j + 1, :] = jnp.sum(dxconv * src, axis=0, keepdims=True)
        dcb_ref[...] = jnp.sum(dxconv, axis=0, keepdims=True)

    col = lambda off: pl.BlockSpec((S, 128), lambda j: (0, off + j))
    vec = pl.BlockSpec((1, 128), lambda j: (0, j))
    mat = pl.BlockSpec((None, 128, 128), lambda j: (j, 0, 0))
    vshape = jax.ShapeDtypeStruct((1, W), F32)
    mshape = jax.ShapeDtypeStruct((8, 128, 128), F32)
    return _pcall(
        body, grid=(8,),
        in_specs=[col(0), col(8), col(0), pl.BlockSpec((None, S, 128), lambda j: (0, 0, j)),
                  pl.BlockSpec((4, 128), lambda j: (0, j)), vec, mat, mat, mat, mat, vec, vec, vec],
        out_specs=[pl.BlockSpec((2, S, 128), lambda j: (0, 0, j)), pl.BlockSpec((4, 128), lambda j: (0, j)), vec,
                   mat, mat, vec, vec, vec],
        out_shape=[jax.ShapeDtypeStruct((4, S, W), BF), jax.ShapeDtypeStruct((4, W), F32), vshape, mshape, mshape,
                   vshape, vshape, vshape],
        scratch=[pltpu.VMEM((S, 128), F32)] * 3, name=name, vmem=56, comm=comm,
    )(h, h, hst, dmix3, cw, cb.reshape(1, W), wa, wx, wat, wxt, ba.reshape(1, W), bx.reshape(1, W), lam.reshape(1, W))


def _odd_d_bwd(h, dmix3, dh4, wp, wpt, dscale, name):
    S = h.shape[0]

    def body(xd_ref, dg_ref, dd_ref, wp_ref, wpt_ref, ds_ref, dh_in, dh_ref, dwp_ref, dds_ref):
        g = pl.program_id(0)
        row = lax.broadcasted_iota(jnp.int32, (S, 256), 0)
        xd = xd_ref[...]
        cnt = jnp.minimum(row + 1, jnp.left_shift(2, g)).astype(F32)
        pooled = _pool_sums(xd, g, row, _shift_down) / cnt - xd
        pb = pooled.astype(BF)
        mixed = _dot(pb, wp_ref[...])
        dg = dg_ref[...]
        sg, dsg = _silu_grad(dg)
        dd = dd_ref[...]
        dmixed = dd * ds_ref[...] * sg
        dds_ref[...] = jnp.sum(dd * mixed * sg, axis=0, keepdims=True)
        dh_ref[1] = (dd * mixed * ds_ref[...] * dsg).astype(BF)
        dmb = dmixed.astype(BF)
        dpooled = _dot(dmb, wpt_ref[...])
        dwp_ref[...] = _dot_tn(pb, dmb)
        dh_ref[0] = (_pool_sums(dpooled / cnt, g, row, _shift_up) - dpooled).astype(BF)

    col = lambda off: pl.BlockSpec((S, 256), lambda g: (0, off + g))
    mat = pl.BlockSpec((None, 256, 256), lambda g: (g, 0, 0))
    vec = pl.BlockSpec((1, 256), lambda g: (0, g))
    return pl.pallas_call(
        body, grid=(4,),
        in_specs=[col(8), col(12), pl.BlockSpec((None, S, 256), lambda g: (1, 0, g)), mat, mat, vec, ANY],
        out_specs=[pl.BlockSpec((2, S, 256), lambda g: (1, 0, g)), mat, vec],
        out_shape=[jax.ShapeDtypeStruct((4, S, W), BF), jax.ShapeDtypeStruct((4, 256, 256), F32),
                   jax.ShapeDtypeStruct((1, W), F32)],
        input_output_aliases={6: 0}, name=name, compiler_params=_cp(56),
    )(h, h, dmix3, wp, wpt, dscale.reshape(1, W), dh4)


def _peer(d):
    x, y, c = lax.axis_index("x"), lax.axis_index("y"), lax.axis_index("c")
    px = 1 - x if d & 4 else x
    py = 1 - y if d & 2 else y
    pc = 1 - c if d & 1 else c
    return (px, py, pc), 4 * px + 2 * py + pc


class _GatherAll(_Comm):
    def __init__(self, xs):
        self.peers = EVERYONE
        self.inputs = [xs]
        self.out_shapes = [jax.ShapeDtypeStruct((N_DEV,) + xs.shape, xs.dtype)]
        self.sem_shapes = [pltpu.SemaphoreType.DMA((N_DEV - 1,)), pltpu.SemaphoreType.DMA((N_DEV - 1,)),
                           pltpu.SemaphoreType.DMA]

    def copies(self, ins, outs, sems):
        (x_ref,), (out_ref,), (send, recv, loc) = ins, outs, sems
        _, me = _peer(0)
        res = [pltpu.make_async_copy(x_ref, out_ref.at[me], loc)]
        for d in range(1, N_DEV):
            peer, _ = _peer(d)
            res.append(pltpu.make_async_remote_copy(src_ref=x_ref, dst_ref=out_ref.at[me], send_sem=send.at[d - 1],
                                                    recv_sem=recv.at[d - 1], device_id=peer, device_id_type=MESH))
        return res


class _ExchangeAll(_Comm):
    def __init__(self, g8):
        self.peers = EVERYONE
        self.inputs = [g8]
        self.out_shapes = [jax.ShapeDtypeStruct(g8.shape, g8.dtype)]
        self.sem_shapes = [pltpu.SemaphoreType.DMA((N_DEV - 1,)), pltpu.SemaphoreType.DMA((N_DEV - 1,)),
                           pltpu.SemaphoreType.DMA]

    def copies(self, ins, outs, sems):
        (g_ref,), (out_ref,), (send, recv, loc) = ins, outs, sems
        _, me = _peer(0)
        res = [pltpu.make_async_copy(g_ref.at[me], out_ref.at[0], loc)]
        for d in range(1, N_DEV):
            peer, pidx = _peer(d)
            res.append(pltpu.make_async_remote_copy(src_ref=g_ref.at[pidx], dst_ref=out_ref.at[d], send_sem=send.at[d - 1],
                                                    recv_sem=recv.at[d - 1], device_id=peer, device_id_type=MESH))
        return res


def _sum8(r8, tr, name):
    _, R, C = r8.shape
    tr = min(tr, R)
    assert R % tr == 0

    def body(r_ref, o_ref):
        acc = r_ref[0]
        for d in range(1, N_DEV):
            acc = acc + r_ref[d]
        o_ref[...] = acc

    return pl.pallas_call(
        body, grid=(R // tr,), in_specs=[pl.BlockSpec((N_DEV, tr, C), lambda i: (0, i, 0))],
        out_specs=pl.BlockSpec((tr, C), lambda i: (i, 0)), out_shape=jax.ShapeDtypeStruct((R, C), F32),
        name=name, compiler_params=_cp(),
    )(r8)


def _adamw_math(w, g, m, v):
    m2 = B1 * m + (1.0 - B1) * g
    v2 = B2 * v + (1.0 - B2) * (g * g)
    m_hat = m2 / (1.0 - B1 ** STEP)
    v_hat = v2 / (1.0 - B2 ** STEP)
    return -LR * (m_hat / (jnp.sqrt(v_hat) + ADAM_EPS) + WD * w), m2, v2


def _adamw_many(ws, gs, ms, vs, name):
    n = len(ws)

    def body(*refs):
        for i in range(n):
            d, m2, v2 = _adamw_math(refs[i][...], refs[n + i][...], refs[2 * n + i][...], refs[3 * n + i][...])
            refs[4 * n + i][...] = d
            refs[5 * n + i][...] = m2
            refs[6 * n + i][...] = v2

    vmem = pl.BlockSpec(memory_space=pltpu.VMEM)
    shapes = [jax.ShapeDtypeStruct(w.shape, F32) for w in ws]
    res = pl.pallas_call(body, in_specs=[vmem] * (4 * n), out_specs=[vmem] * (3 * n), out_shape=shapes * 3, name=name,
                         compiler_params=_cp())(*ws, *gs, *ms, *vs)
    return res[:n], res[n:2 * n], res[2 * n:]


def _adamw(w3, gs, m3, v3, tr, name, comm=None):
    _, R, C = w3.shape
    n = 2 if isinstance(gs[0], tuple) else 1

    def gradient(refs):
        if n == 1:
            return refs[0][...]
        s_ref, r_ref = refs
        return ((s_ref[...].astype(F32) + r_ref[0].astype(F32)) + r_ref[1].astype(F32)) + r_ref[2].astype(F32)

    def body(w_ref, *rest):
        g_refs, (m_ref, v_ref, d_ref, m2_ref, v2_ref, g_ref) = rest[:2 * n], rest[2 * n:]
        g = jnp.where(pl.program_id(0) == 0, gradient(g_refs[:n]), gradient(g_refs[n:]))
        d_ref[...], m2_ref[...], v2_ref[...] = _adamw_math(w_ref[...], g, m_ref[...], v_ref[...])
        g_ref[...] = g

    blk = pl.BlockSpec((None, tr, C), lambda j, i: (j, i, 0))

    def grad_specs(layer):
        at = lambda j, i: jnp.where(j == layer, i, 0)
        if n == 1:
            return [pl.BlockSpec((tr, C), lambda j, i: (at(j, i), 0))]
        return [pl.BlockSpec((None, tr, C), lambda j, i: (0, at(j, i), 0)), pl.BlockSpec((3, tr, C), lambda j, i: (0, at(j, i), 0))]

    flat = [a for g in gs for a in (g if n == 2 else (g,))]
    shp = jax.ShapeDtypeStruct((2, R, C), F32)
    return _pcall(body, grid=(2, R // tr), in_specs=[blk] + grad_specs(0) + grad_specs(1) + [blk, blk], out_specs=[blk] * 4,
                  out_shape=[shp] * 4, name=name, comm=comm)(w3, *flat, m3, v3)


def _rep_pack(a):
    n = a.size
    pad = (-n) % 1024
    f = a.reshape(-1)
    if pad:
        f = jnp.concatenate([f, jnp.zeros((pad,), a.dtype)])
    return f.reshape(N_DEV, -1, 128)


def _rep_unpack(p, shape):
    n = 1
    for s in shape:
        n *= s
    return p.reshape(-1)[:n].reshape(shape)


def _sh_pack(a, axis):
    shp = a.shape
    a = a.reshape(shp[:axis] + (N_DEV, shp[axis] // N_DEV) + shp[axis + 1:])
    return jnp.moveaxis(a, axis, 0).reshape(N_DEV, -1, 128)


def _sh_unpack(p, shape, axis):
    a = p.reshape((N_DEV,) + shape[:axis] + (shape[axis] // N_DEV,) + shape[axis + 1:])
    return jnp.moveaxis(a, 0, axis).reshape(shape)


def _pad_rows(a, mult=8):
    pad = (-a.shape[-2]) % mult
    if pad:
        a = jnp.concatenate([a, jnp.zeros(a.shape[:-2] + (pad, a.shape[-1]), a.dtype)], axis=-2)
    return a


REP = ["even_a_ln_g", "even_a_ln_b", "even_a_ws", "even_a_bs", "even_b_sinks", "even_ln_g", "even_ln_b",
       "odd_w_a", "odd_w_x"]
SH = [("odd_conv_w", (2, 4, W), 2), ("odd_conv_b", (2, W), 1), ("odd_b_a", (2, W), 1), ("odd_b_x", (2, W), 1),
      ("odd_lam", (2, W), 1), ("odd_w_pool", (2, 4, 256, 256), 2), ("odd_d_scale", (2, W), 1),
      ("odd_ln_g", (2, D), 1), ("odd_ln_b", (2, D), 1)]
BIG = ["even_w_in", "even_w_out", "odd_w_in", "odd_w_out"]
NAMES = ["even_w_in", "even_a_ln_g", "even_a_ln_b", "even_a_ws", "even_a_bs", "even_b_sinks", "even_w_out",
         "even_ln_g", "even_ln_b", "odd_w_in", "odd_conv_w", "odd_conv_b", "odd_w_a", "odd_b_a", "odd_w_x", "odd_b_x",
         "odd_lam", "odd_w_pool", "odd_d_scale", "odd_w_out", "odd_ln_g", "odd_ln_b"]


def _rope_table(positions):
    inv = ROPE_THETA ** (-jnp.arange(0, 16, 2, dtype=F32) / 16)
    f = jnp.arange(128) % 64
    ang = positions.astype(F32)[:, None] * inv[f % 8][None, :]
    cos, sin = jnp.cos(ang), jnp.sin(ang)
    return jnp.concatenate([jnp.where(f < 16, cos, 1.0), jnp.where(f < 8, -sin, 0.0),
                            jnp.where((f >= 8) & (f < 16), sin, 0.0)], axis=1)


def kernel(x, positions, even_w_in, even_a_ln_g, even_a_ln_b, even_a_ws, even_a_bs, even_b_sinks, even_w_out, even_ln_g, even_ln_b, odd_w_in, odd_conv_w, odd_conv_b, odd_w_a, odd_b_a, odd_w_x, odd_b_x, odd_lam, odd_w_pool, odd_d_scale, odd_w_out, odd_ln_g, odd_ln_b, loss_target, m_even_w_in, m_even_a_ln_g, m_even_a_ln_b, m_even_a_ws, m_even_a_bs, m_even_b_sinks, m_even_w_out, m_even_ln_g, m_even_ln_b, m_odd_w_in, m_odd_conv_w, m_odd_conv_b, m_odd_w_a, m_odd_b_a, m_odd_w_x, m_odd_b_x, m_odd_lam, m_odd_w_pool, m_odd_d_scale, m_odd_w_out, m_odd_ln_g, m_odd_ln_b, v_even_w_in, v_even_a_ln_g, v_even_a_ln_b, v_even_a_ws, v_even_a_bs, v_even_b_sinks, v_even_w_out, v_even_ln_g, v_even_ln_b, v_odd_w_in, v_odd_conv_w, v_odd_conv_b, v_odd_w_a, v_odd_b_a, v_odd_w_x, v_odd_b_x, v_odd_lam, v_odd_w_pool, v_odd_d_scale, v_odd_w_out, v_odd_ln_g, v_odd_ln_b):
    args = (even_w_in, even_a_ln_g, even_a_ln_b, even_a_ws, even_a_bs, even_b_sinks, even_w_out, even_ln_g, even_ln_b,
            odd_w_in, odd_conv_w, odd_conv_b, odd_w_a, odd_b_a, odd_w_x, odd_b_x, odd_lam, odd_w_pool, odd_d_scale,
            odd_w_out, odd_ln_g, odd_ln_b)
    margs = (m_even_w_in, m_even_a_ln_g, m_even_a_ln_b, m_even_a_ws, m_even_a_bs, m_even_b_sinks, m_even_w_out,
             m_even_ln_g, m_even_ln_b, m_odd_w_in, m_odd_conv_w, m_odd_conv_b, m_odd_w_a, m_odd_b_a, m_odd_w_x,
             m_odd_b_x, m_odd_lam, m_odd_w_pool, m_odd_d_scale, m_odd_w_out, m_odd_ln_g, m_odd_ln_b)
    vargs = (v_even_w_in, v_even_a_ln_g, v_even_a_ln_b, v_even_a_ws, v_even_a_bs, v_even_b_sinks, v_even_w_out,
             v_even_ln_g, v_even_ln_b, v_odd_w_in, v_odd_conv_w, v_odd_conv_b, v_odd_w_a, v_odd_b_a, v_odd_w_x,
             v_odd_b_x, v_odd_lam, v_odd_w_pool, v_odd_d_scale, v_odd_w_out, v_odd_ln_g, v_odd_ln_b)
    wts = dict(zip(NAMES, args))
    mom = dict(zip(NAMES, margs))
    var = dict(zip(NAMES, vargs))
    S = x.shape[1]
    x0 = x[0]
    rope = _rope_table(positions[0])

    kinds = ("even", "odd", "even", "odd")
    blk_in = [jnp.transpose(wts[kinds[l] + "_w_in"][l // 2]).astype(BF) for l in range(4)]
    blk_out = [wts[kinds[l] + "_w_out"][l // 2].astype(BF) for l in range(4)]
    sh_local = _pad_rows(jnp.concatenate([wts[nm].reshape(-1, 128) for nm, _, _ in SH], axis=0), 16)
    me = 4 * lax.axis_index("x") + 2 * lax.axis_index("y") + lax.axis_index("c")
    own_slot = lambda blk: lax.dynamic_update_slice(lax.empty((N_DEV,) + blk.shape, blk.dtype), blk[None], (me, 0, 0))
    reg = {"blk_small": sh_local, "w_small": own_slot(sh_local)}
    sched = _Sched(reg)
    for l in range(4):
        reg[f"blk_in{l}"], reg[f"blk_out{l}"] = blk_in[l], blk_out[l]
        reg[f"w_in{l}"], reg[f"w_out{l}"] = own_slot(blk_in[l]), own_slot(blk_out[l])
    sched.add(_rows("blk_in0", "w_in0", "ag1", blk_in[0].shape[0], ROW_CHUNK[blk_in[0].shape[0]]))
    sched.add(_rows("blk_small", "w_small", "ag1", sh_local.shape[0], sh_local.shape[0]))
    for l in range(4):
        sched.add(_rows(f"blk_out{l}", f"w_out{l}", "ag1", D // N_DEV, ROW_CHUNK[D // N_DEV]))
        if l < 3:
            r = blk_in[l + 1].shape[0]
            sched.add(_rows(f"blk_in{l + 1}", f"w_in{l + 1}", "ag1", r, ROW_CHUNK[r]))

    def gathered(dst, blk):
        sched.flush(dst, FLUSH_EXTRA_US)
        return reg.pop(dst)

    wt_in0 = gathered("w_in0", blk_in[0]).reshape(-1, D)
    full = {nm: wts[nm] for nm in REP}

    def gather_small():
        sh_all = gathered("w_small", sh_local)
        off = 0
        for nm, shape, axis in SH:
            r = wts[nm].size // 128
            full[nm] = _sh_unpack(sh_all[:, off:off + r, :], shape, axis)
            off += r

    saved = []
    wt_in, w_out = [wt_in0, None, None, None], [None] * 4
    xf, xb = x0, x0.astype(BF)
    fwd = lambda name: FWD_OVERBOOK * CARRY_US[name]
    for layer in range(4):
        j = layer // 2
        kind = kinds[layer]
        if wt_in[layer] is None:
            wt_in[layer] = gathered(f"w_in{layer}", blk_in[layer]).reshape(-1, D)
        h = sched.run(_mm_nt, fwd("mm_h_" + kind), xb, wt_in[layer], 1024, 768 if kind == "even" else 512, "mm_h_" + kind)
        if kind == "even":
            bsb = jnp.broadcast_to(full["even_a_bs"][j][:, :, None], (8, 128, 128))
            mix3, o, l = sched.run(_even_fwd, fwd("even_fwd"), h, rope, full["even_a_ln_g"][j], full["even_a_ln_b"][j],
                                   full["even_a_ws"][j], bsb, full["even_b_sinks"][j], "even_fwd")
            extra = (o, l, bsb)
        else:
            if "odd_lam" not in full:
                gather_small()
            wa, wx = full["odd_w_a"][j].astype(BF), full["odd_w_x"][j].astype(BF)
            wp = full["odd_w_pool"][j].astype(BF)
            mix3, hst = sched.run(_odd_c_fwd, fwd("odd_c_fwd"), h, full["odd_conv_w"][j], full["odd_conv_b"][j], wa, wx,
                                  full["odd_b_a"][j], full["odd_b_x"][j], full["odd_lam"][j], "odd_c_fwd")
            mix3 = _odd_d_fwd(h, mix3, wp, full["odd_d_scale"][j], "odd_d_fwd")
            extra = (hst, wa, wx, wp)
        w_out[layer] = gathered(f"w_out{layer}", blk_out[layer]).reshape(D, D)
        z, xn, xnb = sched.run(_mm_out_ln, fwd("mm_out_ln"), mix3, w_out[layer], xf, full[kind + "_ln_g"][j],
                               full[kind + "_ln_b"][j], "mm_out_ln")
        saved.append((xb, h, mix3, z, extra))
        xf, xb = xn, xnb

    dxn = xf

    gsum = {nm: [None, None] for nm in NAMES}

    chip_sums = {}
    sched.overhang = 0.15

    waiting = []

    def chip_sum(g, tag, key):
        r = g.shape[0] // N_DEV
        reg["g_" + key] = g.reshape(N_DEV, r, D)
        sched.add(_rows("g_" + key, "d_" + key, "rsd", r, r), first=True)
        waiting.append((key, tag))

    def add_arrived():
        for key, tag in list(waiting):
            if "d_" + key in reg and not sched.pending("d_" + key):
                waiting.remove((key, tag))
                g8 = reg.pop("g_" + key)
                chip_sums[key] = reg["s_" + key] = _add_pairs(g8, reg.pop("d_" + key), "rs_add_" + tag)
                sched.add(_rows("s_" + key, "r_" + key, "rs", g8.shape[1], ROW_CHUNK[g8.shape[1]] // 2))

    sched.after_landing = add_arrived

    def reduced(key):
        sched.flush("d_" + key, FLUSH_EXTRA_US)
        sched.flush("r_" + key, FLUSH_EXTRA_US)
        return chip_sums[key], reg.pop("r_" + key)

    for layer in (3, 2, 1, 0):
        j = layer // 2
        xb, h, mix3, z, extra = saved[layer]
        kind = kinds[layer]
        if layer == 3:
            dz, dzb, dg, dbeta, part = sched.run(_ln_bwd, CARRY_US["ln_bwd"], dxn, z, full[kind + "_ln_g"][j], "loss_ln_bwd",
                                                 target=loss_target[0])
        else:
            dz, dzb, dg, dbeta = sched.run(_ln_bwd, CARRY_US["ln_bwd"], dxn, z, full[kind + "_ln_g"][j], "ln_bwd")
        gsum[kind + "_ln_g"][j] = dg.reshape(D)
        gsum[kind + "_ln_b"][j] = dbeta.reshape(D)
        chip_sum(sched.run(_mm_tn, CARRY_US["mm_dw_out"], mix3, dzb, 512, "mm_dw_out"), "w_out", f"out{layer}")
        dmix3 = sched.run(_mm_nt, CARRY_US["mm_dmix"], dzb, w_out[layer], 1024, 512, "mm_dmix", out3=True)
        if kind == "even":
            o, l, bsb = extra
            ws = full["even_a_ws"][j]
            dh, dws, dbs, dlng, dlnb, dsink = sched.run(
                _even_bwd, CARRY_US["even_bwd"], h, dmix3, o, l, rope, full["even_a_ln_g"][j], full["even_a_ln_b"][j],
                ws, jnp.swapaxes(ws, 1, 2), bsb, full["even_b_sinks"][j], "even_bwd")
            gsum["even_a_ws"][j] = dws
            gsum["even_a_bs"][j] = jnp.transpose(dbs[:, :8])
            gsum["even_a_ln_g"][j] = dlng.reshape(W)
            gsum["even_a_ln_b"][j] = dlnb.reshape(W)
            gsum["even_b_sinks"][j] = dsink[0, :16]
            if layer == 0:
                rep_rows = [_rep_pack(jnp.stack(gsum[nm]).reshape(wts[nm].shape)) for nm in REP]
                sh_rows = [_sh_pack(jnp.stack(gsum[nm]).reshape(shape), axis) for nm, shape, axis in SH]
                packed = _pad_rows(jnp.concatenate(rep_rows + sh_rows, axis=1))
                gw, (small8, parts) = _mm_tn(dh, xb, 384, "mm_dw_in_even", comm=_Join([_ExchangeAll(packed), _GatherAll(part)]))
                loss = jnp.sum(parts[:, 0, 0]) * (0.5 / D)
            else:
                gw = sched.run(_mm_tn, CARRY_US["mm_dw_in_even"], dh, xb, 384, "mm_dw_in_even")
            chip_sum(gw, "w_in_even", f"in{layer}")
            if layer == 0:
                n_rep = sum(p.shape[1] for p in rep_rows)
                red = _sum8(small8, 1 << 20, "sum_small")
                (rep_all,) = sched.flush("d_in0", FLUSH_EXTRA_US, beside=_GatherAll(_pad_rows(red[:n_rep])))
                sched.overhang = 0.6
            dxn = sched.run(_mm_nn_res, CARRY_US["mm_dx_even"], dh, wt_in[layer], dz, 512, 512, "mm_dx_even")
        else:
            hst, wa, wx, wp = extra
            dh4, dcw, dcb, dwa, dwx, dba, dbx, dlam = sched.run(
                _odd_c_bwd, CARRY_US["odd_c_bwd"], h, hst, dmix3, full["odd_conv_w"][j], full["odd_conv_b"][j], wa, wx,
                jnp.swapaxes(wa, 1, 2), jnp.swapaxes(wx, 1, 2), full["odd_b_a"][j], full["odd_b_x"][j], full["odd_lam"][j],
                "odd_c_bwd")
            dh4, dwp, dds = _odd_d_bwd(h, dmix3, dh4, wp, jnp.swapaxes(wp, 1, 2), full["odd_d_scale"][j], "odd_d_bwd")
            gsum["odd_conv_w"][j], gsum["odd_conv_b"][j] = dcw, dcb.reshape(W)
            gsum["odd_w_a"][j], gsum["odd_w_x"][j] = dwa, dwx
            gsum["odd_b_a"][j], gsum["odd_b_x"][j], gsum["odd_lam"][j] = dba.reshape(W), dbx.reshape(W), dlam.reshape(W)
            gsum["odd_w_pool"][j], gsum["odd_d_scale"][j] = dwp, dds.reshape(W)
            chip_sum(sched.run(_mm_tn, CARRY_US["mm_dw_in_odd"], dh4, xb, 512, "mm_dw_in_odd"), "w_in_odd", f"in{layer}")
            dxn = sched.run(_mm_nn_res, CARRY_US["mm_dx_odd"], dh4, wt_in[layer], dz, 512, 512, "mm_dx_odd")
    grad_x = dxn[None]

    out_g, out_d, out_m, out_v = {}, {}, {}, {}
    for nm, kind, what, layers in (("odd_w_out", "odd", "out", (1, 3)), ("even_w_out", "even", "out", (0, 2)),
                                   ("odd_w_in", "odd", "in", (1, 3)), ("even_w_in", "even", "in", (0, 2))):
        gl = [reduced(f"{what}{l}") for l in layers]
        if nm == "even_w_in":
            view = lambda a: jnp.transpose(a, (0, 2, 1))
            res, _ = _adamw(view(wts[nm]), gl, view(mom[nm]), view(var[nm]), 112, f"adamw_{nm}")
            res = [view(a) for a in res]
        elif what == "in":
            gs = [jnp.transpose(_rs_final(s4, r3, "rs_final_w_in_odd")) for s4, r3 in gl]
            res, _ = _adamw(wts[nm], gs, mom[nm], var[nm], 512, f"adamw_{nm}")
        else:
            res = sched.run(_adamw, CARRY_US["adamw_" + nm], wts[nm], gl, mom[nm], var[nm], 128, f"adamw_{nm}")
        out_d[nm], out_m[nm], out_v[nm], out_g[nm] = res

    g_small = {}
    off = 0
    for nm, p in zip(REP, rep_rows):
        r = p.shape[1]
        g_small[nm] = _rep_unpack(rep_all[:, off:off + r, :], wts[nm].shape)
        off += r
    off = n_rep
    for (nm, shape, axis), p in zip(SH, sh_rows):
        r = p.shape[1]
        g_small[nm] = red[off:off + r].reshape(wts[nm].shape)
        off += r

    def rows(a):
        f = a.reshape(-1)
        pad = (-f.shape[0]) % 128
        if pad:
            f = jnp.concatenate([f, jnp.zeros((pad,), a.dtype)])
        return f.reshape(-1, 128)

    small = REP + [nm for nm, _, _ in SH]
    each = lambda src: [rows(src[nm]) for nm in small]
    d2, m2, v2 = _adamw_many(each(wts), each(g_small), each(mom), each(var), "adamw_small")
    for i, nm in enumerate(small):
        n, shp = wts[nm].size, wts[nm].shape
        take = lambda a: a.reshape(-1)[:n].reshape(shp)
        out_g[nm], out_d[nm], out_m[nm], out_v[nm] = g_small[nm], take(d2[i]), take(m2[i]), take(v2[i])

    return (loss, grad_x, *[out_g[nm] for nm in NAMES], *[out_d[nm] for nm in NAMES],
            *[out_m[nm] for nm in NAMES], *[out_v[nm] for nm in NAMES])
```

```python
import functools

import jax
import jax.numpy as jnp
from jax import lax
from jax.experimental import pallas as pl
from jax.experimental.pallas import tpu as pltpu

F32 = jnp.float32
BF = jnp.bfloat16
MESH = pl.DeviceIdType.MESH
ANY = pl.BlockSpec(memory_space=pl.ANY)

N_DEV = 8
D = 2048
W = 1024
EVEN_IN = 5376
ODD_IN = 4096
CHUNK = 128
ALPHA = (2 * 4) ** 0.25
LN_EPS = 1e-5
ROPE_THETA = 500000.0
LRU_C = 8.0
LR, B1, B2, ADAM_EPS, WD, STEP = 0.001, 0.9, 0.999, 1e-08, 0.01, 10
NEG = -1e30
HEAD_COLS = 4


def _cp(vmem_mb=48, collective_id=None):
    return pltpu.CompilerParams(vmem_limit_bytes=vmem_mb * 1024 * 1024, collective_id=collective_id)


def _sig(x):
    return jax.nn.sigmoid(x)


def _silu_grad(x):
    s = _sig(x)
    return x * s, s * (1.0 + x * (1.0 - s))


def _dot(a, b):
    return jnp.dot(a, b, preferred_element_type=F32)


def _dot_nt(a, b):
    return lax.dot_general(a, b, (((1,), (1,)), ((), ())), preferred_element_type=F32)


def _dot_tn(a, b):
    return lax.dot_general(a, b, (((0,), (0,)), ((), ())), preferred_element_type=F32)


def _coords():
    return lax.axis_index("x"), lax.axis_index("y"), lax.axis_index("c")


def _chip(j):
    x, y, _ = _coords()
    return (1 - x if j & 2 else x), (1 - y if j & 1 else y)


X_NB, Y_NB, DIAG, SIB = 4, 2, 6, 1
EVERYONE = frozenset(range(1, N_DEV))
BARRIER_IDS = {}


class _Comm:
    def collective_id(self):
        return BARRIER_IDS.setdefault(frozenset(self.peers), len(BARRIER_IDS))

    def start(self, ins, outs, sems):
        barrier = pltpu.get_barrier_semaphore()
        for d in sorted(self.peers):
            pl.semaphore_signal(barrier, inc=1, device_id=_peer(d)[0], device_id_type=MESH)
        pl.semaphore_wait(barrier, len(self.peers))
        for cp in self.copies(ins, outs, sems):
            cp.start()

    def wait(self, ins, outs, sems):
        for cp in self.copies(ins, outs, sems):
            cp.wait()


class _Join(_Comm):
    def __init__(self, parts):
        self.parts = list(parts)
        self.peers = frozenset().union(*[p.peers for p in self.parts])
        self.inputs = [a for p in self.parts for a in p.inputs]
        self.out_shapes = [s for p in self.parts for s in p.out_shapes]
        self.sem_shapes = [s for p in self.parts for s in p.sem_shapes]
        self.aliases = {}
        i0 = o0 = 0
        for p in self.parts:
            for i, o in getattr(p, "aliases", {}).items():
                self.aliases[i0 + i] = o0 + o
            i0, o0 = i0 + len(p.inputs), o0 + len(p.out_shapes)

    def copies(self, ins, outs, sems):
        res = []
        i0 = o0 = s0 = 0
        for p in self.parts:
            ni, no, ns = len(p.inputs), len(p.out_shapes), len(p.sem_shapes)
            res += p.copies(ins[i0:i0 + ni], outs[o0:o0 + no], sems[s0:s0 + ns])
            i0, o0, s0 = i0 + ni, o0 + no, s0 + ns
        return res


ROWS_US = {"ag1": 0.104, "ag2": 0.052, "agd": 0.027, "rsd": 0.027, "rs": 0.205}
N_COPIES = {"ag1": 2, "ag2": 2, "agd": 4, "rsd": 4, "rs": 3}
TASK_PEERS = {"ag1": {X_NB, Y_NB}, "ag2": {X_NB, Y_NB}, "agd": {SIB}, "rsd": {SIB}, "rs": {X_NB, Y_NB, DIAG}}
ROW_CHUNK = {672: 224, 512: 128, 256: 128}
CARRY_US = {"mm_h_even": 58, "mm_h_odd": 47, "even_fwd": 42, "odd_c_fwd": 37, "mm_out_ln": 33, "ln_bwd": 23, "mm_dmix": 26,
            "mm_dw_out": 25, "even_bwd": 90, "odd_c_bwd": 58, "mm_dw_in_even": 58, "mm_dw_in_odd": 44, "mm_dx_even": 66,
            "mm_dx_odd": 55, "adamw_even_w_out": 11, "adamw_odd_w_out": 11}
FWD_OVERBOOK = 1.15
FLUSH_EXTRA_US = 60.0


def _cost_us(task, reg):
    kind, src, _, lo, hi = task
    return ROWS_US[kind] * (hi - lo) * reg[src].shape[-1] * reg[src].dtype.itemsize / 4096.0


class _Copies(_Comm):
    def __init__(self, tasks, reg):
        self.tasks = list(tasks)
        self.out_names, self.in_names = [], []
        for kind, src, dst, lo, hi in self.tasks:
            if dst not in self.out_names:
                self.out_names.append(dst)
        for kind, src, dst, lo, hi in self.tasks:
            if src not in self.out_names and src not in self.in_names:
                self.in_names.append(src)
        self.out_shapes, self.aliases = [], {}
        for o, dst in enumerate(self.out_names):
            if dst in reg:
                self.aliases[len(self.in_names)] = o
                self.in_names.append(dst)
                self.out_shapes.append(jax.ShapeDtypeStruct(reg[dst].shape, reg[dst].dtype))
            else:
                kind, src = next((t[0], t[1]) for t in self.tasks if t[2] == dst)
                shape = ({"rsd": 4, "rs": 3}[kind],) + reg[src].shape[1:]
                self.out_shapes.append(jax.ShapeDtypeStruct(shape, reg[src].dtype))
        self.inputs = [reg[nm] for nm in self.in_names]
        n = sum(N_COPIES[t[0]] for t in self.tasks)
        self.sem_shapes = [pltpu.SemaphoreType.DMA((n,)), pltpu.SemaphoreType.DMA((n,))]
        self.peers = frozenset().union(*[TASK_PEERS[t[0]] for t in self.tasks])

    def copies(self, ins, outs, sems):
        send, recv = sems
        x, y, c = _coords()
        me = 4 * x + 2 * y + c
        xn, yn = (1 - x, y, c), (x, 1 - y, c)
        at_xn, at_yn = 4 * (1 - x) + 2 * y + c, 4 * x + 2 * (1 - y) + c
        ref = dict(zip(self.in_names, ins))
        ref.update(zip(self.out_names, outs))
        res = []

        def copy(src, dst, to):
            i = len(res)
            res.append(pltpu.make_async_remote_copy(src_ref=src, dst_ref=dst, send_sem=send.at[i], recv_sem=recv.at[i],
                                                    device_id=to, device_id_type=MESH))

        for kind, src, dst, lo, hi in self.tasks:
            n = hi - lo
            if kind == "ag1":
                for to in (xn, yn):
                    copy(ref[src].at[pl.ds(lo, n)], ref[dst].at[me, pl.ds(lo, n)], to)
            elif kind == "ag2":
                h = n // 2
                first, second = ref[dst].at[at_xn, pl.ds(lo, h)], ref[dst].at[at_yn, pl.ds(lo + h, n - h)]
                copy(first, first, yn)
                copy(second, second, xn)
            elif kind == "agd":
                for j in range(4):
                    px, py = _chip(j)
                    rows = ref[dst].at[4 * px + 2 * py + c, pl.ds(lo, n)]
                    copy(rows, rows, (x, y, 1 - c))
            elif kind == "rsd":
                for j in range(4):
                    px, py = _chip(j)
                    copy(ref[src].at[4 * px + 2 * py + 1 - c, pl.ds(lo, n)], ref[dst].at[j, pl.ds(lo, n)], (x, y, 1 - c))
            else:
                for j in (1, 2, 3):
                    px, py = _chip(j)
                    copy(ref[src].at[j, pl.ds(lo, n)], ref[dst].at[j - 1, pl.ds(lo, n)], (px, py, c))
        return res


class _Sched:
    def __init__(self, reg):
        self.reg, self.queue, self.later = reg, [], []
        self.overhang = 0.5
        self.after_landing = None

    def add(self, tasks, first=False):
        self.queue = list(tasks) + self.queue if first else self.queue + list(tasks)

    def pending(self, dst):
        return any(t[2] == dst for t in self.queue + self.later)

    def take(self, budget_us, must=None, overhang=0.5):
        self.queue, self.later = self.later + self.queue, []
        picked, us = [], 0.0
        rest = []
        for t in self.queue:
            cost = _cost_us(t, self.reg)
            if (must is not None and t[2] == must) or us + (1.0 - overhang) * cost <= budget_us:
                picked.append(t)
                us += cost
                if t[0] in ("ag1", "ag2"):
                    self.later.append(({"ag1": "ag2", "ag2": "agd"}[t[0]], t[2], t[2], t[3], t[4]))
            else:
                rest.append(t)
        self.queue = rest
        return _Copies(picked, self.reg) if picked else None

    def landed(self, comm, got):
        if comm is not None:
            for nm, a in zip(comm.out_names, got):
                self.reg[nm] = a
        if self.after_landing is not None:
            self.after_landing()

    def run(self, builder, budget_us, *args, **kw):
        comm = self.take(budget_us, overhang=self.overhang)
        res, got = builder(*args, comm=comm, **kw)
        self.landed(comm, got)
        return res

    def flush(self, dst, budget_us=0.0, beside=None):
        res = []
        while self.pending(dst):
            comm = self.take(budget_us, must=dst)
            got = _comm_only(comm if beside is None else _Join([comm, beside]), "flush_" + dst)
            res, beside = got[len(comm.out_shapes):], None
            self.landed(comm, got[:len(comm.out_shapes)])
        return res


def _rows(name_src, name_dst, kind, n_rows, chunk):
    return [(kind, name_src, name_dst, lo, min(lo + chunk, n_rows)) for lo in range(0, n_rows, chunk)]


def _pcall(body, *, grid, in_specs, out_specs, out_shape, name, scratch=(), vmem=48, comm=None):
    in_specs, out_specs, out_shape, scratch = list(in_specs), list(out_specs), list(out_shape), list(scratch)
    if comm is None:
        call = pl.pallas_call(body, grid=grid, in_specs=in_specs, out_specs=out_specs, out_shape=out_shape,
                              scratch_shapes=scratch, name=name, compiler_params=_cp(vmem))
        return lambda *args: (call(*args), [])
    n_in, n_out, n_scr = len(in_specs), len(out_specs), len(scratch)
    c_in, c_out = len(comm.inputs), len(comm.out_shapes)
    aliases = {n_in + i: n_out + o for i, o in getattr(comm, "aliases", {}).items()}

    def wrapped(*refs):
        ins, cins = refs[:n_in], refs[n_in:n_in + c_in]
        o0 = n_in + c_in
        outs, couts = refs[o0:o0 + n_out], refs[o0 + n_out:o0 + n_out + c_out]
        s0 = o0 + n_out + c_out
        scr, sems = refs[s0:s0 + n_scr], refs[s0 + n_scr:]
        ids = [pl.program_id(a) for a in range(len(grid))]
        first = functools.reduce(jnp.logical_and, [i == 0 for i in ids])
        last = functools.reduce(jnp.logical_and, [i == g - 1 for i, g in zip(ids, grid)])

        @pl.when(first)
        def _():
            comm.start(cins, couts, sems)

        body(*ins, *outs, *scr)

        @pl.when(last)
        def _():
            comm.wait(cins, couts, sems)

    call = pl.pallas_call(wrapped, grid=grid, in_specs=in_specs + [ANY] * c_in, out_specs=out_specs + [ANY] * c_out,
                          out_shape=out_shape + list(comm.out_shapes), scratch_shapes=scratch + list(comm.sem_shapes),
                          input_output_aliases=aliases, name=name, compiler_params=_cp(vmem, comm.collective_id()))

    def run(*args):
        res = call(*args, *comm.inputs)
        return res[:n_out], res[n_out:]

    return run


def _comm_only(comm, name):
    c_in, c_out = len(comm.inputs), len(comm.out_shapes)

    def body(*refs):
        cins, couts, sems = refs[:c_in], refs[c_in:c_in + c_out], refs[c_in + c_out:]
        comm.start(cins, couts, sems)
        comm.wait(cins, couts, sems)

    return pl.pallas_call(body, in_specs=[ANY] * c_in, out_specs=[ANY] * c_out, out_shape=list(comm.out_shapes),
                          scratch_shapes=list(comm.sem_shapes), input_output_aliases=dict(getattr(comm, "aliases", {})),
                          name=name, compiler_params=pltpu.CompilerParams(collective_id=comm.collective_id()))(*comm.inputs)


def _chip_blocks():
    _, _, c = _coords()
    return jnp.stack([4 * px + 2 * py + c for px, py in map(_chip, range(4))]).astype(jnp.int32)


def _add_pairs(g8, b4, name):
    _, R, C = b4.shape

    def body(idx_ref, a_ref, b_ref, o_ref):
        o_ref[...] = (a_ref[...].astype(F32) + b_ref[...].astype(F32)).astype(BF)

    blk = pl.BlockSpec((None, R, C), lambda j, idx: (j, 0, 0))
    grid_spec = pltpu.PrefetchScalarGridSpec(
        num_scalar_prefetch=1, grid=(4,),
        in_specs=[pl.BlockSpec((None, R, C), lambda j, idx: (idx[j], 0, 0)), blk], out_specs=blk)
    return pl.pallas_call(body, grid_spec=grid_spec, out_shape=jax.ShapeDtypeStruct(b4.shape, BF), name=name,
                          compiler_params=_cp())(_chip_blocks(), g8, b4)


def _rs_final(s4, r3, name):
    _, R, C = s4.shape
    tr = R // 2

    def body(s_ref, r_ref, o_ref):
        o_ref[...] = ((s_ref[...].astype(F32) + r_ref[0].astype(F32)) + r_ref[1].astype(F32)) + r_ref[2].astype(F32)

    return pl.pallas_call(
        body, grid=(2,),
        in_specs=[pl.BlockSpec((None, tr, C), lambda i: (0, i, 0)), pl.BlockSpec((3, tr, C), lambda i: (0, i, 0))],
        out_specs=pl.BlockSpec((tr, C), lambda i: (i, 0)), out_shape=jax.ShapeDtypeStruct((R, C), F32),
        name=name, compiler_params=_cp())(s4, r3)


def _mm_nt(a, w, tm, tn, name, out3=False, comm=None):
    M, K = a.shape
    N = w.shape[0]
    tm = min(tm, M)

    def body(a_ref, w_ref, o_ref):
        o_ref[...] = _dot_nt(a_ref[...], w_ref[...])

    if out3:
        per = W // tn
        out_shape = jax.ShapeDtypeStruct((N // W, M, W), F32)
        out_spec = pl.BlockSpec((None, tm, tn), lambda i, j: (j // per, i, j % per))
    else:
        out_shape = jax.ShapeDtypeStruct((M, N), F32)
        out_spec = pl.BlockSpec((tm, tn), lambda i, j: (i, j))
    (res,), extra = _pcall(
        body, grid=(M // tm, N // tn),
        in_specs=[pl.BlockSpec((tm, K), lambda i, j: (i, 0)), pl.BlockSpec((tn, K), lambda i, j: (j, 0))],
        out_specs=[out_spec], out_shape=[out_shape], name=name, comm=comm)(a, w)
    return res, extra


def _mm_tn(a, b, tm, name, comm=None):
    K, N = b.shape
    if a.ndim == 3:
        M = a.shape[0] * W
        per = W // tm
        a_spec = pl.BlockSpec((None, K, tm), lambda i: (i // per, 0, i % per))
    else:
        M = a.shape[1]
        a_spec = pl.BlockSpec((K, tm), lambda i: (0, i))

    def body(a_ref, b_ref, o_ref):
        o_ref[...] = _dot_tn(a_ref[...], b_ref[...]).astype(BF)

    (out,), extra = _pcall(
        body, grid=(M // tm,),
        in_specs=[a_spec, pl.BlockSpec((K, N), lambda i: (0, 0))],
        out_specs=[pl.BlockSpec((tm, N), lambda i: (i, 0))],
        out_shape=[jax.ShapeDtypeStruct((M, N), BF)], name=name, vmem=56, comm=comm)(a, b)
    return out, extra


def _mm_nn_res(a, w, res, tm, tn, name, comm=None):
    K, N = w.shape
    if a.ndim == 3:
        P, M = a.shape[0], a.shape[1]
        tm = min(tm, M)
        a_spec = pl.BlockSpec((P, tm, W), lambda j, i: (0, i, 0))
    else:
        P, M = 0, a.shape[0]
        tm = min(tm, M)
        a_spec = pl.BlockSpec((tm, K), lambda j, i: (i, 0))

    def body(a_ref, w_ref, r_ref, o_ref):
        if P:
            d = _dot(a_ref[0], w_ref[0:W, :])
            for p in range(1, P):
                d = d + _dot(a_ref[p], w_ref[p * W:(p + 1) * W, :])
        else:
            d = _dot(a_ref[...], w_ref[...])
        o_ref[...] = ALPHA * r_ref[...] + d

    (out,), extra = _pcall(
        body, grid=(N // tn, M // tm),
        in_specs=[a_spec, pl.BlockSpec((K, tn), lambda j, i: (0, j)), pl.BlockSpec((tm, tn), lambda j, i: (i, j))],
        out_specs=[pl.BlockSpec((tm, tn), lambda j, i: (i, j))],
        out_shape=[jax.ShapeDtypeStruct((M, N), F32)], name=name, comm=comm)(a, w, res)
    return out, extra


def _mm_out_ln(mix3, w_out, x, g, b, name, comm=None):
    S = x.shape[0]
    tm = min(256, S)

    def body(m_ref, w_ref, x_ref, g_ref, b_ref, z_ref, xn_ref, xb_ref):
        acc = _dot(m_ref[0], w_ref[0:W, :]) + _dot(m_ref[1], w_ref[W:2 * W, :])
        z = ALPHA * x_ref[...] + acc
        mu = jnp.mean(z, axis=1, keepdims=True)
        zc = z - mu
        var = jnp.mean(zc * zc, axis=1, keepdims=True)
        xn = zc * lax.rsqrt(var + LN_EPS) * g_ref[...] + b_ref[...]
        z_ref[...] = z
        xn_ref[...] = xn
        xb_ref[...] = xn.astype(BF)

    row = pl.BlockSpec((tm, D), lambda i: (i, 0))
    vec = pl.BlockSpec((1, D), lambda i: (0, 0))
    return _pcall(
        body, grid=(S // tm,),
        in_specs=[pl.BlockSpec((2, tm, W), lambda i: (0, i, 0)), pl.BlockSpec((D, D), lambda i: (0, 0)), row, vec, vec],
        out_specs=[row, row, row],
        out_shape=[jax.ShapeDtypeStruct((S, D), F32), jax.ShapeDtypeStruct((S, D), F32), jax.ShapeDtypeStruct((S, D), BF)],
        name=name, comm=comm)(mix3, w_out, x, g.reshape(1, D), b.reshape(1, D))


def _ln_bwd(dxn, z, g, name, comm=None, target=None):
    S = z.shape[0]
    tm = min(256, S)
    head = target is not None

    def body(*refs):
        if head:
            d_ref, t_ref, z_ref, g_ref, dz_ref, dzb_ref, dg_ref, db_ref, p_ref = refs
        else:
            d_ref, z_ref, g_ref, dz_ref, dzb_ref, dg_ref, db_ref = refs
        i = pl.program_id(0)
        zz = z_ref[...]
        mu = jnp.mean(zz, axis=1, keepdims=True)
        zc = zz - mu
        var = jnp.mean(zc * zc, axis=1, keepdims=True)
        rstd = lax.rsqrt(var + LN_EPS)
        xhat = zc * rstd
        dy = d_ref[...]
        if head:
            e = dy - t_ref[...]
            dy = e * (1.0 / D)

            @pl.when(i == 0)
            def _():
                p_ref[...] = jnp.zeros_like(p_ref)

            p_ref[...] += jnp.sum(jnp.sum(e * e, axis=1, keepdims=True), axis=0, keepdims=True)
        dyg = dy * g_ref[...]
        m1 = jnp.mean(dyg, axis=1, keepdims=True)
        m2 = jnp.mean(dyg * xhat, axis=1, keepdims=True)
        dz = rstd * (dyg - m1 - xhat * m2)
        dz_ref[...] = dz
        dzb_ref[...] = dz.astype(BF)

        @pl.when(i == 0)
        def _():
            dg_ref[...] = jnp.zeros_like(dg_ref)
            db_ref[...] = jnp.zeros_like(db_ref)

        dg_ref[...] += jnp.sum(dy * xhat, axis=0, keepdims=True)
        db_ref[...] += jnp.sum(dy, axis=0, keepdims=True)

    row = pl.BlockSpec((tm, D), lambda i: (i, 0))
    vec = pl.BlockSpec((1, D), lambda i: (0, 0))
    out_specs = [row, row, vec, vec] + ([pl.BlockSpec((8, 128), lambda i: (0, 0))] if head else [])
    out_shape = [jax.ShapeDtypeStruct((S, D), F32), jax.ShapeDtypeStruct((S, D), BF), jax.ShapeDtypeStruct((1, D), F32),
                 jax.ShapeDtypeStruct((1, D), F32)] + ([jax.ShapeDtypeStruct((8, 128), F32)] if head else [])
    operands = (dxn, target, z, g.reshape(1, D)) if head else (dxn, z, g.reshape(1, D))
    return _pcall(body, grid=(S // tm,), in_specs=[row] * (len(operands) - 1) + [vec], out_specs=out_specs,
                  out_shape=out_shape, name=name, comm=comm)(*operands)


def _rope_fwd(t, r_ref):
    return (t * r_ref[:, 0:128] + pltpu.roll(t, 120, 1) * r_ref[:, 128:256]
            + pltpu.roll(t, 8, 1) * r_ref[:, 256:384])


def _rope_bwd(g, r_ref):
    return (g * r_ref[:, 0:128] + pltpu.roll(g * r_ref[:, 128:256], 8, 1)
            + pltpu.roll(g * r_ref[:, 256:384], 120, 1))


def _dup_heads(kb):
    lo = lax.broadcasted_iota(jnp.int32, kb.shape, 1) < 64
    sw = pltpu.roll(kb, 64, 1)
    return [jnp.where(lo, kb, sw).astype(BF), jnp.where(lo, sw, kb).astype(BF)]


def _even_fwd(h, rope, lng, lnb, ws, bsb, sinks, name, comm=None):
    S = h.shape[0]
    nb = S // CHUNK

    def body(h_ref, hp_ref, rc_ref, rp_ref, lng_ref, lnb_ref, ws_ref, bsb_ref, sink_ref, mix_ref, o_ref, l_ref):
        n = pl.program_id(0)
        lane = lax.broadcasted_iota(jnp.int32, (128, 128), 1)
        rowi = lax.broadcasted_iota(jnp.int32, (128, 128), 0)
        tri = rowi >= lane
        lane_lo = lane < 64
        v = h_ref[:, W:2 * W]
        mu = jnp.mean(v, axis=1, keepdims=True)
        vc = v - mu
        var = jnp.mean(vc * vc, axis=1, keepdims=True)
        vn = vc * lax.rsqrt(var + LN_EPS) * lng_ref[...] + lnb_ref[...]
        ms = [_dot(jnp.where(tri, ws_ref[g], 0.0).astype(BF), vn[:, g * 128:(g + 1) * 128].astype(BF)) for g in range(8)]
        for g in range(8):
            sl = slice(g * 128, (g + 1) * 128)
            ag = h_ref[:, 2 * W + g * 128:2 * W + (g + 1) * 128]
            mix_ref[0, :, sl] = (h_ref[:, sl] * (ms[g] + bsb_ref[g]) * (ag * _sig(ag))).astype(BF)
        kb = jnp.concatenate([_rope_fwd(hp_ref[:, 0:128], rp_ref), _rope_fwd(h_ref[:, 4096:4224], rc_ref)], axis=0)
        vb = jnp.concatenate([hp_ref[:, 128:256], h_ref[:, 4224:4352]], axis=0)
        k2 = _dup_heads(kb)
        v2 = _dup_heads(vb)
        qi = lax.broadcasted_iota(jnp.int32, (128, 256), 0)
        kj = lax.broadcasted_iota(jnp.int32, (128, 256), 1)
        diff = qi + 128 - kj
        valid = (diff >= 0) & (diff < 128) & ((n > 0) | (kj >= 128))
        lacc = jnp.zeros((128, 128), F32)
        for j0 in range(0, 8, HEAD_COLS):
            heads = [(j, half) for j in range(j0, j0 + HEAD_COLS) for half in range(2)]
            sc, pr, oh = {}, {}, {}
            for j in range(j0, j0 + HEAD_COLS):
                qc = _rope_fwd(h_ref[:, 3072 + j * 128:3072 + (j + 1) * 128], rc_ref)
                sc[j, 0] = _dot_nt(jnp.where(lane_lo, qc, 0.0).astype(BF), k2[j // 4])
                sc[j, 1] = _dot_nt(jnp.where(lane_lo, 0.0, qc).astype(BF), k2[j // 4])
            for j, half in heads:
                hq = 2 * j + half
                s = jnp.where(valid, sc[j, half] * 0.125, NEG)
                sk = sink_ref[hq]
                mx = jnp.maximum(jnp.max(s, axis=1, keepdims=True), sk)
                p = jnp.exp(s - mx)
                den = jnp.sum(p, axis=1, keepdims=True) + jnp.exp(sk - mx)
                pr[j, half] = (p / den).astype(BF)
                lacc = jnp.where(lane == hq, mx + jnp.log(den), lacc)
            for j, half in heads:
                oh[j, half] = _dot(pr[j, half], v2[j // 4])
            for j in range(j0, j0 + HEAD_COLS):
                cs = slice(j * 128, (j + 1) * 128)
                ocol = jnp.where(lane_lo, oh[j, 0], oh[j, 1])
                bg = h_ref[:, 4352 + j * 128:4352 + (j + 1) * 128]
                o_ref[:, cs] = ocol
                mix_ref[1, :, cs] = (ocol * (bg * _sig(bg))).astype(BF)
        l_ref[...] = lacc

    prev = lambda n: jnp.maximum(n - 1, 0)
    full = lambda shape: pl.BlockSpec(shape, lambda n: (0,) * len(shape))
    return _pcall(
        body, grid=(nb,),
        in_specs=[pl.BlockSpec((CHUNK, EVEN_IN), lambda n: (n, 0)),
                  pl.BlockSpec((CHUNK, 256), lambda n: (prev(n), 16)),
                  pl.BlockSpec((CHUNK, 384), lambda n: (n, 0)),
                  pl.BlockSpec((CHUNK, 384), lambda n: (prev(n), 0)),
                  full((1, W)), full((1, W)), full((8, 128, 128)), full((8, 128, 128)),
                  pl.BlockSpec(memory_space=pltpu.SMEM)],
        out_specs=[pl.BlockSpec((2, CHUNK, W), lambda n: (0, n, 0)),
                   pl.BlockSpec((CHUNK, W), lambda n: (n, 0)),
                   pl.BlockSpec((CHUNK, 128), lambda n: (n, 0))],
        out_shape=[jax.ShapeDtypeStruct((2, S, W), BF), jax.ShapeDtypeStruct((S, W), F32),
                   jax.ShapeDtypeStruct((S, 128), F32)],
        name=name, comm=comm)(h, h, rope, rope, lng.reshape(1, W), lnb.reshape(1, W), ws, bsb, sinks)


def _even_bwd(h, dmix3, o, l, rope, lng, lnb, ws, wst, bsb, sinks, name, comm=None):
    S = h.shape[0]
    nb = S // CHUNK

    def body(h_ref, hp_ref, hn_ref, dm_ref, dmn_ref, o_ref, on_ref, l_ref, ln_ref, rc_ref, rp_ref, rn_ref,
             lng_ref, lnb_ref, ws_ref, wst_ref, bsb_ref, sink_ref,
             dh_ref, dws_ref, dbs_ref, dlng_ref, dlnb_ref, dsink_ref, dvn_ref):
        n = pl.program_id(0)

        @pl.when(n == 0)
        def _():
            dws_ref[...] = jnp.zeros_like(dws_ref)
            dbs_ref[...] = jnp.zeros_like(dbs_ref)
            dlng_ref[...] = jnp.zeros_like(dlng_ref)
            dlnb_ref[...] = jnp.zeros_like(dlnb_ref)
            dsink_ref[...] = jnp.zeros_like(dsink_ref)

        lane = lax.broadcasted_iota(jnp.int32, (128, 128), 1)
        rowi = lax.broadcasted_iota(jnp.int32, (128, 128), 0)
        lane1 = lax.broadcasted_iota(jnp.int32, (1, 128), 1)
        tri = rowi >= lane
        tri_t = lane >= rowi
        lane_lo = lane < 64
        v = h_ref[:, W:2 * W]
        mu = jnp.mean(v, axis=1, keepdims=True)
        vc = v - mu
        var = jnp.mean(vc * vc, axis=1, keepdims=True)
        rstd = lax.rsqrt(var + LN_EPS)
        vhat = vc * rstd
        vn = vhat * lng_ref[...] + lnb_ref[...]
        dbs_acc = jnp.zeros((128, 128), F32)
        vng = [vn[:, g * 128:(g + 1) * 128].astype(BF) for g in range(8)]
        ms = [_dot(jnp.where(tri, ws_ref[g], 0.0).astype(BF), vng[g]) for g in range(8)]
        dmb = []
        for g in range(8):
            sl = slice(g * 128, (g + 1) * 128)
            m = ms[g] + bsb_ref[g]
            ag = h_ref[:, 2 * W + g * 128:2 * W + (g + 1) * 128]
            sg, dsg = _silu_grad(ag)
            u = h_ref[:, sl]
            da = dm_ref[0, :, sl]
            dmm = da * u * sg
            dh_ref[:, sl] = (da * m * sg).astype(BF)
            dh_ref[:, 2 * W + g * 128:2 * W + (g + 1) * 128] = (da * u * m * dsg).astype(BF)
            dmb.append(dmm.astype(BF))
            dbs_acc = jnp.where(lane == g, jnp.sum(dmm, axis=1, keepdims=True), dbs_acc)
        dvs = [_dot(jnp.where(tri_t, wst_ref[g], 0.0).astype(BF), dmb[g]) for g in range(8)]
        dwss = [_dot_nt(dmb[g], vng[g]) for g in range(8)]
        for g in range(8):
            dvn_ref[:, g * 128:(g + 1) * 128] = dvs[g]
            dws_ref[g] += jnp.where(tri, dwss[g], 0.0)
        dbs_ref[...] += dbs_acc
        dvn = dvn_ref[...]
        dlng_ref[...] += jnp.sum(dvn * vhat, axis=0, keepdims=True)
        dlnb_ref[...] += jnp.sum(dvn, axis=0, keepdims=True)
        dyg = dvn * lng_ref[...]
        m1 = jnp.mean(dyg, axis=1, keepdims=True)
        m2 = jnp.mean(dyg * vhat, axis=1, keepdims=True)
        dh_ref[:, W:2 * W] = (rstd * (dyg - m1 - vhat * m2)).astype(BF)
        kcur = _rope_fwd(h_ref[:, 4096:4224], rc_ref)
        kb = jnp.concatenate([_rope_fwd(hp_ref[:, 0:128], rp_ref), kcur], axis=0)
        vb = jnp.concatenate([hp_ref[:, 128:256], h_ref[:, 4224:4352]], axis=0)
        k2 = _dup_heads(kb)
        v2 = _dup_heads(vb)
        kc2 = _dup_heads(kcur)
        vc2 = _dup_heads(h_ref[:, 4224:4352])
        qi = lax.broadcasted_iota(jnp.int32, (128, 256), 0)
        kj = lax.broadcasted_iota(jnp.int32, (128, 256), 1)
        diff = qi + 128 - kj
        valid = (diff >= 0) & (diff < 128) & ((n > 0) | (kj >= 128))
        validn = (lane > rowi) & (n < nb - 1)
        lc = l_ref[...]
        lnx = ln_ref[...]
        dk = [jnp.zeros((128, 128), F32), jnp.zeros((128, 128), F32)]
        dv = [jnp.zeros((128, 128), F32), jnp.zeros((128, 128), F32)]
        dsk_acc = jnp.zeros((1, 128), F32)
        for j0 in range(0, 8, HEAD_COLS):
            heads = [(j, half) for j in range(j0, j0 + HEAD_COLS) for half in range(2)]
            t = {}
            for j in range(j0, j0 + HEAD_COLS):
                cs = slice(j * 128, (j + 1) * 128)
                qc = _rope_fwd(h_ref[:, 3072 + j * 128:3072 + (j + 1) * 128], rc_ref)
                qn = _rope_fwd(hn_ref[:, 3072 + j * 128:3072 + (j + 1) * 128], rn_ref)
                bg = h_ref[:, 4352 + j * 128:4352 + (j + 1) * 128]
                sgb, dsgb = _silu_grad(bg)
                db = dm_ref[1, :, cs]
                oc = o_ref[:, cs]
                do = db * sgb
                dh_ref[:, 4352 + j * 128:4352 + (j + 1) * 128] = (db * oc * dsgb).astype(BF)
                bgn = hn_ref[:, 4352 + j * 128:4352 + (j + 1) * 128]
                don = dmn_ref[1, :, cs] * (bgn * _sig(bgn))
                prod = do * oc
                prodn = don * on_ref[:, cs]
                for half in range(2):
                    hq = 2 * j + half
                    hm = lane_lo if half == 0 else jnp.logical_not(lane_lo)
                    t[j, half] = dict(
                        dsum=jnp.sum(jnp.where(hm, prod, 0.0), axis=1, keepdims=True),
                        dsumn=jnp.sum(jnp.where(hm, prodn, 0.0), axis=1, keepdims=True),
                        lh=jnp.sum(jnp.where(lane == hq, lc, 0.0), axis=1, keepdims=True),
                        lhn=jnp.sum(jnp.where(lane == hq, lnx, 0.0), axis=1, keepdims=True),
                        qm=jnp.where(hm, qc, 0.0).astype(BF), dom=jnp.where(hm, do, 0.0).astype(BF),
                        qnm=jnp.where(hm, qn, 0.0).astype(BF), donm=jnp.where(hm, don, 0.0).astype(BF))
            for j, half in heads:
                e, hk = t[j, half], j // 4
                e["s"], e["dp"] = _dot_nt(e["qm"], k2[hk]), _dot_nt(e["dom"], v2[hk])
                e["sn"], e["dpn"] = _dot_nt(e["qnm"], kc2[hk]), _dot_nt(e["donm"], vc2[hk])
            for j, half in heads:
                e, hq = t[j, half], 2 * j + half
                p = jnp.exp(jnp.where(valid, e["s"] * 0.125 - e["lh"], NEG))
                ds = p * (e["dp"] - e["dsum"])
                pn = jnp.exp(jnp.where(validn, e["sn"] * 0.125 - e["lhn"], NEG))
                dsn = pn * (e["dpn"] - e["dsumn"])
                psink = jnp.exp(sink_ref[hq] - e["lh"])
                dsk_acc = jnp.where(lane1 == hq, -jnp.sum(psink * e["dsum"], axis=0, keepdims=True), dsk_acc)
                e["ds"] = ds.astype(BF)
                e["pt"], e["dst"] = jnp.transpose(p[:, 128:256]).astype(BF), jnp.transpose(ds[:, 128:256]).astype(BF)
                e["pnt"], e["dsnt"] = jnp.transpose(pn).astype(BF), jnp.transpose(dsn).astype(BF)
            for j, half in heads:
                e, hk = t[j, half], j // 4
                e["dq"] = _dot(e["ds"], k2[hk])
                e["dv"] = _dot(e["pt"], e["dom"]) + _dot(e["pnt"], e["donm"])
                e["dk"] = _dot(e["dst"], e["qm"]) + _dot(e["dsnt"], e["qnm"])
            for j in range(j0, j0 + HEAD_COLS):
                hk = j // 4
                dqcol = jnp.where(lane_lo, t[j, 0]["dq"], t[j, 1]["dq"]) * 0.125
                dh_ref[:, 3072 + j * 128:3072 + (j + 1) * 128] = _rope_bwd(dqcol, rc_ref).astype(BF)
                dv[hk] = dv[hk] + t[j, 0]["dv"] + t[j, 1]["dv"]
                dk[hk] = dk[hk] + (t[j, 0]["dk"] + t[j, 1]["dk"]) * 0.125
        fold = lambda a: a + pltpu.roll(a, 64, 1)
        dh_ref[:, 4096:4224] = _rope_bwd(jnp.where(lane_lo, fold(dk[0]), fold(dk[1])), rc_ref).astype(BF)
        dh_ref[:, 4224:4352] = jnp.where(lane_lo, fold(dv[0]), fold(dv[1])).astype(BF)
        dsink_ref[...] += dsk_acc

    prev = lambda n: jnp.maximum(n - 1, 0)
    nxt = lambda n: jnp.minimum(n + 1, nb - 1)
    full = lambda shape: pl.BlockSpec(shape, lambda n: (0,) * len(shape))
    return _pcall(
        body, grid=(nb,),
        in_specs=[pl.BlockSpec((CHUNK, EVEN_IN), lambda n: (n, 0)),
                  pl.BlockSpec((CHUNK, 256), lambda n: (prev(n), 16)),
                  pl.BlockSpec((CHUNK, EVEN_IN), lambda n: (nxt(n), 0)),
                  pl.BlockSpec((2, CHUNK, W), lambda n: (0, n, 0)),
                  pl.BlockSpec((2, CHUNK, W), lambda n: (0, nxt(n), 0)),
                  pl.BlockSpec((CHUNK, W), lambda n: (n, 0)),
                  pl.BlockSpec((CHUNK, W), lambda n: (nxt(n), 0)),
                  pl.BlockSpec((CHUNK, 128), lambda n: (n, 0)),
                  pl.BlockSpec((CHUNK, 128), lambda n: (nxt(n), 0)),
                  pl.BlockSpec((CHUNK, 384), lambda n: (n, 0)),
                  pl.BlockSpec((CHUNK, 384), lambda n: (prev(n), 0)),
                  pl.BlockSpec((CHUNK, 384), lambda n: (nxt(n), 0)),
                  full((1, W)), full((1, W)), full((8, 128, 128)), full((8, 128, 128)), full((8, 128, 128)),
                  pl.BlockSpec(memory_space=pltpu.SMEM)],
        out_specs=[pl.BlockSpec((CHUNK, EVEN_IN), lambda n: (n, 0)),
                   full((8, 128, 128)), full((128, 128)), full((1, W)), full((1, W)), full((1, 128))],
        out_shape=[jax.ShapeDtypeStruct((S, EVEN_IN), BF), jax.ShapeDtypeStruct((8, 128, 128), F32),
                   jax.ShapeDtypeStruct((128, 128), F32), jax.ShapeDtypeStruct((1, W), F32),
                   jax.ShapeDtypeStruct((1, W), F32), jax.ShapeDtypeStruct((1, 128), F32)],
        scratch=[pltpu.VMEM((CHUNK, W), F32)], name=name, comm=comm,
    )(h, h, h, dmix3, dmix3, o, o, l, l, rope, rope, rope, lng.reshape(1, W), lnb.reshape(1, W), ws, wst, bsb, sinks)


def _expm1(x):
    ser = x * (1.0 + x * (0.5 + x * (1.0 / 6.0 + x * (1.0 / 24.0))))
    return jnp.where(jnp.abs(x) < 1e-2, ser, jnp.exp(x) - 1.0)


def _softplus_neg(lam):
    z = -lam
    e = jnp.exp(-jnp.abs(z))
    l1p = jnp.where(e < 1e-3, e * (1.0 - e * (0.5 - e * (1.0 / 3.0))), jnp.log(1.0 + e))
    return jnp.maximum(z, 0.0) + l1p


def _shift_down(x, k, row, fill=0.0):
    return jnp.where(row >= k, pltpu.roll(x, k, 0), fill)


def _shift_up(x, k, row, fill=0.0):
    S = x.shape[0]
    return jnp.where(row < S - k, pltpu.roll(x, S - k, 0), fill)


def _lru_gates(xc, row, cw_ref, cb_ref, wa_ref, wx_ref, ba_ref, bx_ref, lam_ref):
    xconv = (cw_ref[3:4, :] * xc + cw_ref[2:3, :] * _shift_down(xc, 1, row) + cw_ref[1:2, :] * _shift_down(xc, 2, row)
             + cw_ref[0:1, :] * _shift_down(xc, 3, row) + cb_ref[...])
    xb = xconv.astype(BF)
    r = _sig(_dot(xb, wa_ref[...]) + ba_ref[...])
    i = _sig(_dot(xb, wx_ref[...]) + bx_ref[...])
    sp = _softplus_neg(lam_ref[...])
    log_a = -LRU_C * r * sp
    a = jnp.exp(log_a)
    mult = jnp.sqrt(-_expm1(2.0 * log_a))
    return xconv, r, i, sp, a, mult


ROWS_PER_TILE = 8


def _steps(a, b, shift, inside, products=True):
    n, k = inside.n, 1
    while k < n:
        b = a * jnp.where(inside(k), shift(b, k), 0.0) + b
        if products or 2 * k < n:
            a = a * jnp.where(inside(k), shift(a, k), 1.0)
        k *= 2
    return a, b


class _Inside:
    def __init__(self, pos, n, reverse):
        self.pos, self.n, self.reverse = pos, n, reverse

    def __call__(self, k):
        return self.pos < self.n - k if self.reverse else self.pos >= k


def _scan_rows(a, b, row, a_ref, b_ref, c_ref, reverse=False):
    S = a.shape[0]
    G = S // ROWS_PER_TILE
    if reverse:
        shift = lambda x, k: pltpu.roll(x, x.shape[0] - k, 0)
    else:
        shift = lambda x, k: pltpu.roll(x, k, 0)
    a, b = _steps(a, b, shift, _Inside(row % ROWS_PER_TILE, ROWS_PER_TILE, reverse))
    a_ref[...] = a
    b_ref[...] = b
    last = 0 if reverse else ROWS_PER_TILE - 1
    grow = lax.broadcasted_iota(jnp.int32, (G, a.shape[1]), 0)
    _, tot = _steps(a_ref[pl.ds(last, G, stride=ROWS_PER_TILE), :], b_ref[pl.ds(last, G, stride=ROWS_PER_TILE), :],
                    shift, _Inside(grow, G, reverse), products=False)
    enters = jnp.where(_Inside(grow, G, reverse)(1), shift(tot, 1), 0.0)
    for r in range(ROWS_PER_TILE):
        c_ref[pl.ds(r, G, stride=ROWS_PER_TILE), :] = enters
    return b + a * c_ref[...]


def _odd_c_fwd(h, cw, cb, wa, wx, ba, bx, lam, name, comm=None):
    S = h.shape[0]

    def body(xc_ref, cg_ref, cw_ref, cb_ref, wa_ref, wx_ref, ba_ref, bx_ref, lam_ref, mix_ref, hst_ref, sa_ref, sb_ref, sc_ref):
        row = lax.broadcasted_iota(jnp.int32, (S, 128), 0)
        xconv, r, i, sp, a, mult = _lru_gates(xc_ref[...], row, cw_ref, cb_ref, wa_ref, wx_ref, ba_ref, bx_ref, lam_ref)
        bb = _scan_rows(a, mult * (i * xconv), row, sa_ref, sb_ref, sc_ref)
        hst_ref[...] = bb
        cg = cg_ref[...]
        mix_ref[...] = (bb * (cg * _sig(cg))).astype(BF)

    col = lambda off: pl.BlockSpec((S, 128), lambda j: (0, off + j))
    vec = pl.BlockSpec((1, 128), lambda j: (0, j))
    mat = pl.BlockSpec((None, 128, 128), lambda j: (j, 0, 0))
    return _pcall(
        body, grid=(8,),
        in_specs=[col(0), col(8), pl.BlockSpec((4, 128), lambda j: (0, j)), vec, mat, mat, vec, vec, vec],
        out_specs=[pl.BlockSpec((None, S, 128), lambda j: (0, 0, j)), pl.BlockSpec((S, 128), lambda j: (0, j))],
        out_shape=[jax.ShapeDtypeStruct((2, S, W), BF), jax.ShapeDtypeStruct((S, W), F32)],
        scratch=[pltpu.VMEM((S, 128), F32)] * 3, name=name, comm=comm,
    )(h, h, cw, cb.reshape(1, W), wa, wx, ba.reshape(1, W), bx.reshape(1, W), lam.reshape(1, W))


def _pool_sums(x, g, row, shift):
    s2 = x + shift(x, 1, row)
    s4 = s2 + shift(s2, 2, row)
    s8 = s4 + shift(s4, 4, row)
    s16 = s8 + shift(s8, 8, row)
    return jnp.where(g == 0, s2, jnp.where(g == 1, s4, jnp.where(g == 2, s8, s16)))


def _odd_d_fwd(h, mix3, wp, dscale, name):
    S = h.shape[0]

    def body(xd_ref, dg_ref, wp_ref, ds_ref, mix_in, mix_ref):
        g = pl.program_id(0)
        row = lax.broadcasted_iota(jnp.int32, (S, 256), 0)
        xd = xd_ref[...]
        cnt = jnp.minimum(row + 1, jnp.left_shift(2, g)).astype(F32)
        pooled = _pool_sums(xd, g, row, _shift_down) / cnt - xd
        mixed = _dot(pooled.astype(BF), wp_ref[...])
        dg = dg_ref[...]
        mix_ref[...] = (mixed * ds_ref[...] * (dg * _sig(dg))).astype(BF)

    col = lambda off: pl.BlockSpec((S, 256), lambda g: (0, off + g))
    return pl.pallas_call(
        body, grid=(4,),
        in_specs=[col(8), col(12), pl.BlockSpec((None, 256, 256), lambda g: (g, 0, 0)),
                  pl.BlockSpec((1, 256), lambda g: (0, g)), ANY],
        out_specs=pl.BlockSpec((None, S, 256), lambda g: (1, 0, g)),
        out_shape=jax.ShapeDtypeStruct((2, S, W), BF), input_output_aliases={4: 0},
        name=name, compiler_params=_cp(),
    )(h, h, wp, dscale.reshape(1, W), mix3)


def _odd_c_bwd(h, hst, dmix3, cw, cb, wa, wx, wat, wxt, ba, bx, lam, name, comm=None):
    S = h.shape[0]

    def body(xc_ref, cg_ref, hst_ref, dc_ref, cw_ref, cb_ref, wa_ref, wx_ref, wat_ref, wxt_ref, ba_ref, bx_ref, lam_ref,
             dh_ref, dcw_ref, dcb_ref, dwa_ref, dwx_ref, dba_ref, dbx_ref, dlam_ref, sa_ref, sb_ref, sc_ref):
        row = lax.broadcasted_iota(jnp.int32, (S, 128), 0)
        xc = xc_ref[...]
        xconv, r, i, sp, a, mult = _lru_gates(xc, row, cw_ref, cb_ref, wa_ref, wx_ref, ba_ref, bx_ref, lam_ref)
        hst = hst_ref[...]
        cg = cg_ref[...]
        sg, dsg = _silu_grad(cg)
        dc = dc_ref[...]
        dh_ref[1] = (dc * hst * dsg).astype(BF)
        lam_t = _scan_rows(_shift_up(a, 1, row), dc * sg, row, sa_ref, sb_ref, sc_ref, reverse=True)
        da = lam_t * _shift_down(hst, 1, row)
        ix = i * xconv
        dmult = lam_t * ix
        di = lam_t * mult * xconv
        dxconv = lam_t * mult * i
        dlog_a = da * a - dmult * (a * a / mult)
        dr = dlog_a * (-LRU_C * sp)
        dsp = jnp.sum(dlog_a * (-LRU_C * r), axis=0, keepdims=True)
        dlam_ref[...] = dsp * (-_sig(-lam_ref[...]))
        dpa = dr * r * (1.0 - r)
        dpx = di * i * (1.0 - i)
        dpab = dpa.astype(BF)
        dpxb = dpx.astype(BF)
        xb = xconv.astype(BF)
        dxconv = dxconv + _dot(dpab, wat_ref[...]) + _dot(dpxb, wxt_ref[...])
        dwa_ref[...] = _dot_tn(xb, dpab)
        dwx_ref[...] = _dot_tn(xb, dpxb)
        dba_ref[...] = jnp.sum(dpa, axis=0, keepdims=True)
        dbx_ref[...] = jnp.sum(dpx, axis=0, keepdims=True)
        dh_ref[0] = (cw_ref[3:4, :] * dxconv + cw_ref[2:3, :] * _shift_up(dxconv, 1, row)
                     + cw_ref[1:2, :] * _shift_up(dxconv, 2, row) + cw_ref[0:1, :] * _shift_up(dxconv, 3, row)).astype(BF)
        for j in range(4):
            src = xc if j == 3 else _shift_down(xc, 3 - j, row)
            dcw_ref[j:j + 1, :] = jnp.sum(dxconv * src, axis=0, keepdims=True)
        dcb_ref[...] = jnp.sum(dxconv, axis=0, keepdims=True)

    col = lambda off: pl.BlockSpec((S, 128), lambda j: (0, off + j))
    vec = pl.BlockSpec((1, 128), lambda j: (0, j))
    mat = pl.BlockSpec((None, 128, 128), lambda j: (j, 0, 0))
    vshape = jax.ShapeDtypeStruct((1, W), F32)
    mshape = jax.ShapeDtypeStruct((8, 128, 128), F32)
    return _pcall(
        body, grid=(8,),
        in_specs=[col(0), col(8), col(0), pl.BlockSpec((None, S, 128), lambda j: (0, 0, j)),
                  pl.BlockSpec((4, 128), lambda j: (0, j)), vec, mat, mat, mat, mat, vec, vec, vec],
        out_specs=[pl.BlockSpec((2, S, 128), lambda j: (0, 0, j)), pl.BlockSpec((4, 128), lambda j: (0, j)), vec,
                   mat, mat, vec, vec, vec],
        out_shape=[jax.ShapeDtypeStruct((4, S, W), BF), jax.ShapeDtypeStruct((4, W), F32), vshape, mshape, mshape,
                   vshape, vshape, vshape],
        scratch=[pltpu.VMEM((S, 128), F32)] * 3, name=name, vmem=56, comm=comm,
    )(h, h, hst, dmix3, cw, cb.reshape(1, W), wa, wx, wat, wxt, ba.reshape(1, W), bx.reshape(1, W), lam.reshape(1, W))


def _odd_d_bwd(h, dmix3, dh4, wp, wpt, dscale, name):
    S = h.shape[0]

    def body(xd_ref, dg_ref, dd_ref, wp_ref, wpt_ref, ds_ref, dh_in, dh_ref, dwp_ref, dds_ref):
        g = pl.program_id(0)
        row = lax.broadcasted_iota(jnp.int32, (S, 256), 0)
        xd = xd_ref[...]
        cnt = jnp.minimum(row + 1, jnp.left_shift(2, g)).astype(F32)
        pooled = _pool_sums(xd, g, row, _shift_down) / cnt - xd
        pb = pooled.astype(BF)
        mixed = _dot(pb, wp_ref[...])
        dg = dg_ref[...]
        sg, dsg = _silu_grad(dg)
        dd = dd_ref[...]
        dmixed = dd * ds_ref[...] * sg
        dds_ref[...] = jnp.sum(dd * mixed * sg, axis=0, keepdims=True)
        dh_ref[1] = (dd * mixed * ds_ref[...] * dsg).astype(BF)
        dmb = dmixed.astype(BF)
        dpooled = _dot(dmb, wpt_ref[...])
        dwp_ref[...] = _dot_tn(pb, dmb)
        dh_ref[0] = (_pool_sums(dpooled / cnt, g, row, _shift_up) - dpooled).astype(BF)

    col = lambda off: pl.BlockSpec((S, 256), lambda g: (0, off + g))
    mat = pl.BlockSpec((None, 256, 256), lambda g: (g, 0, 0))
    vec = pl.BlockSpec((1, 256), lambda g: (0, g))
    return pl.pallas_call(
        body, grid=(4,),
        in_specs=[col(8), col(12), pl.BlockSpec((None, S, 256), lambda g: (1, 0, g)), mat, mat, vec, ANY],
        out_specs=[pl.BlockSpec((2, S, 256), lambda g: (1, 0, g)), mat, vec],
        out_shape=[jax.ShapeDtypeStruct((4, S, W), BF), jax.ShapeDtypeStruct((4, 256, 256), F32),
                   jax.ShapeDtypeStruct((1, W), F32)],
        input_output_aliases={6: 0}, name=name, compiler_params=_cp(56),
    )(h, h, dmix3, wp, wpt, dscale.reshape(1, W), dh4)


def _peer(d):
    x, y, c = lax.axis_index("x"), lax.axis_index("y"), lax.axis_index("c")
    px = 1 - x if d & 4 else x
    py = 1 - y if d & 2 else y
    pc = 1 - c if d & 1 else c
    return (px, py, pc), 4 * px + 2 * py + pc


class _GatherAll(_Comm):
    def __init__(self, xs):
        self.peers = EVERYONE
        self.inputs = [xs]
        self.out_shapes = [jax.ShapeDtypeStruct((N_DEV,) + xs.shape, xs.dtype)]
        self.sem_shapes = [pltpu.SemaphoreType.DMA((N_DEV - 1,)), pltpu.SemaphoreType.DMA((N_DEV - 1,)),
                           pltpu.SemaphoreType.DMA]

    def copies(self, ins, outs, sems):
        (x_ref,), (out_ref,), (send, recv, loc) = ins, outs, sems
        _, me = _peer(0)
        res = [pltpu.make_async_copy(x_ref, out_ref.at[me], loc)]
        for d in range(1, N_DEV):
            peer, _ = _peer(d)
            res.append(pltpu.make_async_remote_copy(src_ref=x_ref, dst_ref=out_ref.at[me], send_sem=send.at[d - 1],
                                                    recv_sem=recv.at[d - 1], device_id=peer, device_id_type=MESH))
        return res


class _ExchangeAll(_Comm):
    def __init__(self, g8):
        self.peers = EVERYONE
        self.inputs = [g8]
        self.out_shapes = [jax.ShapeDtypeStruct(g8.shape, g8.dtype)]
        self.sem_shapes = [pltpu.SemaphoreType.DMA((N_DEV - 1,)), pltpu.SemaphoreType.DMA((N_DEV - 1,)),
                           pltpu.SemaphoreType.DMA]

    def copies(self, ins, outs, sems):
        (g_ref,), (out_ref,), (send, recv, loc) = ins, outs, sems
        _, me = _peer(0)
        res = [pltpu.make_async_copy(g_ref.at[me], out_ref.at[0], loc)]
        for d in range(1, N_DEV):
            peer, pidx = _peer(d)
            res.append(pltpu.make_async_remote_copy(src_ref=g_ref.at[pidx], dst_ref=out_ref.at[d], send_sem=send.at[d - 1],
                                                    recv_sem=recv.at[d - 1], device_id=peer, device_id_type=MESH))
        return res


def _sum8(r8, tr, name):
    _, R, C = r8.shape
    tr = min(tr, R)
    assert R % tr == 0

    def body(r_ref, o_ref):
        acc = r_ref[0]
        for d in range(1, N_DEV):
            acc = acc + r_ref[d]
        o_ref[...] = acc

    return pl.pallas_call(
        body, grid=(R // tr,), in_specs=[pl.BlockSpec((N_DEV, tr, C), lambda i: (0, i, 0))],
        out_specs=pl.BlockSpec((tr, C), lambda i: (i, 0)), out_shape=jax.ShapeDtypeStruct((R, C), F32),
        name=name, compiler_params=_cp(),
    )(r8)


def _adamw_math(w, g, m, v):
    m2 = B1 * m + (1.0 - B1) * g
    v2 = B2 * v + (1.0 - B2) * (g * g)
    m_hat = m2 / (1.0 - B1 ** STEP)
    v_hat = v2 / (1.0 - B2 ** STEP)
    return -LR * (m_hat / (jnp.sqrt(v_hat) + ADAM_EPS) + WD * w), m2, v2


def _adamw_many(ws, gs, ms, vs, name):
    n = len(ws)

    def body(*refs):
        for i in range(n):
            d, m2, v2 = _adamw_math(refs[i][...], refs[n + i][...], refs[2 * n + i][...], refs[3 * n + i][...])
            refs[4 * n + i][...] = d
            refs[5 * n + i][...] = m2
            refs[6 * n + i][...] = v2

    vmem = pl.BlockSpec(memory_space=pltpu.VMEM)
    shapes = [jax.ShapeDtypeStruct(w.shape, F32) for w in ws]
    res = pl.pallas_call(body, in_specs=[vmem] * (4 * n), out_specs=[vmem] * (3 * n), out_shape=shapes * 3, name=name,
                         compiler_params=_cp())(*ws, *gs, *ms, *vs)
    return res[:n], res[n:2 * n], res[2 * n:]


def _adamw(w3, gs, m3, v3, tr, name, comm=None):
    _, R, C = w3.shape
    n = 2 if isinstance(gs[0], tuple) else 1

    def gradient(refs):
        if n == 1:
            return refs[0][...]
        s_ref, r_ref = refs
        return ((s_ref[...].astype(F32) + r_ref[0].astype(F32)) + r_ref[1].astype(F32)) + r_ref[2].astype(F32)

    def body(w_ref, *rest):
        g_refs, (m_ref, v_ref, d_ref, m2_ref, v2_ref, g_ref) = rest[:2 * n], rest[2 * n:]
        g = jnp.where(pl.program_id(0) == 0, gradient(g_refs[:n]), gradient(g_refs[n:]))
        d_ref[...], m2_ref[...], v2_ref[...] = _adamw_math(w_ref[...], g, m_ref[...], v_ref[...])
        g_ref[...] = g

    blk = pl.BlockSpec((None, tr, C), lambda j, i: (j, i, 0))

    def grad_specs(layer):
        at = lambda j, i: jnp.where(j == layer, i, 0)
        if n == 1:
            return [pl.BlockSpec((tr, C), lambda j, i: (at(j, i), 0))]
        return [pl.BlockSpec((None, tr, C), lambda j, i: (0, at(j, i), 0)), pl.BlockSpec((3, tr, C), lambda j, i: (0, at(j, i), 0))]

    flat = [a for g in gs for a in (g if n == 2 else (g,))]
    shp = jax.ShapeDtypeStruct((2, R, C), F32)
    return _pcall(body, grid=(2, R // tr), in_specs=[blk] + grad_specs(0) + grad_specs(1) + [blk, blk], out_specs=[blk] * 4,
                  out_shape=[shp] * 4, name=name, comm=comm)(w3, *flat, m3, v3)


def _rep_pack(a):
    n = a.size
    pad = (-n) % 1024
    f = a.reshape(-1)
    if pad:
        f = jnp.concatenate([f, jnp.zeros((pad,), a.dtype)])
    return f.reshape(N_DEV, -1, 128)


def _rep_unpack(p, shape):
    n = 1
    for s in shape:
        n *= s
    return p.reshape(-1)[:n].reshape(shape)


def _sh_pack(a, axis):
    shp = a.shape
    a = a.reshape(shp[:axis] + (N_DEV, shp[axis] // N_DEV) + shp[axis + 1:])
    return jnp.moveaxis(a, axis, 0).reshape(N_DEV, -1, 128)


def _sh_unpack(p, shape, axis):
    a = p.reshape((N_DEV,) + shape[:axis] + (shape[axis] // N_DEV,) + shape[axis + 1:])
    return jnp.moveaxis(a, 0, axis).reshape(shape)


def _pad_rows(a, mult=8):
    pad = (-a.shape[-2]) % mult
    if pad:
        a = jnp.concatenate([a, jnp.zeros(a.shape[:-2] + (pad, a.shape[-1]), a.dtype)], axis=-2)
    return a


REP = ["even_a_ln_g", "even_a_ln_b", "even_a_ws", "even_a_bs", "even_b_sinks", "even_ln_g", "even_ln_b",
       "odd_w_a", "odd_w_x"]
SH = [("odd_conv_w", (2, 4, W), 2), ("odd_conv_b", (2, W), 1), ("odd_b_a", (2, W), 1), ("odd_b_x", (2, W), 1),
      ("odd_lam", (2, W), 1), ("odd_w_pool", (2, 4, 256, 256), 2), ("odd_d_scale", (2, W), 1),
      ("odd_ln_g", (2, D), 1), ("odd_ln_b", (2, D), 1)]
BIG = ["even_w_in", "even_w_out", "odd_w_in", "odd_w_out"]
NAMES = ["even_w_in", "even_a_ln_g", "even_a_ln_b", "even_a_ws", "even_a_bs", "even_b_sinks", "even_w_out",
         "even_ln_g", "even_ln_b", "odd_w_in", "odd_conv_w", "odd_conv_b", "odd_w_a", "odd_b_a", "odd_w_x", "odd_b_x",
         "odd_lam", "odd_w_pool", "odd_d_scale", "odd_w_out", "odd_ln_g", "odd_ln_b"]


def _rope_table(positions):
    inv = ROPE_THETA ** (-jnp.arange(0, 16, 2, dtype=F32) / 16)
    f = jnp.arange(128) % 64
    ang = positions.astype(F32)[:, None] * inv[f % 8][None, :]
    cos, sin = jnp.cos(ang), jnp.sin(ang)
    return jnp.concatenate([jnp.where(f < 16, cos, 1.0), jnp.where(f < 8, -sin, 0.0),
                            jnp.where((f >= 8) & (f < 16), sin, 0.0)], axis=1)


def kernel(x, positions, even_w_in, even_a_ln_g, even_a_ln_b, even_a_ws, even_a_bs, even_b_sinks, even_w_out, even_ln_g, even_ln_b, odd_w_in, odd_conv_w, odd_conv_b, odd_w_a, odd_b_a, odd_w_x, odd_b_x, odd_lam, odd_w_pool, odd_d_scale, odd_w_out, odd_ln_g, odd_ln_b, loss_target, m_even_w_in, m_even_a_ln_g, m_even_a_ln_b, m_even_a_ws, m_even_a_bs, m_even_b_sinks, m_even_w_out, m_even_ln_g, m_even_ln_b, m_odd_w_in, m_odd_conv_w, m_odd_conv_b, m_odd_w_a, m_odd_b_a, m_odd_w_x, m_odd_b_x, m_odd_lam, m_odd_w_pool, m_odd_d_scale, m_odd_w_out, m_odd_ln_g, m_odd_ln_b, v_even_w_in, v_even_a_ln_g, v_even_a_ln_b, v_even_a_ws, v_even_a_bs, v_even_b_sinks, v_even_w_out, v_even_ln_g, v_even_ln_b, v_odd_w_in, v_odd_conv_w, v_odd_conv_b, v_odd_w_a, v_odd_b_a, v_odd_w_x, v_odd_b_x, v_odd_lam, v_odd_w_pool, v_odd_d_scale, v_odd_w_out, v_odd_ln_g, v_odd_ln_b):
    args = (even_w_in, even_a_ln_g, even_a_ln_b, even_a_ws, even_a_bs, even_b_sinks, even_w_out, even_ln_g, even_ln_b,
            odd_w_in, odd_conv_w, odd_conv_b, odd_w_a, odd_b_a, odd_w_x, odd_b_x, odd_lam, odd_w_pool, odd_d_scale,
            odd_w_out, odd_ln_g, odd_ln_b)
    margs = (m_even_w_in, m_even_a_ln_g, m_even_a_ln_b, m_even_a_ws, m_even_a_bs, m_even_b_sinks, m_even_w_out,
             m_even_ln_g, m_even_ln_b, m_odd_w_in, m_odd_conv_w, m_odd_conv_b, m_odd_w_a, m_odd_b_a, m_odd_w_x,
             m_odd_b_x, m_odd_lam, m_odd_w_pool, m_odd_d_scale, m_odd_w_out, m_odd_ln_g, m_odd_ln_b)
    vargs = (v_even_w_in, v_even_a_ln_g, v_even_a_ln_b, v_even_a_ws, v_even_a_bs, v_even_b_sinks, v_even_w_out,
             v_even_ln_g, v_even_ln_b, v_odd_w_in, v_odd_conv_w, v_odd_conv_b, v_odd_w_a, v_odd_b_a, v_odd_w_x,
             v_odd_b_x, v_odd_lam, v_odd_w_pool, v_odd_d_scale, v_odd_w_out, v_odd_ln_g, v_odd_ln_b)
    wts = dict(zip(NAMES, args))
    mom = dict(zip(NAMES, margs))
    var = dict(zip(NAMES, vargs))
    S = x.shape[1]
    x0 = x[0]
    rope = _rope_table(positions[0])

    kinds = ("even", "odd", "even", "odd")
    blk_in = [jnp.transpose(wts[kinds[l] + "_w_in"][l // 2]).astype(BF) for l in range(4)]
    blk_out = [wts[kinds[l] + "_w_out"][l // 2].astype(BF) for l in range(4)]
    sh_local = _pad_rows(jnp.concatenate([wts[nm].reshape(-1, 128) for nm, _, _ in SH], axis=0), 16)
    me = 4 * lax.axis_index("x") + 2 * lax.axis_index("y") + lax.axis_index("c")
    own_slot = lambda blk: lax.dynamic_update_slice(lax.empty((N_DEV,) + blk.shape, blk.dtype), blk[None], (me, 0, 0))
    reg = {"blk_small": sh_local, "w_small": own_slot(sh_local)}
    sched = _Sched(reg)
    for l in range(4):
        reg[f"blk_in{l}"], reg[f"blk_out{l}"] = blk_in[l], blk_out[l]
        reg[f"w_in{l}"], reg[f"w_out{l}"] = own_slot(blk_in[l]), own_slot(blk_out[l])
    sched.add(_rows("blk_in0", "w_in0", "ag1", blk_in[0].shape[0], ROW_CHUNK[blk_in[0].shape[0]]))
    sched.add(_rows("blk_small", "w_small", "ag1", sh_local.shape[0], sh_local.shape[0]))
    for l in range(4):
        sched.add(_rows(f"blk_out{l}", f"w_out{l}", "ag1", D // N_DEV, ROW_CHUNK[D // N_DEV]))
        if l < 3:
            r = blk_in[l + 1].shape[0]
            sched.add(_rows(f"blk_in{l + 1}", f"w_in{l + 1}", "ag1", r, ROW_CHUNK[r]))

    def gathered(dst, blk):
        sched.flush(dst, FLUSH_EXTRA_US)
        return reg.pop(dst)

    wt_in0 = gathered("w_in0", blk_in[0]).reshape(-1, D)
    full = {nm: wts[nm] for nm in REP}

    def gather_small():
        sh_all = gathered("w_small", sh_local)
        off = 0
        for nm, shape, axis in SH:
            r = wts[nm].size // 128
            full[nm] = _sh_unpack(sh_all[:, off:off + r, :], shape, axis)
            off += r

    saved = []
    wt_in, w_out = [wt_in0, None, None, None], [None] * 4
    xf, xb = x0, x0.astype(BF)
    fwd = lambda name: FWD_OVERBOOK * CARRY_US[name]
    for layer in range(4):
        j = layer // 2
        kind = kinds[layer]
        if wt_in[layer] is None:
            wt_in[layer] = gathered(f"w_in{layer}", blk_in[layer]).reshape(-1, D)
        h = sched.run(_mm_nt, fwd("mm_h_" + kind), xb, wt_in[layer], 1024, 768 if kind == "even" else 512, "mm_h_" + kind)
        if kind == "even":
            bsb = jnp.broadcast_to(full["even_a_bs"][j][:, :, None], (8, 128, 128))
            mix3, o, l = sched.run(_even_fwd, fwd("even_fwd"), h, rope, full["even_a_ln_g"][j], full["even_a_ln_b"][j],
                                   full["even_a_ws"][j], bsb, full["even_b_sinks"][j], "even_fwd")
            extra = (o, l, bsb)
        else:
            if "odd_lam" not in full:
                gather_small()
            wa, wx = full["odd_w_a"][j].astype(BF), full["odd_w_x"][j].astype(BF)
            wp = full["odd_w_pool"][j].astype(BF)
            mix3, hst = sched.run(_odd_c_fwd, fwd("odd_c_fwd"), h, full["odd_conv_w"][j], full["odd_conv_b"][j], wa, wx,
                                  full["odd_b_a"][j], full["odd_b_x"][j], full["odd_lam"][j], "odd_c_fwd")
            mix3 = _odd_d_fwd(h, mix3, wp, full["odd_d_scale"][j], "odd_d_fwd")
            extra = (hst, wa, wx, wp)
        w_out[layer] = gathered(f"w_out{layer}", blk_out[layer]).reshape(D, D)
        z, xn, xnb = sched.run(_mm_out_ln, fwd("mm_out_ln"), mix3, w_out[layer], xf, full[kind + "_ln_g"][j],
                               full[kind + "_ln_b"][j], "mm_out_ln")
        saved.append((xb, h, mix3, z, extra))
        xf, xb = xn, xnb

    dxn = xf

    gsum = {nm: [None, None] for nm in NAMES}

    chip_sums = {}
    sched.overhang = 0.15

    waiting = []

    def chip_sum(g, tag, key):
        r = g.shape[0] // N_DEV
        reg["g_" + key] = g.reshape(N_DEV, r, D)
        sched.add(_rows("g_" + key, "d_" + key, "rsd", r, r), first=True)
        waiting.append((key, tag))

    def add_arrived():
        for key, tag in list(waiting):
            if "d_" + key in reg and not sched.pending("d_" + key):
                waiting.remove((key, tag))
                g8 = reg.pop("g_" + key)
                chip_sums[key] = reg["s_" + key] = _add_pairs(g8, reg.pop("d_" + key), "rs_add_" + tag)
                sched.add(_rows("s_" + key, "r_" + key, "rs", g8.shape[1], ROW_CHUNK[g8.shape[1]] // 2))

    sched.after_landing = add_arrived

    def reduced(key):
        sched.flush("d_" + key, FLUSH_EXTRA_US)
        sched.flush("r_" + key, FLUSH_EXTRA_US)
        return chip_sums[key], reg.pop("r_" + key)

    for layer in (3, 2, 1, 0):
        j = layer // 2
        xb, h, mix3, z, extra = saved[layer]
        kind = kinds[layer]
        if layer == 3:
            dz, dzb, dg, dbeta, part = sched.run(_ln_bwd, CARRY_US["ln_bwd"], dxn, z, full[kind + "_ln_g"][j], "loss_ln_bwd",
                                                 target=loss_target[0])
        else:
            dz, dzb, dg, dbeta = sched.run(_ln_bwd, CARRY_US["ln_bwd"], dxn, z, full[kind + "_ln_g"][j], "ln_bwd")
        gsum[kind + "_ln_g"][j] = dg.reshape(D)
        gsum[kind + "_ln_b"][j] = dbeta.reshape(D)
        chip_sum(sched.run(_mm_tn, CARRY_US["mm_dw_out"], mix3, dzb, 512, "mm_dw_out"), "w_out", f"out{layer}")
        dmix3 = sched.run(_mm_nt, CARRY_US["mm_dmix"], dzb, w_out[layer], 1024, 512, "mm_dmix", out3=True)
        if kind == "even":
            o, l, bsb = extra
            ws = full["even_a_ws"][j]
            dh, dws, dbs, dlng, dlnb, dsink = sched.run(
                _even_bwd, CARRY_US["even_bwd"], h, dmix3, o, l, rope, full["even_a_ln_g"][j], full["even_a_ln_b"][j],
                ws, jnp.swapaxes(ws, 1, 2), bsb, full["even_b_sinks"][j], "even_bwd")
            gsum["even_a_ws"][j] = dws
            gsum["even_a_bs"][j] = jnp.transpose(dbs[:, :8])
            gsum["even_a_ln_g"][j] = dlng.reshape(W)
            gsum["even_a_ln_b"][j] = dlnb.reshape(W)
            gsum["even_b_sinks"][j] = dsink[0, :16]
            if layer == 0:
                rep_rows = [_rep_pack(jnp.stack(gsum[nm]).reshape(wts[nm].shape)) for nm in REP]
                sh_rows = [_sh_pack(jnp.stack(gsum[nm]).reshape(shape), axis) for nm, shape, axis in SH]
                packed = _pad_rows(jnp.concatenate(rep_rows + sh_rows, axis=1))
                gw, (small8, parts) = _mm_tn(dh, xb, 384, "mm_dw_in_even", comm=_Join([_ExchangeAll(packed), _GatherAll(part)]))
                loss = jnp.sum(parts[:, 0, 0]) * (0.5 / D)
            else:
                gw = sched.run(_mm_tn, CARRY_US["mm_dw_in_even"], dh, xb, 384, "mm_dw_in_even")
            chip_sum(gw, "w_in_even", f"in{layer}")
            if layer == 0:
                n_rep = sum(p.shape[1] for p in rep_rows)
                red = _sum8(small8, 1 << 20, "sum_small")
                (rep_all,) = sched.flush("d_in0", FLUSH_EXTRA_US, beside=_GatherAll(_pad_rows(red[:n_rep])))
                sched.overhang = 0.6
            dxn = sched.run(_mm_nn_res, CARRY_US["mm_dx_even"], dh, wt_in[layer], dz, 512, 1024, "mm_dx_even")
        else:
            hst, wa, wx, wp = extra
            dh4, dcw, dcb, dwa, dwx, dba, dbx, dlam = sched.run(
                _odd_c_bwd, CARRY_US["odd_c_bwd"], h, hst, dmix3, full["odd_conv_w"][j], full["odd_conv_b"][j], wa, wx,
                jnp.swapaxes(wa, 1, 2), jnp.swapaxes(wx, 1, 2), full["odd_b_a"][j], full["odd_b_x"][j], full["odd_lam"][j],
                "odd_c_bwd")
            dh4, dwp, dds = _odd_d_bwd(h, dmix3, dh4, wp, jnp.swapaxes(wp, 1, 2), full["odd_d_scale"][j], "odd_d_bwd")
            gsum["odd_conv_w"][j], gsum["odd_conv_b"][j] = dcw, dcb.reshape(W)
            gsum["odd_w_a"][j], gsum["odd_w_x"][j] = dwa, dwx
            gsum["odd_b_a"][j], gsum["odd_b_x"][j], gsum["odd_lam"][j] = dba.reshape(W), dbx.reshape(W), dlam.reshape(W)
            gsum["odd_w_pool"][j], gsum["odd_d_scale"][j] = dwp, dds.reshape(W)
            chip_sum(sched.run(_mm_tn, CARRY_US["mm_dw_in_odd"], dh4, xb, 512, "mm_dw_in_odd"), "w_in_odd", f"in{layer}")
            dxn = sched.run(_mm_nn_res, CARRY_US["mm_dx_odd"], dh4, wt_in[layer], dz, 512, 1024, "mm_dx_odd")
    grad_x = dxn[None]

    out_g, out_d, out_m, out_v = {}, {}, {}, {}
    for nm, kind, what, layers in (("odd_w_out", "odd", "out", (1, 3)), ("even_w_out", "even", "out", (0, 2)),
                                   ("odd_w_in", "odd", "in", (1, 3)), ("even_w_in", "even", "in", (0, 2))):
        gl = [reduced(f"{what}{l}") for l in layers]
        if nm == "even_w_in":
            view = lambda a: jnp.transpose(a, (0, 2, 1))
            res, _ = _adamw(view(wts[nm]), gl, view(mom[nm]), view(var[nm]), 112, f"adamw_{nm}")
            res = [view(a) for a in res]
        elif what == "in":
            gs = [jnp.transpose(_rs_final(s4, r3, "rs_final_w_in_odd")) for s4, r3 in gl]
            res, _ = _adamw(wts[nm], gs, mom[nm], var[nm], 512, f"adamw_{nm}")
        else:
            res = sched.run(_adamw, CARRY_US["adamw_" + nm], wts[nm], gl, mom[nm], var[nm], 128, f"adamw_{nm}")
        out_d[nm], out_m[nm], out_v[nm], out_g[nm] = res

    g_small = {}
    off = 0
    for nm, p in zip(REP, rep_rows):
        r = p.shape[1]
        g_small[nm] = _rep_unpack(rep_all[:, off:off + r, :], wts[nm].shape)
        off += r
    off = n_rep
    for (nm, shape, axis), p in zip(SH, sh_rows):
        r = p.shape[1]
        g_small[nm] = red[off:off + r].reshape(wts[nm].shape)
        off += r

    def rows(a):
        f = a.reshape(-1)
        pad = (-f.shape[0]) % 128
        if pad:
            f = jnp.concatenate([f, jnp.zeros((pad,), a.dtype)])
        return f.reshape(-1, 128)

    small = REP + [nm for nm, _, _ in SH]
    each = lambda src: [rows(src[nm]) for nm in small]
    d2, m2, v2 = _adamw_many(each(wts), each(g_small), each(mom), each(var), "adamw_small")
    for i, nm in enumerate(small):
        n, shp = wts[nm].size, wts[nm].shape
        take = lambda a: a.reshape(-1)[:n].reshape(shp)
        out_g[nm], out_d[nm], out_m[nm], out_v[nm] = g_small[nm], take(d2[i]), take(m2[i]), take(v2[i])

    return (loss, grad_x, *[out_g[nm] for nm in NAMES], *[out_d[nm] for nm in NAMES],
            *[out_m[nm] for nm in NAMES], *[out_v[nm] for nm in NAMES])
```

```python
import functools

import jax
import jax.numpy as jnp
from jax import lax
from jax.experimental import pallas as pl
from jax.experimental.pallas import tpu as pltpu

F32 = jnp.float32
BF = jnp.bfloat16
MESH = pl.DeviceIdType.MESH
ANY = pl.BlockSpec(memory_space=pl.ANY)

N_DEV = 8
D = 2048
W = 1024
EVEN_IN = 5376
ODD_IN = 4096
CHUNK = 128
ALPHA = (2 * 4) ** 0.25
LN_EPS = 1e-5
ROPE_THETA = 500000.0
LRU_C = 8.0
LR, B1, B2, ADAM_EPS, WD, STEP = 0.001, 0.9, 0.999, 1e-08, 0.01, 10
NEG = -1e30
HEAD_COLS = 4


def _cp(vmem_mb=48, collective_id=None):
    return pltpu.CompilerParams(vmem_limit_bytes=vmem_mb * 1024 * 1024, collective_id=collective_id)


def _sig(x):
    return jax.nn.sigmoid(x)


def _silu_grad(x):
    s = _sig(x)
    return x * s, s * (1.0 + x * (1.0 - s))


def _dot(a, b):
    return jnp.dot(a, b, preferred_element_type=F32)


def _dot_nt(a, b):
    return lax.dot_general(a, b, (((1,), (1,)), ((), ())), preferred_element_type=F32)


def _dot_tn(a, b):
    return lax.dot_general(a, b, (((0,), (0,)), ((), ())), preferred_element_type=F32)


def _coords():
    return lax.axis_index("x"), lax.axis_index("y"), lax.axis_index("c")


def _chip(j):
    x, y, _ = _coords()
    return (1 - x if j & 2 else x), (1 - y if j & 1 else y)


X_NB, Y_NB, DIAG, SIB = 4, 2, 6, 1
EVERYONE = frozenset(range(1, N_DEV))
BARRIER_IDS = {}


class _Comm:
    def collective_id(self):
        return BARRIER_IDS.setdefault(frozenset(self.peers), len(BARRIER_IDS))

    def start(self, ins, outs, sems):
        barrier = pltpu.get_barrier_semaphore()
        for d in sorted(self.peers):
            pl.semaphore_signal(barrier, inc=1, device_id=_peer(d)[0], device_id_type=MESH)
        pl.semaphore_wait(barrier, len(self.peers))
        for cp in self.copies(ins, outs, sems):
            cp.start()

    def wait(self, ins, outs, sems):
        for cp in self.copies(ins, outs, sems):
            cp.wait()


class _Join(_Comm):
    def __init__(self, parts):
        self.parts = list(parts)
        self.peers = frozenset().union(*[p.peers for p in self.parts])
        self.inputs = [a for p in self.parts for a in p.inputs]
        self.out_shapes = [s for p in self.parts for s in p.out_shapes]
        self.sem_shapes = [s for p in self.parts for s in p.sem_shapes]
        self.aliases = {}
        i0 = o0 = 0
        for p in self.parts:
            for i, o in getattr(p, "aliases", {}).items():
                self.aliases[i0 + i] = o0 + o
            i0, o0 = i0 + len(p.inputs), o0 + len(p.out_shapes)

    def copies(self, ins, outs, sems):
        res = []
        i0 = o0 = s0 = 0
        for p in self.parts:
            ni, no, ns = len(p.inputs), len(p.out_shapes), len(p.sem_shapes)
            res += p.copies(ins[i0:i0 + ni], outs[o0:o0 + no], sems[s0:s0 + ns])
            i0, o0, s0 = i0 + ni, o0 + no, s0 + ns
        return res


ROWS_US = {"ag1": 0.104, "ag2": 0.052, "agd": 0.027, "rsd": 0.027, "rs": 0.205}
N_COPIES = {"ag1": 2, "ag2": 2, "agd": 4, "rsd": 4, "rs": 3}
TASK_PEERS = {"ag1": {X_NB, Y_NB}, "ag2": {X_NB, Y_NB}, "agd": {SIB}, "rsd": {SIB}, "rs": {X_NB, Y_NB, DIAG}}
ROW_CHUNK = {672: 224, 512: 128, 256: 128}
CARRY_US = {"mm_h_even": 58, "mm_h_odd": 47, "even_fwd": 42, "odd_c_fwd": 37, "mm_out_ln": 33, "ln_bwd": 23, "mm_dmix": 26,
            "mm_dw_out": 25, "even_bwd": 90, "odd_c_bwd": 58, "mm_dw_in_even": 58, "mm_dw_in_odd": 44, "mm_dx_even": 58,
            "mm_dx_odd": 47, "adamw_even_w_out": 11, "adamw_odd_w_out": 11}
FWD_OVERBOOK = 1.15
FLUSH_EXTRA_US = 60.0


def _cost_us(task, reg):
    kind, src, _, lo, hi = task
    return ROWS_US[kind] * (hi - lo) * reg[src].shape[-1] * reg[src].dtype.itemsize / 4096.0


class _Copies(_Comm):
    def __init__(self, tasks, reg):
        self.tasks = list(tasks)
        self.out_names, self.in_names = [], []
        for kind, src, dst, lo, hi in self.tasks:
            if dst not in self.out_names:
                self.out_names.append(dst)
        for kind, src, dst, lo, hi in self.tasks:
            if src not in self.out_names and src not in self.in_names:
                self.in_names.append(src)
        self.out_shapes, self.aliases = [], {}
        for o, dst in enumerate(self.out_names):
            if dst in reg:
                self.aliases[len(self.in_names)] = o
                self.in_names.append(dst)
                self.out_shapes.append(jax.ShapeDtypeStruct(reg[dst].shape, reg[dst].dtype))
            else:
                kind, src = next((t[0], t[1]) for t in self.tasks if t[2] == dst)
                shape = ({"rsd": 4, "rs": 3}[kind],) + reg[src].shape[1:]
                self.out_shapes.append(jax.ShapeDtypeStruct(shape, reg[src].dtype))
        self.inputs = [reg[nm] for nm in self.in_names]
        n = sum(N_COPIES[t[0]] for t in self.tasks)
        self.sem_shapes = [pltpu.SemaphoreType.DMA((n,)), pltpu.SemaphoreType.DMA((n,))]
        self.peers = frozenset().union(*[TASK_PEERS[t[0]] for t in self.tasks])

    def copies(self, ins, outs, sems):
        send, recv = sems
        x, y, c = _coords()
        me = 4 * x + 2 * y + c
        xn, yn = (1 - x, y, c), (x, 1 - y, c)
        at_xn, at_yn = 4 * (1 - x) + 2 * y + c, 4 * x + 2 * (1 - y) + c
        ref = dict(zip(self.in_names, ins))
        ref.update(zip(self.out_names, outs))
        res = []

        def copy(src, dst, to):
            i = len(res)
            res.append(pltpu.make_async_remote_copy(src_ref=src, dst_ref=dst, send_sem=send.at[i], recv_sem=recv.at[i],
                                                    device_id=to, device_id_type=MESH))

        for kind, src, dst, lo, hi in self.tasks:
            n = hi - lo
            if kind == "ag1":
                for to in (xn, yn):
                    copy(ref[src].at[pl.ds(lo, n)], ref[dst].at[me, pl.ds(lo, n)], to)
            elif kind == "ag2":
                h = n // 2
                first, second = ref[dst].at[at_xn, pl.ds(lo, h)], ref[dst].at[at_yn, pl.ds(lo + h, n - h)]
                copy(first, first, yn)
                copy(second, second, xn)
            elif kind == "agd":
                for j in range(4):
                    px, py = _chip(j)
                    rows = ref[dst].at[4 * px + 2 * py + c, pl.ds(lo, n)]
                    copy(rows, rows, (x, y, 1 - c))
            elif kind == "rsd":
                for j in range(4):
                    px, py = _chip(j)
                    copy(ref[src].at[4 * px + 2 * py + 1 - c, pl.ds(lo, n)], ref[dst].at[j, pl.ds(lo, n)], (x, y, 1 - c))
            else:
                for j in (1, 2, 3):
                    px, py = _chip(j)
                    copy(ref[src].at[j, pl.ds(lo, n)], ref[dst].at[j - 1, pl.ds(lo, n)], (px, py, c))
        return res


class _Sched:
    def __init__(self, reg):
        self.reg, self.queue, self.later = reg, [], []
        self.overhang = 0.5
        self.after_landing = None

    def add(self, tasks, first=False):
        self.queue = list(tasks) + self.queue if first else self.queue + list(tasks)

    def pending(self, dst):
        return any(t[2] == dst for t in self.queue + self.later)

    def take(self, budget_us, must=None, overhang=0.5):
        self.queue, self.later = self.later + self.queue, []
        picked, us = [], 0.0
        rest = []
        for t in self.queue:
            cost = _cost_us(t, self.reg)
            if (must is not None and t[2] == must) or us + (1.0 - overhang) * cost <= budget_us:
                picked.append(t)
                us += cost
                if t[0] in ("ag1", "ag2"):
                    self.later.append(({"ag1": "ag2", "ag2": "agd"}[t[0]], t[2], t[2], t[3], t[4]))
            else:
                rest.append(t)
        self.queue = rest
        return _Copies(picked, self.reg) if picked else None

    def landed(self, comm, got):
        if comm is not None:
            for nm, a in zip(comm.out_names, got):
                self.reg[nm] = a
        if self.after_landing is not None:
            self.after_landing()

    def run(self, builder, budget_us, *args, **kw):
        comm = self.take(budget_us, overhang=self.overhang)
        res, got = builder(*args, comm=comm, **kw)
        self.landed(comm, got)
        return res

    def flush(self, dst, budget_us=0.0, beside=None):
        res = []
        while self.pending(dst):
            comm = self.take(budget_us, must=dst)
            got = _comm_only(comm if beside is None else _Join([comm, beside]), "flush_" + dst)
            res, beside = got[len(comm.out_shapes):], None
            self.landed(comm, got[:len(comm.out_shapes)])
        return res


def _rows(name_src, name_dst, kind, n_rows, chunk):
    return [(kind, name_src, name_dst, lo, min(lo + chunk, n_rows)) for lo in range(0, n_rows, chunk)]


def _pcall(body, *, grid, in_specs, out_specs, out_shape, name, scratch=(), vmem=48, comm=None):
    in_specs, out_specs, out_shape, scratch = list(in_specs), list(out_specs), list(out_shape), list(scratch)
    if comm is None:
        call = pl.pallas_call(body, grid=grid, in_specs=in_specs, out_specs=out_specs, out_shape=out_shape,
                              scratch_shapes=scratch, name=name, compiler_params=_cp(vmem))
        return lambda *args: (call(*args), [])
    n_in, n_out, n_scr = len(in_specs), len(out_specs), len(scratch)
    c_in, c_out = len(comm.inputs), len(comm.out_shapes)
    aliases = {n_in + i: n_out + o for i, o in getattr(comm, "aliases", {}).items()}

    def wrapped(*refs):
        ins, cins = refs[:n_in], refs[n_in:n_in + c_in]
        o0 = n_in + c_in
        outs, couts = refs[o0:o0 + n_out], refs[o0 + n_out:o0 + n_out + c_out]
        s0 = o0 + n_out + c_out
        scr, sems = refs[s0:s0 + n_scr], refs[s0 + n_scr:]
        ids = [pl.program_id(a) for a in range(len(grid))]
        first = functools.reduce(jnp.logical_and, [i == 0 for i in ids])
        last = functools.reduce(jnp.logical_and, [i == g - 1 for i, g in zip(ids, grid)])

        @pl.when(first)
        def _():
            comm.start(cins, couts, sems)

        body(*ins, *outs, *scr)

        @pl.when(last)
        def _():
            comm.wait(cins, couts, sems)

    call = pl.pallas_call(wrapped, grid=grid, in_specs=in_specs + [ANY] * c_in, out_specs=out_specs + [ANY] * c_out,
                          out_shape=out_shape + list(comm.out_shapes), scratch_shapes=scratch + list(comm.sem_shapes),
                          input_output_aliases=aliases, name=name, compiler_params=_cp(vmem, comm.collective_id()))

    def run(*args):
        res = call(*args, *comm.inputs)
        return res[:n_out], res[n_out:]

    return run


def _comm_only(comm, name):
    c_in, c_out = len(comm.inputs), len(comm.out_shapes)

    def body(*refs):
        cins, couts, sems = refs[:c_in], refs[c_in:c_in + c_out], refs[c_in + c_out:]
        comm.start(cins, couts, sems)
        comm.wait(cins, couts, sems)

    return pl.pallas_call(body, in_specs=[ANY] * c_in, out_specs=[ANY] * c_out, out_shape=list(comm.out_shapes),
                          scratch_shapes=list(comm.sem_shapes), input_output_aliases=dict(getattr(comm, "aliases", {})),
                          name=name, compiler_params=pltpu.CompilerParams(collective_id=comm.collective_id()))(*comm.inputs)


def _chip_blocks():
    _, _, c = _coords()
    return jnp.stack([4 * px + 2 * py + c for px, py in map(_chip, range(4))]).astype(jnp.int32)


def _add_pairs(g8, b4, name):
    _, R, C = b4.shape

    def body(idx_ref, a_ref, b_ref, o_ref):
        o_ref[...] = (a_ref[...].astype(F32) + b_ref[...].astype(F32)).astype(BF)

    blk = pl.BlockSpec((None, R, C), lambda j, idx: (j, 0, 0))
    grid_spec = pltpu.PrefetchScalarGridSpec(
        num_scalar_prefetch=1, grid=(4,),
        in_specs=[pl.BlockSpec((None, R, C), lambda j, idx: (idx[j], 0, 0)), blk], out_specs=blk)
    return pl.pallas_call(body, grid_spec=grid_spec, out_shape=jax.ShapeDtypeStruct(b4.shape, BF), name=name,
                          compiler_params=_cp())(_chip_blocks(), g8, b4)


def _rs_final(s4, r3, name):
    _, R, C = s4.shape
    tr = R // 2

    def body(s_ref, r_ref, o_ref):
        o_ref[...] = ((s_ref[...].astype(F32) + r_ref[0].astype(F32)) + r_ref[1].astype(F32)) + r_ref[2].astype(F32)

    return pl.pallas_call(
        body, grid=(2,),
        in_specs=[pl.BlockSpec((None, tr, C), lambda i: (0, i, 0)), pl.BlockSpec((3, tr, C), lambda i: (0, i, 0))],
        out_specs=pl.BlockSpec((tr, C), lambda i: (i, 0)), out_shape=jax.ShapeDtypeStruct((R, C), F32),
        name=name, compiler_params=_cp())(s4, r3)


def _mm_nt(a, w, tm, tn, name, out3=False, comm=None):
    M, K = a.shape
    N = w.shape[0]
    tm = min(tm, M)

    def body(a_ref, w_ref, o_ref):
        o_ref[...] = _dot_nt(a_ref[...], w_ref[...])

    if out3:
        per = W // tn
        out_shape = jax.ShapeDtypeStruct((N // W, M, W), F32)
        out_spec = pl.BlockSpec((None, tm, tn), lambda i, j: (j // per, i, j % per))
    else:
        out_shape = jax.ShapeDtypeStruct((M, N), F32)
        out_spec = pl.BlockSpec((tm, tn), lambda i, j: (i, j))
    (res,), extra = _pcall(
        body, grid=(M // tm, N // tn),
        in_specs=[pl.BlockSpec((tm, K), lambda i, j: (i, 0)), pl.BlockSpec((tn, K), lambda i, j: (j, 0))],
        out_specs=[out_spec], out_shape=[out_shape], name=name, comm=comm)(a, w)
    return res, extra


def _mm_tn(a, b, tm, name, comm=None):
    K, N = b.shape
    if a.ndim == 3:
        M = a.shape[0] * W
        per = W // tm
        a_spec = pl.BlockSpec((None, K, tm), lambda i: (i // per, 0, i % per))
    else:
        M = a.shape[1]
        a_spec = pl.BlockSpec((K, tm), lambda i: (0, i))

    def body(a_ref, b_ref, o_ref):
        o_ref[...] = _dot_tn(a_ref[...], b_ref[...]).astype(BF)

    (out,), extra = _pcall(
        body, grid=(M // tm,),
        in_specs=[a_spec, pl.BlockSpec((K, N), lambda i: (0, 0))],
        out_specs=[pl.BlockSpec((tm, N), lambda i: (i, 0))],
        out_shape=[jax.ShapeDtypeStruct((M, N), BF)], name=name, vmem=56, comm=comm)(a, b)
    return out, extra


def _mm_nn_res(a, w, res, tm, tn, name, comm=None):
    K, N = w.shape
    if a.ndim == 3:
        P, M = a.shape[0], a.shape[1]
        tm = min(tm, M)
        a_spec = pl.BlockSpec((P, tm, W), lambda j, i: (0, i, 0))
    else:
        P, M = 0, a.shape[0]
        tm = min(tm, M)
        a_spec = pl.BlockSpec((tm, K), lambda j, i: (i, 0))

    def body(a_ref, w_ref, r_ref, o_ref):
        if P:
            d = _dot(a_ref[0], w_ref[0:W, :])
            for p in range(1, P):
                d = d + _dot(a_ref[p], w_ref[p * W:(p + 1) * W, :])
        else:
            d = _dot(a_ref[...], w_ref[...])
        o_ref[...] = ALPHA * r_ref[...] + d

    (out,), extra = _pcall(
        body, grid=(N // tn, M // tm),
        in_specs=[a_spec, pl.BlockSpec((K, tn), lambda j, i: (0, j)), pl.BlockSpec((tm, tn), lambda j, i: (i, j))],
        out_specs=[pl.BlockSpec((tm, tn), lambda j, i: (i, j))],
        out_shape=[jax.ShapeDtypeStruct((M, N), F32)], name=name, comm=comm)(a, w, res)
    return out, extra


def _mm_out_ln(mix3, w_out, x, g, b, name, comm=None):
    S = x.shape[0]
    tm = min(256, S)

    def body(m_ref, w_ref, x_ref, g_ref, b_ref, z_ref, xn_ref, xb_ref):
        acc = _dot(m_ref[0], w_ref[0:W, :]) + _dot(m_ref[1], w_ref[W:2 * W, :])
        z = ALPHA * x_ref[...] + acc
        mu = jnp.mean(z, axis=1, keepdims=True)
        zc = z - mu
        var = jnp.mean(zc * zc, axis=1, keepdims=True)
        xn = zc * lax.rsqrt(var + LN_EPS) * g_ref[...] + b_ref[...]
        z_ref[...] = z
        xn_ref[...] = xn
        xb_ref[...] = xn.astype(BF)

    row = pl.BlockSpec((tm, D), lambda i: (i, 0))
    vec = pl.BlockSpec((1, D), lambda i: (0, 0))
    return _pcall(
        body, grid=(S // tm,),
        in_specs=[pl.BlockSpec((2, tm, W), lambda i: (0, i, 0)), pl.BlockSpec((D, D), lambda i: (0, 0)), row, vec, vec],
        out_specs=[row, row, row],
        out_shape=[jax.ShapeDtypeStruct((S, D), F32), jax.ShapeDtypeStruct((S, D), F32), jax.ShapeDtypeStruct((S, D), BF)],
        name=name, comm=comm)(mix3, w_out, x, g.reshape(1, D), b.reshape(1, D))


def _ln_bwd(dxn, z, g, name, comm=None, target=None):
    S = z.shape[0]
    tm = min(256, S)
    head = target is not None

    def body(*refs):
        if head:
            d_ref, t_ref, z_ref, g_ref, dz_ref, dzb_ref, dg_ref, db_ref, p_ref = refs
        else:
            d_ref, z_ref, g_ref, dz_ref, dzb_ref, dg_ref, db_ref = refs
        i = pl.program_id(0)
        zz = z_ref[...]
        mu = jnp.mean(zz, axis=1, keepdims=True)
        zc = zz - mu
        var = jnp.mean(zc * zc, axis=1, keepdims=True)
        rstd = lax.rsqrt(var + LN_EPS)
        xhat = zc * rstd
        dy = d_ref[...]
        if head:
            e = dy - t_ref[...]
            dy = e * (1.0 / D)

            @pl.when(i == 0)
            def _():
                p_ref[...] = jnp.zeros_like(p_ref)

            p_ref[...] += jnp.sum(jnp.sum(e * e, axis=1, keepdims=True), axis=0, keepdims=True)
        dyg = dy * g_ref[...]
        m1 = jnp.mean(dyg, axis=1, keepdims=True)
        m2 = jnp.mean(dyg * xhat, axis=1, keepdims=True)
        dz = rstd * (dyg - m1 - xhat * m2)
        dz_ref[...] = dz
        dzb_ref[...] = dz.astype(BF)

        @pl.when(i == 0)
        def _():
            dg_ref[...] = jnp.zeros_like(dg_ref)
            db_ref[...] = jnp.zeros_like(db_ref)

        dg_ref[...] += jnp.sum(dy * xhat, axis=0, keepdims=True)
        db_ref[...] += jnp.sum(dy, axis=0, keepdims=True)

    row = pl.BlockSpec((tm, D), lambda i: (i, 0))
    vec = pl.BlockSpec((1, D), lambda i: (0, 0))
    out_specs = [row, row, vec, vec] + ([pl.BlockSpec((8, 128), lambda i: (0, 0))] if head else [])
    out_shape = [jax.ShapeDtypeStruct((S, D), F32), jax.ShapeDtypeStruct((S, D), BF), jax.ShapeDtypeStruct((1, D), F32),
                 jax.ShapeDtypeStruct((1, D), F32)] + ([jax.ShapeDtypeStruct((8, 128), F32)] if head else [])
    operands = (dxn, target, z, g.reshape(1, D)) if head else (dxn, z, g.reshape(1, D))
    return _pcall(body, grid=(S // tm,), in_specs=[row] * (len(operands) - 1) + [vec], out_specs=out_specs,
                  out_shape=out_shape, name=name, comm=comm)(*operands)


def _rope_fwd(t, r_ref):
    return (t * r_ref[:, 0:128] + pltpu.roll(t, 120, 1) * r_ref[:, 128:256]
            + pltpu.roll(t, 8, 1) * r_ref[:, 256:384])


def _rope_bwd(g, r_ref):
    return (g * r_ref[:, 0:128] + pltpu.roll(g * r_ref[:, 128:256], 8, 1)
            + pltpu.roll(g * r_ref[:, 256:384], 120, 1))


def _dup_heads(kb):
    lo = lax.broadcasted_iota(jnp.int32, kb.shape, 1) < 64
    sw = pltpu.roll(kb, 64, 1)
    return [jnp.where(lo, kb, sw).astype(BF), jnp.where(lo, sw, kb).astype(BF)]


def _even_fwd(h, rope, lng, lnb, ws, bsb, sinks, name, comm=None):
    S = h.shape[0]
    nb = S // CHUNK

    def body(h_ref, hp_ref, rc_ref, rp_ref, lng_ref, lnb_ref, ws_ref, bsb_ref, sink_ref, mix_ref, o_ref, l_ref):
        n = pl.program_id(0)
        lane = lax.broadcasted_iota(jnp.int32, (128, 128), 1)
        rowi = lax.broadcasted_iota(jnp.int32, (128, 128), 0)
        tri = rowi >= lane
        lane_lo = lane < 64
        v = h_ref[:, W:2 * W]
        mu = jnp.mean(v, axis=1, keepdims=True)
        vc = v - mu
        var = jnp.mean(vc * vc, axis=1, keepdims=True)
        vn = vc * lax.rsqrt(var + LN_EPS) * lng_ref[...] + lnb_ref[...]
        ms = [_dot(jnp.where(tri, ws_ref[g], 0.0).astype(BF), vn[:, g * 128:(g + 1) * 128].astype(BF)) for g in range(8)]
        for g in range(8):
            sl = slice(g * 128, (g + 1) * 128)
            ag = h_ref[:, 2 * W + g * 128:2 * W + (g + 1) * 128]
            mix_ref[0, :, sl] = (h_ref[:, sl] * (ms[g] + bsb_ref[g]) * (ag * _sig(ag))).astype(BF)
        kb = jnp.concatenate([_rope_fwd(hp_ref[:, 0:128], rp_ref), _rope_fwd(h_ref[:, 4096:4224], rc_ref)], axis=0)
        vb = jnp.concatenate([hp_ref[:, 128:256], h_ref[:, 4224:4352]], axis=0)
        k2 = _dup_heads(kb)
        v2 = _dup_heads(vb)
        qi = lax.broadcasted_iota(jnp.int32, (128, 256), 0)
        kj = lax.broadcasted_iota(jnp.int32, (128, 256), 1)
        diff = qi + 128 - kj
        valid = (diff >= 0) & (diff < 128) & ((n > 0) | (kj >= 128))
        lacc = jnp.zeros((128, 128), F32)
        for j0 in range(0, 8, HEAD_COLS):
            heads = [(j, half) for j in range(j0, j0 + HEAD_COLS) for half in range(2)]
            sc, pr, oh = {}, {}, {}
            for j in range(j0, j0 + HEAD_COLS):
                qc = _rope_fwd(h_ref[:, 3072 + j * 128:3072 + (j + 1) * 128], rc_ref)
                sc[j, 0] = _dot_nt(jnp.where(lane_lo, qc, 0.0).astype(BF), k2[j // 4])
                sc[j, 1] = _dot_nt(jnp.where(lane_lo, 0.0, qc).astype(BF), k2[j // 4])
            for j, half in heads:
                hq = 2 * j + half
                s = jnp.where(valid, sc[j, half] * 0.125, NEG)
                sk = sink_ref[hq]
                mx = jnp.maximum(jnp.max(s, axis=1, keepdims=True), sk)
                p = jnp.exp(s - mx)
                den = jnp.sum(p, axis=1, keepdims=True) + jnp.exp(sk - mx)
                pr[j, half] = (p / den).astype(BF)
                lacc = jnp.where(lane == hq, mx + jnp.log(den), lacc)
            for j, half in heads:
                oh[j, half] = _dot(pr[j, half], v2[j // 4])
            for j in range(j0, j0 + HEAD_COLS):
                cs = slice(j * 128, (j + 1) * 128)
                ocol = jnp.where(lane_lo, oh[j, 0], oh[j, 1])
                bg = h_ref[:, 4352 + j * 128:4352 + (j + 1) * 128]
                o_ref[:, cs] = ocol
                mix_ref[1, :, cs] = (ocol * (bg * _sig(bg))).astype(BF)
        l_ref[...] = lacc

    prev = lambda n: jnp.maximum(n - 1, 0)
    full = lambda shape: pl.BlockSpec(shape, lambda n: (0,) * len(shape))
    return _pcall(
        body, grid=(nb,),
        in_specs=[pl.BlockSpec((CHUNK, EVEN_IN), lambda n: (n, 0)),
                  pl.BlockSpec((CHUNK, 256), lambda n: (prev(n), 16)),
                  pl.BlockSpec((CHUNK, 384), lambda n: (n, 0)),
                  pl.BlockSpec((CHUNK, 384), lambda n: (prev(n), 0)),
                  full((1, W)), full((1, W)), full((8, 128, 128)), full((8, 128, 128)),
                  pl.BlockSpec(memory_space=pltpu.SMEM)],
        out_specs=[pl.BlockSpec((2, CHUNK, W), lambda n: (0, n, 0)),
                   pl.BlockSpec((CHUNK, W), lambda n: (n, 0)),
                   pl.BlockSpec((CHUNK, 128), lambda n: (n, 0))],
        out_shape=[jax.ShapeDtypeStruct((2, S, W), BF), jax.ShapeDtypeStruct((S, W), F32),
                   jax.ShapeDtypeStruct((S, 128), F32)],
        name=name, comm=comm)(h, h, rope, rope, lng.reshape(1, W), lnb.reshape(1, W), ws, bsb, sinks)


def _even_bwd(h, dmix3, o, l, rope, lng, lnb, ws, wst, bsb, sinks, name, comm=None):
    S = h.shape[0]
    nb = S // CHUNK

    def body(h_ref, hp_ref, hn_ref, dm_ref, dmn_ref, o_ref, on_ref, l_ref, ln_ref, rc_ref, rp_ref, rn_ref,
             lng_ref, lnb_ref, ws_ref, wst_ref, bsb_ref, sink_ref,
             dh_ref, dws_ref, dbs_ref, dlng_ref, dlnb_ref, dsink_ref, dvn_ref):
        n = pl.program_id(0)

        @pl.when(n == 0)
        def _():
            dws_ref[...] = jnp.zeros_like(dws_ref)
            dbs_ref[...] = jnp.zeros_like(dbs_ref)
            dlng_ref[...] = jnp.zeros_like(dlng_ref)
            dlnb_ref[...] = jnp.zeros_like(dlnb_ref)
            dsink_ref[...] = jnp.zeros_like(dsink_ref)

        lane = lax.broadcasted_iota(jnp.int32, (128, 128), 1)
        rowi = lax.broadcasted_iota(jnp.int32, (128, 128), 0)
        lane1 = lax.broadcasted_iota(jnp.int32, (1, 128), 1)
        tri = rowi >= lane
        tri_t = lane >= rowi
        lane_lo = lane < 64
        v = h_ref[:, W:2 * W]
        mu = jnp.mean(v, axis=1, keepdims=True)
        vc = v - mu
        var = jnp.mean(vc * vc, axis=1, keepdims=True)
        rstd = lax.rsqrt(var + LN_EPS)
        vhat = vc * rstd
        vn = vhat * lng_ref[...] + lnb_ref[...]
        dbs_acc = jnp.zeros((128, 128), F32)
        vng = [vn[:, g * 128:(g + 1) * 128].astype(BF) for g in range(8)]
        ms = [_dot(jnp.where(tri, ws_ref[g], 0.0).astype(BF), vng[g]) for g in range(8)]
        dmb = []
        for g in range(8):
            sl = slice(g * 128, (g + 1) * 128)
            m = ms[g] + bsb_ref[g]
            ag = h_ref[:, 2 * W + g * 128:2 * W + (g + 1) * 128]
            sg, dsg = _silu_grad(ag)
            u = h_ref[:, sl]
            da = dm_ref[0, :, sl]
            dmm = da * u * sg
            dh_ref[:, sl] = (da * m * sg).astype(BF)
            dh_ref[:, 2 * W + g * 128:2 * W + (g + 1) * 128] = (da * u * m * dsg).astype(BF)
            dmb.append(dmm.astype(BF))
            dbs_acc = jnp.where(lane == g, jnp.sum(dmm, axis=1, keepdims=True), dbs_acc)
        dvs = [_dot(jnp.where(tri_t, wst_ref[g], 0.0).astype(BF), dmb[g]) for g in range(8)]
        dwss = [_dot_nt(dmb[g], vng[g]) for g in range(8)]
        for g in range(8):
            dvn_ref[:, g * 128:(g + 1) * 128] = dvs[g]
            dws_ref[g] += jnp.where(tri, dwss[g], 0.0)
        dbs_ref[...] += dbs_acc
        dvn = dvn_ref[...]
        dlng_ref[...] += jnp.sum(dvn * vhat, axis=0, keepdims=True)
        dlnb_ref[...] += jnp.sum(dvn, axis=0, keepdims=True)
        dyg = dvn * lng_ref[...]
        m1 = jnp.mean(dyg, axis=1, keepdims=True)
        m2 = jnp.mean(dyg * vhat, axis=1, keepdims=True)
        dh_ref[:, W:2 * W] = (rstd * (dyg - m1 - vhat * m2)).astype(BF)
        kcur = _rope_fwd(h_ref[:, 4096:4224], rc_ref)
        kb = jnp.concatenate([_rope_fwd(hp_ref[:, 0:128], rp_ref), kcur], axis=0)
        vb = jnp.concatenate([hp_ref[:, 128:256], h_ref[:, 4224:4352]], axis=0)
        k2 = _dup_heads(kb)
        v2 = _dup_heads(vb)
        kc2 = _dup_heads(kcur)
        vc2 = _dup_heads(h_ref[:, 4224:4352])
        qi = lax.broadcasted_iota(jnp.int32, (128, 256), 0)
        kj = lax.broadcasted_iota(jnp.int32, (128, 256), 1)
        diff = qi + 128 - kj
        valid = (diff >= 0) & (diff < 128) & ((n > 0) | (kj >= 128))
        validn = (lane > rowi) & (n < nb - 1)
        lc = l_ref[...]
        lnx = ln_ref[...]
        dk = [jnp.zeros((128, 128), F32), jnp.zeros((128, 128), F32)]
        dv = [jnp.zeros((128, 128), F32), jnp.zeros((128, 128), F32)]
        dsk_acc = jnp.zeros((1, 128), F32)
        for j0 in range(0, 8, HEAD_COLS):
            heads = [(j, half) for j in range(j0, j0 + HEAD_COLS) for half in range(2)]
            t = {}
            for j in range(j0, j0 + HEAD_COLS):
                cs = slice(j * 128, (j + 1) * 128)
                qc = _rope_fwd(h_ref[:, 3072 + j * 128:3072 + (j + 1) * 128], rc_ref)
                qn = _rope_fwd(hn_ref[:, 3072 + j * 128:3072 + (j + 1) * 128], rn_ref)
                bg = h_ref[:, 4352 + j * 128:4352 + (j + 1) * 128]
                sgb, dsgb = _silu_grad(bg)
                db = dm_ref[1, :, cs]
                oc = o_ref[:, cs]
                do = db * sgb
                dh_ref[:, 4352 + j * 128:4352 + (j + 1) * 128] = (db * oc * dsgb).astype(BF)
                bgn = hn_ref[:, 4352 + j * 128:4352 + (j + 1) * 128]
                don = dmn_ref[1, :, cs] * (bgn * _sig(bgn))
                prod = do * oc
                prodn = don * on_ref[:, cs]
                for half in range(2):
                    hq = 2 * j + half
                    hm = lane_lo if half == 0 else jnp.logical_not(lane_lo)
                    t[j, half] = dict(
                        dsum=jnp.sum(jnp.where(hm, prod, 0.0), axis=1, keepdims=True),
                        dsumn=jnp.sum(jnp.where(hm, prodn, 0.0), axis=1, keepdims=True),
                        lh=jnp.sum(jnp.where(lane == hq, lc, 0.0), axis=1, keepdims=True),
                        lhn=jnp.sum(jnp.where(lane == hq, lnx, 0.0), axis=1, keepdims=True),
                        qm=jnp.where(hm, qc, 0.0).astype(BF), dom=jnp.where(hm, do, 0.0).astype(BF),
                        qnm=jnp.where(hm, qn, 0.0).astype(BF), donm=jnp.where(hm, don, 0.0).astype(BF))
            for j, half in heads:
                e, hk = t[j, half], j // 4
                e["s"], e["dp"] = _dot_nt(e["qm"], k2[hk]), _dot_nt(e["dom"], v2[hk])
                e["sn"], e["dpn"] = _dot_nt(e["qnm"], kc2[hk]), _dot_nt(e["donm"], vc2[hk])
            for j, half in heads:
                e, hq = t[j, half], 2 * j + half
                p = jnp.exp(jnp.where(valid, e["s"] * 0.125 - e["lh"], NEG))
                ds = p * (e["dp"] - e["dsum"])
                pn = jnp.exp(jnp.where(validn, e["sn"] * 0.125 - e["lhn"], NEG))
                dsn = pn * (e["dpn"] - e["dsumn"])
                psink = jnp.exp(sink_ref[hq] - e["lh"])
                dsk_acc = jnp.where(lane1 == hq, -jnp.sum(psink * e["dsum"], axis=0, keepdims=True), dsk_acc)
                e["ds"] = ds.astype(BF)
                e["pt"], e["dst"] = jnp.transpose(p[:, 128:256]).astype(BF), jnp.transpose(ds[:, 128:256]).astype(BF)
                e["pnt"], e["dsnt"] = jnp.transpose(pn).astype(BF), jnp.transpose(dsn).astype(BF)
            for j, half in heads:
                e, hk = t[j, half], j // 4
                e["dq"] = _dot(e["ds"], k2[hk])
                e["dv"] = _dot(e["pt"], e["dom"]) + _dot(e["pnt"], e["donm"])
                e["dk"] = _dot(e["dst"], e["qm"]) + _dot(e["dsnt"], e["qnm"])
            for j in range(j0, j0 + HEAD_COLS):
                hk = j // 4
                dqcol = jnp.where(lane_lo, t[j, 0]["dq"], t[j, 1]["dq"]) * 0.125
                dh_ref[:, 3072 + j * 128:3072 + (j + 1) * 128] = _rope_bwd(dqcol, rc_ref).astype(BF)
                dv[hk] = dv[hk] + t[j, 0]["dv"] + t[j, 1]["dv"]
                dk[hk] = dk[hk] + (t[j, 0]["dk"] + t[j, 1]["dk"]) * 0.125
        fold = lambda a: a + pltpu.roll(a, 64, 1)
        dh_ref[:, 4096:4224] = _rope_bwd(jnp.where(lane_lo, fold(dk[0]), fold(dk[1])), rc_ref).astype(BF)
        dh_ref[:, 4224:4352] = jnp.where(lane_lo, fold(dv[0]), fold(dv[1])).astype(BF)
        dsink_ref[...] += dsk_acc

    prev = lambda n: jnp.maximum(n - 1, 0)
    nxt = lambda n: jnp.minimum(n + 1, nb - 1)
    full = lambda shape: pl.BlockSpec(shape, lambda n: (0,) * len(shape))
    return _pcall(
        body, grid=(nb,),
        in_specs=[pl.BlockSpec((CHUNK, EVEN_IN), lambda n: (n, 0)),
                  pl.BlockSpec((CHUNK, 256), lambda n: (prev(n), 16)),
                  pl.BlockSpec((CHUNK, EVEN_IN), lambda n: (nxt(n), 0)),
                  pl.BlockSpec((2, CHUNK, W), lambda n: (0, n, 0)),
                  pl.BlockSpec((2, CHUNK, W), lambda n: (0, nxt(n), 0)),
                  pl.BlockSpec((CHUNK, W), lambda n: (n, 0)),
                  pl.BlockSpec((CHUNK, W), lambda n: (nxt(n), 0)),
                  pl.BlockSpec((CHUNK, 128), lambda n: (n, 0)),
                  pl.BlockSpec((CHUNK, 128), lambda n: (nxt(n), 0)),
                  pl.BlockSpec((CHUNK, 384), lambda n: (n, 0)),
                  pl.BlockSpec((CHUNK, 384), lambda n: (prev(n), 0)),
                  pl.BlockSpec((CHUNK, 384), lambda n: (nxt(n), 0)),
                  full((1, W)), full((1, W)), full((8, 128, 128)), full((8, 128, 128)), full((8, 128, 128)),
                  pl.BlockSpec(memory_space=pltpu.SMEM)],
        out_specs=[pl.BlockSpec((CHUNK, EVEN_IN), lambda n: (n, 0)),
                   full((8, 128, 128)), full((128, 128)), full((1, W)), full((1, W)), full((1, 128))],
        out_shape=[jax.ShapeDtypeStruct((S, EVEN_IN), BF), jax.ShapeDtypeStruct((8, 128, 128), F32),
                   jax.ShapeDtypeStruct((128, 128), F32), jax.ShapeDtypeStruct((1, W), F32),
                   jax.ShapeDtypeStruct((1, W), F32), jax.ShapeDtypeStruct((1, 128), F32)],
        scratch=[pltpu.VMEM((CHUNK, W), F32)], name=name, comm=comm,
    )(h, h, h, dmix3, dmix3, o, o, l, l, rope, rope, rope, lng.reshape(1, W), lnb.reshape(1, W), ws, wst, bsb, sinks)


def _expm1(x):
    ser = x * (1.0 + x * (0.5 + x * (1.0 / 6.0 + x * (1.0 / 24.0))))
    return jnp.where(jnp.abs(x) < 1e-2, ser, jnp.exp(x) - 1.0)


def _softplus_neg(lam):
    z = -lam
    e = jnp.exp(-jnp.abs(z))
    l1p = jnp.where(e < 1e-3, e * (1.0 - e * (0.5 - e * (1.0 / 3.0))), jnp.log(1.0 + e))
    return jnp.maximum(z, 0.0) + l1p


def _shift_down(x, k, row, fill=0.0):
    return jnp.where(row >= k, pltpu.roll(x, k, 0), fill)


def _shift_up(x, k, row, fill=0.0):
    S = x.shape[0]
    return jnp.where(row < S - k, pltpu.roll(x, S - k, 0), fill)


def _lru_gates(xc, row, cw_ref, cb_ref, wa_ref, wx_ref, ba_ref, bx_ref, lam_ref):
    xconv = (cw_ref[3:4, :] * xc + cw_ref[2:3, :] * _shift_down(xc, 1, row) + cw_ref[1:2, :] * _shift_down(xc, 2, row)
             + cw_ref[0:1, :] * _shift_down(xc, 3, row) + cb_ref[...])
    xb = xconv.astype(BF)
    r = _sig(_dot(xb, wa_ref[...]) + ba_ref[...])
    i = _sig(_dot(xb, wx_ref[...]) + bx_ref[...])
    sp = _softplus_neg(lam_ref[...])
    log_a = -LRU_C * r * sp
    a = jnp.exp(log_a)
    mult = jnp.sqrt(-_expm1(2.0 * log_a))
    return xconv, r, i, sp, a, mult


ROWS_PER_TILE = 8


def _steps(a, b, shift, inside, products=True):
    n, k = inside.n, 1
    while k < n:
        b = a * jnp.where(inside(k), shift(b, k), 0.0) + b
        if products or 2 * k < n:
            a = a * jnp.where(inside(k), shift(a, k), 1.0)
        k *= 2
    return a, b


class _Inside:
    def __init__(self, pos, n, reverse):
        self.pos, self.n, self.reverse = pos, n, reverse

    def __call__(self, k):
        return self.pos < self.n - k if self.reverse else self.pos >= k


def _scan_rows(a, b, row, a_ref, b_ref, c_ref, reverse=False):
    S = a.shape[0]
    G = S // ROWS_PER_TILE
    if reverse:
        shift = lambda x, k: pltpu.roll(x, x.shape[0] - k, 0)
    else:
        shift = lambda x, k: pltpu.roll(x, k, 0)
    a, b = _steps(a, b, shift, _Inside(row % ROWS_PER_TILE, ROWS_PER_TILE, reverse))
    a_ref[...] = a
    b_ref[...] = b
    last = 0 if reverse else ROWS_PER_TILE - 1
    grow = lax.broadcasted_iota(jnp.int32, (G, a.shape[1]), 0)
    _, tot = _steps(a_ref[pl.ds(last, G, stride=ROWS_PER_TILE), :], b_ref[pl.ds(last, G, stride=ROWS_PER_TILE), :],
                    shift, _Inside(grow, G, reverse), products=False)
    enters = jnp.where(_Inside(grow, G, reverse)(1), shift(tot, 1), 0.0)
    for r in range(ROWS_PER_TILE):
        c_ref[pl.ds(r, G, stride=ROWS_PER_TILE), :] = enters
    return b + a * c_ref[...]


def _odd_c_fwd(h, cw, cb, wa, wx, ba, bx, lam, name, comm=None):
    S = h.shape[0]

    def body(xc_ref, cg_ref, cw_ref, cb_ref, wa_ref, wx_ref, ba_ref, bx_ref, lam_ref, mix_ref, hst_ref, sa_ref, sb_ref, sc_ref):
        row = lax.broadcasted_iota(jnp.int32, (S, 128), 0)
        xconv, r, i, sp, a, mult = _lru_gates(xc_ref[...], row, cw_ref, cb_ref, wa_ref, wx_ref, ba_ref, bx_ref, lam_ref)
        bb = _scan_rows(a, mult * (i * xconv), row, sa_ref, sb_ref, sc_ref)
        hst_ref[...] = bb
        cg = cg_ref[...]
        mix_ref[...] = (bb * (cg * _sig(cg))).astype(BF)

    col = lambda off: pl.BlockSpec((S, 128), lambda j: (0, off + j))
    vec = pl.BlockSpec((1, 128), lambda j: (0, j))
    mat = pl.BlockSpec((None, 128, 128), lambda j: (j, 0, 0))
    return _pcall(
        body, grid=(8,),
        in_specs=[col(0), col(8), pl.BlockSpec((4, 128), lambda j: (0, j)), vec, mat, mat, vec, vec, vec],
        out_specs=[pl.BlockSpec((None, S, 128), lambda j: (0, 0, j)), pl.BlockSpec((S, 128), lambda j: (0, j))],
        out_shape=[jax.ShapeDtypeStruct((2, S, W), BF), jax.ShapeDtypeStruct((S, W), F32)],
        scratch=[pltpu.VMEM((S, 128), F32)] * 3, name=name, comm=comm,
    )(h, h, cw, cb.reshape(1, W), wa, wx, ba.reshape(1, W), bx.reshape(1, W), lam.reshape(1, W))


def _pool_sums(x, g, row, shift):
    s2 = x + shift(x, 1, row)
    s4 = s2 + shift(s2, 2, row)
    s8 = s4 + shift(s4, 4, row)
    s16 = s8 + shift(s8, 8, row)
    return jnp.where(g == 0, s2, jnp.where(g == 1, s4, jnp.where(g == 2, s8, s16)))


def _odd_d_fwd(h, mix3, wp, dscale, name):
    S = h.shape[0]

    def body(xd_ref, dg_ref, wp_ref, ds_ref, mix_in, mix_ref):
        g = pl.program_id(0)
        row = lax.broadcasted_iota(jnp.int32, (S, 256), 0)
        xd = xd_ref[...]
        cnt = jnp.minimum(row + 1, jnp.left_shift(2, g)).astype(F32)
        pooled = _pool_sums(xd, g, row, _shift_down) / cnt - xd
        mixed = _dot(pooled.astype(BF), wp_ref[...])
        dg = dg_ref[...]
        mix_ref[...] = (mixed * ds_ref[...] * (dg * _sig(dg))).astype(BF)

    col = lambda off: pl.BlockSpec((S, 256), lambda g: (0, off + g))
    return pl.pallas_call(
        body, grid=(4,),
        in_specs=[col(8), col(12), pl.BlockSpec((None, 256, 256), lambda g: (g, 0, 0)),
                  pl.BlockSpec((1, 256), lambda g: (0, g)), ANY],
        out_specs=pl.BlockSpec((None, S, 256), lambda g: (1, 0, g)),
        out_shape=jax.ShapeDtypeStruct((2, S, W), BF), input_output_aliases={4: 0},
        name=name, compiler_params=_cp(),
    )(h, h, wp, dscale.reshape(1, W), mix3)


def _odd_c_bwd(h, hst, dmix3, cw, cb, wa, wx, wat, wxt, ba, bx, lam, name, comm=None):
    S = h.shape[0]

    def body(xc_ref, cg_ref, hst_ref, dc_ref, cw_ref, cb_ref, wa_ref, wx_ref, wat_ref, wxt_ref, ba_ref, bx_ref, lam_ref,
             dh_ref, dcw_ref, dcb_ref, dwa_ref, dwx_ref, dba_ref, dbx_ref, dlam_ref, sa_ref, sb_ref, sc_ref):
        row = lax.broadcasted_iota(jnp.int32, (S, 128), 0)
        xc = xc_ref[...]
        xconv, r, i, sp, a, mult = _lru_gates(xc, row, cw_ref, cb_ref, wa_ref, wx_ref, ba_ref, bx_ref, lam_ref)
        hst = hst_ref[...]
        cg = cg_ref[...]
        sg, dsg = _silu_grad(cg)
        dc = dc_ref[...]
        dh_ref[1] = (dc * hst * dsg).astype(BF)
        lam_t = _scan_rows(_shift_up(a, 1, row), dc * sg, row, sa_ref, sb_ref, sc_ref, reverse=True)
        da = lam_t * _shift_down(hst, 1, row)
        ix = i * xconv
        dmult = lam_t * ix
        di = lam_t * mult * xconv
        dxconv = lam_t * mult * i
        dlog_a = da * a - dmult * (a * a / mult)
        dr = dlog_a * (-LRU_C * sp)
        dsp = jnp.sum(dlog_a * (-LRU_C * r), axis=0, keepdims=True)
        dlam_ref[...] = dsp * (-_sig(-lam_ref[...]))
        dpa = dr * r * (1.0 - r)
        dpx = di * i * (1.0 - i)
        dpab = dpa.astype(BF)
        dpxb = dpx.astype(BF)
        xb = xconv.astype(BF)
        dxconv = dxconv + _dot(dpab, wat_ref[...]) + _dot(dpxb, wxt_ref[...])
        dwa_ref[...] = _dot_tn(xb, dpab)
        dwx_ref[...] = _dot_tn(xb, dpxb)
        dba_ref[...] = jnp.sum(dpa, axis=0, keepdims=True)
        dbx_ref[...] = jnp.sum(dpx, axis=0, keepdims=True)
        dh_ref[0] = (cw_ref[3:4, :] * dxconv + cw_ref[2:3, :] * _shift_up(dxconv, 1, row)
                     + cw_ref[1:2, :] * _shift_up(dxconv, 2, row) + cw_ref[0:1, :] * _shift_up(dxconv, 3, row)).astype(BF)
        for j in range(4):
            src = xc if j == 3 else _shift_down(xc, 3 - j, row)
            dcw_ref[j:j + 1, :] = jnp.sum(dxconv * src, axis=0, keepdims=True)
        dcb_ref[...] = jnp.sum(dxconv, axis=0, keepdims=True)

    col = lambda off: pl.BlockSpec((S, 128), lambda j: (0, off + j))
    vec = pl.BlockSpec((1, 128), lambda j: (0, j))
    mat = pl.BlockSpec((None, 128, 128), lambda j: (j, 0, 0))
    vshape = jax.ShapeDtypeStruct((1, W), F32)
    mshape = jax.ShapeDtypeStruct((8, 128, 128), F32)
    return _pcall(
        body, grid=(8,),
        in_specs=[col(0), col(8), col(0), pl.BlockSpec((None, S, 128), lambda j: (0, 0, j)),
                  pl.BlockSpec((4, 128), lambda j: (0, j)), vec, mat, mat, mat, mat, vec, vec, vec],
        out_specs=[pl.BlockSpec((2, S, 128), lambda j: (0, 0, j)), pl.BlockSpec((4, 128), lambda j: (0, j)), vec,
                   mat, mat, vec, vec, vec],
        out_shape=[jax.ShapeDtypeStruct((4, S, W), BF), jax.ShapeDtypeStruct((4, W), F32), vshape, mshape, mshape,
                   vshape, vshape, vshape],
        scratch=[pltpu.VMEM((S, 128), F32)] * 3, name=name, vmem=56, comm=comm,
    )(h, h, hst, dmix3, cw, cb.reshape(1, W), wa, wx, wat, wxt, ba.reshape(1, W), bx.reshape(1, W), lam.reshape(1, W))


def _odd_d_bwd(h, dmix3, dh4, wp, wpt, dscale, name):
    S = h.shape[0]

    def body(xd_ref, dg_ref, dd_ref, wp_ref, wpt_ref, ds_ref, dh_in, dh_ref, dwp_ref, dds_ref):
        g = pl.program_id(0)
        row = lax.broadcasted_iota(jnp.int32, (S, 256), 0)
        xd = xd_ref[...]
        cnt = jnp.minimum(row + 1, jnp.left_shift(2, g)).astype(F32)
        pooled = _pool_sums(xd, g, row, _shift_down) / cnt - xd
        pb = pooled.astype(BF)
        mixed = _dot(pb, wp_ref[...])
        dg = dg_ref[...]
        sg, dsg = _silu_grad(dg)
        dd = dd_ref[...]
        dmixed = dd * ds_ref[...] * sg
        dds_ref[...] = jnp.sum(dd * mixed * sg, axis=0, keepdims=True)
        dh_ref[1] = (dd * mixed * ds_ref[...] * dsg).astype(BF)
        dmb = dmixed.astype(BF)
        dpooled = _dot(dmb, wpt_ref[...])
        dwp_ref[...] = _dot_tn(pb, dmb)
        dh_ref[0] = (_pool_sums(dpooled / cnt, g, row, _shift_up) - dpooled).astype(BF)

    col = lambda off: pl.BlockSpec((S, 256), lambda g: (0, off + g))
    mat = pl.BlockSpec((None, 256, 256), lambda g: (g, 0, 0))
    vec = pl.BlockSpec((1, 256), lambda g: (0, g))
    return pl.pallas_call(
        body, grid=(4,),
        in_specs=[col(8), col(12), pl.BlockSpec((None, S, 256), lambda g: (1, 0, g)), mat, mat, vec, ANY],
        out_specs=[pl.BlockSpec((2, S, 256), lambda g: (1, 0, g)), mat, vec],
        out_shape=[jax.ShapeDtypeStruct((4, S, W), BF), jax.ShapeDtypeStruct((4, 256, 256), F32),
                   jax.ShapeDtypeStruct((1, W), F32)],
        input_output_aliases={6: 0}, name=name, compiler_params=_cp(56),
    )(h, h, dmix3, wp, wpt, dscale.reshape(1, W), dh4)


def _peer(d):
    x, y, c = lax.axis_index("x"), lax.axis_index("y"), lax.axis_index("c")
    px = 1 - x if d & 4 else x
    py = 1 - y if d & 2 else y
    pc = 1 - c if d & 1 else c
    return (px, py, pc), 4 * px + 2 * py + pc


class _GatherAll(_Comm):
    def __init__(self, xs):
        self.peers = EVERYONE
        self.inputs = [xs]
        self.out_shapes = [jax.ShapeDtypeStruct((N_DEV,) + xs.shape, xs.dtype)]
        self.sem_shapes = [pltpu.SemaphoreType.DMA((N_DEV - 1,)), pltpu.SemaphoreType.DMA((N_DEV - 1,)),
                           pltpu.SemaphoreType.DMA]

    def copies(self, ins, outs, sems):
        (x_ref,), (out_ref,), (send, recv, loc) = ins, outs, sems
        _, me = _peer(0)
        res = [pltpu.make_async_copy(x_ref, out_ref.at[me], loc)]
        for d in range(1, N_DEV):
            peer, _ = _peer(d)
            res.append(pltpu.make_async_remote_copy(src_ref=x_ref, dst_ref=out_ref.at[me], send_sem=send.at[d - 1],
                                                    recv_sem=recv.at[d - 1], device_id=peer, device_id_type=MESH))
        return res


class _ExchangeAll(_Comm):
    def __init__(self, g8):
        self.peers = EVERYONE
        self.inputs = [g8]
        self.out_shapes = [jax.ShapeDtypeStruct(g8.shape, g8.dtype)]
        self.sem_shapes = [pltpu.SemaphoreType.DMA((N_DEV - 1,)), pltpu.SemaphoreType.DMA((N_DEV - 1,)),
                           pltpu.SemaphoreType.DMA]

    def copies(self, ins, outs, sems):
        (g_ref,), (out_ref,), (send, recv, loc) = ins, outs, sems
        _, me = _peer(0)
        res = [pltpu.make_async_copy(g_ref.at[me], out_ref.at[0], loc)]
        for d in range(1, N_DEV):
            peer, pidx = _peer(d)
            res.append(pltpu.make_async_remote_copy(src_ref=g_ref.at[pidx], dst_ref=out_ref.at[d], send_sem=send.at[d - 1],
                                                    recv_sem=recv.at[d - 1], device_id=peer, device_id_type=MESH))
        return res


def _sum8(r8, tr, name):
    _, R, C = r8.shape
    tr = min(tr, R)
    assert R % tr == 0

    def body(r_ref, o_ref):
        acc = r_ref[0]
        for d in range(1, N_DEV):
            acc = acc + r_ref[d]
        o_ref[...] = acc

    return pl.pallas_call(
        body, grid=(R // tr,), in_specs=[pl.BlockSpec((N_DEV, tr, C), lambda i: (0, i, 0))],
        out_specs=pl.BlockSpec((tr, C), lambda i: (i, 0)), out_shape=jax.ShapeDtypeStruct((R, C), F32),
        name=name, compiler_params=_cp(),
    )(r8)


def _adamw_math(w, g, m, v):
    m2 = B1 * m + (1.0 - B1) * g
    v2 = B2 * v + (1.0 - B2) * (g * g)
    m_hat = m2 / (1.0 - B1 ** STEP)
    v_hat = v2 / (1.0 - B2 ** STEP)
    return -LR * (m_hat / (jnp.sqrt(v_hat) + ADAM_EPS) + WD * w), m2, v2


def _adamw_many(ws, gs, ms, vs, name):
    n = len(ws)

    def body(*refs):
        for i in range(n):
            d, m2, v2 = _adamw_math(refs[i][...], refs[n + i][...], refs[2 * n + i][...], refs[3 * n + i][...])
            refs[4 * n + i][...] = d
            refs[5 * n + i][...] = m2
            refs[6 * n + i][...] = v2

    vmem = pl.BlockSpec(memory_space=pltpu.VMEM)
    shapes = [jax.ShapeDtypeStruct(w.shape, F32) for w in ws]
    res = pl.pallas_call(body, in_specs=[vmem] * (4 * n), out_specs=[vmem] * (3 * n), out_shape=shapes * 3, name=name,
                         compiler_params=_cp())(*ws, *gs, *ms, *vs)
    return res[:n], res[n:2 * n], res[2 * n:]


def _adamw(w3, gs, m3, v3, tr, name, comm=None):
    _, R, C = w3.shape
    n = 2 if isinstance(gs[0], tuple) else 1

    def gradient(refs):
        if n == 1:
            return refs[0][...]
        s_ref, r_ref = refs
        return ((s_ref[...].astype(F32) + r_ref[0].astype(F32)) + r_ref[1].astype(F32)) + r_ref[2].astype(F32)

    def body(w_ref, *rest):
        g_refs, (m_ref, v_ref, d_ref, m2_ref, v2_ref, g_ref) = rest[:2 * n], rest[2 * n:]
        g = jnp.where(pl.program_id(0) == 0, gradient(g_refs[:n]), gradient(g_refs[n:]))
        d_ref[...], m2_ref[...], v2_ref[...] = _adamw_math(w_ref[...], g, m_ref[...], v_ref[...])
        g_ref[...] = g

    blk = pl.BlockSpec((None, tr, C), lambda j, i: (j, i, 0))

    def grad_specs(layer):
        at = lambda j, i: jnp.where(j == layer, i, 0)
        if n == 1:
            return [pl.BlockSpec((tr, C), lambda j, i: (at(j, i), 0))]
        return [pl.BlockSpec((None, tr, C), lambda j, i: (0, at(j, i), 0)), pl.BlockSpec((3, tr, C), lambda j, i: (0, at(j, i), 0))]

    flat = [a for g in gs for a in (g if n == 2 else (g,))]
    shp = jax.ShapeDtypeStruct((2, R, C), F32)
    return _pcall(body, grid=(2, R // tr), in_specs=[blk] + grad_specs(0) + grad_specs(1) + [blk, blk], out_specs=[blk] * 4,
                  out_shape=[shp] * 4, name=name, comm=comm)(w3, *flat, m3, v3)


def _rep_pack(a):
    n = a.size
    pad = (-n) % 1024
    f = a.reshape(-1)
    if pad:
        f = jnp.concatenate([f, jnp.zeros((pad,), a.dtype)])
    return f.reshape(N_DEV, -1, 128)


def _rep_unpack(p, shape):
    n = 1
    for s in shape:
        n *= s
    return p.reshape(-1)[:n].reshape(shape)


def _sh_pack(a, axis):
    shp = a.shape
    a = a.reshape(shp[:axis] + (N_DEV, shp[axis] // N_DEV) + shp[axis + 1:])
    return jnp.moveaxis(a, axis, 0).reshape(N_DEV, -1, 128)


def _sh_unpack(p, shape, axis):
    a = p.reshape((N_DEV,) + shape[:axis] + (shape[axis] // N_DEV,) + shape[axis + 1:])
    return jnp.moveaxis(a, 0, axis).reshape(shape)


def _pad_rows(a, mult=8):
    pad = (-a.shape[-2]) % mult
    if pad:
        a = jnp.concatenate([a, jnp.zeros(a.shape[:-2] + (pad, a.shape[-1]), a.dtype)], axis=-2)
    return a


REP = ["even_a_ln_g", "even_a_ln_b", "even_a_ws", "even_a_bs", "even_b_sinks", "even_ln_g", "even_ln_b",
       "odd_w_a", "odd_w_x"]
SH = [("odd_conv_w", (2, 4, W), 2), ("odd_conv_b", (2, W), 1), ("odd_b_a", (2, W), 1), ("odd_b_x", (2, W), 1),
      ("odd_lam", (2, W), 1), ("odd_w_pool", (2, 4, 256, 256), 2), ("odd_d_scale", (2, W), 1),
      ("odd_ln_g", (2, D), 1), ("odd_ln_b", (2, D), 1)]
BIG = ["even_w_in", "even_w_out", "odd_w_in", "odd_w_out"]
NAMES = ["even_w_in", "even_a_ln_g", "even_a_ln_b", "even_a_ws", "even_a_bs", "even_b_sinks", "even_w_out",
         "even_ln_g", "even_ln_b", "odd_w_in", "odd_conv_w", "odd_conv_b", "odd_w_a", "odd_b_a", "odd_w_x", "odd_b_x",
         "odd_lam", "odd_w_pool", "odd_d_scale", "odd_w_out", "odd_ln_g", "odd_ln_b"]


def _rope_table(positions):
    inv = ROPE_THETA ** (-jnp.arange(0, 16, 2, dtype=F32) / 16)
    f = jnp.arange(128) % 64
    ang = positions.astype(F32)[:, None] * inv[f % 8][None, :]
    cos, sin = jnp.cos(ang), jnp.sin(ang)
    return jnp.concatenate([jnp.where(f < 16, cos, 1.0), jnp.where(f < 8, -sin, 0.0),
                            jnp.where((f >= 8) & (f < 16), sin, 0.0)], axis=1)


def kernel(x, positions, even_w_in, even_a_ln_g, even_a_ln_b, even_a_ws, even_a_bs, even_b_sinks, even_w_out, even_ln_g, even_ln_b, odd_w_in, odd_conv_w, odd_conv_b, odd_w_a, odd_b_a, odd_w_x, odd_b_x, odd_lam, odd_w_pool, odd_d_scale, odd_w_out, odd_ln_g, odd_ln_b, loss_target, m_even_w_in, m_even_a_ln_g, m_even_a_ln_b, m_even_a_ws, m_even_a_bs, m_even_b_sinks, m_even_w_out, m_even_ln_g, m_even_ln_b, m_odd_w_in, m_odd_conv_w, m_odd_conv_b, m_odd_w_a, m_odd_b_a, m_odd_w_x, m_odd_b_x, m_odd_lam, m_odd_w_pool, m_odd_d_scale, m_odd_w_out, m_odd_ln_g, m_odd_ln_b, v_even_w_in, v_even_a_ln_g, v_even_a_ln_b, v_even_a_ws, v_even_a_bs, v_even_b_sinks, v_even_w_out, v_even_ln_g, v_even_ln_b, v_odd_w_in, v_odd_conv_w, v_odd_conv_b, v_odd_w_a, v_odd_b_a, v_odd_w_x, v_odd_b_x, v_odd_lam, v_odd_w_pool, v_odd_d_scale, v_odd_w_out, v_odd_ln_g, v_odd_ln_b):
    args = (even_w_in, even_a_ln_g, even_a_ln_b, even_a_ws, even_a_bs, even_b_sinks, even_w_out, even_ln_g, even_ln_b,
            odd_w_in, odd_conv_w, odd_conv_b, odd_w_a, odd_b_a, odd_w_x, odd_b_x, odd_lam, odd_w_pool, odd_d_scale,
            odd_w_out, odd_ln_g, odd_ln_b)
    margs = (m_even_w_in, m_even_a_ln_g, m_even_a_ln_b, m_even_a_ws, m_even_a_bs, m_even_b_sinks, m_even_w_out,
             m_even_ln_g, m_even_ln_b, m_odd_w_in, m_odd_conv_w, m_odd_conv_b, m_odd_w_a, m_odd_b_a, m_odd_w_x,
             m_odd_b_x, m_odd_lam, m_odd_w_pool, m_odd_d_scale, m_odd_w_out, m_odd_ln_g, m_odd_ln_b)
    vargs = (v_even_w_in, v_even_a_ln_g, v_even_a_ln_b, v_even_a_ws, v_even_a_bs, v_even_b_sinks, v_even_w_out,
             v_even_ln_g, v_even_ln_b, v_odd_w_in, v_odd_conv_w, v_odd_conv_b, v_odd_w_a, v_odd_b_a, v_odd_w_x,
             v_odd_b_x, v_odd_lam, v_odd_w_pool, v_odd_d_scale, v_odd_w_out, v_odd_ln_g, v_odd_ln_b)
    wts = dict(zip(NAMES, args))
    mom = dict(zip(NAMES, margs))
    var = dict(zip(NAMES, vargs))
    S = x.shape[1]
    x0 = x[0]
    rope = _rope_table(positions[0])

    kinds = ("even", "odd", "even", "odd")
    blk_in = [jnp.transpose(wts[kinds[l] + "_w_in"][l // 2]).astype(BF) for l in range(4)]
    blk_out = [wts[kinds[l] + "_w_out"][l // 2].astype(BF) for l in range(4)]
    sh_local = _pad_rows(jnp.concatenate([wts[nm].reshape(-1, 128) for nm, _, _ in SH], axis=0), 16)
    me = 4 * lax.axis_index("x") + 2 * lax.axis_index("y") + lax.axis_index("c")
    own_slot = lambda blk: lax.dynamic_update_slice(lax.empty((N_DEV,) + blk.shape, blk.dtype), blk[None], (me, 0, 0))
    reg = {"blk_small": sh_local, "w_small": own_slot(sh_local)}
    sched = _Sched(reg)
    for l in range(4):
        reg[f"blk_in{l}"], reg[f"blk_out{l}"] = blk_in[l], blk_out[l]
        reg[f"w_in{l}"], reg[f"w_out{l}"] = own_slot(blk_in[l]), own_slot(blk_out[l])
    sched.add(_rows("blk_in0", "w_in0", "ag1", blk_in[0].shape[0], ROW_CHUNK[blk_in[0].shape[0]]))
    sched.add(_rows("blk_small", "w_small", "ag1", sh_local.shape[0], sh_local.shape[0]))
    for l in range(4):
        sched.add(_rows(f"blk_out{l}", f"w_out{l}", "ag1", D // N_DEV, ROW_CHUNK[D // N_DEV]))
        if l < 3:
            r = blk_in[l + 1].shape[0]
            sched.add(_rows(f"blk_in{l + 1}", f"w_in{l + 1}", "ag1", r, ROW_CHUNK[r]))

    def gathered(dst, blk):
        sched.flush(dst, FLUSH_EXTRA_US)
        return reg.pop(dst)

    wt_in0 = gathered("w_in0", blk_in[0]).reshape(-1, D)
    full = {nm: wts[nm] for nm in REP}

    def gather_small():
        sh_all = gathered("w_small", sh_local)
        off = 0
        for nm, shape, axis in SH:
            r = wts[nm].size // 128
            full[nm] = _sh_unpack(sh_all[:, off:off + r, :], shape, axis)
            off += r

    saved = []
    wt_in, w_out = [wt_in0, None, None, None], [None] * 4
    xf, xb = x0, x0.astype(BF)
    fwd = lambda name: FWD_OVERBOOK * CARRY_US[name]
    for layer in range(4):
        j = layer // 2
        kind = kinds[layer]
        if wt_in[layer] is None:
            wt_in[layer] = gathered(f"w_in{layer}", blk_in[layer]).reshape(-1, D)
        h = sched.run(_mm_nt, fwd("mm_h_" + kind), xb, wt_in[layer], 1024, 768 if kind == "even" else 512, "mm_h_" + kind)
        if kind == "even":
            bsb = jnp.broadcast_to(full["even_a_bs"][j][:, :, None], (8, 128, 128))
            mix3, o, l = sched.run(_even_fwd, fwd("even_fwd"), h, rope, full["even_a_ln_g"][j], full["even_a_ln_b"][j],
                                   full["even_a_ws"][j], bsb, full["even_b_sinks"][j], "even_fwd")
            extra = (o, l, bsb)
        else:
            if "odd_lam" not in full:
                gather_small()
            wa, wx = full["odd_w_a"][j].astype(BF), full["odd_w_x"][j].astype(BF)
            wp = full["odd_w_pool"][j].astype(BF)
            mix3, hst = sched.run(_odd_c_fwd, fwd("odd_c_fwd"), h, full["odd_conv_w"][j], full["odd_conv_b"][j], wa, wx,
                                  full["odd_b_a"][j], full["odd_b_x"][j], full["odd_lam"][j], "odd_c_fwd")
            mix3 = _odd_d_fwd(h, mix3, wp, full["odd_d_scale"][j], "odd_d_fwd")
            extra = (hst, wa, wx, wp)
        w_out[layer] = gathered(f"w_out{layer}", blk_out[layer]).reshape(D, D)
        z, xn, xnb = sched.run(_mm_out_ln, fwd("mm_out_ln"), mix3, w_out[layer], xf, full[kind + "_ln_g"][j],
                               full[kind + "_ln_b"][j], "mm_out_ln")
        saved.append((xb, h, mix3, z, extra))
        xf, xb = xn, xnb

    dxn = xf

    gsum = {nm: [None, None] for nm in NAMES}

    chip_sums = {}
    sched.overhang = 0.15

    waiting = []

    def chip_sum(g, tag, key):
        r = g.shape[0] // N_DEV
        reg["g_" + key] = g.reshape(N_DEV, r, D)
        sched.add(_rows("g_" + key, "d_" + key, "rsd", r, r), first=True)
        waiting.append((key, tag))

    def add_arrived():
        for key, tag in list(waiting):
            if "d_" + key in reg and not sched.pending("d_" + key):
                waiting.remove((key, tag))
                g8 = reg.pop("g_" + key)
                chip_sums[key] = reg["s_" + key] = _add_pairs(g8, reg.pop("d_" + key), "rs_add_" + tag)
                sched.add(_rows("s_" + key, "r_" + key, "rs", g8.shape[1], ROW_CHUNK[g8.shape[1]] // 2))

    sched.after_landing = add_arrived

    def reduced(key):
        sched.flush("d_" + key, FLUSH_EXTRA_US)
        sched.flush("r_" + key, FLUSH_EXTRA_US)
        return chip_sums[key], reg.pop("r_" + key)

    for layer in (3, 2, 1, 0):
        j = layer // 2
        xb, h, mix3, z, extra = saved[layer]
        kind = kinds[layer]
        if layer == 3:
            dz, dzb, dg, dbeta, part = sched.run(_ln_bwd, CARRY_US["ln_bwd"], dxn, z, full[kind + "_ln_g"][j], "loss_ln_bwd",
                                                 target=loss_target[0])
        else:
            dz, dzb, dg, dbeta = sched.run(_ln_bwd, CARRY_US["ln_bwd"], dxn, z, full[kind + "_ln_g"][j], "ln_bwd")
        gsum[kind + "_ln_g"][j] = dg.reshape(D)
        gsum[kind + "_ln_b"][j] = dbeta.reshape(D)
        chip_sum(sched.run(_mm_tn, CARRY_US["mm_dw_out"], mix3, dzb, 512, "mm_dw_out"), "w_out", f"out{layer}")
        dmix3 = sched.run(_mm_nt, CARRY_US["mm_dmix"], dzb, w_out[layer], 1024, 512, "mm_dmix", out3=True)
        if kind == "even":
            o, l, bsb = extra
            ws = full["even_a_ws"][j]
            dh, dws, dbs, dlng, dlnb, dsink = sched.run(
                _even_bwd, CARRY_US["even_bwd"], h, dmix3, o, l, rope, full["even_a_ln_g"][j], full["even_a_ln_b"][j],
                ws, jnp.swapaxes(ws, 1, 2), bsb, full["even_b_sinks"][j], "even_bwd")
            gsum["even_a_ws"][j] = dws
            gsum["even_a_bs"][j] = jnp.transpose(dbs[:, :8])
            gsum["even_a_ln_g"][j] = dlng.reshape(W)
            gsum["even_a_ln_b"][j] = dlnb.reshape(W)
            gsum["even_b_sinks"][j] = dsink[0, :16]
            if layer == 0:
                rep_rows = [_rep_pack(jnp.stack(gsum[nm]).reshape(wts[nm].shape)) for nm in REP]
                sh_rows = [_sh_pack(jnp.stack(gsum[nm]).reshape(shape), axis) for nm, shape, axis in SH]
                packed = _pad_rows(jnp.concatenate(rep_rows + sh_rows, axis=1))
                gw, (small8, parts) = _mm_tn(dh, xb, 768, "mm_dw_in_even", comm=_Join([_ExchangeAll(packed), _GatherAll(part)]))
                loss = jnp.sum(parts[:, 0, 0]) * (0.5 / D)
            else:
                gw = sched.run(_mm_tn, CARRY_US["mm_dw_in_even"], dh, xb, 768, "mm_dw_in_even")
            chip_sum(gw, "w_in_even", f"in{layer}")
            if layer == 0:
                n_rep = sum(p.shape[1] for p in rep_rows)
                red = _sum8(small8, 1 << 20, "sum_small")
                (rep_all,) = sched.flush("d_in0", FLUSH_EXTRA_US, beside=_GatherAll(_pad_rows(red[:n_rep])))
                sched.overhang = 0.6
            dxn = sched.run(_mm_nn_res, CARRY_US["mm_dx_even"], dh, wt_in[layer], dz, 512, 1024, "mm_dx_even")
        else:
            hst, wa, wx, wp = extra
            dh4, dcw, dcb, dwa, dwx, dba, dbx, dlam = sched.run(
                _odd_c_bwd, CARRY_US["odd_c_bwd"], h, hst, dmix3, full["odd_conv_w"][j], full["odd_conv_b"][j], wa, wx,
                jnp.swapaxes(wa, 1, 2), jnp.swapaxes(wx, 1, 2), full["odd_b_a"][j], full["odd_b_x"][j], full["odd_lam"][j],
                "odd_c_bwd")
            dh4, dwp, dds = _odd_d_bwd(h, dmix3, dh4, wp, jnp.swapaxes(wp, 1, 2), full["odd_d_scale"][j], "odd_d_bwd")
            gsum["odd_conv_w"][j], gsum["odd_conv_b"][j] = dcw, dcb.reshape(W)
            gsum["odd_w_a"][j], gsum["odd_w_x"][j] = dwa, dwx
            gsum["odd_b_a"][j], gsum["odd_b_x"][j], gsum["odd_lam"][j] = dba.reshape(W), dbx.reshape(W), dlam.reshape(W)
            gsum["odd_w_pool"][j], gsum["odd_d_scale"][j] = dwp, dds.reshape(W)
            chip_sum(sched.run(_mm_tn, CARRY_US["mm_dw_in_odd"], dh4, xb, 512, "mm_dw_in_odd"), "w_in_odd", f"in{layer}")
            dxn = sched.run(_mm_nn_res, CARRY_US["mm_dx_odd"], dh4, wt_in[layer], dz, 512, 1024, "mm_dx_odd")
    grad_x = dxn[None]

    out_g, out_d, out_m, out_v = {}, {}, {}, {}
    for nm, kind, what, layers in (("odd_w_out", "odd", "out", (1, 3)), ("even_w_out", "even", "out", (0, 2)),
                                   ("odd_w_in", "odd", "in", (1, 3)), ("even_w_in", "even", "in", (0, 2))):
        gl = [reduced(f"{what}{l}") for l in layers]
        if nm == "even_w_in":
            view = lambda a: jnp.transpose(a, (0, 2, 1))
            res, _ = _adamw(view(wts[nm]), gl, view(mom[nm]), view(var[nm]), 112, f"adamw_{nm}")
            res = [view(a) for a in res]
        elif what == "in":
            gs = [jnp.transpose(_rs_final(s4, r3, "rs_final_w_in_odd")) for s4, r3 in gl]
            res, _ = _adamw(wts[nm], gs, mom[nm], var[nm], 512, f"adamw_{nm}")
        else:
            res = sched.run(_adamw, CARRY_US["adamw_" + nm], wts[nm], gl, mom[nm], var[nm], 128, f"adamw_{nm}")
        out_d[nm], out_m[nm], out_v[nm], out_g[nm] = res

    g_small = {}
    off = 0
    for nm, p in zip(REP, rep_rows):
        r = p.shape[1]
        g_small[nm] = _rep_unpack(rep_all[:, off:off + r, :], wts[nm].shape)
        off += r
    off = n_rep
    for (nm, shape, axis), p in zip(SH, sh_rows):
        r = p.shape[1]
        g_small[nm] = red[off:off + r].reshape(wts[nm].shape)
        off += r

    def rows(a):
        f = a.reshape(-1)
        pad = (-f.shape[0]) % 128
        if pad:
            f = jnp.concatenate([f, jnp.zeros((pad,), a.dtype)])
        return f.reshape(-1, 128)

    small = REP + [nm for nm, _, _ in SH]
    each = lambda src: [rows(src[nm]) for nm in small]
    d2, m2, v2 = _adamw_many(each(wts), each(g_small), each(mom), each(var), "adamw_small")
    for i, nm in enumerate(small):
        n, shp = wts[nm].size, wts[nm].shape
        take = lambda a: a.reshape(-1)[:n].reshape(shp)
        out_g[nm], out_d[nm], out_m[nm], out_v[nm] = g_small[nm], take(d2[i]), take(m2[i]), take(v2[i])

    return (loss, grad_x, *[out_g[nm] for nm in NAMES], *[out_d[nm] for nm in NAMES],
            *[out_m[nm] for nm in NAMES], *[out_v[nm] for nm in NAMES])
```

```python
import functools

import jax
import jax.numpy as jnp
from jax import lax
from jax.experimental import pallas as pl
from jax.experimental.pallas import tpu as pltpu

F32 = jnp.float32
BF = jnp.bfloat16
MESH = pl.DeviceIdType.MESH
ANY = pl.BlockSpec(memory_space=pl.ANY)

N_DEV = 8
D = 2048
W = 1024
EVEN_IN = 5376
ODD_IN = 4096
CHUNK = 128
ALPHA = (2 * 4) ** 0.25
LN_EPS = 1e-5
ROPE_THETA = 500000.0
LRU_C = 8.0
LR, B1, B2, ADAM_EPS, WD, STEP = 0.001, 0.9, 0.999, 1e-08, 0.01, 10
NEG = -1e30
HEAD_COLS = 4


def _cp(vmem_mb=48, collective_id=None):
    return pltpu.CompilerParams(vmem_limit_bytes=vmem_mb * 1024 * 1024, collective_id=collective_id)


def _sig(x):
    return jax.nn.sigmoid(x)


def _silu_grad(x):
    s = _sig(x)
    return x * s, s * (1.0 + x * (1.0 - s))


def _dot(a, b):
    return jnp.dot(a, b, preferred_element_type=F32)


def _dot_nt(a, b):
    return lax.dot_general(a, b, (((1,), (1,)), ((), ())), preferred_element_type=F32)


def _dot_tn(a, b):
    return lax.dot_general(a, b, (((0,), (0,)), ((), ())), preferred_element_type=F32)


def _coords():
    return lax.axis_index("x"), lax.axis_index("y"), lax.axis_index("c")


def _chip(j):
    x, y, _ = _coords()
    return (1 - x if j & 2 else x), (1 - y if j & 1 else y)


X_NB, Y_NB, DIAG, SIB = 4, 2, 6, 1
EVERYONE = frozenset(range(1, N_DEV))
BARRIER_IDS = {}


class _Comm:
    def collective_id(self):
        return BARRIER_IDS.setdefault(frozenset(self.peers), len(BARRIER_IDS))

    def start(self, ins, outs, sems):
        barrier = pltpu.get_barrier_semaphore()
        for d in sorted(self.peers):
            pl.semaphore_signal(barrier, inc=1, device_id=_peer(d)[0], device_id_type=MESH)
        pl.semaphore_wait(barrier, len(self.peers))
        for cp in self.copies(ins, outs, sems):
            cp.start()

    def wait(self, ins, outs, sems):
        for cp in self.copies(ins, outs, sems):
            cp.wait()


class _Join(_Comm):
    def __init__(self, parts):
        self.parts = list(parts)
        self.peers = frozenset().union(*[p.peers for p in self.parts])
        self.inputs = [a for p in self.parts for a in p.inputs]
        self.out_shapes = [s for p in self.parts for s in p.out_shapes]
        self.sem_shapes = [s for p in self.parts for s in p.sem_shapes]
        self.aliases = {}
        i0 = o0 = 0
        for p in self.parts:
            for i, o in getattr(p, "aliases", {}).items():
                self.aliases[i0 + i] = o0 + o
            i0, o0 = i0 + len(p.inputs), o0 + len(p.out_shapes)

    def copies(self, ins, outs, sems):
        res = []
        i0 = o0 = s0 = 0
        for p in self.parts:
            ni, no, ns = len(p.inputs), len(p.out_shapes), len(p.sem_shapes)
            res += p.copies(ins[i0:i0 + ni], outs[o0:o0 + no], sems[s0:s0 + ns])
            i0, o0, s0 = i0 + ni, o0 + no, s0 + ns
        return res


ROWS_US = {"ag1": 0.104, "ag2": 0.052, "agd": 0.027, "rsd": 0.027, "rs": 0.205}
N_COPIES = {"ag1": 2, "ag2": 2, "agd": 4, "rsd": 4, "rs": 3}
TASK_PEERS = {"ag1": {X_NB, Y_NB}, "ag2": {X_NB, Y_NB}, "agd": {SIB}, "rsd": {SIB}, "rs": {X_NB, Y_NB, DIAG}}
ROW_CHUNK = {672: 224, 512: 128, 256: 128}
CARRY_US = {"mm_h_even": 58, "mm_h_odd": 47, "even_fwd": 42, "odd_c_fwd": 37, "mm_out_ln": 33, "ln_bwd": 23, "mm_dmix": 26,
            "mm_dw_out": 25, "even_bwd": 90, "odd_c_bwd": 58, "mm_dw_in_even": 58, "mm_dw_in_odd": 44, "mm_dx_even": 66,
            "mm_dx_odd": 55, "adamw_even_w_out": 11, "adamw_odd_w_out": 11}
FWD_OVERBOOK = 1.15
FLUSH_EXTRA_US = 60.0


def _cost_us(task, reg):
    kind, src, _, lo, hi = task
    return ROWS_US[kind] * (hi - lo) * reg[src].shape[-1] * reg[src].dtype.itemsize / 4096.0


class _Copies(_Comm):
    def __init__(self, tasks, reg):
        self.tasks = list(tasks)
        self.out_names, self.in_names = [], []
        for kind, src, dst, lo, hi in self.tasks:
            if dst not in self.out_names:
                self.out_names.append(dst)
        for kind, src, dst, lo, hi in self.tasks:
            if src not in self.out_names and src not in self.in_names:
                self.in_names.append(src)
        self.out_shapes, self.aliases = [], {}
        for o, dst in enumerate(self.out_names):
            if dst in reg:
                self.aliases[len(self.in_names)] = o
                self.in_names.append(dst)
                self.out_shapes.append(jax.ShapeDtypeStruct(reg[dst].shape, reg[dst].dtype))
            else:
                kind, src = next((t[0], t[1]) for t in self.tasks if t[2] == dst)
                shape = ({"rsd": 4, "rs": 3}[kind],) + reg[src].shape[1:]
                self.out_shapes.append(jax.ShapeDtypeStruct(shape, reg[src].dtype))
        self.inputs = [reg[nm] for nm in self.in_names]
        n = sum(N_COPIES[t[0]] for t in self.tasks)
        self.sem_shapes = [pltpu.SemaphoreType.DMA((n,)), pltpu.SemaphoreType.DMA((n,))]
        self.peers = frozenset().union(*[TASK_PEERS[t[0]] for t in self.tasks])

    def copies(self, ins, outs, sems):
        send, recv = sems
        x, y, c = _coords()
        me = 4 * x + 2 * y + c
        xn, yn = (1 - x, y, c), (x, 1 - y, c)
        at_xn, at_yn = 4 * (1 - x) + 2 * y + c, 4 * x + 2 * (1 - y) + c
        ref = dict(zip(self.in_names, ins))
        ref.update(zip(self.out_names, outs))
        res = []

        def copy(src, dst, to):
            i = len(res)
            res.append(pltpu.make_async_remote_copy(src_ref=src, dst_ref=dst, send_sem=send.at[i], recv_sem=recv.at[i],
                                                    device_id=to, device_id_type=MESH))

        for kind, src, dst, lo, hi in self.tasks:
            n = hi - lo
            if kind == "ag1":
                for to in (xn, yn):
                    copy(ref[src].at[pl.ds(lo, n)], ref[dst].at[me, pl.ds(lo, n)], to)
            elif kind == "ag2":
                h = n // 2
                first, second = ref[dst].at[at_xn, pl.ds(lo, h)], ref[dst].at[at_yn, pl.ds(lo + h, n - h)]
                copy(first, first, yn)
                copy(second, second, xn)
            elif kind == "agd":
                for j in range(4):
                    px, py = _chip(j)
                    rows = ref[dst].at[4 * px + 2 * py + c, pl.ds(lo, n)]
                    copy(rows, rows, (x, y, 1 - c))
            elif kind == "rsd":
                for j in range(4):
                    px, py = _chip(j)
                    copy(ref[src].at[4 * px + 2 * py + 1 - c, pl.ds(lo, n)], ref[dst].at[j, pl.ds(lo, n)], (x, y, 1 - c))
            else:
                for j in (1, 2, 3):
                    px, py = _chip(j)
                    copy(ref[src].at[j, pl.ds(lo, n)], ref[dst].at[j - 1, pl.ds(lo, n)], (px, py, c))
        return res


class _Sched:
    def __init__(self, reg):
        self.reg, self.queue, self.later = reg, [], []
        self.overhang = 0.5
        self.after_landing = None

    def add(self, tasks, first=False):
        self.queue = list(tasks) + self.queue if first else self.queue + list(tasks)

    def pending(self, dst):
        return any(t[2] == dst for t in self.queue + self.later)

    def take(self, budget_us, must=None, overhang=0.5):
        self.queue, self.later = self.later + self.queue, []
        picked, us = [], 0.0
        rest = []
        for t in self.queue:
            cost = _cost_us(t, self.reg)
            if (must is not None and t[2] == must) or us + (1.0 - overhang) * cost <= budget_us:
                picked.append(t)
                us += cost
                if t[0] in ("ag1", "ag2"):
                    self.later.append(({"ag1": "ag2", "ag2": "agd"}[t[0]], t[2], t[2], t[3], t[4]))
            else:
                rest.append(t)
        self.queue = rest
        return _Copies(picked, self.reg) if picked else None

    def landed(self, comm, got):
        if comm is not None:
            for nm, a in zip(comm.out_names, got):
                self.reg[nm] = a
        if self.after_landing is not None:
            self.after_landing()

    def run(self, builder, budget_us, *args, **kw):
        comm = self.take(budget_us, overhang=self.overhang)
        res, got = builder(*args, comm=comm, **kw)
        self.landed(comm, got)
        return res

    def flush(self, dst, budget_us=0.0, beside=None):
        res = []
        while self.pending(dst):
            comm = self.take(budget_us, must=dst)
            got = _comm_only(comm if beside is None else _Join([comm, beside]), "flush_" + dst)
            res, beside = got[len(comm.out_shapes):], None
            self.landed(comm, got[:len(comm.out_shapes)])
        return res


def _rows(name_src, name_dst, kind, n_rows, chunk):
    return [(kind, name_src, name_dst, lo, min(lo + chunk, n_rows)) for lo in range(0, n_rows, chunk)]


def _pcall(body, *, grid, in_specs, out_specs, out_shape, name, scratch=(), vmem=48, comm=None):
    in_specs, out_specs, out_shape, scratch = list(in_specs), list(out_specs), list(out_shape), list(scratch)
    if comm is None:
        call = pl.pallas_call(body, grid=grid, in_specs=in_specs, out_specs=out_specs, out_shape=out_shape,
                              scratch_shapes=scratch, name=name, compiler_params=_cp(vmem))
        return lambda *args: (call(*args), [])
    n_in, n_out, n_scr = len(in_specs), len(out_specs), len(scratch)
    c_in, c_out = len(comm.inputs), len(comm.out_shapes)
    aliases = {n_in + i: n_out + o for i, o in getattr(comm, "aliases", {}).items()}

    def wrapped(*refs):
        ins, cins = refs[:n_in], refs[n_in:n_in + c_in]
        o0 = n_in + c_in
        outs, couts = refs[o0:o0 + n_out], refs[o0 + n_out:o0 + n_out + c_out]
        s0 = o0 + n_out + c_out
        scr, sems = refs[s0:s0 + n_scr], refs[s0 + n_scr:]
        ids = [pl.program_id(a) for a in range(len(grid))]
        first = functools.reduce(jnp.logical_and, [i == 0 for i in ids])
        last = functools.reduce(jnp.logical_and, [i == g - 1 for i, g in zip(ids, grid)])

        @pl.when(first)
        def _():
            comm.start(cins, couts, sems)

        body(*ins, *outs, *scr)

        @pl.when(last)
        def _():
            comm.wait(cins, couts, sems)

    call = pl.pallas_call(wrapped, grid=grid, in_specs=in_specs + [ANY] * c_in, out_specs=out_specs + [ANY] * c_out,
                          out_shape=out_shape + list(comm.out_shapes), scratch_shapes=scratch + list(comm.sem_shapes),
                          input_output_aliases=aliases, name=name, compiler_params=_cp(vmem, comm.collective_id()))

    def run(*args):
        res = call(*args, *comm.inputs)
        return res[:n_out], res[n_out:]

    return run


def _comm_only(comm, name):
    c_in, c_out = len(comm.inputs), len(comm.out_shapes)

    def body(*refs):
        cins, couts, sems = refs[:c_in], refs[c_in:c_in + c_out], refs[c_in + c_out:]
        comm.start(cins, couts, sems)
        comm.wait(cins, couts, sems)

    return pl.pallas_call(body, in_specs=[ANY] * c_in, out_specs=[ANY] * c_out, out_shape=list(comm.out_shapes),
                          scratch_shapes=list(comm.sem_shapes), input_output_aliases=dict(getattr(comm, "aliases", {})),
                          name=name, compiler_params=pltpu.CompilerParams(collective_id=comm.collective_id()))(*comm.inputs)


def _chip_blocks():
    _, _, c = _coords()
    return jnp.stack([4 * px + 2 * py + c for px, py in map(_chip, range(4))]).astype(jnp.int32)


def _add_pairs(g8, b4, name):
    _, R, C = b4.shape

    def body(idx_ref, a_ref, b_ref, o_ref):
        o_ref[...] = (a_ref[...].astype(F32) + b_ref[...].astype(F32)).astype(BF)

    blk = pl.BlockSpec((None, R, C), lambda j, idx: (j, 0, 0))
    grid_spec = pltpu.PrefetchScalarGridSpec(
        num_scalar_prefetch=1, grid=(4,),
        in_specs=[pl.BlockSpec((None, R, C), lambda j, idx: (idx[j], 0, 0)), blk], out_specs=blk)
    return pl.pallas_call(body, grid_spec=grid_spec, out_shape=jax.ShapeDtypeStruct(b4.shape, BF), name=name,
                          compiler_params=_cp())(_chip_blocks(), g8, b4)


def _rs_final(s4, r3, name):
    _, R, C = s4.shape
    tr = R // 2

    def body(s_ref, r_ref, o_ref):
        o_ref[...] = ((s_ref[...].astype(F32) + r_ref[0].astype(F32)) + r_ref[1].astype(F32)) + r_ref[2].astype(F32)

    return pl.pallas_call(
        body, grid=(2,),
        in_specs=[pl.BlockSpec((None, tr, C), lambda i: (0, i, 0)), pl.BlockSpec((3, tr, C), lambda i: (0, i, 0))],
        out_specs=pl.BlockSpec((tr, C), lambda i: (i, 0)), out_shape=jax.ShapeDtypeStruct((R, C), F32),
        name=name, compiler_params=_cp())(s4, r3)


def _mm_nt(a, w, tm, tn, name, out3=False, comm=None):
    M, K = a.shape
    N = w.shape[0]
    tm = min(tm, M)

    def body(a_ref, w_ref, o_ref):
        o_ref[...] = _dot_nt(a_ref[...], w_ref[...])

    if out3:
        per = W // tn
        out_shape = jax.ShapeDtypeStruct((N // W, M, W), F32)
        out_spec = pl.BlockSpec((None, tm, tn), lambda i, j: (j // per, i, j % per))
    else:
        out_shape = jax.ShapeDtypeStruct((M, N), F32)
        out_spec = pl.BlockSpec((tm, tn), lambda i, j: (i, j))
    (res,), extra = _pcall(
        body, grid=(M // tm, N // tn),
        in_specs=[pl.BlockSpec((tm, K), lambda i, j: (i, 0)), pl.BlockSpec((tn, K), lambda i, j: (j, 0))],
        out_specs=[out_spec], out_shape=[out_shape], name=name, comm=comm)(a, w)
    return res, extra


def _mm_tn(a, b, tm, name, comm=None):
    K, N = b.shape
    if a.ndim == 3:
        M = a.shape[0] * W
        per = W // tm
        a_spec = pl.BlockSpec((None, K, tm), lambda i: (i // per, 0, i % per))
    else:
        M = a.shape[1]
        a_spec = pl.BlockSpec((K, tm), lambda i: (0, i))

    def body(a_ref, b_ref, o_ref):
        o_ref[...] = _dot_tn(a_ref[...], b_ref[...]).astype(BF)

    (out,), extra = _pcall(
        body, grid=(M // tm,),
        in_specs=[a_spec, pl.BlockSpec((K, N), lambda i: (0, 0))],
        out_specs=[pl.BlockSpec((tm, N), lambda i: (i, 0))],
        out_shape=[jax.ShapeDtypeStruct((M, N), BF)], name=name, vmem=56, comm=comm)(a, b)
    return out, extra


def _mm_nn_res(a, w, res, tm, tn, name, comm=None):
    K, N = w.shape
    if a.ndim == 3:
        P, M = a.shape[0], a.shape[1]
        tm = min(tm, M)
        a_spec = pl.BlockSpec((P, tm, W), lambda j, i: (0, i, 0))
    else:
        P, M = 0, a.shape[0]
        tm = min(tm, M)
        a_spec = pl.BlockSpec((tm, K), lambda j, i: (i, 0))

    def body(a_ref, w_ref, r_ref, o_ref):
        if P:
            d = _dot(a_ref[0], w_ref[0:W, :])
            for p in range(1, P):
                d = d + _dot(a_ref[p], w_ref[p * W:(p + 1) * W, :])
        else:
            d = _dot(a_ref[...], w_ref[...])
        o_ref[...] = ALPHA * r_ref[...] + d

    (out,), extra = _pcall(
        body, grid=(N // tn, M // tm),
        in_specs=[a_spec, pl.BlockSpec((K, tn), lambda j, i: (0, j)), pl.BlockSpec((tm, tn), lambda j, i: (i, j))],
        out_specs=[pl.BlockSpec((tm, tn), lambda j, i: (i, j))],
        out_shape=[jax.ShapeDtypeStruct((M, N), F32)], name=name, comm=comm)(a, w, res)
    return out, extra


def _mm_out_ln(mix3, w_out, x, g, b, name, comm=None):
    S = x.shape[0]
    tm = min(512, S)

    def body(m_ref, w_ref, x_ref, g_ref, b_ref, z_ref, xn_ref, xb_ref):
        acc = _dot(m_ref[0], w_ref[0:W, :]) + _dot(m_ref[1], w_ref[W:2 * W, :])
        z = ALPHA * x_ref[...] + acc
        mu = jnp.mean(z, axis=1, keepdims=True)
        zc = z - mu
        var = jnp.mean(zc * zc, axis=1, keepdims=True)
        xn = zc * lax.rsqrt(var + LN_EPS) * g_ref[...] + b_ref[...]
        z_ref[...] = z
        xn_ref[...] = xn
        xb_ref[...] = xn.astype(BF)

    row = pl.BlockSpec((tm, D), lambda i: (i, 0))
    vec = pl.BlockSpec((1, D), lambda i: (0, 0))
    return _pcall(
        body, grid=(S // tm,),
        in_specs=[pl.BlockSpec((2, tm, W), lambda i: (0, i, 0)),
                  pl.BlockSpec((D, D), lambda i: (0, 0), pipeline_mode=pl.Buffered(1)), row, vec, vec],
        out_specs=[row, row, row],
        out_shape=[jax.ShapeDtypeStruct((S, D), F32), jax.ShapeDtypeStruct((S, D), F32), jax.ShapeDtypeStruct((S, D), BF)],
        name=name, comm=comm)(mix3, w_out, x, g.reshape(1, D), b.reshape(1, D))


def _ln_bwd(dxn, z, g, name, comm=None, target=None):
    S = z.shape[0]
    tm = min(256, S)
    head = target is not None

    def body(*refs):
        if head:
            d_ref, t_ref, z_ref, g_ref, dz_ref, dzb_ref, dg_ref, db_ref, p_ref = refs
        else:
            d_ref, z_ref, g_ref, dz_ref, dzb_ref, dg_ref, db_ref = refs
        i = pl.program_id(0)
        zz = z_ref[...]
        mu = jnp.mean(zz, axis=1, keepdims=True)
        zc = zz - mu
        var = jnp.mean(zc * zc, axis=1, keepdims=True)
        rstd = lax.rsqrt(var + LN_EPS)
        xhat = zc * rstd
        dy = d_ref[...]
        if head:
            e = dy - t_ref[...]
            dy = e * (1.0 / D)

            @pl.when(i == 0)
            def _():
                p_ref[...] = jnp.zeros_like(p_ref)

            p_ref[...] += jnp.sum(jnp.sum(e * e, axis=1, keepdims=True), axis=0, keepdims=True)
        dyg = dy * g_ref[...]
        m1 = jnp.mean(dyg, axis=1, keepdims=True)
        m2 = jnp.mean(dyg * xhat, axis=1, keepdims=True)
        dz = rstd * (dyg - m1 - xhat * m2)
        dz_ref[...] = dz
        dzb_ref[...] = dz.astype(BF)

        @pl.when(i == 0)
        def _():
            dg_ref[...] = jnp.zeros_like(dg_ref)
            db_ref[...] = jnp.zeros_like(db_ref)

        dg_ref[...] += jnp.sum(dy * xhat, axis=0, keepdims=True)
        db_ref[...] += jnp.sum(dy, axis=0, keepdims=True)

    row = pl.BlockSpec((tm, D), lambda i: (i, 0))
    vec = pl.BlockSpec((1, D), lambda i: (0, 0))
    out_specs = [row, row, vec, vec] + ([pl.BlockSpec((8, 128), lambda i: (0, 0))] if head else [])
    out_shape = [jax.ShapeDtypeStruct((S, D), F32), jax.ShapeDtypeStruct((S, D), BF), jax.ShapeDtypeStruct((1, D), F32),
                 jax.ShapeDtypeStruct((1, D), F32)] + ([jax.ShapeDtypeStruct((8, 128), F32)] if head else [])
    operands = (dxn, target, z, g.reshape(1, D)) if head else (dxn, z, g.reshape(1, D))
    return _pcall(body, grid=(S // tm,), in_specs=[row] * (len(operands) - 1) + [vec], out_specs=out_specs,
                  out_shape=out_shape, name=name, comm=comm)(*operands)


def _rope_fwd(t, r_ref):
    return (t * r_ref[:, 0:128] + pltpu.roll(t, 120, 1) * r_ref[:, 128:256]
            + pltpu.roll(t, 8, 1) * r_ref[:, 256:384])


def _rope_bwd(g, r_ref):
    return (g * r_ref[:, 0:128] + pltpu.roll(g * r_ref[:, 128:256], 8, 1)
            + pltpu.roll(g * r_ref[:, 256:384], 120, 1))


def _dup_heads(kb):
    lo = lax.broadcasted_iota(jnp.int32, kb.shape, 1) < 64
    sw = pltpu.roll(kb, 64, 1)
    return [jnp.where(lo, kb, sw).astype(BF), jnp.where(lo, sw, kb).astype(BF)]


def _even_fwd(h, rope, lng, lnb, ws, bsb, sinks, name, comm=None):
    S = h.shape[0]
    nb = S // CHUNK

    def body(h_ref, hp_ref, rc_ref, rp_ref, lng_ref, lnb_ref, ws_ref, bsb_ref, sink_ref, mix_ref, o_ref, l_ref):
        n = pl.program_id(0)
        lane = lax.broadcasted_iota(jnp.int32, (128, 128), 1)
        rowi = lax.broadcasted_iota(jnp.int32, (128, 128), 0)
        tri = rowi >= lane
        lane_lo = lane < 64
        v = h_ref[:, W:2 * W]
        mu = jnp.mean(v, axis=1, keepdims=True)
        vc = v - mu
        var = jnp.mean(vc * vc, axis=1, keepdims=True)
        vn = vc * lax.rsqrt(var + LN_EPS) * lng_ref[...] + lnb_ref[...]
        ms = [_dot(jnp.where(tri, ws_ref[g], 0.0).astype(BF), vn[:, g * 128:(g + 1) * 128].astype(BF)) for g in range(8)]
        for g in range(8):
            sl = slice(g * 128, (g + 1) * 128)
            ag = h_ref[:, 2 * W + g * 128:2 * W + (g + 1) * 128]
            mix_ref[0, :, sl] = (h_ref[:, sl] * (ms[g] + bsb_ref[g]) * (ag * _sig(ag))).astype(BF)
        kb = jnp.concatenate([_rope_fwd(hp_ref[:, 0:128], rp_ref), _rope_fwd(h_ref[:, 4096:4224], rc_ref)], axis=0)
        vb = jnp.concatenate([hp_ref[:, 128:256], h_ref[:, 4224:4352]], axis=0)
        k2 = _dup_heads(kb)
        v2 = _dup_heads(vb)
        qi = lax.broadcasted_iota(jnp.int32, (128, 256), 0)
        kj = lax.broadcasted_iota(jnp.int32, (128, 256), 1)
        diff = qi + 128 - kj
        valid = (diff >= 0) & (diff < 128) & ((n > 0) | (kj >= 128))
        lacc = jnp.zeros((128, 128), F32)
        for j0 in range(0, 8, HEAD_COLS):
            heads = [(j, half) for j in range(j0, j0 + HEAD_COLS) for half in range(2)]
            sc, pr, oh = {}, {}, {}
            for j in range(j0, j0 + HEAD_COLS):
                qc = _rope_fwd(h_ref[:, 3072 + j * 128:3072 + (j + 1) * 128], rc_ref)
                sc[j, 0] = _dot_nt(jnp.where(lane_lo, qc, 0.0).astype(BF), k2[j // 4])
                sc[j, 1] = _dot_nt(jnp.where(lane_lo, 0.0, qc).astype(BF), k2[j // 4])
            for j, half in heads:
                hq = 2 * j + half
                s = jnp.where(valid, sc[j, half] * 0.125, NEG)
                sk = sink_ref[hq]
                mx = jnp.maximum(jnp.max(s, axis=1, keepdims=True), sk)
                p = jnp.exp(s - mx)
                den = jnp.sum(p, axis=1, keepdims=True) + jnp.exp(sk - mx)
                pr[j, half] = (p / den).astype(BF)
                lacc = jnp.where(lane == hq, mx + jnp.log(den), lacc)
            for j, half in heads:
                oh[j, half] = _dot(pr[j, half], v2[j // 4])
            for j in range(j0, j0 + HEAD_COLS):
                cs = slice(j * 128, (j + 1) * 128)
                ocol = jnp.where(lane_lo, oh[j, 0], oh[j, 1])
                bg = h_ref[:, 4352 + j * 128:4352 + (j + 1) * 128]
                o_ref[:, cs] = ocol
                mix_ref[1, :, cs] = (ocol * (bg * _sig(bg))).astype(BF)
        l_ref[...] = lacc

    prev = lambda n: jnp.maximum(n - 1, 0)
    full = lambda shape: pl.BlockSpec(shape, lambda n: (0,) * len(shape))
    return _pcall(
        body, grid=(nb,),
        in_specs=[pl.BlockSpec((CHUNK, EVEN_IN), lambda n: (n, 0)),
                  pl.BlockSpec((CHUNK, 256), lambda n: (prev(n), 16)),
                  pl.BlockSpec((CHUNK, 384), lambda n: (n, 0)),
                  pl.BlockSpec((CHUNK, 384), lambda n: (prev(n), 0)),
                  full((1, W)), full((1, W)), full((8, 128, 128)), full((8, 128, 128)),
                  pl.BlockSpec(memory_space=pltpu.SMEM)],
        out_specs=[pl.BlockSpec((2, CHUNK, W), lambda n: (0, n, 0)),
                   pl.BlockSpec((CHUNK, W), lambda n: (n, 0)),
                   pl.BlockSpec((CHUNK, 128), lambda n: (n, 0))],
        out_shape=[jax.ShapeDtypeStruct((2, S, W), BF), jax.ShapeDtypeStruct((S, W), F32),
                   jax.ShapeDtypeStruct((S, 128), F32)],
        name=name, comm=comm)(h, h, rope, rope, lng.reshape(1, W), lnb.reshape(1, W), ws, bsb, sinks)


def _even_bwd(h, dmix3, o, l, rope, lng, lnb, ws, wst, bsb, sinks, name, comm=None):
    S = h.shape[0]
    nb = S // CHUNK

    def body(h_ref, hp_ref, hn_ref, dm_ref, dmn_ref, o_ref, on_ref, l_ref, ln_ref, rc_ref, rp_ref, rn_ref,
             lng_ref, lnb_ref, ws_ref, wst_ref, bsb_ref, sink_ref,
             dh_ref, dws_ref, dbs_ref, dlng_ref, dlnb_ref, dsink_ref, dvn_ref):
        n = pl.program_id(0)

        @pl.when(n == 0)
        def _():
            dws_ref[...] = jnp.zeros_like(dws_ref)
            dbs_ref[...] = jnp.zeros_like(dbs_ref)
            dlng_ref[...] = jnp.zeros_like(dlng_ref)
            dlnb_ref[...] = jnp.zeros_like(dlnb_ref)
            dsink_ref[...] = jnp.zeros_like(dsink_ref)

        lane = lax.broadcasted_iota(jnp.int32, (128, 128), 1)
        rowi = lax.broadcasted_iota(jnp.int32, (128, 128), 0)
        lane1 = lax.broadcasted_iota(jnp.int32, (1, 128), 1)
        tri = rowi >= lane
        tri_t = lane >= rowi
        lane_lo = lane < 64
        v = h_ref[:, W:2 * W]
        mu = jnp.mean(v, axis=1, keepdims=True)
        vc = v - mu
        var = jnp.mean(vc * vc, axis=1, keepdims=True)
        rstd = lax.rsqrt(var + LN_EPS)
        vhat = vc * rstd
        vn = vhat * lng_ref[...] + lnb_ref[...]
        dbs_acc = jnp.zeros((128, 128), F32)
        vng = [vn[:, g * 128:(g + 1) * 128].astype(BF) for g in range(8)]
        ms = [_dot(jnp.where(tri, ws_ref[g], 0.0).astype(BF), vng[g]) for g in range(8)]
        dmb = []
        for g in range(8):
            sl = slice(g * 128, (g + 1) * 128)
            m = ms[g] + bsb_ref[g]
            ag = h_ref[:, 2 * W + g * 128:2 * W + (g + 1) * 128]
            sg, dsg = _silu_grad(ag)
            u = h_ref[:, sl]
            da = dm_ref[0, :, sl]
            dmm = da * u * sg
            dh_ref[:, sl] = (da * m * sg).astype(BF)
            dh_ref[:, 2 * W + g * 128:2 * W + (g + 1) * 128] = (da * u * m * dsg).astype(BF)
            dmb.append(dmm.astype(BF))
            dbs_acc = jnp.where(lane == g, jnp.sum(dmm, axis=1, keepdims=True), dbs_acc)
        dvs = [_dot(jnp.where(tri_t, wst_ref[g], 0.0).astype(BF), dmb[g]) for g in range(8)]
        dwss = [_dot_nt(dmb[g], vng[g]) for g in range(8)]
        for g in range(8):
            dvn_ref[:, g * 128:(g + 1) * 128] = dvs[g]
            dws_ref[g] += jnp.where(tri, dwss[g], 0.0)
        dbs_ref[...] += dbs_acc
        dvn = dvn_ref[...]
        dlng_ref[...] += jnp.sum(dvn * vhat, axis=0, keepdims=True)
        dlnb_ref[...] += jnp.sum(dvn, axis=0, keepdims=True)
        dyg = dvn * lng_ref[...]
        m1 = jnp.mean(dyg, axis=1, keepdims=True)
        m2 = jnp.mean(dyg * vhat, axis=1, keepdims=True)
        dh_ref[:, W:2 * W] = (rstd * (dyg - m1 - vhat * m2)).astype(BF)
        kcur = _rope_fwd(h_ref[:, 4096:4224], rc_ref)
        kb = jnp.concatenate([_rope_fwd(hp_ref[:, 0:128], rp_ref), kcur], axis=0)
        vb = jnp.concatenate([hp_ref[:, 128:256], h_ref[:, 4224:4352]], axis=0)
        k2 = _dup_heads(kb)
        v2 = _dup_heads(vb)
        kc2 = _dup_heads(kcur)
        vc2 = _dup_heads(h_ref[:, 4224:4352])
        qi = lax.broadcasted_iota(jnp.int32, (128, 256), 0)
        kj = lax.broadcasted_iota(jnp.int32, (128, 256), 1)
        diff = qi + 128 - kj
        valid = (diff >= 0) & (diff < 128) & ((n > 0) | (kj >= 128))
        validn = (lane > rowi) & (n < nb - 1)
        lc = l_ref[...]
        lnx = ln_ref[...]
        dk = [jnp.zeros((128, 128), F32), jnp.zeros((128, 128), F32)]
        dv = [jnp.zeros((128, 128), F32), jnp.zeros((128, 128), F32)]
        dsk_acc = jnp.zeros((1, 128), F32)
        for j0 in range(0, 8, HEAD_COLS):
            heads = [(j, half) for j in range(j0, j0 + HEAD_COLS) for half in range(2)]
            t = {}
            for j in range(j0, j0 + HEAD_COLS):
                cs = slice(j * 128, (j + 1) * 128)
                qc = _rope_fwd(h_ref[:, 3072 + j * 128:3072 + (j + 1) * 128], rc_ref)
                qn = _rope_fwd(hn_ref[:, 3072 + j * 128:3072 + (j + 1) * 128], rn_ref)
                bg = h_ref[:, 4352 + j * 128:4352 + (j + 1) * 128]
                sgb, dsgb = _silu_grad(bg)
                db = dm_ref[1, :, cs]
                oc = o_ref[:, cs]
                do = db * sgb
                dh_ref[:, 4352 + j * 128:4352 + (j + 1) * 128] = (db * oc * dsgb).astype(BF)
                bgn = hn_ref[:, 4352 + j * 128:4352 + (j + 1) * 128]
                don = dmn_ref[1, :, cs] * (bgn * _sig(bgn))
                prod = do * oc
                prodn = don * on_ref[:, cs]
                for half in range(2):
                    hq = 2 * j + half
                    hm = lane_lo if half == 0 else jnp.logical_not(lane_lo)
                    t[j, half] = dict(
                        dsum=jnp.sum(jnp.where(hm, prod, 0.0), axis=1, keepdims=True),
                        dsumn=jnp.sum(jnp.where(hm, prodn, 0.0), axis=1, keepdims=True),
                        lh=jnp.sum(jnp.where(lane == hq, lc, 0.0), axis=1, keepdims=True),
                        lhn=jnp.sum(jnp.where(lane == hq, lnx, 0.0), axis=1, keepdims=True),
                        qm=jnp.where(hm, qc, 0.0).astype(BF), dom=jnp.where(hm, do, 0.0).astype(BF),
                        qnm=jnp.where(hm, qn, 0.0).astype(BF), donm=jnp.where(hm, don, 0.0).astype(BF))
            for j, half in heads:
                e, hk = t[j, half], j // 4
                e["s"], e["dp"] = _dot_nt(e["qm"], k2[hk]), _dot_nt(e["dom"], v2[hk])
                e["sn"], e["dpn"] = _dot_nt(e["qnm"], kc2[hk]), _dot_nt(e["donm"], vc2[hk])
            for j, half in heads:
                e, hq = t[j, half], 2 * j + half
                p = jnp.exp(jnp.where(valid, e["s"] * 0.125 - e["lh"], NEG))
                ds = p * (e["dp"] - e["dsum"])
                pn = jnp.exp(jnp.where(validn, e["sn"] * 0.125 - e["lhn"], NEG))
                dsn = pn * (e["dpn"] - e["dsumn"])
                psink = jnp.exp(sink_ref[hq] - e["lh"])
                dsk_acc = jnp.where(lane1 == hq, -jnp.sum(psink * e["dsum"], axis=0, keepdims=True), dsk_acc)
                e["ds"] = ds.astype(BF)
                e["pt"], e["dst"] = jnp.transpose(p[:, 128:256]).astype(BF), jnp.transpose(ds[:, 128:256]).astype(BF)
                e["pnt"], e["dsnt"] = jnp.transpose(pn).astype(BF), jnp.transpose(dsn).astype(BF)
            for j, half in heads:
                e, hk = t[j, half], j // 4
                e["dq"] = _dot(e["ds"], k2[hk])
                e["dv"] = _dot(e["pt"], e["dom"]) + _dot(e["pnt"], e["donm"])
                e["dk"] = _dot(e["dst"], e["qm"]) + _dot(e["dsnt"], e["qnm"])
            for j in range(j0, j0 + HEAD_COLS):
                hk = j // 4
                dqcol = jnp.where(lane_lo, t[j, 0]["dq"], t[j, 1]["dq"]) * 0.125
                dh_ref[:, 3072 + j * 128:3072 + (j + 1) * 128] = _rope_bwd(dqcol, rc_ref).astype(BF)
                dv[hk] = dv[hk] + t[j, 0]["dv"] + t[j, 1]["dv"]
                dk[hk] = dk[hk] + (t[j, 0]["dk"] + t[j, 1]["dk"]) * 0.125
        fold = lambda a: a + pltpu.roll(a, 64, 1)
        dh_ref[:, 4096:4224] = _rope_bwd(jnp.where(lane_lo, fold(dk[0]), fold(dk[1])), rc_ref).astype(BF)
        dh_ref[:, 4224:4352] = jnp.where(lane_lo, fold(dv[0]), fold(dv[1])).astype(BF)
        dsink_ref[...] += dsk_acc

    prev = lambda n: jnp.maximum(n - 1, 0)
    nxt = lambda n: jnp.minimum(n + 1, nb - 1)
    full = lambda shape: pl.BlockSpec(shape, lambda n: (0,) * len(shape))
    return _pcall(
        body, grid=(nb,),
        in_specs=[pl.BlockSpec((CHUNK, EVEN_IN), lambda n: (n, 0)),
                  pl.BlockSpec((CHUNK, 256), lambda n: (prev(n), 16)),
                  pl.BlockSpec((CHUNK, EVEN_IN), lambda n: (nxt(n), 0)),
                  pl.BlockSpec((2, CHUNK, W), lambda n: (0, n, 0)),
                  pl.BlockSpec((2, CHUNK, W), lambda n: (0, nxt(n), 0)),
                  pl.BlockSpec((CHUNK, W), lambda n: (n, 0)),
                  pl.BlockSpec((CHUNK, W), lambda n: (nxt(n), 0)),
                  pl.BlockSpec((CHUNK, 128), lambda n: (n, 0)),
                  pl.BlockSpec((CHUNK, 128), lambda n: (nxt(n), 0)),
                  pl.BlockSpec((CHUNK, 384), lambda n: (n, 0)),
                  pl.BlockSpec((CHUNK, 384), lambda n: (prev(n), 0)),
                  pl.BlockSpec((CHUNK, 384), lambda n: (nxt(n), 0)),
                  full((1, W)), full((1, W)), full((8, 128, 128)), full((8, 128, 128)), full((8, 128, 128)),
                  pl.BlockSpec(memory_space=pltpu.SMEM)],
        out_specs=[pl.BlockSpec((CHUNK, EVEN_IN), lambda n: (n, 0)),
                   full((8, 128, 128)), full((128, 128)), full((1, W)), full((1, W)), full((1, 128))],
        out_shape=[jax.ShapeDtypeStruct((S, EVEN_IN), BF), jax.ShapeDtypeStruct((8, 128, 128), F32),
                   jax.ShapeDtypeStruct((128, 128), F32), jax.ShapeDtypeStruct((1, W), F32),
                   jax.ShapeDtypeStruct((1, W), F32), jax.ShapeDtypeStruct((1, 128), F32)],
        scratch=[pltpu.VMEM((CHUNK, W), F32)], name=name, comm=comm,
    )(h, h, h, dmix3, dmix3, o, o, l, l, rope, rope, rope, lng.reshape(1, W), lnb.reshape(1, W), ws, wst, bsb, sinks)


def _expm1(x):
    ser = x * (1.0 + x * (0.5 + x * (1.0 / 6.0 + x * (1.0 / 24.0))))
    return jnp.where(jnp.abs(x) < 1e-2, ser, jnp.exp(x) - 1.0)


def _softplus_neg(lam):
    z = -lam
    e = jnp.exp(-jnp.abs(z))
    l1p = jnp.where(e < 1e-3, e * (1.0 - e * (0.5 - e * (1.0 / 3.0))), jnp.log(1.0 + e))
    return jnp.maximum(z, 0.0) + l1p


def _shift_down(x, k, row, fill=0.0):
    return jnp.where(row >= k, pltpu.roll(x, k, 0), fill)


def _shift_up(x, k, row, fill=0.0):
    S = x.shape[0]
    return jnp.where(row < S - k, pltpu.roll(x, S - k, 0), fill)


def _lru_gates(xc, row, cw_ref, cb_ref, wa_ref, wx_ref, ba_ref, bx_ref, lam_ref):
    xconv = (cw_ref[3:4, :] * xc + cw_ref[2:3, :] * _shift_down(xc, 1, row) + cw_ref[1:2, :] * _shift_down(xc, 2, row)
             + cw_ref[0:1, :] * _shift_down(xc, 3, row) + cb_ref[...])
    xb = xconv.astype(BF)
    r = _sig(_dot(xb, wa_ref[...]) + ba_ref[...])
    i = _sig(_dot(xb, wx_ref[...]) + bx_ref[...])
    sp = _softplus_neg(lam_ref[...])
    log_a = -LRU_C * r * sp
    a = jnp.exp(log_a)
    mult = jnp.sqrt(-_expm1(2.0 * log_a))
    return xconv, r, i, sp, a, mult


ROWS_PER_TILE = 8


def _steps(a, b, shift, inside, products=True):
    n, k = inside.n, 1
    while k < n:
        b = a * jnp.where(inside(k), shift(b, k), 0.0) + b
        if products or 2 * k < n:
            a = a * jnp.where(inside(k), shift(a, k), 1.0)
        k *= 2
    return a, b


class _Inside:
    def __init__(self, pos, n, reverse):
        self.pos, self.n, self.reverse = pos, n, reverse

    def __call__(self, k):
        return self.pos < self.n - k if self.reverse else self.pos >= k


def _scan_rows(a, b, row, a_ref, b_ref, c_ref, reverse=False):
    S = a.shape[0]
    G = S // ROWS_PER_TILE
    if reverse:
        shift = lambda x, k: pltpu.roll(x, x.shape[0] - k, 0)
    else:
        shift = lambda x, k: pltpu.roll(x, k, 0)
    a, b = _steps(a, b, shift, _Inside(row % ROWS_PER_TILE, ROWS_PER_TILE, reverse))
    a_ref[...] = a
    b_ref[...] = b
    last = 0 if reverse else ROWS_PER_TILE - 1
    grow = lax.broadcasted_iota(jnp.int32, (G, a.shape[1]), 0)
    _, tot = _steps(a_ref[pl.ds(last, G, stride=ROWS_PER_TILE), :], b_ref[pl.ds(last, G, stride=ROWS_PER_TILE), :],
                    shift, _Inside(grow, G, reverse), products=False)
    enters = jnp.where(_Inside(grow, G, reverse)(1), shift(tot, 1), 0.0)
    for r in range(ROWS_PER_TILE):
        c_ref[pl.ds(r, G, stride=ROWS_PER_TILE), :] = enters
    return b + a * c_ref[...]


def _odd_c_fwd(h, cw, cb, wa, wx, ba, bx, lam, name, comm=None):
    S = h.shape[0]

    def body(xc_ref, cg_ref, cw_ref, cb_ref, wa_ref, wx_ref, ba_ref, bx_ref, lam_ref, mix_ref, hst_ref, sa_ref, sb_ref, sc_ref):
        row = lax.broadcasted_iota(jnp.int32, (S, 128), 0)
        xconv, r, i, sp, a, mult = _lru_gates(xc_ref[...], row, cw_ref, cb_ref, wa_ref, wx_ref, ba_ref, bx_ref, lam_ref)
        bb = _scan_rows(a, mult * (i * xconv), row, sa_ref, sb_ref, sc_ref)
        hst_ref[...] = bb
        cg = cg_ref[...]
        mix_ref[...] = (bb * (cg * _sig(cg))).astype(BF)

    col = lambda off: pl.BlockSpec((S, 128), lambda j: (0, off + j))
    vec = pl.BlockSpec((1, 128), lambda j: (0, j))
    mat = pl.BlockSpec((None, 128, 128), lambda j: (j, 0, 0))
    return _pcall(
        body, grid=(8,),
        in_specs=[col(0), col(8), pl.BlockSpec((4, 128), lambda j: (0, j)), vec, mat, mat, vec, vec, vec],
        out_specs=[pl.BlockSpec((None, S, 128), lambda j: (0, 0, j)), pl.BlockSpec((S, 128), lambda j: (0, j))],
        out_shape=[jax.ShapeDtypeStruct((2, S, W), BF), jax.ShapeDtypeStruct((S, W), F32)],
        scratch=[pltpu.VMEM((S, 128), F32)] * 3, name=name, comm=comm,
    )(h, h, cw, cb.reshape(1, W), wa, wx, ba.reshape(1, W), bx.reshape(1, W), lam.reshape(1, W))


def _pool_sums(x, g, row, shift):
    s2 = x + shift(x, 1, row)
    s4 = s2 + shift(s2, 2, row)
    s8 = s4 + shift(s4, 4, row)
    s16 = s8 + shift(s8, 8, row)
    return jnp.where(g == 0, s2, jnp.where(g == 1, s4, jnp.where(g == 2, s8, s16)))


def _odd_d_fwd(h, mix3, wp, dscale, name):
    S = h.shape[0]

    def body(xd_ref, dg_ref, wp_ref, ds_ref, mix_in, mix_ref):
        g = pl.program_id(0)
        row = lax.broadcasted_iota(jnp.int32, (S, 256), 0)
        xd = xd_ref[...]
        cnt = jnp.minimum(row + 1, jnp.left_shift(2, g)).astype(F32)
        pooled = _pool_sums(xd, g, row, _shift_down) / cnt - xd
        mixed = _dot(pooled.astype(BF), wp_ref[...])
        dg = dg_ref[...]
        mix_ref[...] = (mixed * ds_ref[...] * (dg * _sig(dg))).astype(BF)

    col = lambda off: pl.BlockSpec((S, 256), lambda g: (0, off + g))
    return pl.pallas_call(
        body, grid=(4,),
        in_specs=[col(8), col(12), pl.BlockSpec((None, 256, 256), lambda g: (g, 0, 0)),
                  pl.BlockSpec((1, 256), lambda g: (0, g)), ANY],
        out_specs=pl.BlockSpec((None, S, 256), lambda g: (1, 0, g)),
        out_shape=jax.ShapeDtypeStruct((2, S, W), BF), input_output_aliases={4: 0},
        name=name, compiler_params=_cp(),
    )(h, h, wp, dscale.reshape(1, W), mix3)


def _odd_c_bwd(h, hst, dmix3, cw, cb, wa, wx, wat, wxt, ba, bx, lam, name, comm=None):
    S = h.shape[0]

    def body(xc_ref, cg_ref, hst_ref, dc_ref, cw_ref, cb_ref, wa_ref, wx_ref, wat_ref, wxt_ref, ba_ref, bx_ref, lam_ref,
             dh_ref, dcw_ref, dcb_ref, dwa_ref, dwx_ref, dba_ref, dbx_ref, dlam_ref, sa_ref, sb_ref, sc_ref):
        row = lax.broadcasted_iota(jnp.int32, (S, 128), 0)
        xc = xc_ref[...]
        xconv, r, i, sp, a, mult = _lru_gates(xc, row, cw_ref, cb_ref, wa_ref, wx_ref, ba_ref, bx_ref, lam_ref)
        hst = hst_ref[...]
        cg = cg_ref[...]
        sg, dsg = _silu_grad(cg)
        dc = dc_ref[...]
        dh_ref[1] = (dc * hst * dsg).astype(BF)
        lam_t = _scan_rows(_shift_up(a, 1, row), dc * sg, row, sa_ref, sb_ref, sc_ref, reverse=True)
        da = lam_t * _shift_down(hst, 1, row)
        ix = i * xconv
        dmult = lam_t * ix
        di = lam_t * mult * xconv
        dxconv = lam_t * mult * i
        dlog_a = da * a - dmult * (a * a / mult)
        dr = dlog_a * (-LRU_C * sp)
        dsp = jnp.sum(dlog_a * (-LRU_C * r), axis=0, keepdims=True)
        dlam_ref[...] = dsp * (-_sig(-lam_ref[...]))
        dpa = dr * r * (1.0 - r)
        dpx = di * i * (1.0 - i)
        dpab = dpa.astype(BF)
        dpxb = dpx.astype(BF)
        xb = xconv.astype(BF)
        dxconv = dxconv + _dot(dpab, wat_ref[...]) + _dot(dpxb, wxt_ref[...])
        dwa_ref[...] = _dot_tn(xb, dpab)
        dwx_ref[...] = _dot_tn(xb, dpxb)
        dba_ref[...] = jnp.sum(dpa, axis=0, keepdims=True)
        dbx_ref[...] = jnp.sum(dpx, axis=0, keepdims=True)
        dh_ref[0] = (cw_ref[3:4, :] * dxconv + cw_ref[2:3, :] * _shift_up(dxconv, 1, row)
                     + cw_ref[1:2, :] * _shift_up(dxconv, 2, row) + cw_ref[0:1, :] * _shift_up(dxconv, 3, row)).astype(BF)
        for j in range(4):
            src = xc if j == 3 else _shift_down(xc, 3 - j, row)
            dcw_ref[j:j + 1, :] = jnp.sum(dxconv * src, axis=0, keepdims=True)
        dcb_ref[...] = jnp.sum(dxconv, axis=0, keepdims=True)

    col = lambda off: pl.BlockSpec((S, 128), lambda j: (0, off + j))
    vec = pl.BlockSpec((1, 128), lambda j: (0, j))
    mat = pl.BlockSpec((None, 128, 128), lambda j: (j, 0, 0))
    vshape = jax.ShapeDtypeStruct((1, W), F32)
    mshape = jax.ShapeDtypeStruct((8, 128, 128), F32)
    return _pcall(
        body, grid=(8,),
        in_specs=[col(0), col(8), col(0), pl.BlockSpec((None, S, 128), lambda j: (0, 0, j)),
                  pl.BlockSpec((4, 128), lambda j: (0, j)), vec, mat, mat, mat, mat, vec, vec, vec],
        out_specs=[pl.BlockSpec((2, S, 128), lambda j: (0, 0, j)), pl.BlockSpec((4, 128), lambda j: (0, j)), vec,
                   mat, mat, vec, vec, vec],
        out_shape=[jax.ShapeDtypeStruct((4, S, W), BF), jax.ShapeDtypeStruct((4, W), F32), vshape, mshape, mshape,
                   vshape, vshape, vshape],
        scratch=[pltpu.VMEM((S, 128), F32)] * 3, name=name, vmem=56, comm=comm,
    )(h, h, hst, dmix3, cw, cb.reshape(1, W), wa, wx, wat, wxt, ba.reshape(1, W), bx.reshape(1, W), lam.reshape(1, W))


def _odd_d_bwd(h, dmix3, dh4, wp, wpt, dscale, name):
    S = h.shape[0]

    def body(xd_ref, dg_ref, dd_ref, wp_ref, wpt_ref, ds_ref, dh_in, dh_ref, dwp_ref, dds_ref):
        g = pl.program_id(0)
        row = lax.broadcasted_iota(jnp.int32, (S, 256), 0)
        xd = xd_ref[...]
        cnt = jnp.minimum(row + 1, jnp.left_shift(2, g)).astype(F32)
        pooled = _pool_sums(xd, g, row, _shift_down) / cnt - xd
        pb = pooled.astype(BF)
        mixed = _dot(pb, wp_ref[...])
        dg = dg_ref[...]
        sg, dsg = _silu_grad(dg)
        dd = dd_ref[...]
        dmixed = dd * ds_ref[...] * sg
        dds_ref[...] = jnp.sum(dd * mixed * sg, axis=0, keepdims=True)
        dh_ref[1] = (dd * mixed * ds_ref[...] * dsg).astype(BF)
        dmb = dmixed.astype(BF)
        dpooled = _dot(dmb, wpt_ref[...])
        dwp_ref[...] = _dot_tn(pb, dmb)
        dh_ref[0] = (_pool_sums(dpooled / cnt, g, row, _shift_up) - dpooled).astype(BF)

    col = lambda off: pl.BlockSpec((S, 256), lambda g: (0, off + g))
    mat = pl.BlockSpec((None, 256, 256), lambda g: (g, 0, 0))
    vec = pl.BlockSpec((1, 256), lambda g: (0, g))
    return pl.pallas_call(
        body, grid=(4,),
        in_specs=[col(8), col(12), pl.BlockSpec((None, S, 256), lambda g: (1, 0, g)), mat, mat, vec, ANY],
        out_specs=[pl.BlockSpec((2, S, 256), lambda g: (1, 0, g)), mat, vec],
        out_shape=[jax.ShapeDtypeStruct((4, S, W), BF), jax.ShapeDtypeStruct((4, 256, 256), F32),
                   jax.ShapeDtypeStruct((1, W), F32)],
        input_output_aliases={6: 0}, name=name, compiler_params=_cp(56),
    )(h, h, dmix3, wp, wpt, dscale.reshape(1, W), dh4)


def _peer(d):
    x, y, c = lax.axis_index("x"), lax.axis_index("y"), lax.axis_index("c")
    px = 1 - x if d & 4 else x
    py = 1 - y if d & 2 else y
    pc = 1 - c if d & 1 else c
    return (px, py, pc), 4 * px + 2 * py + pc


class _GatherAll(_Comm):
    def __init__(self, xs):
        self.peers = EVERYONE
        self.inputs = [xs]
        self.out_shapes = [jax.ShapeDtypeStruct((N_DEV,) + xs.shape, xs.dtype)]
        self.sem_shapes = [pltpu.SemaphoreType.DMA((N_DEV - 1,)), pltpu.SemaphoreType.DMA((N_DEV - 1,)),
                           pltpu.SemaphoreType.DMA]

    def copies(self, ins, outs, sems):
        (x_ref,), (out_ref,), (send, recv, loc) = ins, outs, sems
        _, me = _peer(0)
        res = [pltpu.make_async_copy(x_ref, out_ref.at[me], loc)]
        for d in range(1, N_DEV):
            peer, _ = _peer(d)
            res.append(pltpu.make_async_remote_copy(src_ref=x_ref, dst_ref=out_ref.at[me], send_sem=send.at[d - 1],
                                                    recv_sem=recv.at[d - 1], device_id=peer, device_id_type=MESH))
        return res


class _ExchangeAll(_Comm):
    def __init__(self, g8):
        self.peers = EVERYONE
        self.inputs = [g8]
        self.out_shapes = [jax.ShapeDtypeStruct(g8.shape, g8.dtype)]
        self.sem_shapes = [pltpu.SemaphoreType.DMA((N_DEV - 1,)), pltpu.SemaphoreType.DMA((N_DEV - 1,)),
                           pltpu.SemaphoreType.DMA]

    def copies(self, ins, outs, sems):
        (g_ref,), (out_ref,), (send, recv, loc) = ins, outs, sems
        _, me = _peer(0)
        res = [pltpu.make_async_copy(g_ref.at[me], out_ref.at[0], loc)]
        for d in range(1, N_DEV):
            peer, pidx = _peer(d)
            res.append(pltpu.make_async_remote_copy(src_ref=g_ref.at[pidx], dst_ref=out_ref.at[d], send_sem=send.at[d - 1],
                                                    recv_sem=recv.at[d - 1], device_id=peer, device_id_type=MESH))
        return res


def _sum8(r8, tr, name):
    _, R, C = r8.shape
    tr = min(tr, R)
    assert R % tr == 0

    def body(r_ref, o_ref):
        acc = r_ref[0]
        for d in range(1, N_DEV):
            acc = acc + r_ref[d]
        o_ref[...] = acc

    return pl.pallas_call(
        body, grid=(R // tr,), in_specs=[pl.BlockSpec((N_DEV, tr, C), lambda i: (0, i, 0))],
        out_specs=pl.BlockSpec((tr, C), lambda i: (i, 0)), out_shape=jax.ShapeDtypeStruct((R, C), F32),
        name=name, compiler_params=_cp(),
    )(r8)


def _adamw_math(w, g, m, v):
    m2 = B1 * m + (1.0 - B1) * g
    v2 = B2 * v + (1.0 - B2) * (g * g)
    m_hat = m2 / (1.0 - B1 ** STEP)
    v_hat = v2 / (1.0 - B2 ** STEP)
    return -LR * (m_hat / (jnp.sqrt(v_hat) + ADAM_EPS) + WD * w), m2, v2


def _adamw_many(ws, gs, ms, vs, name):
    n = len(ws)

    def body(*refs):
        for i in range(n):
            d, m2, v2 = _adamw_math(refs[i][...], refs[n + i][...], refs[2 * n + i][...], refs[3 * n + i][...])
            refs[4 * n + i][...] = d
            refs[5 * n + i][...] = m2
            refs[6 * n + i][...] = v2

    vmem = pl.BlockSpec(memory_space=pltpu.VMEM)
    shapes = [jax.ShapeDtypeStruct(w.shape, F32) for w in ws]
    res = pl.pallas_call(body, in_specs=[vmem] * (4 * n), out_specs=[vmem] * (3 * n), out_shape=shapes * 3, name=name,
                         compiler_params=_cp())(*ws, *gs, *ms, *vs)
    return res[:n], res[n:2 * n], res[2 * n:]


def _adamw(w3, gs, m3, v3, tr, name, comm=None):
    _, R, C = w3.shape
    n = 2 if isinstance(gs[0], tuple) else 1

    def gradient(refs):
        if n == 1:
            return refs[0][...]
        s_ref, r_ref = refs
        return ((s_ref[...].astype(F32) + r_ref[0].astype(F32)) + r_ref[1].astype(F32)) + r_ref[2].astype(F32)

    def body(w_ref, *rest):
        g_refs, (m_ref, v_ref, d_ref, m2_ref, v2_ref, g_ref) = rest[:2 * n], rest[2 * n:]
        g = jnp.where(pl.program_id(0) == 0, gradient(g_refs[:n]), gradient(g_refs[n:]))
        d_ref[...], m2_ref[...], v2_ref[...] = _adamw_math(w_ref[...], g, m_ref[...], v_ref[...])
        g_ref[...] = g

    blk = pl.BlockSpec((None, tr, C), lambda j, i: (j, i, 0))

    def grad_specs(layer):
        at = lambda j, i: jnp.where(j == layer, i, 0)
        if n == 1:
            return [pl.BlockSpec((tr, C), lambda j, i: (at(j, i), 0))]
        return [pl.BlockSpec((None, tr, C), lambda j, i: (0, at(j, i), 0)), pl.BlockSpec((3, tr, C), lambda j, i: (0, at(j, i), 0))]

    flat = [a for g in gs for a in (g if n == 2 else (g,))]
    shp = jax.ShapeDtypeStruct((2, R, C), F32)
    return _pcall(body, grid=(2, R // tr), in_specs=[blk] + grad_specs(0) + grad_specs(1) + [blk, blk], out_specs=[blk] * 4,
                  out_shape=[shp] * 4, name=name, comm=comm)(w3, *flat, m3, v3)


def _rep_pack(a):
    n = a.size
    pad = (-n) % 1024
    f = a.reshape(-1)
    if pad:
        f = jnp.concatenate([f, jnp.zeros((pad,), a.dtype)])
    return f.reshape(N_DEV, -1, 128)


def _rep_unpack(p, shape):
    n = 1
    for s in shape:
        n *= s
    return p.reshape(-1)[:n].reshape(shape)


def _sh_pack(a, axis):
    shp = a.shape
    a = a.reshape(shp[:axis] + (N_DEV, shp[axis] // N_DEV) + shp[axis + 1:])
    return jnp.moveaxis(a, axis, 0).reshape(N_DEV, -1, 128)


def _sh_unpack(p, shape, axis):
    a = p.reshape((N_DEV,) + shape[:axis] + (shape[axis] // N_DEV,) + shape[axis + 1:])
    return jnp.moveaxis(a, 0, axis).reshape(shape)


def _pad_rows(a, mult=8):
    pad = (-a.shape[-2]) % mult
    if pad:
        a = jnp.concatenate([a, jnp.zeros(a.shape[:-2] + (pad, a.shape[-1]), a.dtype)], axis=-2)
    return a


REP = ["even_a_ln_g", "even_a_ln_b", "even_a_ws", "even_a_bs", "even_b_sinks", "even_ln_g", "even_ln_b",
       "odd_w_a", "odd_w_x"]
SH = [("odd_conv_w", (2, 4, W), 2), ("odd_conv_b", (2, W), 1), ("odd_b_a", (2, W), 1), ("odd_b_x", (2, W), 1),
      ("odd_lam", (2, W), 1), ("odd_w_pool", (2, 4, 256, 256), 2), ("odd_d_scale", (2, W), 1),
      ("odd_ln_g", (2, D), 1), ("odd_ln_b", (2, D), 1)]
BIG = ["even_w_in", "even_w_out", "odd_w_in", "odd_w_out"]
NAMES = ["even_w_in", "even_a_ln_g", "even_a_ln_b", "even_a_ws", "even_a_bs", "even_b_sinks", "even_w_out",
         "even_ln_g", "even_ln_b", "odd_w_in", "odd_conv_w", "odd_conv_b", "odd_w_a", "odd_b_a", "odd_w_x", "odd_b_x",
         "odd_lam", "odd_w_pool", "odd_d_scale", "odd_w_out", "odd_ln_g", "odd_ln_b"]


def _rope_table(positions):
    inv = ROPE_THETA ** (-jnp.arange(0, 16, 2, dtype=F32) / 16)
    f = jnp.arange(128) % 64
    ang = positions.astype(F32)[:, None] * inv[f % 8][None, :]
    cos, sin = jnp.cos(ang), jnp.sin(ang)
    return jnp.concatenate([jnp.where(f < 16, cos, 1.0), jnp.where(f < 8, -sin, 0.0),
                            jnp.where((f >= 8) & (f < 16), sin, 0.0)], axis=1)


def kernel(x, positions, even_w_in, even_a_ln_g, even_a_ln_b, even_a_ws, even_a_bs, even_b_sinks, even_w_out, even_ln_g, even_ln_b, odd_w_in, odd_conv_w, odd_conv_b, odd_w_a, odd_b_a, odd_w_x, odd_b_x, odd_lam, odd_w_pool, odd_d_scale, odd_w_out, odd_ln_g, odd_ln_b, loss_target, m_even_w_in, m_even_a_ln_g, m_even_a_ln_b, m_even_a_ws, m_even_a_bs, m_even_b_sinks, m_even_w_out, m_even_ln_g, m_even_ln_b, m_odd_w_in, m_odd_conv_w, m_odd_conv_b, m_odd_w_a, m_odd_b_a, m_odd_w_x, m_odd_b_x, m_odd_lam, m_odd_w_pool, m_odd_d_scale, m_odd_w_out, m_odd_ln_g, m_odd_ln_b, v_even_w_in, v_even_a_ln_g, v_even_a_ln_b, v_even_a_ws, v_even_a_bs, v_even_b_sinks, v_even_w_out, v_even_ln_g, v_even_ln_b, v_odd_w_in, v_odd_conv_w, v_odd_conv_b, v_odd_w_a, v_odd_b_a, v_odd_w_x, v_odd_b_x, v_odd_lam, v_odd_w_pool, v_odd_d_scale, v_odd_w_out, v_odd_ln_g, v_odd_ln_b):
    args = (even_w_in, even_a_ln_g, even_a_ln_b, even_a_ws, even_a_bs, even_b_sinks, even_w_out, even_ln_g, even_ln_b,
            odd_w_in, odd_conv_w, odd_conv_b, odd_w_a, odd_b_a, odd_w_x, odd_b_x, odd_lam, odd_w_pool, odd_d_scale,
            odd_w_out, odd_ln_g, odd_ln_b)
    margs = (m_even_w_in, m_even_a_ln_g, m_even_a_ln_b, m_even_a_ws, m_even_a_bs, m_even_b_sinks, m_even_w_out,
             m_even_ln_g, m_even_ln_b, m_odd_w_in, m_odd_conv_w, m_odd_conv_b, m_odd_w_a, m_odd_b_a, m_odd_w_x,
             m_odd_b_x, m_odd_lam, m_odd_w_pool, m_odd_d_scale, m_odd_w_out, m_odd_ln_g, m_odd_ln_b)
    vargs = (v_even_w_in, v_even_a_ln_g, v_even_a_ln_b, v_even_a_ws, v_even_a_bs, v_even_b_sinks, v_even_w_out,
             v_even_ln_g, v_even_ln_b, v_odd_w_in, v_odd_conv_w, v_odd_conv_b, v_odd_w_a, v_odd_b_a, v_odd_w_x,
             v_odd_b_x, v_odd_lam, v_odd_w_pool, v_odd_d_scale, v_odd_w_out, v_odd_ln_g, v_odd_ln_b)
    wts = dict(zip(NAMES, args))
    mom = dict(zip(NAMES, margs))
    var = dict(zip(NAMES, vargs))
    S = x.shape[1]
    x0 = x[0]
    rope = _rope_table(positions[0])

    kinds = ("even", "odd", "even", "odd")
    blk_in = [jnp.transpose(wts[kinds[l] + "_w_in"][l // 2]).astype(BF) for l in range(4)]
    blk_out = [wts[kinds[l] + "_w_out"][l // 2].astype(BF) for l in range(4)]
    sh_local = _pad_rows(jnp.concatenate([wts[nm].reshape(-1, 128) for nm, _, _ in SH], axis=0), 16)
    me = 4 * lax.axis_index("x") + 2 * lax.axis_index("y") + lax.axis_index("c")
    own_slot = lambda blk: lax.dynamic_update_slice(lax.empty((N_DEV,) + blk.shape, blk.dtype), blk[None], (me, 0, 0))
    reg = {"blk_small": sh_local, "w_small": own_slot(sh_local)}
    sched = _Sched(reg)
    for l in range(4):
        reg[f"blk_in{l}"], reg[f"blk_out{l}"] = blk_in[l], blk_out[l]
        reg[f"w_in{l}"], reg[f"w_out{l}"] = own_slot(blk_in[l]), own_slot(blk_out[l])
    sched.add(_rows("blk_in0", "w_in0", "ag1", blk_in[0].shape[0], ROW_CHUNK[blk_in[0].shape[0]]))
    sched.add(_rows("blk_small", "w_small", "ag1", sh_local.shape[0], sh_local.shape[0]))
    for l in range(4):
        sched.add(_rows(f"blk_out{l}", f"w_out{l}", "ag1", D // N_DEV, ROW_CHUNK[D // N_DEV]))
        if l < 3:
            r = blk_in[l + 1].shape[0]
            sched.add(_rows(f"blk_in{l + 1}", f"w_in{l + 1}", "ag1", r, ROW_CHUNK[r]))

    def gathered(dst, blk):
        sched.flush(dst, FLUSH_EXTRA_US)
        return reg.pop(dst)

    wt_in0 = gathered("w_in0", blk_in[0]).reshape(-1, D)
    full = {nm: wts[nm] for nm in REP}

    def gather_small():
        sh_all = gathered("w_small", sh_local)
        off = 0
        for nm, shape, axis in SH:
            r = wts[nm].size // 128
            full[nm] = _sh_unpack(sh_all[:, off:off + r, :], shape, axis)
            off += r

    saved = []
    wt_in, w_out = [wt_in0, None, None, None], [None] * 4
    xf, xb = x0, x0.astype(BF)
    fwd = lambda name: FWD_OVERBOOK * CARRY_US[name]
    for layer in range(4):
        j = layer // 2
        kind = kinds[layer]
        if wt_in[layer] is None:
            wt_in[layer] = gathered(f"w_in{layer}", blk_in[layer]).reshape(-1, D)
        h = sched.run(_mm_nt, fwd("mm_h_" + kind), xb, wt_in[layer], 1024, 768 if kind == "even" else 512, "mm_h_" + kind)
        if kind == "even":
            bsb = jnp.broadcast_to(full["even_a_bs"][j][:, :, None], (8, 128, 128))
            mix3, o, l = sched.run(_even_fwd, fwd("even_fwd"), h, rope, full["even_a_ln_g"][j], full["even_a_ln_b"][j],
                                   full["even_a_ws"][j], bsb, full["even_b_sinks"][j], "even_fwd")
            extra = (o, l, bsb)
        else:
            if "odd_lam" not in full:
                gather_small()
            wa, wx = full["odd_w_a"][j].astype(BF), full["odd_w_x"][j].astype(BF)
            wp = full["odd_w_pool"][j].astype(BF)
            mix3, hst = sched.run(_odd_c_fwd, fwd("odd_c_fwd"), h, full["odd_conv_w"][j], full["odd_conv_b"][j], wa, wx,
                                  full["odd_b_a"][j], full["odd_b_x"][j], full["odd_lam"][j], "odd_c_fwd")
            mix3 = _odd_d_fwd(h, mix3, wp, full["odd_d_scale"][j], "odd_d_fwd")
            extra = (hst, wa, wx, wp)
        w_out[layer] = gathered(f"w_out{layer}", blk_out[layer]).reshape(D, D)
        z, xn, xnb = sched.run(_mm_out_ln, fwd("mm_out_ln"), mix3, w_out[layer], xf, full[kind + "_ln_g"][j],
                               full[kind + "_ln_b"][j], "mm_out_ln")
        saved.append((xb, h, mix3, z, extra))
        xf, xb = xn, xnb

    dxn = xf

    gsum = {nm: [None, None] for nm in NAMES}

    chip_sums = {}
    sched.overhang = 0.15

    waiting = []

    def chip_sum(g, tag, key):
        r = g.shape[0] // N_DEV
        reg["g_" + key] = g.reshape(N_DEV, r, D)
        sched.add(_rows("g_" + key, "d_" + key, "rsd", r, r), first=True)
        waiting.append((key, tag))

    def add_arrived():
        for key, tag in list(waiting):
            if "d_" + key in reg and not sched.pending("d_" + key):
                waiting.remove((key, tag))
                g8 = reg.pop("g_" + key)
                chip_sums[key] = reg["s_" + key] = _add_pairs(g8, reg.pop("d_" + key), "rs_add_" + tag)
                sched.add(_rows("s_" + key, "r_" + key, "rs", g8.shape[1], ROW_CHUNK[g8.shape[1]] // 2))

    sched.after_landing = add_arrived

    def reduced(key):
        sched.flush("d_" + key, FLUSH_EXTRA_US)
        sched.flush("r_" + key, FLUSH_EXTRA_US)
        return chip_sums[key], reg.pop("r_" + key)

    for layer in (3, 2, 1, 0):
        j = layer // 2
        xb, h, mix3, z, extra = saved[layer]
        kind = kinds[layer]
        if layer == 3:
            dz, dzb, dg, dbeta, part = sched.run(_ln_bwd, CARRY_US["ln_bwd"], dxn, z, full[kind + "_ln_g"][j], "loss_ln_bwd",
                                                 target=loss_target[0])
        else:
            dz, dzb, dg, dbeta = sched.run(_ln_bwd, CARRY_US["ln_bwd"], dxn, z, full[kind + "_ln_g"][j], "ln_bwd")
        gsum[kind + "_ln_g"][j] = dg.reshape(D)
        gsum[kind + "_ln_b"][j] = dbeta.reshape(D)
        chip_sum(sched.run(_mm_tn, CARRY_US["mm_dw_out"], mix3, dzb, 512, "mm_dw_out"), "w_out", f"out{layer}")
        dmix3 = sched.run(_mm_nt, CARRY_US["mm_dmix"], dzb, w_out[layer], 1024, 512, "mm_dmix", out3=True)
        if kind == "even":
            o, l, bsb = extra
            ws = full["even_a_ws"][j]
            dh, dws, dbs, dlng, dlnb, dsink = sched.run(
                _even_bwd, CARRY_US["even_bwd"], h, dmix3, o, l, rope, full["even_a_ln_g"][j], full["even_a_ln_b"][j],
                ws, jnp.swapaxes(ws, 1, 2), bsb, full["even_b_sinks"][j], "even_bwd")
            gsum["even_a_ws"][j] = dws
            gsum["even_a_bs"][j] = jnp.transpose(dbs[:, :8])
            gsum["even_a_ln_g"][j] = dlng.reshape(W)
            gsum["even_a_ln_b"][j] = dlnb.reshape(W)
            gsum["even_b_sinks"][j] = dsink[0, :16]
            if layer == 0:
                rep_rows = [_rep_pack(jnp.stack(gsum[nm]).reshape(wts[nm].shape)) for nm in REP]
                sh_rows = [_sh_pack(jnp.stack(gsum[nm]).reshape(shape), axis) for nm, shape, axis in SH]
                packed = _pad_rows(jnp.concatenate(rep_rows + sh_rows, axis=1))
                gw, (small8, parts) = _mm_tn(dh, xb, 384, "mm_dw_in_even", comm=_Join([_ExchangeAll(packed), _GatherAll(part)]))
                loss = jnp.sum(parts[:, 0, 0]) * (0.5 / D)
            else:
                gw = sched.run(_mm_tn, CARRY_US["mm_dw_in_even"], dh, xb, 384, "mm_dw_in_even")
            chip_sum(gw, "w_in_even", f"in{layer}")
            if layer == 0:
                n_rep = sum(p.shape[1] for p in rep_rows)
                red = _sum8(small8, 1 << 20, "sum_small")
                (rep_all,) = sched.flush("d_in0", FLUSH_EXTRA_US, beside=_GatherAll(_pad_rows(red[:n_rep])))
                sched.overhang = 0.6
            dxn = sched.run(_mm_nn_res, CARRY_US["mm_dx_even"], dh, wt_in[layer], dz, 512, 1024, "mm_dx_even")
        else:
            hst, wa, wx, wp = extra
            dh4, dcw, dcb, dwa, dwx, dba, dbx, dlam = sched.run(
                _odd_c_bwd, CARRY_US["odd_c_bwd"], h, hst, dmix3, full["odd_conv_w"][j], full["odd_conv_b"][j], wa, wx,
                jnp.swapaxes(wa, 1, 2), jnp.swapaxes(wx, 1, 2), full["odd_b_a"][j], full["odd_b_x"][j], full["odd_lam"][j],
                "odd_c_bwd")
            dh4, dwp, dds = _odd_d_bwd(h, dmix3, dh4, wp, jnp.swapaxes(wp, 1, 2), full["odd_d_scale"][j], "odd_d_bwd")
            gsum["odd_conv_w"][j], gsum["odd_conv_b"][j] = dcw, dcb.reshape(W)
            gsum["odd_w_a"][j], gsum["odd_w_x"][j] = dwa, dwx
            gsum["odd_b_a"][j], gsum["odd_b_x"][j], gsum["odd_lam"][j] = dba.reshape(W), dbx.reshape(W), dlam.reshape(W)
            gsum["odd_w_pool"][j], gsum["odd_d_scale"][j] = dwp, dds.reshape(W)
            chip_sum(sched.run(_mm_tn, CARRY_US["mm_dw_in_odd"], dh4, xb, 512, "mm_dw_in_odd"), "w_in_odd", f"in{layer}")
            dxn = sched.run(_mm_nn_res, CARRY_US["mm_dx_odd"], dh4, wt_in[layer], dz, 512, 1024, "mm_dx_odd")
    grad_x = dxn[None]

    out_g, out_d, out_m, out_v = {}, {}, {}, {}
    for nm, kind, what, layers in (("odd_w_out", "odd", "out", (1, 3)), ("even_w_out", "even", "out", (0, 2)),
                                   ("odd_w_in", "odd", "in", (1, 3)), ("even_w_in", "even", "in", (0, 2))):
        gl = [reduced(f"{what}{l}") for l in layers]
        if nm == "even_w_in":
            view = lambda a: jnp.transpose(a, (0, 2, 1))
            res, _ = _adamw(view(wts[nm]), gl, view(mom[nm]), view(var[nm]), 112, f"adamw_{nm}")
            res = [view(a) for a in res]
        elif what == "in":
            gs = [jnp.transpose(_rs_final(s4, r3, "rs_final_w_in_odd")) for s4, r3 in gl]
            res, _ = _adamw(wts[nm], gs, mom[nm], var[nm], 512, f"adamw_{nm}")
        else:
            res = sched.run(_adamw, CARRY_US["adamw_" + nm], wts[nm], gl, mom[nm], var[nm], 128, f"adamw_{nm}")
        out_d[nm], out_m[nm], out_v[nm], out_g[nm] = res

    g_small = {}
    off = 0
    for nm, p in zip(REP, rep_rows):
        r = p.shape[1]
        g_small[nm] = _rep_unpack(rep_all[:, off:off + r, :], wts[nm].shape)
        off += r
    off = n_rep
    for (nm, shape, axis), p in zip(SH, sh_rows):
        r = p.shape[1]
        g_small[nm] = red[off:off + r].reshape(wts[nm].shape)
        off += r

    def rows(a):
        f = a.reshape(-1)
        pad = (-f.shape[0]) % 128
        if pad:
            f = jnp.concatenate([f, jnp.zeros((pad,), a.dtype)])
        return f.reshape(-1, 128)

    small = REP + [nm for nm, _, _ in SH]
    each = lambda src: [rows(src[nm]) for nm in small]
    d2, m2, v2 = _adamw_many(each(wts), each(g_small), each(mom), each(var), "adamw_small")
    for i, nm in enumerate(small):
        n, shp = wts[nm].size, wts[nm].shape
        take = lambda a: a.reshape(-1)[:n].reshape(shp)
        out_g[nm], out_d[nm], out_m[nm], out_v[nm] = g_small[nm], take(d2[i]), take(m2[i]), take(v2[i])

    return (loss, grad_x, *[out_g[nm] for nm in NAMES], *[out_d[nm] for nm in NAMES],
            *[out_m[nm] for nm in NAMES], *[out_v[nm] for nm in NAMES])
```

```python
import functools

import jax
import jax.numpy as jnp
from jax import lax
from jax.experimental import pallas as pl
from jax.experimental.pallas import tpu as pltpu

F32 = jnp.float32
BF = jnp.bfloat16
MESH = pl.DeviceIdType.MESH
ANY = pl.BlockSpec(memory_space=pl.ANY)

N_DEV = 8
D = 2048
W = 1024
EVEN_IN = 5376
ODD_IN = 4096
CHUNK = 128
ALPHA = (2 * 4) ** 0.25
LN_EPS = 1e-5
ROPE_THETA = 500000.0
LRU_C = 8.0
LR, B1, B2, ADAM_EPS, WD, STEP = 0.001, 0.9, 0.999, 1e-08, 0.01, 10
NEG = -1e30
HEAD_COLS = 4


def _cp(vmem_mb=48, collective_id=None):
    return pltpu.CompilerParams(vmem_limit_bytes=vmem_mb * 1024 * 1024, collective_id=collective_id)


def _sig(x):
    return jax.nn.sigmoid(x)


def _silu_grad(x):
    s = _sig(x)
    return x * s, s * (1.0 + x * (1.0 - s))


def _dot(a, b):
    return jnp.dot(a, b, preferred_element_type=F32)


def _dot_nt(a, b):
    return lax.dot_general(a, b, (((1,), (1,)), ((), ())), preferred_element_type=F32)


def _dot_tn(a, b):
    return lax.dot_general(a, b, (((0,), (0,)), ((), ())), preferred_element_type=F32)


def _coords():
    return lax.axis_index("x"), lax.axis_index("y"), lax.axis_index("c")


def _chip(j):
    x, y, _ = _coords()
    return (1 - x if j & 2 else x), (1 - y if j & 1 else y)


X_NB, Y_NB, DIAG, SIB = 4, 2, 6, 1
EVERYONE = frozenset(range(1, N_DEV))
BARRIER_IDS = {}


class _Comm:
    def collective_id(self):
        return BARRIER_IDS.setdefault(frozenset(self.peers), len(BARRIER_IDS))

    def start(self, ins, outs, sems):
        barrier = pltpu.get_barrier_semaphore()
        for d in sorted(self.peers):
            pl.semaphore_signal(barrier, inc=1, device_id=_peer(d)[0], device_id_type=MESH)
        pl.semaphore_wait(barrier, len(self.peers))
        for cp in self.copies(ins, outs, sems):
            cp.start()

    def wait(self, ins, outs, sems):
        for cp in self.copies(ins, outs, sems):
            cp.wait()


class _Join(_Comm):
    def __init__(self, parts):
        self.parts = list(parts)
        self.peers = frozenset().union(*[p.peers for p in self.parts])
        self.inputs = [a for p in self.parts for a in p.inputs]
        self.out_shapes = [s for p in self.parts for s in p.out_shapes]
        self.sem_shapes = [s for p in self.parts for s in p.sem_shapes]
        self.aliases = {}
        i0 = o0 = 0
        for p in self.parts:
            for i, o in getattr(p, "aliases", {}).items():
                self.aliases[i0 + i] = o0 + o
            i0, o0 = i0 + len(p.inputs), o0 + len(p.out_shapes)

    def copies(self, ins, outs, sems):
        res = []
        i0 = o0 = s0 = 0
        for p in self.parts:
            ni, no, ns = len(p.inputs), len(p.out_shapes), len(p.sem_shapes)
            res += p.copies(ins[i0:i0 + ni], outs[o0:o0 + no], sems[s0:s0 + ns])
            i0, o0, s0 = i0 + ni, o0 + no, s0 + ns
        return res


ROWS_US = {"ag1": 0.104, "ag2": 0.052, "agd": 0.027, "rsd": 0.027, "rs": 0.205}
N_COPIES = {"ag1": 2, "ag2": 2, "agd": 4, "rsd": 4, "rs": 3}
TASK_PEERS = {"ag1": {X_NB, Y_NB}, "ag2": {X_NB, Y_NB}, "agd": {SIB}, "rsd": {SIB}, "rs": {X_NB, Y_NB, DIAG}}
ROW_CHUNK = {672: 224, 512: 128, 256: 128}
CARRY_US = {"mm_h_even": 58, "mm_h_odd": 47, "even_fwd": 42, "odd_c_fwd": 37, "mm_out_ln": 33, "ln_bwd": 23, "mm_dmix": 26,
            "mm_dw_out": 25, "even_bwd": 90, "odd_c_bwd": 58, "mm_dw_in_even": 58, "mm_dw_in_odd": 44, "mm_dx_even": 66,
            "mm_dx_odd": 55, "adamw_even_w_out": 11, "adamw_odd_w_out": 11}
FWD_OVERBOOK = 1.15
FLUSH_EXTRA_US = 60.0


def _cost_us(task, reg):
    kind, src, _, lo, hi = task
    return ROWS_US[kind] * (hi - lo) * reg[src].shape[-1] * reg[src].dtype.itemsize / 4096.0


class _Copies(_Comm):
    def __init__(self, tasks, reg):
        self.tasks = list(tasks)
        self.out_names, self.in_names = [], []
        for kind, src, dst, lo, hi in self.tasks:
            if dst not in self.out_names:
                self.out_names.append(dst)
        for kind, src, dst, lo, hi in self.tasks:
            if src not in self.out_names and src not in self.in_names:
                self.in_names.append(src)
        self.out_shapes, self.aliases = [], {}
        for o, dst in enumerate(self.out_names):
            if dst in reg:
                self.aliases[len(self.in_names)] = o
                self.in_names.append(dst)
                self.out_shapes.append(jax.ShapeDtypeStruct(reg[dst].shape, reg[dst].dtype))
            else:
                kind, src = next((t[0], t[1]) for t in self.tasks if t[2] == dst)
                shape = ({"rsd": 4, "rs": 3}[kind],) + reg[src].shape[1:]
                self.out_shapes.append(jax.ShapeDtypeStruct(shape, reg[src].dtype))
        self.inputs = [reg[nm] for nm in self.in_names]
        n = sum(N_COPIES[t[0]] for t in self.tasks)
        self.sem_shapes = [pltpu.SemaphoreType.DMA((n,)), pltpu.SemaphoreType.DMA((n,))]
        self.peers = frozenset().union(*[TASK_PEERS[t[0]] for t in self.tasks])

    def copies(self, ins, outs, sems):
        send, recv = sems
        x, y, c = _coords()
        me = 4 * x + 2 * y + c
        xn, yn = (1 - x, y, c), (x, 1 - y, c)
        at_xn, at_yn = 4 * (1 - x) + 2 * y + c, 4 * x + 2 * (1 - y) + c
        ref = dict(zip(self.in_names, ins))
        ref.update(zip(self.out_names, outs))
        res = []

        def copy(src, dst, to):
            i = len(res)
            res.append(pltpu.make_async_remote_copy(src_ref=src, dst_ref=dst, send_sem=send.at[i], recv_sem=recv.at[i],
                                                    device_id=to, device_id_type=MESH))

        for kind, src, dst, lo, hi in self.tasks:
            n = hi - lo
            if kind == "ag1":
                for to in (xn, yn):
                    copy(ref[src].at[pl.ds(lo, n)], ref[dst].at[me, pl.ds(lo, n)], to)
            elif kind == "ag2":
                h = n // 2
                first, second = ref[dst].at[at_xn, pl.ds(lo, h)], ref[dst].at[at_yn, pl.ds(lo + h, n - h)]
                copy(first, first, yn)
                copy(second, second, xn)
            elif kind == "agd":
                for j in range(4):
                    px, py = _chip(j)
                    rows = ref[dst].at[4 * px + 2 * py + c, pl.ds(lo, n)]
                    copy(rows, rows, (x, y, 1 - c))
            elif kind == "rsd":
                for j in range(4):
                    px, py = _chip(j)
                    copy(ref[src].at[4 * px + 2 * py + 1 - c, pl.ds(lo, n)], ref[dst].at[j, pl.ds(lo, n)], (x, y, 1 - c))
            else:
                for j in (1, 2, 3):
                    px, py = _chip(j)
                    copy(ref[src].at[j, pl.ds(lo, n)], ref[dst].at[j - 1, pl.ds(lo, n)], (px, py, c))
        return res


class _Sched:
    def __init__(self, reg):
        self.reg, self.queue, self.later = reg, [], []
        self.overhang = 0.5
        self.after_landing = None

    def add(self, tasks, first=False):
        self.queue = list(tasks) + self.queue if first else self.queue + list(tasks)

    def pending(self, dst):
        return any(t[2] == dst for t in self.queue + self.later)

    def take(self, budget_us, must=None, overhang=0.5):
        self.queue, self.later = self.later + self.queue, []
        picked, us = [], 0.0
        rest = []
        for t in self.queue:
            cost = _cost_us(t, self.reg)
            if (must is not None and t[2] == must) or us + (1.0 - overhang) * cost <= budget_us:
                picked.append(t)
                us += cost
                if t[0] in ("ag1", "ag2"):
                    self.later.append(({"ag1": "ag2", "ag2": "agd"}[t[0]], t[2], t[2], t[3], t[4]))
            else:
                rest.append(t)
        self.queue = rest
        return _Copies(picked, self.reg) if picked else None

    def landed(self, comm, got):
        if comm is not None:
            for nm, a in zip(comm.out_names, got):
                self.reg[nm] = a
        if self.after_landing is not None:
            self.after_landing()

    def run(self, builder, budget_us, *args, **kw):
        comm = self.take(budget_us, overhang=self.overhang)
        res, got = builder(*args, comm=comm, **kw)
        self.landed(comm, got)
        return res

    def flush(self, dst, budget_us=0.0, beside=None):
        res = []
        while self.pending(dst):
            comm = self.take(budget_us, must=dst)
            got = _comm_only(comm if beside is None else _Join([comm, beside]), "flush_" + dst)
            res, beside = got[len(comm.out_shapes):], None
            self.landed(comm, got[:len(comm.out_shapes)])
        return res


def _rows(name_src, name_dst, kind, n_rows, chunk):
    return [(kind, name_src, name_dst, lo, min(lo + chunk, n_rows)) for lo in range(0, n_rows, chunk)]


def _pcall(body, *, grid, in_specs, out_specs, out_shape, name, scratch=(), vmem=48, comm=None):
    in_specs, out_specs, out_shape, scratch = list(in_specs), list(out_specs), list(out_shape), list(scratch)
    if comm is None:
        call = pl.pallas_call(body, grid=grid, in_specs=in_specs, out_specs=out_specs, out_shape=out_shape,
                              scratch_shapes=scratch, name=name, compiler_params=_cp(vmem))
        return lambda *args: (call(*args), [])
    n_in, n_out, n_scr = len(in_specs), len(out_specs), len(scratch)
    c_in, c_out = len(comm.inputs), len(comm.out_shapes)
    aliases = {n_in + i: n_out + o for i, o in getattr(comm, "aliases", {}).items()}

    def wrapped(*refs):
        ins, cins = refs[:n_in], refs[n_in:n_in + c_in]
        o0 = n_in + c_in
        outs, couts = refs[o0:o0 + n_out], refs[o0 + n_out:o0 + n_out + c_out]
        s0 = o0 + n_out + c_out
        scr, sems = refs[s0:s0 + n_scr], refs[s0 + n_scr:]
        ids = [pl.program_id(a) for a in range(len(grid))]
        first = functools.reduce(jnp.logical_and, [i == 0 for i in ids])
        last = functools.reduce(jnp.logical_and, [i == g - 1 for i, g in zip(ids, grid)])

        @pl.when(first)
        def _():
            comm.start(cins, couts, sems)

        body(*ins, *outs, *scr)

        @pl.when(last)
        def _():
            comm.wait(cins, couts, sems)

    call = pl.pallas_call(wrapped, grid=grid, in_specs=in_specs + [ANY] * c_in, out_specs=out_specs + [ANY] * c_out,
                          out_shape=out_shape + list(comm.out_shapes), scratch_shapes=scratch + list(comm.sem_shapes),
                          input_output_aliases=aliases, name=name, compiler_params=_cp(vmem, comm.collective_id()))

    def run(*args):
        res = call(*args, *comm.inputs)
        return res[:n_out], res[n_out:]

    return run


def _comm_only(comm, name):
    c_in, c_out = len(comm.inputs), len(comm.out_shapes)

    def body(*refs):
        cins, couts, sems = refs[:c_in], refs[c_in:c_in + c_out], refs[c_in + c_out:]
        comm.start(cins, couts, sems)
        comm.wait(cins, couts, sems)

    return pl.pallas_call(body, in_specs=[ANY] * c_in, out_specs=[ANY] * c_out, out_shape=list(comm.out_shapes),
                          scratch_shapes=list(comm.sem_shapes), input_output_aliases=dict(getattr(comm, "aliases", {})),
                          name=name, compiler_params=pltpu.CompilerParams(collective_id=comm.collective_id()))(*comm.inputs)


def _chip_blocks():
    _, _, c = _coords()
    return jnp.stack([4 * px + 2 * py + c for px, py in map(_chip, range(4))]).astype(jnp.int32)


def _add_pairs(g8, b4, name):
    _, R, C = b4.shape

    def body(idx_ref, a_ref, b_ref, o_ref):
        o_ref[...] = (a_ref[...].astype(F32) + b_ref[...].astype(F32)).astype(BF)

    blk = pl.BlockSpec((None, R, C), lambda j, idx: (j, 0, 0))
    grid_spec = pltpu.PrefetchScalarGridSpec(
        num_scalar_prefetch=1, grid=(4,),
        in_specs=[pl.BlockSpec((None, R, C), lambda j, idx: (idx[j], 0, 0)), blk], out_specs=blk)
    return pl.pallas_call(body, grid_spec=grid_spec, out_shape=jax.ShapeDtypeStruct(b4.shape, BF), name=name,
                          compiler_params=_cp())(_chip_blocks(), g8, b4)


def _rs_final(s4, r3, name):
    _, R, C = s4.shape
    tr = R // 2

    def body(s_ref, r_ref, o_ref):
        o_ref[...] = ((s_ref[...].astype(F32) + r_ref[0].astype(F32)) + r_ref[1].astype(F32)) + r_ref[2].astype(F32)

    return pl.pallas_call(
        body, grid=(2,),
        in_specs=[pl.BlockSpec((None, tr, C), lambda i: (0, i, 0)), pl.BlockSpec((3, tr, C), lambda i: (0, i, 0))],
        out_specs=pl.BlockSpec((tr, C), lambda i: (i, 0)), out_shape=jax.ShapeDtypeStruct((R, C), F32),
        name=name, compiler_params=_cp())(s4, r3)


def _mm_nt(a, w, tm, tn, name, out3=False, comm=None):
    M, K = a.shape
    N = w.shape[0]
    tm = min(tm, M)

    def body(a_ref, w_ref, o_ref):
        o_ref[...] = _dot_nt(a_ref[...], w_ref[...])

    if out3:
        per = W // tn
        out_shape = jax.ShapeDtypeStruct((N // W, M, W), F32)
        out_spec = pl.BlockSpec((None, tm, tn), lambda i, j: (j // per, i, j % per))
    else:
        out_shape = jax.ShapeDtypeStruct((M, N), F32)
        out_spec = pl.BlockSpec((tm, tn), lambda i, j: (i, j))
    (res,), extra = _pcall(
        body, grid=(M // tm, N // tn),
        in_specs=[pl.BlockSpec((tm, K), lambda i, j: (i, 0)), pl.BlockSpec((tn, K), lambda i, j: (j, 0))],
        out_specs=[out_spec], out_shape=[out_shape], name=name, comm=comm)(a, w)
    return res, extra


def _mm_tn(a, b, tm, name, comm=None):
    K, N = b.shape
    if a.ndim == 3:
        M = a.shape[0] * W
        per = W // tm
        a_spec = pl.BlockSpec((None, K, tm), lambda i: (i // per, 0, i % per))
    else:
        M = a.shape[1]
        a_spec = pl.BlockSpec((K, tm), lambda i: (0, i))

    def body(a_ref, b_ref, o_ref):
        o_ref[...] = _dot_tn(a_ref[...], b_ref[...]).astype(BF)

    (out,), extra = _pcall(
        body, grid=(M // tm,),
        in_specs=[a_spec, pl.BlockSpec((K, N), lambda i: (0, 0))],
        out_specs=[pl.BlockSpec((tm, N), lambda i: (i, 0))],
        out_shape=[jax.ShapeDtypeStruct((M, N), BF)], name=name, vmem=56, comm=comm)(a, b)
    return out, extra


def _mm_nn_res(a, w, res, tm, tn, name, comm=None):
    K, N = w.shape
    if a.ndim == 3:
        P, M = a.shape[0], a.shape[1]
        tm = min(tm, M)
        a_spec = pl.BlockSpec((P, tm, W), lambda j, i: (0, i, 0))
    else:
        P, M = 0, a.shape[0]
        tm = min(tm, M)
        a_spec = pl.BlockSpec((tm, K), lambda j, i: (i, 0))

    def body(a_ref, w_ref, r_ref, o_ref):
        if P:
            d = _dot(a_ref[0], w_ref[0:W, :])
            for p in range(1, P):
                d = d + _dot(a_ref[p], w_ref[p * W:(p + 1) * W, :])
        else:
            d = _dot(a_ref[...], w_ref[...])
        o_ref[...] = ALPHA * r_ref[...] + d

    (out,), extra = _pcall(
        body, grid=(N // tn, M // tm),
        in_specs=[a_spec, pl.BlockSpec((K, tn), lambda j, i: (0, j)), pl.BlockSpec((tm, tn), lambda j, i: (i, j))],
        out_specs=[pl.BlockSpec((tm, tn), lambda j, i: (i, j))],
        out_shape=[jax.ShapeDtypeStruct((M, N), F32)], name=name, comm=comm)(a, w, res)
    return out, extra


def _mm_out_ln(mix3, w_out, x, g, b, name, comm=None):
    S = x.shape[0]
    tm = min(512, S)

    def body(m_ref, w_ref, x_ref, g_ref, b_ref, z_ref, xn_ref, xb_ref):
        acc = _dot(m_ref[0], w_ref[0:W, :]) + _dot(m_ref[1], w_ref[W:2 * W, :])
        z = ALPHA * x_ref[...] + acc
        mu = jnp.mean(z, axis=1, keepdims=True)
        zc = z - mu
        var = jnp.mean(zc * zc, axis=1, keepdims=True)
        xn = zc * lax.rsqrt(var + LN_EPS) * g_ref[...] + b_ref[...]
        z_ref[...] = z
        xn_ref[...] = xn
        xb_ref[...] = xn.astype(BF)

    row = pl.BlockSpec((tm, D), lambda i: (i, 0))
    vec = pl.BlockSpec((1, D), lambda i: (0, 0))
    return _pcall(
        body, grid=(S // tm,),
        in_specs=[pl.BlockSpec((2, tm, W), lambda i: (0, i, 0)),
                  pl.BlockSpec((D, D), lambda i: (0, 0), pipeline_mode=pl.Buffered(1)), row, vec, vec],
        out_specs=[row, row, row],
        out_shape=[jax.ShapeDtypeStruct((S, D), F32), jax.ShapeDtypeStruct((S, D), F32), jax.ShapeDtypeStruct((S, D), BF)],
        name=name, comm=comm)(mix3, w_out, x, g.reshape(1, D), b.reshape(1, D))


def _ln_bwd(dxn, z, g, name, comm=None, target=None):
    S = z.shape[0]
    tm = min(256, S)
    head = target is not None

    def body(*refs):
        if head:
            d_ref, t_ref, z_ref, g_ref, dz_ref, dzb_ref, dg_ref, db_ref, p_ref = refs
        else:
            d_ref, z_ref, g_ref, dz_ref, dzb_ref, dg_ref, db_ref = refs
        i = pl.program_id(0)
        zz = z_ref[...]
        mu = jnp.mean(zz, axis=1, keepdims=True)
        zc = zz - mu
        var = jnp.mean(zc * zc, axis=1, keepdims=True)
        rstd = lax.rsqrt(var + LN_EPS)
        xhat = zc * rstd
        dy = d_ref[...]
        if head:
            e = dy - t_ref[...]
            dy = e * (1.0 / D)

            @pl.when(i == 0)
            def _():
                p_ref[...] = jnp.zeros_like(p_ref)

            p_ref[...] += jnp.sum(jnp.sum(e * e, axis=1, keepdims=True), axis=0, keepdims=True)
        dyg = dy * g_ref[...]
        m1 = jnp.mean(dyg, axis=1, keepdims=True)
        m2 = jnp.mean(dyg * xhat, axis=1, keepdims=True)
        dz = rstd * (dyg - m1 - xhat * m2)
        dz_ref[...] = dz
        dzb_ref[...] = dz.astype(BF)

        @pl.when(i == 0)
        def _():
            dg_ref[...] = jnp.zeros_like(dg_ref)
            db_ref[...] = jnp.zeros_like(db_ref)

        dg_ref[...] += jnp.sum(dy * xhat, axis=0, keepdims=True)
        db_ref[...] += jnp.sum(dy, axis=0, keepdims=True)

    row = pl.BlockSpec((tm, D), lambda i: (i, 0))
    vec = pl.BlockSpec((1, D), lambda i: (0, 0))
    out_specs = [row, row, vec, vec] + ([pl.BlockSpec((8, 128), lambda i: (0, 0))] if head else [])
    out_shape = [jax.ShapeDtypeStruct((S, D), F32), jax.ShapeDtypeStruct((S, D), BF), jax.ShapeDtypeStruct((1, D), F32),
                 jax.ShapeDtypeStruct((1, D), F32)] + ([jax.ShapeDtypeStruct((8, 128), F32)] if head else [])
    operands = (dxn, target, z, g.reshape(1, D)) if head else (dxn, z, g.reshape(1, D))
    return _pcall(body, grid=(S // tm,), in_specs=[row] * (len(operands) - 1) + [vec], out_specs=out_specs,
                  out_shape=out_shape, name=name, comm=comm)(*operands)


def _rope_fwd(t, r_ref):
    return (t * r_ref[:, 0:128] + pltpu.roll(t, 120, 1) * r_ref[:, 128:256]
            + pltpu.roll(t, 8, 1) * r_ref[:, 256:384])


def _rope_bwd(g, r_ref):
    return (g * r_ref[:, 0:128] + pltpu.roll(g * r_ref[:, 128:256], 8, 1)
            + pltpu.roll(g * r_ref[:, 256:384], 120, 1))


def _dup_heads(kb):
    lo = lax.broadcasted_iota(jnp.int32, kb.shape, 1) < 64
    sw = pltpu.roll(kb, 64, 1)
    return [jnp.where(lo, kb, sw).astype(BF), jnp.where(lo, sw, kb).astype(BF)]


def _even_fwd(h, rope, lng, lnb, ws, bsb, sinks, name, comm=None):
    S = h.shape[0]
    nb = S // CHUNK

    def body(h_ref, hp_ref, rc_ref, rp_ref, lng_ref, lnb_ref, ws_ref, bsb_ref, sink_ref, mix_ref, o_ref, l_ref):
        n = pl.program_id(0)
        lane = lax.broadcasted_iota(jnp.int32, (128, 128), 1)
        rowi = lax.broadcasted_iota(jnp.int32, (128, 128), 0)
        tri = rowi >= lane
        lane_lo = lane < 64
        v = h_ref[:, W:2 * W]
        mu = jnp.mean(v, axis=1, keepdims=True)
        vc = v - mu
        var = jnp.mean(vc * vc, axis=1, keepdims=True)
        vn = vc * lax.rsqrt(var + LN_EPS) * lng_ref[...] + lnb_ref[...]
        ms = [_dot(jnp.where(tri, ws_ref[g], 0.0).astype(BF), vn[:, g * 128:(g + 1) * 128].astype(BF)) for g in range(8)]
        for g in range(8):
            sl = slice(g * 128, (g + 1) * 128)
            ag = h_ref[:, 2 * W + g * 128:2 * W + (g + 1) * 128]
            mix_ref[0, :, sl] = (h_ref[:, sl] * (ms[g] + bsb_ref[g]) * (ag * _sig(ag))).astype(BF)
        kb = jnp.concatenate([_rope_fwd(hp_ref[:, 0:128], rp_ref), _rope_fwd(h_ref[:, 4096:4224], rc_ref)], axis=0)
        vb = jnp.concatenate([hp_ref[:, 128:256], h_ref[:, 4224:4352]], axis=0)
        k2 = _dup_heads(kb)
        v2 = _dup_heads(vb)
        qi = lax.broadcasted_iota(jnp.int32, (128, 256), 0)
        kj = lax.broadcasted_iota(jnp.int32, (128, 256), 1)
        diff = qi + 128 - kj
        valid = (diff >= 0) & (diff < 128) & ((n > 0) | (kj >= 128))
        lacc = jnp.zeros((128, 128), F32)
        for j0 in range(0, 8, HEAD_COLS):
            heads = [(j, half) for j in range(j0, j0 + HEAD_COLS) for half in range(2)]
            sc, pr, oh = {}, {}, {}
            for j in range(j0, j0 + HEAD_COLS):
                qc = _rope_fwd(h_ref[:, 3072 + j * 128:3072 + (j + 1) * 128], rc_ref)
                sc[j, 0] = _dot_nt(jnp.where(lane_lo, qc, 0.0).astype(BF), k2[j // 4])
                sc[j, 1] = _dot_nt(jnp.where(lane_lo, 0.0, qc).astype(BF), k2[j // 4])
            for j, half in heads:
                hq = 2 * j + half
                s = jnp.where(valid, sc[j, half] * 0.125, NEG)
                sk = sink_ref[hq]
                mx = jnp.maximum(jnp.max(s, axis=1, keepdims=True), sk)
                p = jnp.exp(s - mx)
                den = jnp.sum(p, axis=1, keepdims=True) + jnp.exp(sk - mx)
                pr[j, half] = (p / den).astype(BF)
                lacc = jnp.where(lane == hq, mx + jnp.log(den), lacc)
            for j, half in heads:
                oh[j, half] = _dot(pr[j, half], v2[j // 4])
            for j in range(j0, j0 + HEAD_COLS):
                cs = slice(j * 128, (j + 1) * 128)
                ocol = jnp.where(lane_lo, oh[j, 0], oh[j, 1])
                bg = h_ref[:, 4352 + j * 128:4352 + (j + 1) * 128]
                o_ref[:, cs] = ocol
                mix_ref[1, :, cs] = (ocol * (bg * _sig(bg))).astype(BF)
        l_ref[...] = lacc

    prev = lambda n: jnp.maximum(n - 1, 0)
    full = lambda shape: pl.BlockSpec(shape, lambda n: (0,) * len(shape))
    return _pcall(
        body, grid=(nb,),
        in_specs=[pl.BlockSpec((CHUNK, EVEN_IN), lambda n: (n, 0)),
                  pl.BlockSpec((CHUNK, 256), lambda n: (prev(n), 16)),
                  pl.BlockSpec((CHUNK, 384), lambda n: (n, 0)),
                  pl.BlockSpec((CHUNK, 384), lambda n: (prev(n), 0)),
                  full((1, W)), full((1, W)), full((8, 128, 128)), full((8, 128, 128)),
                  pl.BlockSpec(memory_space=pltpu.SMEM)],
        out_specs=[pl.BlockSpec((2, CHUNK, W), lambda n: (0, n, 0)),
                   pl.BlockSpec((CHUNK, W), lambda n: (n, 0)),
                   pl.BlockSpec((CHUNK, 128), lambda n: (n, 0))],
        out_shape=[jax.ShapeDtypeStruct((2, S, W), BF), jax.ShapeDtypeStruct((S, W), F32),
                   jax.ShapeDtypeStruct((S, 128), F32)],
        name=name, comm=comm)(h, h, rope, rope, lng.reshape(1, W), lnb.reshape(1, W), ws, bsb, sinks)


def _even_bwd(h, dmix3, o, l, rope, lng, lnb, ws, wst, bsb, sinks, name, comm=None):
    S = h.shape[0]
    nb = S // CHUNK

    def body(h_ref, hp_ref, hn_ref, dm_ref, dmn_ref, o_ref, on_ref, l_ref, ln_ref, rc_ref, rp_ref, rn_ref,
             lng_ref, lnb_ref, ws_ref, wst_ref, bsb_ref, sink_ref,
             dh_ref, dws_ref, dbs_ref, dlng_ref, dlnb_ref, dsink_ref, dvn_ref):
        n = pl.program_id(0)

        @pl.when(n == 0)
        def _():
            dws_ref[...] = jnp.zeros_like(dws_ref)
            dbs_ref[...] = jnp.zeros_like(dbs_ref)
            dlng_ref[...] = jnp.zeros_like(dlng_ref)
            dlnb_ref[...] = jnp.zeros_like(dlnb_ref)
            dsink_ref[...] = jnp.zeros_like(dsink_ref)

        lane = lax.broadcasted_iota(jnp.int32, (128, 128), 1)
        rowi = lax.broadcasted_iota(jnp.int32, (128, 128), 0)
        lane1 = lax.broadcasted_iota(jnp.int32, (1, 128), 1)
        tri = rowi >= lane
        tri_t = lane >= rowi
        lane_lo = lane < 64
        v = h_ref[:, W:2 * W]
        mu = jnp.mean(v, axis=1, keepdims=True)
        vc = v - mu
        var = jnp.mean(vc * vc, axis=1, keepdims=True)
        rstd = lax.rsqrt(var + LN_EPS)
        vhat = vc * rstd
        vn = vhat * lng_ref[...] + lnb_ref[...]
        dbs_acc = jnp.zeros((128, 128), F32)
        vng = [vn[:, g * 128:(g + 1) * 128].astype(BF) for g in range(8)]
        ms = [_dot(jnp.where(tri, ws_ref[g], 0.0).astype(BF), vng[g]) for g in range(8)]
        dmb = []
        for g in range(8):
            sl = slice(g * 128, (g + 1) * 128)
            m = ms[g] + bsb_ref[g]
            ag = h_ref[:, 2 * W + g * 128:2 * W + (g + 1) * 128]
            sg, dsg = _silu_grad(ag)
            u = h_ref[:, sl]
            da = dm_ref[0, :, sl]
            dmm = da * u * sg
            dh_ref[:, sl] = (da * m * sg).astype(BF)
            dh_ref[:, 2 * W + g * 128:2 * W + (g + 1) * 128] = (da * u * m * dsg).astype(BF)
            dmb.append(dmm.astype(BF))
            dbs_acc = jnp.where(lane == g, jnp.sum(dmm, axis=1, keepdims=True), dbs_acc)
        dvs = [_dot(jnp.where(tri_t, wst_ref[g], 0.0).astype(BF), dmb[g]) for g in range(8)]
        dwss = [_dot_nt(dmb[g], vng[g]) for g in range(8)]
        for g in range(8):
            dvn_ref[:, g * 128:(g + 1) * 128] = dvs[g]
            dws_ref[g] += jnp.where(tri, dwss[g], 0.0)
        dbs_ref[...] += dbs_acc
        dvn = dvn_ref[...]
        dlng_ref[...] += jnp.sum(dvn * vhat, axis=0, keepdims=True)
        dlnb_ref[...] += jnp.sum(dvn, axis=0, keepdims=True)
        dyg = dvn * lng_ref[...]
        m1 = jnp.mean(dyg, axis=1, keepdims=True)
        m2 = jnp.mean(dyg * vhat, axis=1, keepdims=True)
        dh_ref[:, W:2 * W] = (rstd * (dyg - m1 - vhat * m2)).astype(BF)
        kcur = _rope_fwd(h_ref[:, 4096:4224], rc_ref)
        kb = jnp.concatenate([_rope_fwd(hp_ref[:, 0:128], rp_ref), kcur], axis=0)
        vb = jnp.concatenate([hp_ref[:, 128:256], h_ref[:, 4224:4352]], axis=0)
        k2 = _dup_heads(kb)
        v2 = _dup_heads(vb)
        kc2 = _dup_heads(kcur)
        vc2 = _dup_heads(h_ref[:, 4224:4352])
        qi = lax.broadcasted_iota(jnp.int32, (128, 256), 0)
        kj = lax.broadcasted_iota(jnp.int32, (128, 256), 1)
        diff = qi + 128 - kj
        valid = (diff >= 0) & (diff < 128) & ((n > 0) | (kj >= 128))
        validn = (lane > rowi) & (n < nb - 1)
        lc = l_ref[...]
        lnx = ln_ref[...]
        dk = [jnp.zeros((128, 128), F32), jnp.zeros((128, 128), F32)]
        dv = [jnp.zeros((128, 128), F32), jnp.zeros((128, 128), F32)]
        dsk_acc = jnp.zeros((1, 128), F32)
        for j0 in range(0, 8, HEAD_COLS):
            heads = [(j, half) for j in range(j0, j0 + HEAD_COLS) for half in range(2)]
            t = {}
            for j in range(j0, j0 + HEAD_COLS):
                cs = slice(j * 128, (j + 1) * 128)
                qc = _rope_fwd(h_ref[:, 3072 + j * 128:3072 + (j + 1) * 128], rc_ref)
                qn = _rope_fwd(hn_ref[:, 3072 + j * 128:3072 + (j + 1) * 128], rn_ref)
                bg = h_ref[:, 4352 + j * 128:4352 + (j + 1) * 128]
                sgb, dsgb = _silu_grad(bg)
                db = dm_ref[1, :, cs]
                oc = o_ref[:, cs]
                do = db * sgb
                dh_ref[:, 4352 + j * 128:4352 + (j + 1) * 128] = (db * oc * dsgb).astype(BF)
                bgn = hn_ref[:, 4352 + j * 128:4352 + (j + 1) * 128]
                don = dmn_ref[1, :, cs] * (bgn * _sig(bgn))
                prod = do * oc
                prodn = don * on_ref[:, cs]
                for half in range(2):
                    hq = 2 * j + half
                    hm = lane_lo if half == 0 else jnp.logical_not(lane_lo)
                    t[j, half] = dict(
                        dsum=jnp.sum(jnp.where(hm, prod, 0.0), axis=1, keepdims=True),
                        dsumn=jnp.sum(jnp.where(hm, prodn, 0.0), axis=1, keepdims=True),
                        lh=jnp.sum(jnp.where(lane == hq, lc, 0.0), axis=1, keepdims=True),
                        lhn=jnp.sum(jnp.where(lane == hq, lnx, 0.0), axis=1, keepdims=True),
                        qm=jnp.where(hm, qc, 0.0).astype(BF), dom=jnp.where(hm, do, 0.0).astype(BF),
                        qnm=jnp.where(hm, qn, 0.0).astype(BF), donm=jnp.where(hm, don, 0.0).astype(BF))
            for j, half in heads:
                e, hk = t[j, half], j // 4
                e["s"], e["dp"] = _dot_nt(e["qm"], k2[hk]), _dot_nt(e["dom"], v2[hk])
                e["sn"], e["dpn"] = _dot_nt(e["qnm"], kc2[hk]), _dot_nt(e["donm"], vc2[hk])
            for j, half in heads:
                e, hq = t[j, half], 2 * j + half
                p = jnp.exp(jnp.where(valid, e["s"] * 0.125 - e["lh"], NEG))
                ds = p * (e["dp"] - e["dsum"])
                pn = jnp.exp(jnp.where(validn, e["sn"] * 0.125 - e["lhn"], NEG))
                dsn = pn * (e["dpn"] - e["dsumn"])
                psink = jnp.exp(sink_ref[hq] - e["lh"])
                dsk_acc = jnp.where(lane1 == hq, -jnp.sum(psink * e["dsum"], axis=0, keepdims=True), dsk_acc)
                e["ds"] = ds.astype(BF)
                e["pt"], e["dst"] = jnp.transpose(p[:, 128:256]).astype(BF), jnp.transpose(ds[:, 128:256]).astype(BF)
                e["pnt"], e["dsnt"] = jnp.transpose(pn).astype(BF), jnp.transpose(dsn).astype(BF)
            for j, half in heads:
                e, hk = t[j, half], j // 4
                e["dq"] = _dot(e["ds"], k2[hk])
                e["dv"] = _dot(e["pt"], e["dom"]) + _dot(e["pnt"], e["donm"])
                e["dk"] = _dot(e["dst"], e["qm"]) + _dot(e["dsnt"], e["qnm"])
            for j in range(j0, j0 + HEAD_COLS):
                hk = j // 4
                dqcol = jnp.where(lane_lo, t[j, 0]["dq"], t[j, 1]["dq"]) * 0.125
                dh_ref[:, 3072 + j * 128:3072 + (j + 1) * 128] = _rope_bwd(dqcol, rc_ref).astype(BF)
                dv[hk] = dv[hk] + t[j, 0]["dv"] + t[j, 1]["dv"]
                dk[hk] = dk[hk] + (t[j, 0]["dk"] + t[j, 1]["dk"]) * 0.125
        fold = lambda a: a + pltpu.roll(a, 64, 1)
        dh_ref[:, 4096:4224] = _rope_bwd(jnp.where(lane_lo, fold(dk[0]), fold(dk[1])), rc_ref).astype(BF)
        dh_ref[:, 4224:4352] = jnp.where(lane_lo, fold(dv[0]), fold(dv[1])).astype(BF)
        dsink_ref[...] += dsk_acc

    prev = lambda n: jnp.maximum(n - 1, 0)
    nxt = lambda n: jnp.minimum(n + 1, nb - 1)
    full = lambda shape: pl.BlockSpec(shape, lambda n: (0,) * len(shape))
    return _pcall(
        body, grid=(nb,),
        in_specs=[pl.BlockSpec((CHUNK, EVEN_IN), lambda n: (n, 0)),
                  pl.BlockSpec((CHUNK, 256), lambda n: (prev(n), 16)),
                  pl.BlockSpec((CHUNK, EVEN_IN), lambda n: (nxt(n), 0)),
                  pl.BlockSpec((2, CHUNK, W), lambda n: (0, n, 0)),
                  pl.BlockSpec((2, CHUNK, W), lambda n: (0, nxt(n), 0)),
                  pl.BlockSpec((CHUNK, W), lambda n: (n, 0)),
                  pl.BlockSpec((CHUNK, W), lambda n: (nxt(n), 0)),
                  pl.BlockSpec((CHUNK, 128), lambda n: (n, 0)),
                  pl.BlockSpec((CHUNK, 128), lambda n: (nxt(n), 0)),
                  pl.BlockSpec((CHUNK, 384), lambda n: (n, 0)),
                  pl.BlockSpec((CHUNK, 384), lambda n: (prev(n), 0)),
                  pl.BlockSpec((CHUNK, 384), lambda n: (nxt(n), 0)),
                  full((1, W)), full((1, W)), full((8, 128, 128)), full((8, 128, 128)), full((8, 128, 128)),
                  pl.BlockSpec(memory_space=pltpu.SMEM)],
        out_specs=[pl.BlockSpec((CHUNK, EVEN_IN), lambda n: (n, 0)),
                   full((8, 128, 128)), full((128, 128)), full((1, W)), full((1, W)), full((1, 128))],
        out_shape=[jax.ShapeDtypeStruct((S, EVEN_IN), BF), jax.ShapeDtypeStruct((8, 128, 128), F32),
                   jax.ShapeDtypeStruct((128, 128), F32), jax.ShapeDtypeStruct((1, W), F32),
                   jax.ShapeDtypeStruct((1, W), F32), jax.ShapeDtypeStruct((1, 128), F32)],
        scratch=[pltpu.VMEM((CHUNK, W), F32)], name=name, comm=comm,
    )(h, h, h, dmix3, dmix3, o, o, l, l, rope, rope, rope, lng.reshape(1, W), lnb.reshape(1, W), ws, wst, bsb, sinks)


def _expm1(x):
    ser = x * (1.0 + x * (0.5 + x * (1.0 / 6.0 + x * (1.0 / 24.0))))
    return jnp.where(jnp.abs(x) < 1e-2, ser, jnp.exp(x) - 1.0)


def _softplus_neg(lam):
    z = -lam
    e = jnp.exp(-jnp.abs(z))
    l1p = jnp.where(e < 1e-3, e * (1.0 - e * (0.5 - e * (1.0 / 3.0))), jnp.log(1.0 + e))
    return jnp.maximum(z, 0.0) + l1p


def _shift_down(x, k, row, fill=0.0):
    return jnp.where(row >= k, pltpu.roll(x, k, 0), fill)


def _shift_up(x, k, row, fill=0.0):
    S = x.shape[0]
    return jnp.where(row < S - k, pltpu.roll(x, S - k, 0), fill)


def _lru_gates(xc, row, cw_ref, cb_ref, wa_ref, wx_ref, ba_ref, bx_ref, lam_ref):
    xconv = (cw_ref[3:4, :] * xc + cw_ref[2:3, :] * _shift_down(xc, 1, row) + cw_ref[1:2, :] * _shift_down(xc, 2, row)
             + cw_ref[0:1, :] * _shift_down(xc, 3, row) + cb_ref[...])
    xb = xconv.astype(BF)
    r = _sig(_dot(xb, wa_ref[...]) + ba_ref[...])
    i = _sig(_dot(xb, wx_ref[...]) + bx_ref[...])
    sp = _softplus_neg(lam_ref[...])
    log_a = -LRU_C * r * sp
    a = jnp.exp(log_a)
    mult = jnp.sqrt(-_expm1(2.0 * log_a))
    return xconv, r, i, sp, a, mult


ROWS_PER_TILE = 8


def _steps(a, b, shift, inside, products=True):
    n, k = inside.n, 1
    while k < n:
        b = a * jnp.where(inside(k), shift(b, k), 0.0) + b
        if products or 2 * k < n:
            a = a * jnp.where(inside(k), shift(a, k), 1.0)
        k *= 2
    return a, b


class _Inside:
    def __init__(self, pos, n, reverse):
        self.pos, self.n, self.reverse = pos, n, reverse

    def __call__(self, k):
        return self.pos < self.n - k if self.reverse else self.pos >= k


def _scan_rows(a, b, row, a_ref, b_ref, c_ref, reverse=False):
    S = a.shape[0]
    G = S // ROWS_PER_TILE
    if reverse:
        shift = lambda x, k: pltpu.roll(x, x.shape[0] - k, 0)
    else:
        shift = lambda x, k: pltpu.roll(x, k, 0)
    a, b = _steps(a, b, shift, _Inside(row % ROWS_PER_TILE, ROWS_PER_TILE, reverse))
    a_ref[...] = a
    b_ref[...] = b
    last = 0 if reverse else ROWS_PER_TILE - 1
    grow = lax.broadcasted_iota(jnp.int32, (G, a.shape[1]), 0)
    _, tot = _steps(a_ref[pl.ds(last, G, stride=ROWS_PER_TILE), :], b_ref[pl.ds(last, G, stride=ROWS_PER_TILE), :],
                    shift, _Inside(grow, G, reverse), products=False)
    enters = jnp.where(_Inside(grow, G, reverse)(1), shift(tot, 1), 0.0)
    for r in range(ROWS_PER_TILE):
        c_ref[pl.ds(r, G, stride=ROWS_PER_TILE), :] = enters
    return b + a * c_ref[...]


def _odd_c_fwd(h, cw, cb, wa, wx, ba, bx, lam, name, comm=None):
    S = h.shape[0]

    def body(xc_ref, cg_ref, cw_ref, cb_ref, wa_ref, wx_ref, ba_ref, bx_ref, lam_ref, mix_ref, hst_ref, sa_ref, sb_ref, sc_ref):
        row = lax.broadcasted_iota(jnp.int32, (S, 128), 0)
        xconv, r, i, sp, a, mult = _lru_gates(xc_ref[...], row, cw_ref, cb_ref, wa_ref, wx_ref, ba_ref, bx_ref, lam_ref)
        bb = _scan_rows(a, mult * (i * xconv), row, sa_ref, sb_ref, sc_ref)
        hst_ref[...] = bb
        cg = cg_ref[...]
        mix_ref[...] = (bb * (cg * _sig(cg))).astype(BF)

    col = lambda off: pl.BlockSpec((S, 128), lambda j: (0, off + j))
    vec = pl.BlockSpec((1, 128), lambda j: (0, j))
    mat = pl.BlockSpec((None, 128, 128), lambda j: (j, 0, 0))
    return _pcall(
        body, grid=(8,),
        in_specs=[col(0), col(8), pl.BlockSpec((4, 128), lambda j: (0, j)), vec, mat, mat, vec, vec, vec],
        out_specs=[pl.BlockSpec((None, S, 128), lambda j: (0, 0, j)), pl.BlockSpec((S, 128), lambda j: (0, j))],
        out_shape=[jax.ShapeDtypeStruct((2, S, W), BF), jax.ShapeDtypeStruct((S, W), F32)],
        scratch=[pltpu.VMEM((S, 128), F32)] * 3, name=name, comm=comm,
    )(h, h, cw, cb.reshape(1, W), wa, wx, ba.reshape(1, W), bx.reshape(1, W), lam.reshape(1, W))


def _pool_sums(x, g, row, shift):
    s2 = x + shift(x, 1, row)
    s4 = s2 + shift(s2, 2, row)
    s8 = s4 + shift(s4, 4, row)
    s16 = s8 + shift(s8, 8, row)
    return jnp.where(g == 0, s2, jnp.where(g == 1, s4, jnp.where(g == 2, s8, s16)))


def _odd_d_fwd(h, mix3, wp, dscale, name):
    S = h.shape[0]

    def body(xd_ref, dg_ref, wp_ref, ds_ref, mix_in, mix_ref):
        g = pl.program_id(0)
        row = lax.broadcasted_iota(jnp.int32, (S, 256), 0)
        xd = xd_ref[...]
        cnt = jnp.minimum(row + 1, jnp.left_shift(2, g)).astype(F32)
        pooled = _pool_sums(xd, g, row, _shift_down) / cnt - xd
        mixed = _dot(pooled.astype(BF), wp_ref[...])
        dg = dg_ref[...]
        mix_ref[...] = (mixed * ds_ref[...] * (dg * _sig(dg))).astype(BF)

    col = lambda off: pl.BlockSpec((S, 256), lambda g: (0, off + g))
    return pl.pallas_call(
        body, grid=(4,),
        in_specs=[col(8), col(12), pl.BlockSpec((None, 256, 256), lambda g: (g, 0, 0)),
                  pl.BlockSpec((1, 256), lambda g: (0, g)), ANY],
        out_specs=pl.BlockSpec((None, S, 256), lambda g: (1, 0, g)),
        out_shape=jax.ShapeDtypeStruct((2, S, W), BF), input_output_aliases={4: 0},
        name=name, compiler_params=_cp(),
    )(h, h, wp, dscale.reshape(1, W), mix3)


def _odd_c_bwd(h, hst, dmix3, cw, cb, wa, wx, wat, wxt, ba, bx, lam, name, comm=None):
    S = h.shape[0]

    def body(xc_ref, cg_ref, hst_ref, dc_ref, cw_ref, cb_ref, wa_ref, wx_ref, wat_ref, wxt_ref, ba_ref, bx_ref, lam_ref,
             dh_ref, dcw_ref, dcb_ref, dwa_ref, dwx_ref, dba_ref, dbx_ref, dlam_ref, sa_ref, sb_ref, sc_ref):
        row = lax.broadcasted_iota(jnp.int32, (S, 128), 0)
        xc = xc_ref[...]
        xconv, r, i, sp, a, mult = _lru_gates(xc, row, cw_ref, cb_ref, wa_ref, wx_ref, ba_ref, bx_ref, lam_ref)
        hst = hst_ref[...]
        cg = cg_ref[...]
        sg, dsg = _silu_grad(cg)
        dc = dc_ref[...]
        dh_ref[1] = (dc * hst * dsg).astype(BF)
        lam_t = _scan_rows(_shift_up(a, 1, row), dc * sg, row, sa_ref, sb_ref, sc_ref, reverse=True)
        da = lam_t * _shift_down(hst, 1, row)
        ix = i * xconv
        dmult = lam_t * ix
        di = lam_t * mult * xconv
        dxconv = lam_t * mult * i
        dlog_a = da * a - dmult * (a * a / mult)
        dr = dlog_a * (-LRU_C * sp)
        dsp = jnp.sum(dlog_a * (-LRU_C * r), axis=0, keepdims=True)
        dlam_ref[...] = dsp * (-_sig(-lam_ref[...]))
        dpa = dr * r * (1.0 - r)
        dpx = di * i * (1.0 - i)
        dpab = dpa.astype(BF)
        dpxb = dpx.astype(BF)
        xb = xconv.astype(BF)
        dxconv = dxconv + _dot(dpab, wat_ref[...]) + _dot(dpxb, wxt_ref[...])
        dwa_ref[...] = _dot_tn(xb, dpab)
        dwx_ref[...] = _dot_tn(xb, dpxb)
        dba_ref[...] = jnp.sum(dpa, axis=0, keepdims=True)
        dbx_ref[...] = jnp.sum(dpx, axis=0, keepdims=True)
        dh_ref[0] = (cw_ref[3:4, :] * dxconv + cw_ref[2:3, :] * _shift_up(dxconv, 1, row)
                     + cw_ref[1:2, :] * _shift_up(dxconv, 2, row) + cw_ref[0:1, :] * _shift_up(dxconv, 3, row)).astype(BF)
        for j in range(4):
            src = xc if j == 3 else _shift_down(xc, 3 - j, row)
            dcw_ref[j:j + 1, :] = jnp.sum(dxconv * src, axis=0, keepdims=True)
        dcb_ref[...] = jnp.sum(dxconv, axis=0, keepdims=True)

    col = lambda off: pl.BlockSpec((S, 128), lambda j: (0, off + j))
    vec = pl.BlockSpec((1, 128), lambda j: (0, j))
    mat = pl.BlockSpec((None, 128, 128), lambda j: (j, 0, 0))
    vshape = jax.ShapeDtypeStruct((1, W), F32)
    mshape = jax.ShapeDtypeStruct((8, 128, 128), F32)
    return _pcall(
        body, grid=(8,),
        in_specs=[col(0), col(8), col(0), pl.BlockSpec((None, S, 128), lambda j: (0, 0, j)),
                  pl.BlockSpec((4, 128), lambda j: (0, j)), vec, mat, mat, mat, mat, vec, vec, vec],
        out_specs=[pl.BlockSpec((2, S, 128), lambda j: (0, 0, j)), pl.BlockSpec((4, 128), lambda j: (0, j)), vec,
                   mat, mat, vec, vec, vec],
        out_shape=[jax.ShapeDtypeStruct((4, S, W), BF), jax.ShapeDtypeStruct((4, W), F32), vshape, mshape, mshape,
                   vshape, vshape, vshape],
        scratch=[pltpu.VMEM((S, 128), F32)] * 3, name=name, vmem=56, comm=comm,
    )(h, h, hst, dmix3, cw, cb.reshape(1, W), wa, wx, wat, wxt, ba.reshape(1, W), bx.reshape(1, W), lam.reshape(1, W))


def _odd_d_bwd(h, dmix3, dh4, wp, wpt, dscale, name):
    S = h.shape[0]

    def body(xd_ref, dg_ref, dd_ref, wp_ref, wpt_ref, ds_ref, dh_in, dh_ref, dwp_ref, dds_ref):
        g = pl.program_id(0)
        row = lax.broadcasted_iota(jnp.int32, (S, 256), 0)
        xd = xd_ref[...]
        cnt = jnp.minimum(row + 1, jnp.left_shift(2, g)).astype(F32)
        pooled = _pool_sums(xd, g, row, _shift_down) / cnt - xd
        pb = pooled.astype(BF)
        mixed = _dot(pb, wp_ref[...])
        dg = dg_ref[...]
        sg, dsg = _silu_grad(dg)
        dd = dd_ref[...]
        dmixed = dd * ds_ref[...] * sg
        dds_ref[...] = jnp.sum(dd * mixed * sg, axis=0, keepdims=True)
        dh_ref[1] = (dd * mixed * ds_ref[...] * dsg).astype(BF)
        dmb = dmixed.astype(BF)
        dpooled = _dot(dmb, wpt_ref[...])
        dwp_ref[...] = _dot_tn(pb, dmb)
        dh_ref[0] = (_pool_sums(dpooled / cnt, g, row, _shift_up) - dpooled).astype(BF)

    col = lambda off: pl.BlockSpec((S, 256), lambda g: (0, off + g))
    mat = pl.BlockSpec((None, 256, 256), lambda g: (g, 0, 0))
    vec = pl.BlockSpec((1, 256), lambda g: (0, g))
    return pl.pallas_call(
        body, grid=(4,),
        in_specs=[col(8), col(12), pl.BlockSpec((None, S, 256), lambda g: (1, 0, g)), mat, mat, vec, ANY],
        out_specs=[pl.BlockSpec((2, S, 256), lambda g: (1, 0, g)), mat, vec],
        out_shape=[jax.ShapeDtypeStruct((4, S, W), BF), jax.ShapeDtypeStruct((4, 256, 256), F32),
                   jax.ShapeDtypeStruct((1, W), F32)],
        input_output_aliases={6: 0}, name=name, compiler_params=_cp(56),
    )(h, h, dmix3, wp, wpt, dscale.reshape(1, W), dh4)


def _peer(d):
    x, y, c = lax.axis_index("x"), lax.axis_index("y"), lax.axis_index("c")
    px = 1 - x if d & 4 else x
    py = 1 - y if d & 2 else y
    pc = 1 - c if d & 1 else c
    return (px, py, pc), 4 * px + 2 * py + pc


class _GatherAll(_Comm):
    def __init__(self, xs):
        self.peers = EVERYONE
        self.inputs = [xs]
        self.out_shapes = [jax.ShapeDtypeStruct((N_DEV,) + xs.shape, xs.dtype)]
        self.sem_shapes = [pltpu.SemaphoreType.DMA((N_DEV - 1,)), pltpu.SemaphoreType.DMA((N_DEV - 1,)),
                           pltpu.SemaphoreType.DMA]

    def copies(self, ins, outs, sems):
        (x_ref,), (out_ref,), (send, recv, loc) = ins, outs, sems
        _, me = _peer(0)
        res = [pltpu.make_async_copy(x_ref, out_ref.at[me], loc)]
        for d in range(1, N_DEV):
            peer, _ = _peer(d)
            res.append(pltpu.make_async_remote_copy(src_ref=x_ref, dst_ref=out_ref.at[me], send_sem=send.at[d - 1],
                                                    recv_sem=recv.at[d - 1], device_id=peer, device_id_type=MESH))
        return res


class _ExchangeAll(_Comm):
    def __init__(self, g8):
        self.peers = EVERYONE
        self.inputs = [g8]
        self.out_shapes = [jax.ShapeDtypeStruct(g8.shape, g8.dtype)]
        self.sem_shapes = [pltpu.SemaphoreType.DMA((N_DEV - 1,)), pltpu.SemaphoreType.DMA((N_DEV - 1,)),
                           pltpu.SemaphoreType.DMA]

    def copies(self, ins, outs, sems):
        (g_ref,), (out_ref,), (send, recv, loc) = ins, outs, sems
        _, me = _peer(0)
        res = [pltpu.make_async_copy(g_ref.at[me], out_ref.at[0], loc)]
        for d in range(1, N_DEV):
            peer, pidx = _peer(d)
            res.append(pltpu.make_async_remote_copy(src_ref=g_ref.at[pidx], dst_ref=out_ref.at[d], send_sem=send.at[d - 1],
                                                    recv_sem=recv.at[d - 1], device_id=peer, device_id_type=MESH))
        return res


def _sum8(r8, tr, name):
    _, R, C = r8.shape
    tr = min(tr, R)
    assert R % tr == 0

    def body(r_ref, o_ref):
        acc = r_ref[0]
        for d in range(1, N_DEV):
            acc = acc + r_ref[d]
        o_ref[...] = acc

    return pl.pallas_call(
        body, grid=(R // tr,), in_specs=[pl.BlockSpec((N_DEV, tr, C), lambda i: (0, i, 0))],
        out_specs=pl.BlockSpec((tr, C), lambda i: (i, 0)), out_shape=jax.ShapeDtypeStruct((R, C), F32),
        name=name, compiler_params=_cp(),
    )(r8)


def _adamw_math(w, g, m, v):
    m2 = B1 * m + (1.0 - B1) * g
    v2 = B2 * v + (1.0 - B2) * (g * g)
    m_hat = m2 / (1.0 - B1 ** STEP)
    v_hat = v2 / (1.0 - B2 ** STEP)
    return -LR * (m_hat / (jnp.sqrt(v_hat) + ADAM_EPS) + WD * w), m2, v2


def _adamw_many(ws, gs, ms, vs, name):
    n = len(ws)

    def body(*refs):
        for i in range(n):
            d, m2, v2 = _adamw_math(refs[i][...], refs[n + i][...], refs[2 * n + i][...], refs[3 * n + i][...])
            refs[4 * n + i][...] = d
            refs[5 * n + i][...] = m2
            refs[6 * n + i][...] = v2

    vmem = pl.BlockSpec(memory_space=pltpu.VMEM)
    shapes = [jax.ShapeDtypeStruct(w.shape, F32) for w in ws]
    res = pl.pallas_call(body, in_specs=[vmem] * (4 * n), out_specs=[vmem] * (3 * n), out_shape=shapes * 3, name=name,
                         compiler_params=_cp())(*ws, *gs, *ms, *vs)
    return res[:n], res[n:2 * n], res[2 * n:]


def _adamw(w3, gs, m3, v3, tr, name, comm=None):
    _, R, C = w3.shape
    n = 2 if isinstance(gs[0], tuple) else 1

    def gradient(refs):
        if n == 1:
            return refs[0][...]
        s_ref, r_ref = refs
        return ((s_ref[...].astype(F32) + r_ref[0].astype(F32)) + r_ref[1].astype(F32)) + r_ref[2].astype(F32)

    def body(w_ref, *rest):
        g_refs, (m_ref, v_ref, d_ref, m2_ref, v2_ref, g_ref) = rest[:2 * n], rest[2 * n:]
        g = jnp.where(pl.program_id(0) == 0, gradient(g_refs[:n]), gradient(g_refs[n:]))
        d_ref[...], m2_ref[...], v2_ref[...] = _adamw_math(w_ref[...], g, m_ref[...], v_ref[...])
        g_ref[...] = g

    blk = pl.BlockSpec((None, tr, C), lambda j, i: (j, i, 0))

    def grad_specs(layer):
        at = lambda j, i: jnp.where(j == layer, i, 0)
        if n == 1:
            return [pl.BlockSpec((tr, C), lambda j, i: (at(j, i), 0))]
        return [pl.BlockSpec((None, tr, C), lambda j, i: (0, at(j, i), 0)), pl.BlockSpec((3, tr, C), lambda j, i: (0, at(j, i), 0))]

    flat = [a for g in gs for a in (g if n == 2 else (g,))]
    shp = jax.ShapeDtypeStruct((2, R, C), F32)
    return _pcall(body, grid=(2, R // tr), in_specs=[blk] + grad_specs(0) + grad_specs(1) + [blk, blk], out_specs=[blk] * 4,
                  out_shape=[shp] * 4, name=name, comm=comm)(w3, *flat, m3, v3)


def _rep_pack(a):
    n = a.size
    pad = (-n) % 1024
    f = a.reshape(-1)
    if pad:
        f = jnp.concatenate([f, jnp.zeros((pad,), a.dtype)])
    return f.reshape(N_DEV, -1, 128)


def _rep_unpack(p, shape):
    n = 1
    for s in shape:
        n *= s
    return p.reshape(-1)[:n].reshape(shape)


def _sh_pack(a, axis):
    shp = a.shape
    a = a.reshape(shp[:axis] + (N_DEV, shp[axis] // N_DEV) + shp[axis + 1:])
    return jnp.moveaxis(a, axis, 0).reshape(N_DEV, -1, 128)


def _sh_unpack(p, shape, axis):
    a = p.reshape((N_DEV,) + shape[:axis] + (shape[axis] // N_DEV,) + shape[axis + 1:])
    return jnp.moveaxis(a, 0, axis).reshape(shape)


def _pad_rows(a, mult=8):
    pad = (-a.shape[-2]) % mult
    if pad:
        a = jnp.concatenate([a, jnp.zeros(a.shape[:-2] + (pad, a.shape[-1]), a.dtype)], axis=-2)
    return a


REP = ["even_a_ln_g", "even_a_ln_b", "even_a_ws", "even_a_bs", "even_b_sinks", "even_ln_g", "even_ln_b",
       "odd_w_a", "odd_w_x"]
SH = [("odd_conv_w", (2, 4, W), 2), ("odd_conv_b", (2, W), 1), ("odd_b_a", (2, W), 1), ("odd_b_x", (2, W), 1),
      ("odd_lam", (2, W), 1), ("odd_w_pool", (2, 4, 256, 256), 2), ("odd_d_scale", (2, W), 1),
      ("odd_ln_g", (2, D), 1), ("odd_ln_b", (2, D), 1)]
BIG = ["even_w_in", "even_w_out", "odd_w_in", "odd_w_out"]
NAMES = ["even_w_in", "even_a_ln_g", "even_a_ln_b", "even_a_ws", "even_a_bs", "even_b_sinks", "even_w_out",
         "even_ln_g", "even_ln_b", "odd_w_in", "odd_conv_w", "odd_conv_b", "odd_w_a", "odd_b_a", "odd_w_x", "odd_b_x",
         "odd_lam", "odd_w_pool", "odd_d_scale", "odd_w_out", "odd_ln_g", "odd_ln_b"]


def _rope_table(positions):
    inv = ROPE_THETA ** (-jnp.arange(0, 16, 2, dtype=F32) / 16)
    f = jnp.arange(128) % 64
    ang = positions.astype(F32)[:, None] * inv[f % 8][None, :]
    cos, sin = jnp.cos(ang), jnp.sin(ang)
    return jnp.concatenate([jnp.where(f < 16, cos, 1.0), jnp.where(f < 8, -sin, 0.0),
                            jnp.where((f >= 8) & (f < 16), sin, 0.0)], axis=1)


def kernel(x, positions, even_w_in, even_a_ln_g, even_a_ln_b, even_a_ws, even_a_bs, even_b_sinks, even_w_out, even_ln_g, even_ln_b, odd_w_in, odd_conv_w, odd_conv_b, odd_w_a, odd_b_a, odd_w_x, odd_b_x, odd_lam, odd_w_pool, odd_d_scale, odd_w_out, odd_ln_g, odd_ln_b, loss_target, m_even_w_in, m_even_a_ln_g, m_even_a_ln_b, m_even_a_ws, m_even_a_bs, m_even_b_sinks, m_even_w_out, m_even_ln_g, m_even_ln_b, m_odd_w_in, m_odd_conv_w, m_odd_conv_b, m_odd_w_a, m_odd_b_a, m_odd_w_x, m_odd_b_x, m_odd_lam, m_odd_w_pool, m_odd_d_scale, m_odd_w_out, m_odd_ln_g, m_odd_ln_b, v_even_w_in, v_even_a_ln_g, v_even_a_ln_b, v_even_a_ws, v_even_a_bs, v_even_b_sinks, v_even_w_out, v_even_ln_g, v_even_ln_b, v_odd_w_in, v_odd_conv_w, v_odd_conv_b, v_odd_w_a, v_odd_b_a, v_odd_w_x, v_odd_b_x, v_odd_lam, v_odd_w_pool, v_odd_d_scale, v_odd_w_out, v_odd_ln_g, v_odd_ln_b):
    args = (even_w_in, even_a_ln_g, even_a_ln_b, even_a_ws, even_a_bs, even_b_sinks, even_w_out, even_ln_g, even_ln_b,
            odd_w_in, odd_conv_w, odd_conv_b, odd_w_a, odd_b_a, odd_w_x, odd_b_x, odd_lam, odd_w_pool, odd_d_scale,
            odd_w_out, odd_ln_g, odd_ln_b)
    margs = (m_even_w_in, m_even_a_ln_g, m_even_a_ln_b, m_even_a_ws, m_even_a_bs, m_even_b_sinks, m_even_w_out,
             m_even_ln_g, m_even_ln_b, m_odd_w_in, m_odd_conv_w, m_odd_conv_b, m_odd_w_a, m_odd_b_a, m_odd_w_x,
             m_odd_b_x, m_odd_lam, m_odd_w_pool, m_odd_d_scale, m_odd_w_out, m_odd_ln_g, m_odd_ln_b)
    vargs = (v_even_w_in, v_even_a_ln_g, v_even_a_ln_b, v_even_a_ws, v_even_a_bs, v_even_b_sinks, v_even_w_out,
             v_even_ln_g, v_even_ln_b, v_odd_w_in, v_odd_conv_w, v_odd_conv_b, v_odd_w_a, v_odd_b_a, v_odd_w_x,
             v_odd_b_x, v_odd_lam, v_odd_w_pool, v_odd_d_scale, v_odd_w_out, v_odd_ln_g, v_odd_ln_b)
    wts = dict(zip(NAMES, args))
    mom = dict(zip(NAMES, margs))
    var = dict(zip(NAMES, vargs))
    S = x.shape[1]
    x0 = x[0]
    rope = _rope_table(positions[0])

    kinds = ("even", "odd", "even", "odd")
    blk_in = [jnp.transpose(wts[kinds[l] + "_w_in"][l // 2]).astype(BF) for l in range(4)]
    blk_out = [wts[kinds[l] + "_w_out"][l // 2].astype(BF) for l in range(4)]
    sh_local = _pad_rows(jnp.concatenate([wts[nm].reshape(-1, 128) for nm, _, _ in SH], axis=0), 16)
    me = 4 * lax.axis_index("x") + 2 * lax.axis_index("y") + lax.axis_index("c")
    own_slot = lambda blk: lax.dynamic_update_slice(lax.empty((N_DEV,) + blk.shape, blk.dtype), blk[None], (me, 0, 0))
    reg = {"blk_small": sh_local, "w_small": own_slot(sh_local)}
    sched = _Sched(reg)
    for l in range(4):
        reg[f"blk_in{l}"], reg[f"blk_out{l}"] = blk_in[l], blk_out[l]
        reg[f"w_in{l}"], reg[f"w_out{l}"] = own_slot(blk_in[l]), own_slot(blk_out[l])
    sched.add(_rows("blk_in0", "w_in0", "ag1", blk_in[0].shape[0], ROW_CHUNK[blk_in[0].shape[0]]))
    sched.add(_rows("blk_small", "w_small", "ag1", sh_local.shape[0], sh_local.shape[0]))
    for l in range(4):
        sched.add(_rows(f"blk_out{l}", f"w_out{l}", "ag1", D // N_DEV, ROW_CHUNK[D // N_DEV]))
        if l < 3:
            r = blk_in[l + 1].shape[0]
            sched.add(_rows(f"blk_in{l + 1}", f"w_in{l + 1}", "ag1", r, ROW_CHUNK[r]))

    def gathered(dst, blk):
        sched.flush(dst, FLUSH_EXTRA_US)
        return reg.pop(dst)

    wt_in0 = gathered("w_in0", blk_in[0]).reshape(-1, D)
    full = {nm: wts[nm] for nm in REP}

    def gather_small():
        sh_all = gathered("w_small", sh_local)
        off = 0
        for nm, shape, axis in SH:
            r = wts[nm].size // 128
            full[nm] = _sh_unpack(sh_all[:, off:off + r, :], shape, axis)
            off += r

    saved = []
    wt_in, w_out = [wt_in0, None, None, None], [None] * 4
    xf, xb = x0, x0.astype(BF)
    fwd = lambda name: FWD_OVERBOOK * CARRY_US[name]
    for layer in range(4):
        j = layer // 2
        kind = kinds[layer]
        if wt_in[layer] is None:
            wt_in[layer] = gathered(f"w_in{layer}", blk_in[layer]).reshape(-1, D)
        h = sched.run(_mm_nt, fwd("mm_h_" + kind), xb, wt_in[layer], 1024, 768 if kind == "even" else 512, "mm_h_" + kind)
        if kind == "even":
            bsb = jnp.broadcast_to(full["even_a_bs"][j][:, :, None], (8, 128, 128))
            mix3, o, l = sched.run(_even_fwd, fwd("even_fwd"), h, rope, full["even_a_ln_g"][j], full["even_a_ln_b"][j],
                                   full["even_a_ws"][j], bsb, full["even_b_sinks"][j], "even_fwd")
            extra = (o, l, bsb)
        else:
            if "odd_lam" not in full:
                gather_small()
            wa, wx = full["odd_w_a"][j].astype(BF), full["odd_w_x"][j].astype(BF)
            wp = full["odd_w_pool"][j].astype(BF)
            mix3, hst = sched.run(_odd_c_fwd, fwd("odd_c_fwd"), h, full["odd_conv_w"][j], full["odd_conv_b"][j], wa, wx,
                                  full["odd_b_a"][j], full["odd_b_x"][j], full["odd_lam"][j], "odd_c_fwd")
            mix3 = _odd_d_fwd(h, mix3, wp, full["odd_d_scale"][j], "odd_d_fwd")
            extra = (hst, wa, wx, wp)
        w_out[layer] = gathered(f"w_out{layer}", blk_out[layer]).reshape(D, D)
        z, xn, xnb = sched.run(_mm_out_ln, fwd("mm_out_ln"), mix3, w_out[layer], xf, full[kind + "_ln_g"][j],
                               full[kind + "_ln_b"][j], "mm_out_ln")
        saved.append((xb, h, mix3, z, extra))
        xf, xb = xn, xnb

    dxn = xf

    gsum = {nm: [None, None] for nm in NAMES}

    chip_sums = {}
    sched.overhang = 0.15

    waiting = []

    def chip_sum(g, tag, key):
        r = g.shape[0] // N_DEV
        reg["g_" + key] = g.reshape(N_DEV, r, D)
        sched.add(_rows("g_" + key, "d_" + key, "rsd", r, r), first=True)
        waiting.append((key, tag))

    def add_arrived():
        for key, tag in list(waiting):
            if "d_" + key in reg and not sched.pending("d_" + key):
                waiting.remove((key, tag))
                g8 = reg.pop("g_" + key)
                chip_sums[key] = reg["s_" + key] = _add_pairs(g8, reg.pop("d_" + key), "rs_add_" + tag)
                sched.add(_rows("s_" + key, "r_" + key, "rs", g8.shape[1], ROW_CHUNK[g8.shape[1]] // 2))

    sched.after_landing = add_arrived

    def reduced(key):
        sched.flush("d_" + key, FLUSH_EXTRA_US)
        sched.flush("r_" + key, FLUSH_EXTRA_US)
        return chip_sums[key], reg.pop("r_" + key)

    for layer in (3, 2, 1, 0):
        j = layer // 2
        xb, h, mix3, z, extra = saved[layer]
        kind = kinds[layer]
        if layer == 3:
            dz, dzb, dg, dbeta, part = sched.run(_ln_bwd, CARRY_US["ln_bwd"], dxn, z, full[kind + "_ln_g"][j], "loss_ln_bwd",
                                                 target=loss_target[0])
        else:
            dz, dzb, dg, dbeta = sched.run(_ln_bwd, CARRY_US["ln_bwd"], dxn, z, full[kind + "_ln_g"][j], "ln_bwd")
        gsum[kind + "_ln_g"][j] = dg.reshape(D)
        gsum[kind + "_ln_b"][j] = dbeta.reshape(D)
        chip_sum(sched.run(_mm_tn, CARRY_US["mm_dw_out"], mix3, dzb, 512, "mm_dw_out"), "w_out", f"out{layer}")
        dmix3 = sched.run(_mm_nt, CARRY_US["mm_dmix"], dzb, w_out[layer], 1024, 1024, "mm_dmix", out3=True)
        if kind == "even":
            o, l, bsb = extra
            ws = full["even_a_ws"][j]
            dh, dws, dbs, dlng, dlnb, dsink = sched.run(
                _even_bwd, CARRY_US["even_bwd"], h, dmix3, o, l, rope, full["even_a_ln_g"][j], full["even_a_ln_b"][j],
                ws, jnp.swapaxes(ws, 1, 2), bsb, full["even_b_sinks"][j], "even_bwd")
            gsum["even_a_ws"][j] = dws
            gsum["even_a_bs"][j] = jnp.transpose(dbs[:, :8])
            gsum["even_a_ln_g"][j] = dlng.reshape(W)
            gsum["even_a_ln_b"][j] = dlnb.reshape(W)
            gsum["even_b_sinks"][j] = dsink[0, :16]
            if layer == 0:
                rep_rows = [_rep_pack(jnp.stack(gsum[nm]).reshape(wts[nm].shape)) for nm in REP]
                sh_rows = [_sh_pack(jnp.stack(gsum[nm]).reshape(shape), axis) for nm, shape, axis in SH]
                packed = _pad_rows(jnp.concatenate(rep_rows + sh_rows, axis=1))
                gw, (small8, parts) = _mm_tn(dh, xb, 384, "mm_dw_in_even", comm=_Join([_ExchangeAll(packed), _GatherAll(part)]))
                loss = jnp.sum(parts[:, 0, 0]) * (0.5 / D)
            else:
                gw = sched.run(_mm_tn, CARRY_US["mm_dw_in_even"], dh, xb, 384, "mm_dw_in_even")
            chip_sum(gw, "w_in_even", f"in{layer}")
            if layer == 0:
                n_rep = sum(p.shape[1] for p in rep_rows)
                red = _sum8(small8, 1 << 20, "sum_small")
                (rep_all,) = sched.flush("d_in0", FLUSH_EXTRA_US, beside=_GatherAll(_pad_rows(red[:n_rep])))
                sched.overhang = 0.6
            dxn = sched.run(_mm_nn_res, CARRY_US["mm_dx_even"], dh, wt_in[layer], dz, 512, 1024, "mm_dx_even")
        else:
            hst, wa, wx, wp = extra
            dh4, dcw, dcb, dwa, dwx, dba, dbx, dlam = sched.run(
                _odd_c_bwd, CARRY_US["odd_c_bwd"], h, hst, dmix3, full["odd_conv_w"][j], full["odd_conv_b"][j], wa, wx,
                jnp.swapaxes(wa, 1, 2), jnp.swapaxes(wx, 1, 2), full["odd_b_a"][j], full["odd_b_x"][j], full["odd_lam"][j],
                "odd_c_bwd")
            dh4, dwp, dds = _odd_d_bwd(h, dmix3, dh4, wp, jnp.swapaxes(wp, 1, 2), full["odd_d_scale"][j], "odd_d_bwd")
            gsum["odd_conv_w"][j], gsum["odd_conv_b"][j] = dcw, dcb.reshape(W)
            gsum["odd_w_a"][j], gsum["odd_w_x"][j] = dwa, dwx
            gsum["odd_b_a"][j], gsum["odd_b_x"][j], gsum["odd_lam"][j] = dba.reshape(W), dbx.reshape(W), dlam.reshape(W)
            gsum["odd_w_pool"][j], gsum["odd_d_scale"][j] = dwp, dds.reshape(W)
            chip_sum(sched.run(_mm_tn, CARRY_US["mm_dw_in_odd"], dh4, xb, 1024, "mm_dw_in_odd"), "w_in_odd", f"in{layer}")
            dxn = sched.run(_mm_nn_res, CARRY_US["mm_dx_odd"], dh4, wt_in[layer], dz, 512, 1024, "mm_dx_odd")
    grad_x = dxn[None]

    out_g, out_d, out_m, out_v = {}, {}, {}, {}
    for nm, kind, what, layers in (("odd_w_out", "odd", "out", (1, 3)), ("even_w_out", "even", "out", (0, 2)),
                                   ("odd_w_in", "odd", "in", (1, 3)), ("even_w_in", "even", "in", (0, 2))):
        gl = [reduced(f"{what}{l}") for l in layers]
        if nm == "even_w_in":
            view = lambda a: jnp.transpose(a, (0, 2, 1))
            res, _ = _adamw(view(wts[nm]), gl, view(mom[nm]), view(var[nm]), 112, f"adamw_{nm}")
            res = [view(a) for a in res]
        elif what == "in":
            gs = [jnp.transpose(_rs_final(s4, r3, "rs_final_w_in_odd")) for s4, r3 in gl]
            res, _ = _adamw(wts[nm], gs, mom[nm], var[nm], 512, f"adamw_{nm}")
        else:
            res = sched.run(_adamw, CARRY_US["adamw_" + nm], wts[nm], gl, mom[nm], var[nm], 128, f"adamw_{nm}")
        out_d[nm], out_m[nm], out_v[nm], out_g[nm] = res

    g_small = {}
    off = 0
    for nm, p in zip(REP, rep_rows):
        r = p.shape[1]
        g_small[nm] = _rep_unpack(rep_all[:, off:off + r, :], wts[nm].shape)
        off += r
    off = n_rep
    for (nm, shape, axis), p in zip(SH, sh_rows):
        r = p.shape[1]
        g_small[nm] = red[off:off + r].reshape(wts[nm].shape)
        off += r

    def rows(a):
        f = a.reshape(-1)
        pad = (-f.shape[0]) % 128
        if pad:
            f = jnp.concatenate([f, jnp.zeros((pad,), a.dtype)])
        return f.reshape(-1, 128)

    small = REP + [nm for nm, _, _ in SH]
    each = lambda src: [rows(src[nm]) for nm in small]
    d2, m2, v2 = _adamw_many(each(wts), each(g_small), each(mom), each(var), "adamw_small")
    for i, nm in enumerate(small):
        n, shp = wts[nm].size, wts[nm].shape
        take = lambda a: a.reshape(-1)[:n].reshape(shp)
        out_g[nm], out_d[nm], out_m[nm], out_v[nm] = g_small[nm], take(d2[i]), take(m2[i]), take(v2[i])

    return (loss, grad_x, *[out_g[nm] for nm in NAMES], *[out_d[nm] for nm in NAMES],
            *[out_m[nm] for nm in NAMES], *[out_v[nm] for nm in NAMES])
```

```python
import functools

import jax
import jax.numpy as jnp
from jax import lax
from jax.experimental import pallas as pl
from jax.experimental.pallas import tpu as pltpu

F32 = jnp.float32
BF = jnp.bfloat16
MESH = pl.DeviceIdType.MESH
ANY = pl.BlockSpec(memory_space=pl.ANY)

N_DEV = 8
D = 2048
W = 1024
EVEN_IN = 5376
ODD_IN = 4096
CHUNK = 128
ALPHA = (2 * 4) ** 0.25
LN_EPS = 1e-5
ROPE_THETA = 500000.0
LRU_C = 8.0
LR, B1, B2, ADAM_EPS, WD, STEP = 0.001, 0.9, 0.999, 1e-08, 0.01, 10
NEG = -1e30
HEAD_COLS = 4


def _cp(vmem_mb=48, collective_id=None):
    return pltpu.CompilerParams(vmem_limit_bytes=vmem_mb * 1024 * 1024, collective_id=collective_id)


def _sig(x):
    return jax.nn.sigmoid(x)


def _silu_grad(x):
    s = _sig(x)
    return x * s, s * (1.0 + x * (1.0 - s))


def _dot(a, b):
    return jnp.dot(a, b, preferred_element_type=F32)


def _dot_nt(a, b):
    return lax.dot_general(a, b, (((1,), (1,)), ((), ())), preferred_element_type=F32)


def _dot_tn(a, b):
    return lax.dot_general(a, b, (((0,), (0,)), ((), ())), preferred_element_type=F32)


def _coords():
    return lax.axis_index("x"), lax.axis_index("y"), lax.axis_index("c")


def _chip(j):
    x, y, _ = _coords()
    return (1 - x if j & 2 else x), (1 - y if j & 1 else y)


X_NB, Y_NB, DIAG, SIB = 4, 2, 6, 1
EVERYONE = frozenset(range(1, N_DEV))
BARRIER_IDS = {}


class _Comm:
    def collective_id(self):
        return BARRIER_IDS.setdefault(frozenset(self.peers), len(BARRIER_IDS))

    def start(self, ins, outs, sems):
        barrier = pltpu.get_barrier_semaphore()
        for d in sorted(self.peers):
            pl.semaphore_signal(barrier, inc=1, device_id=_peer(d)[0], device_id_type=MESH)
        pl.semaphore_wait(barrier, len(self.peers))
        for cp in self.copies(ins, outs, sems):
            cp.start()

    def wait(self, ins, outs, sems):
        for cp in self.copies(ins, outs, sems):
            cp.wait()


class _Join(_Comm):
    def __init__(self, parts):
        self.parts = list(parts)
        self.peers = frozenset().union(*[p.peers for p in self.parts])
        self.inputs = [a for p in self.parts for a in p.inputs]
        self.out_shapes = [s for p in self.parts for s in p.out_shapes]
        self.sem_shapes = [s for p in self.parts for s in p.sem_shapes]
        self.aliases = {}
        i0 = o0 = 0
        for p in self.parts:
            for i, o in getattr(p, "aliases", {}).items():
                self.aliases[i0 + i] = o0 + o
            i0, o0 = i0 + len(p.inputs), o0 + len(p.out_shapes)

    def copies(self, ins, outs, sems):
        res = []
        i0 = o0 = s0 = 0
        for p in self.parts:
            ni, no, ns = len(p.inputs), len(p.out_shapes), len(p.sem_shapes)
            res += p.copies(ins[i0:i0 + ni], outs[o0:o0 + no], sems[s0:s0 + ns])
            i0, o0, s0 = i0 + ni, o0 + no, s0 + ns
        return res


ROWS_US = {"ag1": 0.104, "ag2": 0.052, "agd": 0.027, "rsd": 0.027, "rs": 0.205}
N_COPIES = {"ag1": 2, "ag2": 2, "agd": 4, "rsd": 4, "rs": 3}
TASK_PEERS = {"ag1": {X_NB, Y_NB}, "ag2": {X_NB, Y_NB}, "agd": {SIB}, "rsd": {SIB}, "rs": {X_NB, Y_NB, DIAG}}
ROW_CHUNK = {672: 224, 512: 128, 256: 128}
CARRY_US = {"mm_h_even": 58, "mm_h_odd": 47, "even_fwd": 42, "odd_c_fwd": 37, "mm_out_ln": 33, "ln_bwd": 23, "mm_dmix": 26,
            "mm_dw_out": 25, "even_bwd": 90, "odd_c_bwd": 58, "mm_dw_in_even": 58, "mm_dw_in_odd": 44, "mm_dx_even": 66,
            "mm_dx_odd": 55, "adamw_even_w_out": 11, "adamw_odd_w_out": 11}
FWD_OVERBOOK = 1.15
FIRST_CARRY_US = 60.0
FLUSH_EXTRA_US = 60.0


def _cost_us(task, reg):
    kind, src, _, lo, hi = task
    return ROWS_US[kind] * (hi - lo) * reg[src].shape[-1] * reg[src].dtype.itemsize / 4096.0


class _Copies(_Comm):
    def __init__(self, tasks, reg):
        self.tasks = list(tasks)
        self.out_names, self.in_names = [], []
        for kind, src, dst, lo, hi in self.tasks:
            if dst not in self.out_names:
                self.out_names.append(dst)
        for kind, src, dst, lo, hi in self.tasks:
            if src not in self.out_names and src not in self.in_names:
                self.in_names.append(src)
        self.out_shapes, self.aliases = [], {}
        for o, dst in enumerate(self.out_names):
            if dst in reg:
                self.aliases[len(self.in_names)] = o
                self.in_names.append(dst)
                self.out_shapes.append(jax.ShapeDtypeStruct(reg[dst].shape, reg[dst].dtype))
            else:
                kind, src = next((t[0], t[1]) for t in self.tasks if t[2] == dst)
                shape = ({"rsd": 4, "rs": 3}[kind],) + reg[src].shape[1:]
                self.out_shapes.append(jax.ShapeDtypeStruct(shape, reg[src].dtype))
        self.inputs = [reg[nm] for nm in self.in_names]
        n = sum(N_COPIES[t[0]] for t in self.tasks)
        self.sem_shapes = [pltpu.SemaphoreType.DMA((n,)), pltpu.SemaphoreType.DMA((n,))]
        self.peers = frozenset().union(*[TASK_PEERS[t[0]] for t in self.tasks])

    def copies(self, ins, outs, sems):
        send, recv = sems
        x, y, c = _coords()
        me = 4 * x + 2 * y + c
        xn, yn = (1 - x, y, c), (x, 1 - y, c)
        at_xn, at_yn = 4 * (1 - x) + 2 * y + c, 4 * x + 2 * (1 - y) + c
        ref = dict(zip(self.in_names, ins))
        ref.update(zip(self.out_names, outs))
        res = []

        def copy(src, dst, to):
            i = len(res)
            res.append(pltpu.make_async_remote_copy(src_ref=src, dst_ref=dst, send_sem=send.at[i], recv_sem=recv.at[i],
                                                    device_id=to, device_id_type=MESH))

        for kind, src, dst, lo, hi in self.tasks:
            n = hi - lo
            if kind == "ag1":
                for to in (xn, yn):
                    copy(ref[src].at[pl.ds(lo, n)], ref[dst].at[me, pl.ds(lo, n)], to)
            elif kind == "ag2":
                h = n // 2
                first, second = ref[dst].at[at_xn, pl.ds(lo, h)], ref[dst].at[at_yn, pl.ds(lo + h, n - h)]
                copy(first, first, yn)
                copy(second, second, xn)
            elif kind == "agd":
                for j in range(4):
                    px, py = _chip(j)
                    rows = ref[dst].at[4 * px + 2 * py + c, pl.ds(lo, n)]
                    copy(rows, rows, (x, y, 1 - c))
            elif kind == "rsd":
                for j in range(4):
                    px, py = _chip(j)
                    copy(ref[src].at[4 * px + 2 * py + 1 - c, pl.ds(lo, n)], ref[dst].at[j, pl.ds(lo, n)], (x, y, 1 - c))
            else:
                for j in (1, 2, 3):
                    px, py = _chip(j)
                    copy(ref[src].at[j, pl.ds(lo, n)], ref[dst].at[j - 1, pl.ds(lo, n)], (px, py, c))
        return res


class _Sched:
    def __init__(self, reg):
        self.reg, self.queue, self.later = reg, [], []
        self.overhang = 0.5
        self.after_landing = None

    def add(self, tasks, first=False):
        self.queue = list(tasks) + self.queue if first else self.queue + list(tasks)

    def pending(self, dst):
        return any(t[2] == dst for t in self.queue + self.later)

    def take(self, budget_us, must=None, overhang=0.5):
        self.queue, self.later = self.later + self.queue, []
        picked, us = [], 0.0
        rest = []
        for t in self.queue:
            cost = _cost_us(t, self.reg)
            if (must is not None and t[2] == must) or us + (1.0 - overhang) * cost <= budget_us:
                picked.append(t)
                us += cost
                if t[0] in ("ag1", "ag2"):
                    self.later.append(({"ag1": "ag2", "ag2": "agd"}[t[0]], t[2], t[2], t[3], t[4]))
            else:
                rest.append(t)
        self.queue = rest
        return _Copies(picked, self.reg) if picked else None

    def landed(self, comm, got):
        if comm is not None:
            for nm, a in zip(comm.out_names, got):
                self.reg[nm] = a
        if self.after_landing is not None:
            self.after_landing()

    def run(self, builder, budget_us, *args, **kw):
        comm = self.take(budget_us, overhang=self.overhang)
        res, got = builder(*args, comm=comm, **kw)
        self.landed(comm, got)
        return res

    def flush(self, dst, budget_us=0.0, beside=None):
        res = []
        while self.pending(dst):
            comm = self.take(budget_us, must=dst)
            got = _comm_only(comm if beside is None else _Join([comm, beside]), "flush_" + dst)
            res, beside = got[len(comm.out_shapes):], None
            self.landed(comm, got[:len(comm.out_shapes)])
        return res


def _rows(name_src, name_dst, kind, n_rows, chunk):
    return [(kind, name_src, name_dst, lo, min(lo + chunk, n_rows)) for lo in range(0, n_rows, chunk)]


def _pcall(body, *, grid, in_specs, out_specs, out_shape, name, scratch=(), vmem=48, comm=None):
    in_specs, out_specs, out_shape, scratch = list(in_specs), list(out_specs), list(out_shape), list(scratch)
    if comm is None:
        call = pl.pallas_call(body, grid=grid, in_specs=in_specs, out_specs=out_specs, out_shape=out_shape,
                              scratch_shapes=scratch, name=name, compiler_params=_cp(vmem))
        return lambda *args: (call(*args), [])
    n_in, n_out, n_scr = len(in_specs), len(out_specs), len(scratch)
    c_in, c_out = len(comm.inputs), len(comm.out_shapes)
    aliases = {n_in + i: n_out + o for i, o in getattr(comm, "aliases", {}).items()}

    def wrapped(*refs):
        ins, cins = refs[:n_in], refs[n_in:n_in + c_in]
        o0 = n_in + c_in
        outs, couts = refs[o0:o0 + n_out], refs[o0 + n_out:o0 + n_out + c_out]
        s0 = o0 + n_out + c_out
        scr, sems = refs[s0:s0 + n_scr], refs[s0 + n_scr:]
        ids = [pl.program_id(a) for a in range(len(grid))]
        first = functools.reduce(jnp.logical_and, [i == 0 for i in ids])
        last = functools.reduce(jnp.logical_and, [i == g - 1 for i, g in zip(ids, grid)])

        @pl.when(first)
        def _():
            comm.start(cins, couts, sems)

        body(*ins, *outs, *scr)

        @pl.when(last)
        def _():
            comm.wait(cins, couts, sems)

    call = pl.pallas_call(wrapped, grid=grid, in_specs=in_specs + [ANY] * c_in, out_specs=out_specs + [ANY] * c_out,
                          out_shape=out_shape + list(comm.out_shapes), scratch_shapes=scratch + list(comm.sem_shapes),
                          input_output_aliases=aliases, name=name, compiler_params=_cp(vmem, comm.collective_id()))

    def run(*args):
        res = call(*args, *comm.inputs)
        return res[:n_out], res[n_out:]

    return run


def _comm_only(comm, name):
    c_in, c_out = len(comm.inputs), len(comm.out_shapes)

    def body(*refs):
        cins, couts, sems = refs[:c_in], refs[c_in:c_in + c_out], refs[c_in + c_out:]
        comm.start(cins, couts, sems)
        comm.wait(cins, couts, sems)

    return pl.pallas_call(body, in_specs=[ANY] * c_in, out_specs=[ANY] * c_out, out_shape=list(comm.out_shapes),
                          scratch_shapes=list(comm.sem_shapes), input_output_aliases=dict(getattr(comm, "aliases", {})),
                          name=name, compiler_params=pltpu.CompilerParams(collective_id=comm.collective_id()))(*comm.inputs)


def _chip_blocks():
    _, _, c = _coords()
    return jnp.stack([4 * px + 2 * py + c for px, py in map(_chip, range(4))]).astype(jnp.int32)


def _add_pairs(g8, b4, name):
    _, R, C = b4.shape

    def body(idx_ref, a_ref, b_ref, o_ref):
        o_ref[...] = (a_ref[...].astype(F32) + b_ref[...].astype(F32)).astype(BF)

    blk = pl.BlockSpec((None, R, C), lambda j, idx: (j, 0, 0))
    grid_spec = pltpu.PrefetchScalarGridSpec(
        num_scalar_prefetch=1, grid=(4,),
        in_specs=[pl.BlockSpec((None, R, C), lambda j, idx: (idx[j], 0, 0)), blk], out_specs=blk)
    return pl.pallas_call(body, grid_spec=grid_spec, out_shape=jax.ShapeDtypeStruct(b4.shape, BF), name=name,
                          compiler_params=_cp())(_chip_blocks(), g8, b4)


def _rs_final(s4, r3, name):
    _, R, C = s4.shape
    tr = R // 2

    def body(s_ref, r_ref, o_ref):
        o_ref[...] = ((s_ref[...].astype(F32) + r_ref[0].astype(F32)) + r_ref[1].astype(F32)) + r_ref[2].astype(F32)

    return pl.pallas_call(
        body, grid=(2,),
        in_specs=[pl.BlockSpec((None, tr, C), lambda i: (0, i, 0)), pl.BlockSpec((3, tr, C), lambda i: (0, i, 0))],
        out_specs=pl.BlockSpec((tr, C), lambda i: (i, 0)), out_shape=jax.ShapeDtypeStruct((R, C), F32),
        name=name, compiler_params=_cp())(s4, r3)


def _mm_nt(a, w, tm, tn, name, out3=False, comm=None):
    M, K = a.shape
    N = w.shape[0]
    tm = min(tm, M)

    def body(a_ref, w_ref, o_ref):
        o_ref[...] = _dot_nt(a_ref[...], w_ref[...])

    if out3:
        per = W // tn
        out_shape = jax.ShapeDtypeStruct((N // W, M, W), F32)
        out_spec = pl.BlockSpec((None, tm, tn), lambda i, j: (j // per, i, j % per))
    else:
        out_shape = jax.ShapeDtypeStruct((M, N), F32)
        out_spec = pl.BlockSpec((tm, tn), lambda i, j: (i, j))
    (res,), extra = _pcall(
        body, grid=(M // tm, N // tn),
        in_specs=[pl.BlockSpec((tm, K), lambda i, j: (i, 0)), pl.BlockSpec((tn, K), lambda i, j: (j, 0))],
        out_specs=[out_spec], out_shape=[out_shape], name=name, comm=comm)(a, w)
    return res, extra


def _mm_tn(a, b, tm, name, comm=None):
    K, N = b.shape
    if a.ndim == 3:
        M = a.shape[0] * W
        per = W // tm
        a_spec = pl.BlockSpec((None, K, tm), lambda i: (i // per, 0, i % per))
    else:
        M = a.shape[1]
        a_spec = pl.BlockSpec((K, tm), lambda i: (0, i))

    def body(a_ref, b_ref, o_ref):
        o_ref[...] = _dot_tn(a_ref[...], b_ref[...]).astype(BF)

    (out,), extra = _pcall(
        body, grid=(M // tm,),
        in_specs=[a_spec, pl.BlockSpec((K, N), lambda i: (0, 0))],
        out_specs=[pl.BlockSpec((tm, N), lambda i: (i, 0))],
        out_shape=[jax.ShapeDtypeStruct((M, N), BF)], name=name, vmem=56, comm=comm)(a, b)
    return out, extra


def _cast_rows(x, name, comm=None):
    S = x.shape[0]
    tm = min(512, S)

    def body(x_ref, o_ref):
        o_ref[...] = x_ref[...].astype(BF)

    row = pl.BlockSpec((tm, D), lambda i: (i, 0))
    return _pcall(body, grid=(S // tm,), in_specs=[row], out_specs=[row], out_shape=[jax.ShapeDtypeStruct((S, D), BF)],
                  name=name, comm=comm)(x)


def _mm_dz(dzb, w_out, mix3, name, comm=None):
    S = dzb.shape[0]
    tw, tm, tn = 512, min(1024, S), 1024
    n_w, n_j = D // tw, D // tn
    n_d = (S // tm) * n_j
    per_w, per_n = W // tw, W // tn

    def body(dz_ref, mix_ref, dzrow_ref, w_ref, dw_ref, dm_ref):
        s = pl.program_id(0)

        @pl.when(s < n_w)
        def _():
            dw_ref[...] = _dot_tn(mix_ref[...], dz_ref[...]).astype(BF)

        @pl.when(s >= n_w)
        def _():
            dm_ref[...] = _dot_nt(dzrow_ref[...], w_ref[...])

    first = lambda s: jnp.minimum(s, n_w - 1)
    then = lambda s: jnp.maximum(s - n_w, 0)
    return _pcall(
        body, grid=(n_w + n_d,),
        in_specs=[pl.BlockSpec((S, D), lambda s: (0, 0), pipeline_mode=pl.Buffered(1)),
                  pl.BlockSpec((None, S, tw), lambda s: (first(s) // per_w, 0, first(s) % per_w)),
                  pl.BlockSpec((tm, D), lambda s: (then(s) // n_j, 0)),
                  pl.BlockSpec((tn, D), lambda s: (then(s) % n_j, 0))],
        out_specs=[pl.BlockSpec((tw, D), lambda s: (first(s), 0)),
                   pl.BlockSpec((None, tm, tn), lambda s: ((then(s) % n_j) // per_n, then(s) // n_j, (then(s) % n_j) % per_n))],
        out_shape=[jax.ShapeDtypeStruct((D, D), BF), jax.ShapeDtypeStruct((2, S, W), F32)],
        name=name, vmem=56, comm=comm)(dzb, mix3, dzb, w_out)


def _mm_nn_res(a, w, res, tm, tn, name, comm=None):
    K, N = w.shape
    if a.ndim == 3:
        P, M = a.shape[0], a.shape[1]
        tm = min(tm, M)
        a_spec = pl.BlockSpec((P, tm, W), lambda j, i: (0, i, 0))
    else:
        P, M = 0, a.shape[0]
        tm = min(tm, M)
        a_spec = pl.BlockSpec((tm, K), lambda j, i: (i, 0))

    def body(a_ref, w_ref, r_ref, o_ref):
        if P:
            d = _dot(a_ref[0], w_ref[0:W, :])
            for p in range(1, P):
                d = d + _dot(a_ref[p], w_ref[p * W:(p + 1) * W, :])
        else:
            d = _dot(a_ref[...], w_ref[...])
        o_ref[...] = ALPHA * r_ref[...] + d

    (out,), extra = _pcall(
        body, grid=(N // tn, M // tm),
        in_specs=[a_spec, pl.BlockSpec((K, tn), lambda j, i: (0, j)), pl.BlockSpec((tm, tn), lambda j, i: (i, j))],
        out_specs=[pl.BlockSpec((tm, tn), lambda j, i: (i, j))],
        out_shape=[jax.ShapeDtypeStruct((M, N), F32)], name=name, comm=comm)(a, w, res)
    return out, extra


def _mm_out_ln(mix3, w_out, x, g, b, name, comm=None):
    S = x.shape[0]
    tm = min(512, S)

    def body(m_ref, w_ref, x_ref, g_ref, b_ref, z_ref, xn_ref, xb_ref):
        acc = _dot(m_ref[0], w_ref[0:W, :]) + _dot(m_ref[1], w_ref[W:2 * W, :])
        z = ALPHA * x_ref[...] + acc
        mu = jnp.mean(z, axis=1, keepdims=True)
        zc = z - mu
        var = jnp.mean(zc * zc, axis=1, keepdims=True)
        xn = zc * lax.rsqrt(var + LN_EPS) * g_ref[...] + b_ref[...]
        z_ref[...] = z
        xn_ref[...] = xn
        xb_ref[...] = xn.astype(BF)

    row = pl.BlockSpec((tm, D), lambda i: (i, 0))
    vec = pl.BlockSpec((1, D), lambda i: (0, 0))
    return _pcall(
        body, grid=(S // tm,),
        in_specs=[pl.BlockSpec((2, tm, W), lambda i: (0, i, 0)),
                  pl.BlockSpec((D, D), lambda i: (0, 0), pipeline_mode=pl.Buffered(1)), row, vec, vec],
        out_specs=[row, row, row],
        out_shape=[jax.ShapeDtypeStruct((S, D), F32), jax.ShapeDtypeStruct((S, D), F32), jax.ShapeDtypeStruct((S, D), BF)],
        name=name, comm=comm)(mix3, w_out, x, g.reshape(1, D), b.reshape(1, D))


def _ln_bwd(dxn, z, g, name, comm=None, target=None):
    S = z.shape[0]
    tm = min(256, S)
    head = target is not None

    def body(*refs):
        if head:
            d_ref, t_ref, z_ref, g_ref, dz_ref, dzb_ref, dg_ref, db_ref, p_ref = refs
        else:
            d_ref, z_ref, g_ref, dz_ref, dzb_ref, dg_ref, db_ref = refs
        i = pl.program_id(0)
        zz = z_ref[...]
        mu = jnp.mean(zz, axis=1, keepdims=True)
        zc = zz - mu
        var = jnp.mean(zc * zc, axis=1, keepdims=True)
        rstd = lax.rsqrt(var + LN_EPS)
        xhat = zc * rstd
        dy = d_ref[...]
        if head:
            e = dy - t_ref[...]
            dy = e * (1.0 / D)

            @pl.when(i == 0)
            def _():
                p_ref[...] = jnp.zeros_like(p_ref)

            p_ref[...] += jnp.sum(jnp.sum(e * e, axis=1, keepdims=True), axis=0, keepdims=True)
        dyg = dy * g_ref[...]
        m1 = jnp.mean(dyg, axis=1, keepdims=True)
        m2 = jnp.mean(dyg * xhat, axis=1, keepdims=True)
        dz = rstd * (dyg - m1 - xhat * m2)
        dz_ref[...] = dz
        dzb_ref[...] = dz.astype(BF)

        @pl.when(i == 0)
        def _():
            dg_ref[...] = jnp.zeros_like(dg_ref)
            db_ref[...] = jnp.zeros_like(db_ref)

        dg_ref[...] += jnp.sum(dy * xhat, axis=0, keepdims=True)
        db_ref[...] += jnp.sum(dy, axis=0, keepdims=True)

    row = pl.BlockSpec((tm, D), lambda i: (i, 0))
    vec = pl.BlockSpec((1, D), lambda i: (0, 0))
    out_specs = [row, row, vec, vec] + ([pl.BlockSpec((8, 128), lambda i: (0, 0))] if head else [])
    out_shape = [jax.ShapeDtypeStruct((S, D), F32), jax.ShapeDtypeStruct((S, D), BF), jax.ShapeDtypeStruct((1, D), F32),
                 jax.ShapeDtypeStruct((1, D), F32)] + ([jax.ShapeDtypeStruct((8, 128), F32)] if head else [])
    operands = (dxn, target, z, g.reshape(1, D)) if head else (dxn, z, g.reshape(1, D))
    return _pcall(body, grid=(S // tm,), in_specs=[row] * (len(operands) - 1) + [vec], out_specs=out_specs,
                  out_shape=out_shape, name=name, comm=comm)(*operands)


def _rope_fwd(t, r_ref):
    return (t * r_ref[:, 0:128] + pltpu.roll(t, 120, 1) * r_ref[:, 128:256]
            + pltpu.roll(t, 8, 1) * r_ref[:, 256:384])


def _rope_bwd(g, r_ref):
    return (g * r_ref[:, 0:128] + pltpu.roll(g * r_ref[:, 128:256], 8, 1)
            + pltpu.roll(g * r_ref[:, 256:384], 120, 1))


def _dup_heads(kb):
    lo = lax.broadcasted_iota(jnp.int32, kb.shape, 1) < 64
    sw = pltpu.roll(kb, 64, 1)
    return [jnp.where(lo, kb, sw).astype(BF), jnp.where(lo, sw, kb).astype(BF)]


def _even_fwd(h, rope, lng, lnb, ws, bsb, sinks, name, comm=None):
    S = h.shape[0]
    nb = S // CHUNK

    def body(h_ref, hp_ref, rc_ref, rp_ref, lng_ref, lnb_ref, ws_ref, bsb_ref, sink_ref, mix_ref, o_ref, l_ref):
        n = pl.program_id(0)
        lane = lax.broadcasted_iota(jnp.int32, (128, 128), 1)
        rowi = lax.broadcasted_iota(jnp.int32, (128, 128), 0)
        tri = rowi >= lane
        lane_lo = lane < 64
        v = h_ref[:, W:2 * W]
        mu = jnp.mean(v, axis=1, keepdims=True)
        vc = v - mu
        var = jnp.mean(vc * vc, axis=1, keepdims=True)
        vn = vc * lax.rsqrt(var + LN_EPS) * lng_ref[...] + lnb_ref[...]
        ms = [_dot(jnp.where(tri, ws_ref[g], 0.0).astype(BF), vn[:, g * 128:(g + 1) * 128].astype(BF)) for g in range(8)]
        for g in range(8):
            sl = slice(g * 128, (g + 1) * 128)
            ag = h_ref[:, 2 * W + g * 128:2 * W + (g + 1) * 128]
            mix_ref[0, :, sl] = (h_ref[:, sl] * (ms[g] + bsb_ref[g]) * (ag * _sig(ag))).astype(BF)
        kb = jnp.concatenate([_rope_fwd(hp_ref[:, 0:128], rp_ref), _rope_fwd(h_ref[:, 4096:4224], rc_ref)], axis=0)
        vb = jnp.concatenate([hp_ref[:, 128:256], h_ref[:, 4224:4352]], axis=0)
        k2 = _dup_heads(kb)
        v2 = _dup_heads(vb)
        qi = lax.broadcasted_iota(jnp.int32, (128, 256), 0)
        kj = lax.broadcasted_iota(jnp.int32, (128, 256), 1)
        diff = qi + 128 - kj
        valid = (diff >= 0) & (diff < 128) & ((n > 0) | (kj >= 128))
        lacc = jnp.zeros((128, 128), F32)
        for j0 in range(0, 8, HEAD_COLS):
            heads = [(j, half) for j in range(j0, j0 + HEAD_COLS) for half in range(2)]
            sc, pr, oh = {}, {}, {}
            for j in range(j0, j0 + HEAD_COLS):
                qc = _rope_fwd(h_ref[:, 3072 + j * 128:3072 + (j + 1) * 128], rc_ref)
                sc[j, 0] = _dot_nt(jnp.where(lane_lo, qc, 0.0).astype(BF), k2[j // 4])
                sc[j, 1] = _dot_nt(jnp.where(lane_lo, 0.0, qc).astype(BF), k2[j // 4])
            for j, half in heads:
                hq = 2 * j + half
                s = jnp.where(valid, sc[j, half] * 0.125, NEG)
                sk = sink_ref[hq]
                mx = jnp.maximum(jnp.max(s, axis=1, keepdims=True), sk)
                p = jnp.exp(s - mx)
                den = jnp.sum(p, axis=1, keepdims=True) + jnp.exp(sk - mx)
                pr[j, half] = (p / den).astype(BF)
                lacc = jnp.where(lane == hq, mx + jnp.log(den), lacc)
            for j, half in heads:
                oh[j, half] = _dot(pr[j, half], v2[j // 4])
            for j in range(j0, j0 + HEAD_COLS):
                cs = slice(j * 128, (j + 1) * 128)
                ocol = jnp.where(lane_lo, oh[j, 0], oh[j, 1])
                bg = h_ref[:, 4352 + j * 128:4352 + (j + 1) * 128]
                o_ref[:, cs] = ocol
                mix_ref[1, :, cs] = (ocol * (bg * _sig(bg))).astype(BF)
        l_ref[...] = lacc

    prev = lambda n: jnp.maximum(n - 1, 0)
    full = lambda shape: pl.BlockSpec(shape, lambda n: (0,) * len(shape))
    return _pcall(
        body, grid=(nb,),
        in_specs=[pl.BlockSpec((CHUNK, EVEN_IN), lambda n: (n, 0)),
                  pl.BlockSpec((CHUNK, 256), lambda n: (prev(n), 16)),
                  pl.BlockSpec((CHUNK, 384), lambda n: (n, 0)),
                  pl.BlockSpec((CHUNK, 384), lambda n: (prev(n), 0)),
                  full((1, W)), full((1, W)), full((8, 128, 128)), full((8, 128, 128)),
                  pl.BlockSpec(memory_space=pltpu.SMEM)],
        out_specs=[pl.BlockSpec((2, CHUNK, W), lambda n: (0, n, 0)),
                   pl.BlockSpec((CHUNK, W), lambda n: (n, 0)),
                   pl.BlockSpec((CHUNK, 128), lambda n: (n, 0))],
        out_shape=[jax.ShapeDtypeStruct((2, S, W), BF), jax.ShapeDtypeStruct((S, W), F32),
                   jax.ShapeDtypeStruct((S, 128), F32)],
        name=name, comm=comm)(h, h, rope, rope, lng.reshape(1, W), lnb.reshape(1, W), ws, bsb, sinks)


def _even_bwd(h, dmix3, o, l, rope, lng, lnb, ws, wst, bsb, sinks, name, comm=None):
    S = h.shape[0]
    nb = S // CHUNK

    def body(h_ref, hp_ref, hn_ref, dm_ref, dmn_ref, o_ref, on_ref, l_ref, ln_ref, rc_ref, rp_ref, rn_ref,
             lng_ref, lnb_ref, ws_ref, wst_ref, bsb_ref, sink_ref,
             dh_ref, dws_ref, dbs_ref, dlng_ref, dlnb_ref, dsink_ref, dvn_ref):
        n = pl.program_id(0)

        @pl.when(n == 0)
        def _():
            dws_ref[...] = jnp.zeros_like(dws_ref)
            dbs_ref[...] = jnp.zeros_like(dbs_ref)
            dlng_ref[...] = jnp.zeros_like(dlng_ref)
            dlnb_ref[...] = jnp.zeros_like(dlnb_ref)
            dsink_ref[...] = jnp.zeros_like(dsink_ref)

        lane = lax.broadcasted_iota(jnp.int32, (128, 128), 1)
        rowi = lax.broadcasted_iota(jnp.int32, (128, 128), 0)
        lane1 = lax.broadcasted_iota(jnp.int32, (1, 128), 1)
        tri = rowi >= lane
        tri_t = lane >= rowi
        lane_lo = lane < 64
        v = h_ref[:, W:2 * W]
        mu = jnp.mean(v, axis=1, keepdims=True)
        vc = v - mu
        var = jnp.mean(vc * vc, axis=1, keepdims=True)
        rstd = lax.rsqrt(var + LN_EPS)
        vhat = vc * rstd
        vn = vhat * lng_ref[...] + lnb_ref[...]
        dbs_acc = jnp.zeros((128, 128), F32)
        vng = [vn[:, g * 128:(g + 1) * 128].astype(BF) for g in range(8)]
        ms = [_dot(jnp.where(tri, ws_ref[g], 0.0).astype(BF), vng[g]) for g in range(8)]
        dmb = []
        for g in range(8):
            sl = slice(g * 128, (g + 1) * 128)
            m = ms[g] + bsb_ref[g]
            ag = h_ref[:, 2 * W + g * 128:2 * W + (g + 1) * 128]
            sg, dsg = _silu_grad(ag)
            u = h_ref[:, sl]
            da = dm_ref[0, :, sl]
            dmm = da * u * sg
            dh_ref[:, sl] = (da * m * sg).astype(BF)
            dh_ref[:, 2 * W + g * 128:2 * W + (g + 1) * 128] = (da * u * m * dsg).astype(BF)
            dmb.append(dmm.astype(BF))
            dbs_acc = jnp.where(lane == g, jnp.sum(dmm, axis=1, keepdims=True), dbs_acc)
        dvs = [_dot(jnp.where(tri_t, wst_ref[g], 0.0).astype(BF), dmb[g]) for g in range(8)]
        dwss = [_dot_nt(dmb[g], vng[g]) for g in range(8)]
        for g in range(8):
            dvn_ref[:, g * 128:(g + 1) * 128] = dvs[g]
            dws_ref[g] += jnp.where(tri, dwss[g], 0.0)
        dbs_ref[...] += dbs_acc
        dvn = dvn_ref[...]
        dlng_ref[...] += jnp.sum(dvn * vhat, axis=0, keepdims=True)
        dlnb_ref[...] += jnp.sum(dvn, axis=0, keepdims=True)
        dyg = dvn * lng_ref[...]
        m1 = jnp.mean(dyg, axis=1, keepdims=True)
        m2 = jnp.mean(dyg * vhat, axis=1, keepdims=True)
        dh_ref[:, W:2 * W] = (rstd * (dyg - m1 - vhat * m2)).astype(BF)
        kcur = _rope_fwd(h_ref[:, 4096:4224], rc_ref)
        kb = jnp.concatenate([_rope_fwd(hp_ref[:, 0:128], rp_ref), kcur], axis=0)
        vb = jnp.concatenate([hp_ref[:, 128:256], h_ref[:, 4224:4352]], axis=0)
        k2 = _dup_heads(kb)
        v2 = _dup_heads(vb)
        kc2 = _dup_heads(kcur)
        vc2 = _dup_heads(h_ref[:, 4224:4352])
        qi = lax.broadcasted_iota(jnp.int32, (128, 256), 0)
        kj = lax.broadcasted_iota(jnp.int32, (128, 256), 1)
        diff = qi + 128 - kj
        valid = (diff >= 0) & (diff < 128) & ((n > 0) | (kj >= 128))
        validn = (lane > rowi) & (n < nb - 1)
        lc = l_ref[...]
        lnx = ln_ref[...]
        dk = [jnp.zeros((128, 128), F32), jnp.zeros((128, 128), F32)]
        dv = [jnp.zeros((128, 128), F32), jnp.zeros((128, 128), F32)]
        dsk_acc = jnp.zeros((1, 128), F32)
        for j0 in range(0, 8, HEAD_COLS):
            heads = [(j, half) for j in range(j0, j0 + HEAD_COLS) for half in range(2)]
            t = {}
            for j in range(j0, j0 + HEAD_COLS):
                cs = slice(j * 128, (j + 1) * 128)
                qc = _rope_fwd(h_ref[:, 3072 + j * 128:3072 + (j + 1) * 128], rc_ref)
                qn = _rope_fwd(hn_ref[:, 3072 + j * 128:3072 + (j + 1) * 128], rn_ref)
                bg = h_ref[:, 4352 + j * 128:4352 + (j + 1) * 128]
                sgb, dsgb = _silu_grad(bg)
                db = dm_ref[1, :, cs]
                oc = o_ref[:, cs]
                do = db * sgb
                dh_ref[:, 4352 + j * 128:4352 + (j + 1) * 128] = (db * oc * dsgb).astype(BF)
                bgn = hn_ref[:, 4352 + j * 128:4352 + (j + 1) * 128]
                don = dmn_ref[1, :, cs] * (bgn * _sig(bgn))
                prod = do * oc
                prodn = don * on_ref[:, cs]
                for half in range(2):
                    hq = 2 * j + half
                    hm = lane_lo if half == 0 else jnp.logical_not(lane_lo)
                    t[j, half] = dict(
                        dsum=jnp.sum(jnp.where(hm, prod, 0.0), axis=1, keepdims=True),
                        dsumn=jnp.sum(jnp.where(hm, prodn, 0.0), axis=1, keepdims=True),
                        lh=jnp.sum(jnp.where(lane == hq, lc, 0.0), axis=1, keepdims=True),
                        lhn=jnp.sum(jnp.where(lane == hq, lnx, 0.0), axis=1, keepdims=True),
                        qm=jnp.where(hm, qc, 0.0).astype(BF), dom=jnp.where(hm, do, 0.0).astype(BF),
                        qnm=jnp.where(hm, qn, 0.0).astype(BF), donm=jnp.where(hm, don, 0.0).astype(BF))
            for j, half in heads:
                e, hk = t[j, half], j // 4
                e["s"], e["dp"] = _dot_nt(e["qm"], k2[hk]), _dot_nt(e["dom"], v2[hk])
                e["sn"], e["dpn"] = _dot_nt(e["qnm"], kc2[hk]), _dot_nt(e["donm"], vc2[hk])
            for j, half in heads:
                e, hq = t[j, half], 2 * j + half
                p = jnp.exp(jnp.where(valid, e["s"] * 0.125 - e["lh"], NEG))
                ds = p * (e["dp"] - e["dsum"])
                pn = jnp.exp(jnp.where(validn, e["sn"] * 0.125 - e["lhn"], NEG))
                dsn = pn * (e["dpn"] - e["dsumn"])
                psink = jnp.exp(sink_ref[hq] - e["lh"])
                dsk_acc = jnp.where(lane1 == hq, -jnp.sum(psink * e["dsum"], axis=0, keepdims=True), dsk_acc)
                e["ds"] = ds.astype(BF)
                e["pt"], e["dst"] = jnp.transpose(p[:, 128:256]).astype(BF), jnp.transpose(ds[:, 128:256]).astype(BF)
                e["pnt"], e["dsnt"] = jnp.transpose(pn).astype(BF), jnp.transpose(dsn).astype(BF)
            for j, half in heads:
                e, hk = t[j, half], j // 4
                e["dq"] = _dot(e["ds"], k2[hk])
                e["dv"] = _dot(e["pt"], e["dom"]) + _dot(e["pnt"], e["donm"])
                e["dk"] = _dot(e["dst"], e["qm"]) + _dot(e["dsnt"], e["qnm"])
            for j in range(j0, j0 + HEAD_COLS):
                hk = j // 4
                dqcol = jnp.where(lane_lo, t[j, 0]["dq"], t[j, 1]["dq"]) * 0.125
                dh_ref[:, 3072 + j * 128:3072 + (j + 1) * 128] = _rope_bwd(dqcol, rc_ref).astype(BF)
                dv[hk] = dv[hk] + t[j, 0]["dv"] + t[j, 1]["dv"]
                dk[hk] = dk[hk] + (t[j, 0]["dk"] + t[j, 1]["dk"]) * 0.125
        fold = lambda a: a + pltpu.roll(a, 64, 1)
        dh_ref[:, 4096:4224] = _rope_bwd(jnp.where(lane_lo, fold(dk[0]), fold(dk[1])), rc_ref).astype(BF)
        dh_ref[:, 4224:4352] = jnp.where(lane_lo, fold(dv[0]), fold(dv[1])).astype(BF)
        dsink_ref[...] += dsk_acc

    prev = lambda n: jnp.maximum(n - 1, 0)
    nxt = lambda n: jnp.minimum(n + 1, nb - 1)
    full = lambda shape: pl.BlockSpec(shape, lambda n: (0,) * len(shape))
    return _pcall(
        body, grid=(nb,),
        in_specs=[pl.BlockSpec((CHUNK, EVEN_IN), lambda n: (n, 0)),
                  pl.BlockSpec((CHUNK, 256), lambda n: (prev(n), 16)),
                  pl.BlockSpec((CHUNK, EVEN_IN), lambda n: (nxt(n), 0)),
                  pl.BlockSpec((2, CHUNK, W), lambda n: (0, n, 0)),
                  pl.BlockSpec((2, CHUNK, W), lambda n: (0, nxt(n), 0)),
                  pl.BlockSpec((CHUNK, W), lambda n: (n, 0)),
                  pl.BlockSpec((CHUNK, W), lambda n: (nxt(n), 0)),
                  pl.BlockSpec((CHUNK, 128), lambda n: (n, 0)),
                  pl.BlockSpec((CHUNK, 128), lambda n: (nxt(n), 0)),
                  pl.BlockSpec((CHUNK, 384), lambda n: (n, 0)),
                  pl.BlockSpec((CHUNK, 384), lambda n: (prev(n), 0)),
                  pl.BlockSpec((CHUNK, 384), lambda n: (nxt(n), 0)),
                  full((1, W)), full((1, W)), full((8, 128, 128)), full((8, 128, 128)), full((8, 128, 128)),
                  pl.BlockSpec(memory_space=pltpu.SMEM)],
        out_specs=[pl.BlockSpec((CHUNK, EVEN_IN), lambda n: (n, 0)),
                   full((8, 128, 128)), full((128, 128)), full((1, W)), full((1, W)), full((1, 128))],
        out_shape=[jax.ShapeDtypeStruct((S, EVEN_IN), BF), jax.ShapeDtypeStruct((8, 128, 128), F32),
                   jax.ShapeDtypeStruct((128, 128), F32), jax.ShapeDtypeStruct((1, W), F32),
                   jax.ShapeDtypeStruct((1, W), F32), jax.ShapeDtypeStruct((1, 128), F32)],
        scratch=[pltpu.VMEM((CHUNK, W), F32)], name=name, comm=comm,
    )(h, h, h, dmix3, dmix3, o, o, l, l, rope, rope, rope, lng.reshape(1, W), lnb.reshape(1, W), ws, wst, bsb, sinks)


def _expm1(x):
    ser = x * (1.0 + x * (0.5 + x * (1.0 / 6.0 + x * (1.0 / 24.0))))
    return jnp.where(jnp.abs(x) < 1e-2, ser, jnp.exp(x) - 1.0)


def _softplus_neg(lam):
    z = -lam
    e = jnp.exp(-jnp.abs(z))
    l1p = jnp.where(e < 1e-3, e * (1.0 - e * (0.5 - e * (1.0 / 3.0))), jnp.log(1.0 + e))
    return jnp.maximum(z, 0.0) + l1p


def _shift_down(x, k, row, fill=0.0):
    return jnp.where(row >= k, pltpu.roll(x, k, 0), fill)


def _shift_up(x, k, row, fill=0.0):
    S = x.shape[0]
    return jnp.where(row < S - k, pltpu.roll(x, S - k, 0), fill)


def _lru_gates(xc, row, cw_ref, cb_ref, wa_ref, wx_ref, ba_ref, bx_ref, lam_ref):
    xconv = (cw_ref[3:4, :] * xc + cw_ref[2:3, :] * _shift_down(xc, 1, row) + cw_ref[1:2, :] * _shift_down(xc, 2, row)
             + cw_ref[0:1, :] * _shift_down(xc, 3, row) + cb_ref[...])
    xb = xconv.astype(BF)
    r = _sig(_dot(xb, wa_ref[...]) + ba_ref[...])
    i = _sig(_dot(xb, wx_ref[...]) + bx_ref[...])
    sp = _softplus_neg(lam_ref[...])
    log_a = -LRU_C * r * sp
    a = jnp.exp(log_a)
    mult = jnp.sqrt(-_expm1(2.0 * log_a))
    return xconv, r, i, sp, a, mult


ROWS_PER_TILE = 8


def _steps(a, b, shift, inside, products=True):
    n, k = inside.n, 1
    while k < n:
        b = a * jnp.where(inside(k), shift(b, k), 0.0) + b
        if products or 2 * k < n:
            a = a * jnp.where(inside(k), shift(a, k), 1.0)
        k *= 2
    return a, b


class _Inside:
    def __init__(self, pos, n, reverse):
        self.pos, self.n, self.reverse = pos, n, reverse

    def __call__(self, k):
        return self.pos < self.n - k if self.reverse else self.pos >= k


def _scan_rows(a, b, row, a_ref, b_ref, c_ref, reverse=False):
    S = a.shape[0]
    G = S // ROWS_PER_TILE
    if reverse:
        shift = lambda x, k: pltpu.roll(x, x.shape[0] - k, 0)
    else:
        shift = lambda x, k: pltpu.roll(x, k, 0)
    a, b = _steps(a, b, shift, _Inside(row % ROWS_PER_TILE, ROWS_PER_TILE, reverse))
    a_ref[...] = a
    b_ref[...] = b
    last = 0 if reverse else ROWS_PER_TILE - 1
    grow = lax.broadcasted_iota(jnp.int32, (G, a.shape[1]), 0)
    _, tot = _steps(a_ref[pl.ds(last, G, stride=ROWS_PER_TILE), :], b_ref[pl.ds(last, G, stride=ROWS_PER_TILE), :],
                    shift, _Inside(grow, G, reverse), products=False)
    enters = jnp.where(_Inside(grow, G, reverse)(1), shift(tot, 1), 0.0)
    for r in range(ROWS_PER_TILE):
        c_ref[pl.ds(r, G, stride=ROWS_PER_TILE), :] = enters
    return b + a * c_ref[...]


def _odd_c_fwd(h, cw, cb, wa, wx, ba, bx, lam, name, comm=None):
    S = h.shape[0]

    def body(xc_ref, cg_ref, cw_ref, cb_ref, wa_ref, wx_ref, ba_ref, bx_ref, lam_ref, mix_ref, hst_ref, sa_ref, sb_ref, sc_ref):
        row = lax.broadcasted_iota(jnp.int32, (S, 128), 0)
        xconv, r, i, sp, a, mult = _lru_gates(xc_ref[...], row, cw_ref, cb_ref, wa_ref, wx_ref, ba_ref, bx_ref, lam_ref)
        bb = _scan_rows(a, mult * (i * xconv), row, sa_ref, sb_ref, sc_ref)
        hst_ref[...] = bb
        cg = cg_ref[...]
        mix_ref[...] = (bb * (cg * _sig(cg))).astype(BF)

    col = lambda off: pl.BlockSpec((S, 128), lambda j: (0, off + j))
    vec = pl.BlockSpec((1, 128), lambda j: (0, j))
    mat = pl.BlockSpec((None, 128, 128), lambda j: (j, 0, 0))
    return _pcall(
        body, grid=(8,),
        in_specs=[col(0), col(8), pl.BlockSpec((4, 128), lambda j: (0, j)), vec, mat, mat, vec, vec, vec],
        out_specs=[pl.BlockSpec((None, S, 128), lambda j: (0, 0, j)), pl.BlockSpec((S, 128), lambda j: (0, j))],
        out_shape=[jax.ShapeDtypeStruct((2, S, W), BF), jax.ShapeDtypeStruct((S, W), F32)],
        scratch=[pltpu.VMEM((S, 128), F32)] * 3, name=name, comm=comm,
    )(h, h, cw, cb.reshape(1, W), wa, wx, ba.reshape(1, W), bx.reshape(1, W), lam.reshape(1, W))


def _pool_sums(x, g, row, shift):
    s2 = x + shift(x, 1, row)
    s4 = s2 + shift(s2, 2, row)
    s8 = s4 + shift(s4, 4, row)
    s16 = s8 + shift(s8, 8, row)
    return jnp.where(g == 0, s2, jnp.where(g == 1, s4, jnp.where(g == 2, s8, s16)))


def _odd_d_fwd(h, mix3, wp, dscale, name):
    S = h.shape[0]

    def body(xd_ref, dg_ref, wp_ref, ds_ref, mix_in, mix_ref):
        g = pl.program_id(0)
        row = lax.broadcasted_iota(jnp.int32, (S, 256), 0)
        xd = xd_ref[...]
        cnt = jnp.minimum(row + 1, jnp.left_shift(2, g)).astype(F32)
        pooled = _pool_sums(xd, g, row, _shift_down) / cnt - xd
        mixed = _dot(pooled.astype(BF), wp_ref[...])
        dg = dg_ref[...]
        mix_ref[...] = (mixed * ds_ref[...] * (dg * _sig(dg))).astype(BF)

    col = lambda off: pl.BlockSpec((S, 256), lambda g: (0, off + g))
    return pl.pallas_call(
        body, grid=(4,),
        in_specs=[col(8), col(12), pl.BlockSpec((None, 256, 256), lambda g: (g, 0, 0)),
                  pl.BlockSpec((1, 256), lambda g: (0, g)), ANY],
        out_specs=pl.BlockSpec((None, S, 256), lambda g: (1, 0, g)),
        out_shape=jax.ShapeDtypeStruct((2, S, W), BF), input_output_aliases={4: 0},
        name=name, compiler_params=_cp(),
    )(h, h, wp, dscale.reshape(1, W), mix3)


def _odd_c_bwd(h, hst, dmix3, cw, cb, wa, wx, wat, wxt, ba, bx, lam, name, comm=None):
    S = h.shape[0]

    def body(xc_ref, cg_ref, hst_ref, dc_ref, cw_ref, cb_ref, wa_ref, wx_ref, wat_ref, wxt_ref, ba_ref, bx_ref, lam_ref,
             dh_ref, dcw_ref, dcb_ref, dwa_ref, dwx_ref, dba_ref, dbx_ref, dlam_ref, sa_ref, sb_ref, sc_ref):
        row = lax.broadcasted_iota(jnp.int32, (S, 128), 0)
        xc = xc_ref[...]
        xconv, r, i, sp, a, mult = _lru_gates(xc, row, cw_ref, cb_ref, wa_ref, wx_ref, ba_ref, bx_ref, lam_ref)
        hst = hst_ref[...]
        cg = cg_ref[...]
        sg, dsg = _silu_grad(cg)
        dc = dc_ref[...]
        dh_ref[1] = (dc * hst * dsg).astype(BF)
        lam_t = _scan_rows(_shift_up(a, 1, row), dc * sg, row, sa_ref, sb_ref, sc_ref, reverse=True)
        da = lam_t * _shift_down(hst, 1, row)
        ix = i * xconv
        dmult = lam_t * ix
        di = lam_t * mult * xconv
        dxconv = lam_t * mult * i
        dlog_a = da * a - dmult * (a * a / mult)
        dr = dlog_a * (-LRU_C * sp)
        dsp = jnp.sum(dlog_a * (-LRU_C * r), axis=0, keepdims=True)
        dlam_ref[...] = dsp * (-_sig(-lam_ref[...]))
        dpa = dr * r * (1.0 - r)
        dpx = di * i * (1.0 - i)
        dpab = dpa.astype(BF)
        dpxb = dpx.astype(BF)
        xb = xconv.astype(BF)
        dxconv = dxconv + _dot(dpab, wat_ref[...]) + _dot(dpxb, wxt_ref[...])
        dwa_ref[...] = _dot_tn(xb, dpab)
        dwx_ref[...] = _dot_tn(xb, dpxb)
        dba_ref[...] = jnp.sum(dpa, axis=0, keepdims=True)
        dbx_ref[...] = jnp.sum(dpx, axis=0, keepdims=True)
        dh_ref[0] = (cw_ref[3:4, :] * dxconv + cw_ref[2:3, :] * _shift_up(dxconv, 1, row)
                     + cw_ref[1:2, :] * _shift_up(dxconv, 2, row) + cw_ref[0:1, :] * _shift_up(dxconv, 3, row)).astype(BF)
        for j in range(4):
            src = xc if j == 3 else _shift_down(xc, 3 - j, row)
            dcw_ref[j:j + 1, :] = jnp.sum(dxconv * src, axis=0, keepdims=True)
        dcb_ref[...] = jnp.sum(dxconv, axis=0, keepdims=True)

    col = lambda off: pl.BlockSpec((S, 128), lambda j: (0, off + j))
    vec = pl.BlockSpec((1, 128), lambda j: (0, j))
    mat = pl.BlockSpec((None, 128, 128), lambda j: (j, 0, 0))
    vshape = jax.ShapeDtypeStruct((1, W), F32)
    mshape = jax.ShapeDtypeStruct((8, 128, 128), F32)
    return _pcall(
        body, grid=(8,),
        in_specs=[col(0), col(8), col(0), pl.BlockSpec((None, S, 128), lambda j: (0, 0, j)),
                  pl.BlockSpec((4, 128), lambda j: (0, j)), vec, mat, mat, mat, mat, vec, vec, vec],
        out_specs=[pl.BlockSpec((2, S, 128), lambda j: (0, 0, j)), pl.BlockSpec((4, 128), lambda j: (0, j)), vec,
                   mat, mat, vec, vec, vec],
        out_shape=[jax.ShapeDtypeStruct((4, S, W), BF), jax.ShapeDtypeStruct((4, W), F32), vshape, mshape, mshape,
                   vshape, vshape, vshape],
        scratch=[pltpu.VMEM((S, 128), F32)] * 3, name=name, vmem=56, comm=comm,
    )(h, h, hst, dmix3, cw, cb.reshape(1, W), wa, wx, wat, wxt, ba.reshape(1, W), bx.reshape(1, W), lam.reshape(1, W))


def _odd_d_bwd(h, dmix3, dh4, wp, wpt, dscale, name):
    S = h.shape[0]

    def body(xd_ref, dg_ref, dd_ref, wp_ref, wpt_ref, ds_ref, dh_in, dh_ref, dwp_ref, dds_ref):
        g = pl.program_id(0)
        row = lax.broadcasted_iota(jnp.int32, (S, 256), 0)
        xd = xd_ref[...]
        cnt = jnp.minimum(row + 1, jnp.left_shift(2, g)).astype(F32)
        pooled = _pool_sums(xd, g, row, _shift_down) / cnt - xd
        pb = pooled.astype(BF)
        mixed = _dot(pb, wp_ref[...])
        dg = dg_ref[...]
        sg, dsg = _silu_grad(dg)
        dd = dd_ref[...]
        dmixed = dd * ds_ref[...] * sg
        dds_ref[...] = jnp.sum(dd * mixed * sg, axis=0, keepdims=True)
        dh_ref[1] = (dd * mixed * ds_ref[...] * dsg).astype(BF)
        dmb = dmixed.astype(BF)
        dpooled = _dot(dmb, wpt_ref[...])
        dwp_ref[...] = _dot_tn(pb, dmb)
        dh_ref[0] = (_pool_sums(dpooled / cnt, g, row, _shift_up) - dpooled).astype(BF)

    col = lambda off: pl.BlockSpec((S, 256), lambda g: (0, off + g))
    mat = pl.BlockSpec((None, 256, 256), lambda g: (g, 0, 0))
    vec = pl.BlockSpec((1, 256), lambda g: (0, g))
    return pl.pallas_call(
        body, grid=(4,),
        in_specs=[col(8), col(12), pl.BlockSpec((None, S, 256), lambda g: (1, 0, g)), mat, mat, vec, ANY],
        out_specs=[pl.BlockSpec((2, S, 256), lambda g: (1, 0, g)), mat, vec],
        out_shape=[jax.ShapeDtypeStruct((4, S, W), BF), jax.ShapeDtypeStruct((4, 256, 256), F32),
                   jax.ShapeDtypeStruct((1, W), F32)],
        input_output_aliases={6: 0}, name=name, compiler_params=_cp(56),
    )(h, h, dmix3, wp, wpt, dscale.reshape(1, W), dh4)


def _peer(d):
    x, y, c = lax.axis_index("x"), lax.axis_index("y"), lax.axis_index("c")
    px = 1 - x if d & 4 else x
    py = 1 - y if d & 2 else y
    pc = 1 - c if d & 1 else c
    return (px, py, pc), 4 * px + 2 * py + pc


class _GatherAll(_Comm):
    def __init__(self, xs):
        self.peers = EVERYONE
        self.inputs = [xs]
        self.out_shapes = [jax.ShapeDtypeStruct((N_DEV,) + xs.shape, xs.dtype)]
        self.sem_shapes = [pltpu.SemaphoreType.DMA((N_DEV - 1,)), pltpu.SemaphoreType.DMA((N_DEV - 1,)),
                           pltpu.SemaphoreType.DMA]

    def copies(self, ins, outs, sems):
        (x_ref,), (out_ref,), (send, recv, loc) = ins, outs, sems
        _, me = _peer(0)
        res = [pltpu.make_async_copy(x_ref, out_ref.at[me], loc)]
        for d in range(1, N_DEV):
            peer, _ = _peer(d)
            res.append(pltpu.make_async_remote_copy(src_ref=x_ref, dst_ref=out_ref.at[me], send_sem=send.at[d - 1],
                                                    recv_sem=recv.at[d - 1], device_id=peer, device_id_type=MESH))
        return res


class _ExchangeAll(_Comm):
    def __init__(self, g8):
        self.peers = EVERYONE
        self.inputs = [g8]
        self.out_shapes = [jax.ShapeDtypeStruct(g8.shape, g8.dtype)]
        self.sem_shapes = [pltpu.SemaphoreType.DMA((N_DEV - 1,)), pltpu.SemaphoreType.DMA((N_DEV - 1,)),
                           pltpu.SemaphoreType.DMA]

    def copies(self, ins, outs, sems):
        (g_ref,), (out_ref,), (send, recv, loc) = ins, outs, sems
        _, me = _peer(0)
        res = [pltpu.make_async_copy(g_ref.at[me], out_ref.at[0], loc)]
        for d in range(1, N_DEV):
            peer, pidx = _peer(d)
            res.append(pltpu.make_async_remote_copy(src_ref=g_ref.at[pidx], dst_ref=out_ref.at[d], send_sem=send.at[d - 1],
                                                    recv_sem=recv.at[d - 1], device_id=peer, device_id_type=MESH))
        return res


def _sum8(r8, tr, name):
    _, R, C = r8.shape
    tr = min(tr, R)
    assert R % tr == 0

    def body(r_ref, o_ref):
        acc = r_ref[0]
        for d in range(1, N_DEV):
            acc = acc + r_ref[d]
        o_ref[...] = acc

    return pl.pallas_call(
        body, grid=(R // tr,), in_specs=[pl.BlockSpec((N_DEV, tr, C), lambda i: (0, i, 0))],
        out_specs=pl.BlockSpec((tr, C), lambda i: (i, 0)), out_shape=jax.ShapeDtypeStruct((R, C), F32),
        name=name, compiler_params=_cp(),
    )(r8)


def _adamw_math(w, g, m, v):
    m2 = B1 * m + (1.0 - B1) * g
    v2 = B2 * v + (1.0 - B2) * (g * g)
    m_hat = m2 / (1.0 - B1 ** STEP)
    v_hat = v2 / (1.0 - B2 ** STEP)
    return -LR * (m_hat / (jnp.sqrt(v_hat) + ADAM_EPS) + WD * w), m2, v2


def _adamw_many(ws, gs, ms, vs, name):
    n = len(ws)

    def body(*refs):
        for i in range(n):
            d, m2, v2 = _adamw_math(refs[i][...], refs[n + i][...], refs[2 * n + i][...], refs[3 * n + i][...])
            refs[4 * n + i][...] = d
            refs[5 * n + i][...] = m2
            refs[6 * n + i][...] = v2

    vmem = pl.BlockSpec(memory_space=pltpu.VMEM)
    shapes = [jax.ShapeDtypeStruct(w.shape, F32) for w in ws]
    res = pl.pallas_call(body, in_specs=[vmem] * (4 * n), out_specs=[vmem] * (3 * n), out_shape=shapes * 3, name=name,
                         compiler_params=_cp())(*ws, *gs, *ms, *vs)
    return res[:n], res[n:2 * n], res[2 * n:]


def _adamw(w3, gs, m3, v3, tr, name, comm=None):
    _, R, C = w3.shape
    n = 2 if isinstance(gs[0], tuple) else 1

    def gradient(refs):
        if n == 1:
            return refs[0][...]
        s_ref, r_ref = refs
        return ((s_ref[...].astype(F32) + r_ref[0].astype(F32)) + r_ref[1].astype(F32)) + r_ref[2].astype(F32)

    def body(w_ref, *rest):
        g_refs, (m_ref, v_ref, d_ref, m2_ref, v2_ref, g_ref) = rest[:2 * n], rest[2 * n:]
        g = jnp.where(pl.program_id(0) == 0, gradient(g_refs[:n]), gradient(g_refs[n:]))
        d_ref[...], m2_ref[...], v2_ref[...] = _adamw_math(w_ref[...], g, m_ref[...], v_ref[...])
        g_ref[...] = g

    blk = pl.BlockSpec((None, tr, C), lambda j, i: (j, i, 0))

    def grad_specs(layer):
        at = lambda j, i: jnp.where(j == layer, i, 0)
        if n == 1:
            return [pl.BlockSpec((tr, C), lambda j, i: (at(j, i), 0))]
        return [pl.BlockSpec((None, tr, C), lambda j, i: (0, at(j, i), 0)), pl.BlockSpec((3, tr, C), lambda j, i: (0, at(j, i), 0))]

    flat = [a for g in gs for a in (g if n == 2 else (g,))]
    shp = jax.ShapeDtypeStruct((2, R, C), F32)
    return _pcall(body, grid=(2, R // tr), in_specs=[blk] + grad_specs(0) + grad_specs(1) + [blk, blk], out_specs=[blk] * 4,
                  out_shape=[shp] * 4, name=name, comm=comm)(w3, *flat, m3, v3)


def _rep_pack(a):
    n = a.size
    pad = (-n) % 1024
    f = a.reshape(-1)
    if pad:
        f = jnp.concatenate([f, jnp.zeros((pad,), a.dtype)])
    return f.reshape(N_DEV, -1, 128)


def _rep_unpack(p, shape):
    n = 1
    for s in shape:
        n *= s
    return p.reshape(-1)[:n].reshape(shape)


def _sh_pack(a, axis):
    shp = a.shape
    a = a.reshape(shp[:axis] + (N_DEV, shp[axis] // N_DEV) + shp[axis + 1:])
    return jnp.moveaxis(a, axis, 0).reshape(N_DEV, -1, 128)


def _sh_unpack(p, shape, axis):
    a = p.reshape((N_DEV,) + shape[:axis] + (shape[axis] // N_DEV,) + shape[axis + 1:])
    return jnp.moveaxis(a, 0, axis).reshape(shape)


def _pad_rows(a, mult=8):
    pad = (-a.shape[-2]) % mult
    if pad:
        a = jnp.concatenate([a, jnp.zeros(a.shape[:-2] + (pad, a.shape[-1]), a.dtype)], axis=-2)
    return a


REP = ["even_a_ln_g", "even_a_ln_b", "even_a_ws", "even_a_bs", "even_b_sinks", "even_ln_g", "even_ln_b",
       "odd_w_a", "odd_w_x"]
SH = [("odd_conv_w", (2, 4, W), 2), ("odd_conv_b", (2, W), 1), ("odd_b_a", (2, W), 1), ("odd_b_x", (2, W), 1),
      ("odd_lam", (2, W), 1), ("odd_w_pool", (2, 4, 256, 256), 2), ("odd_d_scale", (2, W), 1),
      ("odd_ln_g", (2, D), 1), ("odd_ln_b", (2, D), 1)]
BIG = ["even_w_in", "even_w_out", "odd_w_in", "odd_w_out"]
NAMES = ["even_w_in", "even_a_ln_g", "even_a_ln_b", "even_a_ws", "even_a_bs", "even_b_sinks", "even_w_out",
         "even_ln_g", "even_ln_b", "odd_w_in", "odd_conv_w", "odd_conv_b", "odd_w_a", "odd_b_a", "odd_w_x", "odd_b_x",
         "odd_lam", "odd_w_pool", "odd_d_scale", "odd_w_out", "odd_ln_g", "odd_ln_b"]


def _rope_table(positions):
    inv = ROPE_THETA ** (-jnp.arange(0, 16, 2, dtype=F32) / 16)
    f = jnp.arange(128) % 64
    ang = positions.astype(F32)[:, None] * inv[f % 8][None, :]
    cos, sin = jnp.cos(ang), jnp.sin(ang)
    return jnp.concatenate([jnp.where(f < 16, cos, 1.0), jnp.where(f < 8, -sin, 0.0),
                            jnp.where((f >= 8) & (f < 16), sin, 0.0)], axis=1)


def kernel(x, positions, even_w_in, even_a_ln_g, even_a_ln_b, even_a_ws, even_a_bs, even_b_sinks, even_w_out, even_ln_g, even_ln_b, odd_w_in, odd_conv_w, odd_conv_b, odd_w_a, odd_b_a, odd_w_x, odd_b_x, odd_lam, odd_w_pool, odd_d_scale, odd_w_out, odd_ln_g, odd_ln_b, loss_target, m_even_w_in, m_even_a_ln_g, m_even_a_ln_b, m_even_a_ws, m_even_a_bs, m_even_b_sinks, m_even_w_out, m_even_ln_g, m_even_ln_b, m_odd_w_in, m_odd_conv_w, m_odd_conv_b, m_odd_w_a, m_odd_b_a, m_odd_w_x, m_odd_b_x, m_odd_lam, m_odd_w_pool, m_odd_d_scale, m_odd_w_out, m_odd_ln_g, m_odd_ln_b, v_even_w_in, v_even_a_ln_g, v_even_a_ln_b, v_even_a_ws, v_even_a_bs, v_even_b_sinks, v_even_w_out, v_even_ln_g, v_even_ln_b, v_odd_w_in, v_odd_conv_w, v_odd_conv_b, v_odd_w_a, v_odd_b_a, v_odd_w_x, v_odd_b_x, v_odd_lam, v_odd_w_pool, v_odd_d_scale, v_odd_w_out, v_odd_ln_g, v_odd_ln_b):
    args = (even_w_in, even_a_ln_g, even_a_ln_b, even_a_ws, even_a_bs, even_b_sinks, even_w_out, even_ln_g, even_ln_b,
            odd_w_in, odd_conv_w, odd_conv_b, odd_w_a, odd_b_a, odd_w_x, odd_b_x, odd_lam, odd_w_pool, odd_d_scale,
            odd_w_out, odd_ln_g, odd_ln_b)
    margs = (m_even_w_in, m_even_a_ln_g, m_even_a_ln_b, m_even_a_ws, m_even_a_bs, m_even_b_sinks, m_even_w_out,
             m_even_ln_g, m_even_ln_b, m_odd_w_in, m_odd_conv_w, m_odd_conv_b, m_odd_w_a, m_odd_b_a, m_odd_w_x,
             m_odd_b_x, m_odd_lam, m_odd_w_pool, m_odd_d_scale, m_odd_w_out, m_odd_ln_g, m_odd_ln_b)
    vargs = (v_even_w_in, v_even_a_ln_g, v_even_a_ln_b, v_even_a_ws, v_even_a_bs, v_even_b_sinks, v_even_w_out,
             v_even_ln_g, v_even_ln_b, v_odd_w_in, v_odd_conv_w, v_odd_conv_b, v_odd_w_a, v_odd_b_a, v_odd_w_x,
             v_odd_b_x, v_odd_lam, v_odd_w_pool, v_odd_d_scale, v_odd_w_out, v_odd_ln_g, v_odd_ln_b)
    wts = dict(zip(NAMES, args))
    mom = dict(zip(NAMES, margs))
    var = dict(zip(NAMES, vargs))
    S = x.shape[1]
    x0 = x[0]
    rope = _rope_table(positions[0])

    kinds = ("even", "odd", "even", "odd")
    blk_in = [jnp.transpose(wts[kinds[l] + "_w_in"][l // 2]).astype(BF) for l in range(4)]
    blk_out = [wts[kinds[l] + "_w_out"][l // 2].astype(BF) for l in range(4)]
    sh_local = _pad_rows(jnp.concatenate([wts[nm].reshape(-1, 128) for nm, _, _ in SH], axis=0), 16)
    me = 4 * lax.axis_index("x") + 2 * lax.axis_index("y") + lax.axis_index("c")
    own_slot = lambda blk: lax.dynamic_update_slice(lax.empty((N_DEV,) + blk.shape, blk.dtype), blk[None], (me, 0, 0))
    reg = {"blk_small": sh_local, "w_small": own_slot(sh_local)}
    sched = _Sched(reg)
    for l in range(4):
        reg[f"blk_in{l}"], reg[f"blk_out{l}"] = blk_in[l], blk_out[l]
        reg[f"w_in{l}"], reg[f"w_out{l}"] = own_slot(blk_in[l]), own_slot(blk_out[l])
    sched.add(_rows("blk_in0", "w_in0", "ag1", blk_in[0].shape[0], ROW_CHUNK[blk_in[0].shape[0]]))
    sched.add(_rows("blk_small", "w_small", "ag1", sh_local.shape[0], sh_local.shape[0]))
    for l in range(4):
        sched.add(_rows(f"blk_out{l}", f"w_out{l}", "ag1", D // N_DEV, ROW_CHUNK[D // N_DEV]))
        if l < 3:
            r = blk_in[l + 1].shape[0]
            sched.add(_rows(f"blk_in{l + 1}", f"w_in{l + 1}", "ag1", r, ROW_CHUNK[r]))

    def gathered(dst, blk):
        sched.flush(dst, FLUSH_EXTRA_US)
        return reg.pop(dst)

    (xb0,) = sched.run(_cast_rows, FIRST_CARRY_US, x0, "cast_x")
    wt_in0 = gathered("w_in0", blk_in[0]).reshape(-1, D)
    full = {nm: wts[nm] for nm in REP}

    def gather_small():
        sh_all = gathered("w_small", sh_local)
        off = 0
        for nm, shape, axis in SH:
            r = wts[nm].size // 128
            full[nm] = _sh_unpack(sh_all[:, off:off + r, :], shape, axis)
            off += r

    saved = []
    wt_in, w_out = [wt_in0, None, None, None], [None] * 4
    xf, xb = x0, xb0
    fwd = lambda name: FWD_OVERBOOK * CARRY_US[name]
    for layer in range(4):
        j = layer // 2
        kind = kinds[layer]
        if wt_in[layer] is None:
            wt_in[layer] = gathered(f"w_in{layer}", blk_in[layer]).reshape(-1, D)
        h = sched.run(_mm_nt, fwd("mm_h_" + kind), xb, wt_in[layer], 1024, 768 if kind == "even" else 512, "mm_h_" + kind)
        if kind == "even":
            bsb = jnp.broadcast_to(full["even_a_bs"][j][:, :, None], (8, 128, 128))
            mix3, o, l = sched.run(_even_fwd, fwd("even_fwd"), h, rope, full["even_a_ln_g"][j], full["even_a_ln_b"][j],
                                   full["even_a_ws"][j], bsb, full["even_b_sinks"][j], "even_fwd")
            extra = (o, l, bsb)
        else:
            if "odd_lam" not in full:
                gather_small()
            wa, wx = full["odd_w_a"][j].astype(BF), full["odd_w_x"][j].astype(BF)
            wp = full["odd_w_pool"][j].astype(BF)
            mix3, hst = sched.run(_odd_c_fwd, fwd("odd_c_fwd"), h, full["odd_conv_w"][j], full["odd_conv_b"][j], wa, wx,
                                  full["odd_b_a"][j], full["odd_b_x"][j], full["odd_lam"][j], "odd_c_fwd")
            mix3 = _odd_d_fwd(h, mix3, wp, full["odd_d_scale"][j], "odd_d_fwd")
            extra = (hst, wa, wx, wp)
        w_out[layer] = gathered(f"w_out{layer}", blk_out[layer]).reshape(D, D)
        z, xn, xnb = sched.run(_mm_out_ln, fwd("mm_out_ln"), mix3, w_out[layer], xf, full[kind + "_ln_g"][j],
                               full[kind + "_ln_b"][j], "mm_out_ln")
        saved.append((xb, h, mix3, z, extra))
        xf, xb = xn, xnb

    dxn = xf

    gsum = {nm: [None, None] for nm in NAMES}

    chip_sums = {}
    sched.overhang = 0.15

    waiting = []

    def chip_sum(g, tag, key):
        r = g.shape[0] // N_DEV
        reg["g_" + key] = g.reshape(N_DEV, r, D)
        sched.add(_rows("g_" + key, "d_" + key, "rsd", r, r), first=True)
        waiting.append((key, tag))

    def add_arrived():
        for key, tag in list(waiting):
            if "d_" + key in reg and not sched.pending("d_" + key):
                waiting.remove((key, tag))
                g8 = reg.pop("g_" + key)
                chip_sums[key] = reg["s_" + key] = _add_pairs(g8, reg.pop("d_" + key), "rs_add_" + tag)
                sched.add(_rows("s_" + key, "r_" + key, "rs", g8.shape[1], ROW_CHUNK[g8.shape[1]] // 2))

    sched.after_landing = add_arrived

    def reduced(key):
        sched.flush("d_" + key, FLUSH_EXTRA_US)
        sched.flush("r_" + key, FLUSH_EXTRA_US)
        return chip_sums[key], reg.pop("r_" + key)

    for layer in (3, 2, 1, 0):
        j = layer // 2
        xb, h, mix3, z, extra = saved[layer]
        kind = kinds[layer]
        if layer == 3:
            dz, dzb, dg, dbeta, part = sched.run(_ln_bwd, CARRY_US["ln_bwd"], dxn, z, full[kind + "_ln_g"][j], "loss_ln_bwd",
                                                 target=loss_target[0])
        else:
            dz, dzb, dg, dbeta = sched.run(_ln_bwd, CARRY_US["ln_bwd"], dxn, z, full[kind + "_ln_g"][j], "ln_bwd")
        gsum[kind + "_ln_g"][j] = dg.reshape(D)
        gsum[kind + "_ln_b"][j] = dbeta.reshape(D)
        if layer == 0:
            chip_sum(sched.run(_mm_tn, CARRY_US["mm_dw_out"], mix3, dzb, 512, "mm_dw_out"), "w_out", f"out{layer}")
            dmix3 = sched.run(_mm_nt, CARRY_US["mm_dmix"], dzb, w_out[layer], 1024, 1024, "mm_dmix", out3=True)
        else:
            gw, dmix3 = sched.run(_mm_dz, CARRY_US["mm_dw_out"] + CARRY_US["mm_dmix"], dzb, w_out[layer], mix3, "mm_dz")
            chip_sum(gw, "w_out", f"out{layer}")
        if kind == "even":
            o, l, bsb = extra
            ws = full["even_a_ws"][j]
            dh, dws, dbs, dlng, dlnb, dsink = sched.run(
                _even_bwd, CARRY_US["even_bwd"], h, dmix3, o, l, rope, full["even_a_ln_g"][j], full["even_a_ln_b"][j],
                ws, jnp.swapaxes(ws, 1, 2), bsb, full["even_b_sinks"][j], "even_bwd")
            gsum["even_a_ws"][j] = dws
            gsum["even_a_bs"][j] = jnp.transpose(dbs[:, :8])
            gsum["even_a_ln_g"][j] = dlng.reshape(W)
            gsum["even_a_ln_b"][j] = dlnb.reshape(W)
            gsum["even_b_sinks"][j] = dsink[0, :16]
            if layer == 0:
                rep_rows = [_rep_pack(jnp.stack(gsum[nm]).reshape(wts[nm].shape)) for nm in REP]
                sh_rows = [_sh_pack(jnp.stack(gsum[nm]).reshape(shape), axis) for nm, shape, axis in SH]
                packed = _pad_rows(jnp.concatenate(rep_rows + sh_rows, axis=1))
                gw, (small8, parts) = _mm_tn(dh, xb, 384, "mm_dw_in_even", comm=_Join([_ExchangeAll(packed), _GatherAll(part)]))
                loss = jnp.sum(parts[:, 0, 0]) * (0.5 / D)
            else:
                gw = sched.run(_mm_tn, CARRY_US["mm_dw_in_even"], dh, xb, 384, "mm_dw_in_even")
            chip_sum(gw, "w_in_even", f"in{layer}")
            if layer == 0:
                n_rep = sum(p.shape[1] for p in rep_rows)
                red = _sum8(small8, 1 << 20, "sum_small")
                (rep_all,) = sched.flush("d_in0", FLUSH_EXTRA_US, beside=_GatherAll(_pad_rows(red[:n_rep])))
                sched.overhang = 0.6
            dxn = sched.run(_mm_nn_res, CARRY_US["mm_dx_even"], dh, wt_in[layer], dz, 512, 1024, "mm_dx_even")
        else:
            hst, wa, wx, wp = extra
            dh4, dcw, dcb, dwa, dwx, dba, dbx, dlam = sched.run(
                _odd_c_bwd, CARRY_US["odd_c_bwd"], h, hst, dmix3, full["odd_conv_w"][j], full["odd_conv_b"][j], wa, wx,
                jnp.swapaxes(wa, 1, 2), jnp.swapaxes(wx, 1, 2), full["odd_b_a"][j], full["odd_b_x"][j], full["odd_lam"][j],
                "odd_c_bwd")
            dh4, dwp, dds = _odd_d_bwd(h, dmix3, dh4, wp, jnp.swapaxes(wp, 1, 2), full["odd_d_scale"][j], "odd_d_bwd")
            gsum["odd_conv_w"][j], gsum["odd_conv_b"][j] = dcw, dcb.reshape(W)
            gsum["odd_w_a"][j], gsum["odd_w_x"][j] = dwa, dwx
            gsum["odd_b_a"][j], gsum["odd_b_x"][j], gsum["odd_lam"][j] = dba.reshape(W), dbx.reshape(W), dlam.reshape(W)
            gsum["odd_w_pool"][j], gsum["odd_d_scale"][j] = dwp, dds.reshape(W)
            chip_sum(sched.run(_mm_tn, CARRY_US["mm_dw_in_odd"], dh4, xb, 1024, "mm_dw_in_odd"), "w_in_odd", f"in{layer}")
            dxn = sched.run(_mm_nn_res, CARRY_US["mm_dx_odd"], dh4, wt_in[layer], dz, 512, 1024, "mm_dx_odd")
    grad_x = dxn[None]

    out_g, out_d, out_m, out_v = {}, {}, {}, {}
    for nm, kind, what, layers in (("odd_w_out", "odd", "out", (1, 3)), ("even_w_out", "even", "out", (0, 2)),
                                   ("odd_w_in", "odd", "in", (1, 3)), ("even_w_in", "even", "in", (0, 2))):
        gl = [reduced(f"{what}{l}") for l in layers]
        if nm == "even_w_in":
            view = lambda a: jnp.transpose(a, (0, 2, 1))
            res, _ = _adamw(view(wts[nm]), gl, view(mom[nm]), view(var[nm]), 112, f"adamw_{nm}")
            res = [view(a) for a in res]
        elif what == "in":
            gs = [jnp.transpose(_rs_final(s4, r3, "rs_final_w_in_odd")) for s4, r3 in gl]
            res, _ = _adamw(wts[nm], gs, mom[nm], var[nm], 512, f"adamw_{nm}")
        else:
            res = sched.run(_adamw, CARRY_US["adamw_" + nm], wts[nm], gl, mom[nm], var[nm], 128, f"adamw_{nm}")
        out_d[nm], out_m[nm], out_v[nm], out_g[nm] = res

    g_small = {}
    off = 0
    for nm, p in zip(REP, rep_rows):
        r = p.shape[1]
        g_small[nm] = _rep_unpack(rep_all[:, off:off + r, :], wts[nm].shape)
        off += r
    off = n_rep
    for (nm, shape, axis), p in zip(SH, sh_rows):
        r = p.shape[1]
        g_small[nm] = red[off:off + r].reshape(wts[nm].shape)
        off += r

    def rows(a):
        f = a.reshape(-1)
        pad = (-f.shape[0]) % 128
        if pad:
            f = jnp.concatenate([f, jnp.zeros((pad,), a.dtype)])
        return f.reshape(-1, 128)

    small = REP + [nm for nm, _, _ in SH]
    each = lambda src: [rows(src[nm]) for nm in small]
    d2, m2, v2 = _adamw_many(each(wts), each(g_small), each(mom), each(var), "adamw_small")
    for i, nm in enumerate(small):
        n, shp = wts[nm].size, wts[nm].shape
        take = lambda a: a.reshape(-1)[:n].reshape(shp)
        out_g[nm], out_d[nm], out_m[nm], out_v[nm] = g_small[nm], take(d2[i]), take(m2[i]), take(v2[i])

    return (loss, grad_x, *[out_g[nm] for nm in NAMES], *[out_d[nm] for nm in NAMES],
            *[out_m[nm] for nm in NAMES], *[out_v[nm] for nm in NAMES])
```

```python
import functools

import jax
import jax.numpy as jnp
from jax import lax
from jax.experimental import pallas as pl
from jax.experimental.pallas import tpu as pltpu

F32 = jnp.float32
BF = jnp.bfloat16
MESH = pl.DeviceIdType.MESH
ANY = pl.BlockSpec(memory_space=pl.ANY)

N_DEV = 8
D = 2048
W = 1024
EVEN_IN = 5376
ODD_IN = 4096
CHUNK = 128
ALPHA = (2 * 4) ** 0.25
LN_EPS = 1e-5
ROPE_THETA = 500000.0
LRU_C = 8.0
LR, B1, B2, ADAM_EPS, WD, STEP = 0.001, 0.9, 0.999, 1e-08, 0.01, 10
NEG = -1e30
HEAD_COLS = 4


def _cp(vmem_mb=48, collective_id=None):
    return pltpu.CompilerParams(vmem_limit_bytes=vmem_mb * 1024 * 1024, collective_id=collective_id)


def _sig(x):
    return jax.nn.sigmoid(x)


def _silu_grad(x):
    s = _sig(x)
    return x * s, s * (1.0 + x * (1.0 - s))


def _dot(a, b):
    return jnp.dot(a, b, preferred_element_type=F32)


def _dot_nt(a, b):
    return lax.dot_general(a, b, (((1,), (1,)), ((), ())), preferred_element_type=F32)


def _dot_tn(a, b):
    return lax.dot_general(a, b, (((0,), (0,)), ((), ())), preferred_element_type=F32)


def _coords():
    return lax.axis_index("x"), lax.axis_index("y"), lax.axis_index("c")


def _chip(j):
    x, y, _ = _coords()
    return (1 - x if j & 2 else x), (1 - y if j & 1 else y)


X_NB, Y_NB, DIAG, SIB = 4, 2, 6, 1
EVERYONE = frozenset(range(1, N_DEV))
BARRIER_IDS = {}


class _Comm:
    def collective_id(self):
        return BARRIER_IDS.setdefault(frozenset(self.peers), len(BARRIER_IDS))

    def start(self, ins, outs, sems):
        barrier = pltpu.get_barrier_semaphore()
        for d in sorted(self.peers):
            pl.semaphore_signal(barrier, inc=1, device_id=_peer(d)[0], device_id_type=MESH)
        pl.semaphore_wait(barrier, len(self.peers))
        for cp in self.copies(ins, outs, sems):
            cp.start()

    def wait(self, ins, outs, sems):
        for cp in self.copies(ins, outs, sems):
            cp.wait()


class _Join(_Comm):
    def __init__(self, parts):
        self.parts = list(parts)
        self.peers = frozenset().union(*[p.peers for p in self.parts])
        self.inputs = [a for p in self.parts for a in p.inputs]
        self.out_shapes = [s for p in self.parts for s in p.out_shapes]
        self.sem_shapes = [s for p in self.parts for s in p.sem_shapes]
        self.aliases = {}
        i0 = o0 = 0
        for p in self.parts:
            for i, o in getattr(p, "aliases", {}).items():
                self.aliases[i0 + i] = o0 + o
            i0, o0 = i0 + len(p.inputs), o0 + len(p.out_shapes)

    def copies(self, ins, outs, sems):
        res = []
        i0 = o0 = s0 = 0
        for p in self.parts:
            ni, no, ns = len(p.inputs), len(p.out_shapes), len(p.sem_shapes)
            res += p.copies(ins[i0:i0 + ni], outs[o0:o0 + no], sems[s0:s0 + ns])
            i0, o0, s0 = i0 + ni, o0 + no, s0 + ns
        return res


ROWS_US = {"ag1": 0.104, "ag2": 0.052, "agd": 0.027, "rsd": 0.027, "rs": 0.205}
N_COPIES = {"ag1": 2, "ag2": 2, "agd": 4, "rsd": 4, "rs": 3}
TASK_PEERS = {"ag1": {X_NB, Y_NB}, "ag2": {X_NB, Y_NB}, "agd": {SIB}, "rsd": {SIB}, "rs": {X_NB, Y_NB, DIAG}}
ROW_CHUNK = {672: 224, 512: 128, 256: 128}
CARRY_US = {"mm_h_even": 58, "mm_h_odd": 47, "even_fwd": 42, "odd_c_fwd": 37, "mm_out_ln": 33, "ln_bwd": 23, "mm_dmix": 26,
            "mm_dw_out": 25, "even_bwd": 90, "odd_c_bwd": 58, "mm_dw_in_even": 58, "mm_dw_in_odd": 44, "mm_dx_even": 66,
            "mm_dx_odd": 55, "adamw_even_w_out": 11, "adamw_odd_w_out": 11}
FWD_OVERBOOK = 1.15
FIRST_CARRY_US = 60.0
FLUSH_EXTRA_US = 60.0


def _cost_us(task, reg):
    kind, src, _, lo, hi = task
    return ROWS_US[kind] * (hi - lo) * reg[src].shape[-1] * reg[src].dtype.itemsize / 4096.0


class _Copies(_Comm):
    def __init__(self, tasks, reg):
        self.tasks = list(tasks)
        self.out_names, self.in_names = [], []
        for kind, src, dst, lo, hi in self.tasks:
            if dst not in self.out_names:
                self.out_names.append(dst)
        for kind, src, dst, lo, hi in self.tasks:
            if src not in self.out_names and src not in self.in_names:
                self.in_names.append(src)
        self.out_shapes, self.aliases = [], {}
        for o, dst in enumerate(self.out_names):
            if dst in reg:
                self.aliases[len(self.in_names)] = o
                self.in_names.append(dst)
                self.out_shapes.append(jax.ShapeDtypeStruct(reg[dst].shape, reg[dst].dtype))
            else:
                kind, src = next((t[0], t[1]) for t in self.tasks if t[2] == dst)
                shape = ({"rsd": 4, "rs": 3}[kind],) + reg[src].shape[1:]
                self.out_shapes.append(jax.ShapeDtypeStruct(shape, reg[src].dtype))
        self.inputs = [reg[nm] for nm in self.in_names]
        n = sum(N_COPIES[t[0]] for t in self.tasks)
        self.sem_shapes = [pltpu.SemaphoreType.DMA((n,)), pltpu.SemaphoreType.DMA((n,))]
        self.peers = frozenset().union(*[TASK_PEERS[t[0]] for t in self.tasks])

    def copies(self, ins, outs, sems):
        send, recv = sems
        x, y, c = _coords()
        me = 4 * x + 2 * y + c
        xn, yn = (1 - x, y, c), (x, 1 - y, c)
        at_xn, at_yn = 4 * (1 - x) + 2 * y + c, 4 * x + 2 * (1 - y) + c
        ref = dict(zip(self.in_names, ins))
        ref.update(zip(self.out_names, outs))
        res = []

        def copy(src, dst, to):
            i = len(res)
            res.append(pltpu.make_async_remote_copy(src_ref=src, dst_ref=dst, send_sem=send.at[i], recv_sem=recv.at[i],
                                                    device_id=to, device_id_type=MESH))

        for kind, src, dst, lo, hi in self.tasks:
            n = hi - lo
            if kind == "ag1":
                for to in (xn, yn):
                    copy(ref[src].at[pl.ds(lo, n)], ref[dst].at[me, pl.ds(lo, n)], to)
            elif kind == "ag2":
                h = n // 2
                first, second = ref[dst].at[at_xn, pl.ds(lo, h)], ref[dst].at[at_yn, pl.ds(lo + h, n - h)]
                copy(first, first, yn)
                copy(second, second, xn)
            elif kind == "agd":
                for j in range(4):
                    px, py = _chip(j)
                    rows = ref[dst].at[4 * px + 2 * py + c, pl.ds(lo, n)]
                    copy(rows, rows, (x, y, 1 - c))
            elif kind == "rsd":
                for j in range(4):
                    px, py = _chip(j)
                    copy(ref[src].at[4 * px + 2 * py + 1 - c, pl.ds(lo, n)], ref[dst].at[j, pl.ds(lo, n)], (x, y, 1 - c))
            else:
                for j in (1, 2, 3):
                    px, py = _chip(j)
                    copy(ref[src].at[j, pl.ds(lo, n)], ref[dst].at[j - 1, pl.ds(lo, n)], (px, py, c))
        return res


class _Sched:
    def __init__(self, reg):
        self.reg, self.queue, self.later = reg, [], []
        self.overhang = 0.5
        self.after_landing = None

    def add(self, tasks, first=False):
        self.queue = list(tasks) + self.queue if first else self.queue + list(tasks)

    def pending(self, dst):
        return any(t[2] == dst for t in self.queue + self.later)

    def take(self, budget_us, must=None, overhang=0.5):
        self.queue, self.later = self.later + self.queue, []
        picked, us = [], 0.0
        rest = []
        for t in self.queue:
            cost = _cost_us(t, self.reg)
            if (must is not None and t[2] == must) or us + (1.0 - overhang) * cost <= budget_us:
                picked.append(t)
                us += cost
                if t[0] in ("ag1", "ag2"):
                    self.later.append(({"ag1": "ag2", "ag2": "agd"}[t[0]], t[2], t[2], t[3], t[4]))
            else:
                rest.append(t)
        self.queue = rest
        return _Copies(picked, self.reg) if picked else None

    def landed(self, comm, got):
        if comm is not None:
            for nm, a in zip(comm.out_names, got):
                self.reg[nm] = a
        if self.after_landing is not None:
            self.after_landing()

    def run(self, builder, budget_us, *args, **kw):
        comm = self.take(budget_us, overhang=self.overhang)
        res, got = builder(*args, comm=comm, **kw)
        self.landed(comm, got)
        return res

    def flush(self, dst, budget_us=0.0, beside=None):
        res = []
        while self.pending(dst):
            comm = self.take(budget_us, must=dst)
            got = _comm_only(comm if beside is None else _Join([comm, beside]), "flush_" + dst)
            res, beside = got[len(comm.out_shapes):], None
            self.landed(comm, got[:len(comm.out_shapes)])
        return res


def _rows(name_src, name_dst, kind, n_rows, chunk):
    return [(kind, name_src, name_dst, lo, min(lo + chunk, n_rows)) for lo in range(0, n_rows, chunk)]


def _pcall(body, *, grid, in_specs, out_specs, out_shape, name, scratch=(), vmem=48, comm=None):
    in_specs, out_specs, out_shape, scratch = list(in_specs), list(out_specs), list(out_shape), list(scratch)
    if comm is None:
        call = pl.pallas_call(body, grid=grid, in_specs=in_specs, out_specs=out_specs, out_shape=out_shape,
                              scratch_shapes=scratch, name=name, compiler_params=_cp(vmem))
        return lambda *args: (call(*args), [])
    n_in, n_out, n_scr = len(in_specs), len(out_specs), len(scratch)
    c_in, c_out = len(comm.inputs), len(comm.out_shapes)
    aliases = {n_in + i: n_out + o for i, o in getattr(comm, "aliases", {}).items()}

    def wrapped(*refs):
        ins, cins = refs[:n_in], refs[n_in:n_in + c_in]
        o0 = n_in + c_in
        outs, couts = refs[o0:o0 + n_out], refs[o0 + n_out:o0 + n_out + c_out]
        s0 = o0 + n_out + c_out
        scr, sems = refs[s0:s0 + n_scr], refs[s0 + n_scr:]
        ids = [pl.program_id(a) for a in range(len(grid))]
        first = functools.reduce(jnp.logical_and, [i == 0 for i in ids])
        last = functools.reduce(jnp.logical_and, [i == g - 1 for i, g in zip(ids, grid)])

        @pl.when(first)
        def _():
            comm.start(cins, couts, sems)

        body(*ins, *outs, *scr)

        @pl.when(last)
        def _():
            comm.wait(cins, couts, sems)

    call = pl.pallas_call(wrapped, grid=grid, in_specs=in_specs + [ANY] * c_in, out_specs=out_specs + [ANY] * c_out,
                          out_shape=out_shape + list(comm.out_shapes), scratch_shapes=scratch + list(comm.sem_shapes),
                          input_output_aliases=aliases, name=name, compiler_params=_cp(vmem, comm.collective_id()))

    def run(*args):
        res = call(*args, *comm.inputs)
        return res[:n_out], res[n_out:]

    return run


def _comm_only(comm, name):
    c_in, c_out = len(comm.inputs), len(comm.out_shapes)

    def body(*refs):
        cins, couts, sems = refs[:c_in], refs[c_in:c_in + c_out], refs[c_in + c_out:]
        comm.start(cins, couts, sems)
        comm.wait(cins, couts, sems)

    return pl.pallas_call(body, in_specs=[ANY] * c_in, out_specs=[ANY] * c_out, out_shape=list(comm.out_shapes),
                          scratch_shapes=list(comm.sem_shapes), input_output_aliases=dict(getattr(comm, "aliases", {})),
                          name=name, compiler_params=pltpu.CompilerParams(collective_id=comm.collective_id()))(*comm.inputs)


def _chip_blocks():
    _, _, c = _coords()
    return jnp.stack([4 * px + 2 * py + c for px, py in map(_chip, range(4))]).astype(jnp.int32)


def _add_pairs(g8, b4, name):
    _, R, C = b4.shape

    def body(idx_ref, a_ref, b_ref, o_ref):
        o_ref[...] = (a_ref[...].astype(F32) + b_ref[...].astype(F32)).astype(BF)

    blk = pl.BlockSpec((None, R, C), lambda j, idx: (j, 0, 0))
    grid_spec = pltpu.PrefetchScalarGridSpec(
        num_scalar_prefetch=1, grid=(4,),
        in_specs=[pl.BlockSpec((None, R, C), lambda j, idx: (idx[j], 0, 0)), blk], out_specs=blk)
    return pl.pallas_call(body, grid_spec=grid_spec, out_shape=jax.ShapeDtypeStruct(b4.shape, BF), name=name,
                          compiler_params=_cp())(_chip_blocks(), g8, b4)


def _rs_final(s4, r3, name):
    _, R, C = s4.shape
    tr = R // 2

    def body(s_ref, r_ref, o_ref):
        o_ref[...] = ((s_ref[...].astype(F32) + r_ref[0].astype(F32)) + r_ref[1].astype(F32)) + r_ref[2].astype(F32)

    return pl.pallas_call(
        body, grid=(2,),
        in_specs=[pl.BlockSpec((None, tr, C), lambda i: (0, i, 0)), pl.BlockSpec((3, tr, C), lambda i: (0, i, 0))],
        out_specs=pl.BlockSpec((tr, C), lambda i: (i, 0)), out_shape=jax.ShapeDtypeStruct((R, C), F32),
        name=name, compiler_params=_cp())(s4, r3)


def _mm_nt(a, w, tm, tn, name, out3=False, comm=None):
    M, K = a.shape
    N = w.shape[0]
    tm = min(tm, M)

    def body(a_ref, w_ref, o_ref):
        o_ref[...] = _dot_nt(a_ref[...], w_ref[...])

    if out3:
        per = W // tn
        out_shape = jax.ShapeDtypeStruct((N // W, M, W), F32)
        out_spec = pl.BlockSpec((None, tm, tn), lambda i, j: (j // per, i, j % per))
    else:
        out_shape = jax.ShapeDtypeStruct((M, N), F32)
        out_spec = pl.BlockSpec((tm, tn), lambda i, j: (i, j))
    (res,), extra = _pcall(
        body, grid=(M // tm, N // tn),
        in_specs=[pl.BlockSpec((tm, K), lambda i, j: (i, 0)), pl.BlockSpec((tn, K), lambda i, j: (j, 0))],
        out_specs=[out_spec], out_shape=[out_shape], name=name, comm=comm)(a, w)
    return res, extra


def _mm_tn(a, b, tm, name, comm=None):
    K, N = b.shape
    if a.ndim == 3:
        M = a.shape[0] * W
        per = W // tm
        a_spec = pl.BlockSpec((None, K, tm), lambda i: (i // per, 0, i % per))
    else:
        M = a.shape[1]
        a_spec = pl.BlockSpec((K, tm), lambda i: (0, i))

    def body(a_ref, b_ref, o_ref):
        o_ref[...] = _dot_tn(a_ref[...], b_ref[...]).astype(BF)

    (out,), extra = _pcall(
        body, grid=(M // tm,),
        in_specs=[a_spec, pl.BlockSpec((K, N), lambda i: (0, 0))],
        out_specs=[pl.BlockSpec((tm, N), lambda i: (i, 0))],
        out_shape=[jax.ShapeDtypeStruct((M, N), BF)], name=name, vmem=56, comm=comm)(a, b)
    return out, extra


def _cast_rows(x, name, comm=None):
    S = x.shape[0]
    tm = min(512, S)

    def body(x_ref, o_ref):
        o_ref[...] = x_ref[...].astype(BF)

    row = pl.BlockSpec((tm, D), lambda i: (i, 0))
    return _pcall(body, grid=(S // tm,), in_specs=[row], out_specs=[row], out_shape=[jax.ShapeDtypeStruct((S, D), BF)],
                  name=name, comm=comm)(x)


def _mm_nn_res(a, w, res, tm, tn, name, comm=None):
    K, N = w.shape
    if a.ndim == 3:
        P, M = a.shape[0], a.shape[1]
        tm = min(tm, M)
        a_spec = pl.BlockSpec((P, tm, W), lambda j, i: (0, i, 0))
    else:
        P, M = 0, a.shape[0]
        tm = min(tm, M)
        a_spec = pl.BlockSpec((tm, K), lambda j, i: (i, 0))

    def body(a_ref, w_ref, r_ref, o_ref):
        if P:
            d = _dot(a_ref[0], w_ref[0:W, :])
            for p in range(1, P):
                d = d + _dot(a_ref[p], w_ref[p * W:(p + 1) * W, :])
        else:
            d = _dot(a_ref[...], w_ref[...])
        o_ref[...] = ALPHA * r_ref[...] + d

    (out,), extra = _pcall(
        body, grid=(N // tn, M // tm),
        in_specs=[a_spec, pl.BlockSpec((K, tn), lambda j, i: (0, j)), pl.BlockSpec((tm, tn), lambda j, i: (i, j))],
        out_specs=[pl.BlockSpec((tm, tn), lambda j, i: (i, j))],
        out_shape=[jax.ShapeDtypeStruct((M, N), F32)], name=name, comm=comm)(a, w, res)
    return out, extra


def _mm_out_ln(mix3, w_out, x, g, b, name, comm=None):
    S = x.shape[0]
    tm = min(512, S)

    def body(m_ref, w_ref, x_ref, g_ref, b_ref, z_ref, xn_ref, xb_ref):
        acc = _dot(m_ref[0], w_ref[0:W, :]) + _dot(m_ref[1], w_ref[W:2 * W, :])
        z = ALPHA * x_ref[...] + acc
        mu = jnp.mean(z, axis=1, keepdims=True)
        zc = z - mu
        var = jnp.mean(zc * zc, axis=1, keepdims=True)
        xn = zc * lax.rsqrt(var + LN_EPS) * g_ref[...] + b_ref[...]
        z_ref[...] = z
        xn_ref[...] = xn
        xb_ref[...] = xn.astype(BF)

    row = pl.BlockSpec((tm, D), lambda i: (i, 0))
    vec = pl.BlockSpec((1, D), lambda i: (0, 0))
    return _pcall(
        body, grid=(S // tm,),
        in_specs=[pl.BlockSpec((2, tm, W), lambda i: (0, i, 0)),
                  pl.BlockSpec((D, D), lambda i: (0, 0), pipeline_mode=pl.Buffered(1)), row, vec, vec],
        out_specs=[row, row, row],
        out_shape=[jax.ShapeDtypeStruct((S, D), F32), jax.ShapeDtypeStruct((S, D), F32), jax.ShapeDtypeStruct((S, D), BF)],
        name=name, comm=comm)(mix3, w_out, x, g.reshape(1, D), b.reshape(1, D))


def _ln_bwd(dxn, z, g, name, comm=None, target=None):
    S = z.shape[0]
    tm = min(256, S)
    head = target is not None

    def body(*refs):
        if head:
            d_ref, t_ref, z_ref, g_ref, dz_ref, dzb_ref, dg_ref, db_ref, p_ref = refs
        else:
            d_ref, z_ref, g_ref, dz_ref, dzb_ref, dg_ref, db_ref = refs
        i = pl.program_id(0)
        zz = z_ref[...]
        mu = jnp.mean(zz, axis=1, keepdims=True)
        zc = zz - mu
        var = jnp.mean(zc * zc, axis=1, keepdims=True)
        rstd = lax.rsqrt(var + LN_EPS)
        xhat = zc * rstd
        dy = d_ref[...]
        if head:
            e = dy - t_ref[...]
            dy = e * (1.0 / D)

            @pl.when(i == 0)
            def _():
                p_ref[...] = jnp.zeros_like(p_ref)

            p_ref[...] += jnp.sum(jnp.sum(e * e, axis=1, keepdims=True), axis=0, keepdims=True)
        dyg = dy * g_ref[...]
        m1 = jnp.mean(dyg, axis=1, keepdims=True)
        m2 = jnp.mean(dyg * xhat, axis=1, keepdims=True)
        dz = rstd * (dyg - m1 - xhat * m2)
        dz_ref[...] = dz
        dzb_ref[...] = dz.astype(BF)

        @pl.when(i == 0)
        def _():
            dg_ref[...] = jnp.zeros_like(dg_ref)
            db_ref[...] = jnp.zeros_like(db_ref)

        dg_ref[...] += jnp.sum(dy * xhat, axis=0, keepdims=True)
        db_ref[...] += jnp.sum(dy, axis=0, keepdims=True)

    row = pl.BlockSpec((tm, D), lambda i: (i, 0))
    vec = pl.BlockSpec((1, D), lambda i: (0, 0))
    out_specs = [row, row, vec, vec] + ([pl.BlockSpec((8, 128), lambda i: (0, 0))] if head else [])
    out_shape = [jax.ShapeDtypeStruct((S, D), F32), jax.ShapeDtypeStruct((S, D), BF), jax.ShapeDtypeStruct((1, D), F32),
                 jax.ShapeDtypeStruct((1, D), F32)] + ([jax.ShapeDtypeStruct((8, 128), F32)] if head else [])
    operands = (dxn, target, z, g.reshape(1, D)) if head else (dxn, z, g.reshape(1, D))
    return _pcall(body, grid=(S // tm,), in_specs=[row] * (len(operands) - 1) + [vec], out_specs=out_specs,
                  out_shape=out_shape, name=name, comm=comm)(*operands)


def _rope_fwd(t, r_ref):
    return (t * r_ref[:, 0:128] + pltpu.roll(t, 120, 1) * r_ref[:, 128:256]
            + pltpu.roll(t, 8, 1) * r_ref[:, 256:384])


def _rope_bwd(g, r_ref):
    return (g * r_ref[:, 0:128] + pltpu.roll(g * r_ref[:, 128:256], 8, 1)
            + pltpu.roll(g * r_ref[:, 256:384], 120, 1))


def _dup_heads(kb):
    lo = lax.broadcasted_iota(jnp.int32, kb.shape, 1) < 64
    sw = pltpu.roll(kb, 64, 1)
    return [jnp.where(lo, kb, sw).astype(BF), jnp.where(lo, sw, kb).astype(BF)]


def _even_fwd(h, rope, lng, lnb, ws, bsb, sinks, name, comm=None):
    S = h.shape[0]
    nb = S // CHUNK

    def body(h_ref, hp_ref, rc_ref, rp_ref, lng_ref, lnb_ref, ws_ref, bsb_ref, sink_ref, mix_ref, o_ref, l_ref):
        n = pl.program_id(0)
        lane = lax.broadcasted_iota(jnp.int32, (128, 128), 1)
        rowi = lax.broadcasted_iota(jnp.int32, (128, 128), 0)
        tri = rowi >= lane
        lane_lo = lane < 64
        v = h_ref[:, W:2 * W]
        mu = jnp.mean(v, axis=1, keepdims=True)
        vc = v - mu
        var = jnp.mean(vc * vc, axis=1, keepdims=True)
        vn = vc * lax.rsqrt(var + LN_EPS) * lng_ref[...] + lnb_ref[...]
        ms = [_dot(jnp.where(tri, ws_ref[g], 0.0).astype(BF), vn[:, g * 128:(g + 1) * 128].astype(BF)) for g in range(8)]
        for g in range(8):
            sl = slice(g * 128, (g + 1) * 128)
            ag = h_ref[:, 2 * W + g * 128:2 * W + (g + 1) * 128]
            mix_ref[0, :, sl] = (h_ref[:, sl] * (ms[g] + bsb_ref[g]) * (ag * _sig(ag))).astype(BF)
        kb = jnp.concatenate([_rope_fwd(hp_ref[:, 0:128], rp_ref), _rope_fwd(h_ref[:, 4096:4224], rc_ref)], axis=0)
        vb = jnp.concatenate([hp_ref[:, 128:256], h_ref[:, 4224:4352]], axis=0)
        k2 = _dup_heads(kb)
        v2 = _dup_heads(vb)
        qi = lax.broadcasted_iota(jnp.int32, (128, 256), 0)
        kj = lax.broadcasted_iota(jnp.int32, (128, 256), 1)
        diff = qi + 128 - kj
        valid = (diff >= 0) & (diff < 128) & ((n > 0) | (kj >= 128))
        lacc = jnp.zeros((128, 128), F32)
        for j0 in range(0, 8, HEAD_COLS):
            heads = [(j, half) for j in range(j0, j0 + HEAD_COLS) for half in range(2)]
            sc, pr, oh = {}, {}, {}
            for j in range(j0, j0 + HEAD_COLS):
                qc = _rope_fwd(h_ref[:, 3072 + j * 128:3072 + (j + 1) * 128], rc_ref)
                sc[j, 0] = _dot_nt(jnp.where(lane_lo, qc, 0.0).astype(BF), k2[j // 4])
                sc[j, 1] = _dot_nt(jnp.where(lane_lo, 0.0, qc).astype(BF), k2[j // 4])
            for j, half in heads:
                hq = 2 * j + half
                s = jnp.where(valid, sc[j, half] * 0.125, NEG)
                sk = sink_ref[hq]
                mx = jnp.maximum(jnp.max(s, axis=1, keepdims=True), sk)
                p = jnp.exp(s - mx)
                den = jnp.sum(p, axis=1, keepdims=True) + jnp.exp(sk - mx)
                pr[j, half] = (p / den).astype(BF)
                lacc = jnp.where(lane == hq, mx + jnp.log(den), lacc)
            for j, half in heads:
                oh[j, half] = _dot(pr[j, half], v2[j // 4])
            for j in range(j0, j0 + HEAD_COLS):
                cs = slice(j * 128, (j + 1) * 128)
                ocol = jnp.where(lane_lo, oh[j, 0], oh[j, 1])
                bg = h_ref[:, 4352 + j * 128:4352 + (j + 1) * 128]
                o_ref[:, cs] = ocol
                mix_ref[1, :, cs] = (ocol * (bg * _sig(bg))).astype(BF)
        l_ref[...] = lacc

    prev = lambda n: jnp.maximum(n - 1, 0)
    full = lambda shape: pl.BlockSpec(shape, lambda n: (0,) * len(shape))
    return _pcall(
        body, grid=(nb,),
        in_specs=[pl.BlockSpec((CHUNK, EVEN_IN), lambda n: (n, 0)),
                  pl.BlockSpec((CHUNK, 256), lambda n: (prev(n), 16)),
                  pl.BlockSpec((CHUNK, 384), lambda n: (n, 0)),
                  pl.BlockSpec((CHUNK, 384), lambda n: (prev(n), 0)),
                  full((1, W)), full((1, W)), full((8, 128, 128)), full((8, 128, 128)),
                  pl.BlockSpec(memory_space=pltpu.SMEM)],
        out_specs=[pl.BlockSpec((2, CHUNK, W), lambda n: (0, n, 0)),
                   pl.BlockSpec((CHUNK, W), lambda n: (n, 0)),
                   pl.BlockSpec((CHUNK, 128), lambda n: (n, 0))],
        out_shape=[jax.ShapeDtypeStruct((2, S, W), BF), jax.ShapeDtypeStruct((S, W), F32),
                   jax.ShapeDtypeStruct((S, 128), F32)],
        name=name, comm=comm)(h, h, rope, rope, lng.reshape(1, W), lnb.reshape(1, W), ws, bsb, sinks)


def _even_bwd(h, dmix3, o, l, rope, lng, lnb, ws, wst, bsb, sinks, name, comm=None):
    S = h.shape[0]
    nb = S // CHUNK

    def body(h_ref, hp_ref, hn_ref, dm_ref, dmn_ref, o_ref, on_ref, l_ref, ln_ref, rc_ref, rp_ref, rn_ref,
             lng_ref, lnb_ref, ws_ref, wst_ref, bsb_ref, sink_ref,
             dh_ref, dws_ref, dbs_ref, dlng_ref, dlnb_ref, dsink_ref, dvn_ref):
        n = pl.program_id(0)

        @pl.when(n == 0)
        def _():
            dws_ref[...] = jnp.zeros_like(dws_ref)
            dbs_ref[...] = jnp.zeros_like(dbs_ref)
            dlng_ref[...] = jnp.zeros_like(dlng_ref)
            dlnb_ref[...] = jnp.zeros_like(dlnb_ref)
            dsink_ref[...] = jnp.zeros_like(dsink_ref)

        lane = lax.broadcasted_iota(jnp.int32, (128, 128), 1)
        rowi = lax.broadcasted_iota(jnp.int32, (128, 128), 0)
        lane1 = lax.broadcasted_iota(jnp.int32, (1, 128), 1)
        tri = rowi >= lane
        tri_t = lane >= rowi
        lane_lo = lane < 64
        v = h_ref[:, W:2 * W]
        mu = jnp.mean(v, axis=1, keepdims=True)
        vc = v - mu
        var = jnp.mean(vc * vc, axis=1, keepdims=True)
        rstd = lax.rsqrt(var + LN_EPS)
        vhat = vc * rstd
        vn = vhat * lng_ref[...] + lnb_ref[...]
        dbs_acc = jnp.zeros((128, 128), F32)
        vng = [vn[:, g * 128:(g + 1) * 128].astype(BF) for g in range(8)]
        ms = [_dot(jnp.where(tri, ws_ref[g], 0.0).astype(BF), vng[g]) for g in range(8)]
        dmb = []
        for g in range(8):
            sl = slice(g * 128, (g + 1) * 128)
            m = ms[g] + bsb_ref[g]
            ag = h_ref[:, 2 * W + g * 128:2 * W + (g + 1) * 128]
            sg, dsg = _silu_grad(ag)
            u = h_ref[:, sl]
            da = dm_ref[0, :, sl]
            dmm = da * u * sg
            dh_ref[:, sl] = (da * m * sg).astype(BF)
            dh_ref[:, 2 * W + g * 128:2 * W + (g + 1) * 128] = (da * u * m * dsg).astype(BF)
            dmb.append(dmm.astype(BF))
            dbs_acc = jnp.where(lane == g, jnp.sum(dmm, axis=1, keepdims=True), dbs_acc)
        dvs = [_dot(jnp.where(tri_t, wst_ref[g], 0.0).astype(BF), dmb[g]) for g in range(8)]
        dwss = [_dot_nt(dmb[g], vng[g]) for g in range(8)]
        for g in range(8):
            dvn_ref[:, g * 128:(g + 1) * 128] = dvs[g]
            dws_ref[g] += jnp.where(tri, dwss[g], 0.0)
        dbs_ref[...] += dbs_acc
        dvn = dvn_ref[...]
        dlng_ref[...] += jnp.sum(dvn * vhat, axis=0, keepdims=True)
        dlnb_ref[...] += jnp.sum(dvn, axis=0, keepdims=True)
        dyg = dvn * lng_ref[...]
        m1 = jnp.mean(dyg, axis=1, keepdims=True)
        m2 = jnp.mean(dyg * vhat, axis=1, keepdims=True)
        dh_ref[:, W:2 * W] = (rstd * (dyg - m1 - vhat * m2)).astype(BF)
        kcur = _rope_fwd(h_ref[:, 4096:4224], rc_ref)
        kb = jnp.concatenate([_rope_fwd(hp_ref[:, 0:128], rp_ref), kcur], axis=0)
        vb = jnp.concatenate([hp_ref[:, 128:256], h_ref[:, 4224:4352]], axis=0)
        k2 = _dup_heads(kb)
        v2 = _dup_heads(vb)
        kc2 = _dup_heads(kcur)
        vc2 = _dup_heads(h_ref[:, 4224:4352])
        qi = lax.broadcasted_iota(jnp.int32, (128, 256), 0)
        kj = lax.broadcasted_iota(jnp.int32, (128, 256), 1)
        diff = qi + 128 - kj
        valid = (diff >= 0) & (diff < 128) & ((n > 0) | (kj >= 128))
        validn = (lane > rowi) & (n < nb - 1)
        lc = l_ref[...]
        lnx = ln_ref[...]
        dk = [jnp.zeros((128, 128), F32), jnp.zeros((128, 128), F32)]
        dv = [jnp.zeros((128, 128), F32), jnp.zeros((128, 128), F32)]
        dsk_acc = jnp.zeros((1, 128), F32)
        for j0 in range(0, 8, HEAD_COLS):
            heads = [(j, half) for j in range(j0, j0 + HEAD_COLS) for half in range(2)]
            t = {}
            for j in range(j0, j0 + HEAD_COLS):
                cs = slice(j * 128, (j + 1) * 128)
                qc = _rope_fwd(h_ref[:, 3072 + j * 128:3072 + (j + 1) * 128], rc_ref)
                qn = _rope_fwd(hn_ref[:, 3072 + j * 128:3072 + (j + 1) * 128], rn_ref)
                bg = h_ref[:, 4352 + j * 128:4352 + (j + 1) * 128]
                sgb, dsgb = _silu_grad(bg)
                db = dm_ref[1, :, cs]
                oc = o_ref[:, cs]
                do = db * sgb
                dh_ref[:, 4352 + j * 128:4352 + (j + 1) * 128] = (db * oc * dsgb).astype(BF)
                bgn = hn_ref[:, 4352 + j * 128:4352 + (j + 1) * 128]
                don = dmn_ref[1, :, cs] * (bgn * _sig(bgn))
                prod = do * oc
                prodn = don * on_ref[:, cs]
                for half in range(2):
                    hq = 2 * j + half
                    hm = lane_lo if half == 0 else jnp.logical_not(lane_lo)
                    t[j, half] = dict(
                        dsum=jnp.sum(jnp.where(hm, prod, 0.0), axis=1, keepdims=True),
                        dsumn=jnp.sum(jnp.where(hm, prodn, 0.0), axis=1, keepdims=True),
                        lh=jnp.sum(jnp.where(lane == hq, lc, 0.0), axis=1, keepdims=True),
                        lhn=jnp.sum(jnp.where(lane == hq, lnx, 0.0), axis=1, keepdims=True),
                        qm=jnp.where(hm, qc, 0.0).astype(BF), dom=jnp.where(hm, do, 0.0).astype(BF),
                        qnm=jnp.where(hm, qn, 0.0).astype(BF), donm=jnp.where(hm, don, 0.0).astype(BF))
            for j, half in heads:
                e, hk = t[j, half], j // 4
                e["s"], e["dp"] = _dot_nt(e["qm"], k2[hk]), _dot_nt(e["dom"], v2[hk])
                e["sn"], e["dpn"] = _dot_nt(e["qnm"], kc2[hk]), _dot_nt(e["donm"], vc2[hk])
            for j, half in heads:
                e, hq = t[j, half], 2 * j + half
                p = jnp.exp(jnp.where(valid, e["s"] * 0.125 - e["lh"], NEG))
                ds = p * (e["dp"] - e["dsum"])
                pn = jnp.exp(jnp.where(validn, e["sn"] * 0.125 - e["lhn"], NEG))
                dsn = pn * (e["dpn"] - e["dsumn"])
                psink = jnp.exp(sink_ref[hq] - e["lh"])
                dsk_acc = jnp.where(lane1 == hq, -jnp.sum(psink * e["dsum"], axis=0, keepdims=True), dsk_acc)
                e["ds"] = ds.astype(BF)
                e["pt"], e["dst"] = jnp.transpose(p[:, 128:256]).astype(BF), jnp.transpose(ds[:, 128:256]).astype(BF)
                e["pnt"], e["dsnt"] = jnp.transpose(pn).astype(BF), jnp.transpose(dsn).astype(BF)
            for j, half in heads:
                e, hk = t[j, half], j // 4
                e["dq"] = _dot(e["ds"], k2[hk])
                e["dv"] = _dot(e["pt"], e["dom"]) + _dot(e["pnt"], e["donm"])
                e["dk"] = _dot(e["dst"], e["qm"]) + _dot(e["dsnt"], e["qnm"])
            for j in range(j0, j0 + HEAD_COLS):
                hk = j // 4
                dqcol = jnp.where(lane_lo, t[j, 0]["dq"], t[j, 1]["dq"]) * 0.125
                dh_ref[:, 3072 + j * 128:3072 + (j + 1) * 128] = _rope_bwd(dqcol, rc_ref).astype(BF)
                dv[hk] = dv[hk] + t[j, 0]["dv"] + t[j, 1]["dv"]
                dk[hk] = dk[hk] + (t[j, 0]["dk"] + t[j, 1]["dk"]) * 0.125
        fold = lambda a: a + pltpu.roll(a, 64, 1)
        dh_ref[:, 4096:4224] = _rope_bwd(jnp.where(lane_lo, fold(dk[0]), fold(dk[1])), rc_ref).astype(BF)
        dh_ref[:, 4224:4352] = jnp.where(lane_lo, fold(dv[0]), fold(dv[1])).astype(BF)
        dsink_ref[...] += dsk_acc

    prev = lambda n: jnp.maximum(n - 1, 0)
    nxt = lambda n: jnp.minimum(n + 1, nb - 1)
    full = lambda shape: pl.BlockSpec(shape, lambda n: (0,) * len(shape))
    return _pcall(
        body, grid=(nb,),
        in_specs=[pl.BlockSpec((CHUNK, EVEN_IN), lambda n: (n, 0)),
                  pl.BlockSpec((CHUNK, 256), lambda n: (prev(n), 16)),
                  pl.BlockSpec((CHUNK, EVEN_IN), lambda n: (nxt(n), 0)),
                  pl.BlockSpec((2, CHUNK, W), lambda n: (0, n, 0)),
                  pl.BlockSpec((2, CHUNK, W), lambda n: (0, nxt(n), 0)),
                  pl.BlockSpec((CHUNK, W), lambda n: (n, 0)),
                  pl.BlockSpec((CHUNK, W), lambda n: (nxt(n), 0)),
                  pl.BlockSpec((CHUNK, 128), lambda n: (n, 0)),
                  pl.BlockSpec((CHUNK, 128), lambda n: (nxt(n), 0)),
                  pl.BlockSpec((CHUNK, 384), lambda n: (n, 0)),
                  pl.BlockSpec((CHUNK, 384), lambda n: (prev(n), 0)),
                  pl.BlockSpec((CHUNK, 384), lambda n: (nxt(n), 0)),
                  full((1, W)), full((1, W)), full((8, 128, 128)), full((8, 128, 128)), full((8, 128, 128)),
                  pl.BlockSpec(memory_space=pltpu.SMEM)],
        out_specs=[pl.BlockSpec((CHUNK, EVEN_IN), lambda n: (n, 0)),
                   full((8, 128, 128)), full((128, 128)), full((1, W)), full((1, W)), full((1, 128))],
        out_shape=[jax.ShapeDtypeStruct((S, EVEN_IN), BF), jax.ShapeDtypeStruct((8, 128, 128), F32),
                   jax.ShapeDtypeStruct((128, 128), F32), jax.ShapeDtypeStruct((1, W), F32),
                   jax.ShapeDtypeStruct((1, W), F32), jax.ShapeDtypeStruct((1, 128), F32)],
        scratch=[pltpu.VMEM((CHUNK, W), F32)], name=name, comm=comm,
    )(h, h, h, dmix3, dmix3, o, o, l, l, rope, rope, rope, lng.reshape(1, W), lnb.reshape(1, W), ws, wst, bsb, sinks)


def _expm1(x):
    ser = x * (1.0 + x * (0.5 + x * (1.0 / 6.0 + x * (1.0 / 24.0))))
    return jnp.where(jnp.abs(x) < 1e-2, ser, jnp.exp(x) - 1.0)


def _softplus_neg(lam):
    z = -lam
    e = jnp.exp(-jnp.abs(z))
    l1p = jnp.where(e < 1e-3, e * (1.0 - e * (0.5 - e * (1.0 / 3.0))), jnp.log(1.0 + e))
    return jnp.maximum(z, 0.0) + l1p


def _shift_down(x, k, row, fill=0.0):
    return jnp.where(row >= k, pltpu.roll(x, k, 0), fill)


def _shift_up(x, k, row, fill=0.0):
    S = x.shape[0]
    return jnp.where(row < S - k, pltpu.roll(x, S - k, 0), fill)


def _lru_gates(xc, row, cw_ref, cb_ref, wa_ref, wx_ref, ba_ref, bx_ref, lam_ref):
    xconv = (cw_ref[3:4, :] * xc + cw_ref[2:3, :] * _shift_down(xc, 1, row) + cw_ref[1:2, :] * _shift_down(xc, 2, row)
             + cw_ref[0:1, :] * _shift_down(xc, 3, row) + cb_ref[...])
    xb = xconv.astype(BF)
    r = _sig(_dot(xb, wa_ref[...]) + ba_ref[...])
    i = _sig(_dot(xb, wx_ref[...]) + bx_ref[...])
    sp = _softplus_neg(lam_ref[...])
    log_a = -LRU_C * r * sp
    a = jnp.exp(log_a)
    mult = jnp.sqrt(-_expm1(2.0 * log_a))
    return xconv, r, i, sp, a, mult


ROWS_PER_TILE = 8


def _steps(a, b, shift, inside, products=True):
    n, k = inside.n, 1
    while k < n:
        b = a * jnp.where(inside(k), shift(b, k), 0.0) + b
        if products or 2 * k < n:
            a = a * jnp.where(inside(k), shift(a, k), 1.0)
        k *= 2
    return a, b


class _Inside:
    def __init__(self, pos, n, reverse):
        self.pos, self.n, self.reverse = pos, n, reverse

    def __call__(self, k):
        return self.pos < self.n - k if self.reverse else self.pos >= k


def _scan_rows(a, b, row, a_ref, b_ref, c_ref, reverse=False):
    S = a.shape[0]
    G = S // ROWS_PER_TILE
    if reverse:
        shift = lambda x, k: pltpu.roll(x, x.shape[0] - k, 0)
    else:
        shift = lambda x, k: pltpu.roll(x, k, 0)
    a, b = _steps(a, b, shift, _Inside(row % ROWS_PER_TILE, ROWS_PER_TILE, reverse))
    a_ref[...] = a
    b_ref[...] = b
    last = 0 if reverse else ROWS_PER_TILE - 1
    grow = lax.broadcasted_iota(jnp.int32, (G, a.shape[1]), 0)
    _, tot = _steps(a_ref[pl.ds(last, G, stride=ROWS_PER_TILE), :], b_ref[pl.ds(last, G, stride=ROWS_PER_TILE), :],
                    shift, _Inside(grow, G, reverse), products=False)
    enters = jnp.where(_Inside(grow, G, reverse)(1), shift(tot, 1), 0.0)
    for r in range(ROWS_PER_TILE):
        c_ref[pl.ds(r, G, stride=ROWS_PER_TILE), :] = enters
    return b + a * c_ref[...]


def _odd_c_fwd(h, cw, cb, wa, wx, ba, bx, lam, name, comm=None):
    S = h.shape[0]

    def body(xc_ref, cg_ref, cw_ref, cb_ref, wa_ref, wx_ref, ba_ref, bx_ref, lam_ref, mix_ref, hst_ref, sa_ref, sb_ref, sc_ref):
        row = lax.broadcasted_iota(jnp.int32, (S, 128), 0)
        xconv, r, i, sp, a, mult = _lru_gates(xc_ref[...], row, cw_ref, cb_ref, wa_ref, wx_ref, ba_ref, bx_ref, lam_ref)
        bb = _scan_rows(a, mult * (i * xconv), row, sa_ref, sb_ref, sc_ref)
        hst_ref[...] = bb
        cg = cg_ref[...]
        mix_ref[...] = (bb * (cg * _sig(cg))).astype(BF)

    col = lambda off: pl.BlockSpec((S, 128), lambda j: (0, off + j))
    vec = pl.BlockSpec((1, 128), lambda j: (0, j))
    mat = pl.BlockSpec((None, 128, 128), lambda j: (j, 0, 0))
    return _pcall(
        body, grid=(8,),
        in_specs=[col(0), col(8), pl.BlockSpec((4, 128), lambda j: (0, j)), vec, mat, mat, vec, vec, vec],
        out_specs=[pl.BlockSpec((None, S, 128), lambda j: (0, 0, j)), pl.BlockSpec((S, 128), lambda j: (0, j))],
        out_shape=[jax.ShapeDtypeStruct((2, S, W), BF), jax.ShapeDtypeStruct((S, W), F32)],
        scratch=[pltpu.VMEM((S, 128), F32)] * 3, name=name, comm=comm,
    )(h, h, cw, cb.reshape(1, W), wa, wx, ba.reshape(1, W), bx.reshape(1, W), lam.reshape(1, W))


def _pool_sums(x, g, row, shift):
    s2 = x + shift(x, 1, row)
    s4 = s2 + shift(s2, 2, row)
    s8 = s4 + shift(s4, 4, row)
    s16 = s8 + shift(s8, 8, row)
    return jnp.where(g == 0, s2, jnp.where(g == 1, s4, jnp.where(g == 2, s8, s16)))


def _odd_d_fwd(h, mix3, wp, dscale, name):
    S = h.shape[0]

    def body(xd_ref, dg_ref, wp_ref, ds_ref, mix_in, mix_ref):
        g = pl.program_id(0)
        row = lax.broadcasted_iota(jnp.int32, (S, 256), 0)
        xd = xd_ref[...]
        cnt = jnp.minimum(row + 1, jnp.left_shift(2, g)).astype(F32)
        pooled = _pool_sums(xd, g, row, _shift_down) / cnt - xd
        mixed = _dot(pooled.astype(BF), wp_ref[...])
        dg = dg_ref[...]
        mix_ref[...] = (mixed * ds_ref[...] * (dg * _sig(dg))).astype(BF)

    col = lambda off: pl.BlockSpec((S, 256), lambda g: (0, off + g))
    return pl.pallas_call(
        body, grid=(4,),
        in_specs=[col(8), col(12), pl.BlockSpec((None, 256, 256), lambda g: (g, 0, 0)),
                  pl.BlockSpec((1, 256), lambda g: (0, g)), ANY],
        out_specs=pl.BlockSpec((None, S, 256), lambda g: (1, 0, g)),
        out_shape=jax.ShapeDtypeStruct((2, S, W), BF), input_output_aliases={4: 0},
        name=name, compiler_params=_cp(),
    )(h, h, wp, dscale.reshape(1, W), mix3)


def _odd_c_bwd(h, hst, dmix3, cw, cb, wa, wx, wat, wxt, ba, bx, lam, name, comm=None):
    S = h.shape[0]

    def body(xc_ref, cg_ref, hst_ref, dc_ref, cw_ref, cb_ref, wa_ref, wx_ref, wat_ref, wxt_ref, ba_ref, bx_ref, lam_ref,
             dh_ref, dcw_ref, dcb_ref, dwa_ref, dwx_ref, dba_ref, dbx_ref, dlam_ref, sa_ref, sb_ref, sc_ref):
        row = lax.broadcasted_iota(jnp.int32, (S, 128), 0)
        xc = xc_ref[...]
        xconv, r, i, sp, a, mult = _lru_gates(xc, row, cw_ref, cb_ref, wa_ref, wx_ref, ba_ref, bx_ref, lam_ref)
        hst = hst_ref[...]
        cg = cg_ref[...]
        sg, dsg = _silu_grad(cg)
        dc = dc_ref[...]
        dh_ref[1] = (dc * hst * dsg).astype(BF)
        lam_t = _scan_rows(_shift_up(a, 1, row), dc * sg, row, sa_ref, sb_ref, sc_ref, reverse=True)
        da = lam_t * _shift_down(hst, 1, row)
        ix = i * xconv
        dmult = lam_t * ix
        di = lam_t * mult * xconv
        dxconv = lam_t * mult * i
        dlog_a = da * a - dmult * (a * a / mult)
        dr = dlog_a * (-LRU_C * sp)
        dsp = jnp.sum(dlog_a * (-LRU_C * r), axis=0, keepdims=True)
        dlam_ref[...] = dsp * (-_sig(-lam_ref[...]))
        dpa = dr * r * (1.0 - r)
        dpx = di * i * (1.0 - i)
        dpab = dpa.astype(BF)
        dpxb = dpx.astype(BF)
        xb = xconv.astype(BF)
        dxconv = dxconv + _dot(dpab, wat_ref[...]) + _dot(dpxb, wxt_ref[...])
        dwa_ref[...] = _dot_tn(xb, dpab)
        dwx_ref[...] = _dot_tn(xb, dpxb)
        dba_ref[...] = jnp.sum(dpa, axis=0, keepdims=True)
        dbx_ref[...] = jnp.sum(dpx, axis=0, keepdims=True)
        dh_ref[0] = (cw_ref[3:4, :] * dxconv + cw_ref[2:3, :] * _shift_up(dxconv, 1, row)
                     + cw_ref[1:2, :] * _shift_up(dxconv, 2, row) + cw_ref[0:1, :] * _shift_up(dxconv, 3, row)).astype(BF)
        for j in range(4):
            src = xc if j == 3 else _shift_down(xc, 3 - j, row)
            dcw_ref[j:j + 1, :] = jnp.sum(dxconv * src, axis=0, keepdims=True)
        dcb_ref[...] = jnp.sum(dxconv, axis=0, keepdims=True)

    col = lambda off: pl.BlockSpec((S, 128), lambda j: (0, off + j))
    vec = pl.BlockSpec((1, 128), lambda j: (0, j))
    mat = pl.BlockSpec((None, 128, 128), lambda j: (j, 0, 0))
    vshape = jax.ShapeDtypeStruct((1, W), F32)
    mshape = jax.ShapeDtypeStruct((8, 128, 128), F32)
    return _pcall(
        body, grid=(8,),
        in_specs=[col(0), col(8), col(0), pl.BlockSpec((None, S, 128), lambda j: (0, 0, j)),
                  pl.BlockSpec((4, 128), lambda j: (0, j)), vec, mat, mat, mat, mat, vec, vec, vec],
        out_specs=[pl.BlockSpec((2, S, 128), lambda j: (0, 0, j)), pl.BlockSpec((4, 128), lambda j: (0, j)), vec,
                   mat, mat, vec, vec, vec],
        out_shape=[jax.ShapeDtypeStruct((4, S, W), BF), jax.ShapeDtypeStruct((4, W), F32), vshape, mshape, mshape,
                   vshape, vshape, vshape],
        scratch=[pltpu.VMEM((S, 128), F32)] * 3, name=name, vmem=56, comm=comm,
    )(h, h, hst, dmix3, cw, cb.reshape(1, W), wa, wx, wat, wxt, ba.reshape(1, W), bx.reshape(1, W), lam.reshape(1, W))


def _odd_d_bwd(h, dmix3, dh4, wp, wpt, dscale, name):
    S = h.shape[0]

    def body(xd_ref, dg_ref, dd_ref, wp_ref, wpt_ref, ds_ref, dh_in, dh_ref, dwp_ref, dds_ref):
        g = pl.program_id(0)
        row = lax.broadcasted_iota(jnp.int32, (S, 256), 0)
        xd = xd_ref[...]
        cnt = jnp.minimum(row + 1, jnp.left_shift(2, g)).astype(F32)
        pooled = _pool_sums(xd, g, row, _shift_down) / cnt - xd
        pb = pooled.astype(BF)
        mixed = _dot(pb, wp_ref[...])
        dg = dg_ref[...]
        sg, dsg = _silu_grad(dg)
        dd = dd_ref[...]
        dmixed = dd * ds_ref[...] * sg
        dds_ref[...] = jnp.sum(dd * mixed * sg, axis=0, keepdims=True)
        dh_ref[1] = (dd * mixed * ds_ref[...] * dsg).astype(BF)
        dmb = dmixed.astype(BF)
        dpooled = _dot(dmb, wpt_ref[...])
        dwp_ref[...] = _dot_tn(pb, dmb)
        dh_ref[0] = (_pool_sums(dpooled / cnt, g, row, _shift_up) - dpooled).astype(BF)

    col = lambda off: pl.BlockSpec((S, 256), lambda g: (0, off + g))
    mat = pl.BlockSpec((None, 256, 256), lambda g: (g, 0, 0))
    vec = pl.BlockSpec((1, 256), lambda g: (0, g))
    return pl.pallas_call(
        body, grid=(4,),
        in_specs=[col(8), col(12), pl.BlockSpec((None, S, 256), lambda g: (1, 0, g)), mat, mat, vec, ANY],
        out_specs=[pl.BlockSpec((2, S, 256), lambda g: (1, 0, g)), mat, vec],
        out_shape=[jax.ShapeDtypeStruct((4, S, W), BF), jax.ShapeDtypeStruct((4, 256, 256), F32),
                   jax.ShapeDtypeStruct((1, W), F32)],
        input_output_aliases={6: 0}, name=name, compiler_params=_cp(56),
    )(h, h, dmix3, wp, wpt, dscale.reshape(1, W), dh4)


def _peer(d):
    x, y, c = lax.axis_index("x"), lax.axis_index("y"), lax.axis_index("c")
    px = 1 - x if d & 4 else x
    py = 1 - y if d & 2 else y
    pc = 1 - c if d & 1 else c
    return (px, py, pc), 4 * px + 2 * py + pc


class _GatherAll(_Comm):
    def __init__(self, xs):
        self.peers = EVERYONE
        self.inputs = [xs]
        self.out_shapes = [jax.ShapeDtypeStruct((N_DEV,) + xs.shape, xs.dtype)]
        self.sem_shapes = [pltpu.SemaphoreType.DMA((N_DEV - 1,)), pltpu.SemaphoreType.DMA((N_DEV - 1,)),
                           pltpu.SemaphoreType.DMA]

    def copies(self, ins, outs, sems):
        (x_ref,), (out_ref,), (send, recv, loc) = ins, outs, sems
        _, me = _peer(0)
        res = [pltpu.make_async_copy(x_ref, out_ref.at[me], loc)]
        for d in range(1, N_DEV):
            peer, _ = _peer(d)
            res.append(pltpu.make_async_remote_copy(src_ref=x_ref, dst_ref=out_ref.at[me], send_sem=send.at[d - 1],
                                                    recv_sem=recv.at[d - 1], device_id=peer, device_id_type=MESH))
        return res


class _ExchangeAll(_Comm):
    def __init__(self, g8):
        self.peers = EVERYONE
        self.inputs = [g8]
        self.out_shapes = [jax.ShapeDtypeStruct(g8.shape, g8.dtype)]
        self.sem_shapes = [pltpu.SemaphoreType.DMA((N_DEV - 1,)), pltpu.SemaphoreType.DMA((N_DEV - 1,)),
                           pltpu.SemaphoreType.DMA]

    def copies(self, ins, outs, sems):
        (g_ref,), (out_ref,), (send, recv, loc) = ins, outs, sems
        _, me = _peer(0)
        res = [pltpu.make_async_copy(g_ref.at[me], out_ref.at[0], loc)]
        for d in range(1, N_DEV):
            peer, pidx = _peer(d)
            res.append(pltpu.make_async_remote_copy(src_ref=g_ref.at[pidx], dst_ref=out_ref.at[d], send_sem=send.at[d - 1],
                                                    recv_sem=recv.at[d - 1], device_id=peer, device_id_type=MESH))
        return res


def _sum8(r8, tr, name):
    _, R, C = r8.shape
    tr = min(tr, R)
    assert R % tr == 0

    def body(r_ref, o_ref):
        acc = r_ref[0]
        for d in range(1, N_DEV):
            acc = acc + r_ref[d]
        o_ref[...] = acc

    return pl.pallas_call(
        body, grid=(R // tr,), in_specs=[pl.BlockSpec((N_DEV, tr, C), lambda i: (0, i, 0))],
        out_specs=pl.BlockSpec((tr, C), lambda i: (i, 0)), out_shape=jax.ShapeDtypeStruct((R, C), F32),
        name=name, compiler_params=_cp(),
    )(r8)


def _adamw_math(w, g, m, v):
    m2 = B1 * m + (1.0 - B1) * g
    v2 = B2 * v + (1.0 - B2) * (g * g)
    m_hat = m2 / (1.0 - B1 ** STEP)
    v_hat = v2 / (1.0 - B2 ** STEP)
    return -LR * (m_hat / (jnp.sqrt(v_hat) + ADAM_EPS) + WD * w), m2, v2


def _adamw_many(ws, gs, ms, vs, name):
    n = len(ws)

    def body(*refs):
        for i in range(n):
            d, m2, v2 = _adamw_math(refs[i][...], refs[n + i][...], refs[2 * n + i][...], refs[3 * n + i][...])
            refs[4 * n + i][...] = d
            refs[5 * n + i][...] = m2
            refs[6 * n + i][...] = v2

    vmem = pl.BlockSpec(memory_space=pltpu.VMEM)
    shapes = [jax.ShapeDtypeStruct(w.shape, F32) for w in ws]
    res = pl.pallas_call(body, in_specs=[vmem] * (4 * n), out_specs=[vmem] * (3 * n), out_shape=shapes * 3, name=name,
                         compiler_params=_cp())(*ws, *gs, *ms, *vs)
    return res[:n], res[n:2 * n], res[2 * n:]


def _adamw(w3, gs, m3, v3, tr, name, comm=None):
    _, R, C = w3.shape
    n = 2 if isinstance(gs[0], tuple) else 1

    def gradient(refs):
        if n == 1:
            return refs[0][...]
        s_ref, r_ref = refs
        return ((s_ref[...].astype(F32) + r_ref[0].astype(F32)) + r_ref[1].astype(F32)) + r_ref[2].astype(F32)

    def body(w_ref, *rest):
        g_refs, (m_ref, v_ref, d_ref, m2_ref, v2_ref, g_ref) = rest[:2 * n], rest[2 * n:]
        g = jnp.where(pl.program_id(0) == 0, gradient(g_refs[:n]), gradient(g_refs[n:]))
        d_ref[...], m2_ref[...], v2_ref[...] = _adamw_math(w_ref[...], g, m_ref[...], v_ref[...])
        g_ref[...] = g

    blk = pl.BlockSpec((None, tr, C), lambda j, i: (j, i, 0))

    def grad_specs(layer):
        at = lambda j, i: jnp.where(j == layer, i, 0)
        if n == 1:
            return [pl.BlockSpec((tr, C), lambda j, i: (at(j, i), 0))]
        return [pl.BlockSpec((None, tr, C), lambda j, i: (0, at(j, i), 0)), pl.BlockSpec((3, tr, C), lambda j, i: (0, at(j, i), 0))]

    flat = [a for g in gs for a in (g if n == 2 else (g,))]
    shp = jax.ShapeDtypeStruct((2, R, C), F32)
    return _pcall(body, grid=(2, R // tr), in_specs=[blk] + grad_specs(0) + grad_specs(1) + [blk, blk], out_specs=[blk] * 4,
                  out_shape=[shp] * 4, name=name, comm=comm)(w3, *flat, m3, v3)


def _rep_pack(a):
    n = a.size
    pad = (-n) % 1024
    f = a.reshape(-1)
    if pad:
        f = jnp.concatenate([f, jnp.zeros((pad,), a.dtype)])
    return f.reshape(N_DEV, -1, 128)


def _rep_unpack(p, shape):
    n = 1
    for s in shape:
        n *= s
    return p.reshape(-1)[:n].reshape(shape)


def _sh_pack(a, axis):
    shp = a.shape
    a = a.reshape(shp[:axis] + (N_DEV, shp[axis] // N_DEV) + shp[axis + 1:])
    return jnp.moveaxis(a, axis, 0).reshape(N_DEV, -1, 128)


def _sh_unpack(p, shape, axis):
    a = p.reshape((N_DEV,) + shape[:axis] + (shape[axis] // N_DEV,) + shape[axis + 1:])
    return jnp.moveaxis(a, 0, axis).reshape(shape)


def _pad_rows(a, mult=8):
    pad = (-a.shape[-2]) % mult
    if pad:
        a = jnp.concatenate([a, jnp.zeros(a.shape[:-2] + (pad, a.shape[-1]), a.dtype)], axis=-2)
    return a


REP = ["even_a_ln_g", "even_a_ln_b", "even_a_ws", "even_a_bs", "even_b_sinks", "even_ln_g", "even_ln_b",
       "odd_w_a", "odd_w_x"]
SH = [("odd_conv_w", (2, 4, W), 2), ("odd_conv_b", (2, W), 1), ("odd_b_a", (2, W), 1), ("odd_b_x", (2, W), 1),
      ("odd_lam", (2, W), 1), ("odd_w_pool", (2, 4, 256, 256), 2), ("odd_d_scale", (2, W), 1),
      ("odd_ln_g", (2, D), 1), ("odd_ln_b", (2, D), 1)]
BIG = ["even_w_in", "even_w_out", "odd_w_in", "odd_w_out"]
NAMES = ["even_w_in", "even_a_ln_g", "even_a_ln_b", "even_a_ws", "even_a_bs", "even_b_sinks", "even_w_out",
         "even_ln_g", "even_ln_b", "odd_w_in", "odd_conv_w", "odd_conv_b", "odd_w_a", "odd_b_a", "odd_w_x", "odd_b_x",
         "odd_lam", "odd_w_pool", "odd_d_scale", "odd_w_out", "odd_ln_g", "odd_ln_b"]


def _rope_table(positions):
    inv = ROPE_THETA ** (-jnp.arange(0, 16, 2, dtype=F32) / 16)
    f = jnp.arange(128) % 64
    ang = positions.astype(F32)[:, None] * inv[f % 8][None, :]
    cos, sin = jnp.cos(ang), jnp.sin(ang)
    return jnp.concatenate([jnp.where(f < 16, cos, 1.0), jnp.where(f < 8, -sin, 0.0),
                            jnp.where((f >= 8) & (f < 16), sin, 0.0)], axis=1)


def kernel(x, positions, even_w_in, even_a_ln_g, even_a_ln_b, even_a_ws, even_a_bs, even_b_sinks, even_w_out, even_ln_g, even_ln_b, odd_w_in, odd_conv_w, odd_conv_b, odd_w_a, odd_b_a, odd_w_x, odd_b_x, odd_lam, odd_w_pool, odd_d_scale, odd_w_out, odd_ln_g, odd_ln_b, loss_target, m_even_w_in, m_even_a_ln_g, m_even_a_ln_b, m_even_a_ws, m_even_a_bs, m_even_b_sinks, m_even_w_out, m_even_ln_g, m_even_ln_b, m_odd_w_in, m_odd_conv_w, m_odd_conv_b, m_odd_w_a, m_odd_b_a, m_odd_w_x, m_odd_b_x, m_odd_lam, m_odd_w_pool, m_odd_d_scale, m_odd_w_out, m_odd_ln_g, m_odd_ln_b, v_even_w_in, v_even_a_ln_g, v_even_a_ln_b, v_even_a_ws, v_even_a_bs, v_even_b_sinks, v_even_w_out, v_even_ln_g, v_even_ln_b, v_odd_w_in, v_odd_conv_w, v_odd_conv_b, v_odd_w_a, v_odd_b_a, v_odd_w_x, v_odd_b_x, v_odd_lam, v_odd_w_pool, v_odd_d_scale, v_odd_w_out, v_odd_ln_g, v_odd_ln_b):
    args = (even_w_in, even_a_ln_g, even_a_ln_b, even_a_ws, even_a_bs, even_b_sinks, even_w_out, even_ln_g, even_ln_b,
            odd_w_in, odd_conv_w, odd_conv_b, odd_w_a, odd_b_a, odd_w_x, odd_b_x, odd_lam, odd_w_pool, odd_d_scale,
            odd_w_out, odd_ln_g, odd_ln_b)
    margs = (m_even_w_in, m_even_a_ln_g, m_even_a_ln_b, m_even_a_ws, m_even_a_bs, m_even_b_sinks, m_even_w_out,
             m_even_ln_g, m_even_ln_b, m_odd_w_in, m_odd_conv_w, m_odd_conv_b, m_odd_w_a, m_odd_b_a, m_odd_w_x,
             m_odd_b_x, m_odd_lam, m_odd_w_pool, m_odd_d_scale, m_odd_w_out, m_odd_ln_g, m_odd_ln_b)
    vargs = (v_even_w_in, v_even_a_ln_g, v_even_a_ln_b, v_even_a_ws, v_even_a_bs, v_even_b_sinks, v_even_w_out,
             v_even_ln_g, v_even_ln_b, v_odd_w_in, v_odd_conv_w, v_odd_conv_b, v_odd_w_a, v_odd_b_a, v_odd_w_x,
             v_odd_b_x, v_odd_lam, v_odd_w_pool, v_odd_d_scale, v_odd_w_out, v_odd_ln_g, v_odd_ln_b)
    wts = dict(zip(NAMES, args))
    mom = dict(zip(NAMES, margs))
    var = dict(zip(NAMES, vargs))
    S = x.shape[1]
    x0 = x[0]
    rope = _rope_table(positions[0])

    kinds = ("even", "odd", "even", "odd")
    blk_in = [jnp.transpose(wts[kinds[l] + "_w_in"][l // 2]).astype(BF) for l in range(4)]
    blk_out = [wts[kinds[l] + "_w_out"][l // 2].astype(BF) for l in range(4)]
    sh_local = _pad_rows(jnp.concatenate([wts[nm].reshape(-1, 128) for nm, _, _ in SH], axis=0), 16)
    me = 4 * lax.axis_index("x") + 2 * lax.axis_index("y") + lax.axis_index("c")
    own_slot = lambda blk: lax.dynamic_update_slice(lax.empty((N_DEV,) + blk.shape, blk.dtype), blk[None], (me, 0, 0))
    reg = {"blk_small": sh_local, "w_small": own_slot(sh_local)}
    sched = _Sched(reg)
    for l in range(4):
        reg[f"blk_in{l}"], reg[f"blk_out{l}"] = blk_in[l], blk_out[l]
        reg[f"w_in{l}"], reg[f"w_out{l}"] = own_slot(blk_in[l]), own_slot(blk_out[l])
    sched.add(_rows("blk_in0", "w_in0", "ag1", blk_in[0].shape[0], ROW_CHUNK[blk_in[0].shape[0]]))
    sched.add(_rows("blk_small", "w_small", "ag1", sh_local.shape[0], sh_local.shape[0]))
    for l in range(4):
        sched.add(_rows(f"blk_out{l}", f"w_out{l}", "ag1", D // N_DEV, ROW_CHUNK[D // N_DEV]))
        if l < 3:
            r = blk_in[l + 1].shape[0]
            sched.add(_rows(f"blk_in{l + 1}", f"w_in{l + 1}", "ag1", r, ROW_CHUNK[r]))

    def gathered(dst, blk):
        sched.flush(dst, FLUSH_EXTRA_US)
        return reg.pop(dst)

    (xb0,) = sched.run(_cast_rows, FIRST_CARRY_US, x0, "cast_x")
    wt_in0 = gathered("w_in0", blk_in[0]).reshape(-1, D)
    full = {nm: wts[nm] for nm in REP}

    def gather_small():
        sh_all = gathered("w_small", sh_local)
        off = 0
        for nm, shape, axis in SH:
            r = wts[nm].size // 128
            full[nm] = _sh_unpack(sh_all[:, off:off + r, :], shape, axis)
            off += r

    saved = []
    wt_in, w_out = [wt_in0, None, None, None], [None] * 4
    xf, xb = x0, xb0
    fwd = lambda name: FWD_OVERBOOK * CARRY_US[name]
    for layer in range(4):
        j = layer // 2
        kind = kinds[layer]
        if wt_in[layer] is None:
            wt_in[layer] = gathered(f"w_in{layer}", blk_in[layer]).reshape(-1, D)
        h = sched.run(_mm_nt, fwd("mm_h_" + kind), xb, wt_in[layer], 1024, 768 if kind == "even" else 512, "mm_h_" + kind)
        if kind == "even":
            bsb = jnp.broadcast_to(full["even_a_bs"][j][:, :, None], (8, 128, 128))
            mix3, o, l = sched.run(_even_fwd, fwd("even_fwd"), h, rope, full["even_a_ln_g"][j], full["even_a_ln_b"][j],
                                   full["even_a_ws"][j], bsb, full["even_b_sinks"][j], "even_fwd")
            extra = (o, l, bsb)
        else:
            if "odd_lam" not in full:
                gather_small()
            wa, wx = full["odd_w_a"][j].astype(BF), full["odd_w_x"][j].astype(BF)
            wp = full["odd_w_pool"][j].astype(BF)
            mix3, hst = sched.run(_odd_c_fwd, fwd("odd_c_fwd"), h, full["odd_conv_w"][j], full["odd_conv_b"][j], wa, wx,
                                  full["odd_b_a"][j], full["odd_b_x"][j], full["odd_lam"][j], "odd_c_fwd")
            mix3 = _odd_d_fwd(h, mix3, wp, full["odd_d_scale"][j], "odd_d_fwd")
            extra = (hst, wa, wx, wp)
        w_out[layer] = gathered(f"w_out{layer}", blk_out[layer]).reshape(D, D)
        z, xn, xnb = sched.run(_mm_out_ln, fwd("mm_out_ln"), mix3, w_out[layer], xf, full[kind + "_ln_g"][j],
                               full[kind + "_ln_b"][j], "mm_out_ln")
        saved.append((xb, h, mix3, z, extra))
        xf, xb = xn, xnb

    dxn = xf

    gsum = {nm: [None, None] for nm in NAMES}

    chip_sums = {}
    sched.overhang = 0.15

    waiting = []

    def chip_sum(g, tag, key):
        r = g.shape[0] // N_DEV
        reg["g_" + key] = g.reshape(N_DEV, r, D)
        sched.add(_rows("g_" + key, "d_" + key, "rsd", r, r), first=True)
        waiting.append((key, tag))

    def add_arrived():
        for key, tag in list(waiting):
            if "d_" + key in reg and not sched.pending("d_" + key):
                waiting.remove((key, tag))
                g8 = reg.pop("g_" + key)
                chip_sums[key] = reg["s_" + key] = _add_pairs(g8, reg.pop("d_" + key), "rs_add_" + tag)
                sched.add(_rows("s_" + key, "r_" + key, "rs", g8.shape[1], ROW_CHUNK[g8.shape[1]] // 2))

    sched.after_landing = add_arrived

    def reduced(key):
        sched.flush("d_" + key, FLUSH_EXTRA_US)
        sched.flush("r_" + key, FLUSH_EXTRA_US)
        return chip_sums[key], reg.pop("r_" + key)

    for layer in (3, 2, 1, 0):
        j = layer // 2
        xb, h, mix3, z, extra = saved[layer]
        kind = kinds[layer]
        if layer == 3:
            dz, dzb, dg, dbeta, part = sched.run(_ln_bwd, CARRY_US["ln_bwd"], dxn, z, full[kind + "_ln_g"][j], "loss_ln_bwd",
                                                 target=loss_target[0])
        else:
            dz, dzb, dg, dbeta = sched.run(_ln_bwd, CARRY_US["ln_bwd"], dxn, z, full[kind + "_ln_g"][j], "ln_bwd")
        gsum[kind + "_ln_g"][j] = dg.reshape(D)
        gsum[kind + "_ln_b"][j] = dbeta.reshape(D)
        chip_sum(sched.run(_mm_tn, CARRY_US["mm_dw_out"], mix3, dzb, 512, "mm_dw_out"), "w_out", f"out{layer}")
        dmix3 = sched.run(_mm_nt, CARRY_US["mm_dmix"], dzb, w_out[layer], 1024, 1024, "mm_dmix", out3=True)
        if kind == "even":
            o, l, bsb = extra
            ws = full["even_a_ws"][j]
            dh, dws, dbs, dlng, dlnb, dsink = sched.run(
                _even_bwd, CARRY_US["even_bwd"], h, dmix3, o, l, rope, full["even_a_ln_g"][j], full["even_a_ln_b"][j],
                ws, jnp.swapaxes(ws, 1, 2), bsb, full["even_b_sinks"][j], "even_bwd")
            gsum["even_a_ws"][j] = dws
            gsum["even_a_bs"][j] = jnp.transpose(dbs[:, :8])
            gsum["even_a_ln_g"][j] = dlng.reshape(W)
            gsum["even_a_ln_b"][j] = dlnb.reshape(W)
            gsum["even_b_sinks"][j] = dsink[0, :16]
            if layer == 0:
                rep_rows = [_rep_pack(jnp.stack(gsum[nm]).reshape(wts[nm].shape)) for nm in REP]
                sh_rows = [_sh_pack(jnp.stack(gsum[nm]).reshape(shape), axis) for nm, shape, axis in SH]
                packed = _pad_rows(jnp.concatenate(rep_rows + sh_rows, axis=1))
                gw, (small8, parts) = _mm_tn(dh, xb, 384, "mm_dw_in_even", comm=_Join([_ExchangeAll(packed), _GatherAll(part)]))
                loss = jnp.sum(parts[:, 0, 0]) * (0.5 / D)
            else:
                gw = sched.run(_mm_tn, CARRY_US["mm_dw_in_even"], dh, xb, 384, "mm_dw_in_even")
            chip_sum(gw, "w_in_even", f"in{layer}")
            if layer == 0:
                n_rep = sum(p.shape[1] for p in rep_rows)
                red = _sum8(small8, 1 << 20, "sum_small")
                (rep_all,) = sched.flush("d_in0", FLUSH_EXTRA_US, beside=_GatherAll(_pad_rows(red[:n_rep])))
                sched.overhang = 0.6
            dxn = sched.run(_mm_nn_res, CARRY_US["mm_dx_even"], dh, wt_in[layer], dz, 512, 1024, "mm_dx_even")
        else:
            hst, wa, wx, wp = extra
            dh4, dcw, dcb, dwa, dwx, dba, dbx, dlam = sched.run(
                _odd_c_bwd, CARRY_US["odd_c_bwd"], h, hst, dmix3, full["odd_conv_w"][j], full["odd_conv_b"][j], wa, wx,
                jnp.swapaxes(wa, 1, 2), jnp.swapaxes(wx, 1, 2), full["odd_b_a"][j], full["odd_b_x"][j], full["odd_lam"][j],
                "odd_c_bwd")
            dh4, dwp, dds = _odd_d_bwd(h, dmix3, dh4, wp, jnp.swapaxes(wp, 1, 2), full["odd_d_scale"][j], "odd_d_bwd")
            gsum["odd_conv_w"][j], gsum["odd_conv_b"][j] = dcw, dcb.reshape(W)
            gsum["odd_w_a"][j], gsum["odd_w_x"][j] = dwa, dwx
            gsum["odd_b_a"][j], gsum["odd_b_x"][j], gsum["odd_lam"][j] = dba.reshape(W), dbx.reshape(W), dlam.reshape(W)
            gsum["odd_w_pool"][j], gsum["odd_d_scale"][j] = dwp, dds.reshape(W)
            chip_sum(sched.run(_mm_tn, CARRY_US["mm_dw_in_odd"], dh4, xb, 1024, "mm_dw_in_odd"), "w_in_odd", f"in{layer}")
            dxn = sched.run(_mm_nn_res, CARRY_US["mm_dx_odd"], dh4, wt_in[layer], dz, 512, 1024, "mm_dx_odd")
    grad_x = dxn[None]

    out_g, out_d, out_m, out_v = {}, {}, {}, {}
    for nm, kind, what, layers in (("odd_w_out", "odd", "out", (1, 3)), ("even_w_out", "even", "out", (0, 2)),
                                   ("odd_w_in", "odd", "in", (1, 3)), ("even_w_in", "even", "in", (0, 2))):
        gl = [reduced(f"{what}{l}") for l in layers]
        if nm == "even_w_in":
            view = lambda a: jnp.transpose(a, (0, 2, 1))
            res, _ = _adamw(view(wts[nm]), gl, view(mom[nm]), view(var[nm]), 112, f"adamw_{nm}")
            res = [view(a) for a in res]
        elif what == "in":
            gs = [jnp.transpose(_rs_final(s4, r3, "rs_final_w_in_odd")) for s4, r3 in gl]
            res, _ = _adamw(wts[nm], gs, mom[nm], var[nm], 512, f"adamw_{nm}")
        else:
            res = sched.run(_adamw, CARRY_US["adamw_" + nm], wts[nm], gl, mom[nm], var[nm], 128, f"adamw_{nm}")
        out_d[nm], out_m[nm], out_v[nm], out_g[nm] = res

    g_small = {}
    off = 0
    for nm, p in zip(REP, rep_rows):
        r = p.shape[1]
        g_small[nm] = _rep_unpack(rep_all[:, off:off + r, :], wts[nm].shape)
        off += r
    off = n_rep
    for (nm, shape, axis), p in zip(SH, sh_rows):
        r = p.shape[1]
        g_small[nm] = red[off:off + r].reshape(wts[nm].shape)
        off += r

    def rows(a):
        f = a.reshape(-1)
        pad = (-f.shape[0]) % 128
        if pad:
            f = jnp.concatenate([f, jnp.zeros((pad,), a.dtype)])
        return f.reshape(-1, 128)

    small = REP + [nm for nm, _, _ in SH]
    each = lambda src: [rows(src[nm]) for nm in small]
    d2, m2, v2 = _adamw_many(each(wts), each(g_small), each(mom), each(var), "adamw_small")
    for i, nm in enumerate(small):
        n, shp = wts[nm].size, wts[nm].shape
        take = lambda a: a.reshape(-1)[:n].reshape(shp)
        out_g[nm], out_d[nm], out_m[nm], out_v[nm] = g_small[nm], take(d2[i]), take(m2[i]), take(v2[i])

    return (loss, grad_x, *[out_g[nm] for nm in NAMES], *[out_d[nm] for nm in NAMES],
            *[out_m[nm] for nm in NAMES], *[out_v[nm] for nm in NAMES])
```

```python
import functools

import jax
import jax.numpy as jnp
from jax import lax
from jax.experimental import pallas as pl
from jax.experimental.pallas import tpu as pltpu

F32 = jnp.float32
BF = jnp.bfloat16
MESH = pl.DeviceIdType.MESH
ANY = pl.BlockSpec(memory_space=pl.ANY)

N_DEV = 8
D = 2048
W = 1024
EVEN_IN = 5376
ODD_IN = 4096
CHUNK = 128
ALPHA = (2 * 4) ** 0.25
LN_EPS = 1e-5
ROPE_THETA = 500000.0
LRU_C = 8.0
LR, B1, B2, ADAM_EPS, WD, STEP = 0.001, 0.9, 0.999, 1e-08, 0.01, 10
NEG = -1e30
HEAD_COLS = 4


def _cp(vmem_mb=48, collective_id=None):
    return pltpu.CompilerParams(vmem_limit_bytes=vmem_mb * 1024 * 1024, collective_id=collective_id)


def _sig(x):
    return jax.nn.sigmoid(x)


def _silu_grad(x):
    s = _sig(x)
    return x * s, s * (1.0 + x * (1.0 - s))


def _dot(a, b):
    return jnp.dot(a, b, preferred_element_type=F32)


def _dot_nt(a, b):
    return lax.dot_general(a, b, (((1,), (1,)), ((), ())), preferred_element_type=F32)


def _dot_tn(a, b):
    return lax.dot_general(a, b, (((0,), (0,)), ((), ())), preferred_element_type=F32)


def _coords():
    return lax.axis_index("x"), lax.axis_index("y"), lax.axis_index("c")


def _chip(j):
    x, y, _ = _coords()
    return (1 - x if j & 2 else x), (1 - y if j & 1 else y)


X_NB, Y_NB, DIAG, SIB = 4, 2, 6, 1
EVERYONE = frozenset(range(1, N_DEV))
BARRIER_IDS = {}


class _Comm:
    def collective_id(self):
        return BARRIER_IDS.setdefault(frozenset(self.peers), len(BARRIER_IDS))

    def start(self, ins, outs, sems):
        barrier = pltpu.get_barrier_semaphore()
        for d in sorted(self.peers):
            pl.semaphore_signal(barrier, inc=1, device_id=_peer(d)[0], device_id_type=MESH)
        pl.semaphore_wait(barrier, len(self.peers))
        for cp in self.copies(ins, outs, sems):
            cp.start()

    def wait(self, ins, outs, sems):
        for cp in self.copies(ins, outs, sems):
            cp.wait()


class _Join(_Comm):
    def __init__(self, parts):
        self.parts = list(parts)
        self.peers = frozenset().union(*[p.peers for p in self.parts])
        self.inputs = [a for p in self.parts for a in p.inputs]
        self.out_shapes = [s for p in self.parts for s in p.out_shapes]
        self.sem_shapes = [s for p in self.parts for s in p.sem_shapes]
        self.aliases = {}
        i0 = o0 = 0
        for p in self.parts:
            for i, o in getattr(p, "aliases", {}).items():
                self.aliases[i0 + i] = o0 + o
            i0, o0 = i0 + len(p.inputs), o0 + len(p.out_shapes)

    def copies(self, ins, outs, sems):
        res = []
        i0 = o0 = s0 = 0
        for p in self.parts:
            ni, no, ns = len(p.inputs), len(p.out_shapes), len(p.sem_shapes)
            res += p.copies(ins[i0:i0 + ni], outs[o0:o0 + no], sems[s0:s0 + ns])
            i0, o0, s0 = i0 + ni, o0 + no, s0 + ns
        return res


ROWS_US = {"ag1": 0.104, "ag2": 0.052, "agd": 0.027, "rsd": 0.027, "rs": 0.205}
N_COPIES = {"ag1": 2, "ag2": 2, "agd": 4, "rsd": 4, "rs": 3}
TASK_PEERS = {"ag1": {X_NB, Y_NB}, "ag2": {X_NB, Y_NB}, "agd": {SIB}, "rsd": {SIB}, "rs": {X_NB, Y_NB, DIAG}}
ROW_CHUNK = {672: 224, 512: 128, 256: 128}
CARRY_US = {"mm_h_even": 58, "mm_h_odd": 47, "even_fwd": 42, "odd_c_fwd": 37, "mm_out_ln": 33, "ln_bwd": 23, "mm_dmix": 26,
            "mm_dw_out": 25, "even_bwd": 90, "odd_c_bwd": 58, "mm_dw_in_even": 58, "mm_dw_in_odd": 44, "mm_dx_even": 66,
            "mm_dx_odd": 55, "adamw_even_w_out": 11, "adamw_odd_w_out": 11}
FWD_OVERBOOK = 1.15
SMALL_CARRY_US = 10.0
FIRST_CARRY_US = 60.0
FLUSH_EXTRA_US = 60.0


def _cost_us(task, reg):
    kind, src, _, lo, hi = task
    return ROWS_US[kind] * (hi - lo) * reg[src].shape[-1] * reg[src].dtype.itemsize / 4096.0


class _Copies(_Comm):
    def __init__(self, tasks, reg):
        self.tasks = list(tasks)
        self.out_names, self.in_names = [], []
        for kind, src, dst, lo, hi in self.tasks:
            if dst not in self.out_names:
                self.out_names.append(dst)
        for kind, src, dst, lo, hi in self.tasks:
            if src not in self.out_names and src not in self.in_names:
                self.in_names.append(src)
        self.out_shapes, self.aliases = [], {}
        for o, dst in enumerate(self.out_names):
            if dst in reg:
                self.aliases[len(self.in_names)] = o
                self.in_names.append(dst)
                self.out_shapes.append(jax.ShapeDtypeStruct(reg[dst].shape, reg[dst].dtype))
            else:
                kind, src = next((t[0], t[1]) for t in self.tasks if t[2] == dst)
                shape = ({"rsd": 4, "rs": 3}[kind],) + reg[src].shape[1:]
                self.out_shapes.append(jax.ShapeDtypeStruct(shape, reg[src].dtype))
        self.inputs = [reg[nm] for nm in self.in_names]
        n = sum(N_COPIES[t[0]] for t in self.tasks)
        self.sem_shapes = [pltpu.SemaphoreType.DMA((n,)), pltpu.SemaphoreType.DMA((n,))]
        self.peers = frozenset().union(*[TASK_PEERS[t[0]] for t in self.tasks])

    def copies(self, ins, outs, sems):
        send, recv = sems
        x, y, c = _coords()
        me = 4 * x + 2 * y + c
        xn, yn = (1 - x, y, c), (x, 1 - y, c)
        at_xn, at_yn = 4 * (1 - x) + 2 * y + c, 4 * x + 2 * (1 - y) + c
        ref = dict(zip(self.in_names, ins))
        ref.update(zip(self.out_names, outs))
        res = []

        def copy(src, dst, to):
            i = len(res)
            res.append(pltpu.make_async_remote_copy(src_ref=src, dst_ref=dst, send_sem=send.at[i], recv_sem=recv.at[i],
                                                    device_id=to, device_id_type=MESH))

        for kind, src, dst, lo, hi in self.tasks:
            n = hi - lo
            if kind == "ag1":
                for to in (xn, yn):
                    copy(ref[src].at[pl.ds(lo, n)], ref[dst].at[me, pl.ds(lo, n)], to)
            elif kind == "ag2":
                h = n // 2
                first, second = ref[dst].at[at_xn, pl.ds(lo, h)], ref[dst].at[at_yn, pl.ds(lo + h, n - h)]
                copy(first, first, yn)
                copy(second, second, xn)
            elif kind == "agd":
                for j in range(4):
                    px, py = _chip(j)
                    rows = ref[dst].at[4 * px + 2 * py + c, pl.ds(lo, n)]
                    copy(rows, rows, (x, y, 1 - c))
            elif kind == "rsd":
                for j in range(4):
                    px, py = _chip(j)
                    copy(ref[src].at[4 * px + 2 * py + 1 - c, pl.ds(lo, n)], ref[dst].at[j, pl.ds(lo, n)], (x, y, 1 - c))
            else:
                for j in (1, 2, 3):
                    px, py = _chip(j)
                    copy(ref[src].at[j, pl.ds(lo, n)], ref[dst].at[j - 1, pl.ds(lo, n)], (px, py, c))
        return res


class _Sched:
    def __init__(self, reg):
        self.reg, self.queue, self.later = reg, [], []
        self.overhang = 0.5
        self.after_landing = None

    def add(self, tasks, first=False):
        self.queue = list(tasks) + self.queue if first else self.queue + list(tasks)

    def pending(self, dst):
        return any(t[2] == dst for t in self.queue + self.later)

    def take(self, budget_us, must=None, overhang=0.5):
        self.queue, self.later = self.later + self.queue, []
        picked, us = [], 0.0
        rest = []
        for t in self.queue:
            cost = _cost_us(t, self.reg)
            if (must is not None and t[2] == must) or us + (1.0 - overhang) * cost <= budget_us:
                picked.append(t)
                us += cost
                if t[0] in ("ag1", "ag2"):
                    self.later.append(({"ag1": "ag2", "ag2": "agd"}[t[0]], t[2], t[2], t[3], t[4]))
            else:
                rest.append(t)
        self.queue = rest
        return _Copies(picked, self.reg) if picked else None

    def landed(self, comm, got):
        if comm is not None:
            for nm, a in zip(comm.out_names, got):
                self.reg[nm] = a
        if self.after_landing is not None:
            self.after_landing()

    def run(self, builder, budget_us, *args, **kw):
        comm = self.take(budget_us, overhang=self.overhang)
        res, got = builder(*args, comm=comm, **kw)
        self.landed(comm, got)
        return res

    def flush(self, dst, budget_us=0.0, beside=None):
        res = []
        while self.pending(dst):
            comm = self.take(budget_us, must=dst)
            got = _comm_only(comm if beside is None else _Join([comm, beside]), "flush_" + dst)
            res, beside = got[len(comm.out_shapes):], None
            self.landed(comm, got[:len(comm.out_shapes)])
        return res


def _rows(name_src, name_dst, kind, n_rows, chunk):
    return [(kind, name_src, name_dst, lo, min(lo + chunk, n_rows)) for lo in range(0, n_rows, chunk)]


def _pcall(body, *, grid, in_specs, out_specs, out_shape, name, scratch=(), vmem=48, comm=None):
    in_specs, out_specs, out_shape, scratch = list(in_specs), list(out_specs), list(out_shape), list(scratch)
    if comm is None:
        call = pl.pallas_call(body, grid=grid, in_specs=in_specs, out_specs=out_specs, out_shape=out_shape,
                              scratch_shapes=scratch, name=name, compiler_params=_cp(vmem))
        return lambda *args: (call(*args), [])
    n_in, n_out, n_scr = len(in_specs), len(out_specs), len(scratch)
    c_in, c_out = len(comm.inputs), len(comm.out_shapes)
    aliases = {n_in + i: n_out + o for i, o in getattr(comm, "aliases", {}).items()}

    def wrapped(*refs):
        ins, cins = refs[:n_in], refs[n_in:n_in + c_in]
        o0 = n_in + c_in
        outs, couts = refs[o0:o0 + n_out], refs[o0 + n_out:o0 + n_out + c_out]
        s0 = o0 + n_out + c_out
        scr, sems = refs[s0:s0 + n_scr], refs[s0 + n_scr:]
        ids = [pl.program_id(a) for a in range(len(grid))]
        first = functools.reduce(jnp.logical_and, [i == 0 for i in ids])
        last = functools.reduce(jnp.logical_and, [i == g - 1 for i, g in zip(ids, grid)])

        @pl.when(first)
        def _():
            comm.start(cins, couts, sems)

        body(*ins, *outs, *scr)

        @pl.when(last)
        def _():
            comm.wait(cins, couts, sems)

    call = pl.pallas_call(wrapped, grid=grid, in_specs=in_specs + [ANY] * c_in, out_specs=out_specs + [ANY] * c_out,
                          out_shape=out_shape + list(comm.out_shapes), scratch_shapes=scratch + list(comm.sem_shapes),
                          input_output_aliases=aliases, name=name, compiler_params=_cp(vmem, comm.collective_id()))

    def run(*args):
        res = call(*args, *comm.inputs)
        return res[:n_out], res[n_out:]

    return run


def _comm_only(comm, name):
    c_in, c_out = len(comm.inputs), len(comm.out_shapes)

    def body(*refs):
        cins, couts, sems = refs[:c_in], refs[c_in:c_in + c_out], refs[c_in + c_out:]
        comm.start(cins, couts, sems)
        comm.wait(cins, couts, sems)

    return pl.pallas_call(body, in_specs=[ANY] * c_in, out_specs=[ANY] * c_out, out_shape=list(comm.out_shapes),
                          scratch_shapes=list(comm.sem_shapes), input_output_aliases=dict(getattr(comm, "aliases", {})),
                          name=name, compiler_params=pltpu.CompilerParams(collective_id=comm.collective_id()))(*comm.inputs)


def _chip_blocks():
    _, _, c = _coords()
    return jnp.stack([4 * px + 2 * py + c for px, py in map(_chip, range(4))]).astype(jnp.int32)


def _add_pairs(g8, b4, name):
    _, R, C = b4.shape

    def body(idx_ref, a_ref, b_ref, o_ref):
        o_ref[...] = (a_ref[...].astype(F32) + b_ref[...].astype(F32)).astype(BF)

    blk = pl.BlockSpec((None, R, C), lambda j, idx: (j, 0, 0))
    grid_spec = pltpu.PrefetchScalarGridSpec(
        num_scalar_prefetch=1, grid=(4,),
        in_specs=[pl.BlockSpec((None, R, C), lambda j, idx: (idx[j], 0, 0)), blk], out_specs=blk)
    return pl.pallas_call(body, grid_spec=grid_spec, out_shape=jax.ShapeDtypeStruct(b4.shape, BF), name=name,
                          compiler_params=_cp())(_chip_blocks(), g8, b4)


def _rs_final(s4, r3, name):
    _, R, C = s4.shape
    tr = R // 2

    def body(s_ref, r_ref, o_ref):
        o_ref[...] = ((s_ref[...].astype(F32) + r_ref[0].astype(F32)) + r_ref[1].astype(F32)) + r_ref[2].astype(F32)

    return pl.pallas_call(
        body, grid=(2,),
        in_specs=[pl.BlockSpec((None, tr, C), lambda i: (0, i, 0)), pl.BlockSpec((3, tr, C), lambda i: (0, i, 0))],
        out_specs=pl.BlockSpec((tr, C), lambda i: (i, 0)), out_shape=jax.ShapeDtypeStruct((R, C), F32),
        name=name, compiler_params=_cp())(s4, r3)


def _mm_nt(a, w, tm, tn, name, out3=False, comm=None):
    M, K = a.shape
    N = w.shape[0]
    tm = min(tm, M)

    def body(a_ref, w_ref, o_ref):
        o_ref[...] = _dot_nt(a_ref[...], w_ref[...])

    if out3:
        per = W // tn
        out_shape = jax.ShapeDtypeStruct((N // W, M, W), F32)
        out_spec = pl.BlockSpec((None, tm, tn), lambda i, j: (j // per, i, j % per))
    else:
        out_shape = jax.ShapeDtypeStruct((M, N), F32)
        out_spec = pl.BlockSpec((tm, tn), lambda i, j: (i, j))
    (res,), extra = _pcall(
        body, grid=(M // tm, N // tn),
        in_specs=[pl.BlockSpec((tm, K), lambda i, j: (i, 0)), pl.BlockSpec((tn, K), lambda i, j: (j, 0))],
        out_specs=[out_spec], out_shape=[out_shape], name=name, comm=comm)(a, w)
    return res, extra


def _mm_tn(a, b, tm, name, comm=None):
    K, N = b.shape
    if a.ndim == 3:
        M = a.shape[0] * W
        per = W // tm
        a_spec = pl.BlockSpec((None, K, tm), lambda i: (i // per, 0, i % per))
    else:
        M = a.shape[1]
        a_spec = pl.BlockSpec((K, tm), lambda i: (0, i))

    def body(a_ref, b_ref, o_ref):
        o_ref[...] = _dot_tn(a_ref[...], b_ref[...]).astype(BF)

    (out,), extra = _pcall(
        body, grid=(M // tm,),
        in_specs=[a_spec, pl.BlockSpec((K, N), lambda i: (0, 0))],
        out_specs=[pl.BlockSpec((tm, N), lambda i: (i, 0))],
        out_shape=[jax.ShapeDtypeStruct((M, N), BF)], name=name, vmem=56, comm=comm)(a, b)
    return out, extra


def _cast_rows(x, name, comm=None):
    S = x.shape[0]
    tm = min(512, S)

    def body(x_ref, o_ref):
        o_ref[...] = x_ref[...].astype(BF)

    row = pl.BlockSpec((tm, D), lambda i: (i, 0))
    return _pcall(body, grid=(S // tm,), in_specs=[row], out_specs=[row], out_shape=[jax.ShapeDtypeStruct((S, D), BF)],
                  name=name, comm=comm)(x)


def _mm_nn_res(a, w, res, tm, tn, name, comm=None):
    K, N = w.shape
    if a.ndim == 3:
        P, M = a.shape[0], a.shape[1]
        tm = min(tm, M)
        a_spec = pl.BlockSpec((P, tm, W), lambda j, i: (0, i, 0))
    else:
        P, M = 0, a.shape[0]
        tm = min(tm, M)
        a_spec = pl.BlockSpec((tm, K), lambda j, i: (i, 0))

    def body(a_ref, w_ref, r_ref, o_ref):
        if P:
            d = _dot(a_ref[0], w_ref[0:W, :])
            for p in range(1, P):
                d = d + _dot(a_ref[p], w_ref[p * W:(p + 1) * W, :])
        else:
            d = _dot(a_ref[...], w_ref[...])
        o_ref[...] = ALPHA * r_ref[...] + d

    (out,), extra = _pcall(
        body, grid=(N // tn, M // tm),
        in_specs=[a_spec, pl.BlockSpec((K, tn), lambda j, i: (0, j)), pl.BlockSpec((tm, tn), lambda j, i: (i, j))],
        out_specs=[pl.BlockSpec((tm, tn), lambda j, i: (i, j))],
        out_shape=[jax.ShapeDtypeStruct((M, N), F32)], name=name, comm=comm)(a, w, res)
    return out, extra


def _mm_out_ln(mix3, w_out, x, g, b, name, comm=None):
    S = x.shape[0]
    tm = min(512, S)

    def body(m_ref, w_ref, x_ref, g_ref, b_ref, z_ref, xn_ref, xb_ref):
        acc = _dot(m_ref[0], w_ref[0:W, :]) + _dot(m_ref[1], w_ref[W:2 * W, :])
        z = ALPHA * x_ref[...] + acc
        mu = jnp.mean(z, axis=1, keepdims=True)
        zc = z - mu
        var = jnp.mean(zc * zc, axis=1, keepdims=True)
        xn = zc * lax.rsqrt(var + LN_EPS) * g_ref[...] + b_ref[...]
        z_ref[...] = z
        xn_ref[...] = xn
        xb_ref[...] = xn.astype(BF)

    row = pl.BlockSpec((tm, D), lambda i: (i, 0))
    vec = pl.BlockSpec((1, D), lambda i: (0, 0))
    return _pcall(
        body, grid=(S // tm,),
        in_specs=[pl.BlockSpec((2, tm, W), lambda i: (0, i, 0)),
                  pl.BlockSpec((D, D), lambda i: (0, 0), pipeline_mode=pl.Buffered(1)), row, vec, vec],
        out_specs=[row, row, row],
        out_shape=[jax.ShapeDtypeStruct((S, D), F32), jax.ShapeDtypeStruct((S, D), F32), jax.ShapeDtypeStruct((S, D), BF)],
        name=name, comm=comm)(mix3, w_out, x, g.reshape(1, D), b.reshape(1, D))


def _ln_bwd(dxn, z, g, name, comm=None, target=None):
    S = z.shape[0]
    tm = min(256, S)
    head = target is not None

    def body(*refs):
        if head:
            d_ref, t_ref, z_ref, g_ref, dz_ref, dzb_ref, dg_ref, db_ref, p_ref = refs
        else:
            d_ref, z_ref, g_ref, dz_ref, dzb_ref, dg_ref, db_ref = refs
        i = pl.program_id(0)
        zz = z_ref[...]
        mu = jnp.mean(zz, axis=1, keepdims=True)
        zc = zz - mu
        var = jnp.mean(zc * zc, axis=1, keepdims=True)
        rstd = lax.rsqrt(var + LN_EPS)
        xhat = zc * rstd
        dy = d_ref[...]
        if head:
            e = dy - t_ref[...]
            dy = e * (1.0 / D)

            @pl.when(i == 0)
            def _():
                p_ref[...] = jnp.zeros_like(p_ref)

            p_ref[...] += jnp.sum(jnp.sum(e * e, axis=1, keepdims=True), axis=0, keepdims=True)
        dyg = dy * g_ref[...]
        m1 = jnp.mean(dyg, axis=1, keepdims=True)
        m2 = jnp.mean(dyg * xhat, axis=1, keepdims=True)
        dz = rstd * (dyg - m1 - xhat * m2)
        dz_ref[...] = dz
        dzb_ref[...] = dz.astype(BF)

        @pl.when(i == 0)
        def _():
            dg_ref[...] = jnp.zeros_like(dg_ref)
            db_ref[...] = jnp.zeros_like(db_ref)

        dg_ref[...] += jnp.sum(dy * xhat, axis=0, keepdims=True)
        db_ref[...] += jnp.sum(dy, axis=0, keepdims=True)

    row = pl.BlockSpec((tm, D), lambda i: (i, 0))
    vec = pl.BlockSpec((1, D), lambda i: (0, 0))
    out_specs = [row, row, vec, vec] + ([pl.BlockSpec((8, 128), lambda i: (0, 0))] if head else [])
    out_shape = [jax.ShapeDtypeStruct((S, D), F32), jax.ShapeDtypeStruct((S, D), BF), jax.ShapeDtypeStruct((1, D), F32),
                 jax.ShapeDtypeStruct((1, D), F32)] + ([jax.ShapeDtypeStruct((8, 128), F32)] if head else [])
    operands = (dxn, target, z, g.reshape(1, D)) if head else (dxn, z, g.reshape(1, D))
    return _pcall(body, grid=(S // tm,), in_specs=[row] * (len(operands) - 1) + [vec], out_specs=out_specs,
                  out_shape=out_shape, name=name, comm=comm)(*operands)


def _rope_fwd(t, r_ref):
    return (t * r_ref[:, 0:128] + pltpu.roll(t, 120, 1) * r_ref[:, 128:256]
            + pltpu.roll(t, 8, 1) * r_ref[:, 256:384])


def _rope_bwd(g, r_ref):
    return (g * r_ref[:, 0:128] + pltpu.roll(g * r_ref[:, 128:256], 8, 1)
            + pltpu.roll(g * r_ref[:, 256:384], 120, 1))


def _dup_heads(kb):
    lo = lax.broadcasted_iota(jnp.int32, kb.shape, 1) < 64
    sw = pltpu.roll(kb, 64, 1)
    return [jnp.where(lo, kb, sw).astype(BF), jnp.where(lo, sw, kb).astype(BF)]


def _even_fwd(h, rope, lng, lnb, ws, bsb, sinks, name, comm=None):
    S = h.shape[0]
    nb = S // CHUNK

    def body(h_ref, hp_ref, rc_ref, rp_ref, lng_ref, lnb_ref, ws_ref, bsb_ref, sink_ref, mix_ref, o_ref, l_ref):
        n = pl.program_id(0)
        lane = lax.broadcasted_iota(jnp.int32, (128, 128), 1)
        rowi = lax.broadcasted_iota(jnp.int32, (128, 128), 0)
        tri = rowi >= lane
        lane_lo = lane < 64
        v = h_ref[:, W:2 * W]
        mu = jnp.mean(v, axis=1, keepdims=True)
        vc = v - mu
        var = jnp.mean(vc * vc, axis=1, keepdims=True)
        vn = vc * lax.rsqrt(var + LN_EPS) * lng_ref[...] + lnb_ref[...]
        ms = [_dot(jnp.where(tri, ws_ref[g], 0.0).astype(BF), vn[:, g * 128:(g + 1) * 128].astype(BF)) for g in range(8)]
        for g in range(8):
            sl = slice(g * 128, (g + 1) * 128)
            ag = h_ref[:, 2 * W + g * 128:2 * W + (g + 1) * 128]
            mix_ref[0, :, sl] = (h_ref[:, sl] * (ms[g] + bsb_ref[g]) * (ag * _sig(ag))).astype(BF)
        kb = jnp.concatenate([_rope_fwd(hp_ref[:, 0:128], rp_ref), _rope_fwd(h_ref[:, 4096:4224], rc_ref)], axis=0)
        vb = jnp.concatenate([hp_ref[:, 128:256], h_ref[:, 4224:4352]], axis=0)
        k2 = _dup_heads(kb)
        v2 = _dup_heads(vb)
        qi = lax.broadcasted_iota(jnp.int32, (128, 256), 0)
        kj = lax.broadcasted_iota(jnp.int32, (128, 256), 1)
        diff = qi + 128 - kj
        valid = (diff >= 0) & (diff < 128) & ((n > 0) | (kj >= 128))
        lacc = jnp.zeros((128, 128), F32)
        for j0 in range(0, 8, HEAD_COLS):
            heads = [(j, half) for j in range(j0, j0 + HEAD_COLS) for half in range(2)]
            sc, pr, oh = {}, {}, {}
            for j in range(j0, j0 + HEAD_COLS):
                qc = _rope_fwd(h_ref[:, 3072 + j * 128:3072 + (j + 1) * 128], rc_ref)
                sc[j, 0] = _dot_nt(jnp.where(lane_lo, qc, 0.0).astype(BF), k2[j // 4])
                sc[j, 1] = _dot_nt(jnp.where(lane_lo, 0.0, qc).astype(BF), k2[j // 4])
            for j, half in heads:
                hq = 2 * j + half
                s = jnp.where(valid, sc[j, half] * 0.125, NEG)
                sk = sink_ref[hq]
                mx = jnp.maximum(jnp.max(s, axis=1, keepdims=True), sk)
                p = jnp.exp(s - mx)
                den = jnp.sum(p, axis=1, keepdims=True) + jnp.exp(sk - mx)
                pr[j, half] = (p / den).astype(BF)
                lacc = jnp.where(lane == hq, mx + jnp.log(den), lacc)
            for j, half in heads:
                oh[j, half] = _dot(pr[j, half], v2[j // 4])
            for j in range(j0, j0 + HEAD_COLS):
                cs = slice(j * 128, (j + 1) * 128)
                ocol = jnp.where(lane_lo, oh[j, 0], oh[j, 1])
                bg = h_ref[:, 4352 + j * 128:4352 + (j + 1) * 128]
                o_ref[:, cs] = ocol
                mix_ref[1, :, cs] = (ocol * (bg * _sig(bg))).astype(BF)
        l_ref[...] = lacc

    prev = lambda n: jnp.maximum(n - 1, 0)
    full = lambda shape: pl.BlockSpec(shape, lambda n: (0,) * len(shape))
    return _pcall(
        body, grid=(nb,),
        in_specs=[pl.BlockSpec((CHUNK, EVEN_IN), lambda n: (n, 0)),
                  pl.BlockSpec((CHUNK, 256), lambda n: (prev(n), 16)),
                  pl.BlockSpec((CHUNK, 384), lambda n: (n, 0)),
                  pl.BlockSpec((CHUNK, 384), lambda n: (prev(n), 0)),
                  full((1, W)), full((1, W)), full((8, 128, 128)), full((8, 128, 128)),
                  pl.BlockSpec(memory_space=pltpu.SMEM)],
        out_specs=[pl.BlockSpec((2, CHUNK, W), lambda n: (0, n, 0)),
                   pl.BlockSpec((CHUNK, W), lambda n: (n, 0)),
                   pl.BlockSpec((CHUNK, 128), lambda n: (n, 0))],
        out_shape=[jax.ShapeDtypeStruct((2, S, W), BF), jax.ShapeDtypeStruct((S, W), F32),
                   jax.ShapeDtypeStruct((S, 128), F32)],
        name=name, comm=comm)(h, h, rope, rope, lng.reshape(1, W), lnb.reshape(1, W), ws, bsb, sinks)


def _even_bwd(h, dmix3, o, l, rope, lng, lnb, ws, wst, bsb, sinks, name, comm=None):
    S = h.shape[0]
    nb = S // CHUNK

    def body(h_ref, hp_ref, hn_ref, dm_ref, dmn_ref, o_ref, on_ref, l_ref, ln_ref, rc_ref, rp_ref, rn_ref,
             lng_ref, lnb_ref, ws_ref, wst_ref, bsb_ref, sink_ref,
             dh_ref, dws_ref, dbs_ref, dlng_ref, dlnb_ref, dsink_ref, dvn_ref):
        n = pl.program_id(0)

        @pl.when(n == 0)
        def _():
            dws_ref[...] = jnp.zeros_like(dws_ref)
            dbs_ref[...] = jnp.zeros_like(dbs_ref)
            dlng_ref[...] = jnp.zeros_like(dlng_ref)
            dlnb_ref[...] = jnp.zeros_like(dlnb_ref)
            dsink_ref[...] = jnp.zeros_like(dsink_ref)

        lane = lax.broadcasted_iota(jnp.int32, (128, 128), 1)
        rowi = lax.broadcasted_iota(jnp.int32, (128, 128), 0)
        lane1 = lax.broadcasted_iota(jnp.int32, (1, 128), 1)
        tri = rowi >= lane
        tri_t = lane >= rowi
        lane_lo = lane < 64
        v = h_ref[:, W:2 * W]
        mu = jnp.mean(v, axis=1, keepdims=True)
        vc = v - mu
        var = jnp.mean(vc * vc, axis=1, keepdims=True)
        rstd = lax.rsqrt(var + LN_EPS)
        vhat = vc * rstd
        vn = vhat * lng_ref[...] + lnb_ref[...]
        dbs_acc = jnp.zeros((128, 128), F32)
        vng = [vn[:, g * 128:(g + 1) * 128].astype(BF) for g in range(8)]
        ms = [_dot(jnp.where(tri, ws_ref[g], 0.0).astype(BF), vng[g]) for g in range(8)]
        dmb = []
        for g in range(8):
            sl = slice(g * 128, (g + 1) * 128)
            m = ms[g] + bsb_ref[g]
            ag = h_ref[:, 2 * W + g * 128:2 * W + (g + 1) * 128]
            sg, dsg = _silu_grad(ag)
            u = h_ref[:, sl]
            da = dm_ref[0, :, sl]
            dmm = da * u * sg
            dh_ref[:, sl] = (da * m * sg).astype(BF)
            dh_ref[:, 2 * W + g * 128:2 * W + (g + 1) * 128] = (da * u * m * dsg).astype(BF)
            dmb.append(dmm.astype(BF))
            dbs_acc = jnp.where(lane == g, jnp.sum(dmm, axis=1, keepdims=True), dbs_acc)
        dvs = [_dot(jnp.where(tri_t, wst_ref[g], 0.0).astype(BF), dmb[g]) for g in range(8)]
        dwss = [_dot_nt(dmb[g], vng[g]) for g in range(8)]
        for g in range(8):
            dvn_ref[:, g * 128:(g + 1) * 128] = dvs[g]
            dws_ref[g] += jnp.where(tri, dwss[g], 0.0)
        dbs_ref[...] += dbs_acc
        dvn = dvn_ref[...]
        dlng_ref[...] += jnp.sum(dvn * vhat, axis=0, keepdims=True)
        dlnb_ref[...] += jnp.sum(dvn, axis=0, keepdims=True)
        dyg = dvn * lng_ref[...]
        m1 = jnp.mean(dyg, axis=1, keepdims=True)
        m2 = jnp.mean(dyg * vhat, axis=1, keepdims=True)
        dh_ref[:, W:2 * W] = (rstd * (dyg - m1 - vhat * m2)).astype(BF)
        kcur = _rope_fwd(h_ref[:, 4096:4224], rc_ref)
        kb = jnp.concatenate([_rope_fwd(hp_ref[:, 0:128], rp_ref), kcur], axis=0)
        vb = jnp.concatenate([hp_ref[:, 128:256], h_ref[:, 4224:4352]], axis=0)
        k2 = _dup_heads(kb)
        v2 = _dup_heads(vb)
        kc2 = _dup_heads(kcur)
        vc2 = _dup_heads(h_ref[:, 4224:4352])
        qi = lax.broadcasted_iota(jnp.int32, (128, 256), 0)
        kj = lax.broadcasted_iota(jnp.int32, (128, 256), 1)
        diff = qi + 128 - kj
        valid = (diff >= 0) & (diff < 128) & ((n > 0) | (kj >= 128))
        validn = (lane > rowi) & (n < nb - 1)
        lc = l_ref[...]
        lnx = ln_ref[...]
        dk = [jnp.zeros((128, 128), F32), jnp.zeros((128, 128), F32)]
        dv = [jnp.zeros((128, 128), F32), jnp.zeros((128, 128), F32)]
        dsk_acc = jnp.zeros((1, 128), F32)
        for j0 in range(0, 8, HEAD_COLS):
            heads = [(j, half) for j in range(j0, j0 + HEAD_COLS) for half in range(2)]
            t = {}
            for j in range(j0, j0 + HEAD_COLS):
                cs = slice(j * 128, (j + 1) * 128)
                qc = _rope_fwd(h_ref[:, 3072 + j * 128:3072 + (j + 1) * 128], rc_ref)
                qn = _rope_fwd(hn_ref[:, 3072 + j * 128:3072 + (j + 1) * 128], rn_ref)
                bg = h_ref[:, 4352 + j * 128:4352 + (j + 1) * 128]
                sgb, dsgb = _silu_grad(bg)
                db = dm_ref[1, :, cs]
                oc = o_ref[:, cs]
                do = db * sgb
                dh_ref[:, 4352 + j * 128:4352 + (j + 1) * 128] = (db * oc * dsgb).astype(BF)
                bgn = hn_ref[:, 4352 + j * 128:4352 + (j + 1) * 128]
                don = dmn_ref[1, :, cs] * (bgn * _sig(bgn))
                prod = do * oc
                prodn = don * on_ref[:, cs]
                for half in range(2):
                    hq = 2 * j + half
                    hm = lane_lo if half == 0 else jnp.logical_not(lane_lo)
                    t[j, half] = dict(
                        dsum=jnp.sum(jnp.where(hm, prod, 0.0), axis=1, keepdims=True),
                        dsumn=jnp.sum(jnp.where(hm, prodn, 0.0), axis=1, keepdims=True),
                        lh=jnp.sum(jnp.where(lane == hq, lc, 0.0), axis=1, keepdims=True),
                        lhn=jnp.sum(jnp.where(lane == hq, lnx, 0.0), axis=1, keepdims=True),
                        qm=jnp.where(hm, qc, 0.0).astype(BF), dom=jnp.where(hm, do, 0.0).astype(BF),
                        qnm=jnp.where(hm, qn, 0.0).astype(BF), donm=jnp.where(hm, don, 0.0).astype(BF))
            for j, half in heads:
                e, hk = t[j, half], j // 4
                e["s"], e["dp"] = _dot_nt(e["qm"], k2[hk]), _dot_nt(e["dom"], v2[hk])
                e["sn"], e["dpn"] = _dot_nt(e["qnm"], kc2[hk]), _dot_nt(e["donm"], vc2[hk])
            for j, half in heads:
                e, hq = t[j, half], 2 * j + half
                p = jnp.exp(jnp.where(valid, e["s"] * 0.125 - e["lh"], NEG))
                ds = p * (e["dp"] - e["dsum"])
                pn = jnp.exp(jnp.where(validn, e["sn"] * 0.125 - e["lhn"], NEG))
                dsn = pn * (e["dpn"] - e["dsumn"])
                psink = jnp.exp(sink_ref[hq] - e["lh"])
                dsk_acc = jnp.where(lane1 == hq, -jnp.sum(psink * e["dsum"], axis=0, keepdims=True), dsk_acc)
                e["ds"] = ds.astype(BF)
                e["pt"], e["dst"] = jnp.transpose(p[:, 128:256]).astype(BF), jnp.transpose(ds[:, 128:256]).astype(BF)
                e["pnt"], e["dsnt"] = jnp.transpose(pn).astype(BF), jnp.transpose(dsn).astype(BF)
            for j, half in heads:
                e, hk = t[j, half], j // 4
                e["dq"] = _dot(e["ds"], k2[hk])
                e["dv"] = _dot(e["pt"], e["dom"]) + _dot(e["pnt"], e["donm"])
                e["dk"] = _dot(e["dst"], e["qm"]) + _dot(e["dsnt"], e["qnm"])
            for j in range(j0, j0 + HEAD_COLS):
                hk = j // 4
                dqcol = jnp.where(lane_lo, t[j, 0]["dq"], t[j, 1]["dq"]) * 0.125
                dh_ref[:, 3072 + j * 128:3072 + (j + 1) * 128] = _rope_bwd(dqcol, rc_ref).astype(BF)
                dv[hk] = dv[hk] + t[j, 0]["dv"] + t[j, 1]["dv"]
                dk[hk] = dk[hk] + (t[j, 0]["dk"] + t[j, 1]["dk"]) * 0.125
        fold = lambda a: a + pltpu.roll(a, 64, 1)
        dh_ref[:, 4096:4224] = _rope_bwd(jnp.where(lane_lo, fold(dk[0]), fold(dk[1])), rc_ref).astype(BF)
        dh_ref[:, 4224:4352] = jnp.where(lane_lo, fold(dv[0]), fold(dv[1])).astype(BF)
        dsink_ref[...] += dsk_acc

    prev = lambda n: jnp.maximum(n - 1, 0)
    nxt = lambda n: jnp.minimum(n + 1, nb - 1)
    full = lambda shape: pl.BlockSpec(shape, lambda n: (0,) * len(shape))
    return _pcall(
        body, grid=(nb,),
        in_specs=[pl.BlockSpec((CHUNK, EVEN_IN), lambda n: (n, 0)),
                  pl.BlockSpec((CHUNK, 256), lambda n: (prev(n), 16)),
                  pl.BlockSpec((CHUNK, EVEN_IN), lambda n: (nxt(n), 0)),
                  pl.BlockSpec((2, CHUNK, W), lambda n: (0, n, 0)),
                  pl.BlockSpec((2, CHUNK, W), lambda n: (0, nxt(n), 0)),
                  pl.BlockSpec((CHUNK, W), lambda n: (n, 0)),
                  pl.BlockSpec((CHUNK, W), lambda n: (nxt(n), 0)),
                  pl.BlockSpec((CHUNK, 128), lambda n: (n, 0)),
                  pl.BlockSpec((CHUNK, 128), lambda n: (nxt(n), 0)),
                  pl.BlockSpec((CHUNK, 384), lambda n: (n, 0)),
                  pl.BlockSpec((CHUNK, 384), lambda n: (prev(n), 0)),
                  pl.BlockSpec((CHUNK, 384), lambda n: (nxt(n), 0)),
                  full((1, W)), full((1, W)), full((8, 128, 128)), full((8, 128, 128)), full((8, 128, 128)),
                  pl.BlockSpec(memory_space=pltpu.SMEM)],
        out_specs=[pl.BlockSpec((CHUNK, EVEN_IN), lambda n: (n, 0)),
                   full((8, 128, 128)), full((128, 128)), full((1, W)), full((1, W)), full((1, 128))],
        out_shape=[jax.ShapeDtypeStruct((S, EVEN_IN), BF), jax.ShapeDtypeStruct((8, 128, 128), F32),
                   jax.ShapeDtypeStruct((128, 128), F32), jax.ShapeDtypeStruct((1, W), F32),
                   jax.ShapeDtypeStruct((1, W), F32), jax.ShapeDtypeStruct((1, 128), F32)],
        scratch=[pltpu.VMEM((CHUNK, W), F32)], name=name, comm=comm,
    )(h, h, h, dmix3, dmix3, o, o, l, l, rope, rope, rope, lng.reshape(1, W), lnb.reshape(1, W), ws, wst, bsb, sinks)


def _expm1(x):
    ser = x * (1.0 + x * (0.5 + x * (1.0 / 6.0 + x * (1.0 / 24.0))))
    return jnp.where(jnp.abs(x) < 1e-2, ser, jnp.exp(x) - 1.0)


def _softplus_neg(lam):
    z = -lam
    e = jnp.exp(-jnp.abs(z))
    l1p = jnp.where(e < 1e-3, e * (1.0 - e * (0.5 - e * (1.0 / 3.0))), jnp.log(1.0 + e))
    return jnp.maximum(z, 0.0) + l1p


def _shift_down(x, k, row, fill=0.0):
    return jnp.where(row >= k, pltpu.roll(x, k, 0), fill)


def _shift_up(x, k, row, fill=0.0):
    S = x.shape[0]
    return jnp.where(row < S - k, pltpu.roll(x, S - k, 0), fill)


def _lru_gates(xc, row, cw_ref, cb_ref, wa_ref, wx_ref, ba_ref, bx_ref, lam_ref):
    xconv = (cw_ref[3:4, :] * xc + cw_ref[2:3, :] * _shift_down(xc, 1, row) + cw_ref[1:2, :] * _shift_down(xc, 2, row)
             + cw_ref[0:1, :] * _shift_down(xc, 3, row) + cb_ref[...])
    xb = xconv.astype(BF)
    r = _sig(_dot(xb, wa_ref[...]) + ba_ref[...])
    i = _sig(_dot(xb, wx_ref[...]) + bx_ref[...])
    sp = _softplus_neg(lam_ref[...])
    log_a = -LRU_C * r * sp
    a = jnp.exp(log_a)
    mult = jnp.sqrt(-_expm1(2.0 * log_a))
    return xconv, r, i, sp, a, mult


ROWS_PER_TILE = 8


def _steps(a, b, shift, inside, products=True):
    n, k = inside.n, 1
    while k < n:
        b = a * jnp.where(inside(k), shift(b, k), 0.0) + b
        if products or 2 * k < n:
            a = a * jnp.where(inside(k), shift(a, k), 1.0)
        k *= 2
    return a, b


class _Inside:
    def __init__(self, pos, n, reverse):
        self.pos, self.n, self.reverse = pos, n, reverse

    def __call__(self, k):
        return self.pos < self.n - k if self.reverse else self.pos >= k


def _scan_rows(a, b, row, a_ref, b_ref, c_ref, reverse=False):
    S = a.shape[0]
    G = S // ROWS_PER_TILE
    if reverse:
        shift = lambda x, k: pltpu.roll(x, x.shape[0] - k, 0)
    else:
        shift = lambda x, k: pltpu.roll(x, k, 0)
    a, b = _steps(a, b, shift, _Inside(row % ROWS_PER_TILE, ROWS_PER_TILE, reverse))
    a_ref[...] = a
    b_ref[...] = b
    last = 0 if reverse else ROWS_PER_TILE - 1
    grow = lax.broadcasted_iota(jnp.int32, (G, a.shape[1]), 0)
    _, tot = _steps(a_ref[pl.ds(last, G, stride=ROWS_PER_TILE), :], b_ref[pl.ds(last, G, stride=ROWS_PER_TILE), :],
                    shift, _Inside(grow, G, reverse), products=False)
    enters = jnp.where(_Inside(grow, G, reverse)(1), shift(tot, 1), 0.0)
    for r in range(ROWS_PER_TILE):
        c_ref[pl.ds(r, G, stride=ROWS_PER_TILE), :] = enters
    return b + a * c_ref[...]


def _odd_c_fwd(h, cw, cb, wa, wx, ba, bx, lam, name, comm=None):
    S = h.shape[0]

    def body(xc_ref, cg_ref, cw_ref, cb_ref, wa_ref, wx_ref, ba_ref, bx_ref, lam_ref, mix_ref, hst_ref, sa_ref, sb_ref, sc_ref):
        row = lax.broadcasted_iota(jnp.int32, (S, 128), 0)
        xconv, r, i, sp, a, mult = _lru_gates(xc_ref[...], row, cw_ref, cb_ref, wa_ref, wx_ref, ba_ref, bx_ref, lam_ref)
        bb = _scan_rows(a, mult * (i * xconv), row, sa_ref, sb_ref, sc_ref)
        hst_ref[...] = bb
        cg = cg_ref[...]
        mix_ref[...] = (bb * (cg * _sig(cg))).astype(BF)

    col = lambda off: pl.BlockSpec((S, 128), lambda j: (0, off + j))
    vec = pl.BlockSpec((1, 128), lambda j: (0, j))
    mat = pl.BlockSpec((None, 128, 128), lambda j: (j, 0, 0))
    return _pcall(
        body, grid=(8,),
        in_specs=[col(0), col(8), pl.BlockSpec((4, 128), lambda j: (0, j)), vec, mat, mat, vec, vec, vec],
        out_specs=[pl.BlockSpec((None, S, 128), lambda j: (0, 0, j)), pl.BlockSpec((S, 128), lambda j: (0, j))],
        out_shape=[jax.ShapeDtypeStruct((2, S, W), BF), jax.ShapeDtypeStruct((S, W), F32)],
        scratch=[pltpu.VMEM((S, 128), F32)] * 3, name=name, comm=comm,
    )(h, h, cw, cb.reshape(1, W), wa, wx, ba.reshape(1, W), bx.reshape(1, W), lam.reshape(1, W))


def _pool_sums(x, g, row, shift):
    s2 = x + shift(x, 1, row)
    s4 = s2 + shift(s2, 2, row)
    s8 = s4 + shift(s4, 4, row)
    s16 = s8 + shift(s8, 8, row)
    return jnp.where(g == 0, s2, jnp.where(g == 1, s4, jnp.where(g == 2, s8, s16)))


def _odd_d_fwd(h, mix3, wp, dscale, name):
    S = h.shape[0]

    def body(xd_ref, dg_ref, wp_ref, ds_ref, mix_in, mix_ref):
        g = pl.program_id(0)
        row = lax.broadcasted_iota(jnp.int32, (S, 256), 0)
        xd = xd_ref[...]
        cnt = jnp.minimum(row + 1, jnp.left_shift(2, g)).astype(F32)
        pooled = _pool_sums(xd, g, row, _shift_down) / cnt - xd
        mixed = _dot(pooled.astype(BF), wp_ref[...])
        dg = dg_ref[...]
        mix_ref[...] = (mixed * ds_ref[...] * (dg * _sig(dg))).astype(BF)

    col = lambda off: pl.BlockSpec((S, 256), lambda g: (0, off + g))
    return pl.pallas_call(
        body, grid=(4,),
        in_specs=[col(8), col(12), pl.BlockSpec((None, 256, 256), lambda g: (g, 0, 0)),
                  pl.BlockSpec((1, 256), lambda g: (0, g)), ANY],
        out_specs=pl.BlockSpec((None, S, 256), lambda g: (1, 0, g)),
        out_shape=jax.ShapeDtypeStruct((2, S, W), BF), input_output_aliases={4: 0},
        name=name, compiler_params=_cp(),
    )(h, h, wp, dscale.reshape(1, W), mix3)


def _odd_c_bwd(h, hst, dmix3, cw, cb, wa, wx, wat, wxt, ba, bx, lam, name, comm=None):
    S = h.shape[0]

    def body(xc_ref, cg_ref, hst_ref, dc_ref, cw_ref, cb_ref, wa_ref, wx_ref, wat_ref, wxt_ref, ba_ref, bx_ref, lam_ref,
             dh_ref, dcw_ref, dcb_ref, dwa_ref, dwx_ref, dba_ref, dbx_ref, dlam_ref, sa_ref, sb_ref, sc_ref):
        row = lax.broadcasted_iota(jnp.int32, (S, 128), 0)
        xc = xc_ref[...]
        xconv, r, i, sp, a, mult = _lru_gates(xc, row, cw_ref, cb_ref, wa_ref, wx_ref, ba_ref, bx_ref, lam_ref)
        hst = hst_ref[...]
        cg = cg_ref[...]
        sg, dsg = _silu_grad(cg)
        dc = dc_ref[...]
        dh_ref[1] = (dc * hst * dsg).astype(BF)
        lam_t = _scan_rows(_shift_up(a, 1, row), dc * sg, row, sa_ref, sb_ref, sc_ref, reverse=True)
        da = lam_t * _shift_down(hst, 1, row)
        ix = i * xconv
        dmult = lam_t * ix
        di = lam_t * mult * xconv
        dxconv = lam_t * mult * i
        dlog_a = da * a - dmult * (a * a / mult)
        dr = dlog_a * (-LRU_C * sp)
        dsp = jnp.sum(dlog_a * (-LRU_C * r), axis=0, keepdims=True)
        dlam_ref[...] = dsp * (-_sig(-lam_ref[...]))
        dpa = dr * r * (1.0 - r)
        dpx = di * i * (1.0 - i)
        dpab = dpa.astype(BF)
        dpxb = dpx.astype(BF)
        xb = xconv.astype(BF)
        dxconv = dxconv + _dot(dpab, wat_ref[...]) + _dot(dpxb, wxt_ref[...])
        dwa_ref[...] = _dot_tn(xb, dpab)
        dwx_ref[...] = _dot_tn(xb, dpxb)
        dba_ref[...] = jnp.sum(dpa, axis=0, keepdims=True)
        dbx_ref[...] = jnp.sum(dpx, axis=0, keepdims=True)
        dh_ref[0] = (cw_ref[3:4, :] * dxconv + cw_ref[2:3, :] * _shift_up(dxconv, 1, row)
                     + cw_ref[1:2, :] * _shift_up(dxconv, 2, row) + cw_ref[0:1, :] * _shift_up(dxconv, 3, row)).astype(BF)
        for j in range(4):
            src = xc if j == 3 else _shift_down(xc, 3 - j, row)
            dcw_ref[j:j + 1, :] = jnp.sum(dxconv * src, axis=0, keepdims=True)
        dcb_ref[...] = jnp.sum(dxconv, axis=0, keepdims=True)

    col = lambda off: pl.BlockSpec((S, 128), lambda j: (0, off + j))
    vec = pl.BlockSpec((1, 128), lambda j: (0, j))
    mat = pl.BlockSpec((None, 128, 128), lambda j: (j, 0, 0))
    vshape = jax.ShapeDtypeStruct((1, W), F32)
    mshape = jax.ShapeDtypeStruct((8, 128, 128), F32)
    return _pcall(
        body, grid=(8,),
        in_specs=[col(0), col(8), col(0), pl.BlockSpec((None, S, 128), lambda j: (0, 0, j)),
                  pl.BlockSpec((4, 128), lambda j: (0, j)), vec, mat, mat, mat, mat, vec, vec, vec],
        out_specs=[pl.BlockSpec((2, S, 128), lambda j: (0, 0, j)), pl.BlockSpec((4, 128), lambda j: (0, j)), vec,
                   mat, mat, vec, vec, vec],
        out_shape=[jax.ShapeDtypeStruct((4, S, W), BF), jax.ShapeDtypeStruct((4, W), F32), vshape, mshape, mshape,
                   vshape, vshape, vshape],
        scratch=[pltpu.VMEM((S, 128), F32)] * 3, name=name, vmem=56, comm=comm,
    )(h, h, hst, dmix3, cw, cb.reshape(1, W), wa, wx, wat, wxt, ba.reshape(1, W), bx.reshape(1, W), lam.reshape(1, W))


def _odd_d_bwd(h, dmix3, dh4, wp, wpt, dscale, name):
    S = h.shape[0]

    def body(xd_ref, dg_ref, dd_ref, wp_ref, wpt_ref, ds_ref, dh_in, dh_ref, dwp_ref, dds_ref):
        g = pl.program_id(0)
        row = lax.broadcasted_iota(jnp.int32, (S, 256), 0)
        xd = xd_ref[...]
        cnt = jnp.minimum(row + 1, jnp.left_shift(2, g)).astype(F32)
        pooled = _pool_sums(xd, g, row, _shift_down) / cnt - xd
        pb = pooled.astype(BF)
        mixed = _dot(pb, wp_ref[...])
        dg = dg_ref[...]
        sg, dsg = _silu_grad(dg)
        dd = dd_ref[...]
        dmixed = dd * ds_ref[...] * sg
        dds_ref[...] = jnp.sum(dd * mixed * sg, axis=0, keepdims=True)
        dh_ref[1] = (dd * mixed * ds_ref[...] * dsg).astype(BF)
        dmb = dmixed.astype(BF)
        dpooled = _dot(dmb, wpt_ref[...])
        dwp_ref[...] = _dot_tn(pb, dmb)
        dh_ref[0] = (_pool_sums(dpooled / cnt, g, row, _shift_up) - dpooled).astype(BF)

    col = lambda off: pl.BlockSpec((S, 256), lambda g: (0, off + g))
    mat = pl.BlockSpec((None, 256, 256), lambda g: (g, 0, 0))
    vec = pl.BlockSpec((1, 256), lambda g: (0, g))
    return pl.pallas_call(
        body, grid=(4,),
        in_specs=[col(8), col(12), pl.BlockSpec((None, S, 256), lambda g: (1, 0, g)), mat, mat, vec, ANY],
        out_specs=[pl.BlockSpec((2, S, 256), lambda g: (1, 0, g)), mat, vec],
        out_shape=[jax.ShapeDtypeStruct((4, S, W), BF), jax.ShapeDtypeStruct((4, 256, 256), F32),
                   jax.ShapeDtypeStruct((1, W), F32)],
        input_output_aliases={6: 0}, name=name, compiler_params=_cp(56),
    )(h, h, dmix3, wp, wpt, dscale.reshape(1, W), dh4)


def _peer(d):
    x, y, c = lax.axis_index("x"), lax.axis_index("y"), lax.axis_index("c")
    px = 1 - x if d & 4 else x
    py = 1 - y if d & 2 else y
    pc = 1 - c if d & 1 else c
    return (px, py, pc), 4 * px + 2 * py + pc


class _GatherAll(_Comm):
    def __init__(self, xs):
        self.peers = EVERYONE
        self.inputs = [xs]
        self.out_shapes = [jax.ShapeDtypeStruct((N_DEV,) + xs.shape, xs.dtype)]
        self.sem_shapes = [pltpu.SemaphoreType.DMA((N_DEV - 1,)), pltpu.SemaphoreType.DMA((N_DEV - 1,)),
                           pltpu.SemaphoreType.DMA]

    def copies(self, ins, outs, sems):
        (x_ref,), (out_ref,), (send, recv, loc) = ins, outs, sems
        _, me = _peer(0)
        res = [pltpu.make_async_copy(x_ref, out_ref.at[me], loc)]
        for d in range(1, N_DEV):
            peer, _ = _peer(d)
            res.append(pltpu.make_async_remote_copy(src_ref=x_ref, dst_ref=out_ref.at[me], send_sem=send.at[d - 1],
                                                    recv_sem=recv.at[d - 1], device_id=peer, device_id_type=MESH))
        return res


class _ExchangeAll(_Comm):
    def __init__(self, g8):
        self.peers = EVERYONE
        self.inputs = [g8]
        self.out_shapes = [jax.ShapeDtypeStruct(g8.shape, g8.dtype)]
        self.sem_shapes = [pltpu.SemaphoreType.DMA((N_DEV - 1,)), pltpu.SemaphoreType.DMA((N_DEV - 1,)),
                           pltpu.SemaphoreType.DMA]

    def copies(self, ins, outs, sems):
        (g_ref,), (out_ref,), (send, recv, loc) = ins, outs, sems
        _, me = _peer(0)
        res = [pltpu.make_async_copy(g_ref.at[me], out_ref.at[0], loc)]
        for d in range(1, N_DEV):
            peer, pidx = _peer(d)
            res.append(pltpu.make_async_remote_copy(src_ref=g_ref.at[pidx], dst_ref=out_ref.at[d], send_sem=send.at[d - 1],
                                                    recv_sem=recv.at[d - 1], device_id=peer, device_id_type=MESH))
        return res


def _sum8(r8, tr, name):
    _, R, C = r8.shape
    tr = min(tr, R)
    assert R % tr == 0

    def body(r_ref, o_ref):
        acc = r_ref[0]
        for d in range(1, N_DEV):
            acc = acc + r_ref[d]
        o_ref[...] = acc

    return pl.pallas_call(
        body, grid=(R // tr,), in_specs=[pl.BlockSpec((N_DEV, tr, C), lambda i: (0, i, 0))],
        out_specs=pl.BlockSpec((tr, C), lambda i: (i, 0)), out_shape=jax.ShapeDtypeStruct((R, C), F32),
        name=name, compiler_params=_cp(),
    )(r8)


def _adamw_math(w, g, m, v):
    m2 = B1 * m + (1.0 - B1) * g
    v2 = B2 * v + (1.0 - B2) * (g * g)
    m_hat = m2 / (1.0 - B1 ** STEP)
    v_hat = v2 / (1.0 - B2 ** STEP)
    return -LR * (m_hat / (jnp.sqrt(v_hat) + ADAM_EPS) + WD * w), m2, v2


def _adamw_many(ws, gs, ms, vs, name, comm=None):
    n = len(ws)

    def body(*refs):
        for i in range(n):
            d, m2, v2 = _adamw_math(refs[i][...], refs[n + i][...], refs[2 * n + i][...], refs[3 * n + i][...])
            refs[4 * n + i][...] = d
            refs[5 * n + i][...] = m2
            refs[6 * n + i][...] = v2

    vmem = pl.BlockSpec(memory_space=pltpu.VMEM)
    shapes = [jax.ShapeDtypeStruct(w.shape, F32) for w in ws]
    res, extra = _pcall(body, grid=(1,), in_specs=[vmem] * (4 * n), out_specs=[vmem] * (3 * n), out_shape=shapes * 3,
                        name=name, comm=comm)(*ws, *gs, *ms, *vs)
    return (res[:n], res[n:2 * n], res[2 * n:]), extra


def _adamw(w3, gs, m3, v3, tr, name, comm=None):
    _, R, C = w3.shape
    n = 2 if isinstance(gs[0], tuple) else 1

    def gradient(refs):
        if n == 1:
            return refs[0][...]
        s_ref, r_ref = refs
        return ((s_ref[...].astype(F32) + r_ref[0].astype(F32)) + r_ref[1].astype(F32)) + r_ref[2].astype(F32)

    def body(w_ref, *rest):
        g_refs, (m_ref, v_ref, d_ref, m2_ref, v2_ref, g_ref) = rest[:2 * n], rest[2 * n:]
        g = jnp.where(pl.program_id(0) == 0, gradient(g_refs[:n]), gradient(g_refs[n:]))
        d_ref[...], m2_ref[...], v2_ref[...] = _adamw_math(w_ref[...], g, m_ref[...], v_ref[...])
        g_ref[...] = g

    blk = pl.BlockSpec((None, tr, C), lambda j, i: (j, i, 0))

    def grad_specs(layer):
        at = lambda j, i: jnp.where(j == layer, i, 0)
        if n == 1:
            return [pl.BlockSpec((tr, C), lambda j, i: (at(j, i), 0))]
        return [pl.BlockSpec((None, tr, C), lambda j, i: (0, at(j, i), 0)), pl.BlockSpec((3, tr, C), lambda j, i: (0, at(j, i), 0))]

    flat = [a for g in gs for a in (g if n == 2 else (g,))]
    shp = jax.ShapeDtypeStruct((2, R, C), F32)
    return _pcall(body, grid=(2, R // tr), in_specs=[blk] + grad_specs(0) + grad_specs(1) + [blk, blk], out_specs=[blk] * 4,
                  out_shape=[shp] * 4, name=name, comm=comm)(w3, *flat, m3, v3)


def _rep_pack(a):
    n = a.size
    pad = (-n) % 1024
    f = a.reshape(-1)
    if pad:
        f = jnp.concatenate([f, jnp.zeros((pad,), a.dtype)])
    return f.reshape(N_DEV, -1, 128)


def _rep_unpack(p, shape):
    n = 1
    for s in shape:
        n *= s
    return p.reshape(-1)[:n].reshape(shape)


def _sh_pack(a, axis):
    shp = a.shape
    a = a.reshape(shp[:axis] + (N_DEV, shp[axis] // N_DEV) + shp[axis + 1:])
    return jnp.moveaxis(a, axis, 0).reshape(N_DEV, -1, 128)


def _sh_unpack(p, shape, axis):
    a = p.reshape((N_DEV,) + shape[:axis] + (shape[axis] // N_DEV,) + shape[axis + 1:])
    return jnp.moveaxis(a, 0, axis).reshape(shape)


def _pad_rows(a, mult=8):
    pad = (-a.shape[-2]) % mult
    if pad:
        a = jnp.concatenate([a, jnp.zeros(a.shape[:-2] + (pad, a.shape[-1]), a.dtype)], axis=-2)
    return a


REP = ["even_a_ln_g", "even_a_ln_b", "even_a_ws", "even_a_bs", "even_b_sinks", "even_ln_g", "even_ln_b",
       "odd_w_a", "odd_w_x"]
SH = [("odd_conv_w", (2, 4, W), 2), ("odd_conv_b", (2, W), 1), ("odd_b_a", (2, W), 1), ("odd_b_x", (2, W), 1),
      ("odd_lam", (2, W), 1), ("odd_w_pool", (2, 4, 256, 256), 2), ("odd_d_scale", (2, W), 1),
      ("odd_ln_g", (2, D), 1), ("odd_ln_b", (2, D), 1)]
BIG = ["even_w_in", "even_w_out", "odd_w_in", "odd_w_out"]
NAMES = ["even_w_in", "even_a_ln_g", "even_a_ln_b", "even_a_ws", "even_a_bs", "even_b_sinks", "even_w_out",
         "even_ln_g", "even_ln_b", "odd_w_in", "odd_conv_w", "odd_conv_b", "odd_w_a", "odd_b_a", "odd_w_x", "odd_b_x",
         "odd_lam", "odd_w_pool", "odd_d_scale", "odd_w_out", "odd_ln_g", "odd_ln_b"]


def _rope_table(positions):
    inv = ROPE_THETA ** (-jnp.arange(0, 16, 2, dtype=F32) / 16)
    f = jnp.arange(128) % 64
    ang = positions.astype(F32)[:, None] * inv[f % 8][None, :]
    cos, sin = jnp.cos(ang), jnp.sin(ang)
    return jnp.concatenate([jnp.where(f < 16, cos, 1.0), jnp.where(f < 8, -sin, 0.0),
                            jnp.where((f >= 8) & (f < 16), sin, 0.0)], axis=1)


def kernel(x, positions, even_w_in, even_a_ln_g, even_a_ln_b, even_a_ws, even_a_bs, even_b_sinks, even_w_out, even_ln_g, even_ln_b, odd_w_in, odd_conv_w, odd_conv_b, odd_w_a, odd_b_a, odd_w_x, odd_b_x, odd_lam, odd_w_pool, odd_d_scale, odd_w_out, odd_ln_g, odd_ln_b, loss_target, m_even_w_in, m_even_a_ln_g, m_even_a_ln_b, m_even_a_ws, m_even_a_bs, m_even_b_sinks, m_even_w_out, m_even_ln_g, m_even_ln_b, m_odd_w_in, m_odd_conv_w, m_odd_conv_b, m_odd_w_a, m_odd_b_a, m_odd_w_x, m_odd_b_x, m_odd_lam, m_odd_w_pool, m_odd_d_scale, m_odd_w_out, m_odd_ln_g, m_odd_ln_b, v_even_w_in, v_even_a_ln_g, v_even_a_ln_b, v_even_a_ws, v_even_a_bs, v_even_b_sinks, v_even_w_out, v_even_ln_g, v_even_ln_b, v_odd_w_in, v_odd_conv_w, v_odd_conv_b, v_odd_w_a, v_odd_b_a, v_odd_w_x, v_odd_b_x, v_odd_lam, v_odd_w_pool, v_odd_d_scale, v_odd_w_out, v_odd_ln_g, v_odd_ln_b):
    args = (even_w_in, even_a_ln_g, even_a_ln_b, even_a_ws, even_a_bs, even_b_sinks, even_w_out, even_ln_g, even_ln_b,
            odd_w_in, odd_conv_w, odd_conv_b, odd_w_a, odd_b_a, odd_w_x, odd_b_x, odd_lam, odd_w_pool, odd_d_scale,
            odd_w_out, odd_ln_g, odd_ln_b)
    margs = (m_even_w_in, m_even_a_ln_g, m_even_a_ln_b, m_even_a_ws, m_even_a_bs, m_even_b_sinks, m_even_w_out,
             m_even_ln_g, m_even_ln_b, m_odd_w_in, m_odd_conv_w, m_odd_conv_b, m_odd_w_a, m_odd_b_a, m_odd_w_x,
             m_odd_b_x, m_odd_lam, m_odd_w_pool, m_odd_d_scale, m_odd_w_out, m_odd_ln_g, m_odd_ln_b)
    vargs = (v_even_w_in, v_even_a_ln_g, v_even_a_ln_b, v_even_a_ws, v_even_a_bs, v_even_b_sinks, v_even_w_out,
             v_even_ln_g, v_even_ln_b, v_odd_w_in, v_odd_conv_w, v_odd_conv_b, v_odd_w_a, v_odd_b_a, v_odd_w_x,
             v_odd_b_x, v_odd_lam, v_odd_w_pool, v_odd_d_scale, v_odd_w_out, v_odd_ln_g, v_odd_ln_b)
    wts = dict(zip(NAMES, args))
    mom = dict(zip(NAMES, margs))
    var = dict(zip(NAMES, vargs))
    S = x.shape[1]
    x0 = x[0]
    rope = _rope_table(positions[0])

    kinds = ("even", "odd", "even", "odd")
    blk_in = [jnp.transpose(wts[kinds[l] + "_w_in"][l // 2]).astype(BF) for l in range(4)]
    blk_out = [wts[kinds[l] + "_w_out"][l // 2].astype(BF) for l in range(4)]
    sh_local = _pad_rows(jnp.concatenate([wts[nm].reshape(-1, 128) for nm, _, _ in SH], axis=0), 16)
    me = 4 * lax.axis_index("x") + 2 * lax.axis_index("y") + lax.axis_index("c")
    own_slot = lambda blk: lax.dynamic_update_slice(lax.empty((N_DEV,) + blk.shape, blk.dtype), blk[None], (me, 0, 0))
    reg = {"blk_small": sh_local, "w_small": own_slot(sh_local)}
    sched = _Sched(reg)
    for l in range(4):
        reg[f"blk_in{l}"], reg[f"blk_out{l}"] = blk_in[l], blk_out[l]
        reg[f"w_in{l}"], reg[f"w_out{l}"] = own_slot(blk_in[l]), own_slot(blk_out[l])
    sched.add(_rows("blk_in0", "w_in0", "ag1", blk_in[0].shape[0], ROW_CHUNK[blk_in[0].shape[0]]))
    sched.add(_rows("blk_small", "w_small", "ag1", sh_local.shape[0], sh_local.shape[0]))
    for l in range(4):
        sched.add(_rows(f"blk_out{l}", f"w_out{l}", "ag1", D // N_DEV, ROW_CHUNK[D // N_DEV]))
        if l < 3:
            r = blk_in[l + 1].shape[0]
            sched.add(_rows(f"blk_in{l + 1}", f"w_in{l + 1}", "ag1", r, ROW_CHUNK[r]))

    def gathered(dst, blk):
        sched.flush(dst, FLUSH_EXTRA_US)
        return reg.pop(dst)

    (xb0,) = sched.run(_cast_rows, FIRST_CARRY_US, x0, "cast_x")
    wt_in0 = gathered("w_in0", blk_in[0]).reshape(-1, D)
    full = {nm: wts[nm] for nm in REP}

    def gather_small():
        sh_all = gathered("w_small", sh_local)
        off = 0
        for nm, shape, axis in SH:
            r = wts[nm].size // 128
            full[nm] = _sh_unpack(sh_all[:, off:off + r, :], shape, axis)
            off += r

    saved = []
    wt_in, w_out = [wt_in0, None, None, None], [None] * 4
    xf, xb = x0, xb0
    fwd = lambda name: FWD_OVERBOOK * CARRY_US[name]
    for layer in range(4):
        j = layer // 2
        kind = kinds[layer]
        if wt_in[layer] is None:
            wt_in[layer] = gathered(f"w_in{layer}", blk_in[layer]).reshape(-1, D)
        h = sched.run(_mm_nt, fwd("mm_h_" + kind), xb, wt_in[layer], 1024, 768 if kind == "even" else 512, "mm_h_" + kind)
        if kind == "even":
            bsb = jnp.broadcast_to(full["even_a_bs"][j][:, :, None], (8, 128, 128))
            mix3, o, l = sched.run(_even_fwd, fwd("even_fwd"), h, rope, full["even_a_ln_g"][j], full["even_a_ln_b"][j],
                                   full["even_a_ws"][j], bsb, full["even_b_sinks"][j], "even_fwd")
            extra = (o, l, bsb)
        else:
            if "odd_lam" not in full:
                gather_small()
            wa, wx = full["odd_w_a"][j].astype(BF), full["odd_w_x"][j].astype(BF)
            wp = full["odd_w_pool"][j].astype(BF)
            mix3, hst = sched.run(_odd_c_fwd, fwd("odd_c_fwd"), h, full["odd_conv_w"][j], full["odd_conv_b"][j], wa, wx,
                                  full["odd_b_a"][j], full["odd_b_x"][j], full["odd_lam"][j], "odd_c_fwd")
            mix3 = _odd_d_fwd(h, mix3, wp, full["odd_d_scale"][j], "odd_d_fwd")
            extra = (hst, wa, wx, wp)
        w_out[layer] = gathered(f"w_out{layer}", blk_out[layer]).reshape(D, D)
        z, xn, xnb = sched.run(_mm_out_ln, fwd("mm_out_ln"), mix3, w_out[layer], xf, full[kind + "_ln_g"][j],
                               full[kind + "_ln_b"][j], "mm_out_ln")
        saved.append((xb, h, mix3, z, extra))
        xf, xb = xn, xnb

    dxn = xf

    gsum = {nm: [None, None] for nm in NAMES}

    chip_sums = {}
    sched.overhang = 0.15

    waiting = []

    def chip_sum(g, tag, key):
        r = g.shape[0] // N_DEV
        reg["g_" + key] = g.reshape(N_DEV, r, D)
        sched.add(_rows("g_" + key, "d_" + key, "rsd", r, r), first=True)
        waiting.append((key, tag))

    def add_arrived():
        for key, tag in list(waiting):
            if "d_" + key in reg and not sched.pending("d_" + key):
                waiting.remove((key, tag))
                g8 = reg.pop("g_" + key)
                chip_sums[key] = reg["s_" + key] = _add_pairs(g8, reg.pop("d_" + key), "rs_add_" + tag)
                sched.add(_rows("s_" + key, "r_" + key, "rs", g8.shape[1], ROW_CHUNK[g8.shape[1]] // 2))

    sched.after_landing = add_arrived

    def reduced(key):
        sched.flush("d_" + key, FLUSH_EXTRA_US)
        sched.flush("r_" + key, FLUSH_EXTRA_US)
        return chip_sums[key], reg.pop("r_" + key)

    for layer in (3, 2, 1, 0):
        j = layer // 2
        xb, h, mix3, z, extra = saved[layer]
        kind = kinds[layer]
        if layer == 3:
            dz, dzb, dg, dbeta, part = sched.run(_ln_bwd, CARRY_US["ln_bwd"], dxn, z, full[kind + "_ln_g"][j], "loss_ln_bwd",
                                                 target=loss_target[0])
        else:
            dz, dzb, dg, dbeta = sched.run(_ln_bwd, CARRY_US["ln_bwd"], dxn, z, full[kind + "_ln_g"][j], "ln_bwd")
        gsum[kind + "_ln_g"][j] = dg.reshape(D)
        gsum[kind + "_ln_b"][j] = dbeta.reshape(D)
        chip_sum(sched.run(_mm_tn, CARRY_US["mm_dw_out"], mix3, dzb, 512, "mm_dw_out"), "w_out", f"out{layer}")
        dmix3 = sched.run(_mm_nt, CARRY_US["mm_dmix"], dzb, w_out[layer], 1024, 1024, "mm_dmix", out3=True)
        if kind == "even":
            o, l, bsb = extra
            ws = full["even_a_ws"][j]
            dh, dws, dbs, dlng, dlnb, dsink = sched.run(
                _even_bwd, CARRY_US["even_bwd"], h, dmix3, o, l, rope, full["even_a_ln_g"][j], full["even_a_ln_b"][j],
                ws, jnp.swapaxes(ws, 1, 2), bsb, full["even_b_sinks"][j], "even_bwd")
            gsum["even_a_ws"][j] = dws
            gsum["even_a_bs"][j] = jnp.transpose(dbs[:, :8])
            gsum["even_a_ln_g"][j] = dlng.reshape(W)
            gsum["even_a_ln_b"][j] = dlnb.reshape(W)
            gsum["even_b_sinks"][j] = dsink[0, :16]
            if layer == 0:
                rep_rows = [_rep_pack(jnp.stack(gsum[nm]).reshape(wts[nm].shape)) for nm in REP]
                sh_rows = [_sh_pack(jnp.stack(gsum[nm]).reshape(shape), axis) for nm, shape, axis in SH]
                packed = _pad_rows(jnp.concatenate(rep_rows + sh_rows, axis=1))
                gw, (small8, parts) = _mm_tn(dh, xb, 384, "mm_dw_in_even", comm=_Join([_ExchangeAll(packed), _GatherAll(part)]))
                loss = jnp.sum(parts[:, 0, 0]) * (0.5 / D)
            else:
                gw = sched.run(_mm_tn, CARRY_US["mm_dw_in_even"], dh, xb, 384, "mm_dw_in_even")
            chip_sum(gw, "w_in_even", f"in{layer}")
            if layer == 0:
                n_rep = sum(p.shape[1] for p in rep_rows)
                red = _sum8(small8, 1 << 20, "sum_small")
                (rep_all,) = sched.flush("d_in0", FLUSH_EXTRA_US, beside=_GatherAll(_pad_rows(red[:n_rep])))
                sched.overhang = 0.6
            dxn = sched.run(_mm_nn_res, CARRY_US["mm_dx_even"], dh, wt_in[layer], dz, 512, 1024, "mm_dx_even")
        else:
            hst, wa, wx, wp = extra
            dh4, dcw, dcb, dwa, dwx, dba, dbx, dlam = sched.run(
                _odd_c_bwd, CARRY_US["odd_c_bwd"], h, hst, dmix3, full["odd_conv_w"][j], full["odd_conv_b"][j], wa, wx,
                jnp.swapaxes(wa, 1, 2), jnp.swapaxes(wx, 1, 2), full["odd_b_a"][j], full["odd_b_x"][j], full["odd_lam"][j],
                "odd_c_bwd")
            dh4, dwp, dds = _odd_d_bwd(h, dmix3, dh4, wp, jnp.swapaxes(wp, 1, 2), full["odd_d_scale"][j], "odd_d_bwd")
            gsum["odd_conv_w"][j], gsum["odd_conv_b"][j] = dcw, dcb.reshape(W)
            gsum["odd_w_a"][j], gsum["odd_w_x"][j] = dwa, dwx
            gsum["odd_b_a"][j], gsum["odd_b_x"][j], gsum["odd_lam"][j] = dba.reshape(W), dbx.reshape(W), dlam.reshape(W)
            gsum["odd_w_pool"][j], gsum["odd_d_scale"][j] = dwp, dds.reshape(W)
            chip_sum(sched.run(_mm_tn, CARRY_US["mm_dw_in_odd"], dh4, xb, 1024, "mm_dw_in_odd"), "w_in_odd", f"in{layer}")
            dxn = sched.run(_mm_nn_res, CARRY_US["mm_dx_odd"], dh4, wt_in[layer], dz, 512, 1024, "mm_dx_odd")
    grad_x = dxn[None]

    out_g, out_d, out_m, out_v = {}, {}, {}, {}

    def big_updates():
        for nm, kind, what, layers in (("odd_w_out", "odd", "out", (1, 3)), ("even_w_out", "even", "out", (0, 2)),
                                       ("odd_w_in", "odd", "in", (1, 3)), ("even_w_in", "even", "in", (0, 2))):
            gl = [reduced(f"{what}{l}") for l in layers]
            if nm == "even_w_in":
                view = lambda a: jnp.transpose(a, (0, 2, 1))
                res, _ = _adamw(view(wts[nm]), gl, view(mom[nm]), view(var[nm]), 112, f"adamw_{nm}")
                res = [view(a) for a in res]
            elif what == "in":
                gs = [jnp.transpose(_rs_final(s4, r3, "rs_final_w_in_odd")) for s4, r3 in gl]
                res, _ = _adamw(wts[nm], gs, mom[nm], var[nm], 512, f"adamw_{nm}")
            else:
                res = sched.run(_adamw, CARRY_US["adamw_" + nm], wts[nm], gl, mom[nm], var[nm], 128, f"adamw_{nm}")
            out_d[nm], out_m[nm], out_v[nm], out_g[nm] = res

    g_small = {}
    off = 0
    for nm, p in zip(REP, rep_rows):
        r = p.shape[1]
        g_small[nm] = _rep_unpack(rep_all[:, off:off + r, :], wts[nm].shape)
        off += r
    off = n_rep
    for (nm, shape, axis), p in zip(SH, sh_rows):
        r = p.shape[1]
        g_small[nm] = red[off:off + r].reshape(wts[nm].shape)
        off += r

    def rows(a):
        f = a.reshape(-1)
        pad = (-f.shape[0]) % 128
        if pad:
            f = jnp.concatenate([f, jnp.zeros((pad,), a.dtype)])
        return f.reshape(-1, 128)

    small = REP + [nm for nm, _, _ in SH]
    each = lambda src: [rows(src[nm]) for nm in small]
    d2, m2, v2 = sched.run(_adamw_many, SMALL_CARRY_US, each(wts), each(g_small), each(mom), each(var), "adamw_small")
    for i, nm in enumerate(small):
        n, shp = wts[nm].size, wts[nm].shape
        take = lambda a: a.reshape(-1)[:n].reshape(shp)
        out_g[nm], out_d[nm], out_m[nm], out_v[nm] = g_small[nm], take(d2[i]), take(m2[i]), take(v2[i])
    big_updates()

    return (loss, grad_x, *[out_g[nm] for nm in NAMES], *[out_d[nm] for nm in NAMES],
            *[out_m[nm] for nm in NAMES], *[out_v[nm] for nm in NAMES])
```
